```python
import math
import jax, jax.numpy as jnp
from jax import lax
import numpy as np

D_MODEL = 2048
BATCH = 8
SEQ = 8192
DEPTH = 1

N_MEM = 256
D_MIX = D_MODEL
RET_WIDTH = D_MIX // 2
LRU_WIDTH = D_MIX - RET_WIDTH
RET_HEADS = 4
RET_HEAD_DIM = RET_WIDTH // RET_HEADS
RET_CHUNK = 128
ROPE_BASE = 10000.0
LRU_BLOCKS = 8
LRU_BLOCK_DIM = LRU_WIDTH // LRU_BLOCKS
LRU_CONV = 4
LRU_C = 8.0
XA_HEADS = 4
XA_HEAD_DIM = D_MODEL // XA_HEADS
D_FF = 5632
FFN_CONV = 3
EPS = 1e-6
IN_COLS = 4 * RET_WIDTH + 2 * LRU_WIDTH

kernel_name = 'hymba_retention_rglru_block'


def rms_norm(x, g):
    xf = x.astype(jnp.float32)
    y = xf * lax.rsqrt(jnp.mean(xf * xf, axis=-1, keepdims=True) + EPS)
    return (y * g.astype(jnp.float32)).astype(x.dtype)


def causal_dwconv(x, w, b):
    K, C = w.shape
    y = lax.conv_general_dilated(x, w[:, None, :].astype(x.dtype), window_strides=(1,),
                                 padding=((K - 1, 0),), dimension_numbers=('NWC', 'WIO', 'NWC'),
                                 feature_group_count=C)
    return y + b.astype(x.dtype)


def rotary(t, pos):
    half = t.shape[-1] // 2
    inv = ROPE_BASE ** (-jnp.arange(half, dtype=jnp.float32) / half)
    ang = pos.astype(jnp.float32)[..., None] * inv
    cos = jnp.cos(ang)[:, :, None, :]
    sin = jnp.sin(ang)[:, :, None, :]
    t1, t2 = t[..., :half], t[..., half:]
    return jnp.concatenate([t1 * cos - t2 * sin, t1 * sin + t2 * cos], axis=-1)


def retention(q, k, v):
    B, S, H, Dk = q.shape
    Dv = v.shape[-1]
    C = RET_CHUNK
    N = S // C
    log_g = jnp.log(1.0 - 2.0 ** (-5.0 - jnp.arange(H, dtype=jnp.float32)))
    idx = jnp.arange(C, dtype=jnp.float32)
    diff = idx[:, None] - idx[None, :]
    intra = jnp.where(diff >= 0, jnp.exp(log_g[:, None, None] * jnp.maximum(diff, 0.0)), 0.0)
    q_dec = jnp.exp(log_g[:, None] * (idx + 1.0))[None, :, :, None]
    k_dec = jnp.exp(log_g[:, None] * (C - 1.0 - idx))[None, :, :, None]
    chunk_dec = jnp.exp(log_g * C)[None, :, None, None]

    def to_chunks(t):
        return t.reshape(B, N, C, H, t.shape[-1]).transpose(1, 0, 3, 2, 4)

    def step(state, qkv):
        qc, kc, vc = qkv
        s = jnp.einsum('bhid,bhjd->bhij', qc, kc) * intra
        inner = jnp.einsum('bhij,bhje->bhie', s, vc)
        cross = jnp.einsum('bhid,bhde->bhie', qc * q_dec, state)
        state = state * chunk_dec + jnp.einsum('bhjd,bhje->bhde', kc * k_dec, vc)
        return state, inner + cross

    state0 = jnp.zeros((B, H, Dk, Dv), jnp.float32)
    _, out = lax.scan(step, state0, (to_chunks(q), to_chunks(k), to_chunks(v)))
    return out.transpose(1, 0, 3, 2, 4).reshape(B, S, H, Dv)


def rg_lru(u, w_a, b_a, w_x, b_x, lam):
    B, S, W = u.shape
    uf = u.astype(jnp.float32)
    ub = uf.reshape(B, S, LRU_BLOCKS, LRU_BLOCK_DIM)
    r = jax.nn.sigmoid(jnp.einsum('bsnc,ncd->bsnd', ub, w_a.astype(jnp.float32)) + b_a.astype(jnp.float32)).reshape(B, S, W)
    i = jax.nn.sigmoid(jnp.einsum('bsnc,ncd->bsnd', ub, w_x.astype(jnp.float32)) + b_x.astype(jnp.float32)).reshape(B, S, W)
    log_a = LRU_C * r * jax.nn.log_sigmoid(lam.astype(jnp.float32))
    a = jnp.exp(log_a)
    b = jnp.sqrt(-jnp.expm1(2.0 * log_a)) * (i * uf)

    def combine(lhs, rhs):
        a1, b1 = lhs
        a2, b2 = rhs
        return a1 * a2, a2 * b1 + b2

    _, h = lax.associative_scan(combine, (a, b), axis=1)
    return h.astype(u.dtype)


def hybrid_mixer(xn, pos, w_in, ret_g, conv_w, conv_b, w_a, b_a, w_x, b_x, lam, w_out):
    B, S, _ = xn.shape
    h = xn @ w_in
    R, L = RET_WIDTH, LRU_WIDTH
    q, k, v, g, u, y = jnp.split(h, [R, 2 * R, 3 * R, 4 * R, 4 * R + L], axis=-1)
    shp = (B, S, RET_HEADS, RET_HEAD_DIM)
    rq = rotary(q.reshape(shp).astype(jnp.float32), pos)
    rk = rotary(k.reshape(shp).astype(jnp.float32), pos) * (RET_HEAD_DIM ** -0.5)
    rv = v.reshape(shp).astype(jnp.float32)
    ret = retention(rq, rk, rv)
    ret = ret * lax.rsqrt(jnp.mean(ret * ret, axis=-1, keepdims=True) + EPS)
    ret = (ret.reshape(B, S, R) * ret_g.astype(jnp.float32) * jax.nn.silu(g.astype(jnp.float32))).astype(xn.dtype)
    uc = causal_dwconv(u, conv_w, conv_b)
    lru = rg_lru(uc, w_a, b_a, w_x, b_x, lam) * jax.nn.gelu(y)
    return jnp.concatenate([ret, lru], axis=-1) @ w_out


def cross_attend(xn, memn, wq, wk, wv, wo):
    B, S, _ = xn.shape
    M = memn.shape[1]
    q = (xn @ wq).reshape(B, S, XA_HEADS, XA_HEAD_DIM)
    k = (memn @ wk).reshape(B, M, XA_HEADS, XA_HEAD_DIM)
    v = (memn @ wv).reshape(B, M, XA_HEADS, XA_HEAD_DIM)
    scores = jnp.einsum('bshd,bmhd->bhsm', q, k).astype(jnp.float32) * (XA_HEAD_DIM ** -0.5)
    p = jax.nn.softmax(scores, axis=-1).astype(v.dtype)
    o = jnp.einsum('bhsm,bmhd->bshd', p, v).reshape(B, S, XA_HEADS * XA_HEAD_DIM)
    return o @ wo


def conv_ffn(xn, w_up, conv_w, conv_b, w_down):
    h = causal_dwconv(xn @ w_up, conv_w, conv_b)
    a, b = jnp.split(h, 2, axis=-1)
    return (jax.nn.silu(a) * b) @ w_down


def _fwd_setup_inputs(seed: int = 0) -> dict:
    key = jax.random.key(seed)
    ks = jax.random.split(key, 28)
    f32 = jnp.float32

    def nrm(k, shape, fan_in):
        return jax.random.normal(k, shape, f32) * (fan_in ** -0.5)

    def gain(k, shape):
        return 1.0 + 0.02 * jax.random.normal(k, shape, f32)

    def bias(k, shape):
        return 0.01 * jax.random.normal(k, shape, f32)

    x = jax.random.normal(ks[0], (BATCH, SEQ, D_MODEL), f32)
    mem = jax.random.normal(ks[1], (BATCH, N_MEM, D_MODEL), f32)
    offset = jax.random.randint(ks[2], (BATCH, 1), 0, 4096, dtype=jnp.int32)
    positions = (offset + jnp.arange(SEQ, dtype=jnp.int32)[None, :]).astype(jnp.int32)
    u = jax.random.uniform(ks[12], (DEPTH, LRU_WIDTH), f32, minval=0.9, maxval=0.999)
    a0 = u ** (1.0 / LRU_C)
    rg_lambda = jnp.log(a0) - jnp.log1p(-a0)
    return {
        'x': x,
        'mem': mem,
        'positions': positions,
        'norm1_g': gain(ks[3], (DEPTH, D_MODEL)),
        'w_in': nrm(ks[4], (DEPTH, D_MODEL, IN_COLS), D_MODEL),
        'ret_g': gain(ks[5], (DEPTH, RET_WIDTH)),
        'rg_conv_w': nrm(ks[6], (DEPTH, LRU_CONV, LRU_WIDTH), LRU_CONV),
        'rg_conv_b': bias(ks[7], (DEPTH, LRU_WIDTH)),
        'rg_wa': nrm(ks[8], (DEPTH, LRU_BLOCKS, LRU_BLOCK_DIM, LRU_BLOCK_DIM), LRU_BLOCK_DIM),
        'rg_ba': bias(ks[9], (DEPTH, LRU_BLOCKS, LRU_BLOCK_DIM)),
        'rg_wx': nrm(ks[10], (DEPTH, LRU_BLOCKS, LRU_BLOCK_DIM, LRU_BLOCK_DIM), LRU_BLOCK_DIM),
        'rg_bx': bias(ks[11], (DEPTH, LRU_BLOCKS, LRU_BLOCK_DIM)),
        'rg_lambda': rg_lambda,
        'w_out': nrm(ks[13], (DEPTH, D_MIX, D_MODEL), D_MIX),
        'norm2_g': gain(ks[14], (DEPTH, D_MODEL)),
        'norm_mem_g': gain(ks[15], (DEPTH, D_MODEL)),
        'xa_wq': nrm(ks[16], (DEPTH, D_MODEL, D_MODEL), D_MODEL),
        'xa_wk': nrm(ks[17], (DEPTH, D_MODEL, D_MODEL), D_MODEL),
        'xa_wv': nrm(ks[18], (DEPTH, D_MODEL, D_MODEL), D_MODEL),
        'xa_wo': nrm(ks[19], (DEPTH, D_MODEL, D_MODEL), D_MODEL),
        'norm3_g': gain(ks[20], (DEPTH, D_MODEL)),
        'ffn_w_up': nrm(ks[21], (DEPTH, D_MODEL, 2 * D_FF), D_MODEL),
        'ffn_conv_w': nrm(ks[22], (DEPTH, FFN_CONV, 2 * D_FF), FFN_CONV),
        'ffn_conv_b': bias(ks[23], (DEPTH, 2 * D_FF)),
        'ffn_w_down': nrm(ks[24], (DEPTH, D_FF, D_MODEL), D_FF),
        'final_g': gain(ks[25], (D_MODEL,)),
    }


def _fwd_reference(x, mem, positions, norm1_g, w_in, ret_g, rg_conv_w, rg_conv_b, rg_wa, rg_ba,
              rg_wx, rg_bx, rg_lambda, w_out, norm2_g, norm_mem_g, xa_wq, xa_wk, xa_wv, xa_wo,
              norm3_g, ffn_w_up, ffn_conv_w, ffn_conv_b, ffn_w_down, final_g):
    for l in range(DEPTH):
        x = x + hybrid_mixer(rms_norm(x, norm1_g[l]), positions, w_in[l], ret_g[l],
                             rg_conv_w[l], rg_conv_b[l], rg_wa[l], rg_ba[l], rg_wx[l], rg_bx[l],
                             rg_lambda[l], w_out[l])
        memn = rms_norm(mem, norm_mem_g[l])
        x = x + cross_attend(rms_norm(x, norm2_g[l]), memn, xa_wq[l], xa_wk[l], xa_wv[l], xa_wo[l])
        x = x + conv_ffn(rms_norm(x, norm3_g[l]), ffn_w_up[l], ffn_conv_w[l], ffn_conv_b[l], ffn_w_down[l])
    return rms_norm(x, final_g)


import jax as _jax
import jax.numpy as _jnp

TWIN_FORMAT = 'train_step'
FWD_PARAMS = ['x', 'mem', 'positions', 'norm1_g', 'w_in', 'ret_g', 'rg_conv_w', 'rg_conv_b', 'rg_wa', 'rg_ba', 'rg_wx', 'rg_bx', 'rg_lambda', 'w_out', 'norm2_g', 'norm_mem_g', 'xa_wq', 'xa_wk', 'xa_wv', 'xa_wo', 'norm3_g', 'ffn_w_up', 'ffn_conv_w', 'ffn_conv_b', 'ffn_w_down', 'final_g']
TWIN_WEIGHTS = ['norm1_g', 'w_in', 'ret_g', 'rg_conv_w', 'rg_conv_b', 'rg_wa', 'rg_ba', 'rg_wx', 'rg_bx', 'rg_lambda', 'w_out', 'norm2_g', 'norm_mem_g', 'xa_wq', 'xa_wk', 'xa_wv', 'xa_wo', 'norm3_g', 'ffn_w_up', 'ffn_conv_w', 'ffn_conv_b', 'ffn_w_down', 'final_g']
TWIN_DIFF_INPUT = 'x'
TWIN_INPUTS = ['x', 'mem', 'positions', 'norm1_g', 'w_in', 'ret_g', 'rg_conv_w', 'rg_conv_b', 'rg_wa', 'rg_ba', 'rg_wx', 'rg_bx', 'rg_lambda', 'w_out', 'norm2_g', 'norm_mem_g', 'xa_wq', 'xa_wk', 'xa_wv', 'xa_wo', 'norm3_g', 'ffn_w_up', 'ffn_conv_w', 'ffn_conv_b', 'ffn_w_down', 'final_g', 'loss_target', 'm_norm1_g', 'm_w_in', 'm_ret_g', 'm_rg_conv_w', 'm_rg_conv_b', 'm_rg_wa', 'm_rg_ba', 'm_rg_wx', 'm_rg_bx', 'm_rg_lambda', 'm_w_out', 'm_norm2_g', 'm_norm_mem_g', 'm_xa_wq', 'm_xa_wk', 'm_xa_wv', 'm_xa_wo', 'm_norm3_g', 'm_ffn_w_up', 'm_ffn_conv_w', 'm_ffn_conv_b', 'm_ffn_w_down', 'm_final_g', 'v_norm1_g', 'v_w_in', 'v_ret_g', 'v_rg_conv_w', 'v_rg_conv_b', 'v_rg_wa', 'v_rg_ba', 'v_rg_wx', 'v_rg_bx', 'v_rg_lambda', 'v_w_out', 'v_norm2_g', 'v_norm_mem_g', 'v_xa_wq', 'v_xa_wk', 'v_xa_wv', 'v_xa_wo', 'v_norm3_g', 'v_ffn_w_up', 'v_ffn_conv_w', 'v_ffn_conv_b', 'v_ffn_w_down', 'v_final_g']
TWIN_OUTPUTS = ['loss', 'grad_x', 'grad_norm1_g', 'grad_w_in', 'grad_ret_g', 'grad_rg_conv_w', 'grad_rg_conv_b', 'grad_rg_wa', 'grad_rg_ba', 'grad_rg_wx', 'grad_rg_bx', 'grad_rg_lambda', 'grad_w_out', 'grad_norm2_g', 'grad_norm_mem_g', 'grad_xa_wq', 'grad_xa_wk', 'grad_xa_wv', 'grad_xa_wo', 'grad_norm3_g', 'grad_ffn_w_up', 'grad_ffn_conv_w', 'grad_ffn_conv_b', 'grad_ffn_w_down', 'grad_final_g', 'delta_norm1_g', 'delta_w_in', 'delta_ret_g', 'delta_rg_conv_w', 'delta_rg_conv_b', 'delta_rg_wa', 'delta_rg_ba', 'delta_rg_wx', 'delta_rg_bx', 'delta_rg_lambda', 'delta_w_out', 'delta_norm2_g', 'delta_norm_mem_g', 'delta_xa_wq', 'delta_xa_wk', 'delta_xa_wv', 'delta_xa_wo', 'delta_norm3_g', 'delta_ffn_w_up', 'delta_ffn_conv_w', 'delta_ffn_conv_b', 'delta_ffn_w_down', 'delta_final_g', 'new_m_norm1_g', 'new_m_w_in', 'new_m_ret_g', 'new_m_rg_conv_w', 'new_m_rg_conv_b', 'new_m_rg_wa', 'new_m_rg_ba', 'new_m_rg_wx', 'new_m_rg_bx', 'new_m_rg_lambda', 'new_m_w_out', 'new_m_norm2_g', 'new_m_norm_mem_g', 'new_m_xa_wq', 'new_m_xa_wk', 'new_m_xa_wv', 'new_m_xa_wo', 'new_m_norm3_g', 'new_m_ffn_w_up', 'new_m_ffn_conv_w', 'new_m_ffn_conv_b', 'new_m_ffn_w_down', 'new_m_final_g', 'new_v_norm1_g', 'new_v_w_in', 'new_v_ret_g', 'new_v_rg_conv_w', 'new_v_rg_conv_b', 'new_v_rg_wa', 'new_v_rg_ba', 'new_v_rg_wx', 'new_v_rg_bx', 'new_v_rg_lambda', 'new_v_w_out', 'new_v_norm2_g', 'new_v_norm_mem_g', 'new_v_xa_wq', 'new_v_xa_wk', 'new_v_xa_wv', 'new_v_xa_wo', 'new_v_norm3_g', 'new_v_ffn_w_up', 'new_v_ffn_conv_w', 'new_v_ffn_conv_b', 'new_v_ffn_w_down', 'new_v_final_g']
TWIN_LEAF_KINDS = {'loss': 'loss', 'grad_x': 'grad_x', 'grad_norm1_g': 'grad_w', 'grad_w_in': 'grad_w', 'grad_ret_g': 'grad_w', 'grad_rg_conv_w': 'grad_w', 'grad_rg_conv_b': 'grad_w', 'grad_rg_wa': 'grad_w', 'grad_rg_ba': 'grad_w', 'grad_rg_wx': 'grad_w', 'grad_rg_bx': 'grad_w', 'grad_rg_lambda': 'grad_w', 'grad_w_out': 'grad_w', 'grad_norm2_g': 'grad_w', 'grad_norm_mem_g': 'grad_w', 'grad_xa_wq': 'grad_w', 'grad_xa_wk': 'grad_w', 'grad_xa_wv': 'grad_w', 'grad_xa_wo': 'grad_w', 'grad_norm3_g': 'grad_w', 'grad_ffn_w_up': 'grad_w', 'grad_ffn_conv_w': 'grad_w', 'grad_ffn_conv_b': 'grad_w', 'grad_ffn_w_down': 'grad_w', 'grad_final_g': 'grad_w', 'delta_norm1_g': 'delta_w', 'delta_w_in': 'delta_w', 'delta_ret_g': 'delta_w', 'delta_rg_conv_w': 'delta_w', 'delta_rg_conv_b': 'delta_w', 'delta_rg_wa': 'delta_w', 'delta_rg_ba': 'delta_w', 'delta_rg_wx': 'delta_w', 'delta_rg_bx': 'delta_w', 'delta_rg_lambda': 'delta_w', 'delta_w_out': 'delta_w', 'delta_norm2_g': 'delta_w', 'delta_norm_mem_g': 'delta_w', 'delta_xa_wq': 'delta_w', 'delta_xa_wk': 'delta_w', 'delta_xa_wv': 'delta_w', 'delta_xa_wo': 'delta_w', 'delta_norm3_g': 'delta_w', 'delta_ffn_w_up': 'delta_w', 'delta_ffn_conv_w': 'delta_w', 'delta_ffn_conv_b': 'delta_w', 'delta_ffn_w_down': 'delta_w', 'delta_final_g': 'delta_w', 'new_m_norm1_g': 'new_m', 'new_m_w_in': 'new_m', 'new_m_ret_g': 'new_m', 'new_m_rg_conv_w': 'new_m', 'new_m_rg_conv_b': 'new_m', 'new_m_rg_wa': 'new_m', 'new_m_rg_ba': 'new_m', 'new_m_rg_wx': 'new_m', 'new_m_rg_bx': 'new_m', 'new_m_rg_lambda': 'new_m', 'new_m_w_out': 'new_m', 'new_m_norm2_g': 'new_m', 'new_m_norm_mem_g': 'new_m', 'new_m_xa_wq': 'new_m', 'new_m_xa_wk': 'new_m', 'new_m_xa_wv': 'new_m', 'new_m_xa_wo': 'new_m', 'new_m_norm3_g': 'new_m', 'new_m_ffn_w_up': 'new_m', 'new_m_ffn_conv_w': 'new_m', 'new_m_ffn_conv_b': 'new_m', 'new_m_ffn_w_down': 'new_m', 'new_m_final_g': 'new_m', 'new_v_norm1_g': 'new_v', 'new_v_w_in': 'new_v', 'new_v_ret_g': 'new_v', 'new_v_rg_conv_w': 'new_v', 'new_v_rg_conv_b': 'new_v', 'new_v_rg_wa': 'new_v', 'new_v_rg_ba': 'new_v', 'new_v_rg_wx': 'new_v', 'new_v_rg_bx': 'new_v', 'new_v_rg_lambda': 'new_v', 'new_v_w_out': 'new_v', 'new_v_norm2_g': 'new_v', 'new_v_norm_mem_g': 'new_v', 'new_v_xa_wq': 'new_v', 'new_v_xa_wk': 'new_v', 'new_v_xa_wv': 'new_v', 'new_v_xa_wo': 'new_v', 'new_v_norm3_g': 'new_v', 'new_v_ffn_w_up': 'new_v', 'new_v_ffn_conv_w': 'new_v', 'new_v_ffn_conv_b': 'new_v', 'new_v_ffn_w_down': 'new_v', 'new_v_final_g': 'new_v'}


def _forward(args):
    return _fwd_reference(*[args[k] for k in FWD_PARAMS])


def _output_shape():
    def fwd():
        inp = _fwd_setup_inputs(0)
        return _fwd_reference(*[inp[k] for k in FWD_PARAMS])
    out = _jax.eval_shape(fwd)
    return out.shape, out.dtype

N_MICROBATCH = 1
ADAM_LR = 0.001
ADAM_B1 = 0.9
ADAM_B2 = 0.999
ADAM_EPS = 1e-08
ADAM_WD = 0.01
ADAM_STEP = 10
PER_EXAMPLE_BATCH_AXIS = {'x': 0, 'mem': 0, 'positions': 0, 'loss_target': 0}
SHARED_INPUTS = []
_WEIGHT_DTYPES = {'norm1_g': _jnp.float32, 'w_in': _jnp.float32, 'ret_g': _jnp.float32, 'rg_conv_w': _jnp.float32, 'rg_conv_b': _jnp.float32, 'rg_wa': _jnp.float32, 'rg_ba': _jnp.float32, 'rg_wx': _jnp.float32, 'rg_bx': _jnp.float32, 'rg_lambda': _jnp.float32, 'w_out': _jnp.float32, 'norm2_g': _jnp.float32, 'norm_mem_g': _jnp.float32, 'xa_wq': _jnp.float32, 'xa_wk': _jnp.float32, 'xa_wv': _jnp.float32, 'xa_wo': _jnp.float32, 'norm3_g': _jnp.float32, 'ffn_w_up': _jnp.float32, 'ffn_conv_w': _jnp.float32, 'ffn_conv_b': _jnp.float32, 'ffn_w_down': _jnp.float32, 'final_g': _jnp.float32}
MOMENT_SCALE = {'norm1_g': 1.180464e-01, 'w_in': 6.823049e-02, 'ret_g': 7.500237e-02, 'rg_conv_w': 5.414441e-02, 'rg_conv_b': 6.483201e-01, 'rg_wa': 1.849302e-02, 'rg_ba': 1.386131e-02, 'rg_wx': 3.261265e-02, 'rg_bx': 1.619744e-02, 'rg_lambda': 2.859293e-02, 'w_out': 6.447664e-02, 'norm2_g': 1.289781e-02, 'norm_mem_g': 1.942413e-02, 'xa_wq': 1.274703e-02, 'xa_wk': 1.274202e-02, 'xa_wv': 1.302082e-02, 'xa_wo': 1.299287e-02, 'norm3_g': 8.672066e-02, 'ffn_w_up': 3.706483e-02, 'ffn_conv_w': 3.673757e-02, 'ffn_conv_b': 3.576203e-02, 'ffn_w_down': 6.063393e-02, 'final_g': 3.197194e+01}


def _to_microbatches(a, axis):
    t = _jnp.moveaxis(a, axis, 0)
    t = t.reshape((N_MICROBATCH, t.shape[0] // N_MICROBATCH) + t.shape[1:])
    return _jnp.moveaxis(t, 1, axis + 1)


def setup_inputs(seed: int = 0) -> dict:
    inp = _fwd_setup_inputs(seed)
    key = _jax.random.fold_in(_jax.random.key(seed), 7919)
    shape, _ = _output_shape()
    out = dict(inp)
    out["loss_target"] = _jax.random.normal(_jax.random.fold_in(key, 0), shape, _jnp.float32)
    for i, name in enumerate(TWIN_WEIGHTS):
        w = inp[name].astype(_jnp.float32)
        if MOMENT_SCALE is None:
            s = _jnp.sqrt(_jnp.mean(_jnp.square(w)) + 1e-30)
        else:
            s = MOMENT_SCALE[name]
        km, kv = _jax.random.split(_jax.random.fold_in(key, i + 1))
        out[name] = w
        out["m_" + name] = s * _jax.random.normal(km, w.shape, _jnp.float32)
        out["v_" + name] = (s * s) * _jax.random.uniform(kv, w.shape, _jnp.float32, 0.5, 1.5)
    if N_MICROBATCH > 1:
        for name, axis in PER_EXAMPLE_BATCH_AXIS.items():
            out[name] = _to_microbatches(out[name], axis)
    return {'x': out['x'], 'mem': out['mem'], 'positions': out['positions'], 'norm1_g': out['norm1_g'], 'w_in': out['w_in'], 'ret_g': out['ret_g'], 'rg_conv_w': out['rg_conv_w'], 'rg_conv_b': out['rg_conv_b'], 'rg_wa': out['rg_wa'], 'rg_ba': out['rg_ba'], 'rg_wx': out['rg_wx'], 'rg_bx': out['rg_bx'], 'rg_lambda': out['rg_lambda'], 'w_out': out['w_out'], 'norm2_g': out['norm2_g'], 'norm_mem_g': out['norm_mem_g'], 'xa_wq': out['xa_wq'], 'xa_wk': out['xa_wk'], 'xa_wv': out['xa_wv'], 'xa_wo': out['xa_wo'], 'norm3_g': out['norm3_g'], 'ffn_w_up': out['ffn_w_up'], 'ffn_conv_w': out['ffn_conv_w'], 'ffn_conv_b': out['ffn_conv_b'], 'ffn_w_down': out['ffn_w_down'], 'final_g': out['final_g'], 'loss_target': out['loss_target'], 'm_norm1_g': out['m_norm1_g'], 'm_w_in': out['m_w_in'], 'm_ret_g': out['m_ret_g'], 'm_rg_conv_w': out['m_rg_conv_w'], 'm_rg_conv_b': out['m_rg_conv_b'], 'm_rg_wa': out['m_rg_wa'], 'm_rg_ba': out['m_rg_ba'], 'm_rg_wx': out['m_rg_wx'], 'm_rg_bx': out['m_rg_bx'], 'm_rg_lambda': out['m_rg_lambda'], 'm_w_out': out['m_w_out'], 'm_norm2_g': out['m_norm2_g'], 'm_norm_mem_g': out['m_norm_mem_g'], 'm_xa_wq': out['m_xa_wq'], 'm_xa_wk': out['m_xa_wk'], 'm_xa_wv': out['m_xa_wv'], 'm_xa_wo': out['m_xa_wo'], 'm_norm3_g': out['m_norm3_g'], 'm_ffn_w_up': out['m_ffn_w_up'], 'm_ffn_conv_w': out['m_ffn_conv_w'], 'm_ffn_conv_b': out['m_ffn_conv_b'], 'm_ffn_w_down': out['m_ffn_w_down'], 'm_final_g': out['m_final_g'], 'v_norm1_g': out['v_norm1_g'], 'v_w_in': out['v_w_in'], 'v_ret_g': out['v_ret_g'], 'v_rg_conv_w': out['v_rg_conv_w'], 'v_rg_conv_b': out['v_rg_conv_b'], 'v_rg_wa': out['v_rg_wa'], 'v_rg_ba': out['v_rg_ba'], 'v_rg_wx': out['v_rg_wx'], 'v_rg_bx': out['v_rg_bx'], 'v_rg_lambda': out['v_rg_lambda'], 'v_w_out': out['v_w_out'], 'v_norm2_g': out['v_norm2_g'], 'v_norm_mem_g': out['v_norm_mem_g'], 'v_xa_wq': out['v_xa_wq'], 'v_xa_wk': out['v_xa_wk'], 'v_xa_wv': out['v_xa_wv'], 'v_xa_wo': out['v_xa_wo'], 'v_norm3_g': out['v_norm3_g'], 'v_ffn_w_up': out['v_ffn_w_up'], 'v_ffn_conv_w': out['v_ffn_conv_w'], 'v_ffn_conv_b': out['v_ffn_conv_b'], 'v_ffn_w_down': out['v_ffn_w_down'], 'v_final_g': out['v_final_g']}


def _loss(weights, diff, rest, loss_target):
    with _jax.named_scope("forward"):
        args = {**rest, TWIN_DIFF_INPUT: diff, **{k: w.astype(_WEIGHT_DTYPES[k]) for k, w in weights.items()}}
        y = _forward(args)
    with _jax.named_scope("loss_head"):
        err = _jnp.square(y.astype(_jnp.float32) - loss_target)
        return 0.5 * _jnp.sum(_jnp.mean(err, axis=-1)) if err.ndim else 0.5 * err


def _adamw(w, g, m, v):
    m = ADAM_B1 * m + (1.0 - ADAM_B1) * g
    v = ADAM_B2 * v + (1.0 - ADAM_B2) * _jnp.square(g)
    m_hat = m / (1.0 - ADAM_B1 ** ADAM_STEP)
    v_hat = v / (1.0 - ADAM_B2 ** ADAM_STEP)
    delta = -ADAM_LR * (m_hat / (_jnp.sqrt(v_hat) + ADAM_EPS) + ADAM_WD * w)
    return delta, m, v


def reference(x, mem, positions, norm1_g, w_in, ret_g, rg_conv_w, rg_conv_b, rg_wa, rg_ba, rg_wx, rg_bx, rg_lambda, w_out, norm2_g, norm_mem_g, xa_wq, xa_wk, xa_wv, xa_wo, norm3_g, ffn_w_up, ffn_conv_w, ffn_conv_b, ffn_w_down, final_g, loss_target, m_norm1_g, m_w_in, m_ret_g, m_rg_conv_w, m_rg_conv_b, m_rg_wa, m_rg_ba, m_rg_wx, m_rg_bx, m_rg_lambda, m_w_out, m_norm2_g, m_norm_mem_g, m_xa_wq, m_xa_wk, m_xa_wv, m_xa_wo, m_norm3_g, m_ffn_w_up, m_ffn_conv_w, m_ffn_conv_b, m_ffn_w_down, m_final_g, v_norm1_g, v_w_in, v_ret_g, v_rg_conv_w, v_rg_conv_b, v_rg_wa, v_rg_ba, v_rg_wx, v_rg_bx, v_rg_lambda, v_w_out, v_norm2_g, v_norm_mem_g, v_xa_wq, v_xa_wk, v_xa_wv, v_xa_wo, v_norm3_g, v_ffn_w_up, v_ffn_conv_w, v_ffn_conv_b, v_ffn_w_down, v_final_g):
    given = dict(x=x, mem=mem, positions=positions, norm1_g=norm1_g, w_in=w_in, ret_g=ret_g, rg_conv_w=rg_conv_w, rg_conv_b=rg_conv_b, rg_wa=rg_wa, rg_ba=rg_ba, rg_wx=rg_wx, rg_bx=rg_bx, rg_lambda=rg_lambda, w_out=w_out, norm2_g=norm2_g, norm_mem_g=norm_mem_g, xa_wq=xa_wq, xa_wk=xa_wk, xa_wv=xa_wv, xa_wo=xa_wo, norm3_g=norm3_g, ffn_w_up=ffn_w_up, ffn_conv_w=ffn_conv_w, ffn_conv_b=ffn_conv_b, ffn_w_down=ffn_w_down, final_g=final_g, loss_target=loss_target, m_norm1_g=m_norm1_g, m_w_in=m_w_in, m_ret_g=m_ret_g, m_rg_conv_w=m_rg_conv_w, m_rg_conv_b=m_rg_conv_b, m_rg_wa=m_rg_wa, m_rg_ba=m_rg_ba, m_rg_wx=m_rg_wx, m_rg_bx=m_rg_bx, m_rg_lambda=m_rg_lambda, m_w_out=m_w_out, m_norm2_g=m_norm2_g, m_norm_mem_g=m_norm_mem_g, m_xa_wq=m_xa_wq, m_xa_wk=m_xa_wk, m_xa_wv=m_xa_wv, m_xa_wo=m_xa_wo, m_norm3_g=m_norm3_g, m_ffn_w_up=m_ffn_w_up, m_ffn_conv_w=m_ffn_conv_w, m_ffn_conv_b=m_ffn_conv_b, m_ffn_w_down=m_ffn_w_down, m_final_g=m_final_g, v_norm1_g=v_norm1_g, v_w_in=v_w_in, v_ret_g=v_ret_g, v_rg_conv_w=v_rg_conv_w, v_rg_conv_b=v_rg_conv_b, v_rg_wa=v_rg_wa, v_rg_ba=v_rg_ba, v_rg_wx=v_rg_wx, v_rg_bx=v_rg_bx, v_rg_lambda=v_rg_lambda, v_w_out=v_w_out, v_norm2_g=v_norm2_g, v_norm_mem_g=v_norm_mem_g, v_xa_wq=v_xa_wq, v_xa_wk=v_xa_wk, v_xa_wv=v_xa_wv, v_xa_wo=v_xa_wo, v_norm3_g=v_norm3_g, v_ffn_w_up=v_ffn_w_up, v_ffn_conv_w=v_ffn_conv_w, v_ffn_conv_b=v_ffn_conv_b, v_ffn_w_down=v_ffn_w_down, v_final_g=v_final_g)
    weights = {n: given[n] for n in TWIN_WEIGHTS}
    shared = {n: given[n] for n in SHARED_INPUTS}
    per_example = {n: given[n] for n in ['x', 'mem', 'positions']}
    grad_fn = _jax.value_and_grad(_loss, argnums=(0, 1))

    def one_microbatch(ex, loss_target):
        ex = dict(ex)
        diff = ex.pop(TWIN_DIFF_INPUT)
        return grad_fn(weights, diff, {**shared, **ex}, loss_target)

    if N_MICROBATCH == 1:
        loss, (grad_w, grad_x) = one_microbatch(per_example, given["loss_target"])
    else:
        def body(carry, xs):
            loss_sum, grad_sum = carry
            l_k, (gw_k, gx_k) = one_microbatch(xs[0], xs[1])
            with _jax.named_scope("update"):
                return (loss_sum + l_k, _jax.tree.map(_jnp.add, grad_sum, gw_k)), gx_k

        init = (_jnp.zeros((), _jnp.float32), _jax.tree.map(_jnp.zeros_like, weights))
        (loss, grad_w), grad_x = _jax.lax.scan(body, init, (per_example, given["loss_target"]))
    with _jax.named_scope("update"):
        delta_w, new_m, new_v = {}, {}, {}
        for n in TWIN_WEIGHTS:
            delta_w[n], new_m[n], new_v[n] = _adamw(weights[n], grad_w[n], given["m_" + n], given["v_" + n])
    return (loss, grad_x, *[grad_w[n] for n in TWIN_WEIGHTS], *[delta_w[n] for n in TWIN_WEIGHTS],
            *[new_m[n] for n in TWIN_WEIGHTS], *[new_v[n] for n in TWIN_WEIGHTS])
```

```python
import functools
import math

import jax
import jax.numpy as jnp
from jax import lax
from jax.experimental import pallas as pl
from jax.experimental.pallas import tpu as pltpu

F32 = jnp.float32
BF16 = jnp.bfloat16

EPS = 1e-6
RET_HEADS = 4
RET_CHUNK = 128
ROPE_BASE = 10000.0
LRU_BLOCKS = 8
LRU_C = 8.0
XA_HEADS = 4

ADAM_LR = 0.001
ADAM_B1 = 0.9
ADAM_B2 = 0.999
ADAM_EPS = 1e-08
ADAM_WD = 0.01
ADAM_STEP = 10

N_DEV = 8
N_CHIP = 4
MESH = pl.DeviceIdType.MESH
SUB = 8
LANE = 128
VMEM_LIMIT = 56 * 1024 * 1024

NN = ((1,), (0,))
NT = ((1,), (1,))
TN = ((0,), (0,))


def _cparams(sem):
    return pltpu.CompilerParams(dimension_semantics=sem, vmem_limit_bytes=VMEM_LIMIT)


def _sigmoid(v):
    return 1.0 / (1.0 + jnp.exp(-v))


def _bdot(a, b, dims):
    return lax.dot_general(a.astype(BF16), b.astype(BF16), (dims, ((), ())), preferred_element_type=F32)


def _row_iota(shape):
    return lax.broadcasted_iota(jnp.int32, shape, 0)


def _shift_down(v, tail, k):
    if k == 0:
        return v
    r = pltpu.roll(v, k, 0)
    rt = pltpu.roll(tail, k, 0)
    first = jnp.where(_row_iota(rt.shape) < k, rt, r[0:SUB])
    return jnp.concatenate([first, r[SUB:]], axis=0)


def _shift_up(v, head, k):
    if k == 0:
        return v
    n = v.shape[0]
    r = pltpu.roll(v, n - k, 0)
    rh = pltpu.roll(head, SUB - k, 0)
    last = jnp.where(_row_iota(rh.shape) >= SUB - k, rh, r[n - SUB:n])
    return jnp.concatenate([r[:n - SUB], last], axis=0)


def _mm(a, b, *, mode, M, N, K, out_dtype, name, tm=512, tn=512, tk=512, a_off=(0, 0), b_off=(0, 0),
        res=None, out=None, out_off=(0, 0), out_full=None):
    tm, tn, tk = min(tm, M), min(tn, N), min(tk, K)
    assert M % tm == 0 and N % tn == 0 and K % tk == 0, (name, M, N, K, tm, tn, tk)
    nk = K // tk
    if mode == "nn":
        a_blk, b_blk, dims = (tm, tk), (tk, tn), NN
        a_map = lambda i, j, k: (i + a_off[0] // tm, k + a_off[1] // tk)
        b_map = lambda i, j, k: (k + b_off[0] // tk, j + b_off[1] // tn)
    elif mode == "nt":
        a_blk, b_blk, dims = (tm, tk), (tn, tk), NT
        a_map = lambda i, j, k: (i + a_off[0] // tm, k + a_off[1] // tk)
        b_map = lambda i, j, k: (j + b_off[0] // tn, k + b_off[1] // tk)
    else:
        a_blk, b_blk, dims = (tk, tm), (tk, tn), TN
        a_map = lambda i, j, k: (k + a_off[0] // tk, i + a_off[1] // tm)
        b_map = lambda i, j, k: (k + b_off[0] // tk, j + b_off[1] // tn)
    for off, blk in ((a_off, a_blk), (b_off, b_blk), (out_off, (tm, tn))):
        assert off[0] % blk[0] == 0 and off[1] % blk[1] == 0, (name, off, blk)
    o_map = lambda i, j, k: (i + out_off[0] // tm, j + out_off[1] // tn)
    has_res, has_out = res is not None, out is not None

    def body(*refs):
        refs = list(refs)
        a_ref, b_ref = refs[0], refs[1]
        pos = 2
        r_ref = None
        if has_res:
            r_ref = refs[pos]
            pos += 1
        if has_out:
            pos += 1
        o_ref = refs[pos]
        acc = refs[pos + 1] if nk > 1 else None
        k = pl.program_id(2)
        part = _bdot(a_ref[...], b_ref[...], dims)

        def finish(total):
            if has_res:
                total = total + r_ref[...].astype(F32)
            o_ref[...] = total.astype(o_ref.dtype)

        if nk == 1:
            finish(part)
        else:
            @pl.when(k == 0)
            def _():
                acc[...] = part

            @pl.when(k > 0)
            def _():
                acc[...] += part

            @pl.when(k == nk - 1)
            def _():
                finish(acc[...])

    in_specs = [pl.BlockSpec(a_blk, a_map), pl.BlockSpec(b_blk, b_map)]
    args = [a, b]
    if has_res:
        in_specs.append(pl.BlockSpec((tm, tn), lambda i, j, k: (i, j)))
        args.append(res)
    aliases = {}
    if has_out:
        in_specs.append(pl.BlockSpec(memory_space=pl.ANY))
        aliases = {len(args): 0}
        args.append(out)
        out_shape = jax.ShapeDtypeStruct(out.shape, out.dtype)
    else:
        out_shape = jax.ShapeDtypeStruct((M, N) if out_full is None else out_full, out_dtype)
    return pl.pallas_call(
        body, name=name, grid=(M // tm, N // tn, nk), in_specs=in_specs,
        out_specs=pl.BlockSpec((tm, tn), o_map), out_shape=out_shape,
        scratch_shapes=[pltpu.VMEM((tm, tn), F32)] if nk > 1 else [],
        input_output_aliases=aliases,
        compiler_params=_cparams(("parallel", "parallel", "arbitrary")),
    )(*args)


def _rmsnorm_fwd(x, g, *, name, ts=512):
    S, D = x.shape
    ts = min(ts, S)

    def body(x_ref, g_ref, o_ref):
        xv = x_ref[...]
        r = lax.rsqrt(jnp.mean(xv * xv, axis=-1, keepdims=True) + EPS)
        o_ref[...] = (xv * r * g_ref[...]).astype(o_ref.dtype)

    return pl.pallas_call(
        body, name=name, grid=(S // ts,),
        in_specs=[pl.BlockSpec((ts, D), lambda i: (i, 0)), pl.BlockSpec((1, D), lambda i: (0, 0))],
        out_specs=pl.BlockSpec((ts, D), lambda i: (i, 0)),
        out_shape=jax.ShapeDtypeStruct((S, D), BF16),
        compiler_params=_cparams(("parallel",)),
    )(x, g)


def _rmsnorm_bwd(x, dxn, g, res, *, name, ts=256):
    S, D = x.shape
    ts = min(ts, S)
    has_res = res is not None

    def body(*refs):
        if has_res:
            x_ref, d_ref, g_ref, r_ref, dx_ref, dg_ref = refs
        else:
            x_ref, d_ref, g_ref, dx_ref, dg_ref = refs
        i = pl.program_id(0)
        xv = x_ref[...]
        dv = d_ref[...].astype(F32)
        r = lax.rsqrt(jnp.mean(xv * xv, axis=-1, keepdims=True) + EPS)
        gd = dv * g_ref[...]
        proj = jnp.mean(xv * gd, axis=-1, keepdims=True)
        dx = r * gd - xv * (r * r * r) * proj
        if has_res:
            dx = dx + r_ref[...]
        dx_ref[...] = dx
        part = jnp.sum(dv * xv * r, axis=0, keepdims=True)

        @pl.when(i == 0)
        def _():
            dg_ref[...] = part

        @pl.when(i > 0)
        def _():
            dg_ref[...] += part

    row = pl.BlockSpec((ts, D), lambda i: (i, 0))
    vec = pl.BlockSpec((1, D), lambda i: (0, 0))
    in_specs = [row, row, vec] + ([row] if has_res else [])
    args = [x, dxn, g] + ([res] if has_res else [])
    return pl.pallas_call(
        body, name=name, grid=(S // ts,), in_specs=in_specs, out_specs=[row, vec],
        out_shape=[jax.ShapeDtypeStruct((S, D), F32), jax.ShapeDtypeStruct((1, D), F32)],
        compiler_params=_cparams(("arbitrary",)),
    )(*args)


def _final_loss(x, target, g, *, name, ts=256):
    S, D = x.shape
    ts = min(ts, S)

    def body(x_ref, t_ref, g_ref, dx_ref, dg_ref, loss_ref):
        i = pl.program_id(0)
        xv = x_ref[...]
        gv = g_ref[...]
        r = lax.rsqrt(jnp.mean(xv * xv, axis=-1, keepdims=True) + EPS)
        y = xv * r * gv
        err = y - t_ref[...]
        row_loss = jnp.mean(err * err, axis=-1, keepdims=True)
        lpart = 0.5 * jnp.sum(row_loss, axis=0, keepdims=True)
        dy = err * (1.0 / D)
        gd = dy * gv
        proj = jnp.mean(xv * gd, axis=-1, keepdims=True)
        dx_ref[...] = r * gd - xv * (r * r * r) * proj
        part = jnp.sum(dy * xv * r, axis=0, keepdims=True)
        lfull = jnp.broadcast_to(lpart, loss_ref.shape)

        @pl.when(i == 0)
        def _():
            dg_ref[...] = part
            loss_ref[...] = lfull

        @pl.when(i > 0)
        def _():
            dg_ref[...] += part
            loss_ref[...] += lfull

    row = pl.BlockSpec((ts, D), lambda i: (i, 0))
    vec = pl.BlockSpec((1, D), lambda i: (0, 0))
    return pl.pallas_call(
        body, name=name, grid=(S // ts,), in_specs=[row, row, vec],
        out_specs=[row, vec, pl.BlockSpec((SUB, LANE), lambda i: (0, 0))],
        out_shape=[jax.ShapeDtypeStruct((S, D), F32), jax.ShapeDtypeStruct((1, D), F32),
                   jax.ShapeDtypeStruct((SUB, LANE), F32)],
        compiler_params=_cparams(("arbitrary",)),
    )(x, target, g)


def _rope_table(pos_col, inv, *, name, ts=1024):
    S = pos_col.shape[0]
    ts = min(ts, S)
    half = inv.shape[1]

    def body(p_ref, inv_ref, c_ref, s_ref):
        ang = p_ref[...].astype(F32) * inv_ref[...]
        c_ref[...] = jnp.cos(ang)
        s_ref[...] = jnp.sin(ang)

    tab = pl.BlockSpec((ts, half), lambda i: (i, 0))
    return pl.pallas_call(
        body, name=name, grid=(S // ts,),
        in_specs=[pl.BlockSpec((ts, 1), lambda i: (i, 0)), pl.BlockSpec((1, half), lambda i: (0, 0))],
        out_specs=[tab, tab],
        out_shape=[jax.ShapeDtypeStruct((S, half), F32), jax.ShapeDtypeStruct((S, half), F32)],
        compiler_params=_cparams(("parallel",)),
    )(pos_col, inv)


def _ret_consts(C, log_g):
    ii = lax.broadcasted_iota(jnp.int32, (C, C), 0)
    jj = lax.broadcasted_iota(jnp.int32, (C, C), 1)
    diff = (ii - jj).astype(F32)
    intra = jnp.where(ii >= jj, jnp.exp(log_g * jnp.maximum(diff, 0.0)), 0.0)
    idx = lax.broadcasted_iota(jnp.int32, (C, 1), 0).astype(F32)
    qd = jnp.exp(log_g * (idx + 1.0))
    kd = jnp.exp(log_g * (C - 1.0 - idx))
    cd = math.exp(log_g * C)
    return intra, qd, kd, cd


def _rot(t, cs, sn):
    half = t.shape[-1] // 2
    t1, t2 = t[:, :half], t[:, half:]
    return jnp.concatenate([t1 * cs - t2 * sn, t1 * sn + t2 * cs], axis=-1)


def _unrot(d, cs, sn):
    half = d.shape[-1] // 2
    d1, d2 = d[:, :half], d[:, half:]
    return jnp.concatenate([d1 * cs + d2 * sn, d2 * cs - d1 * sn], axis=-1)


def _ret_fwd(h, cos, sin, ret_g, mix, *, name, ch=2):
    S = h.shape[0]
    R = ret_g.shape[1]
    H, C = RET_HEADS, RET_CHUNK
    Dh = R // H
    ts = ch * C
    assert S % ts == 0
    log_gs = [math.log(1.0 - 2.0 ** (-5.0 - hd)) for hd in range(H)]
    scale = Dh ** -0.5

    def body(x_ref, c_ref, s_ref, rg_ref, mix_in, ret_ref, st_ref, mix_ref, state):
        i = pl.program_id(0)

        @pl.when(i == 0)
        def _():
            state[...] = jnp.zeros_like(state)

        for c in range(ch):
            rows = pl.ds(c * C, C)
            cs, sn = c_ref[rows, :], s_ref[rows, :]
            for hd in range(H):
                intra, qd, kd, cd = _ret_consts(C, log_gs[hd])
                q = x_ref[rows, pl.ds(hd * Dh, Dh)]
                k = x_ref[rows, pl.ds(R + hd * Dh, Dh)]
                v = x_ref[rows, pl.ds(2 * R + hd * Dh, Dh)]
                g = x_ref[rows, pl.ds(3 * R + hd * Dh, Dh)]
                rq = _rot(q, cs, sn)
                rk = _rot(k, cs, sn) * scale
                st = state[hd]
                st_ref[c, hd] = st.astype(BF16)
                s_ = _bdot(rq, rk, NT) * intra
                ret = _bdot(s_, v, NN) + _bdot(rq * qd, st, NN)
                state[hd] = st * cd + _bdot(rk * kd, v, TN)
                ret_ref[rows, pl.ds(hd * Dh, Dh)] = ret
                rr = lax.rsqrt(jnp.mean(ret * ret, axis=-1, keepdims=True) + EPS)
                out = ret * rr * rg_ref[:, pl.ds(hd * Dh, Dh)] * (g * _sigmoid(g))
                mix_ref[rows, pl.ds(hd * Dh, Dh)] = out.astype(BF16)

    n_chunks = S // C
    return pl.pallas_call(
        body, name=name, grid=(S // ts,),
        in_specs=[pl.BlockSpec((ts, 4 * R), lambda i: (i, 0)),
                  pl.BlockSpec((ts, Dh // 2), lambda i: (i, 0)), pl.BlockSpec((ts, Dh // 2), lambda i: (i, 0)),
                  pl.BlockSpec((1, R), lambda i: (0, 0)), pl.BlockSpec(memory_space=pl.ANY)],
        out_specs=[pl.BlockSpec((ts, R), lambda i: (i, 0)),
                   pl.BlockSpec((ch, H, Dh, Dh), lambda i: (i, 0, 0, 0)),
                   pl.BlockSpec((ts, R), lambda i: (i, 0))],
        out_shape=[jax.ShapeDtypeStruct((S, R), F32), jax.ShapeDtypeStruct((n_chunks, H, Dh, Dh), BF16),
                   jax.ShapeDtypeStruct(mix.shape, mix.dtype)],
        scratch_shapes=[pltpu.VMEM((H, Dh, Dh), F32)],
        input_output_aliases={4: 2},
        compiler_params=_cparams(("arbitrary",)),
    )(h, cos, sin, ret_g, mix)


def _ret_bwd(h, cos, sin, ret_g, states, ret_raw, dmix, *, name, ch=2):
    S = h.shape[0]
    R = ret_g.shape[1]
    H, C = RET_HEADS, RET_CHUNK
    Dh = R // H
    ts = ch * C
    nb = S // ts
    log_gs = [math.log(1.0 - 2.0 ** (-5.0 - hd)) for hd in range(H)]
    scale = Dh ** -0.5

    def body(x_ref, c_ref, s_ref, rg_ref, st_ref, ret_ref, dm_ref, dh_ref, drg_ref, dstate):
        i = pl.program_id(0)

        @pl.when(i == 0)
        def _():
            dstate[...] = jnp.zeros_like(dstate)
            drg_ref[...] = jnp.zeros_like(drg_ref)

        for c in reversed(range(ch)):
            rows = pl.ds(c * C, C)
            cs, sn = c_ref[rows, :], s_ref[rows, :]
            for hd in range(H):
                intra, qd, kd, cd = _ret_consts(C, log_gs[hd])
                cols = pl.ds(hd * Dh, Dh)
                q = x_ref[rows, pl.ds(hd * Dh, Dh)]
                k = x_ref[rows, pl.ds(R + hd * Dh, Dh)]
                v = x_ref[rows, pl.ds(2 * R + hd * Dh, Dh)]
                g = x_ref[rows, pl.ds(3 * R + hd * Dh, Dh)]
                rq = _rot(q, cs, sn)
                rk = _rot(k, cs, sn) * scale
                ret = ret_ref[rows, cols]
                dm = dm_ref[rows, cols]
                rgv = rg_ref[:, cols]
                rr = lax.rsqrt(jnp.mean(ret * ret, axis=-1, keepdims=True) + EPS)
                retn = ret * rr
                sg = _sigmoid(g)
                silu = g * sg
                drg_ref[:, cols] += jnp.sum(dm * retn * silu, axis=0, keepdims=True)
                dg = dm * retn * rgv * (sg * (1.0 + g * (1.0 - sg)))
                dretn = dm * rgv * silu
                d_o = rr * dretn - ret * (rr * rr * rr) * jnp.mean(ret * dretn, axis=-1, keepdims=True)
                st = st_ref[c, hd]
                d_s = dstate[hd]
                a_ = _bdot(rq, rk, NT) * intra
                d_a = _bdot(d_o, v, NT) * intra
                d_qr = _bdot(d_a, rk, NN) + _bdot(d_o, st, NT) * qd
                d_kr = _bdot(d_a, rq, TN) + _bdot(v, d_s, NT) * kd
                d_v = _bdot(a_, d_o, TN) + _bdot(rk * kd, d_s, NN)
                dstate[hd] = d_s * cd + _bdot(rq * qd, d_o, TN)
                dh_ref[rows, pl.ds(hd * Dh, Dh)] = _unrot(d_qr, cs, sn).astype(BF16)
                dh_ref[rows, pl.ds(R + hd * Dh, Dh)] = (_unrot(d_kr, cs, sn) * scale).astype(BF16)
                dh_ref[rows, pl.ds(2 * R + hd * Dh, Dh)] = d_v.astype(BF16)
                dh_ref[rows, pl.ds(3 * R + hd * Dh, Dh)] = dg.astype(BF16)

    rb = lambda i: nb - 1 - i
    return pl.pallas_call(
        body, name=name, grid=(nb,),
        in_specs=[pl.BlockSpec((ts, 4 * R), lambda i: (rb(i), 0)),
                  pl.BlockSpec((ts, Dh // 2), lambda i: (rb(i), 0)), pl.BlockSpec((ts, Dh // 2), lambda i: (rb(i), 0)),
                  pl.BlockSpec((1, R), lambda i: (0, 0)),
                  pl.BlockSpec((ch, H, Dh, Dh), lambda i: (rb(i), 0, 0, 0)),
                  pl.BlockSpec((ts, R), lambda i: (rb(i), 0)),
                  pl.BlockSpec((ts, R), lambda i: (rb(i), 0))],
        out_specs=[pl.BlockSpec((ts, 4 * R), lambda i: (rb(i), 0)), pl.BlockSpec((1, R), lambda i: (0, 0))],
        out_shape=[jax.ShapeDtypeStruct((S, 6 * R), BF16), jax.ShapeDtypeStruct((1, R), F32)],
        scratch_shapes=[pltpu.VMEM((H, Dh, Dh), F32)],
        compiler_params=_cparams(("arbitrary",)),
    )(h, cos, sin, ret_g, states, ret_raw, dmix)


GELU_C = math.sqrt(2.0 / math.pi)
GELU_A = 0.044715


def _gelu_parts(y):
    t = jnp.tanh(GELU_C * (y + GELU_A * y * y * y))
    val = 0.5 * y * (1.0 + t)
    grad = 0.5 * (1.0 + t) + 0.5 * y * (1.0 - t * t) * GELU_C * (1.0 + 3.0 * GELU_A * y * y)
    return val, grad


def _neg_expm1(x):
    series = -x * (1.0 + x * (1.0 / 2.0) * (1.0 + x * (1.0 / 3.0) * (1.0 + x * (1.0 / 4.0) * (
        1.0 + x * (1.0 / 5.0) * (1.0 + x * (1.0 / 6.0) * (1.0 + x * (1.0 / 7.0)))))))
    return jnp.where(x > -0.35, series, 1.0 - jnp.exp(x))


def _log_sigmoid(x):
    return jnp.minimum(x, 0.0) - jnp.log1p(jnp.exp(-jnp.abs(x)))


def _lru_gates(uc, wa_ref, ba_ref, wx_ref, bx_ref):
    nbk = wa_ref.shape[0]
    bd = wa_ref.shape[1]
    rs, gs = [], []
    for n in range(nbk):
        ucn = uc[:, n * bd:(n + 1) * bd]
        rs.append(_sigmoid(_bdot(ucn, wa_ref[n], NN) + ba_ref[:, pl.ds(n * bd, bd)]))
        gs.append(_sigmoid(_bdot(ucn, wx_ref[n], NN) + bx_ref[:, pl.ds(n * bd, bd)]))
    return jnp.concatenate(rs, axis=-1), jnp.concatenate(gs, axis=-1)


def _lru_fwd(h, conv_w, conv_b, wa, ba, wx, bx, lam, *, name, ts=256):
    S = h.shape[0]
    W = lam.shape[1]
    K = conv_w.shape[0]
    ts = min(ts, S)

    def body(u_ref, y_ref, cw_ref, cb_ref, wa_ref, ba_ref, wx_ref, bx_ref, lam_ref, hl_ref, mix_ref, tail, hlast):
        i = pl.program_id(0)

        @pl.when(i == 0)
        def _():
            tail[...] = jnp.zeros_like(tail)
            hlast[...] = jnp.zeros_like(hlast)

        u = u_ref[...]
        tl = tail[...]
        uc = cb_ref[...] + cw_ref[K - 1:K, :] * u
        for k in range(K - 1):
            uc = uc + cw_ref[k:k + 1, :] * _shift_down(u, tl, K - 1 - k)
        tail[...] = u[ts - SUB:ts]
        r, ig = _lru_gates(uc, wa_ref, ba_ref, wx_ref, bx_ref)
        log_a = LRU_C * r * _log_sigmoid(lam_ref[...])
        a = jnp.exp(log_a)
        b = jnp.sqrt(_neg_expm1(2.0 * log_a)) * (ig * uc)
        rid = _row_iota((ts, W))
        d = 1
        while d < ts:
            a_s = jnp.where(rid < d, 1.0, pltpu.roll(a, d, 0))
            b_s = jnp.where(rid < d, 0.0, pltpu.roll(b, d, 0))
            b = a * b_s + b
            a = a * a_s
            d *= 2
        hcur = a * hlast[SUB - 1:SUB, :] + b
        hlast[...] = hcur[ts - SUB:ts]
        hl_ref[...] = hcur
        gy, _ = _gelu_parts(y_ref[...])
        mix_ref[...] = (hcur * gy).astype(BF16)

    full = lambda shape: pl.BlockSpec(shape, lambda i: tuple(0 for _ in shape))
    return pl.pallas_call(
        body, name=name, grid=(S // ts,),
        in_specs=[pl.BlockSpec((ts, W), lambda i: (i, 4)), pl.BlockSpec((ts, W), lambda i: (i, 5)),
                  full(conv_w.shape), full(conv_b.shape), full(wa.shape), full(ba.shape), full(wx.shape),
                  full(bx.shape), full(lam.shape)],
        out_specs=[pl.BlockSpec((ts, W), lambda i: (i, 0)), pl.BlockSpec((ts, W), lambda i: (i, 1))],
        out_shape=[jax.ShapeDtypeStruct((S, W), F32), jax.ShapeDtypeStruct((S, 2 * W), BF16)],
        scratch_shapes=[pltpu.VMEM((SUB, W), F32), pltpu.VMEM((SUB, W), F32)],
        compiler_params=_cparams(("arbitrary",)),
    )(h, h, conv_w, conv_b, wa, ba, wx, bx, lam)


def _lru_bwd(h, hl, dmix, dh, conv_w, conv_b, wa, ba, wx, bx, lam, *, name, ts=256):
    S = h.shape[0]
    W = lam.shape[1]
    K = conv_w.shape[0]
    nbk, bd = wa.shape[0], wa.shape[1]
    ts = min(ts, S)
    nb = S // ts
    t8 = ts // SUB

    def body(u_ref, y_ref, uh_ref, hl_ref, hh_ref, dm_ref, cw_ref, cb_ref, wa_ref, ba_ref, wx_ref, bx_ref, lam_ref,
             dh_in, dh_ref, dcw_ref, dcb_ref, dwa_ref, dba_ref, dwx_ref, dbx_ref, dlam_ref, carry, head):
        i = pl.program_id(0)
        blk = nb - 1 - i

        @pl.when(i == 0)
        def _():
            carry[...] = jnp.zeros_like(carry)
            head[...] = jnp.zeros_like(head)
            for ref in (dcw_ref, dcb_ref, dwa_ref, dba_ref, dwx_ref, dbx_ref, dlam_ref):
                ref[...] = jnp.zeros_like(ref)

        inside = (blk > 0).astype(F32)
        u = u_ref[...]
        tl = uh_ref[...] * inside
        sh = [_shift_down(u, tl, K - 1 - k) for k in range(K)]
        uc = cb_ref[...]
        for k in range(K):
            uc = uc + cw_ref[k:k + 1, :] * sh[k]
        r, ig = _lru_gates(uc, wa_ref, ba_ref, wx_ref, bx_ref)
        lam_v = lam_ref[...]
        ls = _log_sigmoid(lam_v)
        log_a = LRU_C * r * ls
        a = jnp.exp(log_a)
        mult = jnp.sqrt(_neg_expm1(2.0 * log_a))
        hcur = hl_ref[...]
        hprev = _shift_down(hcur, hh_ref[...] * inside, 1)
        gy, dgy = _gelu_parts(y_ref[...])
        dm = dm_ref[...]
        d_y = dm * hcur * dgy
        rid = _row_iota((ts, W))
        bq = dm * gy + jnp.where(rid == ts - 1, carry[0:1, :], 0.0)
        aq = jnp.where(rid == ts - 1, 0.0, pltpu.roll(a, ts - 1, 0))
        d = 1
        while d < ts:
            a_s = jnp.where(rid >= ts - d, 0.0, pltpu.roll(aq, ts - d, 0))
            b_s = jnp.where(rid >= ts - d, 0.0, pltpu.roll(bq, ts - d, 0))
            bq = bq + aq * b_s
            aq = aq * a_s
            d *= 2
        lam_t = bq
        carry[...] = (a * lam_t)[0:SUB]
        d_a = lam_t * hprev
        d_mult = lam_t * (ig * uc)
        d_i = lam_t * mult * uc
        d_uc = lam_t * mult * ig
        d_log_a = d_a * a - d_mult * (a * a) / mult
        d_r = d_log_a * (LRU_C * ls)
        dlam_ref[...] += jnp.sum(d_log_a * (LRU_C * r), axis=0, keepdims=True) * _sigmoid(-lam_v)
        d_pr = d_r * r * (1.0 - r)
        d_pi = d_i * ig * (1.0 - ig)
        dba_ref[...] += jnp.sum(d_pr, axis=0, keepdims=True)
        dbx_ref[...] += jnp.sum(d_pi, axis=0, keepdims=True)
        extra = []
        for n in range(nbk):
            sl = slice(n * bd, (n + 1) * bd)
            ucn = uc[:, sl]
            dwa_ref[n] += _bdot(ucn, d_pr[:, sl], TN)
            dwx_ref[n] += _bdot(ucn, d_pi[:, sl], TN)
            extra.append(_bdot(d_pr[:, sl], wa_ref[n], NT) + _bdot(d_pi[:, sl], wx_ref[n], NT))
        d_uc = d_uc + jnp.concatenate(extra, axis=-1)
        dcb_ref[...] += jnp.sum(d_uc, axis=0, keepdims=True)
        rid8 = _row_iota((SUB, W))
        dcw = jnp.zeros((SUB, W), F32)
        for k in range(K):
            dcw = dcw + jnp.where(rid8 == k, jnp.sum(d_uc * sh[k], axis=0, keepdims=True), 0.0)
        dcw_ref[...] += dcw
        hd = head[...]
        d_u = cw_ref[K - 1:K, :] * d_uc
        for j in range(1, K):
            d_u = d_u + cw_ref[K - 1 - j:K - j, :] * _shift_up(d_uc, hd, j)
        head[...] = d_uc[0:SUB]
        dh_ref[:, 0:W] = d_u.astype(BF16)
        dh_ref[:, W:2 * W] = d_y.astype(BF16)

    rb = lambda i: nb - 1 - i
    prev8 = lambda i: jnp.maximum(rb(i) * t8 - 1, 0)
    full = lambda shape: pl.BlockSpec(shape, lambda i: tuple(0 for _ in shape))
    small = [jax.ShapeDtypeStruct((SUB, W), F32), jax.ShapeDtypeStruct((1, W), F32),
             jax.ShapeDtypeStruct(wa.shape, F32), jax.ShapeDtypeStruct((1, W), F32),
             jax.ShapeDtypeStruct(wx.shape, F32), jax.ShapeDtypeStruct((1, W), F32),
             jax.ShapeDtypeStruct((1, W), F32)]
    return pl.pallas_call(
        body, name=name, grid=(nb,),
        in_specs=[pl.BlockSpec((ts, W), lambda i: (rb(i), 4)), pl.BlockSpec((ts, W), lambda i: (rb(i), 5)),
                  pl.BlockSpec((SUB, W), lambda i: (prev8(i), 4)),
                  pl.BlockSpec((ts, W), lambda i: (rb(i), 0)), pl.BlockSpec((SUB, W), lambda i: (prev8(i), 0)),
                  pl.BlockSpec((ts, W), lambda i: (rb(i), 1)),
                  full(conv_w.shape), full(conv_b.shape), full(wa.shape), full(ba.shape), full(wx.shape),
                  full(bx.shape), full(lam.shape), pl.BlockSpec(memory_space=pl.ANY)],
        out_specs=[pl.BlockSpec((ts, 2 * W), lambda i: (rb(i), 2))] + [full(s.shape) for s in small],
        out_shape=[jax.ShapeDtypeStruct(dh.shape, dh.dtype)] + small,
        scratch_shapes=[pltpu.VMEM((SUB, W), F32), pltpu.VMEM((SUB, W), F32)],
        input_output_aliases={13: 0},
        compiler_params=_cparams(("arbitrary",)),
    )(h, h, h, hl, hl, dmix, conv_w, conv_b, wa, ba, wx, bx, lam, dh)


def _xattn_fwd(q, km, vm, *, name, ts=512):
    S, D = q.shape
    M = km.shape[0]
    H = XA_HEADS
    Dh = D // H
    ts = min(ts, S)
    scale = Dh ** -0.5

    def body(q_ref, k_ref, v_ref, o_ref):
        for hd in range(H):
            cols = pl.ds(hd * Dh, Dh)
            s = _bdot(q_ref[:, cols], k_ref[:, cols], NT) * scale
            s = s - jnp.max(s, axis=-1, keepdims=True)
            e = jnp.exp(s)
            p = e / jnp.sum(e, axis=-1, keepdims=True)
            o_ref[:, cols] = _bdot(p, v_ref[:, cols], NN).astype(o_ref.dtype)

    return pl.pallas_call(
        body, name=name, grid=(S // ts,),
        in_specs=[pl.BlockSpec((ts, D), lambda i: (i, 0)), pl.BlockSpec((M, D), lambda i: (0, 0)),
                  pl.BlockSpec((M, D), lambda i: (0, 0))],
        out_specs=pl.BlockSpec((ts, D), lambda i: (i, 0)),
        out_shape=jax.ShapeDtypeStruct((S, D), BF16),
        compiler_params=_cparams(("parallel",)),
    )(q, km, vm)


def _xattn_bwd(q, km, vm, d_o, *, name, ts=512):
    S, D = q.shape
    M = km.shape[0]
    H = XA_HEADS
    Dh = D // H
    ts = min(ts, S)
    scale = Dh ** -0.5

    def body(q_ref, k_ref, v_ref, do_ref, dq_ref, dk_ref, dv_ref):
        i = pl.program_id(0)

        @pl.when(i == 0)
        def _():
            dk_ref[...] = jnp.zeros_like(dk_ref)
            dv_ref[...] = jnp.zeros_like(dv_ref)

        for hd in range(H):
            cols = pl.ds(hd * Dh, Dh)
            qh, kh, vh, doh = q_ref[:, cols], k_ref[:, cols], v_ref[:, cols], do_ref[:, cols]
            s = _bdot(qh, kh, NT) * scale
            s = s - jnp.max(s, axis=-1, keepdims=True)
            e = jnp.exp(s)
            p = e / jnp.sum(e, axis=-1, keepdims=True)
            dp = _bdot(doh, vh, NT)
            ds = p * (dp - jnp.sum(dp * p, axis=-1, keepdims=True)) * scale
            dq_ref[:, cols] = _bdot(ds, kh, NN).astype(dq_ref.dtype)
            dk_ref[:, cols] += _bdot(ds, qh, TN)
            dv_ref[:, cols] += _bdot(p, doh, TN)

    row = pl.BlockSpec((ts, D), lambda i: (i, 0))
    mem = pl.BlockSpec((M, D), lambda i: (0, 0))
    return pl.pallas_call(
        body, name=name, grid=(S // ts,), in_specs=[row, mem, mem, row], out_specs=[row, mem, mem],
        out_shape=[jax.ShapeDtypeStruct((S, D), BF16), jax.ShapeDtypeStruct((M, D), F32),
                   jax.ShapeDtypeStruct((M, D), F32)],
        compiler_params=_cparams(("arbitrary",)),
    )(q, km, vm, d_o)


def _conv_rows(v, tail, cw_ref, cb_ref):
    K = cw_ref.shape[0]
    sh = [_shift_down(v, tail, K - 1 - k) for k in range(K)]
    out = cb_ref[...]
    for k in range(K):
        out = out + cw_ref[k:k + 1, :] * sh[k]
    return out, sh


def _ffn_gate_fwd(hh, cw, cb, *, name, ts=512, tc=512):
    S, F2 = hh.shape
    F = F2 // 2
    ts, tc = min(ts, S), min(tc, F)
    nj = F // tc
    K = cw.shape[0]

    def body(a_ref, b_ref, cwa_ref, cwb_ref, cba_ref, cbb_ref, o_ref, ta, tb):
        i = pl.program_id(1)

        @pl.when(i == 0)
        def _():
            ta[...] = jnp.zeros_like(ta)
            tb[...] = jnp.zeros_like(tb)

        av, bv = a_ref[...], b_ref[...]
        ac, _ = _conv_rows(av, ta[...], cwa_ref, cba_ref)
        bc, _ = _conv_rows(bv, tb[...], cwb_ref, cbb_ref)
        ta[...] = av[ts - SUB:ts]
        tb[...] = bv[ts - SUB:ts]
        o_ref[...] = (ac * _sigmoid(ac) * bc).astype(o_ref.dtype)

    return pl.pallas_call(
        body, name=name, grid=(nj, S // ts),
        in_specs=[pl.BlockSpec((ts, tc), lambda j, i: (i, j)), pl.BlockSpec((ts, tc), lambda j, i: (i, j + nj)),
                  pl.BlockSpec((K, tc), lambda j, i: (0, j)), pl.BlockSpec((K, tc), lambda j, i: (0, j + nj)),
                  pl.BlockSpec((1, tc), lambda j, i: (0, j)), pl.BlockSpec((1, tc), lambda j, i: (0, j + nj))],
        out_specs=pl.BlockSpec((ts, tc), lambda j, i: (i, j)),
        out_shape=jax.ShapeDtypeStruct((S, F), BF16),
        scratch_shapes=[pltpu.VMEM((SUB, tc), F32), pltpu.VMEM((SUB, tc), F32)],
        compiler_params=_cparams(("parallel", "arbitrary")),
    )(hh, hh, cw, cw, cb, cb)


def _ffn_gate_bwd(hh, dact, cw, cb, *, name, ts=512, tc=512):
    S, F2 = hh.shape
    F = F2 // 2
    ts, tc = min(ts, S), min(tc, F)
    nj = F // tc
    nb = S // ts
    t8 = ts // SUB
    K = cw.shape[0]

    def body(a_ref, ah_ref, b_ref, bh_ref, d_ref, cwa_ref, cwb_ref, cba_ref, cbb_ref,
             da_ref, db_ref, ga_ref, gb_ref, ha, hb):
        i = pl.program_id(1)
        blk = nb - 1 - i

        @pl.when(i == 0)
        def _():
            for ref in (ha, hb, ga_ref, gb_ref):
                ref[...] = jnp.zeros_like(ref)

        inside = (blk > 0).astype(F32)
        ac, sha = _conv_rows(a_ref[...], ah_ref[...] * inside, cwa_ref, cba_ref)
        bc, shb = _conv_rows(b_ref[...], bh_ref[...] * inside, cwb_ref, cbb_ref)
        dv = d_ref[...].astype(F32)
        sg = _sigmoid(ac)
        d_bc = dv * ac * sg
        d_ac = dv * bc * sg * (1.0 + ac * (1.0 - sg))
        rid8 = _row_iota((SUB, tc))
        for d_c, sh, cw_ref, head, o_ref, g_ref in ((d_ac, sha, cwa_ref, ha, da_ref, ga_ref),
                                                     (d_bc, shb, cwb_ref, hb, db_ref, gb_ref)):
            hd = head[...]
            d_in = cw_ref[K - 1:K, :] * d_c
            for j in range(1, K):
                d_in = d_in + cw_ref[K - 1 - j:K - j, :] * _shift_up(d_c, hd, j)
            head[...] = d_c[0:SUB]
            o_ref[...] = d_in.astype(o_ref.dtype)
            gsum = jnp.where(rid8 == K, jnp.sum(d_c, axis=0, keepdims=True), 0.0)
            for k in range(K):
                gsum = gsum + jnp.where(rid8 == k, jnp.sum(d_c * sh[k], axis=0, keepdims=True), 0.0)
            g_ref[...] += gsum

    rb = lambda i: nb - 1 - i
    prev8 = lambda i: jnp.maximum(rb(i) * t8 - 1, 0)
    outs = pl.pallas_call(
        body, name=name, grid=(nj, nb),
        in_specs=[pl.BlockSpec((ts, tc), lambda j, i: (rb(i), j)), pl.BlockSpec((SUB, tc), lambda j, i: (prev8(i), j)),
                  pl.BlockSpec((ts, tc), lambda j, i: (rb(i), j + nj)),
                  pl.BlockSpec((SUB, tc), lambda j, i: (prev8(i), j + nj)),
                  pl.BlockSpec((ts, tc), lambda j, i: (rb(i), j)),
                  pl.BlockSpec((K, tc), lambda j, i: (0, j)), pl.BlockSpec((K, tc), lambda j, i: (0, j + nj)),
                  pl.BlockSpec((1, tc), lambda j, i: (0, j)), pl.BlockSpec((1, tc), lambda j, i: (0, j + nj))],
        out_specs=[pl.BlockSpec((ts, tc), lambda j, i: (rb(i), j)), pl.BlockSpec((ts, tc), lambda j, i: (rb(i), j)),
                   pl.BlockSpec((SUB, tc), lambda j, i: (0, j)), pl.BlockSpec((SUB, tc), lambda j, i: (0, j))],
        out_shape=[jax.ShapeDtypeStruct((S, F), BF16), jax.ShapeDtypeStruct((S, F), BF16),
                   jax.ShapeDtypeStruct((SUB, F), F32), jax.ShapeDtypeStruct((SUB, F), F32)],
        scratch_shapes=[pltpu.VMEM((SUB, tc), F32), pltpu.VMEM((SUB, tc), F32)],
        compiler_params=_cparams(("parallel", "arbitrary")),
    )(hh, hh, hh, hh, dact, cw, cw, cb, cb)
    return outs


ADAM_BLOCK_ELEMS = 128 * 1024


def _adamw(w, m, v, parts, *, name):
    R, C = w.shape
    n = parts.shape[0]
    tr = R
    for cand in (1024, 512, 256, 128, 64, 32, 16):
        if R % cand == 0 and cand * C <= ADAM_BLOCK_ELEMS:
            tr = cand
            break
    c1 = 1.0 - ADAM_B1 ** ADAM_STEP
    c2 = 1.0 - ADAM_B2 ** ADAM_STEP

    def body(w_ref, m_ref, v_ref, p_ref, g_ref, d_ref, nm_ref, nv_ref):
        g = p_ref[0].astype(F32)
        for k in range(1, n):
            g = g + p_ref[k].astype(F32)
        m_new = ADAM_B1 * m_ref[...] + (1.0 - ADAM_B1) * g
        v_new = ADAM_B2 * v_ref[...] + (1.0 - ADAM_B2) * (g * g)
        m_hat = m_new / c1
        v_hat = v_new / c2
        g_ref[...] = g
        d_ref[...] = -ADAM_LR * (m_hat / (jnp.sqrt(v_hat) + ADAM_EPS) + ADAM_WD * w_ref[...])
        nm_ref[...] = m_new
        nv_ref[...] = v_new

    blk = pl.BlockSpec((tr, C), lambda i: (i, 0))
    sds = jax.ShapeDtypeStruct((R, C), F32)
    return pl.pallas_call(
        body, name=name, grid=(R // tr,),
        in_specs=[blk, blk, blk, pl.BlockSpec((n, tr, C), lambda i: (0, i, 0))],
        out_specs=[blk, blk, blk, blk], out_shape=[sds, sds, sds, sds],
        compiler_params=_cparams(("parallel",)),
    )(w, m, v, parts)


def _mesh_place():
    x, y, c = lax.axis_index("x"), lax.axis_index("y"), lax.axis_index("c")
    others = [(1 - x, y), (x, 1 - y), (1 - x, 1 - y)]
    return x, y, c, others


def _gather_weights(shards, axes, splits, *, name):
    n = len(shards)
    fulls = [jax.ShapeDtypeStruct((s.shape[0] * N_CHIP, s.shape[1]) if ax == 0 else (s.shape[0], s.shape[1] * N_CHIP),
                                  s.dtype) for s, ax in zip(shards, axes)]

    def body(*refs):
        srcs, outs = refs[:n], refs[n:2 * n]
        send, recv, fsend, frecv, lsem = refs[2 * n:]
        x, y, c, others = _mesh_place()
        my_chip = 2 * x + y
        sibling = (x, y, 1 - c)

        def region(it, chip, half):
            r, w = srcs[it].shape
            if splits[it]:
                rows = pl.ds(pl.multiple_of(half * (r // 2) + (chip * r if axes[it] == 0 else 0), 16), r // 2)
            else:
                rows = pl.ds(chip * r if axes[it] == 0 else 0, r)
            cols = pl.ds(pl.multiple_of(chip * w, LANE), w) if axes[it] == 1 else pl.ds(0, w)
            return outs[it].at[rows, cols]

        def src_half(it, half):
            r = srcs[it].shape[0]
            if splits[it]:
                return srcs[it].at[pl.ds(pl.multiple_of(half * (r // 2), 16), r // 2), :]
            return srcs[it]

        def whole(it, chip):
            r, w = srcs[it].shape
            rows = pl.ds(chip * r if axes[it] == 0 else 0, r)
            cols = pl.ds(pl.multiple_of(chip * w, LANE), w) if axes[it] == 1 else pl.ds(0, w)
            return outs[it].at[rows, cols]

        def ici(it, j, chip_from):
            return pltpu.make_async_remote_copy(
                src_ref=src_half(it, c), dst_ref=region(it, chip_from, c), send_sem=send.at[3 * it + j],
                recv_sem=recv.at[3 * it + j], device_id=(*others[j], c), device_id_type=MESH)

        def fwd(it, j, half):
            chip = 2 * others[j][0] + others[j][1]
            return pltpu.make_async_remote_copy(
                src_ref=region(it, chip, half), dst_ref=region(it, chip, half), send_sem=fsend.at[3 * it + j],
                recv_sem=frecv.at[3 * it + j], device_id=sibling, device_id_type=MESH)

        local = [pltpu.make_async_copy(srcs[it], whole(it, my_chip), lsem.at[it]) for it in range(n)]
        for cp in local:
            cp.start()
        for it in range(n):
            for j in range(3):
                ici(it, j, my_chip).start()
        for it in range(n):
            for j in range(3):
                ici(it, j, 2 * others[j][0] + others[j][1]).wait_recv()
                if splits[it]:
                    fwd(it, j, c).start()
        for it in range(n):
            for j in range(3):
                if splits[it]:
                    fwd(it, j, 1 - c).wait_recv()
        for it in range(n):
            for j in range(3):
                ici(it, j, my_chip).wait_send()
                if splits[it]:
                    fwd(it, j, c).wait_send()
        for cp in local:
            cp.wait()

    any_spec = pl.BlockSpec(memory_space=pl.ANY)
    return pl.pallas_call(
        body, name=name, in_specs=[any_spec] * n, out_specs=[any_spec] * n, out_shape=fulls,
        scratch_shapes=[pltpu.SemaphoreType.DMA((3 * n,)), pltpu.SemaphoreType.DMA((3 * n,)),
                        pltpu.SemaphoreType.DMA((3 * n,)), pltpu.SemaphoreType.DMA((3 * n,)),
                        pltpu.SemaphoreType.DMA((n,))],
    )(*shards)


def _gather_grads(grads, axes, *, name):
    n = len(grads)
    shard_shapes = [(g.shape[0] // N_CHIP, g.shape[1]) if ax == 0 else (g.shape[0], g.shape[1] // N_CHIP)
                    for g, ax in zip(grads, axes)]
    lands = [jax.ShapeDtypeStruct((N_DEV, *s), g.dtype) for s, g in zip(shard_shapes, grads)]

    def body(*refs):
        srcs, outs = refs[:n], refs[n:2 * n]
        send, recv, fsend, frecv, ssend, srecv, lsem = refs[2 * n:]
        x, y, c, others = _mesh_place()
        my_chip = 2 * x + y
        me = 2 * my_chip + c
        sibling = (x, y, 1 - c)

        def piece(it, chip):
            r, w = shard_shapes[it]
            if axes[it] == 0:
                return srcs[it].at[pl.ds(pl.multiple_of(chip * r, 16), r), :]
            return srcs[it].at[:, pl.ds(pl.multiple_of(chip * w, LANE), w)]

        def ici(it, j, slot):
            chip_to = 2 * others[j][0] + others[j][1]
            return pltpu.make_async_remote_copy(
                src_ref=piece(it, chip_to), dst_ref=outs[it].at[slot], send_sem=send.at[3 * it + j],
                recv_sem=recv.at[3 * it + j], device_id=(*others[j], c), device_id_type=MESH)

        def fwd(it, j, slot):
            return pltpu.make_async_remote_copy(
                src_ref=outs[it].at[slot], dst_ref=outs[it].at[slot], send_sem=fsend.at[3 * it + j],
                recv_sem=frecv.at[3 * it + j], device_id=sibling, device_id_type=MESH)

        def own(it, slot):
            return pltpu.make_async_remote_copy(
                src_ref=piece(it, my_chip), dst_ref=outs[it].at[slot], send_sem=ssend.at[it],
                recv_sem=srecv.at[it], device_id=sibling, device_id_type=MESH)

        local = [pltpu.make_async_copy(piece(it, my_chip), outs[it].at[me], lsem.at[it]) for it in range(n)]
        for it in range(n):
            local[it].start()
            own(it, me).start()
            for j in range(3):
                ici(it, j, me).start()
        for it in range(n):
            for j in range(3):
                slot = 2 * (2 * others[j][0] + others[j][1]) + c
                ici(it, j, slot).wait_recv()
                fwd(it, j, slot).start()
        for it in range(n):
            own(it, 2 * my_chip + 1 - c).wait_recv()
            for j in range(3):
                fwd(it, j, 2 * (2 * others[j][0] + others[j][1]) + 1 - c).wait_recv()
        for it in range(n):
            own(it, me).wait_send()
            for j in range(3):
                ici(it, j, me).wait_send()
                fwd(it, j, 2 * (2 * others[j][0] + others[j][1]) + c).wait_send()
            local[it].wait()

    any_spec = pl.BlockSpec(memory_space=pl.ANY)
    return pl.pallas_call(
        body, name=name, in_specs=[any_spec] * n, out_specs=[any_spec] * n, out_shape=lands,
        scratch_shapes=[pltpu.SemaphoreType.DMA((3 * n,)), pltpu.SemaphoreType.DMA((3 * n,)),
                        pltpu.SemaphoreType.DMA((3 * n,)), pltpu.SemaphoreType.DMA((3 * n,)),
                        pltpu.SemaphoreType.DMA((n,)), pltpu.SemaphoreType.DMA((n,)),
                        pltpu.SemaphoreType.DMA((n,))],
    )(*grads)


def _allreduce_small(vec, *, name):
    R, L = vec.shape

    def body(v_ref, o_ref, buf, send, recv, lsem):
        x, y, c, others = _mesh_place()
        me = 4 * x + 2 * y + c
        sibling = (x, y, 1 - c)

        def copy(k, slot, to, src=None):
            return pltpu.make_async_remote_copy(
                src_ref=buf.at[slot] if src is None else src, dst_ref=buf.at[slot], send_sem=send.at[k],
                recv_sem=recv.at[k], device_id=to, device_id_type=MESH)

        def slot_of(chip, core):
            return 4 * chip[0] + 2 * chip[1] + core

        mine = pltpu.make_async_copy(v_ref, buf.at[me], lsem)
        mine.start()
        first = [copy(0, me, sibling, src=v_ref)]
        first += [copy(1 + j, me, (*chip, c), src=v_ref) for j, chip in enumerate(others)]
        for cp in first:
            cp.start()
        passed = [copy(4 + j, slot_of(chip, c), sibling) for j, chip in enumerate(others)]
        for j, chip in enumerate(others):
            copy(1 + j, slot_of(chip, c), (*chip, c)).wait_recv()
            passed[j].start()
        copy(0, slot_of((x, y), 1 - c), sibling).wait_recv()
        for j, chip in enumerate(others):
            copy(4 + j, slot_of(chip, 1 - c), sibling).wait_recv()
        for cp in first + passed:
            cp.wait_send()
        mine.wait()
        total = buf[0]
        for k in range(1, N_DEV):
            total = total + buf[k]
        o_ref[...] = total

    return pl.pallas_call(
        body, name=name, in_specs=[pl.BlockSpec(memory_space=pltpu.VMEM)],
        out_specs=pl.BlockSpec(memory_space=pltpu.VMEM), out_shape=jax.ShapeDtypeStruct((R, L), F32),
        scratch_shapes=[pltpu.VMEM((N_DEV, R, L), F32), pltpu.SemaphoreType.DMA((7,)), pltpu.SemaphoreType.DMA((7,)),
                        pltpu.SemaphoreType.DMA],
        compiler_params=pltpu.CompilerParams(vmem_limit_bytes=VMEM_LIMIT),
    )(vec)


PACK_ALIGN = 1024


def _pack(arrs, row_multiple):
    flat = []
    for a in arrs:
        f = a.reshape(-1).astype(F32)
        flat.append(jnp.pad(f, (0, (-f.shape[0]) % PACK_ALIGN)))
    v = jnp.concatenate(flat)
    v = jnp.pad(v, (0, (-v.shape[0]) % (LANE * row_multiple)))
    return v.reshape(-1, LANE)


def _unpack(v, shapes):
    flat = v.reshape(-1)
    out, off = [], 0
    for s in shapes:
        size = math.prod(s)
        out.append(flat[off:off + size].reshape(s))
        off += size + (-size) % PACK_ALIGN
    return out


def _tile(dim, target):
    for cand in (1024, 512, 256, 128):
        if cand <= target and dim % cand == 0:
            return cand
    return dim


WEIGHT_NAMES = ('norm1_g', 'w_in', 'ret_g', 'rg_conv_w', 'rg_conv_b', 'rg_wa', 'rg_ba', 'rg_wx', 'rg_bx', 'rg_lambda',
                'w_out', 'norm2_g', 'norm_mem_g', 'xa_wq', 'xa_wk', 'xa_wv', 'xa_wo', 'norm3_g', 'ffn_w_up',
                'ffn_conv_w', 'ffn_conv_b', 'ffn_w_down', 'final_g')
BIG_AXIS = {'w_in': 1, 'w_out': 0, 'xa_wq': 0, 'xa_wk': 0, 'xa_wv': 0, 'xa_wo': 0, 'ffn_w_up': 1, 'ffn_w_down': 0}
SMALL_SHARDED = ('rg_conv_w', 'ffn_conv_w')


def _step(x, mem, positions, loss_target, W, Mo, Vo):
    S, D = x.shape[1], x.shape[2]
    xs, mems, tgt = x[0], mem[0], loss_target[0]
    n_mem = mems.shape[0]
    pos_col = positions.reshape(S, 1)
    chip = 2 * lax.axis_index("x") + lax.axis_index("y")

    big = list(BIG_AXIS)
    shards = {n: W[n][0] for n in big}
    gathered = _gather_weights(
        [shards[n].astype(BF16) for n in big] + [W[n][0] for n in SMALL_SHARDED],
        [BIG_AXIS[n] for n in big] + [1, 1], [True] * len(big) + [False, False], name="gather_weights")
    G = dict(zip(big + list(SMALL_SHARDED), gathered))
    R = W['ret_g'].shape[1]
    Wl = W['rg_lambda'].shape[1]
    IN = G['w_in'].shape[1]
    F2 = G['ffn_w_up'].shape[1]
    F = F2 // 2

    norm1_g, norm2_g, norm3_g = W['norm1_g'], W['norm2_g'], W['norm3_g']
    norm_mem_g, final_g, ret_g = W['norm_mem_g'], W['final_g'].reshape(1, D), W['ret_g']
    rg_cw, rg_cb = G['rg_conv_w'], W['rg_conv_b']
    wa, wx = W['rg_wa'][0], W['rg_wx'][0]
    ba, bx = W['rg_ba'].reshape(1, Wl), W['rg_bx'].reshape(1, Wl)
    lam = W['rg_lambda']
    ffn_cw, ffn_cb = G['ffn_conv_w'], W['ffn_conv_b']

    def fwd_mm(a, wname, N, K, **kw):
        return _mm(a, G[wname], mode="nn", M=a.shape[0], N=N, K=K, tm=_tile(a.shape[0], 1024), tn=512,
                   tk=_tile(K, 2048) if K <= 2048 else 512, **kw)

    def bwd_x_mm(d, wname, N, K, **kw):
        return _mm(d, G[wname], mode="nt", M=d.shape[0], N=N, K=K, tm=_tile(d.shape[0], 1024), tn=_tile(N, 1024),
                   tk=_tile(K, 1024), **kw)

    def bwd_w_mm(a, d, M, N, **kw):
        Ks = a.shape[0]
        return _mm(a, d, mode="tn", M=M, N=N, K=Ks, out_dtype=BF16, tm=_tile(M, 1024), tn=_tile(N, 1024),
                   tk=_tile(Ks, 512), **kw)

    xn1 = _rmsnorm_fwd(xs, norm1_g, name="norm1_fwd")
    h = fwd_mm(xn1, 'w_in', IN, D, out_dtype=F32, name="mm_in")
    half = (R // RET_HEADS) // 2
    inv = (ROPE_BASE ** (-jnp.arange(half, dtype=F32) / half)).reshape(1, half)
    cos, sin = _rope_table(pos_col, inv, name="rope_table")
    hl, mix = _lru_fwd(h, rg_cw, rg_cb, wa, ba, wx, bx, lam, name="lru_fwd")
    ret_raw, states, mix = _ret_fwd(h, cos, sin, ret_g, mix, name="ret_fwd")
    x1 = fwd_mm(mix, 'w_out', D, D, out_dtype=F32, res=xs, name="mm_out")
    memn = _rmsnorm_fwd(mems, norm_mem_g, name="norm_mem_fwd")
    km = fwd_mm(memn, 'xa_wk', D, D, out_dtype=BF16, name="mm_k")
    vm = fwd_mm(memn, 'xa_wv', D, D, out_dtype=BF16, name="mm_v")
    xn2 = _rmsnorm_fwd(x1, norm2_g, name="norm2_fwd")
    q = fwd_mm(xn2, 'xa_wq', D, D, out_dtype=BF16, name="mm_q")
    o = _xattn_fwd(q, km, vm, name="xattn_fwd")
    x2 = fwd_mm(o, 'xa_wo', D, D, out_dtype=F32, res=x1, name="mm_o")
    xn3 = _rmsnorm_fwd(x2, norm3_g, name="norm3_fwd")
    hh = fwd_mm(xn3, 'ffn_w_up', F2, D, out_dtype=F32, name="mm_up")
    act = _ffn_gate_fwd(hh, ffn_cw, ffn_cb, name="ffn_gate_fwd")
    x3 = fwd_mm(act, 'ffn_w_down', D, F, out_dtype=F32, res=x2, name="mm_down")
    dx3, d_final, loss8 = _final_loss(x3, tgt, final_g, name="final_loss")

    gw = {}
    dact = bwd_x_mm(dx3, 'ffn_w_down', F, D, out_dtype=F32, name="mm_dact")
    gw['ffn_w_down'] = bwd_w_mm(act, dx3, F, D, name="mm_dw_down")
    dhh_a, dhh_b, gcw_a, gcw_b = _ffn_gate_bwd(hh, dact, ffn_cw, ffn_cb, name="ffn_gate_bwd")
    gw_up = bwd_w_mm(xn3, dhh_a, D, F, out_full=(D, F2), name="mm_dw_up_a")
    gw['ffn_w_up'] = bwd_w_mm(xn3, dhh_b, D, F, out=gw_up, out_off=(0, F), name="mm_dw_up_b")
    dxn3 = bwd_x_mm(dhh_a, 'ffn_w_up', D, F, out_dtype=F32, name="mm_dxn3_a")
    dxn3 = bwd_x_mm(dhh_b, 'ffn_w_up', D, F, out_dtype=F32, b_off=(0, F), res=dxn3, name="mm_dxn3_b")
    dx2, d_norm3 = _rmsnorm_bwd(x2, dxn3, norm3_g, dx3, name="norm3_bwd")
    Kc = ffn_cw.shape[0]
    d_ffn_cw = jnp.concatenate([gcw_a[:Kc], gcw_b[:Kc]], axis=1)
    d_ffn_cb = jnp.concatenate([gcw_a[Kc:Kc + 1], gcw_b[Kc:Kc + 1]], axis=1)

    d_o = bwd_x_mm(dx2, 'xa_wo', D, D, out_dtype=BF16, name="mm_do")
    gw['xa_wo'] = bwd_w_mm(o, dx2, D, D, name="mm_dw_o")
    dq, dk, dv = _xattn_bwd(q, km, vm, d_o, name="xattn_bwd")
    gw['xa_wq'] = bwd_w_mm(xn2, dq, D, D, name="mm_dw_q")
    dxn2 = bwd_x_mm(dq, 'xa_wq', D, D, out_dtype=F32, name="mm_dxn2")
    gw['xa_wk'] = bwd_w_mm(memn, dk, D, D, name="mm_dw_k")
    gw['xa_wv'] = bwd_w_mm(memn, dv, D, D, name="mm_dw_v")
    dmemn = bwd_x_mm(dk, 'xa_wk', D, D, out_dtype=F32, name="mm_dmem_k")
    dmemn = bwd_x_mm(dv, 'xa_wv', D, D, out_dtype=F32, res=dmemn, name="mm_dmem_v")
    _, d_norm_mem = _rmsnorm_bwd(mems, dmemn, norm_mem_g, None, name="norm_mem_bwd")
    dx1, d_norm2 = _rmsnorm_bwd(x1, dxn2, norm2_g, dx2, name="norm2_bwd")

    dmix = bwd_x_mm(dx1, 'w_out', D, D, out_dtype=F32, name="mm_dmix")
    gw['w_out'] = bwd_w_mm(mix, dx1, D, D, name="mm_dw_out")
    dh, d_ret_g = _ret_bwd(h, cos, sin, ret_g, states, ret_raw, dmix, name="ret_bwd")
    dh, d_rcw, d_rcb, d_wa, d_ba, d_wx, d_bx, d_lam = _lru_bwd(
        h, hl, dmix, dh, rg_cw, rg_cb, wa, ba, wx, bx, lam, name="lru_bwd")
    gw['w_in'] = bwd_w_mm(xn1, dh, D, IN, name="mm_dw_in")
    dxn1 = bwd_x_mm(dh, 'w_in', D, IN, out_dtype=F32, name="mm_dxn1")
    grad_x, d_norm1 = _rmsnorm_bwd(xs, dxn1, norm1_g, dx1, name="norm1_bwd")

    lands = _gather_grads([gw[n] for n in big], [BIG_AXIS[n] for n in big], name="gather_grads")
    small_parts = {
        'norm1_g': d_norm1, 'ret_g': d_ret_g, 'rg_conv_w': d_rcw[:rg_cw.shape[0]], 'rg_conv_b': d_rcb,
        'rg_wa': d_wa, 'rg_ba': d_ba, 'rg_wx': d_wx, 'rg_bx': d_bx, 'rg_lambda': d_lam, 'norm2_g': d_norm2,
        'norm_mem_g': d_norm_mem, 'norm3_g': d_norm3, 'ffn_conv_w': d_ffn_cw, 'ffn_conv_b': d_ffn_cb,
        'final_g': d_final}
    small = [n for n in WEIGHT_NAMES if n not in BIG_AXIS]
    red_shapes = [(1,)] + [tuple(small_parts[n].shape) for n in small]
    reduced = _allreduce_small(_pack([loss8[0:1, 0:1]] + [small_parts[n] for n in small], SUB), name="allreduce_small")
    red = _unpack(reduced, red_shapes)
    loss = red[0][0]
    g_small = dict(zip(small, red[1:]))
    for n in SMALL_SHARDED:
        w_local = W[n].shape[-1]
        g_small[n] = lax.dynamic_slice_in_dim(g_small[n], chip * w_local, w_local, axis=1)

    out_g, out_d, out_m, out_v = {}, {}, {}, {}
    for n, land in zip(big, lands):
        g, d, m_new, v_new = _adamw(shards[n], Mo[n][0], Vo[n][0], land, name="adamw_" + n)
        out_g[n], out_d[n], out_m[n], out_v[n] = (t.reshape(W[n].shape) for t in (g, d, m_new, v_new))
    rows = 512
    pk = lambda d: _pack([d[n] for n in small], rows)
    g_pack = _pack([g_small[n] for n in small], rows)
    res_small = _adamw(pk(W), pk(Mo), pk(Vo), g_pack[None], name="adamw_small")
    shapes_small = [tuple(W[n].shape) for n in small]
    for dst, packed in zip((out_g, out_d, out_m, out_v), res_small):
        for n, val in zip(small, _unpack(packed, shapes_small)):
            dst[n] = val
    return (loss, grad_x[None], *[out_g[n] for n in WEIGHT_NAMES], *[out_d[n] for n in WEIGHT_NAMES],
            *[out_m[n] for n in WEIGHT_NAMES], *[out_v[n] for n in WEIGHT_NAMES])


def kernel(x, mem, positions, norm1_g, w_in, ret_g, rg_conv_w, rg_conv_b, rg_wa, rg_ba, rg_wx, rg_bx, rg_lambda, w_out, norm2_g, norm_mem_g, xa_wq, xa_wk, xa_wv, xa_wo, norm3_g, ffn_w_up, ffn_conv_w, ffn_conv_b, ffn_w_down, final_g, loss_target, m_norm1_g, m_w_in, m_ret_g, m_rg_conv_w, m_rg_conv_b, m_rg_wa, m_rg_ba, m_rg_wx, m_rg_bx, m_rg_lambda, m_w_out, m_norm2_g, m_norm_mem_g, m_xa_wq, m_xa_wk, m_xa_wv, m_xa_wo, m_norm3_g, m_ffn_w_up, m_ffn_conv_w, m_ffn_conv_b, m_ffn_w_down, m_final_g, v_norm1_g, v_w_in, v_ret_g, v_rg_conv_w, v_rg_conv_b, v_rg_wa, v_rg_ba, v_rg_wx, v_rg_bx, v_rg_lambda, v_w_out, v_norm2_g, v_norm_mem_g, v_xa_wq, v_xa_wk, v_xa_wv, v_xa_wo, v_norm3_g, v_ffn_w_up, v_ffn_conv_w, v_ffn_conv_b, v_ffn_w_down, v_final_g):
    W = dict(zip(WEIGHT_NAMES, (norm1_g, w_in, ret_g, rg_conv_w, rg_conv_b, rg_wa, rg_ba, rg_wx, rg_bx, rg_lambda, w_out,
                                norm2_g, norm_mem_g, xa_wq, xa_wk, xa_wv, xa_wo, norm3_g, ffn_w_up, ffn_conv_w,
                                ffn_conv_b, ffn_w_down, final_g)))
    Mo = dict(zip(WEIGHT_NAMES, (m_norm1_g, m_w_in, m_ret_g, m_rg_conv_w, m_rg_conv_b, m_rg_wa, m_rg_ba, m_rg_wx, m_rg_bx,
                                 m_rg_lambda, m_w_out, m_norm2_g, m_norm_mem_g, m_xa_wq, m_xa_wk, m_xa_wv, m_xa_wo,
                                 m_norm3_g, m_ffn_w_up, m_ffn_conv_w, m_ffn_conv_b, m_ffn_w_down, m_final_g)))
    Vo = dict(zip(WEIGHT_NAMES, (v_norm1_g, v_w_in, v_ret_g, v_rg_conv_w, v_rg_conv_b, v_rg_wa, v_rg_ba, v_rg_wx, v_rg_bx,
                                 v_rg_lambda, v_w_out, v_norm2_g, v_norm_mem_g, v_xa_wq, v_xa_wk, v_xa_wv, v_xa_wo,
                                 v_norm3_g, v_ffn_w_up, v_ffn_conv_w, v_ffn_conv_b, v_ffn_w_down, v_final_g)))
    return _step(x, mem, positions, loss_target, W, Mo, Vo)
```

```python
import functools
import math

import jax
import jax.numpy as jnp
from jax import lax
from jax.experimental import pallas as pl
from jax.experimental.pallas import tpu as pltpu

F32 = jnp.float32
BF16 = jnp.bfloat16

EPS = 1e-6
RET_HEADS = 4
RET_CHUNK = 128
ROPE_BASE = 10000.0
LRU_BLOCKS = 8
LRU_C = 8.0
XA_HEADS = 4

ADAM_LR = 0.001
ADAM_B1 = 0.9
ADAM_B2 = 0.999
ADAM_EPS = 1e-08
ADAM_WD = 0.01
ADAM_STEP = 10

N_DEV = 8
N_CHIP = 4
MESH = pl.DeviceIdType.MESH
SUB = 8
LANE = 128
VMEM_LIMIT = 56 * 1024 * 1024

NN = ((1,), (0,))
NT = ((1,), (1,))
TN = ((0,), (0,))


def _cparams(sem):
    return pltpu.CompilerParams(dimension_semantics=sem, vmem_limit_bytes=VMEM_LIMIT)


def _sigmoid(v):
    return 1.0 / (1.0 + jnp.exp(-v))


def _bdot(a, b, dims):
    return lax.dot_general(a.astype(BF16), b.astype(BF16), (dims, ((), ())), preferred_element_type=F32)


def _row_iota(shape):
    return lax.broadcasted_iota(jnp.int32, shape, 0)


def _shift_down(v, tail, k):
    if k == 0:
        return v
    r = pltpu.roll(v, k, 0)
    rt = pltpu.roll(tail, k, 0)
    first = jnp.where(_row_iota(rt.shape) < k, rt, r[0:SUB])
    return jnp.concatenate([first, r[SUB:]], axis=0)


def _shift_up(v, head, k):
    if k == 0:
        return v
    n = v.shape[0]
    r = pltpu.roll(v, n - k, 0)
    rh = pltpu.roll(head, SUB - k, 0)
    last = jnp.where(_row_iota(rh.shape) >= SUB - k, rh, r[n - SUB:n])
    return jnp.concatenate([r[:n - SUB], last], axis=0)


def _mm(a, b, *, mode, M, N, K, out_dtype, name, tm=512, tn=512, tk=512, a_off=(0, 0), b_off=(0, 0),
        res=None, out=None, out_off=(0, 0), out_full=None, norm_g=None):
    tm, tn, tk = min(tm, M), min(tn, N), min(tk, K)
    assert M % tm == 0 and N % tn == 0 and K % tk == 0, (name, M, N, K, tm, tn, tk)
    nk = K // tk
    if mode == "nn":
        a_blk, b_blk, dims = (tm, tk), (tk, tn), NN
        a_map = lambda i, j, k: (i + a_off[0] // tm, k + a_off[1] // tk)
        b_map = lambda i, j, k: (k + b_off[0] // tk, j + b_off[1] // tn)
    elif mode == "nt":
        a_blk, b_blk, dims = (tm, tk), (tn, tk), NT
        a_map = lambda i, j, k: (i + a_off[0] // tm, k + a_off[1] // tk)
        b_map = lambda i, j, k: (j + b_off[0] // tn, k + b_off[1] // tk)
    else:
        a_blk, b_blk, dims = (tk, tm), (tk, tn), TN
        a_map = lambda i, j, k: (k + a_off[0] // tk, i + a_off[1] // tm)
        b_map = lambda i, j, k: (k + b_off[0] // tk, j + b_off[1] // tn)
    for off, blk in ((a_off, a_blk), (b_off, b_blk), (out_off, (tm, tn))):
        assert off[0] % blk[0] == 0 and off[1] % blk[1] == 0, (name, off, blk)
    o_map = lambda i, j, k: (i + out_off[0] // tm, j + out_off[1] // tn)
    has_res, has_out, has_norm = res is not None, out is not None, norm_g is not None
    assert not has_norm or (tn == N and not has_out)

    def body(*refs):
        refs = list(refs)
        a_ref, b_ref = refs[0], refs[1]
        pos = 2
        r_ref = g_ref = n_ref = None
        if has_res:
            r_ref = refs[pos]
            pos += 1
        if has_norm:
            g_ref = refs[pos]
            pos += 1
        if has_out:
            pos += 1
        o_ref = refs[pos]
        pos += 1
        if has_norm:
            n_ref = refs[pos]
            pos += 1
        acc = refs[pos] if nk > 1 else None
        k = pl.program_id(2)
        part = _bdot(a_ref[...], b_ref[...], dims)

        def finish(total):
            if has_res:
                total = total + r_ref[...].astype(F32)
            o_ref[...] = total.astype(o_ref.dtype)
            if has_norm:
                r = lax.rsqrt(jnp.mean(total * total, axis=-1, keepdims=True) + EPS)
                n_ref[...] = (total * r * g_ref[...]).astype(n_ref.dtype)

        if nk == 1:
            finish(part)
        else:
            @pl.when(k == 0)
            def _():
                acc[...] = part

            @pl.when(k > 0)
            def _():
                acc[...] += part

            @pl.when(k == nk - 1)
            def _():
                finish(acc[...])

    in_specs = [pl.BlockSpec(a_blk, a_map), pl.BlockSpec(b_blk, b_map)]
    args = [a, b]
    if has_res:
        in_specs.append(pl.BlockSpec((tm, tn), lambda i, j, k: (i, j)))
        args.append(res)
    if has_norm:
        in_specs.append(pl.BlockSpec((1, N), lambda i, j, k: (0, 0)))
        args.append(norm_g)
    aliases = {}
    if has_out:
        in_specs.append(pl.BlockSpec(memory_space=pl.ANY))
        aliases = {len(args): 0}
        args.append(out)
        out_shape = jax.ShapeDtypeStruct(out.shape, out.dtype)
    else:
        out_shape = jax.ShapeDtypeStruct((M, N) if out_full is None else out_full, out_dtype)
    out_specs = pl.BlockSpec((tm, tn), o_map)
    if has_norm:
        out_shape = [out_shape, jax.ShapeDtypeStruct((M, N), BF16)]
        out_specs = [out_specs, pl.BlockSpec((tm, tn), lambda i, j, k: (i, j))]
    return pl.pallas_call(
        body, name=name, grid=(M // tm, N // tn, nk), in_specs=in_specs,
        out_specs=out_specs, out_shape=out_shape,
        scratch_shapes=[pltpu.VMEM((tm, tn), F32)] if nk > 1 else [],
        input_output_aliases=aliases,
        compiler_params=_cparams(("parallel", "parallel", "arbitrary")),
    )(*args)


def _rmsnorm_fwd(x, g, *, name, ts=512):
    S, D = x.shape
    ts = min(ts, S)

    def body(x_ref, g_ref, o_ref):
        xv = x_ref[...]
        r = lax.rsqrt(jnp.mean(xv * xv, axis=-1, keepdims=True) + EPS)
        o_ref[...] = (xv * r * g_ref[...]).astype(o_ref.dtype)

    return pl.pallas_call(
        body, name=name, grid=(S // ts,),
        in_specs=[pl.BlockSpec((ts, D), lambda i: (i, 0)), pl.BlockSpec((1, D), lambda i: (0, 0))],
        out_specs=pl.BlockSpec((ts, D), lambda i: (i, 0)),
        out_shape=jax.ShapeDtypeStruct((S, D), BF16),
        compiler_params=_cparams(("parallel",)),
    )(x, g)


def _rmsnorm_bwd(x, dxn, g, res, *, name, ts=256, emit_bf16=False):
    S, D = x.shape
    ts = min(ts, S)
    has_res = res is not None

    def body(*refs):
        refs = list(refs)
        dx16_ref = refs.pop() if emit_bf16 else None
        if has_res:
            x_ref, d_ref, g_ref, r_ref, dx_ref, dg_ref = refs
        else:
            x_ref, d_ref, g_ref, dx_ref, dg_ref = refs
        i = pl.program_id(0)
        xv = x_ref[...]
        dv = d_ref[...].astype(F32)
        r = lax.rsqrt(jnp.mean(xv * xv, axis=-1, keepdims=True) + EPS)
        gd = dv * g_ref[...]
        proj = jnp.mean(xv * gd, axis=-1, keepdims=True)
        dx = r * gd - xv * (r * r * r) * proj
        if has_res:
            dx = dx + r_ref[...]
        dx_ref[...] = dx
        if emit_bf16:
            dx16_ref[...] = dx.astype(BF16)
        part = jnp.sum(dv * xv * r, axis=0, keepdims=True)

        @pl.when(i == 0)
        def _():
            dg_ref[...] = part

        @pl.when(i > 0)
        def _():
            dg_ref[...] += part

    row = pl.BlockSpec((ts, D), lambda i: (i, 0))
    vec = pl.BlockSpec((1, D), lambda i: (0, 0))
    in_specs = [row, row, vec] + ([row] if has_res else [])
    args = [x, dxn, g] + ([res] if has_res else [])
    extra = emit_bf16 * [jax.ShapeDtypeStruct((S, D), BF16)]
    return pl.pallas_call(
        body, name=name, grid=(S // ts,), in_specs=in_specs, out_specs=[row, vec] + emit_bf16 * [row],
        out_shape=[jax.ShapeDtypeStruct((S, D), F32), jax.ShapeDtypeStruct((1, D), F32)] + extra,
        compiler_params=_cparams(("arbitrary",)),
    )(*args)


def _final_loss(x, target, g, *, name, ts=256):
    S, D = x.shape
    ts = min(ts, S)

    def body(x_ref, t_ref, g_ref, dx_ref, dg_ref, loss_ref, dx16_ref):
        i = pl.program_id(0)
        xv = x_ref[...]
        gv = g_ref[...]
        r = lax.rsqrt(jnp.mean(xv * xv, axis=-1, keepdims=True) + EPS)
        y = xv * r * gv
        err = y - t_ref[...]
        row_loss = jnp.mean(err * err, axis=-1, keepdims=True)
        lpart = 0.5 * jnp.sum(row_loss, axis=0, keepdims=True)
        dy = err * (1.0 / D)
        gd = dy * gv
        proj = jnp.mean(xv * gd, axis=-1, keepdims=True)
        dx = r * gd - xv * (r * r * r) * proj
        dx_ref[...] = dx
        dx16_ref[...] = dx.astype(BF16)
        part = jnp.sum(dy * xv * r, axis=0, keepdims=True)
        lfull = jnp.broadcast_to(lpart, loss_ref.shape)

        @pl.when(i == 0)
        def _():
            dg_ref[...] = part
            loss_ref[...] = lfull

        @pl.when(i > 0)
        def _():
            dg_ref[...] += part
            loss_ref[...] += lfull

    row = pl.BlockSpec((ts, D), lambda i: (i, 0))
    vec = pl.BlockSpec((1, D), lambda i: (0, 0))
    return pl.pallas_call(
        body, name=name, grid=(S // ts,), in_specs=[row, row, vec],
        out_specs=[row, vec, pl.BlockSpec((SUB, LANE), lambda i: (0, 0)), row],
        out_shape=[jax.ShapeDtypeStruct((S, D), F32), jax.ShapeDtypeStruct((1, D), F32),
                   jax.ShapeDtypeStruct((SUB, LANE), F32), jax.ShapeDtypeStruct((S, D), BF16)],
        compiler_params=_cparams(("arbitrary",)),
    )(x, target, g)


def _rope_table(pos_col, inv, *, name, ts=1024):
    S = pos_col.shape[0]
    ts = min(ts, S)
    half = inv.shape[1]

    def body(p_ref, inv_ref, c_ref, s_ref):
        ang = p_ref[...].astype(F32) * inv_ref[...]
        c_ref[...] = jnp.cos(ang)
        s_ref[...] = jnp.sin(ang)

    tab = pl.BlockSpec((ts, half), lambda i: (i, 0))
    return pl.pallas_call(
        body, name=name, grid=(S // ts,),
        in_specs=[pl.BlockSpec((ts, 1), lambda i: (i, 0)), pl.BlockSpec((1, half), lambda i: (0, 0))],
        out_specs=[tab, tab],
        out_shape=[jax.ShapeDtypeStruct((S, half), F32), jax.ShapeDtypeStruct((S, half), F32)],
        compiler_params=_cparams(("parallel",)),
    )(pos_col, inv)


def _ret_consts(C, log_g):
    ii = lax.broadcasted_iota(jnp.int32, (C, C), 0)
    jj = lax.broadcasted_iota(jnp.int32, (C, C), 1)
    diff = (ii - jj).astype(F32)
    intra = jnp.where(ii >= jj, jnp.exp(log_g * jnp.maximum(diff, 0.0)), 0.0)
    idx = lax.broadcasted_iota(jnp.int32, (C, 1), 0).astype(F32)
    qd = jnp.exp(log_g * (idx + 1.0))
    kd = jnp.exp(log_g * (C - 1.0 - idx))
    cd = math.exp(log_g * C)
    return intra, qd, kd, cd


def _rot(t, cs, sn):
    half = t.shape[-1] // 2
    t1, t2 = t[:, :half], t[:, half:]
    return jnp.concatenate([t1 * cs - t2 * sn, t1 * sn + t2 * cs], axis=-1)


def _unrot(d, cs, sn):
    half = d.shape[-1] // 2
    d1, d2 = d[:, :half], d[:, half:]
    return jnp.concatenate([d1 * cs + d2 * sn, d2 * cs - d1 * sn], axis=-1)


def _ret_fwd(h, cos, sin, ret_g, mix, *, name, ch=2):
    S = h.shape[0]
    R = ret_g.shape[1]
    H, C = RET_HEADS, RET_CHUNK
    Dh = R // H
    ts = ch * C
    assert S % ts == 0
    log_gs = [math.log(1.0 - 2.0 ** (-5.0 - hd)) for hd in range(H)]
    scale = Dh ** -0.5

    def body(x_ref, c_ref, s_ref, rg_ref, mix_in, ret_ref, st_ref, mix_ref, state):
        i = pl.program_id(0)

        @pl.when(i == 0)
        def _():
            state[...] = jnp.zeros_like(state)

        for c in range(ch):
            rows = pl.ds(c * C, C)
            cs, sn = c_ref[rows, :], s_ref[rows, :]
            for hd in range(H):
                intra, qd, kd, cd = _ret_consts(C, log_gs[hd])
                q = x_ref[rows, pl.ds(hd * Dh, Dh)]
                k = x_ref[rows, pl.ds(R + hd * Dh, Dh)]
                v = x_ref[rows, pl.ds(2 * R + hd * Dh, Dh)]
                g = x_ref[rows, pl.ds(3 * R + hd * Dh, Dh)]
                rq = _rot(q, cs, sn)
                rk = _rot(k, cs, sn) * scale
                st = state[hd]
                st_ref[c, hd] = st.astype(BF16)
                s_ = _bdot(rq, rk, NT) * intra
                ret = _bdot(s_, v, NN) + _bdot(rq * qd, st, NN)
                state[hd] = st * cd + _bdot(rk * kd, v, TN)
                ret_ref[rows, pl.ds(hd * Dh, Dh)] = ret
                rr = lax.rsqrt(jnp.mean(ret * ret, axis=-1, keepdims=True) + EPS)
                out = ret * rr * rg_ref[:, pl.ds(hd * Dh, Dh)] * (g * _sigmoid(g))
                mix_ref[rows, pl.ds(hd * Dh, Dh)] = out.astype(BF16)

    n_chunks = S // C
    return pl.pallas_call(
        body, name=name, grid=(S // ts,),
        in_specs=[pl.BlockSpec((ts, 4 * R), lambda i: (i, 0)),
                  pl.BlockSpec((ts, Dh // 2), lambda i: (i, 0)), pl.BlockSpec((ts, Dh // 2), lambda i: (i, 0)),
                  pl.BlockSpec((1, R), lambda i: (0, 0)), pl.BlockSpec(memory_space=pl.ANY)],
        out_specs=[pl.BlockSpec((ts, R), lambda i: (i, 0)),
                   pl.BlockSpec((ch, H, Dh, Dh), lambda i: (i, 0, 0, 0)),
                   pl.BlockSpec((ts, R), lambda i: (i, 0))],
        out_shape=[jax.ShapeDtypeStruct((S, R), F32), jax.ShapeDtypeStruct((n_chunks, H, Dh, Dh), BF16),
                   jax.ShapeDtypeStruct(mix.shape, mix.dtype)],
        scratch_shapes=[pltpu.VMEM((H, Dh, Dh), F32)],
        input_output_aliases={4: 2},
        compiler_params=_cparams(("arbitrary",)),
    )(h, cos, sin, ret_g, mix)


def _ret_bwd(h, cos, sin, ret_g, states, ret_raw, dmix, *, name, ch=2):
    S = h.shape[0]
    R = ret_g.shape[1]
    H, C = RET_HEADS, RET_CHUNK
    Dh = R // H
    ts = ch * C
    nb = S // ts
    log_gs = [math.log(1.0 - 2.0 ** (-5.0 - hd)) for hd in range(H)]
    scale = Dh ** -0.5

    def body(x_ref, c_ref, s_ref, rg_ref, st_ref, ret_ref, dm_ref, dh_ref, drg_ref, dstate):
        i = pl.program_id(0)

        @pl.when(i == 0)
        def _():
            dstate[...] = jnp.zeros_like(dstate)
            drg_ref[...] = jnp.zeros_like(drg_ref)

        for c in reversed(range(ch)):
            rows = pl.ds(c * C, C)
            cs, sn = c_ref[rows, :], s_ref[rows, :]
            for hd in range(H):
                intra, qd, kd, cd = _ret_consts(C, log_gs[hd])
                cols = pl.ds(hd * Dh, Dh)
                q = x_ref[rows, pl.ds(hd * Dh, Dh)]
                k = x_ref[rows, pl.ds(R + hd * Dh, Dh)]
                v = x_ref[rows, pl.ds(2 * R + hd * Dh, Dh)]
                g = x_ref[rows, pl.ds(3 * R + hd * Dh, Dh)]
                rq = _rot(q, cs, sn)
                rk = _rot(k, cs, sn) * scale
                ret = ret_ref[rows, cols]
                dm = dm_ref[rows, cols]
                rgv = rg_ref[:, cols]
                rr = lax.rsqrt(jnp.mean(ret * ret, axis=-1, keepdims=True) + EPS)
                retn = ret * rr
                sg = _sigmoid(g)
                silu = g * sg
                drg_ref[:, cols] += jnp.sum(dm * retn * silu, axis=0, keepdims=True)
                dg = dm * retn * rgv * (sg * (1.0 + g * (1.0 - sg)))
                dretn = dm * rgv * silu
                d_o = rr * dretn - ret * (rr * rr * rr) * jnp.mean(ret * dretn, axis=-1, keepdims=True)
                st = st_ref[c, hd]
                d_s = dstate[hd]
                a_ = _bdot(rq, rk, NT) * intra
                d_a = _bdot(d_o, v, NT) * intra
                d_qr = _bdot(d_a, rk, NN) + _bdot(d_o, st, NT) * qd
                d_kr = _bdot(d_a, rq, TN) + _bdot(v, d_s, NT) * kd
                d_v = _bdot(a_, d_o, TN) + _bdot(rk * kd, d_s, NN)
                dstate[hd] = d_s * cd + _bdot(rq * qd, d_o, TN)
                dh_ref[rows, pl.ds(hd * Dh, Dh)] = _unrot(d_qr, cs, sn).astype(BF16)
                dh_ref[rows, pl.ds(R + hd * Dh, Dh)] = (_unrot(d_kr, cs, sn) * scale).astype(BF16)
                dh_ref[rows, pl.ds(2 * R + hd * Dh, Dh)] = d_v.astype(BF16)
                dh_ref[rows, pl.ds(3 * R + hd * Dh, Dh)] = dg.astype(BF16)

    rb = lambda i: nb - 1 - i
    return pl.pallas_call(
        body, name=name, grid=(nb,),
        in_specs=[pl.BlockSpec((ts, 4 * R), lambda i: (rb(i), 0)),
                  pl.BlockSpec((ts, Dh // 2), lambda i: (rb(i), 0)), pl.BlockSpec((ts, Dh // 2), lambda i: (rb(i), 0)),
                  pl.BlockSpec((1, R), lambda i: (0, 0)),
                  pl.BlockSpec((ch, H, Dh, Dh), lambda i: (rb(i), 0, 0, 0)),
                  pl.BlockSpec((ts, R), lambda i: (rb(i), 0)),
                  pl.BlockSpec((ts, R), lambda i: (rb(i), 0))],
        out_specs=[pl.BlockSpec((ts, 4 * R), lambda i: (rb(i), 0)), pl.BlockSpec((1, R), lambda i: (0, 0))],
        out_shape=[jax.ShapeDtypeStruct((S, 6 * R), BF16), jax.ShapeDtypeStruct((1, R), F32)],
        scratch_shapes=[pltpu.VMEM((H, Dh, Dh), F32)],
        compiler_params=_cparams(("arbitrary",)),
    )(h, cos, sin, ret_g, states, ret_raw, dmix)


GELU_C = math.sqrt(2.0 / math.pi)
GELU_A = 0.044715


def _gelu_parts(y):
    t = jnp.tanh(GELU_C * (y + GELU_A * y * y * y))
    val = 0.5 * y * (1.0 + t)
    grad = 0.5 * (1.0 + t) + 0.5 * y * (1.0 - t * t) * GELU_C * (1.0 + 3.0 * GELU_A * y * y)
    return val, grad


def _neg_expm1(x):
    series = -x * (1.0 + x * (1.0 / 2.0) * (1.0 + x * (1.0 / 3.0) * (1.0 + x * (1.0 / 4.0) * (
        1.0 + x * (1.0 / 5.0) * (1.0 + x * (1.0 / 6.0) * (1.0 + x * (1.0 / 7.0)))))))
    return jnp.where(x > -0.35, series, 1.0 - jnp.exp(x))


def _log_sigmoid(x):
    return jnp.minimum(x, 0.0) - jnp.log1p(jnp.exp(-jnp.abs(x)))


def _lru_gates(uc, wa_ref, ba_ref, wx_ref, bx_ref):
    nbk = wa_ref.shape[0]
    bd = wa_ref.shape[1]
    rs, gs = [], []
    for n in range(nbk):
        ucn = uc[:, n * bd:(n + 1) * bd]
        rs.append(_sigmoid(_bdot(ucn, wa_ref[n], NN) + ba_ref[:, pl.ds(n * bd, bd)]))
        gs.append(_sigmoid(_bdot(ucn, wx_ref[n], NN) + bx_ref[:, pl.ds(n * bd, bd)]))
    return jnp.concatenate(rs, axis=-1), jnp.concatenate(gs, axis=-1)


def _lru_fwd(h, conv_w, conv_b, wa, ba, wx, bx, lam, *, name, ts=256):
    S = h.shape[0]
    W = lam.shape[1]
    K = conv_w.shape[0]
    ts = min(ts, S)

    def body(u_ref, y_ref, cw_ref, cb_ref, wa_ref, ba_ref, wx_ref, bx_ref, lam_ref, hl_ref, mix_ref, tail, hlast):
        i = pl.program_id(0)

        @pl.when(i == 0)
        def _():
            tail[...] = jnp.zeros_like(tail)
            hlast[...] = jnp.zeros_like(hlast)

        u = u_ref[...]
        tl = tail[...]
        uc = cb_ref[...] + cw_ref[K - 1:K, :] * u
        for k in range(K - 1):
            uc = uc + cw_ref[k:k + 1, :] * _shift_down(u, tl, K - 1 - k)
        tail[...] = u[ts - SUB:ts]
        r, ig = _lru_gates(uc, wa_ref, ba_ref, wx_ref, bx_ref)
        log_a = LRU_C * r * _log_sigmoid(lam_ref[...])
        a = jnp.exp(log_a)
        b = jnp.sqrt(_neg_expm1(2.0 * log_a)) * (ig * uc)
        rid = _row_iota((ts, W))
        d = 1
        while d < ts:
            a_s = jnp.where(rid < d, 1.0, pltpu.roll(a, d, 0))
            b_s = jnp.where(rid < d, 0.0, pltpu.roll(b, d, 0))
            b = a * b_s + b
            a = a * a_s
            d *= 2
        hcur = a * hlast[SUB - 1:SUB, :] + b
        hlast[...] = hcur[ts - SUB:ts]
        hl_ref[...] = hcur
        gy, _ = _gelu_parts(y_ref[...])
        mix_ref[...] = (hcur * gy).astype(BF16)

    full = lambda shape: pl.BlockSpec(shape, lambda i: tuple(0 for _ in shape))
    return pl.pallas_call(
        body, name=name, grid=(S // ts,),
        in_specs=[pl.BlockSpec((ts, W), lambda i: (i, 4)), pl.BlockSpec((ts, W), lambda i: (i, 5)),
                  full(conv_w.shape), full(conv_b.shape), full(wa.shape), full(ba.shape), full(wx.shape),
                  full(bx.shape), full(lam.shape)],
        out_specs=[pl.BlockSpec((ts, W), lambda i: (i, 0)), pl.BlockSpec((ts, W), lambda i: (i, 1))],
        out_shape=[jax.ShapeDtypeStruct((S, W), F32), jax.ShapeDtypeStruct((S, 2 * W), BF16)],
        scratch_shapes=[pltpu.VMEM((SUB, W), F32), pltpu.VMEM((SUB, W), F32)],
        compiler_params=_cparams(("arbitrary",)),
    )(h, h, conv_w, conv_b, wa, ba, wx, bx, lam)


def _lru_bwd(h, hl, dmix, dh, conv_w, conv_b, wa, ba, wx, bx, lam, *, name, ts=256):
    S = h.shape[0]
    W = lam.shape[1]
    K = conv_w.shape[0]
    nbk, bd = wa.shape[0], wa.shape[1]
    ts = min(ts, S)
    nb = S // ts
    t8 = ts // SUB

    def body(u_ref, y_ref, uh_ref, hl_ref, hh_ref, dm_ref, cw_ref, cb_ref, wa_ref, ba_ref, wx_ref, bx_ref, lam_ref,
             dh_in, dh_ref, dcw_ref, dcb_ref, dwa_ref, dba_ref, dwx_ref, dbx_ref, dlam_ref, carry, head):
        i = pl.program_id(0)
        blk = nb - 1 - i

        @pl.when(i == 0)
        def _():
            carry[...] = jnp.zeros_like(carry)
            head[...] = jnp.zeros_like(head)
            for ref in (dcw_ref, dcb_ref, dwa_ref, dba_ref, dwx_ref, dbx_ref, dlam_ref):
                ref[...] = jnp.zeros_like(ref)

        inside = (blk > 0).astype(F32)
        u = u_ref[...]
        tl = uh_ref[...] * inside
        sh = [_shift_down(u, tl, K - 1 - k) for k in range(K)]
        uc = cb_ref[...]
        for k in range(K):
            uc = uc + cw_ref[k:k + 1, :] * sh[k]
        r, ig = _lru_gates(uc, wa_ref, ba_ref, wx_ref, bx_ref)
        lam_v = lam_ref[...]
        ls = _log_sigmoid(lam_v)
        log_a = LRU_C * r * ls
        a = jnp.exp(log_a)
        mult = jnp.sqrt(_neg_expm1(2.0 * log_a))
        hcur = hl_ref[...]
        hprev = _shift_down(hcur, hh_ref[...] * inside, 1)
        gy, dgy = _gelu_parts(y_ref[...])
        dm = dm_ref[...]
        d_y = dm * hcur * dgy
        rid = _row_iota((ts, W))
        bq = dm * gy + jnp.where(rid == ts - 1, carry[0:1, :], 0.0)
        aq = jnp.where(rid == ts - 1, 0.0, pltpu.roll(a, ts - 1, 0))
        d = 1
        while d < ts:
            a_s = jnp.where(rid >= ts - d, 0.0, pltpu.roll(aq, ts - d, 0))
            b_s = jnp.where(rid >= ts - d, 0.0, pltpu.roll(bq, ts - d, 0))
            bq = bq + aq * b_s
            aq = aq * a_s
            d *= 2
        lam_t = bq
        carry[...] = (a * lam_t)[0:SUB]
        d_a = lam_t * hprev
        d_mult = lam_t * (ig * uc)
        d_i = lam_t * mult * uc
        d_uc = lam_t * mult * ig
        d_log_a = d_a * a - d_mult * (a * a) / mult
        d_r = d_log_a * (LRU_C * ls)
        dlam_ref[...] += jnp.sum(d_log_a * (LRU_C * r), axis=0, keepdims=True) * _sigmoid(-lam_v)
        d_pr = d_r * r * (1.0 - r)
        d_pi = d_i * ig * (1.0 - ig)
        dba_ref[...] += jnp.sum(d_pr, axis=0, keepdims=True)
        dbx_ref[...] += jnp.sum(d_pi, axis=0, keepdims=True)
        extra = []
        for n in range(nbk):
            sl = slice(n * bd, (n + 1) * bd)
            ucn = uc[:, sl]
            dwa_ref[n] += _bdot(ucn, d_pr[:, sl], TN)
            dwx_ref[n] += _bdot(ucn, d_pi[:, sl], TN)
            extra.append(_bdot(d_pr[:, sl], wa_ref[n], NT) + _bdot(d_pi[:, sl], wx_ref[n], NT))
        d_uc = d_uc + jnp.concatenate(extra, axis=-1)
        dcb_ref[...] += jnp.sum(d_uc, axis=0, keepdims=True)
        rid8 = _row_iota((SUB, W))
        dcw = jnp.zeros((SUB, W), F32)
        for k in range(K):
            dcw = dcw + jnp.where(rid8 == k, jnp.sum(d_uc * sh[k], axis=0, keepdims=True), 0.0)
        dcw_ref[...] += dcw
        hd = head[...]
        d_u = cw_ref[K - 1:K, :] * d_uc
        for j in range(1, K):
            d_u = d_u + cw_ref[K - 1 - j:K - j, :] * _shift_up(d_uc, hd, j)
        head[...] = d_uc[0:SUB]
        dh_ref[:, 0:W] = d_u.astype(BF16)
        dh_ref[:, W:2 * W] = d_y.astype(BF16)

    rb = lambda i: nb - 1 - i
    prev8 = lambda i: jnp.maximum(rb(i) * t8 - 1, 0)
    full = lambda shape: pl.BlockSpec(shape, lambda i: tuple(0 for _ in shape))
    small = [jax.ShapeDtypeStruct((SUB, W), F32), jax.ShapeDtypeStruct((1, W), F32),
             jax.ShapeDtypeStruct(wa.shape, F32), jax.ShapeDtypeStruct((1, W), F32),
             jax.ShapeDtypeStruct(wx.shape, F32), jax.ShapeDtypeStruct((1, W), F32),
             jax.ShapeDtypeStruct((1, W), F32)]
    return pl.pallas_call(
        body, name=name, grid=(nb,),
        in_specs=[pl.BlockSpec((ts, W), lambda i: (rb(i), 4)), pl.BlockSpec((ts, W), lambda i: (rb(i), 5)),
                  pl.BlockSpec((SUB, W), lambda i: (prev8(i), 4)),
                  pl.BlockSpec((ts, W), lambda i: (rb(i), 0)), pl.BlockSpec((SUB, W), lambda i: (prev8(i), 0)),
                  pl.BlockSpec((ts, W), lambda i: (rb(i), 1)),
                  full(conv_w.shape), full(conv_b.shape), full(wa.shape), full(ba.shape), full(wx.shape),
                  full(bx.shape), full(lam.shape), pl.BlockSpec(memory_space=pl.ANY)],
        out_specs=[pl.BlockSpec((ts, 2 * W), lambda i: (rb(i), 2))] + [full(s.shape) for s in small],
        out_shape=[jax.ShapeDtypeStruct(dh.shape, dh.dtype)] + small,
        scratch_shapes=[pltpu.VMEM((SUB, W), F32), pltpu.VMEM((SUB, W), F32)],
        input_output_aliases={13: 0},
        compiler_params=_cparams(("arbitrary",)),
    )(h, h, h, hl, hl, dmix, conv_w, conv_b, wa, ba, wx, bx, lam, dh)


def _xattn_fwd(q, km, vm, *, name, ts=512):
    S, D = q.shape
    M = km.shape[0]
    H = XA_HEADS
    Dh = D // H
    ts = min(ts, S)
    scale = Dh ** -0.5

    def body(q_ref, k_ref, v_ref, o_ref):
        for hd in range(H):
            cols = pl.ds(hd * Dh, Dh)
            s = _bdot(q_ref[:, cols], k_ref[:, cols], NT) * scale
            s = s - jnp.max(s, axis=-1, keepdims=True)
            e = jnp.exp(s)
            p = e / jnp.sum(e, axis=-1, keepdims=True)
            o_ref[:, cols] = _bdot(p, v_ref[:, cols], NN).astype(o_ref.dtype)

    return pl.pallas_call(
        body, name=name, grid=(S // ts,),
        in_specs=[pl.BlockSpec((ts, D), lambda i: (i, 0)), pl.BlockSpec((M, D), lambda i: (0, 0)),
                  pl.BlockSpec((M, D), lambda i: (0, 0))],
        out_specs=pl.BlockSpec((ts, D), lambda i: (i, 0)),
        out_shape=jax.ShapeDtypeStruct((S, D), BF16),
        compiler_params=_cparams(("parallel",)),
    )(q, km, vm)


def _xattn_bwd(q, km, vm, d_o, *, name, ts=512):
    S, D = q.shape
    M = km.shape[0]
    H = XA_HEADS
    Dh = D // H
    ts = min(ts, S)
    scale = Dh ** -0.5

    def body(q_ref, k_ref, v_ref, do_ref, dq_ref, dk_ref, dv_ref):
        i = pl.program_id(0)

        @pl.when(i == 0)
        def _():
            dk_ref[...] = jnp.zeros_like(dk_ref)
            dv_ref[...] = jnp.zeros_like(dv_ref)

        for hd in range(H):
            cols = pl.ds(hd * Dh, Dh)
            qh, kh, vh, doh = q_ref[:, cols], k_ref[:, cols], v_ref[:, cols], do_ref[:, cols]
            s = _bdot(qh, kh, NT) * scale
            s = s - jnp.max(s, axis=-1, keepdims=True)
            e = jnp.exp(s)
            p = e / jnp.sum(e, axis=-1, keepdims=True)
            dp = _bdot(doh, vh, NT)
            ds = p * (dp - jnp.sum(dp * p, axis=-1, keepdims=True)) * scale
            dq_ref[:, cols] = _bdot(ds, kh, NN).astype(dq_ref.dtype)
            dk_ref[:, cols] += _bdot(ds, qh, TN)
            dv_ref[:, cols] += _bdot(p, doh, TN)

    row = pl.BlockSpec((ts, D), lambda i: (i, 0))
    mem = pl.BlockSpec((M, D), lambda i: (0, 0))
    return pl.pallas_call(
        body, name=name, grid=(S // ts,), in_specs=[row, mem, mem, row], out_specs=[row, mem, mem],
        out_shape=[jax.ShapeDtypeStruct((S, D), BF16), jax.ShapeDtypeStruct((M, D), F32),
                   jax.ShapeDtypeStruct((M, D), F32)],
        compiler_params=_cparams(("arbitrary",)),
    )(q, km, vm, d_o)


def _conv_rows(v, tail, cw_ref, cb_ref):
    K = cw_ref.shape[0]
    sh = [_shift_down(v, tail, K - 1 - k) for k in range(K)]
    out = cb_ref[...]
    for k in range(K):
        out = out + cw_ref[k:k + 1, :] * sh[k]
    return out, sh


def _ffn_gate_fwd(hh, cw, cb, *, name, ts=512, tc=512):
    S, F2 = hh.shape
    F = F2 // 2
    ts, tc = min(ts, S), min(tc, F)
    nj = F // tc
    K = cw.shape[0]

    def body(a_ref, b_ref, cwa_ref, cwb_ref, cba_ref, cbb_ref, o_ref, ta, tb):
        i = pl.program_id(1)

        @pl.when(i == 0)
        def _():
            ta[...] = jnp.zeros_like(ta)
            tb[...] = jnp.zeros_like(tb)

        av, bv = a_ref[...], b_ref[...]
        ac, _ = _conv_rows(av, ta[...], cwa_ref, cba_ref)
        bc, _ = _conv_rows(bv, tb[...], cwb_ref, cbb_ref)
        ta[...] = av[ts - SUB:ts]
        tb[...] = bv[ts - SUB:ts]
        o_ref[...] = (ac * _sigmoid(ac) * bc).astype(o_ref.dtype)

    return pl.pallas_call(
        body, name=name, grid=(nj, S // ts),
        in_specs=[pl.BlockSpec((ts, tc), lambda j, i: (i, j)), pl.BlockSpec((ts, tc), lambda j, i: (i, j + nj)),
                  pl.BlockSpec((K, tc), lambda j, i: (0, j)), pl.BlockSpec((K, tc), lambda j, i: (0, j + nj)),
                  pl.BlockSpec((1, tc), lambda j, i: (0, j)), pl.BlockSpec((1, tc), lambda j, i: (0, j + nj))],
        out_specs=pl.BlockSpec((ts, tc), lambda j, i: (i, j)),
        out_shape=jax.ShapeDtypeStruct((S, F), BF16),
        scratch_shapes=[pltpu.VMEM((SUB, tc), F32), pltpu.VMEM((SUB, tc), F32)],
        compiler_params=_cparams(("parallel", "arbitrary")),
    )(hh, hh, cw, cw, cb, cb)


def _ffn_gate_bwd(hh, dact, cw, cb, *, name, ts=512, tc=512):
    S, F2 = hh.shape
    F = F2 // 2
    ts, tc = min(ts, S), min(tc, F)
    nj = F // tc
    nb = S // ts
    t8 = ts // SUB
    K = cw.shape[0]

    def body(a_ref, ah_ref, b_ref, bh_ref, d_ref, cwa_ref, cwb_ref, cba_ref, cbb_ref,
             da_ref, db_ref, ga_ref, gb_ref, ha, hb):
        i = pl.program_id(1)
        blk = nb - 1 - i

        @pl.when(i == 0)
        def _():
            for ref in (ha, hb, ga_ref, gb_ref):
                ref[...] = jnp.zeros_like(ref)

        inside = (blk > 0).astype(F32)
        ac, sha = _conv_rows(a_ref[...], ah_ref[...] * inside, cwa_ref, cba_ref)
        bc, shb = _conv_rows(b_ref[...], bh_ref[...] * inside, cwb_ref, cbb_ref)
        dv = d_ref[...].astype(F32)
        sg = _sigmoid(ac)
        d_bc = dv * ac * sg
        d_ac = dv * bc * sg * (1.0 + ac * (1.0 - sg))
        rid8 = _row_iota((SUB, tc))
        for d_c, sh, cw_ref, head, o_ref, g_ref in ((d_ac, sha, cwa_ref, ha, da_ref, ga_ref),
                                                     (d_bc, shb, cwb_ref, hb, db_ref, gb_ref)):
            hd = head[...]
            d_in = cw_ref[K - 1:K, :] * d_c
            for j in range(1, K):
                d_in = d_in + cw_ref[K - 1 - j:K - j, :] * _shift_up(d_c, hd, j)
            head[...] = d_c[0:SUB]
            o_ref[...] = d_in.astype(o_ref.dtype)
            gsum = jnp.where(rid8 == K, jnp.sum(d_c, axis=0, keepdims=True), 0.0)
            for k in range(K):
                gsum = gsum + jnp.where(rid8 == k, jnp.sum(d_c * sh[k], axis=0, keepdims=True), 0.0)
            g_ref[...] += gsum

    rb = lambda i: nb - 1 - i
    prev8 = lambda i: jnp.maximum(rb(i) * t8 - 1, 0)
    outs = pl.pallas_call(
        body, name=name, grid=(nj, nb),
        in_specs=[pl.BlockSpec((ts, tc), lambda j, i: (rb(i), j)), pl.BlockSpec((SUB, tc), lambda j, i: (prev8(i), j)),
                  pl.BlockSpec((ts, tc), lambda j, i: (rb(i), j + nj)),
                  pl.BlockSpec((SUB, tc), lambda j, i: (prev8(i), j + nj)),
                  pl.BlockSpec((ts, tc), lambda j, i: (rb(i), j)),
                  pl.BlockSpec((K, tc), lambda j, i: (0, j)), pl.BlockSpec((K, tc), lambda j, i: (0, j + nj)),
                  pl.BlockSpec((1, tc), lambda j, i: (0, j)), pl.BlockSpec((1, tc), lambda j, i: (0, j + nj))],
        out_specs=[pl.BlockSpec((ts, tc), lambda j, i: (rb(i), j)), pl.BlockSpec((ts, tc), lambda j, i: (rb(i), j)),
                   pl.BlockSpec((SUB, tc), lambda j, i: (0, j)), pl.BlockSpec((SUB, tc), lambda j, i: (0, j))],
        out_shape=[jax.ShapeDtypeStruct((S, F), BF16), jax.ShapeDtypeStruct((S, F), BF16),
                   jax.ShapeDtypeStruct((SUB, F), F32), jax.ShapeDtypeStruct((SUB, F), F32)],
        scratch_shapes=[pltpu.VMEM((SUB, tc), F32), pltpu.VMEM((SUB, tc), F32)],
        compiler_params=_cparams(("parallel", "arbitrary")),
    )(hh, hh, hh, hh, dact, cw, cw, cb, cb)
    return outs


ADAM_BLOCK_ELEMS = 128 * 1024


def _adamw(w, m, v, parts, *, name):
    R, C = w.shape
    n = parts.shape[0]
    tr = R
    for cand in (1024, 512, 256, 128, 64, 32, 16):
        if R % cand == 0 and cand * C <= ADAM_BLOCK_ELEMS:
            tr = cand
            break
    c1 = 1.0 - ADAM_B1 ** ADAM_STEP
    c2 = 1.0 - ADAM_B2 ** ADAM_STEP

    def body(w_ref, m_ref, v_ref, p_ref, g_ref, d_ref, nm_ref, nv_ref):
        g = p_ref[0].astype(F32)
        for k in range(1, n):
            g = g + p_ref[k].astype(F32)
        m_new = ADAM_B1 * m_ref[...] + (1.0 - ADAM_B1) * g
        v_new = ADAM_B2 * v_ref[...] + (1.0 - ADAM_B2) * (g * g)
        m_hat = m_new / c1
        v_hat = v_new / c2
        g_ref[...] = g
        d_ref[...] = -ADAM_LR * (m_hat / (jnp.sqrt(v_hat) + ADAM_EPS) + ADAM_WD * w_ref[...])
        nm_ref[...] = m_new
        nv_ref[...] = v_new

    blk = pl.BlockSpec((tr, C), lambda i: (i, 0))
    sds = jax.ShapeDtypeStruct((R, C), F32)
    return pl.pallas_call(
        body, name=name, grid=(R // tr,),
        in_specs=[blk, blk, blk, pl.BlockSpec((n, tr, C), lambda i: (0, i, 0))],
        out_specs=[blk, blk, blk, blk], out_shape=[sds, sds, sds, sds],
        compiler_params=_cparams(("parallel",)),
    )(w, m, v, parts)


def _mesh_place():
    x, y, c = lax.axis_index("x"), lax.axis_index("y"), lax.axis_index("c")
    others = [(1 - x, y), (x, 1 - y), (1 - x, 1 - y)]
    return x, y, c, others


def _gather_weights(shards, axes, splits, *, name):
    n = len(shards)
    fulls = [jax.ShapeDtypeStruct((s.shape[0] * N_CHIP, s.shape[1]) if ax == 0 else (s.shape[0], s.shape[1] * N_CHIP),
                                  s.dtype) for s, ax in zip(shards, axes)]

    def body(*refs):
        srcs, outs = refs[:n], refs[n:2 * n]
        send, recv, fsend, frecv, lsem = refs[2 * n:]
        x, y, c, others = _mesh_place()
        my_chip = 2 * x + y
        sibling = (x, y, 1 - c)

        def region(it, chip, half):
            r, w = srcs[it].shape
            if splits[it]:
                rows = pl.ds(pl.multiple_of(half * (r // 2) + (chip * r if axes[it] == 0 else 0), 16), r // 2)
            else:
                rows = pl.ds(chip * r if axes[it] == 0 else 0, r)
            cols = pl.ds(pl.multiple_of(chip * w, LANE), w) if axes[it] == 1 else pl.ds(0, w)
            return outs[it].at[rows, cols]

        def src_half(it, half):
            r = srcs[it].shape[0]
            if splits[it]:
                return srcs[it].at[pl.ds(pl.multiple_of(half * (r // 2), 16), r // 2), :]
            return srcs[it]

        def whole(it, chip):
            r, w = srcs[it].shape
            rows = pl.ds(chip * r if axes[it] == 0 else 0, r)
            cols = pl.ds(pl.multiple_of(chip * w, LANE), w) if axes[it] == 1 else pl.ds(0, w)
            return outs[it].at[rows, cols]

        def ici(it, j, chip_from):
            return pltpu.make_async_remote_copy(
                src_ref=src_half(it, c), dst_ref=region(it, chip_from, c), send_sem=send.at[3 * it + j],
                recv_sem=recv.at[3 * it + j], device_id=(*others[j], c), device_id_type=MESH)

        def fwd(it, j, half):
            chip = 2 * others[j][0] + others[j][1]
            return pltpu.make_async_remote_copy(
                src_ref=region(it, chip, half), dst_ref=region(it, chip, half), send_sem=fsend.at[3 * it + j],
                recv_sem=frecv.at[3 * it + j], device_id=sibling, device_id_type=MESH)

        local = [pltpu.make_async_copy(srcs[it], whole(it, my_chip), lsem.at[it]) for it in range(n)]
        for cp in local:
            cp.start()
        for it in range(n):
            for j in range(3):
                ici(it, j, my_chip).start()
        for it in range(n):
            for j in range(3):
                ici(it, j, 2 * others[j][0] + others[j][1]).wait_recv()
                if splits[it]:
                    fwd(it, j, c).start()
        for it in range(n):
            for j in range(3):
                if splits[it]:
                    fwd(it, j, 1 - c).wait_recv()
        for it in range(n):
            for j in range(3):
                ici(it, j, my_chip).wait_send()
                if splits[it]:
                    fwd(it, j, c).wait_send()
        for cp in local:
            cp.wait()

    any_spec = pl.BlockSpec(memory_space=pl.ANY)
    return pl.pallas_call(
        body, name=name, in_specs=[any_spec] * n, out_specs=[any_spec] * n, out_shape=fulls,
        scratch_shapes=[pltpu.SemaphoreType.DMA((3 * n,)), pltpu.SemaphoreType.DMA((3 * n,)),
                        pltpu.SemaphoreType.DMA((3 * n,)), pltpu.SemaphoreType.DMA((3 * n,)),
                        pltpu.SemaphoreType.DMA((n,))],
    )(*shards)


def _gather_grads(grads, axes, *, name):
    n = len(grads)
    shard_shapes = [(g.shape[0] // N_CHIP, g.shape[1]) if ax == 0 else (g.shape[0], g.shape[1] // N_CHIP)
                    for g, ax in zip(grads, axes)]
    lands = [jax.ShapeDtypeStruct((N_DEV, *s), g.dtype) for s, g in zip(shard_shapes, grads)]

    def body(*refs):
        srcs, outs = refs[:n], refs[n:2 * n]
        send, recv, fsend, frecv, ssend, srecv, lsem = refs[2 * n:]
        x, y, c, others = _mesh_place()
        my_chip = 2 * x + y
        me = 2 * my_chip + c
        sibling = (x, y, 1 - c)

        def piece(it, chip):
            r, w = shard_shapes[it]
            if axes[it] == 0:
                return srcs[it].at[pl.ds(pl.multiple_of(chip * r, 16), r), :]
            return srcs[it].at[:, pl.ds(pl.multiple_of(chip * w, LANE), w)]

        def ici(it, j, slot):
            chip_to = 2 * others[j][0] + others[j][1]
            return pltpu.make_async_remote_copy(
                src_ref=piece(it, chip_to), dst_ref=outs[it].at[slot], send_sem=send.at[3 * it + j],
                recv_sem=recv.at[3 * it + j], device_id=(*others[j], c), device_id_type=MESH)

        def fwd(it, j, slot):
            return pltpu.make_async_remote_copy(
                src_ref=outs[it].at[slot], dst_ref=outs[it].at[slot], send_sem=fsend.at[3 * it + j],
                recv_sem=frecv.at[3 * it + j], device_id=sibling, device_id_type=MESH)

        def own(it, slot):
            return pltpu.make_async_remote_copy(
                src_ref=piece(it, my_chip), dst_ref=outs[it].at[slot], send_sem=ssend.at[it],
                recv_sem=srecv.at[it], device_id=sibling, device_id_type=MESH)

        local = [pltpu.make_async_copy(piece(it, my_chip), outs[it].at[me], lsem.at[it]) for it in range(n)]
        for it in range(n):
            local[it].start()
            own(it, me).start()
            for j in range(3):
                ici(it, j, me).start()
        for it in range(n):
            for j in range(3):
                slot = 2 * (2 * others[j][0] + others[j][1]) + c
                ici(it, j, slot).wait_recv()
                fwd(it, j, slot).start()
        for it in range(n):
            own(it, 2 * my_chip + 1 - c).wait_recv()
            for j in range(3):
                fwd(it, j, 2 * (2 * others[j][0] + others[j][1]) + 1 - c).wait_recv()
        for it in range(n):
            own(it, me).wait_send()
            for j in range(3):
                ici(it, j, me).wait_send()
                fwd(it, j, 2 * (2 * others[j][0] + others[j][1]) + c).wait_send()
            local[it].wait()

    any_spec = pl.BlockSpec(memory_space=pl.ANY)
    return pl.pallas_call(
        body, name=name, in_specs=[any_spec] * n, out_specs=[any_spec] * n, out_shape=lands,
        scratch_shapes=[pltpu.SemaphoreType.DMA((3 * n,)), pltpu.SemaphoreType.DMA((3 * n,)),
                        pltpu.SemaphoreType.DMA((3 * n,)), pltpu.SemaphoreType.DMA((3 * n,)),
                        pltpu.SemaphoreType.DMA((n,)), pltpu.SemaphoreType.DMA((n,)),
                        pltpu.SemaphoreType.DMA((n,))],
    )(*grads)


def _allreduce_small(vec, *, name):
    R, L = vec.shape

    def body(v_ref, o_ref, buf, send, recv, lsem):
        x, y, c, others = _mesh_place()
        me = 4 * x + 2 * y + c
        sibling = (x, y, 1 - c)

        def copy(k, slot, to, src=None):
            return pltpu.make_async_remote_copy(
                src_ref=buf.at[slot] if src is None else src, dst_ref=buf.at[slot], send_sem=send.at[k],
                recv_sem=recv.at[k], device_id=to, device_id_type=MESH)

        def slot_of(chip, core):
            return 4 * chip[0] + 2 * chip[1] + core

        mine = pltpu.make_async_copy(v_ref, buf.at[me], lsem)
        mine.start()
        first = [copy(0, me, sibling, src=v_ref)]
        first += [copy(1 + j, me, (*chip, c), src=v_ref) for j, chip in enumerate(others)]
        for cp in first:
            cp.start()
        passed = [copy(4 + j, slot_of(chip, c), sibling) for j, chip in enumerate(others)]
        for j, chip in enumerate(others):
            copy(1 + j, slot_of(chip, c), (*chip, c)).wait_recv()
            passed[j].start()
        copy(0, slot_of((x, y), 1 - c), sibling).wait_recv()
        for j, chip in enumerate(others):
            copy(4 + j, slot_of(chip, 1 - c), sibling).wait_recv()
        for cp in first + passed:
            cp.wait_send()
        mine.wait()
        total = buf[0]
        for k in range(1, N_DEV):
            total = total + buf[k]
        o_ref[...] = total

    return pl.pallas_call(
        body, name=name, in_specs=[pl.BlockSpec(memory_space=pltpu.VMEM)],
        out_specs=pl.BlockSpec(memory_space=pltpu.VMEM), out_shape=jax.ShapeDtypeStruct((R, L), F32),
        scratch_shapes=[pltpu.VMEM((N_DEV, R, L), F32), pltpu.SemaphoreType.DMA((7,)), pltpu.SemaphoreType.DMA((7,)),
                        pltpu.SemaphoreType.DMA],
        compiler_params=pltpu.CompilerParams(vmem_limit_bytes=VMEM_LIMIT),
    )(vec)


PACK_ALIGN = 1024


def _pack(arrs, row_multiple):
    flat = []
    for a in arrs:
        f = a.reshape(-1).astype(F32)
        flat.append(jnp.pad(f, (0, (-f.shape[0]) % PACK_ALIGN)))
    v = jnp.concatenate(flat)
    v = jnp.pad(v, (0, (-v.shape[0]) % (LANE * row_multiple)))
    return v.reshape(-1, LANE)


def _unpack(v, shapes):
    flat = v.reshape(-1)
    out, off = [], 0
    for s in shapes:
        size = math.prod(s)
        out.append(flat[off:off + size].reshape(s))
        off += size + (-size) % PACK_ALIGN
    return out


def _tile(dim, target):
    for cand in (1024, 512, 256, 128):
        if cand <= target and dim % cand == 0:
            return cand
    return dim


def _div_tile(dim, cap, mult=LANE):
    best = None
    for cand in range(mult, min(cap, dim) + 1, mult):
        if dim % cand == 0:
            best = cand
    return dim if best is None else best


WEIGHT_NAMES = ('norm1_g', 'w_in', 'ret_g', 'rg_conv_w', 'rg_conv_b', 'rg_wa', 'rg_ba', 'rg_wx', 'rg_bx', 'rg_lambda',
                'w_out', 'norm2_g', 'norm_mem_g', 'xa_wq', 'xa_wk', 'xa_wv', 'xa_wo', 'norm3_g', 'ffn_w_up',
                'ffn_conv_w', 'ffn_conv_b', 'ffn_w_down', 'final_g')
BIG_AXIS = {'w_in': 1, 'w_out': 0, 'xa_wq': 0, 'xa_wk': 0, 'xa_wv': 0, 'xa_wo': 0, 'ffn_w_up': 1, 'ffn_w_down': 0}
SMALL_SHARDED = ('rg_conv_w', 'ffn_conv_w')


def _step(x, mem, positions, loss_target, W, Mo, Vo):
    S, D = x.shape[1], x.shape[2]
    xs, mems, tgt = x[0], mem[0], loss_target[0]
    n_mem = mems.shape[0]
    pos_col = positions.reshape(S, 1)
    chip = 2 * lax.axis_index("x") + lax.axis_index("y")

    big = list(BIG_AXIS)
    shards = {n: W[n][0] for n in big}
    gathered = _gather_weights(
        [shards[n].astype(BF16) for n in big] + [W[n][0] for n in SMALL_SHARDED],
        [BIG_AXIS[n] for n in big] + [1, 1], [True] * len(big) + [False, False], name="gather_weights")
    G = dict(zip(big + list(SMALL_SHARDED), gathered))
    R = W['ret_g'].shape[1]
    Wl = W['rg_lambda'].shape[1]
    IN = G['w_in'].shape[1]
    F2 = G['ffn_w_up'].shape[1]
    F = F2 // 2

    norm1_g, norm2_g, norm3_g = W['norm1_g'], W['norm2_g'], W['norm3_g']
    norm_mem_g, final_g, ret_g = W['norm_mem_g'], W['final_g'].reshape(1, D), W['ret_g']
    rg_cw, rg_cb = G['rg_conv_w'], W['rg_conv_b']
    wa, wx = W['rg_wa'][0], W['rg_wx'][0]
    ba, bx = W['rg_ba'].reshape(1, Wl), W['rg_bx'].reshape(1, Wl)
    lam = W['rg_lambda']
    ffn_cw, ffn_cb = G['ffn_conv_w'], W['ffn_conv_b']

    def fwd_mm(a, wname, N, K, **kw):
        return _mm(a, G[wname], mode="nn", M=a.shape[0], N=N, K=K, tm=_tile(a.shape[0], 1024), tn=1024,
                   tk=_div_tile(K, 3072), **kw)

    def fwd_mm_norm(a, wname, res, g, name):
        return _mm(a, G[wname], mode="nn", M=a.shape[0], N=D, K=a.shape[1], tm=512, tn=D, tk=_div_tile(a.shape[1], 2048),
                   out_dtype=F32, res=res, norm_g=g, name=name)

    def bwd_x_mm(d, wname, N, K, **kw):
        return _mm(d, G[wname], mode="nt", M=d.shape[0], N=N, K=K, tm=_tile(d.shape[0], 1024),
                   tn=_div_tile(N, 1024, 256), tk=_div_tile(K, 3072), **kw)

    def bwd_w_mm(a, d, M, N, **kw):
        Ks = a.shape[0]
        return _mm(a, d, mode="tn", M=M, N=N, K=Ks, out_dtype=BF16, tm=_div_tile(M, 1024, 256),
                   tn=_div_tile(N, 1024, 256), tk=_div_tile(Ks, 2048 if d.dtype == BF16 else 1024), **kw)

    xn1 = _rmsnorm_fwd(xs, norm1_g, name="norm1_fwd")
    h = fwd_mm(xn1, 'w_in', IN, D, out_dtype=F32, name="mm_in")
    half = (R // RET_HEADS) // 2
    inv = (ROPE_BASE ** (-jnp.arange(half, dtype=F32) / half)).reshape(1, half)
    cos, sin = _rope_table(pos_col, inv, name="rope_table")
    hl, mix = _lru_fwd(h, rg_cw, rg_cb, wa, ba, wx, bx, lam, name="lru_fwd")
    ret_raw, states, mix = _ret_fwd(h, cos, sin, ret_g, mix, name="ret_fwd")
    x1, xn2 = fwd_mm_norm(mix, 'w_out', xs, norm2_g, "mm_out")
    memn = _rmsnorm_fwd(mems, norm_mem_g, name="norm_mem_fwd")
    km = fwd_mm(memn, 'xa_wk', D, D, out_dtype=BF16, name="mm_k")
    vm = fwd_mm(memn, 'xa_wv', D, D, out_dtype=BF16, name="mm_v")
    q = fwd_mm(xn2, 'xa_wq', D, D, out_dtype=BF16, name="mm_q")
    o = _xattn_fwd(q, km, vm, name="xattn_fwd")
    x2, xn3 = fwd_mm_norm(o, 'xa_wo', x1, norm3_g, "mm_o")
    hh = fwd_mm(xn3, 'ffn_w_up', F2, D, out_dtype=F32, name="mm_up")
    act = _ffn_gate_fwd(hh, ffn_cw, ffn_cb, name="ffn_gate_fwd")
    x3 = fwd_mm(act, 'ffn_w_down', D, F, out_dtype=F32, res=x2, name="mm_down")
    dx3, d_final, loss8, dx3h = _final_loss(x3, tgt, final_g, name="final_loss")

    gw = {}
    dact = bwd_x_mm(dx3h, 'ffn_w_down', F, D, out_dtype=F32, name="mm_dact")
    gw['ffn_w_down'] = bwd_w_mm(act, dx3h, F, D, name="mm_dw_down")
    dhh_a, dhh_b, gcw_a, gcw_b = _ffn_gate_bwd(hh, dact, ffn_cw, ffn_cb, name="ffn_gate_bwd")
    gw_up = bwd_w_mm(xn3, dhh_a, D, F, out_full=(D, F2), name="mm_dw_up_a")
    gw['ffn_w_up'] = bwd_w_mm(xn3, dhh_b, D, F, out=gw_up, out_off=(0, F), name="mm_dw_up_b")
    dxn3 = bwd_x_mm(dhh_a, 'ffn_w_up', D, F, out_dtype=F32, name="mm_dxn3_a")
    dxn3 = bwd_x_mm(dhh_b, 'ffn_w_up', D, F, out_dtype=F32, b_off=(0, F), res=dxn3, name="mm_dxn3_b")
    dx2, d_norm3, dx2h = _rmsnorm_bwd(x2, dxn3, norm3_g, dx3, name="norm3_bwd", emit_bf16=True)
    Kc = ffn_cw.shape[0]
    d_ffn_cw = jnp.concatenate([gcw_a[:Kc], gcw_b[:Kc]], axis=1)
    d_ffn_cb = jnp.concatenate([gcw_a[Kc:Kc + 1], gcw_b[Kc:Kc + 1]], axis=1)

    d_o = bwd_x_mm(dx2h, 'xa_wo', D, D, out_dtype=BF16, name="mm_do")
    gw['xa_wo'] = bwd_w_mm(o, dx2h, D, D, name="mm_dw_o")
    dq, dk, dv = _xattn_bwd(q, km, vm, d_o, name="xattn_bwd")
    gw['xa_wq'] = bwd_w_mm(xn2, dq, D, D, name="mm_dw_q")
    dxn2 = bwd_x_mm(dq, 'xa_wq', D, D, out_dtype=F32, name="mm_dxn2")
    gw['xa_wk'] = bwd_w_mm(memn, dk, D, D, name="mm_dw_k")
    gw['xa_wv'] = bwd_w_mm(memn, dv, D, D, name="mm_dw_v")
    dmemn = bwd_x_mm(dk, 'xa_wk', D, D, out_dtype=F32, name="mm_dmem_k")
    dmemn = bwd_x_mm(dv, 'xa_wv', D, D, out_dtype=F32, res=dmemn, name="mm_dmem_v")
    _, d_norm_mem = _rmsnorm_bwd(mems, dmemn, norm_mem_g, None, name="norm_mem_bwd")
    dx1, d_norm2, dx1h = _rmsnorm_bwd(x1, dxn2, norm2_g, dx2, name="norm2_bwd", emit_bf16=True)

    dmix = bwd_x_mm(dx1h, 'w_out', D, D, out_dtype=F32, name="mm_dmix")
    gw['w_out'] = bwd_w_mm(mix, dx1h, D, D, name="mm_dw_out")
    dh, d_ret_g = _ret_bwd(h, cos, sin, ret_g, states, ret_raw, dmix, name="ret_bwd")
    dh, d_rcw, d_rcb, d_wa, d_ba, d_wx, d_bx, d_lam = _lru_bwd(
        h, hl, dmix, dh, rg_cw, rg_cb, wa, ba, wx, bx, lam, name="lru_bwd")
    gw['w_in'] = bwd_w_mm(xn1, dh, D, IN, name="mm_dw_in")
    dxn1 = bwd_x_mm(dh, 'w_in', D, IN, out_dtype=F32, name="mm_dxn1")
    grad_x, d_norm1 = _rmsnorm_bwd(xs, dxn1, norm1_g, dx1, name="norm1_bwd")

    lands = _gather_grads([gw[n] for n in big], [BIG_AXIS[n] for n in big], name="gather_grads")
    small_parts = {
        'norm1_g': d_norm1, 'ret_g': d_ret_g, 'rg_conv_w': d_rcw[:rg_cw.shape[0]], 'rg_conv_b': d_rcb,
        'rg_wa': d_wa, 'rg_ba': d_ba, 'rg_wx': d_wx, 'rg_bx': d_bx, 'rg_lambda': d_lam, 'norm2_g': d_norm2,
        'norm_mem_g': d_norm_mem, 'norm3_g': d_norm3, 'ffn_conv_w': d_ffn_cw, 'ffn_conv_b': d_ffn_cb,
        'final_g': d_final}
    small = [n for n in WEIGHT_NAMES if n not in BIG_AXIS]
    red_shapes = [(1,)] + [tuple(small_parts[n].shape) for n in small]
    reduced = _allreduce_small(_pack([loss8[0:1, 0:1]] + [small_parts[n] for n in small], SUB), name="allreduce_small")
    red = _unpack(reduced, red_shapes)
    loss = red[0][0]
    g_small = dict(zip(small, red[1:]))
    for n in SMALL_SHARDED:
        w_local = W[n].shape[-1]
        g_small[n] = lax.dynamic_slice_in_dim(g_small[n], chip * w_local, w_local, axis=1)

    out_g, out_d, out_m, out_v = {}, {}, {}, {}
    for n, land in zip(big, lands):
        g, d, m_new, v_new = _adamw(shards[n], Mo[n][0], Vo[n][0], land, name="adamw_" + n)
        out_g[n], out_d[n], out_m[n], out_v[n] = (t.reshape(W[n].shape) for t in (g, d, m_new, v_new))
    rows = 512
    pk = lambda d: _pack([d[n] for n in small], rows)
    g_pack = _pack([g_small[n] for n in small], rows)
    res_small = _adamw(pk(W), pk(Mo), pk(Vo), g_pack[None], name="adamw_small")
    shapes_small = [tuple(W[n].shape) for n in small]
    for dst, packed in zip((out_g, out_d, out_m, out_v), res_small):
        for n, val in zip(small, _unpack(packed, shapes_small)):
            dst[n] = val
    return (loss, grad_x[None], *[out_g[n] for n in WEIGHT_NAMES], *[out_d[n] for n in WEIGHT_NAMES],
            *[out_m[n] for n in WEIGHT_NAMES], *[out_v[n] for n in WEIGHT_NAMES])


def kernel(x, mem, positions, norm1_g, w_in, ret_g, rg_conv_w, rg_conv_b, rg_wa, rg_ba, rg_wx, rg_bx, rg_lambda, w_out, norm2_g, norm_mem_g, xa_wq, xa_wk, xa_wv, xa_wo, norm3_g, ffn_w_up, ffn_conv_w, ffn_conv_b, ffn_w_down, final_g, loss_target, m_norm1_g, m_w_in, m_ret_g, m_rg_conv_w, m_rg_conv_b, m_rg_wa, m_rg_ba, m_rg_wx, m_rg_bx, m_rg_lambda, m_w_out, m_norm2_g, m_norm_mem_g, m_xa_wq, m_xa_wk, m_xa_wv, m_xa_wo, m_norm3_g, m_ffn_w_up, m_ffn_conv_w, m_ffn_conv_b, m_ffn_w_down, m_final_g, v_norm1_g, v_w_in, v_ret_g, v_rg_conv_w, v_rg_conv_b, v_rg_wa, v_rg_ba, v_rg_wx, v_rg_bx, v_rg_lambda, v_w_out, v_norm2_g, v_norm_mem_g, v_xa_wq, v_xa_wk, v_xa_wv, v_xa_wo, v_norm3_g, v_ffn_w_up, v_ffn_conv_w, v_ffn_conv_b, v_ffn_w_down, v_final_g):
    W = dict(zip(WEIGHT_NAMES, (norm1_g, w_in, ret_g, rg_conv_w, rg_conv_b, rg_wa, rg_ba, rg_wx, rg_bx, rg_lambda, w_out,
                                norm2_g, norm_mem_g, xa_wq, xa_wk, xa_wv, xa_wo, norm3_g, ffn_w_up, ffn_conv_w,
                                ffn_conv_b, ffn_w_down, final_g)))
    Mo = dict(zip(WEIGHT_NAMES, (m_norm1_g, m_w_in, m_ret_g, m_rg_conv_w, m_rg_conv_b, m_rg_wa, m_rg_ba, m_rg_wx, m_rg_bx,
                                 m_rg_lambda, m_w_out, m_norm2_g, m_norm_mem_g, m_xa_wq, m_xa_wk, m_xa_wv, m_xa_wo,
                                 m_norm3_g, m_ffn_w_up, m_ffn_conv_w, m_ffn_conv_b, m_ffn_w_down, m_final_g)))
    Vo = dict(zip(WEIGHT_NAMES, (v_norm1_g, v_w_in, v_ret_g, v_rg_conv_w, v_rg_conv_b, v_rg_wa, v_rg_ba, v_rg_wx, v_rg_bx,
                                 v_rg_lambda, v_w_out, v_norm2_g, v_norm_mem_g, v_xa_wq, v_xa_wk, v_xa_wv, v_xa_wo,
                                 v_norm3_g, v_ffn_w_up, v_ffn_conv_w, v_ffn_conv_b, v_ffn_w_down, v_final_g)))
    return _step(x, mem, positions, loss_target, W, Mo, Vo)
```

```python
import functools
import math

import jax
import jax.numpy as jnp
from jax import lax
from jax.experimental import pallas as pl
from jax.experimental.pallas import tpu as pltpu

F32 = jnp.float32
BF16 = jnp.bfloat16

EPS = 1e-6
RET_HEADS = 4
RET_CHUNK = 128
ROPE_BASE = 10000.0
LRU_BLOCKS = 8
LRU_C = 8.0
XA_HEADS = 4

ADAM_LR = 0.001
ADAM_B1 = 0.9
ADAM_B2 = 0.999
ADAM_EPS = 1e-08
ADAM_WD = 0.01
ADAM_STEP = 10

N_DEV = 8
N_CHIP = 4
MESH = pl.DeviceIdType.MESH
SUB = 8
LANE = 128
VMEM_LIMIT = 56 * 1024 * 1024

NN = ((1,), (0,))
NT = ((1,), (1,))
TN = ((0,), (0,))


def _cparams(sem):
    return pltpu.CompilerParams(dimension_semantics=sem, vmem_limit_bytes=VMEM_LIMIT)


def _sigmoid(v):
    return 1.0 / (1.0 + jnp.exp(-v))


def _bdot(a, b, dims):
    return lax.dot_general(a.astype(BF16), b.astype(BF16), (dims, ((), ())), preferred_element_type=F32)


def _row_iota(shape):
    return lax.broadcasted_iota(jnp.int32, shape, 0)


def _shift_down(v, tail, k):
    if k == 0:
        return v
    r = pltpu.roll(v, k, 0)
    rt = pltpu.roll(tail, k, 0)
    first = jnp.where(_row_iota(rt.shape) < k, rt, r[0:SUB])
    return jnp.concatenate([first, r[SUB:]], axis=0)


def _shift_up(v, head, k):
    if k == 0:
        return v
    n = v.shape[0]
    r = pltpu.roll(v, n - k, 0)
    rh = pltpu.roll(head, SUB - k, 0)
    last = jnp.where(_row_iota(rh.shape) >= SUB - k, rh, r[n - SUB:n])
    return jnp.concatenate([r[:n - SUB], last], axis=0)


def _mm(a, b, *, mode, M, N, K, out_dtype, name, tm=512, tn=512, tk=512, a_off=(0, 0), b_off=(0, 0),
        res=None, out=None, out_off=(0, 0), out_full=None, norm_g=None, after=None):
    tm, tn, tk = min(tm, M), min(tn, N), min(tk, K)
    assert M % tm == 0 and N % tn == 0 and K % tk == 0, (name, M, N, K, tm, tn, tk)
    nk = K // tk
    if mode == "nn":
        a_blk, b_blk, dims = (tm, tk), (tk, tn), NN
        a_map = lambda i, j, k: (i + a_off[0] // tm, k + a_off[1] // tk)
        b_map = lambda i, j, k: (k + b_off[0] // tk, j + b_off[1] // tn)
    elif mode == "nt":
        a_blk, b_blk, dims = (tm, tk), (tn, tk), NT
        a_map = lambda i, j, k: (i + a_off[0] // tm, k + a_off[1] // tk)
        b_map = lambda i, j, k: (j + b_off[0] // tn, k + b_off[1] // tk)
    else:
        a_blk, b_blk, dims = (tk, tm), (tk, tn), TN
        a_map = lambda i, j, k: (k + a_off[0] // tk, i + a_off[1] // tm)
        b_map = lambda i, j, k: (k + b_off[0] // tk, j + b_off[1] // tn)
    for off, blk in ((a_off, a_blk), (b_off, b_blk), (out_off, (tm, tn))):
        assert off[0] % blk[0] == 0 and off[1] % blk[1] == 0, (name, off, blk)
    o_map = lambda i, j, k: (i + out_off[0] // tm, j + out_off[1] // tn)
    has_res, has_out, has_norm, has_after = res is not None, out is not None, norm_g is not None, after is not None
    assert not has_norm or (tn == N and not has_out)

    def body(*refs):
        refs = list(refs)
        a_ref, b_ref = refs[0], refs[1]
        pos = 2
        r_ref = g_ref = n_ref = None
        if has_res:
            r_ref = refs[pos]
            pos += 1
        if has_norm:
            g_ref = refs[pos]
            pos += 1
        pos += has_out + has_after
        o_ref = refs[pos]
        pos += 1
        if has_norm:
            n_ref = refs[pos]
            pos += 1
        acc = refs[pos] if nk > 1 else None
        k = pl.program_id(2)
        part = _bdot(a_ref[...], b_ref[...], dims)

        def finish(total):
            if has_res:
                total = total + r_ref[...].astype(F32)
            o_ref[...] = total.astype(o_ref.dtype)
            if has_norm:
                r = lax.rsqrt(jnp.mean(total * total, axis=-1, keepdims=True) + EPS)
                n_ref[...] = (total * r * g_ref[...]).astype(n_ref.dtype)

        if nk == 1:
            finish(part)
        else:
            @pl.when(k == 0)
            def _():
                acc[...] = part

            @pl.when(k > 0)
            def _():
                acc[...] += part

            @pl.when(k == nk - 1)
            def _():
                finish(acc[...])

    in_specs = [pl.BlockSpec(a_blk, a_map), pl.BlockSpec(b_blk, b_map)]
    args = [a, b]
    if has_res:
        in_specs.append(pl.BlockSpec((tm, tn), lambda i, j, k: (i, j)))
        args.append(res)
    if has_norm:
        in_specs.append(pl.BlockSpec((1, N), lambda i, j, k: (0, 0)))
        args.append(norm_g)
    aliases = {}
    if has_out:
        in_specs.append(pl.BlockSpec(memory_space=pl.ANY))
        aliases = {len(args): 0}
        args.append(out)
        out_shape = jax.ShapeDtypeStruct(out.shape, out.dtype)
    else:
        out_shape = jax.ShapeDtypeStruct((M, N) if out_full is None else out_full, out_dtype)
    if has_after:
        in_specs.append(pl.BlockSpec(memory_space=pl.ANY))
        args.append(after)
    out_specs = pl.BlockSpec((tm, tn), o_map)
    if has_norm:
        out_shape = [out_shape, jax.ShapeDtypeStruct((M, N), BF16)]
        out_specs = [out_specs, pl.BlockSpec((tm, tn), lambda i, j, k: (i, j))]
    return pl.pallas_call(
        body, name=name, grid=(M // tm, N // tn, nk), in_specs=in_specs,
        out_specs=out_specs, out_shape=out_shape,
        scratch_shapes=[pltpu.VMEM((tm, tn), F32)] if nk > 1 else [],
        input_output_aliases=aliases,
        compiler_params=_cparams(("parallel", "parallel", "arbitrary")),
    )(*args)


def _rmsnorm_fwd(x, g, *, name, ts=512):
    S, D = x.shape
    ts = min(ts, S)

    def body(x_ref, g_ref, o_ref):
        xv = x_ref[...]
        r = lax.rsqrt(jnp.mean(xv * xv, axis=-1, keepdims=True) + EPS)
        o_ref[...] = (xv * r * g_ref[...]).astype(o_ref.dtype)

    return pl.pallas_call(
        body, name=name, grid=(S // ts,),
        in_specs=[pl.BlockSpec((ts, D), lambda i: (i, 0)), pl.BlockSpec((1, D), lambda i: (0, 0))],
        out_specs=pl.BlockSpec((ts, D), lambda i: (i, 0)),
        out_shape=jax.ShapeDtypeStruct((S, D), BF16),
        compiler_params=_cparams(("parallel",)),
    )(x, g)


def _rmsnorm_bwd(x, dxn, g, res, *, name, ts=256, emit_bf16=False):
    S, D = x.shape
    ts = min(ts, S)
    has_res = res is not None

    def body(*refs):
        refs = list(refs)
        dx16_ref = refs.pop() if emit_bf16 else None
        if has_res:
            x_ref, d_ref, g_ref, r_ref, dx_ref, dg_ref = refs
        else:
            x_ref, d_ref, g_ref, dx_ref, dg_ref = refs
        i = pl.program_id(0)
        xv = x_ref[...]
        dv = d_ref[...].astype(F32)
        r = lax.rsqrt(jnp.mean(xv * xv, axis=-1, keepdims=True) + EPS)
        gd = dv * g_ref[...]
        proj = jnp.mean(xv * gd, axis=-1, keepdims=True)
        dx = r * gd - xv * (r * r * r) * proj
        if has_res:
            dx = dx + r_ref[...]
        dx_ref[...] = dx
        if emit_bf16:
            dx16_ref[...] = dx.astype(BF16)
        part = jnp.sum(dv * xv * r, axis=0, keepdims=True)

        @pl.when(i == 0)
        def _():
            dg_ref[...] = part

        @pl.when(i > 0)
        def _():
            dg_ref[...] += part

    row = pl.BlockSpec((ts, D), lambda i: (i, 0))
    vec = pl.BlockSpec((1, D), lambda i: (0, 0))
    in_specs = [row, row, vec] + ([row] if has_res else [])
    args = [x, dxn, g] + ([res] if has_res else [])
    extra = emit_bf16 * [jax.ShapeDtypeStruct((S, D), BF16)]
    return pl.pallas_call(
        body, name=name, grid=(S // ts,), in_specs=in_specs, out_specs=[row, vec] + emit_bf16 * [row],
        out_shape=[jax.ShapeDtypeStruct((S, D), F32), jax.ShapeDtypeStruct((1, D), F32)] + extra,
        compiler_params=_cparams(("arbitrary",)),
    )(*args)


def _final_loss(x, target, g, *, name, ts=256):
    S, D = x.shape
    ts = min(ts, S)

    def body(x_ref, t_ref, g_ref, dx_ref, dg_ref, loss_ref, dx16_ref):
        i = pl.program_id(0)
        xv = x_ref[...]
        gv = g_ref[...]
        r = lax.rsqrt(jnp.mean(xv * xv, axis=-1, keepdims=True) + EPS)
        y = xv * r * gv
        err = y - t_ref[...]
        row_loss = jnp.mean(err * err, axis=-1, keepdims=True)
        lpart = 0.5 * jnp.sum(row_loss, axis=0, keepdims=True)
        dy = err * (1.0 / D)
        gd = dy * gv
        proj = jnp.mean(xv * gd, axis=-1, keepdims=True)
        dx = r * gd - xv * (r * r * r) * proj
        dx_ref[...] = dx
        dx16_ref[...] = dx.astype(BF16)
        part = jnp.sum(dy * xv * r, axis=0, keepdims=True)
        lfull = jnp.broadcast_to(lpart, loss_ref.shape)

        @pl.when(i == 0)
        def _():
            dg_ref[...] = part
            loss_ref[...] = lfull

        @pl.when(i > 0)
        def _():
            dg_ref[...] += part
            loss_ref[...] += lfull

    row = pl.BlockSpec((ts, D), lambda i: (i, 0))
    vec = pl.BlockSpec((1, D), lambda i: (0, 0))
    return pl.pallas_call(
        body, name=name, grid=(S // ts,), in_specs=[row, row, vec],
        out_specs=[row, vec, pl.BlockSpec((SUB, LANE), lambda i: (0, 0)), row],
        out_shape=[jax.ShapeDtypeStruct((S, D), F32), jax.ShapeDtypeStruct((1, D), F32),
                   jax.ShapeDtypeStruct((SUB, LANE), F32), jax.ShapeDtypeStruct((S, D), BF16)],
        compiler_params=_cparams(("arbitrary",)),
    )(x, target, g)


def _rope_table(pos_col, inv, *, name, ts=1024):
    S = pos_col.shape[0]
    ts = min(ts, S)
    half = inv.shape[1]

    def body(p_ref, inv_ref, c_ref, s_ref):
        ang = p_ref[...].astype(F32) * inv_ref[...]
        c_ref[...] = jnp.cos(ang)
        s_ref[...] = jnp.sin(ang)

    tab = pl.BlockSpec((ts, half), lambda i: (i, 0))
    return pl.pallas_call(
        body, name=name, grid=(S // ts,),
        in_specs=[pl.BlockSpec((ts, 1), lambda i: (i, 0)), pl.BlockSpec((1, half), lambda i: (0, 0))],
        out_specs=[tab, tab],
        out_shape=[jax.ShapeDtypeStruct((S, half), F32), jax.ShapeDtypeStruct((S, half), F32)],
        compiler_params=_cparams(("parallel",)),
    )(pos_col, inv)


def _ret_consts(C, log_g):
    ii = lax.broadcasted_iota(jnp.int32, (C, C), 0)
    jj = lax.broadcasted_iota(jnp.int32, (C, C), 1)
    diff = (ii - jj).astype(F32)
    intra = jnp.where(ii >= jj, jnp.exp(log_g * jnp.maximum(diff, 0.0)), 0.0)
    idx = lax.broadcasted_iota(jnp.int32, (C, 1), 0).astype(F32)
    qd = jnp.exp(log_g * (idx + 1.0))
    kd = jnp.exp(log_g * (C - 1.0 - idx))
    cd = math.exp(log_g * C)
    return intra, qd, kd, cd


def _rot(t, cs, sn):
    half = t.shape[-1] // 2
    t1, t2 = t[:, :half], t[:, half:]
    return jnp.concatenate([t1 * cs - t2 * sn, t1 * sn + t2 * cs], axis=-1)


def _unrot(d, cs, sn):
    half = d.shape[-1] // 2
    d1, d2 = d[:, :half], d[:, half:]
    return jnp.concatenate([d1 * cs + d2 * sn, d2 * cs - d1 * sn], axis=-1)


def _ret_fwd(h, cos, sin, ret_g, mix, *, name, ch=2):
    S = h.shape[0]
    R = ret_g.shape[1]
    H, C = RET_HEADS, RET_CHUNK
    Dh = R // H
    ts = ch * C
    assert S % ts == 0
    log_gs = [math.log(1.0 - 2.0 ** (-5.0 - hd)) for hd in range(H)]
    scale = Dh ** -0.5

    def body(x_ref, c_ref, s_ref, rg_ref, mix_in, ret_ref, st_ref, mix_ref, state):
        i = pl.program_id(0)

        @pl.when(i == 0)
        def _():
            state[...] = jnp.zeros_like(state)

        for c in range(ch):
            rows = pl.ds(c * C, C)
            cs, sn = c_ref[rows, :], s_ref[rows, :]
            for hd in range(H):
                intra, qd, kd, cd = _ret_consts(C, log_gs[hd])
                q = x_ref[rows, pl.ds(hd * Dh, Dh)]
                k = x_ref[rows, pl.ds(R + hd * Dh, Dh)]
                v = x_ref[rows, pl.ds(2 * R + hd * Dh, Dh)]
                g = x_ref[rows, pl.ds(3 * R + hd * Dh, Dh)]
                rq = _rot(q, cs, sn)
                rk = _rot(k, cs, sn) * scale
                st = state[hd]
                st_ref[c, hd] = st.astype(BF16)
                s_ = _bdot(rq, rk, NT) * intra
                ret = _bdot(s_, v, NN) + _bdot(rq * qd, st, NN)
                state[hd] = st * cd + _bdot(rk * kd, v, TN)
                ret_ref[rows, pl.ds(hd * Dh, Dh)] = ret
                rr = lax.rsqrt(jnp.mean(ret * ret, axis=-1, keepdims=True) + EPS)
                out = ret * rr * rg_ref[:, pl.ds(hd * Dh, Dh)] * (g * _sigmoid(g))
                mix_ref[rows, pl.ds(hd * Dh, Dh)] = out.astype(BF16)

    n_chunks = S // C
    return pl.pallas_call(
        body, name=name, grid=(S // ts,),
        in_specs=[pl.BlockSpec((ts, 4 * R), lambda i: (i, 0)),
                  pl.BlockSpec((ts, Dh // 2), lambda i: (i, 0)), pl.BlockSpec((ts, Dh // 2), lambda i: (i, 0)),
                  pl.BlockSpec((1, R), lambda i: (0, 0)), pl.BlockSpec(memory_space=pl.ANY)],
        out_specs=[pl.BlockSpec((ts, R), lambda i: (i, 0)),
                   pl.BlockSpec((ch, H, Dh, Dh), lambda i: (i, 0, 0, 0)),
                   pl.BlockSpec((ts, R), lambda i: (i, 0))],
        out_shape=[jax.ShapeDtypeStruct((S, R), F32), jax.ShapeDtypeStruct((n_chunks, H, Dh, Dh), BF16),
                   jax.ShapeDtypeStruct(mix.shape, mix.dtype)],
        scratch_shapes=[pltpu.VMEM((H, Dh, Dh), F32)],
        input_output_aliases={4: 2},
        compiler_params=_cparams(("arbitrary",)),
    )(h, cos, sin, ret_g, mix)


def _ret_bwd(h, cos, sin, ret_g, states, ret_raw, dmix, *, name, ch=2):
    S = h.shape[0]
    R = ret_g.shape[1]
    H, C = RET_HEADS, RET_CHUNK
    Dh = R // H
    ts = ch * C
    nb = S // ts
    log_gs = [math.log(1.0 - 2.0 ** (-5.0 - hd)) for hd in range(H)]
    scale = Dh ** -0.5

    def body(x_ref, c_ref, s_ref, rg_ref, st_ref, ret_ref, dm_ref, dh_ref, drg_ref, dstate):
        i = pl.program_id(0)

        @pl.when(i == 0)
        def _():
            dstate[...] = jnp.zeros_like(dstate)
            drg_ref[...] = jnp.zeros_like(drg_ref)

        for c in reversed(range(ch)):
            rows = pl.ds(c * C, C)
            cs, sn = c_ref[rows, :], s_ref[rows, :]
            for hd in range(H):
                intra, qd, kd, cd = _ret_consts(C, log_gs[hd])
                cols = pl.ds(hd * Dh, Dh)
                q = x_ref[rows, pl.ds(hd * Dh, Dh)]
                k = x_ref[rows, pl.ds(R + hd * Dh, Dh)]
                v = x_ref[rows, pl.ds(2 * R + hd * Dh, Dh)]
                g = x_ref[rows, pl.ds(3 * R + hd * Dh, Dh)]
                rq = _rot(q, cs, sn)
                rk = _rot(k, cs, sn) * scale
                ret = ret_ref[rows, cols]
                dm = dm_ref[rows, cols]
                rgv = rg_ref[:, cols]
                rr = lax.rsqrt(jnp.mean(ret * ret, axis=-1, keepdims=True) + EPS)
                retn = ret * rr
                sg = _sigmoid(g)
                silu = g * sg
                drg_ref[:, cols] += jnp.sum(dm * retn * silu, axis=0, keepdims=True)
                dg = dm * retn * rgv * (sg * (1.0 + g * (1.0 - sg)))
                dretn = dm * rgv * silu
                d_o = rr * dretn - ret * (rr * rr * rr) * jnp.mean(ret * dretn, axis=-1, keepdims=True)
                st = st_ref[c, hd]
                d_s = dstate[hd]
                a_ = _bdot(rq, rk, NT) * intra
                d_a = _bdot(d_o, v, NT) * intra
                d_qr = _bdot(d_a, rk, NN) + _bdot(d_o, st, NT) * qd
                d_kr = _bdot(d_a, rq, TN) + _bdot(v, d_s, NT) * kd
                d_v = _bdot(a_, d_o, TN) + _bdot(rk * kd, d_s, NN)
                dstate[hd] = d_s * cd + _bdot(rq * qd, d_o, TN)
                dh_ref[rows, pl.ds(hd * Dh, Dh)] = _unrot(d_qr, cs, sn).astype(BF16)
                dh_ref[rows, pl.ds(R + hd * Dh, Dh)] = (_unrot(d_kr, cs, sn) * scale).astype(BF16)
                dh_ref[rows, pl.ds(2 * R + hd * Dh, Dh)] = d_v.astype(BF16)
                dh_ref[rows, pl.ds(3 * R + hd * Dh, Dh)] = dg.astype(BF16)

    rb = lambda i: nb - 1 - i
    return pl.pallas_call(
        body, name=name, grid=(nb,),
        in_specs=[pl.BlockSpec((ts, 4 * R), lambda i: (rb(i), 0)),
                  pl.BlockSpec((ts, Dh // 2), lambda i: (rb(i), 0)), pl.BlockSpec((ts, Dh // 2), lambda i: (rb(i), 0)),
                  pl.BlockSpec((1, R), lambda i: (0, 0)),
                  pl.BlockSpec((ch, H, Dh, Dh), lambda i: (rb(i), 0, 0, 0)),
                  pl.BlockSpec((ts, R), lambda i: (rb(i), 0)),
                  pl.BlockSpec((ts, R), lambda i: (rb(i), 0))],
        out_specs=[pl.BlockSpec((ts, 4 * R), lambda i: (rb(i), 0)), pl.BlockSpec((1, R), lambda i: (0, 0))],
        out_shape=[jax.ShapeDtypeStruct((S, 6 * R), BF16), jax.ShapeDtypeStruct((1, R), F32)],
        scratch_shapes=[pltpu.VMEM((H, Dh, Dh), F32)],
        compiler_params=_cparams(("arbitrary",)),
    )(h, cos, sin, ret_g, states, ret_raw, dmix)


GELU_C = math.sqrt(2.0 / math.pi)
GELU_A = 0.044715


def _gelu_parts(y):
    t = jnp.tanh(GELU_C * (y + GELU_A * y * y * y))
    val = 0.5 * y * (1.0 + t)
    grad = 0.5 * (1.0 + t) + 0.5 * y * (1.0 - t * t) * GELU_C * (1.0 + 3.0 * GELU_A * y * y)
    return val, grad


def _neg_expm1(x):
    series = -x * (1.0 + x * (1.0 / 2.0) * (1.0 + x * (1.0 / 3.0) * (1.0 + x * (1.0 / 4.0) * (
        1.0 + x * (1.0 / 5.0) * (1.0 + x * (1.0 / 6.0) * (1.0 + x * (1.0 / 7.0)))))))
    return jnp.where(x > -0.35, series, 1.0 - jnp.exp(x))


def _log_sigmoid(x):
    return jnp.minimum(x, 0.0) - jnp.log1p(jnp.exp(-jnp.abs(x)))


def _lru_gates(uc, wa_ref, ba_ref, wx_ref, bx_ref):
    nbk = wa_ref.shape[0]
    bd = wa_ref.shape[1]
    rs, gs = [], []
    for n in range(nbk):
        ucn = uc[:, n * bd:(n + 1) * bd]
        rs.append(_sigmoid(_bdot(ucn, wa_ref[n], NN) + ba_ref[:, pl.ds(n * bd, bd)]))
        gs.append(_sigmoid(_bdot(ucn, wx_ref[n], NN) + bx_ref[:, pl.ds(n * bd, bd)]))
    return jnp.concatenate(rs, axis=-1), jnp.concatenate(gs, axis=-1)


def _lru_fwd(h, conv_w, conv_b, wa, ba, wx, bx, lam, *, name, ts=256):
    S = h.shape[0]
    W = lam.shape[1]
    K = conv_w.shape[0]
    ts = min(ts, S)

    def body(u_ref, y_ref, cw_ref, cb_ref, wa_ref, ba_ref, wx_ref, bx_ref, lam_ref, hl_ref, mix_ref, tail, hlast):
        i = pl.program_id(0)

        @pl.when(i == 0)
        def _():
            tail[...] = jnp.zeros_like(tail)
            hlast[...] = jnp.zeros_like(hlast)

        u = u_ref[...]
        tl = tail[...]
        uc = cb_ref[...] + cw_ref[K - 1:K, :] * u
        for k in range(K - 1):
            uc = uc + cw_ref[k:k + 1, :] * _shift_down(u, tl, K - 1 - k)
        tail[...] = u[ts - SUB:ts]
        r, ig = _lru_gates(uc, wa_ref, ba_ref, wx_ref, bx_ref)
        log_a = LRU_C * r * _log_sigmoid(lam_ref[...])
        a = jnp.exp(log_a)
        b = jnp.sqrt(_neg_expm1(2.0 * log_a)) * (ig * uc)
        rid = _row_iota((ts, W))
        d = 1
        while d < ts:
            a_s = jnp.where(rid < d, 1.0, pltpu.roll(a, d, 0))
            b_s = jnp.where(rid < d, 0.0, pltpu.roll(b, d, 0))
            b = a * b_s + b
            a = a * a_s
            d *= 2
        hcur = a * hlast[SUB - 1:SUB, :] + b
        hlast[...] = hcur[ts - SUB:ts]
        hl_ref[...] = hcur
        gy, _ = _gelu_parts(y_ref[...])
        mix_ref[...] = (hcur * gy).astype(BF16)

    full = lambda shape: pl.BlockSpec(shape, lambda i: tuple(0 for _ in shape))
    return pl.pallas_call(
        body, name=name, grid=(S // ts,),
        in_specs=[pl.BlockSpec((ts, W), lambda i: (i, 4)), pl.BlockSpec((ts, W), lambda i: (i, 5)),
                  full(conv_w.shape), full(conv_b.shape), full(wa.shape), full(ba.shape), full(wx.shape),
                  full(bx.shape), full(lam.shape)],
        out_specs=[pl.BlockSpec((ts, W), lambda i: (i, 0)), pl.BlockSpec((ts, W), lambda i: (i, 1))],
        out_shape=[jax.ShapeDtypeStruct((S, W), F32), jax.ShapeDtypeStruct((S, 2 * W), BF16)],
        scratch_shapes=[pltpu.VMEM((SUB, W), F32), pltpu.VMEM((SUB, W), F32)],
        compiler_params=_cparams(("arbitrary",)),
    )(h, h, conv_w, conv_b, wa, ba, wx, bx, lam)


def _lru_bwd(h, hl, dmix, dh, conv_w, conv_b, wa, ba, wx, bx, lam, *, name, ts=256):
    S = h.shape[0]
    W = lam.shape[1]
    K = conv_w.shape[0]
    nbk, bd = wa.shape[0], wa.shape[1]
    ts = min(ts, S)
    nb = S // ts
    t8 = ts // SUB

    def body(u_ref, y_ref, uh_ref, hl_ref, hh_ref, dm_ref, cw_ref, cb_ref, wa_ref, ba_ref, wx_ref, bx_ref, lam_ref,
             dh_in, dh_ref, dcw_ref, dcb_ref, dwa_ref, dba_ref, dwx_ref, dbx_ref, dlam_ref, carry, head):
        i = pl.program_id(0)
        blk = nb - 1 - i

        @pl.when(i == 0)
        def _():
            carry[...] = jnp.zeros_like(carry)
            head[...] = jnp.zeros_like(head)
            for ref in (dcw_ref, dcb_ref, dwa_ref, dba_ref, dwx_ref, dbx_ref, dlam_ref):
                ref[...] = jnp.zeros_like(ref)

        inside = (blk > 0).astype(F32)
        u = u_ref[...]
        tl = uh_ref[...] * inside
        sh = [_shift_down(u, tl, K - 1 - k) for k in range(K)]
        uc = cb_ref[...]
        for k in range(K):
            uc = uc + cw_ref[k:k + 1, :] * sh[k]
        r, ig = _lru_gates(uc, wa_ref, ba_ref, wx_ref, bx_ref)
        lam_v = lam_ref[...]
        ls = _log_sigmoid(lam_v)
        log_a = LRU_C * r * ls
        a = jnp.exp(log_a)
        mult = jnp.sqrt(_neg_expm1(2.0 * log_a))
        hcur = hl_ref[...]
        hprev = _shift_down(hcur, hh_ref[...] * inside, 1)
        gy, dgy = _gelu_parts(y_ref[...])
        dm = dm_ref[...]
        d_y = dm * hcur * dgy
        rid = _row_iota((ts, W))
        bq = dm * gy + jnp.where(rid == ts - 1, carry[0:1, :], 0.0)
        aq = jnp.where(rid == ts - 1, 0.0, pltpu.roll(a, ts - 1, 0))
        d = 1
        while d < ts:
            a_s = jnp.where(rid >= ts - d, 0.0, pltpu.roll(aq, ts - d, 0))
            b_s = jnp.where(rid >= ts - d, 0.0, pltpu.roll(bq, ts - d, 0))
            bq = bq + aq * b_s
            aq = aq * a_s
            d *= 2
        lam_t = bq
        carry[...] = (a * lam_t)[0:SUB]
        d_a = lam_t * hprev
        d_mult = lam_t * (ig * uc)
        d_i = lam_t * mult * uc
        d_uc = lam_t * mult * ig
        d_log_a = d_a * a - d_mult * (a * a) / mult
        d_r = d_log_a * (LRU_C * ls)
        dlam_ref[...] += jnp.sum(d_log_a * (LRU_C * r), axis=0, keepdims=True) * _sigmoid(-lam_v)
        d_pr = d_r * r * (1.0 - r)
        d_pi = d_i * ig * (1.0 - ig)
        dba_ref[...] += jnp.sum(d_pr, axis=0, keepdims=True)
        dbx_ref[...] += jnp.sum(d_pi, axis=0, keepdims=True)
        extra = []
        for n in range(nbk):
            sl = slice(n * bd, (n + 1) * bd)
            ucn = uc[:, sl]
            dwa_ref[n] += _bdot(ucn, d_pr[:, sl], TN)
            dwx_ref[n] += _bdot(ucn, d_pi[:, sl], TN)
            extra.append(_bdot(d_pr[:, sl], wa_ref[n], NT) + _bdot(d_pi[:, sl], wx_ref[n], NT))
        d_uc = d_uc + jnp.concatenate(extra, axis=-1)
        dcb_ref[...] += jnp.sum(d_uc, axis=0, keepdims=True)
        rid8 = _row_iota((SUB, W))
        dcw = jnp.zeros((SUB, W), F32)
        for k in range(K):
            dcw = dcw + jnp.where(rid8 == k, jnp.sum(d_uc * sh[k], axis=0, keepdims=True), 0.0)
        dcw_ref[...] += dcw
        hd = head[...]
        d_u = cw_ref[K - 1:K, :] * d_uc
        for j in range(1, K):
            d_u = d_u + cw_ref[K - 1 - j:K - j, :] * _shift_up(d_uc, hd, j)
        head[...] = d_uc[0:SUB]
        dh_ref[:, 0:W] = d_u.astype(BF16)
        dh_ref[:, W:2 * W] = d_y.astype(BF16)

    rb = lambda i: nb - 1 - i
    prev8 = lambda i: jnp.maximum(rb(i) * t8 - 1, 0)
    full = lambda shape: pl.BlockSpec(shape, lambda i: tuple(0 for _ in shape))
    small = [jax.ShapeDtypeStruct((SUB, W), F32), jax.ShapeDtypeStruct((1, W), F32),
             jax.ShapeDtypeStruct(wa.shape, F32), jax.ShapeDtypeStruct((1, W), F32),
             jax.ShapeDtypeStruct(wx.shape, F32), jax.ShapeDtypeStruct((1, W), F32),
             jax.ShapeDtypeStruct((1, W), F32)]
    return pl.pallas_call(
        body, name=name, grid=(nb,),
        in_specs=[pl.BlockSpec((ts, W), lambda i: (rb(i), 4)), pl.BlockSpec((ts, W), lambda i: (rb(i), 5)),
                  pl.BlockSpec((SUB, W), lambda i: (prev8(i), 4)),
                  pl.BlockSpec((ts, W), lambda i: (rb(i), 0)), pl.BlockSpec((SUB, W), lambda i: (prev8(i), 0)),
                  pl.BlockSpec((ts, W), lambda i: (rb(i), 1)),
                  full(conv_w.shape), full(conv_b.shape), full(wa.shape), full(ba.shape), full(wx.shape),
                  full(bx.shape), full(lam.shape), pl.BlockSpec(memory_space=pl.ANY)],
        out_specs=[pl.BlockSpec((ts, 2 * W), lambda i: (rb(i), 2))] + [full(s.shape) for s in small],
        out_shape=[jax.ShapeDtypeStruct(dh.shape, dh.dtype)] + small,
        scratch_shapes=[pltpu.VMEM((SUB, W), F32), pltpu.VMEM((SUB, W), F32)],
        input_output_aliases={13: 0},
        compiler_params=_cparams(("arbitrary",)),
    )(h, h, h, hl, hl, dmix, conv_w, conv_b, wa, ba, wx, bx, lam, dh)


def _xattn_fwd(q, km, vm, *, name, ts=512):
    S, D = q.shape
    M = km.shape[0]
    H = XA_HEADS
    Dh = D // H
    ts = min(ts, S)
    scale = Dh ** -0.5

    def body(q_ref, k_ref, v_ref, o_ref):
        for hd in range(H):
            cols = pl.ds(hd * Dh, Dh)
            s = _bdot(q_ref[:, cols], k_ref[:, cols], NT) * scale
            s = s - jnp.max(s, axis=-1, keepdims=True)
            e = jnp.exp(s)
            p = e / jnp.sum(e, axis=-1, keepdims=True)
            o_ref[:, cols] = _bdot(p, v_ref[:, cols], NN).astype(o_ref.dtype)

    return pl.pallas_call(
        body, name=name, grid=(S // ts,),
        in_specs=[pl.BlockSpec((ts, D), lambda i: (i, 0)), pl.BlockSpec((M, D), lambda i: (0, 0)),
                  pl.BlockSpec((M, D), lambda i: (0, 0))],
        out_specs=pl.BlockSpec((ts, D), lambda i: (i, 0)),
        out_shape=jax.ShapeDtypeStruct((S, D), BF16),
        compiler_params=_cparams(("parallel",)),
    )(q, km, vm)


def _xattn_bwd(q, km, vm, d_o, *, name, ts=512):
    S, D = q.shape
    M = km.shape[0]
    H = XA_HEADS
    Dh = D // H
    ts = min(ts, S)
    scale = Dh ** -0.5

    def body(q_ref, k_ref, v_ref, do_ref, dq_ref, dk_ref, dv_ref):
        i = pl.program_id(0)

        @pl.when(i == 0)
        def _():
            dk_ref[...] = jnp.zeros_like(dk_ref)
            dv_ref[...] = jnp.zeros_like(dv_ref)

        for hd in range(H):
            cols = pl.ds(hd * Dh, Dh)
            qh, kh, vh, doh = q_ref[:, cols], k_ref[:, cols], v_ref[:, cols], do_ref[:, cols]
            s = _bdot(qh, kh, NT) * scale
            s = s - jnp.max(s, axis=-1, keepdims=True)
            e = jnp.exp(s)
            p = e / jnp.sum(e, axis=-1, keepdims=True)
            dp = _bdot(doh, vh, NT)
            ds = p * (dp - jnp.sum(dp * p, axis=-1, keepdims=True)) * scale
            dq_ref[:, cols] = _bdot(ds, kh, NN).astype(dq_ref.dtype)
            dk_ref[:, cols] += _bdot(ds, qh, TN)
            dv_ref[:, cols] += _bdot(p, doh, TN)

    row = pl.BlockSpec((ts, D), lambda i: (i, 0))
    mem = pl.BlockSpec((M, D), lambda i: (0, 0))
    return pl.pallas_call(
        body, name=name, grid=(S // ts,), in_specs=[row, mem, mem, row], out_specs=[row, mem, mem],
        out_shape=[jax.ShapeDtypeStruct((S, D), BF16), jax.ShapeDtypeStruct((M, D), F32),
                   jax.ShapeDtypeStruct((M, D), F32)],
        compiler_params=_cparams(("arbitrary",)),
    )(q, km, vm, d_o)


def _conv_rows(v, tail, cw_ref, cb_ref):
    K = cw_ref.shape[0]
    sh = [_shift_down(v, tail, K - 1 - k) for k in range(K)]
    out = cb_ref[...]
    for k in range(K):
        out = out + cw_ref[k:k + 1, :] * sh[k]
    return out, sh


def _ffn_gate_fwd(hh, cw, cb, *, name, ts=512, tc=512):
    S, F2 = hh.shape
    F = F2 // 2
    ts, tc = min(ts, S), min(tc, F)
    nj = F // tc
    K = cw.shape[0]

    def body(a_ref, b_ref, cwa_ref, cwb_ref, cba_ref, cbb_ref, o_ref, ta, tb):
        i = pl.program_id(1)

        @pl.when(i == 0)
        def _():
            ta[...] = jnp.zeros_like(ta)
            tb[...] = jnp.zeros_like(tb)

        av, bv = a_ref[...], b_ref[...]
        ac, _ = _conv_rows(av, ta[...], cwa_ref, cba_ref)
        bc, _ = _conv_rows(bv, tb[...], cwb_ref, cbb_ref)
        ta[...] = av[ts - SUB:ts]
        tb[...] = bv[ts - SUB:ts]
        o_ref[...] = (ac * _sigmoid(ac) * bc).astype(o_ref.dtype)

    return pl.pallas_call(
        body, name=name, grid=(nj, S // ts),
        in_specs=[pl.BlockSpec((ts, tc), lambda j, i: (i, j)), pl.BlockSpec((ts, tc), lambda j, i: (i, j + nj)),
                  pl.BlockSpec((K, tc), lambda j, i: (0, j)), pl.BlockSpec((K, tc), lambda j, i: (0, j + nj)),
                  pl.BlockSpec((1, tc), lambda j, i: (0, j)), pl.BlockSpec((1, tc), lambda j, i: (0, j + nj))],
        out_specs=pl.BlockSpec((ts, tc), lambda j, i: (i, j)),
        out_shape=jax.ShapeDtypeStruct((S, F), BF16),
        scratch_shapes=[pltpu.VMEM((SUB, tc), F32), pltpu.VMEM((SUB, tc), F32)],
        compiler_params=_cparams(("parallel", "arbitrary")),
    )(hh, hh, cw, cw, cb, cb)


def _ffn_gate_bwd(hh, dact, cw, cb, *, name, ts=512, tc=512):
    S, F2 = hh.shape
    F = F2 // 2
    ts, tc = min(ts, S), min(tc, F)
    nj = F // tc
    nb = S // ts
    t8 = ts // SUB
    K = cw.shape[0]

    def body(a_ref, ah_ref, b_ref, bh_ref, d_ref, cwa_ref, cwb_ref, cba_ref, cbb_ref,
             da_ref, db_ref, ga_ref, gb_ref, ha, hb):
        i = pl.program_id(1)
        blk = nb - 1 - i

        @pl.when(i == 0)
        def _():
            for ref in (ha, hb, ga_ref, gb_ref):
                ref[...] = jnp.zeros_like(ref)

        inside = (blk > 0).astype(F32)
        ac, sha = _conv_rows(a_ref[...], ah_ref[...] * inside, cwa_ref, cba_ref)
        bc, shb = _conv_rows(b_ref[...], bh_ref[...] * inside, cwb_ref, cbb_ref)
        dv = d_ref[...].astype(F32)
        sg = _sigmoid(ac)
        d_bc = dv * ac * sg
        d_ac = dv * bc * sg * (1.0 + ac * (1.0 - sg))
        rid8 = _row_iota((SUB, tc))
        for d_c, sh, cw_ref, head, o_ref, g_ref in ((d_ac, sha, cwa_ref, ha, da_ref, ga_ref),
                                                     (d_bc, shb, cwb_ref, hb, db_ref, gb_ref)):
            hd = head[...]
            d_in = cw_ref[K - 1:K, :] * d_c
            for j in range(1, K):
                d_in = d_in + cw_ref[K - 1 - j:K - j, :] * _shift_up(d_c, hd, j)
            head[...] = d_c[0:SUB]
            o_ref[...] = d_in.astype(o_ref.dtype)
            gsum = jnp.where(rid8 == K, jnp.sum(d_c, axis=0, keepdims=True), 0.0)
            for k in range(K):
                gsum = gsum + jnp.where(rid8 == k, jnp.sum(d_c * sh[k], axis=0, keepdims=True), 0.0)
            g_ref[...] += gsum

    rb = lambda i: nb - 1 - i
    prev8 = lambda i: jnp.maximum(rb(i) * t8 - 1, 0)
    outs = pl.pallas_call(
        body, name=name, grid=(nj, nb),
        in_specs=[pl.BlockSpec((ts, tc), lambda j, i: (rb(i), j)), pl.BlockSpec((SUB, tc), lambda j, i: (prev8(i), j)),
                  pl.BlockSpec((ts, tc), lambda j, i: (rb(i), j + nj)),
                  pl.BlockSpec((SUB, tc), lambda j, i: (prev8(i), j + nj)),
                  pl.BlockSpec((ts, tc), lambda j, i: (rb(i), j)),
                  pl.BlockSpec((K, tc), lambda j, i: (0, j)), pl.BlockSpec((K, tc), lambda j, i: (0, j + nj)),
                  pl.BlockSpec((1, tc), lambda j, i: (0, j)), pl.BlockSpec((1, tc), lambda j, i: (0, j + nj))],
        out_specs=[pl.BlockSpec((ts, tc), lambda j, i: (rb(i), j)), pl.BlockSpec((ts, tc), lambda j, i: (rb(i), j)),
                   pl.BlockSpec((SUB, tc), lambda j, i: (0, j)), pl.BlockSpec((SUB, tc), lambda j, i: (0, j))],
        out_shape=[jax.ShapeDtypeStruct((S, F), BF16), jax.ShapeDtypeStruct((S, F), BF16),
                   jax.ShapeDtypeStruct((SUB, F), F32), jax.ShapeDtypeStruct((SUB, F), F32)],
        scratch_shapes=[pltpu.VMEM((SUB, tc), F32), pltpu.VMEM((SUB, tc), F32)],
        compiler_params=_cparams(("parallel", "arbitrary")),
    )(hh, hh, hh, hh, dact, cw, cw, cb, cb)
    return outs


ADAM_BLOCK_ELEMS = 128 * 1024


def _adamw(w, m, v, parts, *, name):
    R, C = w.shape
    n = parts.shape[0]
    tr = R
    for cand in (1024, 512, 256, 128, 64, 32, 16):
        if R % cand == 0 and cand * C <= ADAM_BLOCK_ELEMS:
            tr = cand
            break
    c1 = 1.0 - ADAM_B1 ** ADAM_STEP
    c2 = 1.0 - ADAM_B2 ** ADAM_STEP

    def body(w_ref, m_ref, v_ref, p_ref, g_ref, d_ref, nm_ref, nv_ref):
        g = p_ref[0].astype(F32)
        for k in range(1, n):
            g = g + p_ref[k].astype(F32)
        m_new = ADAM_B1 * m_ref[...] + (1.0 - ADAM_B1) * g
        v_new = ADAM_B2 * v_ref[...] + (1.0 - ADAM_B2) * (g * g)
        m_hat = m_new / c1
        v_hat = v_new / c2
        g_ref[...] = g
        d_ref[...] = -ADAM_LR * (m_hat / (jnp.sqrt(v_hat) + ADAM_EPS) + ADAM_WD * w_ref[...])
        nm_ref[...] = m_new
        nv_ref[...] = v_new

    blk = pl.BlockSpec((tr, C), lambda i: (i, 0))
    sds = jax.ShapeDtypeStruct((R, C), F32)
    return pl.pallas_call(
        body, name=name, grid=(R // tr,),
        in_specs=[blk, blk, blk, pl.BlockSpec((n, tr, C), lambda i: (0, i, 0))],
        out_specs=[blk, blk, blk, blk], out_shape=[sds, sds, sds, sds],
        compiler_params=_cparams(("parallel",)),
    )(w, m, v, parts)


def _mesh_place():
    x, y, c = lax.axis_index("x"), lax.axis_index("y"), lax.axis_index("c")
    others = [(1 - x, y), (x, 1 - y), (1 - x, 1 - y)]
    return x, y, c, others


HBM_SPEC = pl.BlockSpec(memory_space=pltpu.HBM)
SEM_SPEC = pl.BlockSpec(memory_space=pltpu.SEMAPHORE)
ANY_SPEC = pl.BlockSpec(memory_space=pl.ANY)
EFFECT = pltpu.SideEffectType.DATAFLOW_SIDE_EFFECTING


def _in_hbm(a):
    return pltpu.with_memory_space_constraint(a, pltpu.HBM)


def _split_start(srcs, lands, copies, *, name):
    n_s, n_l = len(srcs), len(lands)
    n_cp = 3 * n_s

    def body(*refs):
        src_refs, land_refs = refs[:n_s], refs[n_s:n_s + n_l]
        ssem, rsem = refs[n_s + n_l], refs[n_s + n_l + 1]
        token = refs[-1]
        for outgoing, _ in copies(src_refs, land_refs, ssem, rsem):
            outgoing.start()
        token[...] = jnp.zeros_like(token)

    outs = pl.pallas_call(
        body, name=name,
        out_shape=(pltpu.SemaphoreType.DMA((n_cp,)), pltpu.SemaphoreType.DMA((n_cp,)),
                   *[pltpu.HBM(a.shape, a.dtype) for a in srcs], *[pltpu.HBM(a.shape, a.dtype) for a in lands],
                   jax.ShapeDtypeStruct((SUB, LANE), F32)),
        in_specs=[HBM_SPEC] * (n_s + n_l),
        out_specs=(SEM_SPEC, SEM_SPEC, *[HBM_SPEC] * (n_s + n_l), pl.BlockSpec(memory_space=pltpu.VMEM)),
        input_output_aliases={i: 2 + i for i in range(n_s + n_l)},
        compiler_params=pltpu.CompilerParams(has_side_effects=EFFECT),
    )(*[_in_hbm(a) for a in srcs], *[_in_hbm(a) for a in lands])
    ssem, rsem = outs[0], outs[1]
    return ssem, rsem, list(outs[2:2 + n_s]), list(outs[2 + n_s:2 + n_s + n_l]), outs[-1]


def _split_wait(srcs, lands, ssem, rsem, after, copies, *, name):
    n_s, n_l = len(srcs), len(lands)

    def body(*refs):
        src_refs, land_refs = refs[:n_s], refs[n_s:n_s + n_l]
        s_ref, r_ref = refs[n_s + n_l], refs[n_s + n_l + 1]
        for outgoing, incoming in copies(src_refs, land_refs, s_ref, r_ref):
            outgoing.wait_send()
            incoming.wait_recv()

    outs = pl.pallas_call(
        body, name=name,
        out_shape=(*[pltpu.HBM(a.shape, a.dtype) for a in srcs], *[pltpu.HBM(a.shape, a.dtype) for a in lands]),
        in_specs=[HBM_SPEC] * (n_s + n_l) + [SEM_SPEC, SEM_SPEC, ANY_SPEC], out_specs=[HBM_SPEC] * (n_s + n_l),
        input_output_aliases={i: i for i in range(n_s + n_l)},
        compiler_params=pltpu.CompilerParams(has_side_effects=EFFECT),
    )(*srcs, *lands, ssem, rsem, after)
    return list(outs[:n_s]), list(outs[n_s:])


class _WeightGather:
    def __init__(self, shards, axes, splits, tag):
        self.shards, self.axes, self.splits, self.tag = list(shards), axes, splits, tag
        self.n = len(shards)
        self.full_shapes = [(s.shape[0] * N_CHIP, s.shape[1]) if ax == 0 else (s.shape[0], s.shape[1] * N_CHIP)
                            for s, ax in zip(shards, axes)]

    def _region(self, land_refs, it, chip, half):
        r, w = self.shards[it].shape
        by_rows = self.axes[it] == 0
        if self.splits[it] and half is not None:
            rows = pl.ds(pl.multiple_of(half * (r // 2) + (chip * r if by_rows else 0), 16), r // 2)
        else:
            rows = pl.ds(chip * r if by_rows else 0, r)
        cols = pl.ds(0, w) if by_rows else pl.ds(pl.multiple_of(chip * w, LANE), w)
        return land_refs[it].at[rows, cols]

    def _src_half(self, src_refs, it, half):
        r = self.shards[it].shape[0]
        if self.splits[it]:
            return src_refs[it].at[pl.ds(pl.multiple_of(half * (r // 2), 16), r // 2), :]
        return src_refs[it]

    def _ici(self, src_refs, land_refs, ssem, rsem):
        x, y, c, others = _mesh_place()
        pairs = []
        for it in range(self.n):
            for j, chip in enumerate(others):
                def mk(chip_from, it=it, j=j, chip=chip):
                    return pltpu.make_async_remote_copy(
                        src_ref=self._src_half(src_refs, it, c), dst_ref=self._region(land_refs, it, chip_from, c),
                        send_sem=ssem.at[3 * it + j], recv_sem=rsem.at[3 * it + j], device_id=(*chip, c),
                        device_id_type=MESH)
                pairs.append((mk(2 * x + y), mk(2 * chip[0] + chip[1])))
        return pairs

    def start(self):
        lands = [lax.empty(s, a.dtype) for s, a in zip(self.full_shapes, self.shards)]
        self.ssem, self.rsem, self.srcs, self.lands, token = _split_start(
            self.shards, lands, self._ici, name="gather_start_" + self.tag)
        return token

    def finish(self, after):
        srcs, lands = _split_wait(self.srcs, self.lands, self.ssem, self.rsem, after, self._ici,
                                  name="gather_wait_" + self.tag)
        n = self.n
        n_fwd = 3 * sum(self.splits)

        def body(*refs):
            src_refs, out_refs = refs[:n], refs[2 * n:3 * n]
            fsend, frecv, lsem = refs[3 * n:]
            x, y, c, others = _mesh_place()
            sibling = (x, y, 1 - c)

            def fwd(it, slot, chip, half):
                reg = self._region(out_refs, it, 2 * chip[0] + chip[1], half)
                return pltpu.make_async_remote_copy(src_ref=reg, dst_ref=reg, send_sem=fsend.at[slot],
                                                    recv_sem=frecv.at[slot], device_id=sibling, device_id_type=MESH)

            local = [pltpu.make_async_copy(src_refs[it], self._region(out_refs, it, 2 * x + y, None), lsem.at[it])
                     for it in range(n)]
            for cp in local:
                cp.start()
            sends, recvs = [], []
            for it in range(n):
                if self.splits[it]:
                    for chip in others:
                        sends.append(fwd(it, len(sends), chip, c))
                        recvs.append(fwd(it, len(recvs), chip, 1 - c))
            for cp in sends:
                cp.start()
            for cp in recvs:
                cp.wait_recv()
            for cp in sends:
                cp.wait_send()
            for cp in local:
                cp.wait()

        fulls = pl.pallas_call(
            body, name="gather_d2d_" + self.tag, in_specs=[ANY_SPEC] * (2 * n), out_specs=[ANY_SPEC] * n,
            out_shape=[jax.ShapeDtypeStruct(a.shape, a.dtype) for a in lands],
            scratch_shapes=[pltpu.SemaphoreType.DMA((max(n_fwd, 1),)), pltpu.SemaphoreType.DMA((max(n_fwd, 1),)),
                            pltpu.SemaphoreType.DMA((n,))],
            input_output_aliases={n + i: i for i in range(n)},
        )(*srcs, *lands)
        return list(fulls)


class _GradGather:
    def __init__(self, grads, axes, tag):
        self.grads, self.axes, self.tag = list(grads), axes, tag
        self.n = len(grads)
        self.shard_shapes = [(g.shape[0] // N_CHIP, g.shape[1]) if ax == 0 else (g.shape[0], g.shape[1] // N_CHIP)
                             for g, ax in zip(grads, axes)]

    def _piece(self, src_refs, it, chip):
        r, w = self.shard_shapes[it]
        if self.axes[it] == 0:
            return src_refs[it].at[pl.ds(pl.multiple_of(chip * r, 16), r), :]
        return src_refs[it].at[:, pl.ds(pl.multiple_of(chip * w, LANE), w)]

    def _ici(self, src_refs, land_refs, ssem, rsem):
        x, y, c, others = _mesh_place()
        me = 4 * x + 2 * y + c
        pairs = []
        for it in range(self.n):
            for j, chip in enumerate(others):
                def mk(slot, it=it, j=j, chip=chip):
                    return pltpu.make_async_remote_copy(
                        src_ref=self._piece(src_refs, it, 2 * chip[0] + chip[1]), dst_ref=land_refs[it].at[slot],
                        send_sem=ssem.at[3 * it + j], recv_sem=rsem.at[3 * it + j], device_id=(*chip, c),
                        device_id_type=MESH)
                pairs.append((mk(me), mk(4 * chip[0] + 2 * chip[1] + c)))
        return pairs

    def start(self):
        lands = [lax.empty((N_DEV, *s), g.dtype) for s, g in zip(self.shard_shapes, self.grads)]
        self.ssem, self.rsem, self.srcs, self.lands, token = _split_start(
            self.grads, lands, self._ici, name="grads_start_" + self.tag)
        return token

    def finish(self, after):
        srcs, lands = _split_wait(self.srcs, self.lands, self.ssem, self.rsem, after, self._ici,
                                  name="grads_wait_" + self.tag)
        n = self.n

        def body(*refs):
            src_refs, out_refs = refs[:n], refs[2 * n:3 * n]
            fsend, frecv, lsem = refs[3 * n:]
            x, y, c, others = _mesh_place()
            sibling = (x, y, 1 - c)
            me = 4 * x + 2 * y + c

            def to_sibling(it, k, slot, src=None):
                return pltpu.make_async_remote_copy(
                    src_ref=out_refs[it].at[slot] if src is None else src, dst_ref=out_refs[it].at[slot],
                    send_sem=fsend.at[4 * it + k], recv_sem=frecv.at[4 * it + k], device_id=sibling,
                    device_id_type=MESH)

            local, sends, recvs = [], [], []
            for it in range(n):
                own = self._piece(src_refs, it, 2 * x + y)
                local.append(pltpu.make_async_copy(own, out_refs[it].at[me], lsem.at[it]))
                sends.append(to_sibling(it, 0, me, src=own))
                recvs.append(to_sibling(it, 0, 4 * x + 2 * y + 1 - c, src=own))
                for j, chip in enumerate(others):
                    sends.append(to_sibling(it, 1 + j, 4 * chip[0] + 2 * chip[1] + c))
                    recvs.append(to_sibling(it, 1 + j, 4 * chip[0] + 2 * chip[1] + 1 - c))
            for cp in local + sends:
                cp.start()
            for cp in recvs:
                cp.wait_recv()
            for cp in sends:
                cp.wait_send()
            for cp in local:
                cp.wait()

        outs = pl.pallas_call(
            body, name="grads_d2d_" + self.tag, in_specs=[ANY_SPEC] * (2 * n), out_specs=[ANY_SPEC] * n,
            out_shape=[jax.ShapeDtypeStruct(a.shape, a.dtype) for a in lands],
            scratch_shapes=[pltpu.SemaphoreType.DMA((4 * n,)), pltpu.SemaphoreType.DMA((4 * n,)),
                            pltpu.SemaphoreType.DMA((n,))],
            input_output_aliases={n + i: i for i in range(n)},
        )(*srcs, *lands)
        return list(outs)


def _allreduce_small(vec, *, name):
    R, L = vec.shape

    def body(v_ref, o_ref, buf, send, recv, lsem):
        x, y, c, others = _mesh_place()
        me = 4 * x + 2 * y + c
        sibling = (x, y, 1 - c)

        def copy(k, slot, to, src=None):
            return pltpu.make_async_remote_copy(
                src_ref=buf.at[slot] if src is None else src, dst_ref=buf.at[slot], send_sem=send.at[k],
                recv_sem=recv.at[k], device_id=to, device_id_type=MESH)

        def slot_of(chip, core):
            return 4 * chip[0] + 2 * chip[1] + core

        mine = pltpu.make_async_copy(v_ref, buf.at[me], lsem)
        mine.start()
        first = [copy(0, me, sibling, src=v_ref)]
        first += [copy(1 + j, me, (*chip, c), src=v_ref) for j, chip in enumerate(others)]
        for cp in first:
            cp.start()
        passed = [copy(4 + j, slot_of(chip, c), sibling) for j, chip in enumerate(others)]
        for j, chip in enumerate(others):
            copy(1 + j, slot_of(chip, c), (*chip, c)).wait_recv()
            passed[j].start()
        copy(0, slot_of((x, y), 1 - c), sibling).wait_recv()
        for j, chip in enumerate(others):
            copy(4 + j, slot_of(chip, 1 - c), sibling).wait_recv()
        for cp in first + passed:
            cp.wait_send()
        mine.wait()
        total = buf[0]
        for k in range(1, N_DEV):
            total = total + buf[k]
        o_ref[...] = total

    return pl.pallas_call(
        body, name=name, in_specs=[pl.BlockSpec(memory_space=pltpu.VMEM)],
        out_specs=pl.BlockSpec(memory_space=pltpu.VMEM), out_shape=jax.ShapeDtypeStruct((R, L), F32),
        scratch_shapes=[pltpu.VMEM((N_DEV, R, L), F32), pltpu.SemaphoreType.DMA((7,)), pltpu.SemaphoreType.DMA((7,)),
                        pltpu.SemaphoreType.DMA],
        compiler_params=pltpu.CompilerParams(vmem_limit_bytes=VMEM_LIMIT),
    )(vec)


PACK_ALIGN = 1024


def _pack(arrs, row_multiple):
    flat = []
    for a in arrs:
        f = a.reshape(-1).astype(F32)
        flat.append(jnp.pad(f, (0, (-f.shape[0]) % PACK_ALIGN)))
    v = jnp.concatenate(flat)
    v = jnp.pad(v, (0, (-v.shape[0]) % (LANE * row_multiple)))
    return v.reshape(-1, LANE)


def _unpack(v, shapes):
    flat = v.reshape(-1)
    out, off = [], 0
    for s in shapes:
        size = math.prod(s)
        out.append(flat[off:off + size].reshape(s))
        off += size + (-size) % PACK_ALIGN
    return out


def _tile(dim, target):
    for cand in (1024, 512, 256, 128):
        if cand <= target and dim % cand == 0:
            return cand
    return dim


def _div_tile(dim, cap, mult=LANE):
    best = None
    for cand in range(mult, min(cap, dim) + 1, mult):
        if dim % cand == 0:
            best = cand
    return dim if best is None else best


WEIGHT_NAMES = ('norm1_g', 'w_in', 'ret_g', 'rg_conv_w', 'rg_conv_b', 'rg_wa', 'rg_ba', 'rg_wx', 'rg_bx', 'rg_lambda',
                'w_out', 'norm2_g', 'norm_mem_g', 'xa_wq', 'xa_wk', 'xa_wv', 'xa_wo', 'norm3_g', 'ffn_w_up',
                'ffn_conv_w', 'ffn_conv_b', 'ffn_w_down', 'final_g')
BIG_AXIS = {'w_in': 1, 'w_out': 0, 'xa_wq': 0, 'xa_wk': 0, 'xa_wv': 0, 'xa_wo': 0, 'ffn_w_up': 1, 'ffn_w_down': 0}
SMALL_SHARDED = ('rg_conv_w', 'ffn_conv_w')


def _step(x, mem, positions, loss_target, W, Mo, Vo):
    S, D = x.shape[1], x.shape[2]
    xs, mems, tgt = x[0], mem[0], loss_target[0]
    n_mem = mems.shape[0]
    pos_col = positions.reshape(S, 1)
    chip = 2 * lax.axis_index("x") + lax.axis_index("y")

    big = list(BIG_AXIS)
    shards = {n: W[n][0] for n in big}
    G = {}
    gather_groups = (('w_in', 'rg_conv_w'), ('w_out', 'xa_wq', 'xa_wk', 'xa_wv', 'xa_wo'),
                     ('ffn_w_up', 'ffn_w_down', 'ffn_conv_w'))
    gathers, tok = [], None
    for gi, names in enumerate(gather_groups):
        srcs = []
        for n in names:
            s = W[n][0] if tok is None else W[n][0] + tok[0, 0]
            srcs.append(s.astype(BF16) if n in BIG_AXIS else s)
        ag = _WeightGather(srcs, [BIG_AXIS.get(n, 1) for n in names], [n in BIG_AXIS for n in names], "g%d" % gi)
        tok = ag.start()
        gathers.append(ag)

    def finish_gather(gi, after):
        G.update(zip(gather_groups[gi], gathers[gi].finish(after)))

    finish_gather(0, tok)
    R = W['ret_g'].shape[1]
    Wl = W['rg_lambda'].shape[1]
    IN = W['w_in'].shape[2] * N_CHIP
    F2 = W['ffn_w_up'].shape[2] * N_CHIP
    F = F2 // 2

    norm1_g, norm2_g, norm3_g = W['norm1_g'] + tok[0, 0], W['norm2_g'], W['norm3_g']
    norm_mem_g, final_g, ret_g = W['norm_mem_g'], W['final_g'].reshape(1, D), W['ret_g']
    rg_cw, rg_cb = G['rg_conv_w'], W['rg_conv_b']
    wa, wx = W['rg_wa'][0], W['rg_wx'][0]
    ba, bx = W['rg_ba'].reshape(1, Wl), W['rg_bx'].reshape(1, Wl)
    lam = W['rg_lambda']
    ffn_cb = W['ffn_conv_b']

    def fwd_mm(a, wname, N, K, **kw):
        return _mm(a, G[wname], mode="nn", M=a.shape[0], N=N, K=K, tm=_tile(a.shape[0], 1024), tn=1024,
                   tk=_div_tile(K, 3072), **kw)

    def fwd_mm_norm(a, wname, res, g, name):
        return _mm(a, G[wname], mode="nn", M=a.shape[0], N=D, K=a.shape[1], tm=512, tn=D, tk=_div_tile(a.shape[1], 2048),
                   out_dtype=F32, res=res, norm_g=g, name=name)

    def bwd_x_mm(d, wname, N, K, **kw):
        return _mm(d, G[wname], mode="nt", M=d.shape[0], N=N, K=K, tm=_tile(d.shape[0], 1024),
                   tn=_div_tile(N, 1024, 256), tk=_div_tile(K, 3072), **kw)

    def bwd_w_mm(a, d, M, N, **kw):
        Ks = a.shape[0]
        return _mm(a, d, mode="tn", M=M, N=N, K=Ks, out_dtype=BF16, tm=_div_tile(M, 1024, 256),
                   tn=_div_tile(N, 1024, 256), tk=_div_tile(Ks, 2048 if d.dtype == BF16 else 1024), **kw)

    xn1 = _rmsnorm_fwd(xs, norm1_g, name="norm1_fwd")
    h = fwd_mm(xn1, 'w_in', IN, D, out_dtype=F32, name="mm_in")
    half = (R // RET_HEADS) // 2
    inv = (ROPE_BASE ** (-jnp.arange(half, dtype=F32) / half)).reshape(1, half)
    cos, sin = _rope_table(pos_col, inv, name="rope_table")
    hl, mix = _lru_fwd(h, rg_cw, rg_cb, wa, ba, wx, bx, lam, name="lru_fwd")
    ret_raw, states, mix = _ret_fwd(h, cos, sin, ret_g, mix, name="ret_fwd")
    finish_gather(1, mix)
    x1, xn2 = fwd_mm_norm(mix, 'w_out', xs, norm2_g, "mm_out")
    memn = _rmsnorm_fwd(mems, norm_mem_g, name="norm_mem_fwd")
    km = fwd_mm(memn, 'xa_wk', D, D, out_dtype=BF16, name="mm_k")
    vm = fwd_mm(memn, 'xa_wv', D, D, out_dtype=BF16, name="mm_v")
    q = fwd_mm(xn2, 'xa_wq', D, D, out_dtype=BF16, name="mm_q")
    o = _xattn_fwd(q, km, vm, name="xattn_fwd")
    x2, xn3 = fwd_mm_norm(o, 'xa_wo', x1, norm3_g, "mm_o")
    finish_gather(2, xn3)
    ffn_cw = G['ffn_conv_w']
    hh = fwd_mm(xn3, 'ffn_w_up', F2, D, out_dtype=F32, name="mm_up")
    act = _ffn_gate_fwd(hh, ffn_cw, ffn_cb, name="ffn_gate_fwd")
    x3 = fwd_mm(act, 'ffn_w_down', D, F, out_dtype=F32, res=x2, name="mm_down")
    dx3, d_final, loss8, dx3h = _final_loss(x3, tgt, final_g, name="final_loss")

    gw = {}
    grad_groups = []

    def start_grads(names, tag):
        gg = _GradGather([gw[n] for n in names], [BIG_AXIS[n] for n in names], tag)
        grad_groups.append((names, gg))
        return gg.start()

    dact = bwd_x_mm(dx3h, 'ffn_w_down', F, D, out_dtype=F32, name="mm_dact")
    gw['ffn_w_down'] = bwd_w_mm(act, dx3h, F, D, name="mm_dw_down")
    dhh_a, dhh_b, gcw_a, gcw_b = _ffn_gate_bwd(hh, dact, ffn_cw, ffn_cb, name="ffn_gate_bwd")
    gw_up = bwd_w_mm(xn3, dhh_a, D, F, out_full=(D, F2), name="mm_dw_up_a")
    gw['ffn_w_up'] = bwd_w_mm(xn3, dhh_b, D, F, out=gw_up, out_off=(0, F), name="mm_dw_up_b")
    tok_a = start_grads(('ffn_w_down', 'ffn_w_up'), "a")
    dxn3 = bwd_x_mm(dhh_a, 'ffn_w_up', D, F, out_dtype=F32, after=tok_a, name="mm_dxn3_a")
    dxn3 = bwd_x_mm(dhh_b, 'ffn_w_up', D, F, out_dtype=F32, b_off=(0, F), res=dxn3, name="mm_dxn3_b")
    dx2, d_norm3, dx2h = _rmsnorm_bwd(x2, dxn3, norm3_g, dx3, name="norm3_bwd", emit_bf16=True)
    Kc = ffn_cw.shape[0]
    d_ffn_cw = jnp.concatenate([gcw_a[:Kc], gcw_b[:Kc]], axis=1)
    d_ffn_cb = jnp.concatenate([gcw_a[Kc:Kc + 1], gcw_b[Kc:Kc + 1]], axis=1)

    d_o = bwd_x_mm(dx2h, 'xa_wo', D, D, out_dtype=BF16, name="mm_do")
    gw['xa_wo'] = bwd_w_mm(o, dx2h, D, D, name="mm_dw_o")
    dq, dk, dv = _xattn_bwd(q, km, vm, d_o, name="xattn_bwd")
    gw['xa_wq'] = bwd_w_mm(xn2, dq, D, D, name="mm_dw_q")
    dxn2 = bwd_x_mm(dq, 'xa_wq', D, D, out_dtype=F32, name="mm_dxn2")
    gw['xa_wk'] = bwd_w_mm(memn, dk, D, D, name="mm_dw_k")
    gw['xa_wv'] = bwd_w_mm(memn, dv, D, D, name="mm_dw_v")
    dmemn = bwd_x_mm(dk, 'xa_wk', D, D, out_dtype=F32, name="mm_dmem_k")
    dmemn = bwd_x_mm(dv, 'xa_wv', D, D, out_dtype=F32, res=dmemn, name="mm_dmem_v")
    _, d_norm_mem = _rmsnorm_bwd(mems, dmemn, norm_mem_g, None, name="norm_mem_bwd")
    dx1, d_norm2, dx1h = _rmsnorm_bwd(x1, dxn2, norm2_g, dx2, name="norm2_bwd", emit_bf16=True)

    gw['w_out'] = bwd_w_mm(mix, dx1h, D, D, name="mm_dw_out")
    tok_b = start_grads(('xa_wo', 'xa_wq', 'xa_wk', 'xa_wv', 'w_out'), "b")
    dmix = bwd_x_mm(dx1h, 'w_out', D, D, out_dtype=F32, after=tok_b, name="mm_dmix")
    dh, d_ret_g = _ret_bwd(h, cos, sin, ret_g, states, ret_raw, dmix, name="ret_bwd")
    dh, d_rcw, d_rcb, d_wa, d_ba, d_wx, d_bx, d_lam = _lru_bwd(
        h, hl, dmix, dh, rg_cw, rg_cb, wa, ba, wx, bx, lam, name="lru_bwd")
    gw['w_in'] = bwd_w_mm(xn1, dh, D, IN, name="mm_dw_in")
    tok_c = start_grads(('w_in',), "c")
    dxn1 = bwd_x_mm(dh, 'w_in', D, IN, out_dtype=F32, after=tok_c, name="mm_dxn1")
    grad_x, d_norm1 = _rmsnorm_bwd(xs, dxn1, norm1_g, dx1, name="norm1_bwd")

    small_parts = {
        'norm1_g': d_norm1, 'ret_g': d_ret_g, 'rg_conv_w': d_rcw[:rg_cw.shape[0]], 'rg_conv_b': d_rcb,
        'rg_wa': d_wa, 'rg_ba': d_ba, 'rg_wx': d_wx, 'rg_bx': d_bx, 'rg_lambda': d_lam, 'norm2_g': d_norm2,
        'norm_mem_g': d_norm_mem, 'norm3_g': d_norm3, 'ffn_conv_w': d_ffn_cw, 'ffn_conv_b': d_ffn_cb,
        'final_g': d_final}
    small = [n for n in WEIGHT_NAMES if n not in BIG_AXIS]
    red_shapes = [(1,)] + [tuple(small_parts[n].shape) for n in small]
    reduced = _allreduce_small(_pack([loss8[0:1, 0:1]] + [small_parts[n] for n in small], SUB), name="allreduce_small")
    red = _unpack(reduced, red_shapes)
    loss = red[0][0]
    g_small = dict(zip(small, red[1:]))
    for n in SMALL_SHARDED:
        w_local = W[n].shape[-1]
        g_small[n] = lax.dynamic_slice_in_dim(g_small[n], chip * w_local, w_local, axis=1)

    out_g, out_d, out_m, out_v = {}, {}, {}, {}
    for names, gg in grad_groups:
        for n, land in zip(names, gg.finish(reduced)):
            g, d, m_new, v_new = _adamw(shards[n], Mo[n][0], Vo[n][0], land, name="adamw_" + n)
            out_g[n], out_d[n], out_m[n], out_v[n] = (t.reshape(W[n].shape) for t in (g, d, m_new, v_new))
    rows = 512
    pk = lambda d: _pack([d[n] for n in small], rows)
    g_pack = _pack([g_small[n] for n in small], rows)
    res_small = _adamw(pk(W), pk(Mo), pk(Vo), g_pack[None], name="adamw_small")
    shapes_small = [tuple(W[n].shape) for n in small]
    for dst, packed in zip((out_g, out_d, out_m, out_v), res_small):
        for n, val in zip(small, _unpack(packed, shapes_small)):
            dst[n] = val
    return (loss, grad_x[None], *[out_g[n] for n in WEIGHT_NAMES], *[out_d[n] for n in WEIGHT_NAMES],
            *[out_m[n] for n in WEIGHT_NAMES], *[out_v[n] for n in WEIGHT_NAMES])


def kernel(x, mem, positions, norm1_g, w_in, ret_g, rg_conv_w, rg_conv_b, rg_wa, rg_ba, rg_wx, rg_bx, rg_lambda, w_out, norm2_g, norm_mem_g, xa_wq, xa_wk, xa_wv, xa_wo, norm3_g, ffn_w_up, ffn_conv_w, ffn_conv_b, ffn_w_down, final_g, loss_target, m_norm1_g, m_w_in, m_ret_g, m_rg_conv_w, m_rg_conv_b, m_rg_wa, m_rg_ba, m_rg_wx, m_rg_bx, m_rg_lambda, m_w_out, m_norm2_g, m_norm_mem_g, m_xa_wq, m_xa_wk, m_xa_wv, m_xa_wo, m_norm3_g, m_ffn_w_up, m_ffn_conv_w, m_ffn_conv_b, m_ffn_w_down, m_final_g, v_norm1_g, v_w_in, v_ret_g, v_rg_conv_w, v_rg_conv_b, v_rg_wa, v_rg_ba, v_rg_wx, v_rg_bx, v_rg_lambda, v_w_out, v_norm2_g, v_norm_mem_g, v_xa_wq, v_xa_wk, v_xa_wv, v_xa_wo, v_norm3_g, v_ffn_w_up, v_ffn_conv_w, v_ffn_conv_b, v_ffn_w_down, v_final_g):
    W = dict(zip(WEIGHT_NAMES, (norm1_g, w_in, ret_g, rg_conv_w, rg_conv_b, rg_wa, rg_ba, rg_wx, rg_bx, rg_lambda, w_out,
                                norm2_g, norm_mem_g, xa_wq, xa_wk, xa_wv, xa_wo, norm3_g, ffn_w_up, ffn_conv_w,
                                ffn_conv_b, ffn_w_down, final_g)))
    Mo = dict(zip(WEIGHT_NAMES, (m_norm1_g, m_w_in, m_ret_g, m_rg_conv_w, m_rg_conv_b, m_rg_wa, m_rg_ba, m_rg_wx, m_rg_bx,
                                 m_rg_lambda, m_w_out, m_norm2_g, m_norm_mem_g, m_xa_wq, m_xa_wk, m_xa_wv, m_xa_wo,
                                 m_norm3_g, m_ffn_w_up, m_ffn_conv_w, m_ffn_conv_b, m_ffn_w_down, m_final_g)))
    Vo = dict(zip(WEIGHT_NAMES, (v_norm1_g, v_w_in, v_ret_g, v_rg_conv_w, v_rg_conv_b, v_rg_wa, v_rg_ba, v_rg_wx, v_rg_bx,
                                 v_rg_lambda, v_w_out, v_norm2_g, v_norm_mem_g, v_xa_wq, v_xa_wk, v_xa_wv, v_xa_wo,
                                 v_norm3_g, v_ffn_w_up, v_ffn_conv_w, v_ffn_conv_b, v_ffn_w_down, v_final_g)))
    return _step(x, mem, positions, loss_target, W, Mo, Vo)
```

```python
import functools
import math

import jax
import jax.numpy as jnp
from jax import lax
from jax.experimental import pallas as pl
from jax.experimental.pallas import tpu as pltpu

F32 = jnp.float32
BF16 = jnp.bfloat16

EPS = 1e-6
RET_HEADS = 4
RET_CHUNK = 128
ROPE_BASE = 10000.0
LRU_BLOCKS = 8
LRU_C = 8.0
XA_HEADS = 4

ADAM_LR = 0.001
ADAM_B1 = 0.9
ADAM_B2 = 0.999
ADAM_EPS = 1e-08
ADAM_WD = 0.01
ADAM_STEP = 10

N_DEV = 8
N_CHIP = 4
MESH = pl.DeviceIdType.MESH
SUB = 8
LANE = 128
VMEM_LIMIT = 56 * 1024 * 1024

NN = ((1,), (0,))
NT = ((1,), (1,))
TN = ((0,), (0,))


def _cparams(sem):
    return pltpu.CompilerParams(dimension_semantics=sem, vmem_limit_bytes=VMEM_LIMIT)


def _sigmoid(v):
    return 1.0 / (1.0 + jnp.exp(-v))


def _bdot(a, b, dims):
    return lax.dot_general(a.astype(BF16), b.astype(BF16), (dims, ((), ())), preferred_element_type=F32)


def _row_iota(shape):
    return lax.broadcasted_iota(jnp.int32, shape, 0)


def _shift_down(v, tail, k):
    if k == 0:
        return v
    r = pltpu.roll(v, k, 0)
    rt = pltpu.roll(tail, k, 0)
    first = jnp.where(_row_iota(rt.shape) < k, rt, r[0:SUB])
    return jnp.concatenate([first, r[SUB:]], axis=0)


def _shift_up(v, head, k):
    if k == 0:
        return v
    n = v.shape[0]
    r = pltpu.roll(v, n - k, 0)
    rh = pltpu.roll(head, SUB - k, 0)
    last = jnp.where(_row_iota(rh.shape) >= SUB - k, rh, r[n - SUB:n])
    return jnp.concatenate([r[:n - SUB], last], axis=0)


def _mm(a, b, *, mode, M, N, K, out_dtype, name, tm=512, tn=512, tk=512, a_off=(0, 0), b_off=(0, 0),
        res=None, out=None, out_off=(0, 0), out_full=None, norm_g=None, after=None):
    tm, tn, tk = min(tm, M), min(tn, N), min(tk, K)
    assert M % tm == 0 and N % tn == 0 and K % tk == 0, (name, M, N, K, tm, tn, tk)
    nk = K // tk
    if mode == "nn":
        a_blk, b_blk, dims = (tm, tk), (tk, tn), NN
        a_map = lambda i, j, k: (i + a_off[0] // tm, k + a_off[1] // tk)
        b_map = lambda i, j, k: (k + b_off[0] // tk, j + b_off[1] // tn)
    elif mode == "nt":
        a_blk, b_blk, dims = (tm, tk), (tn, tk), NT
        a_map = lambda i, j, k: (i + a_off[0] // tm, k + a_off[1] // tk)
        b_map = lambda i, j, k: (j + b_off[0] // tn, k + b_off[1] // tk)
    else:
        a_blk, b_blk, dims = (tk, tm), (tk, tn), TN
        a_map = lambda i, j, k: (k + a_off[0] // tk, i + a_off[1] // tm)
        b_map = lambda i, j, k: (k + b_off[0] // tk, j + b_off[1] // tn)
    for off, blk in ((a_off, a_blk), (b_off, b_blk), (out_off, (tm, tn))):
        assert off[0] % blk[0] == 0 and off[1] % blk[1] == 0, (name, off, blk)
    o_map = lambda i, j, k: (i + out_off[0] // tm, j + out_off[1] // tn)
    has_res, has_out, has_norm, has_after = res is not None, out is not None, norm_g is not None, after is not None
    assert not has_norm or (tn == N and not has_out)

    def body(*refs):
        refs = list(refs)
        a_ref, b_ref = refs[0], refs[1]
        pos = 2
        r_ref = g_ref = n_ref = None
        if has_res:
            r_ref = refs[pos]
            pos += 1
        if has_norm:
            g_ref = refs[pos]
            pos += 1
        pos += has_out + has_after
        o_ref = refs[pos]
        pos += 1
        if has_norm:
            n_ref = refs[pos]
            pos += 1
        acc = refs[pos] if nk > 1 else None
        k = pl.program_id(2)
        part = _bdot(a_ref[...], b_ref[...], dims)

        def finish(total):
            if has_res:
                total = total + r_ref[...].astype(F32)
            o_ref[...] = total.astype(o_ref.dtype)
            if has_norm:
                r = lax.rsqrt(jnp.mean(total * total, axis=-1, keepdims=True) + EPS)
                n_ref[...] = (total * r * g_ref[...]).astype(n_ref.dtype)

        if nk == 1:
            finish(part)
        else:
            @pl.when(k == 0)
            def _():
                acc[...] = part

            @pl.when(k > 0)
            def _():
                acc[...] += part

            @pl.when(k == nk - 1)
            def _():
                finish(acc[...])

    in_specs = [pl.BlockSpec(a_blk, a_map), pl.BlockSpec(b_blk, b_map)]
    args = [a, b]
    if has_res:
        in_specs.append(pl.BlockSpec((tm, tn), lambda i, j, k: (i, j)))
        args.append(res)
    if has_norm:
        in_specs.append(pl.BlockSpec((1, N), lambda i, j, k: (0, 0)))
        args.append(norm_g)
    aliases = {}
    if has_out:
        in_specs.append(pl.BlockSpec(memory_space=pl.ANY))
        aliases = {len(args): 0}
        args.append(out)
        out_shape = jax.ShapeDtypeStruct(out.shape, out.dtype)
    else:
        out_shape = jax.ShapeDtypeStruct((M, N) if out_full is None else out_full, out_dtype)
    if has_after:
        in_specs.append(pl.BlockSpec(memory_space=pl.ANY))
        args.append(after)
    out_specs = pl.BlockSpec((tm, tn), o_map)
    if has_norm:
        out_shape = [out_shape, jax.ShapeDtypeStruct((M, N), BF16)]
        out_specs = [out_specs, pl.BlockSpec((tm, tn), lambda i, j, k: (i, j))]
    return pl.pallas_call(
        body, name=name, grid=(M // tm, N // tn, nk), in_specs=in_specs,
        out_specs=out_specs, out_shape=out_shape,
        scratch_shapes=[pltpu.VMEM((tm, tn), F32)] if nk > 1 else [],
        input_output_aliases=aliases,
        compiler_params=_cparams(("parallel", "parallel", "arbitrary")),
    )(*args)


def _rmsnorm_fwd(x, g, *, name, ts=512):
    S, D = x.shape
    ts = min(ts, S)

    def body(x_ref, g_ref, o_ref):
        xv = x_ref[...]
        r = lax.rsqrt(jnp.mean(xv * xv, axis=-1, keepdims=True) + EPS)
        o_ref[...] = (xv * r * g_ref[...]).astype(o_ref.dtype)

    return pl.pallas_call(
        body, name=name, grid=(S // ts,),
        in_specs=[pl.BlockSpec((ts, D), lambda i: (i, 0)), pl.BlockSpec((1, D), lambda i: (0, 0))],
        out_specs=pl.BlockSpec((ts, D), lambda i: (i, 0)),
        out_shape=jax.ShapeDtypeStruct((S, D), BF16),
        compiler_params=_cparams(("parallel",)),
    )(x, g)


def _rmsnorm_bwd(x, dxn, g, res, *, name, ts=256, emit_bf16=False):
    S, D = x.shape
    ts = min(ts, S)
    has_res = res is not None

    def body(*refs):
        refs = list(refs)
        dx16_ref = refs.pop() if emit_bf16 else None
        if has_res:
            x_ref, d_ref, g_ref, r_ref, dx_ref, dg_ref = refs
        else:
            x_ref, d_ref, g_ref, dx_ref, dg_ref = refs
        i = pl.program_id(0)
        xv = x_ref[...]
        dv = d_ref[...].astype(F32)
        r = lax.rsqrt(jnp.mean(xv * xv, axis=-1, keepdims=True) + EPS)
        gd = dv * g_ref[...]
        proj = jnp.mean(xv * gd, axis=-1, keepdims=True)
        dx = r * gd - xv * (r * r * r) * proj
        if has_res:
            dx = dx + r_ref[...]
        dx_ref[...] = dx
        if emit_bf16:
            dx16_ref[...] = dx.astype(BF16)
        part = jnp.sum(dv * xv * r, axis=0, keepdims=True)

        @pl.when(i == 0)
        def _():
            dg_ref[...] = part

        @pl.when(i > 0)
        def _():
            dg_ref[...] += part

    row = pl.BlockSpec((ts, D), lambda i: (i, 0))
    vec = pl.BlockSpec((1, D), lambda i: (0, 0))
    in_specs = [row, row, vec] + ([row] if has_res else [])
    args = [x, dxn, g] + ([res] if has_res else [])
    extra = emit_bf16 * [jax.ShapeDtypeStruct((S, D), BF16)]
    return pl.pallas_call(
        body, name=name, grid=(S // ts,), in_specs=in_specs, out_specs=[row, vec] + emit_bf16 * [row],
        out_shape=[jax.ShapeDtypeStruct((S, D), F32), jax.ShapeDtypeStruct((1, D), F32)] + extra,
        compiler_params=_cparams(("arbitrary",)),
    )(*args)


def _final_loss(x, target, g, *, name, ts=256):
    S, D = x.shape
    ts = min(ts, S)

    def body(x_ref, t_ref, g_ref, dx_ref, dg_ref, loss_ref, dx16_ref):
        i = pl.program_id(0)
        xv = x_ref[...]
        gv = g_ref[...]
        r = lax.rsqrt(jnp.mean(xv * xv, axis=-1, keepdims=True) + EPS)
        y = xv * r * gv
        err = y - t_ref[...]
        row_loss = jnp.mean(err * err, axis=-1, keepdims=True)
        lpart = 0.5 * jnp.sum(row_loss, axis=0, keepdims=True)
        dy = err * (1.0 / D)
        gd = dy * gv
        proj = jnp.mean(xv * gd, axis=-1, keepdims=True)
        dx = r * gd - xv * (r * r * r) * proj
        dx_ref[...] = dx
        dx16_ref[...] = dx.astype(BF16)
        part = jnp.sum(dy * xv * r, axis=0, keepdims=True)
        lfull = jnp.broadcast_to(lpart, loss_ref.shape)

        @pl.when(i == 0)
        def _():
            dg_ref[...] = part
            loss_ref[...] = lfull

        @pl.when(i > 0)
        def _():
            dg_ref[...] += part
            loss_ref[...] += lfull

    row = pl.BlockSpec((ts, D), lambda i: (i, 0))
    vec = pl.BlockSpec((1, D), lambda i: (0, 0))
    return pl.pallas_call(
        body, name=name, grid=(S // ts,), in_specs=[row, row, vec],
        out_specs=[row, vec, pl.BlockSpec((SUB, LANE), lambda i: (0, 0)), row],
        out_shape=[jax.ShapeDtypeStruct((S, D), F32), jax.ShapeDtypeStruct((1, D), F32),
                   jax.ShapeDtypeStruct((SUB, LANE), F32), jax.ShapeDtypeStruct((S, D), BF16)],
        compiler_params=_cparams(("arbitrary",)),
    )(x, target, g)


def _rope_table(pos_col, inv, *, name, ts=1024):
    S = pos_col.shape[0]
    ts = min(ts, S)
    half = inv.shape[1]

    def body(p_ref, inv_ref, c_ref, s_ref):
        ang = p_ref[...].astype(F32) * inv_ref[...]
        c_ref[...] = jnp.cos(ang)
        s_ref[...] = jnp.sin(ang)

    tab = pl.BlockSpec((ts, half), lambda i: (i, 0))
    return pl.pallas_call(
        body, name=name, grid=(S // ts,),
        in_specs=[pl.BlockSpec((ts, 1), lambda i: (i, 0)), pl.BlockSpec((1, half), lambda i: (0, 0))],
        out_specs=[tab, tab],
        out_shape=[jax.ShapeDtypeStruct((S, half), F32), jax.ShapeDtypeStruct((S, half), F32)],
        compiler_params=_cparams(("parallel",)),
    )(pos_col, inv)


def _ret_consts(C, log_g):
    ii = lax.broadcasted_iota(jnp.int32, (C, C), 0)
    jj = lax.broadcasted_iota(jnp.int32, (C, C), 1)
    diff = (ii - jj).astype(F32)
    intra = jnp.where(ii >= jj, jnp.exp(log_g * jnp.maximum(diff, 0.0)), 0.0)
    idx = lax.broadcasted_iota(jnp.int32, (C, 1), 0).astype(F32)
    qd = jnp.exp(log_g * (idx + 1.0))
    kd = jnp.exp(log_g * (C - 1.0 - idx))
    cd = math.exp(log_g * C)
    return intra, qd, kd, cd


def _rot(t, cs, sn):
    half = t.shape[-1] // 2
    t1, t2 = t[:, :half], t[:, half:]
    return jnp.concatenate([t1 * cs - t2 * sn, t1 * sn + t2 * cs], axis=-1)


def _unrot(d, cs, sn):
    half = d.shape[-1] // 2
    d1, d2 = d[:, :half], d[:, half:]
    return jnp.concatenate([d1 * cs + d2 * sn, d2 * cs - d1 * sn], axis=-1)


def _ret_fwd(h, cos, sin, ret_g, mix, *, name, ch=2):
    S = h.shape[0]
    R = ret_g.shape[1]
    H, C = RET_HEADS, RET_CHUNK
    Dh = R // H
    ts = ch * C
    assert S % ts == 0
    log_gs = [math.log(1.0 - 2.0 ** (-5.0 - hd)) for hd in range(H)]
    scale = Dh ** -0.5

    def body(x_ref, c_ref, s_ref, rg_ref, mix_in, ret_ref, st_ref, mix_ref, state):
        i = pl.program_id(0)

        @pl.when(i == 0)
        def _():
            state[...] = jnp.zeros_like(state)

        for c in range(ch):
            rows = pl.ds(c * C, C)
            cs, sn = c_ref[rows, :], s_ref[rows, :]
            for hd in range(H):
                intra, qd, kd, cd = _ret_consts(C, log_gs[hd])
                q = x_ref[rows, pl.ds(hd * Dh, Dh)]
                k = x_ref[rows, pl.ds(R + hd * Dh, Dh)]
                v = x_ref[rows, pl.ds(2 * R + hd * Dh, Dh)]
                g = x_ref[rows, pl.ds(3 * R + hd * Dh, Dh)]
                rq = _rot(q, cs, sn)
                rk = _rot(k, cs, sn) * scale
                st = state[hd]
                st_ref[c, hd] = st.astype(BF16)
                s_ = _bdot(rq, rk, NT) * intra
                ret = _bdot(s_, v, NN) + _bdot(rq * qd, st, NN)
                state[hd] = st * cd + _bdot(rk * kd, v, TN)
                ret_ref[rows, pl.ds(hd * Dh, Dh)] = ret
                rr = lax.rsqrt(jnp.mean(ret * ret, axis=-1, keepdims=True) + EPS)
                out = ret * rr * rg_ref[:, pl.ds(hd * Dh, Dh)] * (g * _sigmoid(g))
                mix_ref[rows, pl.ds(hd * Dh, Dh)] = out.astype(BF16)

    n_chunks = S // C
    return pl.pallas_call(
        body, name=name, grid=(S // ts,),
        in_specs=[pl.BlockSpec((ts, 4 * R), lambda i: (i, 0)),
                  pl.BlockSpec((ts, Dh // 2), lambda i: (i, 0)), pl.BlockSpec((ts, Dh // 2), lambda i: (i, 0)),
                  pl.BlockSpec((1, R), lambda i: (0, 0)), pl.BlockSpec(memory_space=pl.ANY)],
        out_specs=[pl.BlockSpec((ts, R), lambda i: (i, 0)),
                   pl.BlockSpec((ch, H, Dh, Dh), lambda i: (i, 0, 0, 0)),
                   pl.BlockSpec((ts, R), lambda i: (i, 0))],
        out_shape=[jax.ShapeDtypeStruct((S, R), F32), jax.ShapeDtypeStruct((n_chunks, H, Dh, Dh), BF16),
                   jax.ShapeDtypeStruct(mix.shape, mix.dtype)],
        scratch_shapes=[pltpu.VMEM((H, Dh, Dh), F32)],
        input_output_aliases={4: 2},
        compiler_params=_cparams(("arbitrary",)),
    )(h, cos, sin, ret_g, mix)


def _ret_bwd(h, cos, sin, ret_g, states, ret_raw, dmix, *, name, ch=2):
    S = h.shape[0]
    R = ret_g.shape[1]
    H, C = RET_HEADS, RET_CHUNK
    Dh = R // H
    ts = ch * C
    nb = S // ts
    log_gs = [math.log(1.0 - 2.0 ** (-5.0 - hd)) for hd in range(H)]
    scale = Dh ** -0.5

    def body(x_ref, c_ref, s_ref, rg_ref, st_ref, ret_ref, dm_ref, dh_ref, drg_ref, dstate):
        i = pl.program_id(0)

        @pl.when(i == 0)
        def _():
            dstate[...] = jnp.zeros_like(dstate)
            drg_ref[...] = jnp.zeros_like(drg_ref)

        for c in reversed(range(ch)):
            rows = pl.ds(c * C, C)
            cs, sn = c_ref[rows, :], s_ref[rows, :]
            for hd in range(H):
                intra, qd, kd, cd = _ret_consts(C, log_gs[hd])
                cols = pl.ds(hd * Dh, Dh)
                q = x_ref[rows, pl.ds(hd * Dh, Dh)]
                k = x_ref[rows, pl.ds(R + hd * Dh, Dh)]
                v = x_ref[rows, pl.ds(2 * R + hd * Dh, Dh)]
                g = x_ref[rows, pl.ds(3 * R + hd * Dh, Dh)]
                rq = _rot(q, cs, sn)
                rk = _rot(k, cs, sn) * scale
                ret = ret_ref[rows, cols]
                dm = dm_ref[rows, cols]
                rgv = rg_ref[:, cols]
                rr = lax.rsqrt(jnp.mean(ret * ret, axis=-1, keepdims=True) + EPS)
                retn = ret * rr
                sg = _sigmoid(g)
                silu = g * sg
                drg_ref[:, cols] += jnp.sum(dm * retn * silu, axis=0, keepdims=True)
                dg = dm * retn * rgv * (sg * (1.0 + g * (1.0 - sg)))
                dretn = dm * rgv * silu
                d_o = rr * dretn - ret * (rr * rr * rr) * jnp.mean(ret * dretn, axis=-1, keepdims=True)
                st = st_ref[c, hd]
                d_s = dstate[hd]
                a_ = _bdot(rq, rk, NT) * intra
                d_a = _bdot(d_o, v, NT) * intra
                d_qr = _bdot(d_a, rk, NN) + _bdot(d_o, st, NT) * qd
                d_kr = _bdot(d_a, rq, TN) + _bdot(v, d_s, NT) * kd
                d_v = _bdot(a_, d_o, TN) + _bdot(rk * kd, d_s, NN)
                dstate[hd] = d_s * cd + _bdot(rq * qd, d_o, TN)
                dh_ref[rows, pl.ds(hd * Dh, Dh)] = _unrot(d_qr, cs, sn).astype(BF16)
                dh_ref[rows, pl.ds(R + hd * Dh, Dh)] = (_unrot(d_kr, cs, sn) * scale).astype(BF16)
                dh_ref[rows, pl.ds(2 * R + hd * Dh, Dh)] = d_v.astype(BF16)
                dh_ref[rows, pl.ds(3 * R + hd * Dh, Dh)] = dg.astype(BF16)

    rb = lambda i: nb - 1 - i
    return pl.pallas_call(
        body, name=name, grid=(nb,),
        in_specs=[pl.BlockSpec((ts, 4 * R), lambda i: (rb(i), 0)),
                  pl.BlockSpec((ts, Dh // 2), lambda i: (rb(i), 0)), pl.BlockSpec((ts, Dh // 2), lambda i: (rb(i), 0)),
                  pl.BlockSpec((1, R), lambda i: (0, 0)),
                  pl.BlockSpec((ch, H, Dh, Dh), lambda i: (rb(i), 0, 0, 0)),
                  pl.BlockSpec((ts, R), lambda i: (rb(i), 0)),
                  pl.BlockSpec((ts, R), lambda i: (rb(i), 0))],
        out_specs=[pl.BlockSpec((ts, 4 * R), lambda i: (rb(i), 0)), pl.BlockSpec((1, R), lambda i: (0, 0))],
        out_shape=[jax.ShapeDtypeStruct((S, 6 * R), BF16), jax.ShapeDtypeStruct((1, R), F32)],
        scratch_shapes=[pltpu.VMEM((H, Dh, Dh), F32)],
        compiler_params=_cparams(("arbitrary",)),
    )(h, cos, sin, ret_g, states, ret_raw, dmix)


GELU_C = math.sqrt(2.0 / math.pi)
GELU_A = 0.044715


def _gelu_parts(y):
    t = jnp.tanh(GELU_C * (y + GELU_A * y * y * y))
    val = 0.5 * y * (1.0 + t)
    grad = 0.5 * (1.0 + t) + 0.5 * y * (1.0 - t * t) * GELU_C * (1.0 + 3.0 * GELU_A * y * y)
    return val, grad


def _neg_expm1(x):
    series = -x * (1.0 + x * (1.0 / 2.0) * (1.0 + x * (1.0 / 3.0) * (1.0 + x * (1.0 / 4.0) * (
        1.0 + x * (1.0 / 5.0) * (1.0 + x * (1.0 / 6.0) * (1.0 + x * (1.0 / 7.0)))))))
    return jnp.where(x > -0.35, series, 1.0 - jnp.exp(x))


def _log_sigmoid(x):
    return jnp.minimum(x, 0.0) - jnp.log1p(jnp.exp(-jnp.abs(x)))


def _lru_gates(uc, wa_ref, ba_ref, wx_ref, bx_ref):
    nbk = wa_ref.shape[0]
    bd = wa_ref.shape[1]
    rs, gs = [], []
    for n in range(nbk):
        ucn = uc[:, n * bd:(n + 1) * bd]
        rs.append(_sigmoid(_bdot(ucn, wa_ref[n], NN) + ba_ref[:, pl.ds(n * bd, bd)]))
        gs.append(_sigmoid(_bdot(ucn, wx_ref[n], NN) + bx_ref[:, pl.ds(n * bd, bd)]))
    return jnp.concatenate(rs, axis=-1), jnp.concatenate(gs, axis=-1)


def _lru_fwd(h, conv_w, conv_b, wa, ba, wx, bx, lam, *, name, ts=256):
    S = h.shape[0]
    W = lam.shape[1]
    K = conv_w.shape[0]
    ts = min(ts, S)

    def body(u_ref, y_ref, cw_ref, cb_ref, wa_ref, ba_ref, wx_ref, bx_ref, lam_ref, hl_ref, mix_ref, tail, hlast):
        i = pl.program_id(0)

        @pl.when(i == 0)
        def _():
            tail[...] = jnp.zeros_like(tail)
            hlast[...] = jnp.zeros_like(hlast)

        u = u_ref[...]
        tl = tail[...]
        uc = cb_ref[...] + cw_ref[K - 1:K, :] * u
        for k in range(K - 1):
            uc = uc + cw_ref[k:k + 1, :] * _shift_down(u, tl, K - 1 - k)
        tail[...] = u[ts - SUB:ts]
        r, ig = _lru_gates(uc, wa_ref, ba_ref, wx_ref, bx_ref)
        log_a = LRU_C * r * _log_sigmoid(lam_ref[...])
        a = jnp.exp(log_a)
        b = jnp.sqrt(_neg_expm1(2.0 * log_a)) * (ig * uc)
        rid = _row_iota((ts, W))
        d = 1
        while d < ts:
            a_s = jnp.where(rid < d, 1.0, pltpu.roll(a, d, 0))
            b_s = jnp.where(rid < d, 0.0, pltpu.roll(b, d, 0))
            b = a * b_s + b
            a = a * a_s
            d *= 2
        hcur = a * hlast[SUB - 1:SUB, :] + b
        hlast[...] = hcur[ts - SUB:ts]
        hl_ref[...] = hcur
        gy, _ = _gelu_parts(y_ref[...])
        mix_ref[...] = (hcur * gy).astype(BF16)

    full = lambda shape: pl.BlockSpec(shape, lambda i: tuple(0 for _ in shape))
    return pl.pallas_call(
        body, name=name, grid=(S // ts,),
        in_specs=[pl.BlockSpec((ts, W), lambda i: (i, 4)), pl.BlockSpec((ts, W), lambda i: (i, 5)),
                  full(conv_w.shape), full(conv_b.shape), full(wa.shape), full(ba.shape), full(wx.shape),
                  full(bx.shape), full(lam.shape)],
        out_specs=[pl.BlockSpec((ts, W), lambda i: (i, 0)), pl.BlockSpec((ts, W), lambda i: (i, 1))],
        out_shape=[jax.ShapeDtypeStruct((S, W), F32), jax.ShapeDtypeStruct((S, 2 * W), BF16)],
        scratch_shapes=[pltpu.VMEM((SUB, W), F32), pltpu.VMEM((SUB, W), F32)],
        compiler_params=_cparams(("arbitrary",)),
    )(h, h, conv_w, conv_b, wa, ba, wx, bx, lam)


def _lru_bwd(h, hl, dmix, dh, conv_w, conv_b, wa, ba, wx, bx, lam, *, name, ts=256):
    S = h.shape[0]
    W = lam.shape[1]
    K = conv_w.shape[0]
    nbk, bd = wa.shape[0], wa.shape[1]
    ts = min(ts, S)
    nb = S // ts
    t8 = ts // SUB

    def body(u_ref, y_ref, uh_ref, hl_ref, hh_ref, dm_ref, cw_ref, cb_ref, wa_ref, ba_ref, wx_ref, bx_ref, lam_ref,
             dh_in, dh_ref, dcw_ref, dcb_ref, dwa_ref, dba_ref, dwx_ref, dbx_ref, dlam_ref, carry, head):
        i = pl.program_id(0)
        blk = nb - 1 - i

        @pl.when(i == 0)
        def _():
            carry[...] = jnp.zeros_like(carry)
            head[...] = jnp.zeros_like(head)
            for ref in (dcw_ref, dcb_ref, dwa_ref, dba_ref, dwx_ref, dbx_ref, dlam_ref):
                ref[...] = jnp.zeros_like(ref)

        inside = (blk > 0).astype(F32)
        u = u_ref[...]
        tl = uh_ref[...] * inside
        sh = [_shift_down(u, tl, K - 1 - k) for k in range(K)]
        uc = cb_ref[...]
        for k in range(K):
            uc = uc + cw_ref[k:k + 1, :] * sh[k]
        r, ig = _lru_gates(uc, wa_ref, ba_ref, wx_ref, bx_ref)
        lam_v = lam_ref[...]
        ls = _log_sigmoid(lam_v)
        log_a = LRU_C * r * ls
        a = jnp.exp(log_a)
        mult = jnp.sqrt(_neg_expm1(2.0 * log_a))
        hcur = hl_ref[...]
        hprev = _shift_down(hcur, hh_ref[...] * inside, 1)
        gy, dgy = _gelu_parts(y_ref[...])
        dm = dm_ref[...]
        d_y = dm * hcur * dgy
        rid = _row_iota((ts, W))
        bq = dm * gy + jnp.where(rid == ts - 1, carry[0:1, :], 0.0)
        aq = jnp.where(rid == ts - 1, 0.0, pltpu.roll(a, ts - 1, 0))
        d = 1
        while d < ts:
            a_s = jnp.where(rid >= ts - d, 0.0, pltpu.roll(aq, ts - d, 0))
            b_s = jnp.where(rid >= ts - d, 0.0, pltpu.roll(bq, ts - d, 0))
            bq = bq + aq * b_s
            aq = aq * a_s
            d *= 2
        lam_t = bq
        carry[...] = (a * lam_t)[0:SUB]
        d_a = lam_t * hprev
        d_mult = lam_t * (ig * uc)
        d_i = lam_t * mult * uc
        d_uc = lam_t * mult * ig
        d_log_a = d_a * a - d_mult * (a * a) / mult
        d_r = d_log_a * (LRU_C * ls)
        dlam_ref[...] += jnp.sum(d_log_a * (LRU_C * r), axis=0, keepdims=True) * _sigmoid(-lam_v)
        d_pr = d_r * r * (1.0 - r)
        d_pi = d_i * ig * (1.0 - ig)
        dba_ref[...] += jnp.sum(d_pr, axis=0, keepdims=True)
        dbx_ref[...] += jnp.sum(d_pi, axis=0, keepdims=True)
        extra = []
        for n in range(nbk):
            sl = slice(n * bd, (n + 1) * bd)
            ucn = uc[:, sl]
            dwa_ref[n] += _bdot(ucn, d_pr[:, sl], TN)
            dwx_ref[n] += _bdot(ucn, d_pi[:, sl], TN)
            extra.append(_bdot(d_pr[:, sl], wa_ref[n], NT) + _bdot(d_pi[:, sl], wx_ref[n], NT))
        d_uc = d_uc + jnp.concatenate(extra, axis=-1)
        dcb_ref[...] += jnp.sum(d_uc, axis=0, keepdims=True)
        rid8 = _row_iota((SUB, W))
        dcw = jnp.zeros((SUB, W), F32)
        for k in range(K):
            dcw = dcw + jnp.where(rid8 == k, jnp.sum(d_uc * sh[k], axis=0, keepdims=True), 0.0)
        dcw_ref[...] += dcw
        hd = head[...]
        d_u = cw_ref[K - 1:K, :] * d_uc
        for j in range(1, K):
            d_u = d_u + cw_ref[K - 1 - j:K - j, :] * _shift_up(d_uc, hd, j)
        head[...] = d_uc[0:SUB]
        dh_ref[:, 0:W] = d_u.astype(BF16)
        dh_ref[:, W:2 * W] = d_y.astype(BF16)

    rb = lambda i: nb - 1 - i
    prev8 = lambda i: jnp.maximum(rb(i) * t8 - 1, 0)
    full = lambda shape: pl.BlockSpec(shape, lambda i: tuple(0 for _ in shape))
    small = [jax.ShapeDtypeStruct((SUB, W), F32), jax.ShapeDtypeStruct((1, W), F32),
             jax.ShapeDtypeStruct(wa.shape, F32), jax.ShapeDtypeStruct((1, W), F32),
             jax.ShapeDtypeStruct(wx.shape, F32), jax.ShapeDtypeStruct((1, W), F32),
             jax.ShapeDtypeStruct((1, W), F32)]
    return pl.pallas_call(
        body, name=name, grid=(nb,),
        in_specs=[pl.BlockSpec((ts, W), lambda i: (rb(i), 4)), pl.BlockSpec((ts, W), lambda i: (rb(i), 5)),
                  pl.BlockSpec((SUB, W), lambda i: (prev8(i), 4)),
                  pl.BlockSpec((ts, W), lambda i: (rb(i), 0)), pl.BlockSpec((SUB, W), lambda i: (prev8(i), 0)),
                  pl.BlockSpec((ts, W), lambda i: (rb(i), 1)),
                  full(conv_w.shape), full(conv_b.shape), full(wa.shape), full(ba.shape), full(wx.shape),
                  full(bx.shape), full(lam.shape), pl.BlockSpec(memory_space=pl.ANY)],
        out_specs=[pl.BlockSpec((ts, 2 * W), lambda i: (rb(i), 2))] + [full(s.shape) for s in small],
        out_shape=[jax.ShapeDtypeStruct(dh.shape, dh.dtype)] + small,
        scratch_shapes=[pltpu.VMEM((SUB, W), F32), pltpu.VMEM((SUB, W), F32)],
        input_output_aliases={13: 0},
        compiler_params=_cparams(("arbitrary",)),
    )(h, h, h, hl, hl, dmix, conv_w, conv_b, wa, ba, wx, bx, lam, dh)


def _xattn_fwd(q, km, vm, *, name, ts=512):
    S, D = q.shape
    M = km.shape[0]
    H = XA_HEADS
    Dh = D // H
    ts = min(ts, S)
    scale = Dh ** -0.5

    def body(q_ref, k_ref, v_ref, o_ref):
        for hd in range(H):
            cols = pl.ds(hd * Dh, Dh)
            s = _bdot(q_ref[:, cols], k_ref[:, cols], NT) * scale
            s = s - jnp.max(s, axis=-1, keepdims=True)
            e = jnp.exp(s)
            p = e / jnp.sum(e, axis=-1, keepdims=True)
            o_ref[:, cols] = _bdot(p, v_ref[:, cols], NN).astype(o_ref.dtype)

    return pl.pallas_call(
        body, name=name, grid=(S // ts,),
        in_specs=[pl.BlockSpec((ts, D), lambda i: (i, 0)), pl.BlockSpec((M, D), lambda i: (0, 0)),
                  pl.BlockSpec((M, D), lambda i: (0, 0))],
        out_specs=pl.BlockSpec((ts, D), lambda i: (i, 0)),
        out_shape=jax.ShapeDtypeStruct((S, D), BF16),
        compiler_params=_cparams(("parallel",)),
    )(q, km, vm)


def _xattn_bwd(q, km, vm, d_o, *, name, ts=512):
    S, D = q.shape
    M = km.shape[0]
    H = XA_HEADS
    Dh = D // H
    ts = min(ts, S)
    scale = Dh ** -0.5

    def body(q_ref, k_ref, v_ref, do_ref, dq_ref, dk_ref, dv_ref):
        i = pl.program_id(0)

        @pl.when(i == 0)
        def _():
            dk_ref[...] = jnp.zeros_like(dk_ref)
            dv_ref[...] = jnp.zeros_like(dv_ref)

        for hd in range(H):
            cols = pl.ds(hd * Dh, Dh)
            qh, kh, vh, doh = q_ref[:, cols], k_ref[:, cols], v_ref[:, cols], do_ref[:, cols]
            s = _bdot(qh, kh, NT) * scale
            s = s - jnp.max(s, axis=-1, keepdims=True)
            e = jnp.exp(s)
            p = e / jnp.sum(e, axis=-1, keepdims=True)
            dp = _bdot(doh, vh, NT)
            ds = p * (dp - jnp.sum(dp * p, axis=-1, keepdims=True)) * scale
            dq_ref[:, cols] = _bdot(ds, kh, NN).astype(dq_ref.dtype)
            dk_ref[:, cols] += _bdot(ds, qh, TN)
            dv_ref[:, cols] += _bdot(p, doh, TN)

    row = pl.BlockSpec((ts, D), lambda i: (i, 0))
    mem = pl.BlockSpec((M, D), lambda i: (0, 0))
    return pl.pallas_call(
        body, name=name, grid=(S // ts,), in_specs=[row, mem, mem, row], out_specs=[row, mem, mem],
        out_shape=[jax.ShapeDtypeStruct((S, D), BF16), jax.ShapeDtypeStruct((M, D), F32),
                   jax.ShapeDtypeStruct((M, D), F32)],
        compiler_params=_cparams(("arbitrary",)),
    )(q, km, vm, d_o)


def _conv_rows(v, tail, cw_ref, cb_ref):
    K = cw_ref.shape[0]
    sh = [_shift_down(v, tail, K - 1 - k) for k in range(K)]
    out = cb_ref[...]
    for k in range(K):
        out = out + cw_ref[k:k + 1, :] * sh[k]
    return out, sh


def _ffn_gate_fwd(hh, cw, cb, *, name, ts=512, tc=512):
    S, F2 = hh.shape
    F = F2 // 2
    ts, tc = min(ts, S), min(tc, F)
    nj = F // tc
    K = cw.shape[0]

    def body(a_ref, b_ref, cwa_ref, cwb_ref, cba_ref, cbb_ref, o_ref, ta, tb):
        i = pl.program_id(1)

        @pl.when(i == 0)
        def _():
            ta[...] = jnp.zeros_like(ta)
            tb[...] = jnp.zeros_like(tb)

        av, bv = a_ref[...], b_ref[...]
        ac, _ = _conv_rows(av, ta[...], cwa_ref, cba_ref)
        bc, _ = _conv_rows(bv, tb[...], cwb_ref, cbb_ref)
        ta[...] = av[ts - SUB:ts]
        tb[...] = bv[ts - SUB:ts]
        o_ref[...] = (ac * _sigmoid(ac) * bc).astype(o_ref.dtype)

    return pl.pallas_call(
        body, name=name, grid=(nj, S // ts),
        in_specs=[pl.BlockSpec((ts, tc), lambda j, i: (i, j)), pl.BlockSpec((ts, tc), lambda j, i: (i, j + nj)),
                  pl.BlockSpec((K, tc), lambda j, i: (0, j)), pl.BlockSpec((K, tc), lambda j, i: (0, j + nj)),
                  pl.BlockSpec((1, tc), lambda j, i: (0, j)), pl.BlockSpec((1, tc), lambda j, i: (0, j + nj))],
        out_specs=pl.BlockSpec((ts, tc), lambda j, i: (i, j)),
        out_shape=jax.ShapeDtypeStruct((S, F), BF16),
        scratch_shapes=[pltpu.VMEM((SUB, tc), F32), pltpu.VMEM((SUB, tc), F32)],
        compiler_params=_cparams(("parallel", "arbitrary")),
    )(hh, hh, cw, cw, cb, cb)


def _ffn_gate_bwd(hh, dact, cw, cb, *, name, ts=512, tc=512):
    S, F2 = hh.shape
    F = F2 // 2
    ts, tc = min(ts, S), min(tc, F)
    nj = F // tc
    nb = S // ts
    t8 = ts // SUB
    K = cw.shape[0]

    def body(a_ref, ah_ref, b_ref, bh_ref, d_ref, cwa_ref, cwb_ref, cba_ref, cbb_ref,
             da_ref, db_ref, ga_ref, gb_ref, ha, hb):
        i = pl.program_id(1)
        blk = nb - 1 - i

        @pl.when(i == 0)
        def _():
            for ref in (ha, hb, ga_ref, gb_ref):
                ref[...] = jnp.zeros_like(ref)

        inside = (blk > 0).astype(F32)
        ac, sha = _conv_rows(a_ref[...], ah_ref[...] * inside, cwa_ref, cba_ref)
        bc, shb = _conv_rows(b_ref[...], bh_ref[...] * inside, cwb_ref, cbb_ref)
        dv = d_ref[...].astype(F32)
        sg = _sigmoid(ac)
        d_bc = dv * ac * sg
        d_ac = dv * bc * sg * (1.0 + ac * (1.0 - sg))
        rid8 = _row_iota((SUB, tc))
        for d_c, sh, cw_ref, head, o_ref, g_ref in ((d_ac, sha, cwa_ref, ha, da_ref, ga_ref),
                                                     (d_bc, shb, cwb_ref, hb, db_ref, gb_ref)):
            hd = head[...]
            d_in = cw_ref[K - 1:K, :] * d_c
            for j in range(1, K):
                d_in = d_in + cw_ref[K - 1 - j:K - j, :] * _shift_up(d_c, hd, j)
            head[...] = d_c[0:SUB]
            o_ref[...] = d_in.astype(o_ref.dtype)
            gsum = jnp.where(rid8 == K, jnp.sum(d_c, axis=0, keepdims=True), 0.0)
            for k in range(K):
                gsum = gsum + jnp.where(rid8 == k, jnp.sum(d_c * sh[k], axis=0, keepdims=True), 0.0)
            g_ref[...] += gsum

    rb = lambda i: nb - 1 - i
    prev8 = lambda i: jnp.maximum(rb(i) * t8 - 1, 0)
    outs = pl.pallas_call(
        body, name=name, grid=(nj, nb),
        in_specs=[pl.BlockSpec((ts, tc), lambda j, i: (rb(i), j)), pl.BlockSpec((SUB, tc), lambda j, i: (prev8(i), j)),
                  pl.BlockSpec((ts, tc), lambda j, i: (rb(i), j + nj)),
                  pl.BlockSpec((SUB, tc), lambda j, i: (prev8(i), j + nj)),
                  pl.BlockSpec((ts, tc), lambda j, i: (rb(i), j)),
                  pl.BlockSpec((K, tc), lambda j, i: (0, j)), pl.BlockSpec((K, tc), lambda j, i: (0, j + nj)),
                  pl.BlockSpec((1, tc), lambda j, i: (0, j)), pl.BlockSpec((1, tc), lambda j, i: (0, j + nj))],
        out_specs=[pl.BlockSpec((ts, tc), lambda j, i: (rb(i), j)), pl.BlockSpec((ts, tc), lambda j, i: (rb(i), j)),
                   pl.BlockSpec((SUB, tc), lambda j, i: (0, j)), pl.BlockSpec((SUB, tc), lambda j, i: (0, j))],
        out_shape=[jax.ShapeDtypeStruct((S, F), BF16), jax.ShapeDtypeStruct((S, F), BF16),
                   jax.ShapeDtypeStruct((SUB, F), F32), jax.ShapeDtypeStruct((SUB, F), F32)],
        scratch_shapes=[pltpu.VMEM((SUB, tc), F32), pltpu.VMEM((SUB, tc), F32)],
        compiler_params=_cparams(("parallel", "arbitrary")),
    )(hh, hh, hh, hh, dact, cw, cw, cb, cb)
    return outs


ADAM_BLOCK_ELEMS = 128 * 1024


def _adamw(w, m, v, parts, *, name):
    R, C = w.shape
    n = parts.shape[0]
    tr = R
    for cand in (1024, 512, 256, 128, 64, 32, 16):
        if R % cand == 0 and cand * C <= ADAM_BLOCK_ELEMS:
            tr = cand
            break
    c1 = 1.0 - ADAM_B1 ** ADAM_STEP
    c2 = 1.0 - ADAM_B2 ** ADAM_STEP

    def body(w_ref, m_ref, v_ref, p_ref, g_ref, d_ref, nm_ref, nv_ref):
        g = p_ref[0].astype(F32)
        for k in range(1, n):
            g = g + p_ref[k].astype(F32)
        m_new = ADAM_B1 * m_ref[...] + (1.0 - ADAM_B1) * g
        v_new = ADAM_B2 * v_ref[...] + (1.0 - ADAM_B2) * (g * g)
        m_hat = m_new / c1
        v_hat = v_new / c2
        g_ref[...] = g
        d_ref[...] = -ADAM_LR * (m_hat / (jnp.sqrt(v_hat) + ADAM_EPS) + ADAM_WD * w_ref[...])
        nm_ref[...] = m_new
        nv_ref[...] = v_new

    blk = pl.BlockSpec((tr, C), lambda i: (i, 0))
    sds = jax.ShapeDtypeStruct((R, C), F32)
    return pl.pallas_call(
        body, name=name, grid=(R // tr,),
        in_specs=[blk, blk, blk, pl.BlockSpec((n, tr, C), lambda i: (0, i, 0))],
        out_specs=[blk, blk, blk, blk], out_shape=[sds, sds, sds, sds],
        compiler_params=_cparams(("parallel",)),
    )(w, m, v, parts)


def _mesh_place():
    x, y, c = lax.axis_index("x"), lax.axis_index("y"), lax.axis_index("c")
    others = [(1 - x, y), (x, 1 - y), (1 - x, 1 - y)]
    return x, y, c, others


HBM_SPEC = pl.BlockSpec(memory_space=pltpu.HBM)
SEM_SPEC = pl.BlockSpec(memory_space=pltpu.SEMAPHORE)
ANY_SPEC = pl.BlockSpec(memory_space=pl.ANY)
EFFECT = pltpu.SideEffectType.DATAFLOW_SIDE_EFFECTING


def _in_hbm(a):
    return pltpu.with_memory_space_constraint(a, pltpu.HBM)


def _split_start(srcs, lands, copies, n_cp, *, name):
    n_s, n_l = len(srcs), len(lands)

    def body(*refs):
        src_refs, land_refs = refs[:n_s], refs[n_s:n_s + n_l]
        ssem, rsem = refs[n_s + n_l], refs[n_s + n_l + 1]
        token = refs[-1]
        for outgoing, _ in copies(src_refs, land_refs, ssem, rsem):
            outgoing.start()
        token[...] = jnp.zeros_like(token)

    outs = pl.pallas_call(
        body, name=name,
        out_shape=(pltpu.SemaphoreType.DMA((n_cp,)), pltpu.SemaphoreType.DMA((n_cp,)),
                   *[pltpu.HBM(a.shape, a.dtype) for a in srcs], *[pltpu.HBM(a.shape, a.dtype) for a in lands],
                   jax.ShapeDtypeStruct((SUB, LANE), F32)),
        in_specs=[HBM_SPEC] * (n_s + n_l),
        out_specs=(SEM_SPEC, SEM_SPEC, *[HBM_SPEC] * (n_s + n_l), pl.BlockSpec(memory_space=pltpu.VMEM)),
        input_output_aliases={i: 2 + i for i in range(n_s + n_l)},
        compiler_params=pltpu.CompilerParams(has_side_effects=EFFECT),
    )(*[_in_hbm(a) for a in srcs], *[_in_hbm(a) for a in lands])
    ssem, rsem = outs[0], outs[1]
    return ssem, rsem, list(outs[2:2 + n_s]), list(outs[2 + n_s:2 + n_s + n_l]), outs[-1]


def _split_wait(srcs, lands, ssem, rsem, after, copies, *, name):
    n_s, n_l = len(srcs), len(lands)

    def body(*refs):
        src_refs, land_refs = refs[:n_s], refs[n_s:n_s + n_l]
        s_ref, r_ref = refs[n_s + n_l], refs[n_s + n_l + 1]
        for outgoing, incoming in copies(src_refs, land_refs, s_ref, r_ref):
            outgoing.wait_send()
            incoming.wait_recv()

    outs = pl.pallas_call(
        body, name=name,
        out_shape=(*[pltpu.HBM(a.shape, a.dtype) for a in srcs], *[pltpu.HBM(a.shape, a.dtype) for a in lands]),
        in_specs=[HBM_SPEC] * (n_s + n_l) + [SEM_SPEC, SEM_SPEC, ANY_SPEC], out_specs=[HBM_SPEC] * (n_s + n_l),
        input_output_aliases={i: i for i in range(n_s + n_l)},
        compiler_params=pltpu.CompilerParams(has_side_effects=EFFECT),
    )(*srcs, *lands, ssem, rsem, after)
    return list(outs[:n_s]), list(outs[n_s:])


class _WeightGather:
    def __init__(self, shards, axes, splits, tag):
        self.shards, self.axes, self.splits, self.tag = list(shards), axes, splits, tag
        self.n = len(shards)
        self.full_shapes = [(s.shape[0] * N_CHIP, s.shape[1]) if ax == 0 else (s.shape[0], s.shape[1] * N_CHIP)
                            for s, ax in zip(shards, axes)]

    def _region(self, land_refs, it, chip, half):
        r, w = self.shards[it].shape
        by_rows = self.axes[it] == 0
        if self.splits[it] and half is not None:
            rows = pl.ds(pl.multiple_of(half * (r // 2) + (chip * r if by_rows else 0), 16), r // 2)
        else:
            rows = pl.ds(chip * r if by_rows else 0, r)
        cols = pl.ds(0, w) if by_rows else pl.ds(pl.multiple_of(chip * w, LANE), w)
        return land_refs[it].at[rows, cols]

    def _src_half(self, src_refs, it, half):
        r = self.shards[it].shape[0]
        if self.splits[it]:
            return src_refs[it].at[pl.ds(pl.multiple_of(half * (r // 2), 16), r // 2), :]
        return src_refs[it]

    def _ici(self, src_refs, land_refs, ssem, rsem):
        x, y, c, others = _mesh_place()
        pairs = []
        for it in range(self.n):
            for j, chip in enumerate(others):
                def mk(chip_from, it=it, j=j, chip=chip):
                    return pltpu.make_async_remote_copy(
                        src_ref=self._src_half(src_refs, it, c), dst_ref=self._region(land_refs, it, chip_from, c),
                        send_sem=ssem.at[3 * it + j], recv_sem=rsem.at[3 * it + j], device_id=(*chip, c),
                        device_id_type=MESH)
                pairs.append((mk(2 * x + y), mk(2 * chip[0] + chip[1])))
        return pairs

    def start(self):
        lands = [lax.empty(s, a.dtype) for s, a in zip(self.full_shapes, self.shards)]
        self.ssem, self.rsem, self.srcs, self.lands, token = _split_start(
            self.shards, lands, self._ici, 3 * self.n, name="gather_start_" + self.tag)
        return token

    def finish(self, after):
        srcs, lands = _split_wait(self.srcs, self.lands, self.ssem, self.rsem, after, self._ici,
                                  name="gather_wait_" + self.tag)
        n = self.n

        def body(*refs):
            src_refs, out_refs, lsem = refs[:n], refs[2 * n:3 * n], refs[3 * n]
            chip = 2 * lax.axis_index("x") + lax.axis_index("y")
            local = [pltpu.make_async_copy(src_refs[it], self._region(out_refs, it, chip, None), lsem.at[it])
                     for it in range(n)]
            for cp in local:
                cp.start()
            for cp in local:
                cp.wait()

        fulls = pl.pallas_call(
            body, name="gather_own_" + self.tag, in_specs=[ANY_SPEC] * (2 * n), out_specs=[ANY_SPEC] * n,
            out_shape=[jax.ShapeDtypeStruct(a.shape, a.dtype) for a in lands],
            scratch_shapes=[pltpu.SemaphoreType.DMA((n,))],
            input_output_aliases={n + i: i for i in range(n)},
        )(*srcs, *lands)
        return list(fulls)


class _GradGather:
    def __init__(self, grads, axes, tag):
        self.grads, self.axes, self.tag = list(grads), axes, tag
        self.n = len(grads)
        self.shard_shapes = [(g.shape[0] // N_CHIP, g.shape[1]) if ax == 0 else (g.shape[0], g.shape[1] // N_CHIP)
                             for g, ax in zip(grads, axes)]

    def _piece(self, src_refs, it, chip):
        r, w = self.shard_shapes[it]
        if self.axes[it] == 0:
            return src_refs[it].at[pl.ds(pl.multiple_of(chip * r, 16), r), :]
        return src_refs[it].at[:, pl.ds(pl.multiple_of(chip * w, LANE), w)]

    PER_ITEM = 7

    def _remote(self, src_refs, land_refs, ssem, rsem):
        x, y, c, others = _mesh_place()
        me = 4 * x + 2 * y + c
        pairs = []
        for it in range(self.n):
            def mk(k, piece_chip, slot, to, it=it):
                return pltpu.make_async_remote_copy(
                    src_ref=self._piece(src_refs, it, piece_chip), dst_ref=land_refs[it].at[slot],
                    send_sem=ssem.at[self.PER_ITEM * it + k], recv_sem=rsem.at[self.PER_ITEM * it + k], device_id=to,
                    device_id_type=MESH)
            for j, chip in enumerate(others):
                for r in range(2):
                    core = c if r == 0 else 1 - c
                    peer = (*chip, core)
                    chip_id = 2 * chip[0] + chip[1]
                    pairs.append((mk(2 * j + r, chip_id, me, peer), mk(2 * j + r, chip_id, 2 * chip_id + core, peer)))
            sibling = (x, y, 1 - c)
            pairs.append((mk(6, 2 * x + y, me, sibling), mk(6, 2 * x + y, 4 * x + 2 * y + 1 - c, sibling)))
        return pairs

    def start(self):
        lands = [lax.empty((N_DEV, *s), g.dtype) for s, g in zip(self.shard_shapes, self.grads)]
        self.ssem, self.rsem, self.srcs, self.lands, token = _split_start(
            self.grads, lands, self._remote, self.PER_ITEM * self.n, name="grads_start_" + self.tag)
        return token

    def finish(self, after):
        srcs, lands = _split_wait(self.srcs, self.lands, self.ssem, self.rsem, after, self._remote,
                                  name="grads_wait_" + self.tag)
        n = self.n

        def body(*refs):
            src_refs, out_refs, lsem = refs[:n], refs[2 * n:3 * n], refs[3 * n]
            x, y, c = lax.axis_index("x"), lax.axis_index("y"), lax.axis_index("c")
            local = [pltpu.make_async_copy(self._piece(src_refs, it, 2 * x + y), out_refs[it].at[4 * x + 2 * y + c],
                                           lsem.at[it]) for it in range(n)]
            for cp in local:
                cp.start()
            for cp in local:
                cp.wait()

        outs = pl.pallas_call(
            body, name="grads_own_" + self.tag, in_specs=[ANY_SPEC] * (2 * n), out_specs=[ANY_SPEC] * n,
            out_shape=[jax.ShapeDtypeStruct(a.shape, a.dtype) for a in lands],
            scratch_shapes=[pltpu.SemaphoreType.DMA((n,))],
            input_output_aliases={n + i: i for i in range(n)},
        )(*srcs, *lands)
        return list(outs)


def _allreduce_small(vec, *, name):
    R, L = vec.shape

    def body(v_ref, o_ref, buf, send, recv, lsem):
        x, y, c, others = _mesh_place()
        me = 4 * x + 2 * y + c
        sibling = (x, y, 1 - c)

        def copy(k, slot, to, src=None):
            return pltpu.make_async_remote_copy(
                src_ref=buf.at[slot] if src is None else src, dst_ref=buf.at[slot], send_sem=send.at[k],
                recv_sem=recv.at[k], device_id=to, device_id_type=MESH)

        def slot_of(chip, core):
            return 4 * chip[0] + 2 * chip[1] + core

        mine = pltpu.make_async_copy(v_ref, buf.at[me], lsem)
        mine.start()
        first = [copy(0, me, sibling, src=v_ref)]
        first += [copy(1 + j, me, (*chip, c), src=v_ref) for j, chip in enumerate(others)]
        for cp in first:
            cp.start()
        passed = [copy(4 + j, slot_of(chip, c), sibling) for j, chip in enumerate(others)]
        for j, chip in enumerate(others):
            copy(1 + j, slot_of(chip, c), (*chip, c)).wait_recv()
            passed[j].start()
        copy(0, slot_of((x, y), 1 - c), sibling).wait_recv()
        for j, chip in enumerate(others):
            copy(4 + j, slot_of(chip, 1 - c), sibling).wait_recv()
        for cp in first + passed:
            cp.wait_send()
        mine.wait()
        total = buf[0]
        for k in range(1, N_DEV):
            total = total + buf[k]
        o_ref[...] = total

    return pl.pallas_call(
        body, name=name, in_specs=[pl.BlockSpec(memory_space=pltpu.VMEM)],
        out_specs=pl.BlockSpec(memory_space=pltpu.VMEM), out_shape=jax.ShapeDtypeStruct((R, L), F32),
        scratch_shapes=[pltpu.VMEM((N_DEV, R, L), F32), pltpu.SemaphoreType.DMA((7,)), pltpu.SemaphoreType.DMA((7,)),
                        pltpu.SemaphoreType.DMA],
        compiler_params=pltpu.CompilerParams(vmem_limit_bytes=VMEM_LIMIT),
    )(vec)


PACK_ALIGN = 1024


def _pack(arrs, row_multiple):
    flat = []
    for a in arrs:
        f = a.reshape(-1).astype(F32)
        flat.append(jnp.pad(f, (0, (-f.shape[0]) % PACK_ALIGN)))
    v = jnp.concatenate(flat)
    v = jnp.pad(v, (0, (-v.shape[0]) % (LANE * row_multiple)))
    return v.reshape(-1, LANE)


def _unpack(v, shapes):
    flat = v.reshape(-1)
    out, off = [], 0
    for s in shapes:
        size = math.prod(s)
        out.append(flat[off:off + size].reshape(s))
        off += size + (-size) % PACK_ALIGN
    return out


def _tile(dim, target):
    for cand in (1024, 512, 256, 128):
        if cand <= target and dim % cand == 0:
            return cand
    return dim


def _div_tile(dim, cap, mult=LANE):
    best = None
    for cand in range(mult, min(cap, dim) + 1, mult):
        if dim % cand == 0:
            best = cand
    return dim if best is None else best


WEIGHT_NAMES = ('norm1_g', 'w_in', 'ret_g', 'rg_conv_w', 'rg_conv_b', 'rg_wa', 'rg_ba', 'rg_wx', 'rg_bx', 'rg_lambda',
                'w_out', 'norm2_g', 'norm_mem_g', 'xa_wq', 'xa_wk', 'xa_wv', 'xa_wo', 'norm3_g', 'ffn_w_up',
                'ffn_conv_w', 'ffn_conv_b', 'ffn_w_down', 'final_g')
BIG_AXIS = {'w_in': 1, 'w_out': 0, 'xa_wq': 0, 'xa_wk': 0, 'xa_wv': 0, 'xa_wo': 0, 'ffn_w_up': 1, 'ffn_w_down': 0}
SMALL_SHARDED = ('rg_conv_w', 'ffn_conv_w')


def _step(x, mem, positions, loss_target, W, Mo, Vo):
    S, D = x.shape[1], x.shape[2]
    xs, mems, tgt = x[0], mem[0], loss_target[0]
    n_mem = mems.shape[0]
    pos_col = positions.reshape(S, 1)
    chip = 2 * lax.axis_index("x") + lax.axis_index("y")

    big = list(BIG_AXIS)
    shards = {n: W[n][0] for n in big}
    G = {}
    gather_groups = (('w_in', 'rg_conv_w'), ('w_out', 'xa_wq', 'xa_wk', 'xa_wv', 'xa_wo'),
                     ('ffn_w_up', 'ffn_conv_w'), ('ffn_w_down',))
    gathers, tok = [], None
    for gi, names in enumerate(gather_groups):
        srcs = []
        for n in names:
            s = W[n][0] if tok is None else W[n][0] + tok[0, 0]
            srcs.append(s.astype(BF16) if n in BIG_AXIS else s)
        ag = _WeightGather(srcs, [BIG_AXIS.get(n, 1) for n in names], [False] * len(names), "g%d" % gi)
        tok = ag.start()
        gathers.append(ag)

    def finish_gather(gi, after):
        G.update(zip(gather_groups[gi], gathers[gi].finish(after)))

    finish_gather(0, tok)
    R = W['ret_g'].shape[1]
    Wl = W['rg_lambda'].shape[1]
    IN = W['w_in'].shape[2] * N_CHIP
    F2 = W['ffn_w_up'].shape[2] * N_CHIP
    F = F2 // 2

    norm1_g, norm2_g, norm3_g = W['norm1_g'] + tok[0, 0], W['norm2_g'], W['norm3_g']
    norm_mem_g, final_g, ret_g = W['norm_mem_g'], W['final_g'].reshape(1, D), W['ret_g']
    rg_cw, rg_cb = G['rg_conv_w'], W['rg_conv_b']
    wa, wx = W['rg_wa'][0], W['rg_wx'][0]
    ba, bx = W['rg_ba'].reshape(1, Wl), W['rg_bx'].reshape(1, Wl)
    lam = W['rg_lambda']
    ffn_cb = W['ffn_conv_b']

    def fwd_mm(a, wname, N, K, **kw):
        return _mm(a, G[wname], mode="nn", M=a.shape[0], N=N, K=K, tm=_tile(a.shape[0], 1024), tn=1024,
                   tk=_div_tile(K, 3072), **kw)

    def fwd_mm_norm(a, wname, res, g, name):
        return _mm(a, G[wname], mode="nn", M=a.shape[0], N=D, K=a.shape[1], tm=512, tn=D, tk=_div_tile(a.shape[1], 2048),
                   out_dtype=F32, res=res, norm_g=g, name=name)

    def bwd_x_mm(d, wname, N, K, **kw):
        return _mm(d, G[wname], mode="nt", M=d.shape[0], N=N, K=K, tm=_tile(d.shape[0], 1024),
                   tn=_div_tile(N, 1024, 256), tk=_div_tile(K, 3072), **kw)

    def bwd_w_mm(a, d, M, N, **kw):
        Ks = a.shape[0]
        return _mm(a, d, mode="tn", M=M, N=N, K=Ks, out_dtype=BF16, tm=_div_tile(M, 1024, 256),
                   tn=_div_tile(N, 1024, 256), tk=_div_tile(Ks, 2048 if d.dtype == BF16 else 1024), **kw)

    xn1 = _rmsnorm_fwd(xs, norm1_g, name="norm1_fwd")
    h = fwd_mm(xn1, 'w_in', IN, D, out_dtype=F32, name="mm_in")
    half = (R // RET_HEADS) // 2
    inv = (ROPE_BASE ** (-jnp.arange(half, dtype=F32) / half)).reshape(1, half)
    cos, sin = _rope_table(pos_col, inv, name="rope_table")
    hl, mix = _lru_fwd(h, rg_cw, rg_cb, wa, ba, wx, bx, lam, name="lru_fwd")
    ret_raw, states, mix = _ret_fwd(h, cos, sin, ret_g, mix, name="ret_fwd")
    finish_gather(1, mix)
    x1, xn2 = fwd_mm_norm(mix, 'w_out', xs, norm2_g, "mm_out")
    memn = _rmsnorm_fwd(mems, norm_mem_g, name="norm_mem_fwd")
    km = fwd_mm(memn, 'xa_wk', D, D, out_dtype=BF16, name="mm_k")
    vm = fwd_mm(memn, 'xa_wv', D, D, out_dtype=BF16, name="mm_v")
    q = fwd_mm(xn2, 'xa_wq', D, D, out_dtype=BF16, name="mm_q")
    o = _xattn_fwd(q, km, vm, name="xattn_fwd")
    x2, xn3 = fwd_mm_norm(o, 'xa_wo', x1, norm3_g, "mm_o")
    finish_gather(2, xn3)
    ffn_cw = G['ffn_conv_w']
    hh = fwd_mm(xn3, 'ffn_w_up', F2, D, out_dtype=F32, name="mm_up")
    act = _ffn_gate_fwd(hh, ffn_cw, ffn_cb, name="ffn_gate_fwd")
    finish_gather(3, act)
    x3 = fwd_mm(act, 'ffn_w_down', D, F, out_dtype=F32, res=x2, name="mm_down")
    dx3, d_final, loss8, dx3h = _final_loss(x3, tgt, final_g, name="final_loss")

    gw = {}
    grad_groups = []

    def start_grads(names, tag):
        gg = _GradGather([gw[n] for n in names], [BIG_AXIS[n] for n in names], tag)
        grad_groups.append((names, gg))
        return gg.start()

    dact = bwd_x_mm(dx3h, 'ffn_w_down', F, D, out_dtype=F32, name="mm_dact")
    gw['ffn_w_down'] = bwd_w_mm(act, dx3h, F, D, name="mm_dw_down")
    dhh_a, dhh_b, gcw_a, gcw_b = _ffn_gate_bwd(hh, dact, ffn_cw, ffn_cb, name="ffn_gate_bwd")
    gw_up = bwd_w_mm(xn3, dhh_a, D, F, out_full=(D, F2), name="mm_dw_up_a")
    gw['ffn_w_up'] = bwd_w_mm(xn3, dhh_b, D, F, out=gw_up, out_off=(0, F), name="mm_dw_up_b")
    tok_a = start_grads(('ffn_w_down', 'ffn_w_up'), "a")
    dxn3 = bwd_x_mm(dhh_a, 'ffn_w_up', D, F, out_dtype=F32, after=tok_a, name="mm_dxn3_a")
    dxn3 = bwd_x_mm(dhh_b, 'ffn_w_up', D, F, out_dtype=F32, b_off=(0, F), res=dxn3, name="mm_dxn3_b")
    dx2, d_norm3, dx2h = _rmsnorm_bwd(x2, dxn3, norm3_g, dx3, name="norm3_bwd", emit_bf16=True)
    Kc = ffn_cw.shape[0]
    d_ffn_cw = jnp.concatenate([gcw_a[:Kc], gcw_b[:Kc]], axis=1)
    d_ffn_cb = jnp.concatenate([gcw_a[Kc:Kc + 1], gcw_b[Kc:Kc + 1]], axis=1)

    d_o = bwd_x_mm(dx2h, 'xa_wo', D, D, out_dtype=BF16, name="mm_do")
    gw['xa_wo'] = bwd_w_mm(o, dx2h, D, D, name="mm_dw_o")
    dq, dk, dv = _xattn_bwd(q, km, vm, d_o, name="xattn_bwd")
    gw['xa_wq'] = bwd_w_mm(xn2, dq, D, D, name="mm_dw_q")
    dxn2 = bwd_x_mm(dq, 'xa_wq', D, D, out_dtype=F32, name="mm_dxn2")
    gw['xa_wk'] = bwd_w_mm(memn, dk, D, D, name="mm_dw_k")
    gw['xa_wv'] = bwd_w_mm(memn, dv, D, D, name="mm_dw_v")
    dmemn = bwd_x_mm(dk, 'xa_wk', D, D, out_dtype=F32, name="mm_dmem_k")
    dmemn = bwd_x_mm(dv, 'xa_wv', D, D, out_dtype=F32, res=dmemn, name="mm_dmem_v")
    _, d_norm_mem = _rmsnorm_bwd(mems, dmemn, norm_mem_g, None, name="norm_mem_bwd")
    dx1, d_norm2, dx1h = _rmsnorm_bwd(x1, dxn2, norm2_g, dx2, name="norm2_bwd", emit_bf16=True)

    gw['w_out'] = bwd_w_mm(mix, dx1h, D, D, name="mm_dw_out")
    tok_b = start_grads(('xa_wo', 'xa_wq', 'xa_wk', 'xa_wv', 'w_out'), "b")
    dmix = bwd_x_mm(dx1h, 'w_out', D, D, out_dtype=F32, after=tok_b, name="mm_dmix")
    dh, d_ret_g = _ret_bwd(h, cos, sin, ret_g, states, ret_raw, dmix, name="ret_bwd")
    dh, d_rcw, d_rcb, d_wa, d_ba, d_wx, d_bx, d_lam = _lru_bwd(
        h, hl, dmix, dh, rg_cw, rg_cb, wa, ba, wx, bx, lam, name="lru_bwd")
    gw['w_in'] = bwd_w_mm(xn1, dh, D, IN, name="mm_dw_in")
    tok_c = start_grads(('w_in',), "c")
    dxn1 = bwd_x_mm(dh, 'w_in', D, IN, out_dtype=F32, after=tok_c, name="mm_dxn1")
    grad_x, d_norm1 = _rmsnorm_bwd(xs, dxn1, norm1_g, dx1, name="norm1_bwd")

    small_parts = {
        'norm1_g': d_norm1, 'ret_g': d_ret_g, 'rg_conv_w': d_rcw[:rg_cw.shape[0]], 'rg_conv_b': d_rcb,
        'rg_wa': d_wa, 'rg_ba': d_ba, 'rg_wx': d_wx, 'rg_bx': d_bx, 'rg_lambda': d_lam, 'norm2_g': d_norm2,
        'norm_mem_g': d_norm_mem, 'norm3_g': d_norm3, 'ffn_conv_w': d_ffn_cw, 'ffn_conv_b': d_ffn_cb,
        'final_g': d_final}
    small = [n for n in WEIGHT_NAMES if n not in BIG_AXIS]
    red_shapes = [(1,)] + [tuple(small_parts[n].shape) for n in small]
    reduced = _allreduce_small(_pack([loss8[0:1, 0:1]] + [small_parts[n] for n in small], SUB), name="allreduce_small")
    red = _unpack(reduced, red_shapes)
    loss = red[0][0]
    g_small = dict(zip(small, red[1:]))
    for n in SMALL_SHARDED:
        w_local = W[n].shape[-1]
        g_small[n] = lax.dynamic_slice_in_dim(g_small[n], chip * w_local, w_local, axis=1)

    out_g, out_d, out_m, out_v = {}, {}, {}, {}
    for names, gg in grad_groups:
        for n, land in zip(names, gg.finish(reduced)):
            g, d, m_new, v_new = _adamw(shards[n], Mo[n][0], Vo[n][0], land, name="adamw_" + n)
            out_g[n], out_d[n], out_m[n], out_v[n] = (t.reshape(W[n].shape) for t in (g, d, m_new, v_new))
    rows = 512
    pk = lambda d: _pack([d[n] for n in small], rows)
    g_pack = _pack([g_small[n] for n in small], rows)
    res_small = _adamw(pk(W), pk(Mo), pk(Vo), g_pack[None], name="adamw_small")
    shapes_small = [tuple(W[n].shape) for n in small]
    for dst, packed in zip((out_g, out_d, out_m, out_v), res_small):
        for n, val in zip(small, _unpack(packed, shapes_small)):
            dst[n] = val
    return (loss, grad_x[None], *[out_g[n] for n in WEIGHT_NAMES], *[out_d[n] for n in WEIGHT_NAMES],
            *[out_m[n] for n in WEIGHT_NAMES], *[out_v[n] for n in WEIGHT_NAMES])


def kernel(x, mem, positions, norm1_g, w_in, ret_g, rg_conv_w, rg_conv_b, rg_wa, rg_ba, rg_wx, rg_bx, rg_lambda, w_out, norm2_g, norm_mem_g, xa_wq, xa_wk, xa_wv, xa_wo, norm3_g, ffn_w_up, ffn_conv_w, ffn_conv_b, ffn_w_down, final_g, loss_target, m_norm1_g, m_w_in, m_ret_g, m_rg_conv_w, m_rg_conv_b, m_rg_wa, m_rg_ba, m_rg_wx, m_rg_bx, m_rg_lambda, m_w_out, m_norm2_g, m_norm_mem_g, m_xa_wq, m_xa_wk, m_xa_wv, m_xa_wo, m_norm3_g, m_ffn_w_up, m_ffn_conv_w, m_ffn_conv_b, m_ffn_w_down, m_final_g, v_norm1_g, v_w_in, v_ret_g, v_rg_conv_w, v_rg_conv_b, v_rg_wa, v_rg_ba, v_rg_wx, v_rg_bx, v_rg_lambda, v_w_out, v_norm2_g, v_norm_mem_g, v_xa_wq, v_xa_wk, v_xa_wv, v_xa_wo, v_norm3_g, v_ffn_w_up, v_ffn_conv_w, v_ffn_conv_b, v_ffn_w_down, v_final_g):
    W = dict(zip(WEIGHT_NAMES, (norm1_g, w_in, ret_g, rg_conv_w, rg_conv_b, rg_wa, rg_ba, rg_wx, rg_bx, rg_lambda, w_out,
                                norm2_g, norm_mem_g, xa_wq, xa_wk, xa_wv, xa_wo, norm3_g, ffn_w_up, ffn_conv_w,
                                ffn_conv_b, ffn_w_down, final_g)))
    Mo = dict(zip(WEIGHT_NAMES, (m_norm1_g, m_w_in, m_ret_g, m_rg_conv_w, m_rg_conv_b, m_rg_wa, m_rg_ba, m_rg_wx, m_rg_bx,
                                 m_rg_lambda, m_w_out, m_norm2_g, m_norm_mem_g, m_xa_wq, m_xa_wk, m_xa_wv, m_xa_wo,
                                 m_norm3_g, m_ffn_w_up, m_ffn_conv_w, m_ffn_conv_b, m_ffn_w_down, m_final_g)))
    Vo = dict(zip(WEIGHT_NAMES, (v_norm1_g, v_w_in, v_ret_g, v_rg_conv_w, v_rg_conv_b, v_rg_wa, v_rg_ba, v_rg_wx, v_rg_bx,
                                 v_rg_lambda, v_w_out, v_norm2_g, v_norm_mem_g, v_xa_wq, v_xa_wk, v_xa_wv, v_xa_wo,
                                 v_norm3_g, v_ffn_w_up, v_ffn_conv_w, v_ffn_conv_b, v_ffn_w_down, v_final_g)))
    return _step(x, mem, positions, loss_target, W, Mo, Vo)
```

```python
import functools
import math

import jax
import jax.numpy as jnp
from jax import lax
from jax.experimental import pallas as pl
from jax.experimental.pallas import tpu as pltpu

F32 = jnp.float32
BF16 = jnp.bfloat16

EPS = 1e-6
RET_HEADS = 4
RET_CHUNK = 128
ROPE_BASE = 10000.0
LRU_BLOCKS = 8
LRU_C = 8.0
XA_HEADS = 4

ADAM_LR = 0.001
ADAM_B1 = 0.9
ADAM_B2 = 0.999
ADAM_EPS = 1e-08
ADAM_WD = 0.01
ADAM_STEP = 10

N_DEV = 8
N_CHIP = 4
MESH = pl.DeviceIdType.MESH
SUB = 8
LANE = 128
VMEM_LIMIT = 56 * 1024 * 1024

NN = ((1,), (0,))
NT = ((1,), (1,))
TN = ((0,), (0,))


def _cparams(sem):
    return pltpu.CompilerParams(dimension_semantics=sem, vmem_limit_bytes=VMEM_LIMIT)


def _sigmoid(v):
    return 1.0 / (1.0 + jnp.exp(-v))


def _bdot(a, b, dims):
    return lax.dot_general(a.astype(BF16), b.astype(BF16), (dims, ((), ())), preferred_element_type=F32)


def _row_iota(shape):
    return lax.broadcasted_iota(jnp.int32, shape, 0)


def _shift_down(v, tail, k):
    if k == 0:
        return v
    r = pltpu.roll(v, k, 0)
    rt = pltpu.roll(tail, k, 0)
    first = jnp.where(_row_iota(rt.shape) < k, rt, r[0:SUB])
    return jnp.concatenate([first, r[SUB:]], axis=0)


def _shift_up(v, head, k):
    if k == 0:
        return v
    n = v.shape[0]
    r = pltpu.roll(v, n - k, 0)
    rh = pltpu.roll(head, SUB - k, 0)
    last = jnp.where(_row_iota(rh.shape) >= SUB - k, rh, r[n - SUB:n])
    return jnp.concatenate([r[:n - SUB], last], axis=0)


def _mm(a, b, *, mode, M, N, K, out_dtype, name, tm=512, tn=512, tk=512, a_off=(0, 0), b_off=(0, 0),
        res=None, out=None, out_off=(0, 0), out_full=None, norm_g=None, after=None):
    tm, tn, tk = min(tm, M), min(tn, N), min(tk, K)
    assert M % tm == 0 and N % tn == 0 and K % tk == 0, (name, M, N, K, tm, tn, tk)
    nk = K // tk
    if mode == "nn":
        a_blk, b_blk, dims = (tm, tk), (tk, tn), NN
        a_map = lambda i, j, k: (i + a_off[0] // tm, k + a_off[1] // tk)
        b_map = lambda i, j, k: (k + b_off[0] // tk, j + b_off[1] // tn)
    elif mode == "nt":
        a_blk, b_blk, dims = (tm, tk), (tn, tk), NT
        a_map = lambda i, j, k: (i + a_off[0] // tm, k + a_off[1] // tk)
        b_map = lambda i, j, k: (j + b_off[0] // tn, k + b_off[1] // tk)
    else:
        a_blk, b_blk, dims = (tk, tm), (tk, tn), TN
        a_map = lambda i, j, k: (k + a_off[0] // tk, i + a_off[1] // tm)
        b_map = lambda i, j, k: (k + b_off[0] // tk, j + b_off[1] // tn)
    for off, blk in ((a_off, a_blk), (b_off, b_blk), (out_off, (tm, tn))):
        assert off[0] % blk[0] == 0 and off[1] % blk[1] == 0, (name, off, blk)
    o_map = lambda i, j, k: (i + out_off[0] // tm, j + out_off[1] // tn)
    has_res, has_out, has_norm, has_after = res is not None, out is not None, norm_g is not None, after is not None
    assert not has_norm or (tn == N and not has_out)

    def body(*refs):
        refs = list(refs)
        a_ref, b_ref = refs[0], refs[1]
        pos = 2
        r_ref = g_ref = n_ref = None
        if has_res:
            r_ref = refs[pos]
            pos += 1
        if has_norm:
            g_ref = refs[pos]
            pos += 1
        pos += has_out + has_after
        o_ref = refs[pos]
        pos += 1
        if has_norm:
            n_ref = refs[pos]
            pos += 1
        acc = refs[pos] if nk > 1 else None
        k = pl.program_id(2)
        part = _bdot(a_ref[...], b_ref[...], dims)

        def finish(total):
            if has_res:
                total = total + r_ref[...].astype(F32)
            o_ref[...] = total.astype(o_ref.dtype)
            if has_norm:
                r = lax.rsqrt(jnp.mean(total * total, axis=-1, keepdims=True) + EPS)
                n_ref[...] = (total * r * g_ref[...]).astype(n_ref.dtype)

        if nk == 1:
            finish(part)
        else:
            @pl.when(k == 0)
            def _():
                acc[...] = part

            @pl.when(k > 0)
            def _():
                acc[...] += part

            @pl.when(k == nk - 1)
            def _():
                finish(acc[...])

    in_specs = [pl.BlockSpec(a_blk, a_map), pl.BlockSpec(b_blk, b_map)]
    args = [a, b]
    if has_res:
        in_specs.append(pl.BlockSpec((tm, tn), lambda i, j, k: (i, j)))
        args.append(res)
    if has_norm:
        in_specs.append(pl.BlockSpec((1, N), lambda i, j, k: (0, 0)))
        args.append(norm_g)
    aliases = {}
    if has_out:
        in_specs.append(pl.BlockSpec(memory_space=pl.ANY))
        aliases = {len(args): 0}
        args.append(out)
        out_shape = jax.ShapeDtypeStruct(out.shape, out.dtype)
    else:
        out_shape = jax.ShapeDtypeStruct((M, N) if out_full is None else out_full, out_dtype)
    if has_after:
        in_specs.append(pl.BlockSpec(memory_space=pl.ANY))
        args.append(after)
    out_specs = pl.BlockSpec((tm, tn), o_map)
    if has_norm:
        out_shape = [out_shape, jax.ShapeDtypeStruct((M, N), BF16)]
        out_specs = [out_specs, pl.BlockSpec((tm, tn), lambda i, j, k: (i, j))]
    return pl.pallas_call(
        body, name=name, grid=(M // tm, N // tn, nk), in_specs=in_specs,
        out_specs=out_specs, out_shape=out_shape,
        scratch_shapes=[pltpu.VMEM((tm, tn), F32)] if nk > 1 else [],
        input_output_aliases=aliases,
        compiler_params=_cparams(("parallel", "parallel", "arbitrary")),
    )(*args)


def _rmsnorm_fwd(x, g, *, name, ts=512):
    S, D = x.shape
    ts = min(ts, S)

    def body(x_ref, g_ref, o_ref):
        xv = x_ref[...]
        r = lax.rsqrt(jnp.mean(xv * xv, axis=-1, keepdims=True) + EPS)
        o_ref[...] = (xv * r * g_ref[...]).astype(o_ref.dtype)

    return pl.pallas_call(
        body, name=name, grid=(S // ts,),
        in_specs=[pl.BlockSpec((ts, D), lambda i: (i, 0)), pl.BlockSpec((1, D), lambda i: (0, 0))],
        out_specs=pl.BlockSpec((ts, D), lambda i: (i, 0)),
        out_shape=jax.ShapeDtypeStruct((S, D), BF16),
        compiler_params=_cparams(("parallel",)),
    )(x, g)


def _rmsnorm_bwd(x, dxn, g, res, *, name, ts=256, emit_bf16=False):
    S, D = x.shape
    ts = min(ts, S)
    has_res = res is not None

    def body(*refs):
        refs = list(refs)
        dx16_ref = refs.pop() if emit_bf16 else None
        if has_res:
            x_ref, d_ref, g_ref, r_ref, dx_ref, dg_ref = refs
        else:
            x_ref, d_ref, g_ref, dx_ref, dg_ref = refs
        i = pl.program_id(0)
        xv = x_ref[...]
        dv = d_ref[...].astype(F32)
        r = lax.rsqrt(jnp.mean(xv * xv, axis=-1, keepdims=True) + EPS)
        gd = dv * g_ref[...]
        proj = jnp.mean(xv * gd, axis=-1, keepdims=True)
        dx = r * gd - xv * (r * r * r) * proj
        if has_res:
            dx = dx + r_ref[...]
        dx_ref[...] = dx
        if emit_bf16:
            dx16_ref[...] = dx.astype(BF16)
        part = jnp.sum(dv * xv * r, axis=0, keepdims=True)

        @pl.when(i == 0)
        def _():
            dg_ref[...] = part

        @pl.when(i > 0)
        def _():
            dg_ref[...] += part

    row = pl.BlockSpec((ts, D), lambda i: (i, 0))
    vec = pl.BlockSpec((1, D), lambda i: (0, 0))
    in_specs = [row, row, vec] + ([row] if has_res else [])
    args = [x, dxn, g] + ([res] if has_res else [])
    extra = emit_bf16 * [jax.ShapeDtypeStruct((S, D), BF16)]
    return pl.pallas_call(
        body, name=name, grid=(S // ts,), in_specs=in_specs, out_specs=[row, vec] + emit_bf16 * [row],
        out_shape=[jax.ShapeDtypeStruct((S, D), F32), jax.ShapeDtypeStruct((1, D), F32)] + extra,
        compiler_params=_cparams(("arbitrary",)),
    )(*args)


def _final_loss(x, target, g, *, name, ts=256):
    S, D = x.shape
    ts = min(ts, S)

    def body(x_ref, t_ref, g_ref, dx_ref, dg_ref, loss_ref, dx16_ref):
        i = pl.program_id(0)
        xv = x_ref[...]
        gv = g_ref[...]
        r = lax.rsqrt(jnp.mean(xv * xv, axis=-1, keepdims=True) + EPS)
        y = xv * r * gv
        err = y - t_ref[...]
        row_loss = jnp.mean(err * err, axis=-1, keepdims=True)
        lpart = 0.5 * jnp.sum(row_loss, axis=0, keepdims=True)
        dy = err * (1.0 / D)
        gd = dy * gv
        proj = jnp.mean(xv * gd, axis=-1, keepdims=True)
        dx = r * gd - xv * (r * r * r) * proj
        dx_ref[...] = dx
        dx16_ref[...] = dx.astype(BF16)
        part = jnp.sum(dy * xv * r, axis=0, keepdims=True)
        lfull = jnp.broadcast_to(lpart, loss_ref.shape)

        @pl.when(i == 0)
        def _():
            dg_ref[...] = part
            loss_ref[...] = lfull

        @pl.when(i > 0)
        def _():
            dg_ref[...] += part
            loss_ref[...] += lfull

    row = pl.BlockSpec((ts, D), lambda i: (i, 0))
    vec = pl.BlockSpec((1, D), lambda i: (0, 0))
    return pl.pallas_call(
        body, name=name, grid=(S // ts,), in_specs=[row, row, vec],
        out_specs=[row, vec, pl.BlockSpec((SUB, LANE), lambda i: (0, 0)), row],
        out_shape=[jax.ShapeDtypeStruct((S, D), F32), jax.ShapeDtypeStruct((1, D), F32),
                   jax.ShapeDtypeStruct((SUB, LANE), F32), jax.ShapeDtypeStruct((S, D), BF16)],
        compiler_params=_cparams(("arbitrary",)),
    )(x, target, g)


def _rope_table(pos_col, inv, *, name, ts=1024):
    S = pos_col.shape[0]
    ts = min(ts, S)
    half = inv.shape[1]

    def body(p_ref, inv_ref, c_ref, s_ref):
        ang = p_ref[...].astype(F32) * inv_ref[...]
        c_ref[...] = jnp.cos(ang)
        s_ref[...] = jnp.sin(ang)

    tab = pl.BlockSpec((ts, half), lambda i: (i, 0))
    return pl.pallas_call(
        body, name=name, grid=(S // ts,),
        in_specs=[pl.BlockSpec((ts, 1), lambda i: (i, 0)), pl.BlockSpec((1, half), lambda i: (0, 0))],
        out_specs=[tab, tab],
        out_shape=[jax.ShapeDtypeStruct((S, half), F32), jax.ShapeDtypeStruct((S, half), F32)],
        compiler_params=_cparams(("parallel",)),
    )(pos_col, inv)


def _ret_consts(C, log_g):
    ii = lax.broadcasted_iota(jnp.int32, (C, C), 0)
    jj = lax.broadcasted_iota(jnp.int32, (C, C), 1)
    diff = (ii - jj).astype(F32)
    intra = jnp.where(ii >= jj, jnp.exp(log_g * jnp.maximum(diff, 0.0)), 0.0)
    idx = lax.broadcasted_iota(jnp.int32, (C, 1), 0).astype(F32)
    qd = jnp.exp(log_g * (idx + 1.0))
    kd = jnp.exp(log_g * (C - 1.0 - idx))
    cd = math.exp(log_g * C)
    return intra, qd, kd, cd


def _rot(t, cs, sn):
    half = t.shape[-1] // 2
    t1, t2 = t[:, :half], t[:, half:]
    return jnp.concatenate([t1 * cs - t2 * sn, t1 * sn + t2 * cs], axis=-1)


def _unrot(d, cs, sn):
    half = d.shape[-1] // 2
    d1, d2 = d[:, :half], d[:, half:]
    return jnp.concatenate([d1 * cs + d2 * sn, d2 * cs - d1 * sn], axis=-1)


def _ret_fwd(h, cos, sin, ret_g, mix, *, name, ch=2):
    S = h.shape[0]
    R = ret_g.shape[1]
    H, C = RET_HEADS, RET_CHUNK
    Dh = R // H
    ts = ch * C
    assert S % ts == 0
    log_gs = [math.log(1.0 - 2.0 ** (-5.0 - hd)) for hd in range(H)]
    scale = Dh ** -0.5

    def body(x_ref, c_ref, s_ref, rg_ref, mix_in, ret_ref, st_ref, mix_ref, state):
        i = pl.program_id(0)

        @pl.when(i == 0)
        def _():
            state[...] = jnp.zeros_like(state)

        for c in range(ch):
            rows = pl.ds(c * C, C)
            cs, sn = c_ref[rows, :], s_ref[rows, :]
            for hd in range(H):
                intra, qd, kd, cd = _ret_consts(C, log_gs[hd])
                q = x_ref[rows, pl.ds(hd * Dh, Dh)]
                k = x_ref[rows, pl.ds(R + hd * Dh, Dh)]
                v = x_ref[rows, pl.ds(2 * R + hd * Dh, Dh)]
                g = x_ref[rows, pl.ds(3 * R + hd * Dh, Dh)]
                rq = _rot(q, cs, sn)
                rk = _rot(k, cs, sn) * scale
                st = state[hd]
                st_ref[c, hd] = st.astype(BF16)
                s_ = _bdot(rq, rk, NT) * intra
                ret = _bdot(s_, v, NN) + _bdot(rq * qd, st, NN)
                state[hd] = st * cd + _bdot(rk * kd, v, TN)
                ret_ref[rows, pl.ds(hd * Dh, Dh)] = ret
                rr = lax.rsqrt(jnp.mean(ret * ret, axis=-1, keepdims=True) + EPS)
                out = ret * rr * rg_ref[:, pl.ds(hd * Dh, Dh)] * (g * _sigmoid(g))
                mix_ref[rows, pl.ds(hd * Dh, Dh)] = out.astype(BF16)

    n_chunks = S // C
    return pl.pallas_call(
        body, name=name, grid=(S // ts,),
        in_specs=[pl.BlockSpec((ts, 4 * R), lambda i: (i, 0)),
                  pl.BlockSpec((ts, Dh // 2), lambda i: (i, 0)), pl.BlockSpec((ts, Dh // 2), lambda i: (i, 0)),
                  pl.BlockSpec((1, R), lambda i: (0, 0)), pl.BlockSpec(memory_space=pl.ANY)],
        out_specs=[pl.BlockSpec((ts, R), lambda i: (i, 0)),
                   pl.BlockSpec((ch, H, Dh, Dh), lambda i: (i, 0, 0, 0)),
                   pl.BlockSpec((ts, R), lambda i: (i, 0))],
        out_shape=[jax.ShapeDtypeStruct((S, R), F32), jax.ShapeDtypeStruct((n_chunks, H, Dh, Dh), BF16),
                   jax.ShapeDtypeStruct(mix.shape, mix.dtype)],
        scratch_shapes=[pltpu.VMEM((H, Dh, Dh), F32)],
        input_output_aliases={4: 2},
        compiler_params=_cparams(("arbitrary",)),
    )(h, cos, sin, ret_g, mix)


def _ret_bwd(h, cos, sin, ret_g, states, ret_raw, dmix, *, name, ch=2):
    S = h.shape[0]
    R = ret_g.shape[1]
    H, C = RET_HEADS, RET_CHUNK
    Dh = R // H
    ts = ch * C
    nb = S // ts
    log_gs = [math.log(1.0 - 2.0 ** (-5.0 - hd)) for hd in range(H)]
    scale = Dh ** -0.5

    def body(x_ref, c_ref, s_ref, rg_ref, st_ref, ret_ref, dm_ref, dh_ref, drg_ref, dstate):
        i = pl.program_id(0)

        @pl.when(i == 0)
        def _():
            dstate[...] = jnp.zeros_like(dstate)
            drg_ref[...] = jnp.zeros_like(drg_ref)

        for c in reversed(range(ch)):
            rows = pl.ds(c * C, C)
            cs, sn = c_ref[rows, :], s_ref[rows, :]
            for hd in range(H):
                intra, qd, kd, cd = _ret_consts(C, log_gs[hd])
                cols = pl.ds(hd * Dh, Dh)
                q = x_ref[rows, pl.ds(hd * Dh, Dh)]
                k = x_ref[rows, pl.ds(R + hd * Dh, Dh)]
                v = x_ref[rows, pl.ds(2 * R + hd * Dh, Dh)]
                g = x_ref[rows, pl.ds(3 * R + hd * Dh, Dh)]
                rq = _rot(q, cs, sn)
                rk = _rot(k, cs, sn) * scale
                ret = ret_ref[rows, cols]
                dm = dm_ref[rows, cols]
                rgv = rg_ref[:, cols]
                rr = lax.rsqrt(jnp.mean(ret * ret, axis=-1, keepdims=True) + EPS)
                retn = ret * rr
                sg = _sigmoid(g)
                silu = g * sg
                drg_ref[:, cols] += jnp.sum(dm * retn * silu, axis=0, keepdims=True)
                dg = dm * retn * rgv * (sg * (1.0 + g * (1.0 - sg)))
                dretn = dm * rgv * silu
                d_o = rr * dretn - ret * (rr * rr * rr) * jnp.mean(ret * dretn, axis=-1, keepdims=True)
                st = st_ref[c, hd]
                d_s = dstate[hd]
                a_ = _bdot(rq, rk, NT) * intra
                d_a = _bdot(d_o, v, NT) * intra
                d_qr = _bdot(d_a, rk, NN) + _bdot(d_o, st, NT) * qd
                d_kr = _bdot(d_a, rq, TN) + _bdot(v, d_s, NT) * kd
                d_v = _bdot(a_, d_o, TN) + _bdot(rk * kd, d_s, NN)
                dstate[hd] = d_s * cd + _bdot(rq * qd, d_o, TN)
                dh_ref[rows, pl.ds(hd * Dh, Dh)] = _unrot(d_qr, cs, sn).astype(BF16)
                dh_ref[rows, pl.ds(R + hd * Dh, Dh)] = (_unrot(d_kr, cs, sn) * scale).astype(BF16)
                dh_ref[rows, pl.ds(2 * R + hd * Dh, Dh)] = d_v.astype(BF16)
                dh_ref[rows, pl.ds(3 * R + hd * Dh, Dh)] = dg.astype(BF16)

    rb = lambda i: nb - 1 - i
    return pl.pallas_call(
        body, name=name, grid=(nb,),
        in_specs=[pl.BlockSpec((ts, 4 * R), lambda i: (rb(i), 0)),
                  pl.BlockSpec((ts, Dh // 2), lambda i: (rb(i), 0)), pl.BlockSpec((ts, Dh // 2), lambda i: (rb(i), 0)),
                  pl.BlockSpec((1, R), lambda i: (0, 0)),
                  pl.BlockSpec((ch, H, Dh, Dh), lambda i: (rb(i), 0, 0, 0)),
                  pl.BlockSpec((ts, R), lambda i: (rb(i), 0)),
                  pl.BlockSpec((ts, R), lambda i: (rb(i), 0))],
        out_specs=[pl.BlockSpec((ts, 4 * R), lambda i: (rb(i), 0)), pl.BlockSpec((1, R), lambda i: (0, 0))],
        out_shape=[jax.ShapeDtypeStruct((S, 6 * R), BF16), jax.ShapeDtypeStruct((1, R), F32)],
        scratch_shapes=[pltpu.VMEM((H, Dh, Dh), F32)],
        compiler_params=_cparams(("arbitrary",)),
    )(h, cos, sin, ret_g, states, ret_raw, dmix)


GELU_C = math.sqrt(2.0 / math.pi)
GELU_A = 0.044715


def _gelu_parts(y):
    t = jnp.tanh(GELU_C * (y + GELU_A * y * y * y))
    val = 0.5 * y * (1.0 + t)
    grad = 0.5 * (1.0 + t) + 0.5 * y * (1.0 - t * t) * GELU_C * (1.0 + 3.0 * GELU_A * y * y)
    return val, grad


def _neg_expm1(x):
    series = -x * (1.0 + x * (1.0 / 2.0) * (1.0 + x * (1.0 / 3.0) * (1.0 + x * (1.0 / 4.0) * (
        1.0 + x * (1.0 / 5.0) * (1.0 + x * (1.0 / 6.0) * (1.0 + x * (1.0 / 7.0)))))))
    return jnp.where(x > -0.35, series, 1.0 - jnp.exp(x))


def _log_sigmoid(x):
    return jnp.minimum(x, 0.0) - jnp.log1p(jnp.exp(-jnp.abs(x)))


def _lru_gates(uc, wa_ref, ba_ref, wx_ref, bx_ref):
    nbk = wa_ref.shape[0]
    bd = wa_ref.shape[1]
    rs, gs = [], []
    for n in range(nbk):
        ucn = uc[:, n * bd:(n + 1) * bd]
        rs.append(_sigmoid(_bdot(ucn, wa_ref[n], NN) + ba_ref[:, pl.ds(n * bd, bd)]))
        gs.append(_sigmoid(_bdot(ucn, wx_ref[n], NN) + bx_ref[:, pl.ds(n * bd, bd)]))
    return jnp.concatenate(rs, axis=-1), jnp.concatenate(gs, axis=-1)


def _lru_fwd(h, conv_w, conv_b, wa, ba, wx, bx, lam, *, name, ts=256):
    S = h.shape[0]
    W = lam.shape[1]
    K = conv_w.shape[0]
    ts = min(ts, S)

    def body(u_ref, y_ref, cw_ref, cb_ref, wa_ref, ba_ref, wx_ref, bx_ref, lam_ref, hl_ref, mix_ref, tail, hlast):
        i = pl.program_id(0)

        @pl.when(i == 0)
        def _():
            tail[...] = jnp.zeros_like(tail)
            hlast[...] = jnp.zeros_like(hlast)

        u = u_ref[...]
        tl = tail[...]
        uc = cb_ref[...] + cw_ref[K - 1:K, :] * u
        for k in range(K - 1):
            uc = uc + cw_ref[k:k + 1, :] * _shift_down(u, tl, K - 1 - k)
        tail[...] = u[ts - SUB:ts]
        r, ig = _lru_gates(uc, wa_ref, ba_ref, wx_ref, bx_ref)
        log_a = LRU_C * r * _log_sigmoid(lam_ref[...])
        a = jnp.exp(log_a)
        b = jnp.sqrt(_neg_expm1(2.0 * log_a)) * (ig * uc)
        rid = _row_iota((ts, W))
        d = 1
        while d < ts:
            a_s = jnp.where(rid < d, 1.0, pltpu.roll(a, d, 0))
            b_s = jnp.where(rid < d, 0.0, pltpu.roll(b, d, 0))
            b = a * b_s + b
            a = a * a_s
            d *= 2
        hcur = a * hlast[SUB - 1:SUB, :] + b
        hlast[...] = hcur[ts - SUB:ts]
        hl_ref[...] = hcur
        gy, _ = _gelu_parts(y_ref[...])
        mix_ref[...] = (hcur * gy).astype(BF16)

    full = lambda shape: pl.BlockSpec(shape, lambda i: tuple(0 for _ in shape))
    return pl.pallas_call(
        body, name=name, grid=(S // ts,),
        in_specs=[pl.BlockSpec((ts, W), lambda i: (i, 4)), pl.BlockSpec((ts, W), lambda i: (i, 5)),
                  full(conv_w.shape), full(conv_b.shape), full(wa.shape), full(ba.shape), full(wx.shape),
                  full(bx.shape), full(lam.shape)],
        out_specs=[pl.BlockSpec((ts, W), lambda i: (i, 0)), pl.BlockSpec((ts, W), lambda i: (i, 1))],
        out_shape=[jax.ShapeDtypeStruct((S, W), F32), jax.ShapeDtypeStruct((S, 2 * W), BF16)],
        scratch_shapes=[pltpu.VMEM((SUB, W), F32), pltpu.VMEM((SUB, W), F32)],
        compiler_params=_cparams(("arbitrary",)),
    )(h, h, conv_w, conv_b, wa, ba, wx, bx, lam)


def _lru_bwd(h, hl, dmix, dh, conv_w, conv_b, wa, ba, wx, bx, lam, *, name, ts=256):
    S = h.shape[0]
    W = lam.shape[1]
    K = conv_w.shape[0]
    nbk, bd = wa.shape[0], wa.shape[1]
    ts = min(ts, S)
    nb = S // ts
    t8 = ts // SUB

    def body(u_ref, y_ref, uh_ref, hl_ref, hh_ref, dm_ref, cw_ref, cb_ref, wa_ref, ba_ref, wx_ref, bx_ref, lam_ref,
             dh_in, dh_ref, dcw_ref, dcb_ref, dwa_ref, dba_ref, dwx_ref, dbx_ref, dlam_ref, carry, head):
        i = pl.program_id(0)
        blk = nb - 1 - i

        @pl.when(i == 0)
        def _():
            carry[...] = jnp.zeros_like(carry)
            head[...] = jnp.zeros_like(head)
            for ref in (dcw_ref, dcb_ref, dwa_ref, dba_ref, dwx_ref, dbx_ref, dlam_ref):
                ref[...] = jnp.zeros_like(ref)

        inside = (blk > 0).astype(F32)
        u = u_ref[...]
        tl = uh_ref[...] * inside
        sh = [_shift_down(u, tl, K - 1 - k) for k in range(K)]
        uc = cb_ref[...]
        for k in range(K):
            uc = uc + cw_ref[k:k + 1, :] * sh[k]
        r, ig = _lru_gates(uc, wa_ref, ba_ref, wx_ref, bx_ref)
        lam_v = lam_ref[...]
        ls = _log_sigmoid(lam_v)
        log_a = LRU_C * r * ls
        a = jnp.exp(log_a)
        mult = jnp.sqrt(_neg_expm1(2.0 * log_a))
        hcur = hl_ref[...]
        hprev = _shift_down(hcur, hh_ref[...] * inside, 1)
        gy, dgy = _gelu_parts(y_ref[...])
        dm = dm_ref[...]
        d_y = dm * hcur * dgy
        rid = _row_iota((ts, W))
        bq = dm * gy + jnp.where(rid == ts - 1, carry[0:1, :], 0.0)
        aq = jnp.where(rid == ts - 1, 0.0, pltpu.roll(a, ts - 1, 0))
        d = 1
        while d < ts:
            a_s = jnp.where(rid >= ts - d, 0.0, pltpu.roll(aq, ts - d, 0))
            b_s = jnp.where(rid >= ts - d, 0.0, pltpu.roll(bq, ts - d, 0))
            bq = bq + aq * b_s
            aq = aq * a_s
            d *= 2
        lam_t = bq
        carry[...] = (a * lam_t)[0:SUB]
        d_a = lam_t * hprev
        d_mult = lam_t * (ig * uc)
        d_i = lam_t * mult * uc
        d_uc = lam_t * mult * ig
        d_log_a = d_a * a - d_mult * (a * a) / mult
        d_r = d_log_a * (LRU_C * ls)
        dlam_ref[...] += jnp.sum(d_log_a * (LRU_C * r), axis=0, keepdims=True) * _sigmoid(-lam_v)
        d_pr = d_r * r * (1.0 - r)
        d_pi = d_i * ig * (1.0 - ig)
        dba_ref[...] += jnp.sum(d_pr, axis=0, keepdims=True)
        dbx_ref[...] += jnp.sum(d_pi, axis=0, keepdims=True)
        extra = []
        for n in range(nbk):
            sl = slice(n * bd, (n + 1) * bd)
            ucn = uc[:, sl]
            dwa_ref[n] += _bdot(ucn, d_pr[:, sl], TN)
            dwx_ref[n] += _bdot(ucn, d_pi[:, sl], TN)
            extra.append(_bdot(d_pr[:, sl], wa_ref[n], NT) + _bdot(d_pi[:, sl], wx_ref[n], NT))
        d_uc = d_uc + jnp.concatenate(extra, axis=-1)
        dcb_ref[...] += jnp.sum(d_uc, axis=0, keepdims=True)
        rid8 = _row_iota((SUB, W))
        dcw = jnp.zeros((SUB, W), F32)
        for k in range(K):
            dcw = dcw + jnp.where(rid8 == k, jnp.sum(d_uc * sh[k], axis=0, keepdims=True), 0.0)
        dcw_ref[...] += dcw
        hd = head[...]
        d_u = cw_ref[K - 1:K, :] * d_uc
        for j in range(1, K):
            d_u = d_u + cw_ref[K - 1 - j:K - j, :] * _shift_up(d_uc, hd, j)
        head[...] = d_uc[0:SUB]
        dh_ref[:, 0:W] = d_u.astype(BF16)
        dh_ref[:, W:2 * W] = d_y.astype(BF16)

    rb = lambda i: nb - 1 - i
    prev8 = lambda i: jnp.maximum(rb(i) * t8 - 1, 0)
    full = lambda shape: pl.BlockSpec(shape, lambda i: tuple(0 for _ in shape))
    small = [jax.ShapeDtypeStruct((SUB, W), F32), jax.ShapeDtypeStruct((1, W), F32),
             jax.ShapeDtypeStruct(wa.shape, F32), jax.ShapeDtypeStruct((1, W), F32),
             jax.ShapeDtypeStruct(wx.shape, F32), jax.ShapeDtypeStruct((1, W), F32),
             jax.ShapeDtypeStruct((1, W), F32)]
    return pl.pallas_call(
        body, name=name, grid=(nb,),
        in_specs=[pl.BlockSpec((ts, W), lambda i: (rb(i), 4)), pl.BlockSpec((ts, W), lambda i: (rb(i), 5)),
                  pl.BlockSpec((SUB, W), lambda i: (prev8(i), 4)),
                  pl.BlockSpec((ts, W), lambda i: (rb(i), 0)), pl.BlockSpec((SUB, W), lambda i: (prev8(i), 0)),
                  pl.BlockSpec((ts, W), lambda i: (rb(i), 1)),
                  full(conv_w.shape), full(conv_b.shape), full(wa.shape), full(ba.shape), full(wx.shape),
                  full(bx.shape), full(lam.shape), pl.BlockSpec(memory_space=pl.ANY)],
        out_specs=[pl.BlockSpec((ts, 2 * W), lambda i: (rb(i), 2))] + [full(s.shape) for s in small],
        out_shape=[jax.ShapeDtypeStruct(dh.shape, dh.dtype)] + small,
        scratch_shapes=[pltpu.VMEM((SUB, W), F32), pltpu.VMEM((SUB, W), F32)],
        input_output_aliases={13: 0},
        compiler_params=_cparams(("arbitrary",)),
    )(h, h, h, hl, hl, dmix, conv_w, conv_b, wa, ba, wx, bx, lam, dh)


def _xattn_fwd(q, km, vm, *, name, ts=512):
    S, D = q.shape
    M = km.shape[0]
    H = XA_HEADS
    Dh = D // H
    ts = min(ts, S)
    scale = Dh ** -0.5

    def body(q_ref, k_ref, v_ref, o_ref):
        for hd in range(H):
            cols = pl.ds(hd * Dh, Dh)
            s = _bdot(q_ref[:, cols], k_ref[:, cols], NT) * scale
            s = s - jnp.max(s, axis=-1, keepdims=True)
            e = jnp.exp(s)
            p = e / jnp.sum(e, axis=-1, keepdims=True)
            o_ref[:, cols] = _bdot(p, v_ref[:, cols], NN).astype(o_ref.dtype)

    return pl.pallas_call(
        body, name=name, grid=(S // ts,),
        in_specs=[pl.BlockSpec((ts, D), lambda i: (i, 0)), pl.BlockSpec((M, D), lambda i: (0, 0)),
                  pl.BlockSpec((M, D), lambda i: (0, 0))],
        out_specs=pl.BlockSpec((ts, D), lambda i: (i, 0)),
        out_shape=jax.ShapeDtypeStruct((S, D), BF16),
        compiler_params=_cparams(("parallel",)),
    )(q, km, vm)


def _xattn_bwd(q, km, vm, d_o, *, name, ts=512):
    S, D = q.shape
    M = km.shape[0]
    H = XA_HEADS
    Dh = D // H
    ts = min(ts, S)
    scale = Dh ** -0.5

    def body(q_ref, k_ref, v_ref, do_ref, dq_ref, dk_ref, dv_ref):
        i = pl.program_id(0)

        @pl.when(i == 0)
        def _():
            dk_ref[...] = jnp.zeros_like(dk_ref)
            dv_ref[...] = jnp.zeros_like(dv_ref)

        for hd in range(H):
            cols = pl.ds(hd * Dh, Dh)
            qh, kh, vh, doh = q_ref[:, cols], k_ref[:, cols], v_ref[:, cols], do_ref[:, cols]
            s = _bdot(qh, kh, NT) * scale
            s = s - jnp.max(s, axis=-1, keepdims=True)
            e = jnp.exp(s)
            p = e / jnp.sum(e, axis=-1, keepdims=True)
            dp = _bdot(doh, vh, NT)
            ds = p * (dp - jnp.sum(dp * p, axis=-1, keepdims=True)) * scale
            dq_ref[:, cols] = _bdot(ds, kh, NN).astype(dq_ref.dtype)
            dk_ref[:, cols] += _bdot(ds, qh, TN)
            dv_ref[:, cols] += _bdot(p, doh, TN)

    row = pl.BlockSpec((ts, D), lambda i: (i, 0))
    mem = pl.BlockSpec((M, D), lambda i: (0, 0))
    return pl.pallas_call(
        body, name=name, grid=(S // ts,), in_specs=[row, mem, mem, row], out_specs=[row, mem, mem],
        out_shape=[jax.ShapeDtypeStruct((S, D), BF16), jax.ShapeDtypeStruct((M, D), F32),
                   jax.ShapeDtypeStruct((M, D), F32)],
        compiler_params=_cparams(("arbitrary",)),
    )(q, km, vm, d_o)


def _conv_rows(v, tail, cw_ref, cb_ref):
    K = cw_ref.shape[0]
    sh = [_shift_down(v, tail, K - 1 - k) for k in range(K)]
    out = cb_ref[...]
    for k in range(K):
        out = out + cw_ref[k:k + 1, :] * sh[k]
    return out, sh


def _ffn_gate_fwd(hh, cw, cb, *, name, ts=512, tc=512):
    S, F2 = hh.shape
    F = F2 // 2
    ts, tc = min(ts, S), min(tc, F)
    nj = F // tc
    K = cw.shape[0]

    def body(a_ref, b_ref, cwa_ref, cwb_ref, cba_ref, cbb_ref, o_ref, ta, tb):
        i = pl.program_id(1)

        @pl.when(i == 0)
        def _():
            ta[...] = jnp.zeros_like(ta)
            tb[...] = jnp.zeros_like(tb)

        av, bv = a_ref[...], b_ref[...]
        ac, _ = _conv_rows(av, ta[...], cwa_ref, cba_ref)
        bc, _ = _conv_rows(bv, tb[...], cwb_ref, cbb_ref)
        ta[...] = av[ts - SUB:ts]
        tb[...] = bv[ts - SUB:ts]
        o_ref[...] = (ac * _sigmoid(ac) * bc).astype(o_ref.dtype)

    return pl.pallas_call(
        body, name=name, grid=(nj, S // ts),
        in_specs=[pl.BlockSpec((ts, tc), lambda j, i: (i, j)), pl.BlockSpec((ts, tc), lambda j, i: (i, j + nj)),
                  pl.BlockSpec((K, tc), lambda j, i: (0, j)), pl.BlockSpec((K, tc), lambda j, i: (0, j + nj)),
                  pl.BlockSpec((1, tc), lambda j, i: (0, j)), pl.BlockSpec((1, tc), lambda j, i: (0, j + nj))],
        out_specs=pl.BlockSpec((ts, tc), lambda j, i: (i, j)),
        out_shape=jax.ShapeDtypeStruct((S, F), BF16),
        scratch_shapes=[pltpu.VMEM((SUB, tc), F32), pltpu.VMEM((SUB, tc), F32)],
        compiler_params=_cparams(("parallel", "arbitrary")),
    )(hh, hh, cw, cw, cb, cb)


def _ffn_gate_bwd(hh, dact, cw, cb, *, name, ts=512, tc=512):
    S, F2 = hh.shape
    F = F2 // 2
    ts, tc = min(ts, S), min(tc, F)
    nj = F // tc
    nb = S // ts
    t8 = ts // SUB
    K = cw.shape[0]

    def body(a_ref, ah_ref, b_ref, bh_ref, d_ref, cwa_ref, cwb_ref, cba_ref, cbb_ref,
             da_ref, db_ref, ga_ref, gb_ref, ha, hb):
        i = pl.program_id(1)
        blk = nb - 1 - i

        @pl.when(i == 0)
        def _():
            for ref in (ha, hb, ga_ref, gb_ref):
                ref[...] = jnp.zeros_like(ref)

        inside = (blk > 0).astype(F32)
        ac, sha = _conv_rows(a_ref[...], ah_ref[...] * inside, cwa_ref, cba_ref)
        bc, shb = _conv_rows(b_ref[...], bh_ref[...] * inside, cwb_ref, cbb_ref)
        dv = d_ref[...].astype(F32)
        sg = _sigmoid(ac)
        d_bc = dv * ac * sg
        d_ac = dv * bc * sg * (1.0 + ac * (1.0 - sg))
        rid8 = _row_iota((SUB, tc))
        for d_c, sh, cw_ref, head, o_ref, g_ref in ((d_ac, sha, cwa_ref, ha, da_ref, ga_ref),
                                                     (d_bc, shb, cwb_ref, hb, db_ref, gb_ref)):
            hd = head[...]
            d_in = cw_ref[K - 1:K, :] * d_c
            for j in range(1, K):
                d_in = d_in + cw_ref[K - 1 - j:K - j, :] * _shift_up(d_c, hd, j)
            head[...] = d_c[0:SUB]
            o_ref[...] = d_in.astype(o_ref.dtype)
            gsum = jnp.where(rid8 == K, jnp.sum(d_c, axis=0, keepdims=True), 0.0)
            for k in range(K):
                gsum = gsum + jnp.where(rid8 == k, jnp.sum(d_c * sh[k], axis=0, keepdims=True), 0.0)
            g_ref[...] += gsum

    rb = lambda i: nb - 1 - i
    prev8 = lambda i: jnp.maximum(rb(i) * t8 - 1, 0)
    outs = pl.pallas_call(
        body, name=name, grid=(nj, nb),
        in_specs=[pl.BlockSpec((ts, tc), lambda j, i: (rb(i), j)), pl.BlockSpec((SUB, tc), lambda j, i: (prev8(i), j)),
                  pl.BlockSpec((ts, tc), lambda j, i: (rb(i), j + nj)),
                  pl.BlockSpec((SUB, tc), lambda j, i: (prev8(i), j + nj)),
                  pl.BlockSpec((ts, tc), lambda j, i: (rb(i), j)),
                  pl.BlockSpec((K, tc), lambda j, i: (0, j)), pl.BlockSpec((K, tc), lambda j, i: (0, j + nj)),
                  pl.BlockSpec((1, tc), lambda j, i: (0, j)), pl.BlockSpec((1, tc), lambda j, i: (0, j + nj))],
        out_specs=[pl.BlockSpec((ts, tc), lambda j, i: (rb(i), j)), pl.BlockSpec((ts, tc), lambda j, i: (rb(i), j)),
                   pl.BlockSpec((SUB, tc), lambda j, i: (0, j)), pl.BlockSpec((SUB, tc), lambda j, i: (0, j))],
        out_shape=[jax.ShapeDtypeStruct((S, F), BF16), jax.ShapeDtypeStruct((S, F), BF16),
                   jax.ShapeDtypeStruct((SUB, F), F32), jax.ShapeDtypeStruct((SUB, F), F32)],
        scratch_shapes=[pltpu.VMEM((SUB, tc), F32), pltpu.VMEM((SUB, tc), F32)],
        compiler_params=_cparams(("parallel", "arbitrary")),
    )(hh, hh, hh, hh, dact, cw, cw, cb, cb)
    return outs


ADAM_BLOCK_ELEMS = 128 * 1024


def _adamw(w, m, v, parts, *, name):
    R, C = w.shape
    n = parts.shape[0]
    tr = R
    for cand in (1024, 512, 256, 128, 64, 32, 16):
        if R % cand == 0 and cand * C <= ADAM_BLOCK_ELEMS:
            tr = cand
            break
    c1 = 1.0 - ADAM_B1 ** ADAM_STEP
    c2 = 1.0 - ADAM_B2 ** ADAM_STEP

    def body(w_ref, m_ref, v_ref, p_ref, g_ref, d_ref, nm_ref, nv_ref):
        g = p_ref[0].astype(F32)
        for k in range(1, n):
            g = g + p_ref[k].astype(F32)
        m_new = ADAM_B1 * m_ref[...] + (1.0 - ADAM_B1) * g
        v_new = ADAM_B2 * v_ref[...] + (1.0 - ADAM_B2) * (g * g)
        m_hat = m_new / c1
        v_hat = v_new / c2
        g_ref[...] = g
        d_ref[...] = -ADAM_LR * (m_hat / (jnp.sqrt(v_hat) + ADAM_EPS) + ADAM_WD * w_ref[...])
        nm_ref[...] = m_new
        nv_ref[...] = v_new

    blk = pl.BlockSpec((tr, C), lambda i: (i, 0))
    sds = jax.ShapeDtypeStruct((R, C), F32)
    return pl.pallas_call(
        body, name=name, grid=(R // tr,),
        in_specs=[blk, blk, blk, pl.BlockSpec((n, tr, C), lambda i: (0, i, 0))],
        out_specs=[blk, blk, blk, blk], out_shape=[sds, sds, sds, sds],
        compiler_params=_cparams(("parallel",)),
    )(w, m, v, parts)


def _mesh_place():
    x, y, c = lax.axis_index("x"), lax.axis_index("y"), lax.axis_index("c")
    others = [(1 - x, y), (x, 1 - y), (1 - x, 1 - y)]
    return x, y, c, others


HBM_SPEC = pl.BlockSpec(memory_space=pltpu.HBM)
SEM_SPEC = pl.BlockSpec(memory_space=pltpu.SEMAPHORE)
ANY_SPEC = pl.BlockSpec(memory_space=pl.ANY)
EFFECT = pltpu.SideEffectType.DATAFLOW_SIDE_EFFECTING


def _in_hbm(a):
    return pltpu.with_memory_space_constraint(a, pltpu.HBM)


def _split_start(srcs, lands, copies, n_cp, *, name):
    n_s, n_l = len(srcs), len(lands)

    def body(*refs):
        src_refs, land_refs = refs[:n_s], refs[n_s:n_s + n_l]
        ssem, rsem = refs[n_s + n_l], refs[n_s + n_l + 1]
        token = refs[-1]
        for outgoing, _ in copies(src_refs, land_refs, ssem, rsem):
            outgoing.start()
        token[...] = jnp.zeros_like(token)

    outs = pl.pallas_call(
        body, name=name,
        out_shape=(pltpu.SemaphoreType.DMA((n_cp,)), pltpu.SemaphoreType.DMA((n_cp,)),
                   *[pltpu.HBM(a.shape, a.dtype) for a in srcs], *[pltpu.HBM(a.shape, a.dtype) for a in lands],
                   jax.ShapeDtypeStruct((SUB, LANE), F32)),
        in_specs=[HBM_SPEC] * (n_s + n_l),
        out_specs=(SEM_SPEC, SEM_SPEC, *[HBM_SPEC] * (n_s + n_l), pl.BlockSpec(memory_space=pltpu.VMEM)),
        input_output_aliases={i: 2 + i for i in range(n_s + n_l)},
        compiler_params=pltpu.CompilerParams(has_side_effects=EFFECT),
    )(*[_in_hbm(a) for a in srcs], *[_in_hbm(a) for a in lands])
    ssem, rsem = outs[0], outs[1]
    return ssem, rsem, list(outs[2:2 + n_s]), list(outs[2 + n_s:2 + n_s + n_l]), outs[-1]


def _split_wait(srcs, lands, ssem, rsem, after, copies, *, name):
    n_s, n_l = len(srcs), len(lands)

    def body(*refs):
        src_refs, land_refs = refs[:n_s], refs[n_s:n_s + n_l]
        s_ref, r_ref = refs[n_s + n_l], refs[n_s + n_l + 1]
        for outgoing, incoming in copies(src_refs, land_refs, s_ref, r_ref):
            outgoing.wait_send()
            incoming.wait_recv()

    outs = pl.pallas_call(
        body, name=name,
        out_shape=(*[pltpu.HBM(a.shape, a.dtype) for a in srcs], *[pltpu.HBM(a.shape, a.dtype) for a in lands]),
        in_specs=[HBM_SPEC] * (n_s + n_l) + [SEM_SPEC, SEM_SPEC, ANY_SPEC], out_specs=[HBM_SPEC] * (n_s + n_l),
        input_output_aliases={i: i for i in range(n_s + n_l)},
        compiler_params=pltpu.CompilerParams(has_side_effects=EFFECT),
    )(*srcs, *lands, ssem, rsem, after)
    return list(outs[:n_s]), list(outs[n_s:])


class _WeightGather:
    def __init__(self, shards, axes, splits, tag):
        self.shards, self.axes, self.splits, self.tag = list(shards), axes, splits, tag
        self.n = len(shards)
        self.full_shapes = [(s.shape[0] * N_CHIP, s.shape[1]) if ax == 0 else (s.shape[0], s.shape[1] * N_CHIP)
                            for s, ax in zip(shards, axes)]

    def _region(self, land_refs, it, chip, half):
        r, w = self.shards[it].shape
        by_rows = self.axes[it] == 0
        if self.splits[it] and half is not None:
            rows = pl.ds(pl.multiple_of(half * (r // 2) + (chip * r if by_rows else 0), 16), r // 2)
        else:
            rows = pl.ds(chip * r if by_rows else 0, r)
        cols = pl.ds(0, w) if by_rows else pl.ds(pl.multiple_of(chip * w, LANE), w)
        return land_refs[it].at[rows, cols]

    def _src_half(self, src_refs, it, half):
        r = self.shards[it].shape[0]
        if self.splits[it]:
            return src_refs[it].at[pl.ds(pl.multiple_of(half * (r // 2), 16), r // 2), :]
        return src_refs[it]

    def _ici(self, src_refs, land_refs, ssem, rsem):
        x, y, c, others = _mesh_place()
        pairs = []
        for it in range(self.n):
            for j, chip in enumerate(others):
                def mk(chip_from, it=it, j=j, chip=chip):
                    return pltpu.make_async_remote_copy(
                        src_ref=self._src_half(src_refs, it, c), dst_ref=self._region(land_refs, it, chip_from, c),
                        send_sem=ssem.at[3 * it + j], recv_sem=rsem.at[3 * it + j], device_id=(*chip, c),
                        device_id_type=MESH)
                pairs.append((mk(2 * x + y), mk(2 * chip[0] + chip[1])))
        return pairs

    def start(self):
        chip = 2 * lax.axis_index("x") + lax.axis_index("y")
        lands = []
        for shape, a, ax in zip(self.full_shapes, self.shards, self.axes):
            at = (chip * a.shape[0], 0) if ax == 0 else (0, chip * a.shape[1])
            lands.append(lax.dynamic_update_slice(lax.empty(shape, a.dtype), a, at))
        self.ssem, self.rsem, self.srcs, self.lands, token = _split_start(
            self.shards, lands, self._ici, 3 * self.n, name="gather_start_" + self.tag)
        return token

    def finish(self, after):
        _, lands = _split_wait(self.srcs, self.lands, self.ssem, self.rsem, after, self._ici,
                               name="gather_wait_" + self.tag)
        n = self.n
        n_fwd = 3 * sum(self.splits)
        if n_fwd == 0:
            return lands

        def body(*refs):
            out_refs = refs[n:2 * n]
            fsend, frecv = refs[2 * n:]
            x, y, c, others = _mesh_place()
            sibling = (x, y, 1 - c)

            def fwd(it, slot, chip, half):
                reg = self._region(out_refs, it, 2 * chip[0] + chip[1], half)
                return pltpu.make_async_remote_copy(src_ref=reg, dst_ref=reg, send_sem=fsend.at[slot],
                                                    recv_sem=frecv.at[slot], device_id=sibling, device_id_type=MESH)

            sends, recvs = [], []
            for it in range(n):
                if self.splits[it]:
                    for chip in others:
                        sends.append(fwd(it, len(sends), chip, c))
                        recvs.append(fwd(it, len(recvs), chip, 1 - c))
            for cp in sends:
                cp.start()
            for cp in recvs:
                cp.wait_recv()
            for cp in sends:
                cp.wait_send()

        fulls = pl.pallas_call(
            body, name="gather_d2d_" + self.tag, in_specs=[ANY_SPEC] * n, out_specs=[ANY_SPEC] * n,
            out_shape=[jax.ShapeDtypeStruct(a.shape, a.dtype) for a in lands],
            scratch_shapes=[pltpu.SemaphoreType.DMA((n_fwd,)), pltpu.SemaphoreType.DMA((n_fwd,))],
            input_output_aliases={i: i for i in range(n)},
        )(*lands)
        return list(fulls)


class _GradGather:
    def __init__(self, grads, axes, tag):
        self.grads, self.axes, self.tag = list(grads), axes, tag
        self.n = len(grads)
        self.shard_shapes = [(g.shape[0] // N_CHIP, g.shape[1]) if ax == 0 else (g.shape[0], g.shape[1] // N_CHIP)
                             for g, ax in zip(grads, axes)]

    def _piece(self, src_refs, it, chip):
        r, w = self.shard_shapes[it]
        if self.axes[it] == 0:
            return src_refs[it].at[pl.ds(pl.multiple_of(chip * r, 16), r), :]
        return src_refs[it].at[:, pl.ds(pl.multiple_of(chip * w, LANE), w)]

    PER_ITEM = 4

    def _remote(self, src_refs, land_refs, ssem, rsem):
        x, y, c, others = _mesh_place()
        me = 4 * x + 2 * y + c
        pairs = []
        for it in range(self.n):
            def mk(k, piece_chip, slot, to, it=it):
                return pltpu.make_async_remote_copy(
                    src_ref=self._piece(src_refs, it, piece_chip), dst_ref=land_refs[it].at[slot],
                    send_sem=ssem.at[self.PER_ITEM * it + k], recv_sem=rsem.at[self.PER_ITEM * it + k], device_id=to,
                    device_id_type=MESH)
            for j, chip in enumerate(others):
                chip_id = 2 * chip[0] + chip[1]
                pairs.append((mk(j, chip_id, me, (*chip, c)), mk(j, chip_id, 2 * chip_id + c, (*chip, c))))
            sibling = (x, y, 1 - c)
            pairs.append((mk(3, 2 * x + y, me, sibling), mk(3, 2 * x + y, 4 * x + 2 * y + 1 - c, sibling)))
        return pairs

    def start(self):
        x, y, c = lax.axis_index("x"), lax.axis_index("y"), lax.axis_index("c")
        lands = []
        for (r, w), g, ax in zip(self.shard_shapes, self.grads, self.axes):
            own = lax.dynamic_slice_in_dim(g, (2 * x + y) * (r if ax == 0 else w), r if ax == 0 else w, axis=ax)
            lands.append(lax.dynamic_update_slice(lax.empty((N_DEV, r, w), g.dtype), own[None], (4 * x + 2 * y + c, 0, 0)))
        self.ssem, self.rsem, self.srcs, self.lands, token = _split_start(
            self.grads, lands, self._remote, self.PER_ITEM * self.n, name="grads_start_" + self.tag)
        return token

    def finish(self, after):
        _, lands = _split_wait(self.srcs, self.lands, self.ssem, self.rsem, after, self._remote,
                               name="grads_wait_" + self.tag)
        n = self.n

        def body(*refs):
            out_refs = refs[n:2 * n]
            fsend, frecv = refs[2 * n:]
            x, y, c, others = _mesh_place()
            sibling = (x, y, 1 - c)

            def fwd(it, j, slot):
                return pltpu.make_async_remote_copy(
                    src_ref=out_refs[it].at[slot], dst_ref=out_refs[it].at[slot], send_sem=fsend.at[3 * it + j],
                    recv_sem=frecv.at[3 * it + j], device_id=sibling, device_id_type=MESH)

            sends = [fwd(it, j, 4 * ch[0] + 2 * ch[1] + c) for it in range(n) for j, ch in enumerate(others)]
            recvs = [fwd(it, j, 4 * ch[0] + 2 * ch[1] + 1 - c) for it in range(n) for j, ch in enumerate(others)]
            for cp in sends:
                cp.start()
            for cp in recvs:
                cp.wait_recv()
            for cp in sends:
                cp.wait_send()

        outs = pl.pallas_call(
            body, name="grads_d2d_" + self.tag, in_specs=[ANY_SPEC] * n, out_specs=[ANY_SPEC] * n,
            out_shape=[jax.ShapeDtypeStruct(a.shape, a.dtype) for a in lands],
            scratch_shapes=[pltpu.SemaphoreType.DMA((3 * n,)), pltpu.SemaphoreType.DMA((3 * n,))],
            input_output_aliases={i: i for i in range(n)},
        )(*lands)
        return list(outs)


def _allreduce_small(vec, *, name):
    R, L = vec.shape

    def body(v_ref, o_ref, buf, send, recv, lsem):
        x, y, c, others = _mesh_place()
        me = 4 * x + 2 * y + c
        sibling = (x, y, 1 - c)

        def copy(k, slot, to, src=None):
            return pltpu.make_async_remote_copy(
                src_ref=buf.at[slot] if src is None else src, dst_ref=buf.at[slot], send_sem=send.at[k],
                recv_sem=recv.at[k], device_id=to, device_id_type=MESH)

        def slot_of(chip, core):
            return 4 * chip[0] + 2 * chip[1] + core

        mine = pltpu.make_async_copy(v_ref, buf.at[me], lsem)
        mine.start()
        first = [copy(0, me, sibling, src=v_ref)]
        first += [copy(1 + j, me, (*chip, c), src=v_ref) for j, chip in enumerate(others)]
        for cp in first:
            cp.start()
        passed = [copy(4 + j, slot_of(chip, c), sibling) for j, chip in enumerate(others)]
        for j, chip in enumerate(others):
            copy(1 + j, slot_of(chip, c), (*chip, c)).wait_recv()
            passed[j].start()
        copy(0, slot_of((x, y), 1 - c), sibling).wait_recv()
        for j, chip in enumerate(others):
            copy(4 + j, slot_of(chip, 1 - c), sibling).wait_recv()
        for cp in first + passed:
            cp.wait_send()
        mine.wait()
        total = buf[0]
        for k in range(1, N_DEV):
            total = total + buf[k]
        o_ref[...] = total

    return pl.pallas_call(
        body, name=name, in_specs=[pl.BlockSpec(memory_space=pltpu.VMEM)],
        out_specs=pl.BlockSpec(memory_space=pltpu.VMEM), out_shape=jax.ShapeDtypeStruct((R, L), F32),
        scratch_shapes=[pltpu.VMEM((N_DEV, R, L), F32), pltpu.SemaphoreType.DMA((7,)), pltpu.SemaphoreType.DMA((7,)),
                        pltpu.SemaphoreType.DMA],
        compiler_params=pltpu.CompilerParams(vmem_limit_bytes=VMEM_LIMIT),
    )(vec)


PACK_ALIGN = 1024


def _pack(arrs, row_multiple):
    flat = []
    for a in arrs:
        f = a.reshape(-1).astype(F32)
        flat.append(jnp.pad(f, (0, (-f.shape[0]) % PACK_ALIGN)))
    v = jnp.concatenate(flat)
    v = jnp.pad(v, (0, (-v.shape[0]) % (LANE * row_multiple)))
    return v.reshape(-1, LANE)


def _unpack(v, shapes):
    flat = v.reshape(-1)
    out, off = [], 0
    for s in shapes:
        size = math.prod(s)
        out.append(flat[off:off + size].reshape(s))
        off += size + (-size) % PACK_ALIGN
    return out


def _tile(dim, target):
    for cand in (1024, 512, 256, 128):
        if cand <= target and dim % cand == 0:
            return cand
    return dim


def _div_tile(dim, cap, mult=LANE):
    best = None
    for cand in range(mult, min(cap, dim) + 1, mult):
        if dim % cand == 0:
            best = cand
    return dim if best is None else best


WEIGHT_NAMES = ('norm1_g', 'w_in', 'ret_g', 'rg_conv_w', 'rg_conv_b', 'rg_wa', 'rg_ba', 'rg_wx', 'rg_bx', 'rg_lambda',
                'w_out', 'norm2_g', 'norm_mem_g', 'xa_wq', 'xa_wk', 'xa_wv', 'xa_wo', 'norm3_g', 'ffn_w_up',
                'ffn_conv_w', 'ffn_conv_b', 'ffn_w_down', 'final_g')
BIG_AXIS = {'w_in': 1, 'w_out': 0, 'xa_wq': 0, 'xa_wk': 0, 'xa_wv': 0, 'xa_wo': 0, 'ffn_w_up': 1, 'ffn_w_down': 0}
SMALL_SHARDED = ('rg_conv_w', 'ffn_conv_w')


def _step(x, mem, positions, loss_target, W, Mo, Vo):
    S, D = x.shape[1], x.shape[2]
    xs, mems, tgt = x[0], mem[0], loss_target[0]
    n_mem = mems.shape[0]
    pos_col = positions.reshape(S, 1)
    chip = 2 * lax.axis_index("x") + lax.axis_index("y")

    big = list(BIG_AXIS)
    shards = {n: W[n][0] for n in big}
    G = {}
    gather_groups = (('w_in', 'rg_conv_w'), ('w_out', 'xa_wq', 'xa_wk', 'xa_wv', 'xa_wo'),
                     ('ffn_w_up', 'ffn_conv_w'), ('ffn_w_down',))
    gathers, tok = [], None
    for gi, names in enumerate(gather_groups):
        srcs = []
        for n in names:
            s = W[n][0] if tok is None else W[n][0] + tok[0, 0]
            srcs.append(s.astype(BF16) if n in BIG_AXIS else s)
        ag = _WeightGather(srcs, [BIG_AXIS.get(n, 1) for n in names], [n in BIG_AXIS for n in names], "g%d" % gi)
        tok = ag.start()
        gathers.append(ag)

    def finish_gather(gi, after):
        G.update(zip(gather_groups[gi], gathers[gi].finish(after)))

    finish_gather(0, tok)
    R = W['ret_g'].shape[1]
    Wl = W['rg_lambda'].shape[1]
    IN = W['w_in'].shape[2] * N_CHIP
    F2 = W['ffn_w_up'].shape[2] * N_CHIP
    F = F2 // 2

    norm1_g, norm2_g, norm3_g = W['norm1_g'] + tok[0, 0], W['norm2_g'], W['norm3_g']
    norm_mem_g, final_g, ret_g = W['norm_mem_g'], W['final_g'].reshape(1, D), W['ret_g']
    rg_cw, rg_cb = G['rg_conv_w'], W['rg_conv_b']
    wa, wx = W['rg_wa'][0], W['rg_wx'][0]
    ba, bx = W['rg_ba'].reshape(1, Wl), W['rg_bx'].reshape(1, Wl)
    lam = W['rg_lambda']
    ffn_cb = W['ffn_conv_b']

    def fwd_mm(a, wname, N, K, **kw):
        return _mm(a, G[wname], mode="nn", M=a.shape[0], N=N, K=K, tm=_tile(a.shape[0], 1024), tn=1024,
                   tk=_div_tile(K, 3072), **kw)

    def fwd_mm_norm(a, wname, res, g, name):
        return _mm(a, G[wname], mode="nn", M=a.shape[0], N=D, K=a.shape[1], tm=512, tn=D, tk=_div_tile(a.shape[1], 2048),
                   out_dtype=F32, res=res, norm_g=g, name=name)

    def bwd_x_mm(d, wname, N, K, **kw):
        return _mm(d, G[wname], mode="nt", M=d.shape[0], N=N, K=K, tm=_tile(d.shape[0], 1024),
                   tn=_div_tile(N, 1024, 256), tk=_div_tile(K, 3072), **kw)

    def bwd_w_mm(a, d, M, N, **kw):
        Ks = a.shape[0]
        return _mm(a, d, mode="tn", M=M, N=N, K=Ks, out_dtype=BF16, tm=_div_tile(M, 1024, 256),
                   tn=_div_tile(N, 1024, 256), tk=_div_tile(Ks, 2048 if d.dtype == BF16 else 1024), **kw)

    xn1 = _rmsnorm_fwd(xs, norm1_g, name="norm1_fwd")
    h = fwd_mm(xn1, 'w_in', IN, D, out_dtype=F32, name="mm_in")
    half = (R // RET_HEADS) // 2
    inv = (ROPE_BASE ** (-jnp.arange(half, dtype=F32) / half)).reshape(1, half)
    cos, sin = _rope_table(pos_col, inv, name="rope_table")
    hl, mix = _lru_fwd(h, rg_cw, rg_cb, wa, ba, wx, bx, lam, name="lru_fwd")
    ret_raw, states, mix = _ret_fwd(h, cos, sin, ret_g, mix, name="ret_fwd")
    finish_gather(1, mix)
    x1, xn2 = fwd_mm_norm(mix, 'w_out', xs, norm2_g, "mm_out")
    memn = _rmsnorm_fwd(mems, norm_mem_g, name="norm_mem_fwd")
    km = fwd_mm(memn, 'xa_wk', D, D, out_dtype=BF16, name="mm_k")
    vm = fwd_mm(memn, 'xa_wv', D, D, out_dtype=BF16, name="mm_v")
    q = fwd_mm(xn2, 'xa_wq', D, D, out_dtype=BF16, name="mm_q")
    o = _xattn_fwd(q, km, vm, name="xattn_fwd")
    x2, xn3 = fwd_mm_norm(o, 'xa_wo', x1, norm3_g, "mm_o")
    finish_gather(2, xn3)
    ffn_cw = G['ffn_conv_w']
    hh = fwd_mm(xn3, 'ffn_w_up', F2, D, out_dtype=F32, name="mm_up")
    act = _ffn_gate_fwd(hh, ffn_cw, ffn_cb, name="ffn_gate_fwd")
    finish_gather(3, act)
    x3 = fwd_mm(act, 'ffn_w_down', D, F, out_dtype=F32, res=x2, name="mm_down")
    dx3, d_final, loss8, dx3h = _final_loss(x3, tgt, final_g, name="final_loss")

    gw = {}
    grad_groups = []

    def start_grads(names, tag):
        gg = _GradGather([gw[n] for n in names], [BIG_AXIS[n] for n in names], tag)
        grad_groups.append((names, gg))
        return gg.start()

    dact = bwd_x_mm(dx3h, 'ffn_w_down', F, D, out_dtype=F32, name="mm_dact")
    gw['ffn_w_down'] = bwd_w_mm(act, dx3h, F, D, name="mm_dw_down")
    dhh_a, dhh_b, gcw_a, gcw_b = _ffn_gate_bwd(hh, dact, ffn_cw, ffn_cb, name="ffn_gate_bwd")
    gw_up = bwd_w_mm(xn3, dhh_a, D, F, out_full=(D, F2), name="mm_dw_up_a")
    gw['ffn_w_up'] = bwd_w_mm(xn3, dhh_b, D, F, out=gw_up, out_off=(0, F), name="mm_dw_up_b")
    tok_a = start_grads(('ffn_w_down', 'ffn_w_up'), "a")
    dxn3 = bwd_x_mm(dhh_a, 'ffn_w_up', D, F, out_dtype=F32, after=tok_a, name="mm_dxn3_a")
    dxn3 = bwd_x_mm(dhh_b, 'ffn_w_up', D, F, out_dtype=F32, b_off=(0, F), res=dxn3, name="mm_dxn3_b")
    dx2, d_norm3, dx2h = _rmsnorm_bwd(x2, dxn3, norm3_g, dx3, name="norm3_bwd", emit_bf16=True)
    Kc = ffn_cw.shape[0]
    d_ffn_cw = jnp.concatenate([gcw_a[:Kc], gcw_b[:Kc]], axis=1)
    d_ffn_cb = jnp.concatenate([gcw_a[Kc:Kc + 1], gcw_b[Kc:Kc + 1]], axis=1)

    d_o = bwd_x_mm(dx2h, 'xa_wo', D, D, out_dtype=BF16, name="mm_do")
    gw['xa_wo'] = bwd_w_mm(o, dx2h, D, D, name="mm_dw_o")
    dq, dk, dv = _xattn_bwd(q, km, vm, d_o, name="xattn_bwd")
    gw['xa_wq'] = bwd_w_mm(xn2, dq, D, D, name="mm_dw_q")
    dxn2 = bwd_x_mm(dq, 'xa_wq', D, D, out_dtype=F32, name="mm_dxn2")
    gw['xa_wk'] = bwd_w_mm(memn, dk, D, D, name="mm_dw_k")
    gw['xa_wv'] = bwd_w_mm(memn, dv, D, D, name="mm_dw_v")
    dmemn = bwd_x_mm(dk, 'xa_wk', D, D, out_dtype=F32, name="mm_dmem_k")
    dmemn = bwd_x_mm(dv, 'xa_wv', D, D, out_dtype=F32, res=dmemn, name="mm_dmem_v")
    _, d_norm_mem = _rmsnorm_bwd(mems, dmemn, norm_mem_g, None, name="norm_mem_bwd")
    dx1, d_norm2, dx1h = _rmsnorm_bwd(x1, dxn2, norm2_g, dx2, name="norm2_bwd", emit_bf16=True)

    gw['w_out'] = bwd_w_mm(mix, dx1h, D, D, name="mm_dw_out")
    tok_b = start_grads(('xa_wo', 'xa_wq', 'xa_wk', 'xa_wv', 'w_out'), "b")
    dmix = bwd_x_mm(dx1h, 'w_out', D, D, out_dtype=F32, after=tok_b, name="mm_dmix")
    dh, d_ret_g = _ret_bwd(h, cos, sin, ret_g, states, ret_raw, dmix, name="ret_bwd")
    dh, d_rcw, d_rcb, d_wa, d_ba, d_wx, d_bx, d_lam = _lru_bwd(
        h, hl, dmix, dh, rg_cw, rg_cb, wa, ba, wx, bx, lam, name="lru_bwd")
    gw['w_in'] = bwd_w_mm(xn1, dh, D, IN, name="mm_dw_in")
    tok_c = start_grads(('w_in',), "c")
    dxn1 = bwd_x_mm(dh, 'w_in', D, IN, out_dtype=F32, after=tok_c, name="mm_dxn1")
    grad_x, d_norm1 = _rmsnorm_bwd(xs, dxn1, norm1_g, dx1, name="norm1_bwd")

    small_parts = {
        'norm1_g': d_norm1, 'ret_g': d_ret_g, 'rg_conv_w': d_rcw[:rg_cw.shape[0]], 'rg_conv_b': d_rcb,
        'rg_wa': d_wa, 'rg_ba': d_ba, 'rg_wx': d_wx, 'rg_bx': d_bx, 'rg_lambda': d_lam, 'norm2_g': d_norm2,
        'norm_mem_g': d_norm_mem, 'norm3_g': d_norm3, 'ffn_conv_w': d_ffn_cw, 'ffn_conv_b': d_ffn_cb,
        'final_g': d_final}
    small = [n for n in WEIGHT_NAMES if n not in BIG_AXIS]
    red_shapes = [(1,)] + [tuple(small_parts[n].shape) for n in small]
    reduced = _allreduce_small(_pack([loss8[0:1, 0:1]] + [small_parts[n] for n in small], SUB), name="allreduce_small")
    red = _unpack(reduced, red_shapes)
    loss = red[0][0]
    g_small = dict(zip(small, red[1:]))
    for n in SMALL_SHARDED:
        w_local = W[n].shape[-1]
        g_small[n] = lax.dynamic_slice_in_dim(g_small[n], chip * w_local, w_local, axis=1)

    out_g, out_d, out_m, out_v = {}, {}, {}, {}
    for names, gg in grad_groups:
        for n, land in zip(names, gg.finish(reduced)):
            g, d, m_new, v_new = _adamw(shards[n], Mo[n][0], Vo[n][0], land, name="adamw_" + n)
            out_g[n], out_d[n], out_m[n], out_v[n] = (t.reshape(W[n].shape) for t in (g, d, m_new, v_new))
    rows = 512
    pk = lambda d: _pack([d[n] for n in small], rows)
    g_pack = _pack([g_small[n] for n in small], rows)
    res_small = _adamw(pk(W), pk(Mo), pk(Vo), g_pack[None], name="adamw_small")
    shapes_small = [tuple(W[n].shape) for n in small]
    for dst, packed in zip((out_g, out_d, out_m, out_v), res_small):
        for n, val in zip(small, _unpack(packed, shapes_small)):
            dst[n] = val
    return (loss, grad_x[None], *[out_g[n] for n in WEIGHT_NAMES], *[out_d[n] for n in WEIGHT_NAMES],
            *[out_m[n] for n in WEIGHT_NAMES], *[out_v[n] for n in WEIGHT_NAMES])


def kernel(x, mem, positions, norm1_g, w_in, ret_g, rg_conv_w, rg_conv_b, rg_wa, rg_ba, rg_wx, rg_bx, rg_lambda, w_out, norm2_g, norm_mem_g, xa_wq, xa_wk, xa_wv, xa_wo, norm3_g, ffn_w_up, ffn_conv_w, ffn_conv_b, ffn_w_down, final_g, loss_target, m_norm1_g, m_w_in, m_ret_g, m_rg_conv_w, m_rg_conv_b, m_rg_wa, m_rg_ba, m_rg_wx, m_rg_bx, m_rg_lambda, m_w_out, m_norm2_g, m_norm_mem_g, m_xa_wq, m_xa_wk, m_xa_wv, m_xa_wo, m_norm3_g, m_ffn_w_up, m_ffn_conv_w, m_ffn_conv_b, m_ffn_w_down, m_final_g, v_norm1_g, v_w_in, v_ret_g, v_rg_conv_w, v_rg_conv_b, v_rg_wa, v_rg_ba, v_rg_wx, v_rg_bx, v_rg_lambda, v_w_out, v_norm2_g, v_norm_mem_g, v_xa_wq, v_xa_wk, v_xa_wv, v_xa_wo, v_norm3_g, v_ffn_w_up, v_ffn_conv_w, v_ffn_conv_b, v_ffn_w_down, v_final_g):
    W = dict(zip(WEIGHT_NAMES, (norm1_g, w_in, ret_g, rg_conv_w, rg_conv_b, rg_wa, rg_ba, rg_wx, rg_bx, rg_lambda, w_out,
                                norm2_g, norm_mem_g, xa_wq, xa_wk, xa_wv, xa_wo, norm3_g, ffn_w_up, ffn_conv_w,
                                ffn_conv_b, ffn_w_down, final_g)))
    Mo = dict(zip(WEIGHT_NAMES, (m_norm1_g, m_w_in, m_ret_g, m_rg_conv_w, m_rg_conv_b, m_rg_wa, m_rg_ba, m_rg_wx, m_rg_bx,
                                 m_rg_lambda, m_w_out, m_norm2_g, m_norm_mem_g, m_xa_wq, m_xa_wk, m_xa_wv, m_xa_wo,
                                 m_norm3_g, m_ffn_w_up, m_ffn_conv_w, m_ffn_conv_b, m_ffn_w_down, m_final_g)))
    Vo = dict(zip(WEIGHT_NAMES, (v_norm1_g, v_w_in, v_ret_g, v_rg_conv_w, v_rg_conv_b, v_rg_wa, v_rg_ba, v_rg_wx, v_rg_bx,
                                 v_rg_lambda, v_w_out, v_norm2_g, v_norm_mem_g, v_xa_wq, v_xa_wk, v_xa_wv, v_xa_wo,
                                 v_norm3_g, v_ffn_w_up, v_ffn_conv_w, v_ffn_conv_b, v_ffn_w_down, v_final_g)))
    return _step(x, mem, positions, loss_target, W, Mo, Vo)
```

```python
import functools
import math

import jax
import jax.numpy as jnp
from jax import lax
from jax.experimental import pallas as pl
from jax.experimental.pallas import tpu as pltpu

F32 = jnp.float32
BF16 = jnp.bfloat16

EPS = 1e-6
RET_HEADS = 4
RET_CHUNK = 128
ROPE_BASE = 10000.0
LRU_BLOCKS = 8
LRU_C = 8.0
XA_HEADS = 4

ADAM_LR = 0.001
ADAM_B1 = 0.9
ADAM_B2 = 0.999
ADAM_EPS = 1e-08
ADAM_WD = 0.01
ADAM_STEP = 10

N_DEV = 8
N_CHIP = 4
MESH = pl.DeviceIdType.MESH
SUB = 8
LANE = 128
VMEM_LIMIT = 56 * 1024 * 1024

NN = ((1,), (0,))
NT = ((1,), (1,))
TN = ((0,), (0,))


def _cparams(sem):
    return pltpu.CompilerParams(dimension_semantics=sem, vmem_limit_bytes=VMEM_LIMIT)


def _sigmoid(v):
    return 1.0 / (1.0 + jnp.exp(-v))


def _bdot(a, b, dims):
    return lax.dot_general(a.astype(BF16), b.astype(BF16), (dims, ((), ())), preferred_element_type=F32)


def _row_iota(shape):
    return lax.broadcasted_iota(jnp.int32, shape, 0)


def _shift_down(v, tail, k):
    if k == 0:
        return v
    r = pltpu.roll(v, k, 0)
    rt = pltpu.roll(tail, k, 0)
    first = jnp.where(_row_iota(rt.shape) < k, rt, r[0:SUB])
    return jnp.concatenate([first, r[SUB:]], axis=0)


def _shift_up(v, head, k):
    if k == 0:
        return v
    n = v.shape[0]
    r = pltpu.roll(v, n - k, 0)
    rh = pltpu.roll(head, SUB - k, 0)
    last = jnp.where(_row_iota(rh.shape) >= SUB - k, rh, r[n - SUB:n])
    return jnp.concatenate([r[:n - SUB], last], axis=0)


def _mm(a, b, *, mode, M, N, K, out_dtype, name, tm=512, tn=512, tk=512, a_off=(0, 0), b_off=(0, 0),
        res=None, out=None, out_off=(0, 0), out_full=None, norm_g=None, after=None):
    tm, tn, tk = min(tm, M), min(tn, N), min(tk, K)
    assert M % tm == 0 and N % tn == 0 and K % tk == 0, (name, M, N, K, tm, tn, tk)
    nk = K // tk
    if mode == "nn":
        a_blk, b_blk, dims = (tm, tk), (tk, tn), NN
        a_map = lambda i, j, k: (i + a_off[0] // tm, k + a_off[1] // tk)
        b_map = lambda i, j, k: (k + b_off[0] // tk, j + b_off[1] // tn)
    elif mode == "nt":
        a_blk, b_blk, dims = (tm, tk), (tn, tk), NT
        a_map = lambda i, j, k: (i + a_off[0] // tm, k + a_off[1] // tk)
        b_map = lambda i, j, k: (j + b_off[0] // tn, k + b_off[1] // tk)
    else:
        a_blk, b_blk, dims = (tk, tm), (tk, tn), TN
        a_map = lambda i, j, k: (k + a_off[0] // tk, i + a_off[1] // tm)
        b_map = lambda i, j, k: (k + b_off[0] // tk, j + b_off[1] // tn)
    for off, blk in ((a_off, a_blk), (b_off, b_blk), (out_off, (tm, tn))):
        assert off[0] % blk[0] == 0 and off[1] % blk[1] == 0, (name, off, blk)
    o_map = lambda i, j, k: (i + out_off[0] // tm, j + out_off[1] // tn)
    has_res, has_out, has_norm, has_after = res is not None, out is not None, norm_g is not None, after is not None
    assert not has_norm or (tn == N and not has_out)

    def body(*refs):
        refs = list(refs)
        a_ref, b_ref = refs[0], refs[1]
        pos = 2
        r_ref = g_ref = n_ref = None
        if has_res:
            r_ref = refs[pos]
            pos += 1
        if has_norm:
            g_ref = refs[pos]
            pos += 1
        pos += has_out + has_after
        o_ref = refs[pos]
        pos += 1
        if has_norm:
            n_ref = refs[pos]
            pos += 1
        acc = refs[pos] if nk > 1 else None
        k = pl.program_id(2)
        part = _bdot(a_ref[...], b_ref[...], dims)

        def finish(total):
            if has_res:
                total = total + r_ref[...].astype(F32)
            o_ref[...] = total.astype(o_ref.dtype)
            if has_norm:
                r = lax.rsqrt(jnp.mean(total * total, axis=-1, keepdims=True) + EPS)
                n_ref[...] = (total * r * g_ref[...]).astype(n_ref.dtype)

        if nk == 1:
            finish(part)
        else:
            @pl.when(k == 0)
            def _():
                acc[...] = part

            @pl.when(k > 0)
            def _():
                acc[...] += part

            @pl.when(k == nk - 1)
            def _():
                finish(acc[...])

    in_specs = [pl.BlockSpec(a_blk, a_map), pl.BlockSpec(b_blk, b_map)]
    args = [a, b]
    if has_res:
        in_specs.append(pl.BlockSpec((tm, tn), lambda i, j, k: (i, j)))
        args.append(res)
    if has_norm:
        in_specs.append(pl.BlockSpec((1, N), lambda i, j, k: (0, 0)))
        args.append(norm_g)
    aliases = {}
    if has_out:
        in_specs.append(pl.BlockSpec(memory_space=pl.ANY))
        aliases = {len(args): 0}
        args.append(out)
        out_shape = jax.ShapeDtypeStruct(out.shape, out.dtype)
    else:
        out_shape = jax.ShapeDtypeStruct((M, N) if out_full is None else out_full, out_dtype)
    if has_after:
        in_specs.append(pl.BlockSpec(memory_space=pl.ANY))
        args.append(after)
    out_specs = pl.BlockSpec((tm, tn), o_map)
    if has_norm:
        out_shape = [out_shape, jax.ShapeDtypeStruct((M, N), BF16)]
        out_specs = [out_specs, pl.BlockSpec((tm, tn), lambda i, j, k: (i, j))]
    return pl.pallas_call(
        body, name=name, grid=(M // tm, N // tn, nk), in_specs=in_specs,
        out_specs=out_specs, out_shape=out_shape,
        scratch_shapes=[pltpu.VMEM((tm, tn), F32)] if nk > 1 else [],
        input_output_aliases=aliases,
        compiler_params=_cparams(("parallel", "parallel", "arbitrary")),
    )(*args)


def _rmsnorm_fwd(x, g, *, name, ts=512):
    S, D = x.shape
    ts = min(ts, S)

    def body(x_ref, g_ref, o_ref):
        xv = x_ref[...]
        r = lax.rsqrt(jnp.mean(xv * xv, axis=-1, keepdims=True) + EPS)
        o_ref[...] = (xv * r * g_ref[...]).astype(o_ref.dtype)

    return pl.pallas_call(
        body, name=name, grid=(S // ts,),
        in_specs=[pl.BlockSpec((ts, D), lambda i: (i, 0)), pl.BlockSpec((1, D), lambda i: (0, 0))],
        out_specs=pl.BlockSpec((ts, D), lambda i: (i, 0)),
        out_shape=jax.ShapeDtypeStruct((S, D), BF16),
        compiler_params=_cparams(("parallel",)),
    )(x, g)


def _rmsnorm_bwd(x, dxn, g, res, *, name, ts=256, emit_bf16=False):
    S, D = x.shape
    ts = min(ts, S)
    has_res = res is not None

    def body(*refs):
        refs = list(refs)
        dx16_ref = refs.pop() if emit_bf16 else None
        if has_res:
            x_ref, d_ref, g_ref, r_ref, dx_ref, dg_ref = refs
        else:
            x_ref, d_ref, g_ref, dx_ref, dg_ref = refs
        i = pl.program_id(0)
        xv = x_ref[...]
        dv = d_ref[...].astype(F32)
        r = lax.rsqrt(jnp.mean(xv * xv, axis=-1, keepdims=True) + EPS)
        gd = dv * g_ref[...]
        proj = jnp.mean(xv * gd, axis=-1, keepdims=True)
        dx = r * gd - xv * (r * r * r) * proj
        if has_res:
            dx = dx + r_ref[...]
        dx_ref[...] = dx
        if emit_bf16:
            dx16_ref[...] = dx.astype(BF16)
        part = jnp.sum(dv * xv * r, axis=0, keepdims=True)

        @pl.when(i == 0)
        def _():
            dg_ref[...] = part

        @pl.when(i > 0)
        def _():
            dg_ref[...] += part

    row = pl.BlockSpec((ts, D), lambda i: (i, 0))
    vec = pl.BlockSpec((1, D), lambda i: (0, 0))
    in_specs = [row, row, vec] + ([row] if has_res else [])
    args = [x, dxn, g] + ([res] if has_res else [])
    extra = emit_bf16 * [jax.ShapeDtypeStruct((S, D), BF16)]
    return pl.pallas_call(
        body, name=name, grid=(S // ts,), in_specs=in_specs, out_specs=[row, vec] + emit_bf16 * [row],
        out_shape=[jax.ShapeDtypeStruct((S, D), F32), jax.ShapeDtypeStruct((1, D), F32)] + extra,
        compiler_params=_cparams(("arbitrary",)),
    )(*args)


def _final_loss(x, target, g, *, name, ts=256):
    S, D = x.shape
    ts = min(ts, S)

    def body(x_ref, t_ref, g_ref, dx_ref, dg_ref, loss_ref, dx16_ref):
        i = pl.program_id(0)
        xv = x_ref[...]
        gv = g_ref[...]
        r = lax.rsqrt(jnp.mean(xv * xv, axis=-1, keepdims=True) + EPS)
        y = xv * r * gv
        err = y - t_ref[...]
        row_loss = jnp.mean(err * err, axis=-1, keepdims=True)
        lpart = 0.5 * jnp.sum(row_loss, axis=0, keepdims=True)
        dy = err * (1.0 / D)
        gd = dy * gv
        proj = jnp.mean(xv * gd, axis=-1, keepdims=True)
        dx = r * gd - xv * (r * r * r) * proj
        dx_ref[...] = dx
        dx16_ref[...] = dx.astype(BF16)
        part = jnp.sum(dy * xv * r, axis=0, keepdims=True)
        lfull = jnp.broadcast_to(lpart, loss_ref.shape)

        @pl.when(i == 0)
        def _():
            dg_ref[...] = part
            loss_ref[...] = lfull

        @pl.when(i > 0)
        def _():
            dg_ref[...] += part
            loss_ref[...] += lfull

    row = pl.BlockSpec((ts, D), lambda i: (i, 0))
    vec = pl.BlockSpec((1, D), lambda i: (0, 0))
    return pl.pallas_call(
        body, name=name, grid=(S // ts,), in_specs=[row, row, vec],
        out_specs=[row, vec, pl.BlockSpec((SUB, LANE), lambda i: (0, 0)), row],
        out_shape=[jax.ShapeDtypeStruct((S, D), F32), jax.ShapeDtypeStruct((1, D), F32),
                   jax.ShapeDtypeStruct((SUB, LANE), F32), jax.ShapeDtypeStruct((S, D), BF16)],
        compiler_params=_cparams(("arbitrary",)),
    )(x, target, g)


def _rope_table(pos_col, inv, *, name, ts=1024):
    S = pos_col.shape[0]
    ts = min(ts, S)
    half = inv.shape[1]

    def body(p_ref, inv_ref, c_ref, s_ref):
        ang = p_ref[...].astype(F32) * inv_ref[...]
        c_ref[...] = jnp.cos(ang)
        s_ref[...] = jnp.sin(ang)

    tab = pl.BlockSpec((ts, half), lambda i: (i, 0))
    return pl.pallas_call(
        body, name=name, grid=(S // ts,),
        in_specs=[pl.BlockSpec((ts, 1), lambda i: (i, 0)), pl.BlockSpec((1, half), lambda i: (0, 0))],
        out_specs=[tab, tab],
        out_shape=[jax.ShapeDtypeStruct((S, half), F32), jax.ShapeDtypeStruct((S, half), F32)],
        compiler_params=_cparams(("parallel",)),
    )(pos_col, inv)


def _ret_consts(C, log_g):
    ii = lax.broadcasted_iota(jnp.int32, (C, C), 0)
    jj = lax.broadcasted_iota(jnp.int32, (C, C), 1)
    diff = (ii - jj).astype(F32)
    intra = jnp.where(ii >= jj, jnp.exp(log_g * jnp.maximum(diff, 0.0)), 0.0)
    idx = lax.broadcasted_iota(jnp.int32, (C, 1), 0).astype(F32)
    qd = jnp.exp(log_g * (idx + 1.0))
    kd = jnp.exp(log_g * (C - 1.0 - idx))
    cd = math.exp(log_g * C)
    return intra, qd, kd, cd


def _rot(t, cs, sn):
    half = t.shape[-1] // 2
    t1, t2 = t[:, :half], t[:, half:]
    return jnp.concatenate([t1 * cs - t2 * sn, t1 * sn + t2 * cs], axis=-1)


def _unrot(d, cs, sn):
    half = d.shape[-1] // 2
    d1, d2 = d[:, :half], d[:, half:]
    return jnp.concatenate([d1 * cs + d2 * sn, d2 * cs - d1 * sn], axis=-1)


def _ret_fwd(h, cos, sin, ret_g, mix, *, name, ch=2):
    S = h.shape[0]
    R = ret_g.shape[1]
    H, C = RET_HEADS, RET_CHUNK
    Dh = R // H
    ts = ch * C
    assert S % ts == 0
    log_gs = [math.log(1.0 - 2.0 ** (-5.0 - hd)) for hd in range(H)]
    scale = Dh ** -0.5

    def body(x_ref, c_ref, s_ref, rg_ref, mix_in, ret_ref, st_ref, mix_ref, state):
        i = pl.program_id(0)

        @pl.when(i == 0)
        def _():
            state[...] = jnp.zeros_like(state)

        for c in range(ch):
            rows = pl.ds(c * C, C)
            cs, sn = c_ref[rows, :], s_ref[rows, :]
            for hd in range(H):
                intra, qd, kd, cd = _ret_consts(C, log_gs[hd])
                q = x_ref[rows, pl.ds(hd * Dh, Dh)]
                k = x_ref[rows, pl.ds(R + hd * Dh, Dh)]
                v = x_ref[rows, pl.ds(2 * R + hd * Dh, Dh)]
                g = x_ref[rows, pl.ds(3 * R + hd * Dh, Dh)]
                rq = _rot(q, cs, sn)
                rk = _rot(k, cs, sn) * scale
                st = state[hd]
                st_ref[c, hd] = st.astype(BF16)
                s_ = _bdot(rq, rk, NT) * intra
                ret = _bdot(s_, v, NN) + _bdot(rq * qd, st, NN)
                state[hd] = st * cd + _bdot(rk * kd, v, TN)
                ret_ref[rows, pl.ds(hd * Dh, Dh)] = ret
                rr = lax.rsqrt(jnp.mean(ret * ret, axis=-1, keepdims=True) + EPS)
                out = ret * rr * rg_ref[:, pl.ds(hd * Dh, Dh)] * (g * _sigmoid(g))
                mix_ref[rows, pl.ds(hd * Dh, Dh)] = out.astype(BF16)

    n_chunks = S // C
    return pl.pallas_call(
        body, name=name, grid=(S // ts,),
        in_specs=[pl.BlockSpec((ts, 4 * R), lambda i: (i, 0)),
                  pl.BlockSpec((ts, Dh // 2), lambda i: (i, 0)), pl.BlockSpec((ts, Dh // 2), lambda i: (i, 0)),
                  pl.BlockSpec((1, R), lambda i: (0, 0)), pl.BlockSpec(memory_space=pl.ANY)],
        out_specs=[pl.BlockSpec((ts, R), lambda i: (i, 0)),
                   pl.BlockSpec((ch, H, Dh, Dh), lambda i: (i, 0, 0, 0)),
                   pl.BlockSpec((ts, R), lambda i: (i, 0))],
        out_shape=[jax.ShapeDtypeStruct((S, R), F32), jax.ShapeDtypeStruct((n_chunks, H, Dh, Dh), BF16),
                   jax.ShapeDtypeStruct(mix.shape, mix.dtype)],
        scratch_shapes=[pltpu.VMEM((H, Dh, Dh), F32)],
        input_output_aliases={4: 2},
        compiler_params=_cparams(("arbitrary",)),
    )(h, cos, sin, ret_g, mix)


def _ret_bwd(h, cos, sin, ret_g, states, ret_raw, dmix, *, name, ch=2):
    S = h.shape[0]
    R = ret_g.shape[1]
    H, C = RET_HEADS, RET_CHUNK
    Dh = R // H
    ts = ch * C
    nb = S // ts
    log_gs = [math.log(1.0 - 2.0 ** (-5.0 - hd)) for hd in range(H)]
    scale = Dh ** -0.5

    def body(x_ref, c_ref, s_ref, rg_ref, st_ref, ret_ref, dm_ref, dh_ref, drg_ref, dstate):
        i = pl.program_id(0)

        @pl.when(i == 0)
        def _():
            dstate[...] = jnp.zeros_like(dstate)
            drg_ref[...] = jnp.zeros_like(drg_ref)

        for c in reversed(range(ch)):
            rows = pl.ds(c * C, C)
            cs, sn = c_ref[rows, :], s_ref[rows, :]
            for hd in range(H):
                intra, qd, kd, cd = _ret_consts(C, log_gs[hd])
                cols = pl.ds(hd * Dh, Dh)
                q = x_ref[rows, pl.ds(hd * Dh, Dh)]
                k = x_ref[rows, pl.ds(R + hd * Dh, Dh)]
                v = x_ref[rows, pl.ds(2 * R + hd * Dh, Dh)]
                g = x_ref[rows, pl.ds(3 * R + hd * Dh, Dh)]
                rq = _rot(q, cs, sn)
                rk = _rot(k, cs, sn) * scale
                ret = ret_ref[rows, cols]
                dm = dm_ref[rows, cols]
                rgv = rg_ref[:, cols]
                rr = lax.rsqrt(jnp.mean(ret * ret, axis=-1, keepdims=True) + EPS)
                retn = ret * rr
                sg = _sigmoid(g)
                silu = g * sg
                drg_ref[:, cols] += jnp.sum(dm * retn * silu, axis=0, keepdims=True)
                dg = dm * retn * rgv * (sg * (1.0 + g * (1.0 - sg)))
                dretn = dm * rgv * silu
                d_o = rr * dretn - ret * (rr * rr * rr) * jnp.mean(ret * dretn, axis=-1, keepdims=True)
                st = st_ref[c, hd]
                d_s = dstate[hd]
                a_ = _bdot(rq, rk, NT) * intra
                d_a = _bdot(d_o, v, NT) * intra
                d_qr = _bdot(d_a, rk, NN) + _bdot(d_o, st, NT) * qd
                d_kr = _bdot(d_a, rq, TN) + _bdot(v, d_s, NT) * kd
                d_v = _bdot(a_, d_o, TN) + _bdot(rk * kd, d_s, NN)
                dstate[hd] = d_s * cd + _bdot(rq * qd, d_o, TN)
                dh_ref[rows, pl.ds(hd * Dh, Dh)] = _unrot(d_qr, cs, sn).astype(BF16)
                dh_ref[rows, pl.ds(R + hd * Dh, Dh)] = (_unrot(d_kr, cs, sn) * scale).astype(BF16)
                dh_ref[rows, pl.ds(2 * R + hd * Dh, Dh)] = d_v.astype(BF16)
                dh_ref[rows, pl.ds(3 * R + hd * Dh, Dh)] = dg.astype(BF16)

    rb = lambda i: nb - 1 - i
    return pl.pallas_call(
        body, name=name, grid=(nb,),
        in_specs=[pl.BlockSpec((ts, 4 * R), lambda i: (rb(i), 0)),
                  pl.BlockSpec((ts, Dh // 2), lambda i: (rb(i), 0)), pl.BlockSpec((ts, Dh // 2), lambda i: (rb(i), 0)),
                  pl.BlockSpec((1, R), lambda i: (0, 0)),
                  pl.BlockSpec((ch, H, Dh, Dh), lambda i: (rb(i), 0, 0, 0)),
                  pl.BlockSpec((ts, R), lambda i: (rb(i), 0)),
                  pl.BlockSpec((ts, R), lambda i: (rb(i), 0))],
        out_specs=[pl.BlockSpec((ts, 4 * R), lambda i: (rb(i), 0)), pl.BlockSpec((1, R), lambda i: (0, 0))],
        out_shape=[jax.ShapeDtypeStruct((S, 6 * R), BF16), jax.ShapeDtypeStruct((1, R), F32)],
        scratch_shapes=[pltpu.VMEM((H, Dh, Dh), F32)],
        compiler_params=_cparams(("arbitrary",)),
    )(h, cos, sin, ret_g, states, ret_raw, dmix)


GELU_C = math.sqrt(2.0 / math.pi)
GELU_A = 0.044715


def _gelu_parts(y):
    t = jnp.tanh(GELU_C * (y + GELU_A * y * y * y))
    val = 0.5 * y * (1.0 + t)
    grad = 0.5 * (1.0 + t) + 0.5 * y * (1.0 - t * t) * GELU_C * (1.0 + 3.0 * GELU_A * y * y)
    return val, grad


def _neg_expm1(x):
    series = -x * (1.0 + x * (1.0 / 2.0) * (1.0 + x * (1.0 / 3.0) * (1.0 + x * (1.0 / 4.0) * (
        1.0 + x * (1.0 / 5.0) * (1.0 + x * (1.0 / 6.0) * (1.0 + x * (1.0 / 7.0)))))))
    return jnp.where(x > -0.35, series, 1.0 - jnp.exp(x))


def _log_sigmoid(x):
    return jnp.minimum(x, 0.0) - jnp.log1p(jnp.exp(-jnp.abs(x)))


def _lru_gates(uc, wa_ref, ba_ref, wx_ref, bx_ref):
    nbk = wa_ref.shape[0]
    bd = wa_ref.shape[1]
    rs, gs = [], []
    for n in range(nbk):
        ucn = uc[:, n * bd:(n + 1) * bd]
        rs.append(_sigmoid(_bdot(ucn, wa_ref[n], NN) + ba_ref[:, pl.ds(n * bd, bd)]))
        gs.append(_sigmoid(_bdot(ucn, wx_ref[n], NN) + bx_ref[:, pl.ds(n * bd, bd)]))
    return jnp.concatenate(rs, axis=-1), jnp.concatenate(gs, axis=-1)


def _lru_fwd(h, conv_w, conv_b, wa, ba, wx, bx, lam, *, name, ts=256):
    S = h.shape[0]
    W = lam.shape[1]
    K = conv_w.shape[0]
    ts = min(ts, S)

    def body(u_ref, y_ref, cw_ref, cb_ref, wa_ref, ba_ref, wx_ref, bx_ref, lam_ref, hl_ref, mix_ref, tail, hlast):
        i = pl.program_id(0)

        @pl.when(i == 0)
        def _():
            tail[...] = jnp.zeros_like(tail)
            hlast[...] = jnp.zeros_like(hlast)

        u = u_ref[...]
        tl = tail[...]
        uc = cb_ref[...] + cw_ref[K - 1:K, :] * u
        for k in range(K - 1):
            uc = uc + cw_ref[k:k + 1, :] * _shift_down(u, tl, K - 1 - k)
        tail[...] = u[ts - SUB:ts]
        r, ig = _lru_gates(uc, wa_ref, ba_ref, wx_ref, bx_ref)
        log_a = LRU_C * r * _log_sigmoid(lam_ref[...])
        a = jnp.exp(log_a)
        b = jnp.sqrt(_neg_expm1(2.0 * log_a)) * (ig * uc)
        rid = _row_iota((ts, W))
        d = 1
        while d < ts:
            a_s = jnp.where(rid < d, 1.0, pltpu.roll(a, d, 0))
            b_s = jnp.where(rid < d, 0.0, pltpu.roll(b, d, 0))
            b = a * b_s + b
            a = a * a_s
            d *= 2
        hcur = a * hlast[SUB - 1:SUB, :] + b
        hlast[...] = hcur[ts - SUB:ts]
        hl_ref[...] = hcur
        gy, _ = _gelu_parts(y_ref[...])
        mix_ref[...] = (hcur * gy).astype(BF16)

    full = lambda shape: pl.BlockSpec(shape, lambda i: tuple(0 for _ in shape))
    return pl.pallas_call(
        body, name=name, grid=(S // ts,),
        in_specs=[pl.BlockSpec((ts, W), lambda i: (i, 4)), pl.BlockSpec((ts, W), lambda i: (i, 5)),
                  full(conv_w.shape), full(conv_b.shape), full(wa.shape), full(ba.shape), full(wx.shape),
                  full(bx.shape), full(lam.shape)],
        out_specs=[pl.BlockSpec((ts, W), lambda i: (i, 0)), pl.BlockSpec((ts, W), lambda i: (i, 1))],
        out_shape=[jax.ShapeDtypeStruct((S, W), F32), jax.ShapeDtypeStruct((S, 2 * W), BF16)],
        scratch_shapes=[pltpu.VMEM((SUB, W), F32), pltpu.VMEM((SUB, W), F32)],
        compiler_params=_cparams(("arbitrary",)),
    )(h, h, conv_w, conv_b, wa, ba, wx, bx, lam)


def _lru_bwd(h, hl, dmix, dh, conv_w, conv_b, wa, ba, wx, bx, lam, *, name, ts=256):
    S = h.shape[0]
    W = lam.shape[1]
    K = conv_w.shape[0]
    nbk, bd = wa.shape[0], wa.shape[1]
    ts = min(ts, S)
    nb = S // ts
    t8 = ts // SUB

    def body(u_ref, y_ref, uh_ref, hl_ref, hh_ref, dm_ref, cw_ref, cb_ref, wa_ref, ba_ref, wx_ref, bx_ref, lam_ref,
             dh_in, dh_ref, dcw_ref, dcb_ref, dwa_ref, dba_ref, dwx_ref, dbx_ref, dlam_ref, carry, head):
        i = pl.program_id(0)
        blk = nb - 1 - i

        @pl.when(i == 0)
        def _():
            carry[...] = jnp.zeros_like(carry)
            head[...] = jnp.zeros_like(head)
            for ref in (dcw_ref, dcb_ref, dwa_ref, dba_ref, dwx_ref, dbx_ref, dlam_ref):
                ref[...] = jnp.zeros_like(ref)

        inside = (blk > 0).astype(F32)
        u = u_ref[...]
        tl = uh_ref[...] * inside
        sh = [_shift_down(u, tl, K - 1 - k) for k in range(K)]
        uc = cb_ref[...]
        for k in range(K):
            uc = uc + cw_ref[k:k + 1, :] * sh[k]
        r, ig = _lru_gates(uc, wa_ref, ba_ref, wx_ref, bx_ref)
        lam_v = lam_ref[...]
        ls = _log_sigmoid(lam_v)
        log_a = LRU_C * r * ls
        a = jnp.exp(log_a)
        mult = jnp.sqrt(_neg_expm1(2.0 * log_a))
        hcur = hl_ref[...]
        hprev = _shift_down(hcur, hh_ref[...] * inside, 1)
        gy, dgy = _gelu_parts(y_ref[...])
        dm = dm_ref[...]
        d_y = dm * hcur * dgy
        rid = _row_iota((ts, W))
        bq = dm * gy + jnp.where(rid == ts - 1, carry[0:1, :], 0.0)
        aq = jnp.where(rid == ts - 1, 0.0, pltpu.roll(a, ts - 1, 0))
        d = 1
        while d < ts:
            a_s = jnp.where(rid >= ts - d, 0.0, pltpu.roll(aq, ts - d, 0))
            b_s = jnp.where(rid >= ts - d, 0.0, pltpu.roll(bq, ts - d, 0))
            bq = bq + aq * b_s
            aq = aq * a_s
            d *= 2
        lam_t = bq
        carry[...] = (a * lam_t)[0:SUB]
        d_a = lam_t * hprev
        d_mult = lam_t * (ig * uc)
        d_i = lam_t * mult * uc
        d_uc = lam_t * mult * ig
        d_log_a = d_a * a - d_mult * (a * a) / mult
        d_r = d_log_a * (LRU_C * ls)
        dlam_ref[...] += jnp.sum(d_log_a * (LRU_C * r), axis=0, keepdims=True) * _sigmoid(-lam_v)
        d_pr = d_r * r * (1.0 - r)
        d_pi = d_i * ig * (1.0 - ig)
        dba_ref[...] += jnp.sum(d_pr, axis=0, keepdims=True)
        dbx_ref[...] += jnp.sum(d_pi, axis=0, keepdims=True)
        extra = []
        for n in range(nbk):
            sl = slice(n * bd, (n + 1) * bd)
            ucn = uc[:, sl]
            dwa_ref[n] += _bdot(ucn, d_pr[:, sl], TN)
            dwx_ref[n] += _bdot(ucn, d_pi[:, sl], TN)
            extra.append(_bdot(d_pr[:, sl], wa_ref[n], NT) + _bdot(d_pi[:, sl], wx_ref[n], NT))
        d_uc = d_uc + jnp.concatenate(extra, axis=-1)
        dcb_ref[...] += jnp.sum(d_uc, axis=0, keepdims=True)
        rid8 = _row_iota((SUB, W))
        dcw = jnp.zeros((SUB, W), F32)
        for k in range(K):
            dcw = dcw + jnp.where(rid8 == k, jnp.sum(d_uc * sh[k], axis=0, keepdims=True), 0.0)
        dcw_ref[...] += dcw
        hd = head[...]
        d_u = cw_ref[K - 1:K, :] * d_uc
        for j in range(1, K):
            d_u = d_u + cw_ref[K - 1 - j:K - j, :] * _shift_up(d_uc, hd, j)
        head[...] = d_uc[0:SUB]
        dh_ref[:, 0:W] = d_u.astype(BF16)
        dh_ref[:, W:2 * W] = d_y.astype(BF16)

    rb = lambda i: nb - 1 - i
    prev8 = lambda i: jnp.maximum(rb(i) * t8 - 1, 0)
    full = lambda shape: pl.BlockSpec(shape, lambda i: tuple(0 for _ in shape))
    small = [jax.ShapeDtypeStruct((SUB, W), F32), jax.ShapeDtypeStruct((1, W), F32),
             jax.ShapeDtypeStruct(wa.shape, F32), jax.ShapeDtypeStruct((1, W), F32),
             jax.ShapeDtypeStruct(wx.shape, F32), jax.ShapeDtypeStruct((1, W), F32),
             jax.ShapeDtypeStruct((1, W), F32)]
    return pl.pallas_call(
        body, name=name, grid=(nb,),
        in_specs=[pl.BlockSpec((ts, W), lambda i: (rb(i), 4)), pl.BlockSpec((ts, W), lambda i: (rb(i), 5)),
                  pl.BlockSpec((SUB, W), lambda i: (prev8(i), 4)),
                  pl.BlockSpec((ts, W), lambda i: (rb(i), 0)), pl.BlockSpec((SUB, W), lambda i: (prev8(i), 0)),
                  pl.BlockSpec((ts, W), lambda i: (rb(i), 1)),
                  full(conv_w.shape), full(conv_b.shape), full(wa.shape), full(ba.shape), full(wx.shape),
                  full(bx.shape), full(lam.shape), pl.BlockSpec(memory_space=pl.ANY)],
        out_specs=[pl.BlockSpec((ts, 2 * W), lambda i: (rb(i), 2))] + [full(s.shape) for s in small],
        out_shape=[jax.ShapeDtypeStruct(dh.shape, dh.dtype)] + small,
        scratch_shapes=[pltpu.VMEM((SUB, W), F32), pltpu.VMEM((SUB, W), F32)],
        input_output_aliases={13: 0},
        compiler_params=_cparams(("arbitrary",)),
    )(h, h, h, hl, hl, dmix, conv_w, conv_b, wa, ba, wx, bx, lam, dh)


def _xattn_fwd(q, km, vm, *, name, ts=512):
    S, D = q.shape
    M = km.shape[0]
    H = XA_HEADS
    Dh = D // H
    ts = min(ts, S)
    scale = Dh ** -0.5

    def body(q_ref, k_ref, v_ref, o_ref):
        for hd in range(H):
            cols = pl.ds(hd * Dh, Dh)
            s = _bdot(q_ref[:, cols], k_ref[:, cols], NT) * scale
            s = s - jnp.max(s, axis=-1, keepdims=True)
            e = jnp.exp(s)
            p = e / jnp.sum(e, axis=-1, keepdims=True)
            o_ref[:, cols] = _bdot(p, v_ref[:, cols], NN).astype(o_ref.dtype)

    return pl.pallas_call(
        body, name=name, grid=(S // ts,),
        in_specs=[pl.BlockSpec((ts, D), lambda i: (i, 0)), pl.BlockSpec((M, D), lambda i: (0, 0)),
                  pl.BlockSpec((M, D), lambda i: (0, 0))],
        out_specs=pl.BlockSpec((ts, D), lambda i: (i, 0)),
        out_shape=jax.ShapeDtypeStruct((S, D), BF16),
        compiler_params=_cparams(("parallel",)),
    )(q, km, vm)


def _xattn_bwd(q, km, vm, d_o, *, name, ts=512):
    S, D = q.shape
    M = km.shape[0]
    H = XA_HEADS
    Dh = D // H
    ts = min(ts, S)
    scale = Dh ** -0.5

    def body(q_ref, k_ref, v_ref, do_ref, dq_ref, dk_ref, dv_ref):
        i = pl.program_id(0)

        @pl.when(i == 0)
        def _():
            dk_ref[...] = jnp.zeros_like(dk_ref)
            dv_ref[...] = jnp.zeros_like(dv_ref)

        for hd in range(H):
            cols = pl.ds(hd * Dh, Dh)
            qh, kh, vh, doh = q_ref[:, cols], k_ref[:, cols], v_ref[:, cols], do_ref[:, cols]
            s = _bdot(qh, kh, NT) * scale
            s = s - jnp.max(s, axis=-1, keepdims=True)
            e = jnp.exp(s)
            p = e / jnp.sum(e, axis=-1, keepdims=True)
            dp = _bdot(doh, vh, NT)
            ds = p * (dp - jnp.sum(dp * p, axis=-1, keepdims=True)) * scale
            dq_ref[:, cols] = _bdot(ds, kh, NN).astype(dq_ref.dtype)
            dk_ref[:, cols] += _bdot(ds, qh, TN)
            dv_ref[:, cols] += _bdot(p, doh, TN)

    row = pl.BlockSpec((ts, D), lambda i: (i, 0))
    mem = pl.BlockSpec((M, D), lambda i: (0, 0))
    return pl.pallas_call(
        body, name=name, grid=(S // ts,), in_specs=[row, mem, mem, row], out_specs=[row, mem, mem],
        out_shape=[jax.ShapeDtypeStruct((S, D), BF16), jax.ShapeDtypeStruct((M, D), F32),
                   jax.ShapeDtypeStruct((M, D), F32)],
        compiler_params=_cparams(("arbitrary",)),
    )(q, km, vm, d_o)


def _conv_rows(v, tail, cw_ref, cb_ref):
    K = cw_ref.shape[0]
    sh = [_shift_down(v, tail, K - 1 - k) for k in range(K)]
    out = cb_ref[...]
    for k in range(K):
        out = out + cw_ref[k:k + 1, :] * sh[k]
    return out, sh


FFN_SUB = 256


def _ffn_up_gate(xn, w_up, cw, cb, *, name, tm=1024, tn=512):
    S, D = xn.shape
    F2 = w_up.shape[1]
    F = F2 // 2
    tm, tn = min(tm, S), min(tn, F)
    sub = min(FFN_SUB, tm)
    nj = F // tn
    K = cw.shape[0]

    def body(x_ref, wa_ref, wb_ref, cwa_ref, cwb_ref, cba_ref, cbb_ref, act_ref, ha_ref, hb_ref, ta, tb):
        i = pl.program_id(1)

        @pl.when(i == 0)
        def _():
            ta[...] = jnp.zeros_like(ta)
            tb[...] = jnp.zeros_like(tb)

        tail_a, tail_b = ta[...], tb[...]
        for s in range(tm // sub):
            rows = pl.ds(s * sub, sub)
            xs = x_ref[rows, :]
            ha = _bdot(xs, wa_ref[...], NN)
            hb = _bdot(xs, wb_ref[...], NN)
            ac, _ = _conv_rows(ha, tail_a, cwa_ref, cba_ref)
            bc, _ = _conv_rows(hb, tail_b, cwb_ref, cbb_ref)
            tail_a, tail_b = ha[sub - SUB:sub], hb[sub - SUB:sub]
            ha_ref[rows, :] = ha
            hb_ref[rows, :] = hb
            act_ref[rows, :] = (ac * _sigmoid(ac) * bc).astype(act_ref.dtype)
        ta[...] = tail_a
        tb[...] = tail_b

    blk = pl.BlockSpec((tm, tn), lambda j, i: (i, j))
    return pl.pallas_call(
        body, name=name, grid=(nj, S // tm),
        in_specs=[pl.BlockSpec((tm, D), lambda j, i: (i, 0)),
                  pl.BlockSpec((D, tn), lambda j, i: (0, j)), pl.BlockSpec((D, tn), lambda j, i: (0, j + nj)),
                  pl.BlockSpec((K, tn), lambda j, i: (0, j)), pl.BlockSpec((K, tn), lambda j, i: (0, j + nj)),
                  pl.BlockSpec((1, tn), lambda j, i: (0, j)), pl.BlockSpec((1, tn), lambda j, i: (0, j + nj))],
        out_specs=[blk, blk, blk],
        out_shape=[jax.ShapeDtypeStruct((S, F), BF16), jax.ShapeDtypeStruct((S, F), F32),
                   jax.ShapeDtypeStruct((S, F), F32)],
        scratch_shapes=[pltpu.VMEM((SUB, tn), F32), pltpu.VMEM((SUB, tn), F32)],
        compiler_params=_cparams(("parallel", "arbitrary")),
    )(xn, w_up, w_up, cw, cw, cb, cb)


def _ffn_gate_bwd(dx, w_down, hh_a, hh_b, cw, cb, *, name, tm=1024, tn=512):
    S, D = dx.shape
    F = hh_a.shape[1]
    tm, tn = min(tm, S), min(tn, F)
    sub = min(FFN_SUB, tm)
    nj = F // tn
    nb = S // tm
    t8 = tm // SUB
    K = cw.shape[0]

    def body(dx_ref, wd_ref, a_ref, ah_ref, b_ref, bh_ref, cwa_ref, cwb_ref, cba_ref, cbb_ref,
             da_ref, db_ref, ga_ref, gb_ref, ha, hb):
        i = pl.program_id(1)
        blk = nb - 1 - i

        @pl.when(i == 0)
        def _():
            for ref in (ha, hb, ga_ref, gb_ref):
                ref[...] = jnp.zeros_like(ref)

        inside = (blk > 0).astype(F32)
        rid8 = _row_iota((SUB, tn))
        heads = [ha[...], hb[...]]
        gsums = [jnp.zeros((SUB, tn), F32), jnp.zeros((SUB, tn), F32)]
        for s in reversed(range(tm // sub)):
            rows = pl.ds(s * sub, sub)
            before = pl.ds(s * sub - SUB, SUB)
            dv = _bdot(dx_ref[rows, :], wd_ref[...], NT)
            tail_a = a_ref[before, :] if s > 0 else ah_ref[...] * inside
            tail_b = b_ref[before, :] if s > 0 else bh_ref[...] * inside
            ac, sha = _conv_rows(a_ref[rows, :], tail_a, cwa_ref, cba_ref)
            bc, shb = _conv_rows(b_ref[rows, :], tail_b, cwb_ref, cbb_ref)
            sg = _sigmoid(ac)
            d_bc = dv * ac * sg
            d_ac = dv * bc * sg * (1.0 + ac * (1.0 - sg))
            for which, (d_c, sh, cw_ref, o_ref) in enumerate(((d_ac, sha, cwa_ref, da_ref), (d_bc, shb, cwb_ref, db_ref))):
                d_in = cw_ref[K - 1:K, :] * d_c
                for j in range(1, K):
                    d_in = d_in + cw_ref[K - 1 - j:K - j, :] * _shift_up(d_c, heads[which], j)
                heads[which] = d_c[0:SUB]
                o_ref[rows, :] = d_in.astype(o_ref.dtype)
                gsum = gsums[which] + jnp.where(rid8 == K, jnp.sum(d_c, axis=0, keepdims=True), 0.0)
                for k in range(K):
                    gsum = gsum + jnp.where(rid8 == k, jnp.sum(d_c * sh[k], axis=0, keepdims=True), 0.0)
                gsums[which] = gsum
        ha[...], hb[...] = heads
        ga_ref[...] += gsums[0]
        gb_ref[...] += gsums[1]

    rb = lambda i: nb - 1 - i
    prev8 = lambda i: jnp.maximum(rb(i) * t8 - 1, 0)
    blk = pl.BlockSpec((tm, tn), lambda j, i: (rb(i), j))
    halo = pl.BlockSpec((SUB, tn), lambda j, i: (prev8(i), j))
    acc = pl.BlockSpec((SUB, tn), lambda j, i: (0, j))
    return pl.pallas_call(
        body, name=name, grid=(nj, nb),
        in_specs=[pl.BlockSpec((tm, D), lambda j, i: (rb(i), 0)), pl.BlockSpec((tn, D), lambda j, i: (j, 0)),
                  blk, halo, blk, halo,
                  pl.BlockSpec((K, tn), lambda j, i: (0, j)), pl.BlockSpec((K, tn), lambda j, i: (0, j + nj)),
                  pl.BlockSpec((1, tn), lambda j, i: (0, j)), pl.BlockSpec((1, tn), lambda j, i: (0, j + nj))],
        out_specs=[blk, blk, acc, acc],
        out_shape=[jax.ShapeDtypeStruct((S, F), BF16), jax.ShapeDtypeStruct((S, F), BF16),
                   jax.ShapeDtypeStruct((SUB, F), F32), jax.ShapeDtypeStruct((SUB, F), F32)],
        scratch_shapes=[pltpu.VMEM((SUB, tn), F32), pltpu.VMEM((SUB, tn), F32)],
        compiler_params=_cparams(("parallel", "arbitrary")),
    )(dx, w_down, hh_a, hh_a, hh_b, hh_b, cw, cw, cb, cb)


ADAM_BLOCK_ELEMS = 128 * 1024


def _adamw(w, m, v, parts, *, name):
    R, C = w.shape
    n = parts.shape[0]
    tr = R
    for cand in (1024, 512, 256, 128, 64, 32, 16):
        if R % cand == 0 and cand * C <= ADAM_BLOCK_ELEMS:
            tr = cand
            break
    c1 = 1.0 - ADAM_B1 ** ADAM_STEP
    c2 = 1.0 - ADAM_B2 ** ADAM_STEP

    def body(w_ref, m_ref, v_ref, p_ref, g_ref, d_ref, nm_ref, nv_ref):
        g = p_ref[0].astype(F32)
        for k in range(1, n):
            g = g + p_ref[k].astype(F32)
        m_new = ADAM_B1 * m_ref[...] + (1.0 - ADAM_B1) * g
        v_new = ADAM_B2 * v_ref[...] + (1.0 - ADAM_B2) * (g * g)
        m_hat = m_new / c1
        v_hat = v_new / c2
        g_ref[...] = g
        d_ref[...] = -ADAM_LR * (m_hat / (jnp.sqrt(v_hat) + ADAM_EPS) + ADAM_WD * w_ref[...])
        nm_ref[...] = m_new
        nv_ref[...] = v_new

    blk = pl.BlockSpec((tr, C), lambda i: (i, 0))
    sds = jax.ShapeDtypeStruct((R, C), F32)
    return pl.pallas_call(
        body, name=name, grid=(R // tr,),
        in_specs=[blk, blk, blk, pl.BlockSpec((n, tr, C), lambda i: (0, i, 0))],
        out_specs=[blk, blk, blk, blk], out_shape=[sds, sds, sds, sds],
        compiler_params=_cparams(("parallel",)),
    )(w, m, v, parts)


def _mesh_place():
    x, y, c = lax.axis_index("x"), lax.axis_index("y"), lax.axis_index("c")
    others = [(1 - x, y), (x, 1 - y), (1 - x, 1 - y)]
    return x, y, c, others


HBM_SPEC = pl.BlockSpec(memory_space=pltpu.HBM)
SEM_SPEC = pl.BlockSpec(memory_space=pltpu.SEMAPHORE)
ANY_SPEC = pl.BlockSpec(memory_space=pl.ANY)
EFFECT = pltpu.SideEffectType.DATAFLOW_SIDE_EFFECTING


def _in_hbm(a):
    return pltpu.with_memory_space_constraint(a, pltpu.HBM)


def _split_start(srcs, lands, copies, n_cp, *, name):
    n_s, n_l = len(srcs), len(lands)

    def body(*refs):
        src_refs, land_refs = refs[:n_s], refs[n_s:n_s + n_l]
        ssem, rsem = refs[n_s + n_l], refs[n_s + n_l + 1]
        token = refs[-1]
        for outgoing, _ in copies(src_refs, land_refs, ssem, rsem):
            outgoing.start()
        token[...] = jnp.zeros_like(token)

    outs = pl.pallas_call(
        body, name=name,
        out_shape=(pltpu.SemaphoreType.DMA((n_cp,)), pltpu.SemaphoreType.DMA((n_cp,)),
                   *[pltpu.HBM(a.shape, a.dtype) for a in srcs], *[pltpu.HBM(a.shape, a.dtype) for a in lands],
                   jax.ShapeDtypeStruct((SUB, LANE), F32)),
        in_specs=[HBM_SPEC] * (n_s + n_l),
        out_specs=(SEM_SPEC, SEM_SPEC, *[HBM_SPEC] * (n_s + n_l), pl.BlockSpec(memory_space=pltpu.VMEM)),
        input_output_aliases={i: 2 + i for i in range(n_s + n_l)},
        compiler_params=pltpu.CompilerParams(has_side_effects=EFFECT),
    )(*[_in_hbm(a) for a in srcs], *[_in_hbm(a) for a in lands])
    ssem, rsem = outs[0], outs[1]
    return ssem, rsem, list(outs[2:2 + n_s]), list(outs[2 + n_s:2 + n_s + n_l]), outs[-1]


def _split_wait(srcs, lands, ssem, rsem, after, copies, *, name):
    n_s, n_l = len(srcs), len(lands)

    def body(*refs):
        src_refs, land_refs = refs[:n_s], refs[n_s:n_s + n_l]
        s_ref, r_ref = refs[n_s + n_l], refs[n_s + n_l + 1]
        for outgoing, incoming in copies(src_refs, land_refs, s_ref, r_ref):
            outgoing.wait_send()
            incoming.wait_recv()

    outs = pl.pallas_call(
        body, name=name,
        out_shape=(*[pltpu.HBM(a.shape, a.dtype) for a in srcs], *[pltpu.HBM(a.shape, a.dtype) for a in lands]),
        in_specs=[HBM_SPEC] * (n_s + n_l) + [SEM_SPEC, SEM_SPEC, ANY_SPEC], out_specs=[HBM_SPEC] * (n_s + n_l),
        input_output_aliases={i: i for i in range(n_s + n_l)},
        compiler_params=pltpu.CompilerParams(has_side_effects=EFFECT),
    )(*srcs, *lands, ssem, rsem, after)
    return list(outs[:n_s]), list(outs[n_s:])


PLACE_BLOCK_ELEMS = 512 * 1024


def _place_rows(r, w):
    return _div_tile(r, max(16, PLACE_BLOCK_ELEMS // w), 16)


def _cast_place(shard, chip, axis, after, *, name):
    r, w = shard.shape
    tr = _place_rows(r, w)
    nb = r // tr
    full = (r * N_CHIP, w) if axis == 0 else (r, w * N_CHIP)
    has_after = after is not None

    def body(chip_ref, s_ref, *rest):
        rest[-1][...] = s_ref[...].astype(BF16)

    out_map = (lambda i, ch: (ch[0] * nb + i, 0)) if axis == 0 else (lambda i, ch: (i, ch[0]))
    grid_spec = pltpu.PrefetchScalarGridSpec(
        num_scalar_prefetch=1, grid=(nb,),
        in_specs=[pl.BlockSpec((tr, w), lambda i, ch: (i, 0))] + has_after * [ANY_SPEC],
        out_specs=pl.BlockSpec((tr, w), out_map))
    return pl.pallas_call(body, name=name, grid_spec=grid_spec, out_shape=jax.ShapeDtypeStruct(full, BF16),
                          compiler_params=_cparams(("parallel",)))(chip, shard, *(has_after * [after]))


def _slot_place(g, ids, axis, *, name):
    r, w = (g.shape[0] // N_CHIP, g.shape[1]) if axis == 0 else (g.shape[0], g.shape[1] // N_CHIP)
    tr = _place_rows(r, w)
    nb = r // tr

    def body(ids_ref, g_ref, o_ref):
        o_ref[...] = g_ref[...]

    in_map = (lambda i, ids_: (ids_[0] * nb + i, 0)) if axis == 0 else (lambda i, ids_: (i, ids_[0]))
    grid_spec = pltpu.PrefetchScalarGridSpec(
        num_scalar_prefetch=1, grid=(nb,), in_specs=[pl.BlockSpec((tr, w), in_map)],
        out_specs=pl.BlockSpec((None, tr, w), lambda i, ids_: (ids_[1], i, 0)))
    return pl.pallas_call(body, name=name, grid_spec=grid_spec, out_shape=jax.ShapeDtypeStruct((N_DEV, r, w), g.dtype),
                          compiler_params=_cparams(("parallel",)))(ids, g)


class _WeightGather:
    def __init__(self, placed, shard_shapes, axes, splits, tag):
        self.placed, self.shard_shapes, self.axes, self.splits, self.tag = list(placed), shard_shapes, axes, splits, tag
        self.n = len(placed)

    def _region(self, land_refs, it, chip, half):
        r, w = self.shard_shapes[it]
        by_rows = self.axes[it] == 0
        if self.splits[it] and half is not None:
            rows = pl.ds(pl.multiple_of(half * (r // 2) + (chip * r if by_rows else 0), 16), r // 2)
        else:
            rows = pl.ds(chip * r if by_rows else 0, r)
        cols = pl.ds(0, w) if by_rows else pl.ds(pl.multiple_of(chip * w, LANE), w)
        return land_refs[it].at[rows, cols]

    def _ici(self, src_refs, land_refs, ssem, rsem):
        x, y, c, others = _mesh_place()
        pairs = []
        for it in range(self.n):
            for j, chip in enumerate(others):
                def mk(chip_from, it=it, j=j, chip=chip):
                    return pltpu.make_async_remote_copy(
                        src_ref=self._region(land_refs, it, 2 * x + y, c), dst_ref=self._region(land_refs, it, chip_from, c),
                        send_sem=ssem.at[3 * it + j], recv_sem=rsem.at[3 * it + j], device_id=(*chip, c),
                        device_id_type=MESH)
                pairs.append((mk(2 * x + y), mk(2 * chip[0] + chip[1])))
        return pairs

    def start(self):
        self.ssem, self.rsem, _, self.lands, token = _split_start(
            [], self.placed, self._ici, 3 * self.n, name="gather_start_" + self.tag)
        return token

    def finish(self, after):
        _, lands = _split_wait([], self.lands, self.ssem, self.rsem, after, self._ici,
                               name="gather_wait_" + self.tag)
        n = self.n
        n_fwd = 3 * sum(self.splits)
        if n_fwd == 0:
            return lands

        def body(*refs):
            out_refs = refs[n:2 * n]
            fsend, frecv = refs[2 * n:]
            x, y, c, others = _mesh_place()
            sibling = (x, y, 1 - c)

            def fwd(it, slot, chip, half):
                reg = self._region(out_refs, it, 2 * chip[0] + chip[1], half)
                return pltpu.make_async_remote_copy(src_ref=reg, dst_ref=reg, send_sem=fsend.at[slot],
                                                    recv_sem=frecv.at[slot], device_id=sibling, device_id_type=MESH)

            sends, recvs = [], []
            for it in range(n):
                if self.splits[it]:
                    for chip in others:
                        sends.append(fwd(it, len(sends), chip, c))
                        recvs.append(fwd(it, len(recvs), chip, 1 - c))
            for cp in sends:
                cp.start()
            for cp in recvs:
                cp.wait_recv()
            for cp in sends:
                cp.wait_send()

        fulls = pl.pallas_call(
            body, name="gather_d2d_" + self.tag, in_specs=[ANY_SPEC] * n, out_specs=[ANY_SPEC] * n,
            out_shape=[jax.ShapeDtypeStruct(a.shape, a.dtype) for a in lands],
            scratch_shapes=[pltpu.SemaphoreType.DMA((n_fwd,)), pltpu.SemaphoreType.DMA((n_fwd,))],
            input_output_aliases={i: i for i in range(n)},
        )(*lands)
        return list(fulls)


class _GradGather:
    def __init__(self, grads, axes, tag):
        self.grads, self.axes, self.tag = list(grads), axes, tag
        self.n = len(grads)
        self.shard_shapes = [(g.shape[0] // N_CHIP, g.shape[1]) if ax == 0 else (g.shape[0], g.shape[1] // N_CHIP)
                             for g, ax in zip(grads, axes)]

    def _piece(self, src_refs, it, chip):
        r, w = self.shard_shapes[it]
        if self.axes[it] == 0:
            return src_refs[it].at[pl.ds(pl.multiple_of(chip * r, 16), r), :]
        return src_refs[it].at[:, pl.ds(pl.multiple_of(chip * w, LANE), w)]

    PER_ITEM = 4

    def _remote(self, src_refs, land_refs, ssem, rsem):
        x, y, c, others = _mesh_place()
        me = 4 * x + 2 * y + c
        pairs = []
        for it in range(self.n):
            def mk(k, piece_chip, slot, to, it=it):
                return pltpu.make_async_remote_copy(
                    src_ref=self._piece(src_refs, it, piece_chip), dst_ref=land_refs[it].at[slot],
                    send_sem=ssem.at[self.PER_ITEM * it + k], recv_sem=rsem.at[self.PER_ITEM * it + k], device_id=to,
                    device_id_type=MESH)
            for j, chip in enumerate(others):
                chip_id = 2 * chip[0] + chip[1]
                pairs.append((mk(j, chip_id, me, (*chip, c)), mk(j, chip_id, 2 * chip_id + c, (*chip, c))))
            sibling = (x, y, 1 - c)
            pairs.append((mk(3, 2 * x + y, me, sibling), mk(3, 2 * x + y, 4 * x + 2 * y + 1 - c, sibling)))
        return pairs

    def start(self):
        x, y, c = lax.axis_index("x"), lax.axis_index("y"), lax.axis_index("c")
        ids = jnp.stack([2 * x + y, 4 * x + 2 * y + c]).astype(jnp.int32)
        lands = [_slot_place(g, ids, ax, name="grads_own_%s%d" % (self.tag, it))
                 for it, (g, ax) in enumerate(zip(self.grads, self.axes))]
        self.ssem, self.rsem, self.srcs, self.lands, token = _split_start(
            self.grads, lands, self._remote, self.PER_ITEM * self.n, name="grads_start_" + self.tag)
        return token

    def finish(self, after):
        _, lands = _split_wait(self.srcs, self.lands, self.ssem, self.rsem, after, self._remote,
                               name="grads_wait_" + self.tag)
        n = self.n

        def body(*refs):
            out_refs = refs[n:2 * n]
            fsend, frecv = refs[2 * n:]
            x, y, c, others = _mesh_place()
            sibling = (x, y, 1 - c)

            def fwd(it, j, slot):
                return pltpu.make_async_remote_copy(
                    src_ref=out_refs[it].at[slot], dst_ref=out_refs[it].at[slot], send_sem=fsend.at[3 * it + j],
                    recv_sem=frecv.at[3 * it + j], device_id=sibling, device_id_type=MESH)

            sends = [fwd(it, j, 4 * ch[0] + 2 * ch[1] + c) for it in range(n) for j, ch in enumerate(others)]
            recvs = [fwd(it, j, 4 * ch[0] + 2 * ch[1] + 1 - c) for it in range(n) for j, ch in enumerate(others)]
            for cp in sends:
                cp.start()
            for cp in recvs:
                cp.wait_recv()
            for cp in sends:
                cp.wait_send()

        outs = pl.pallas_call(
            body, name="grads_d2d_" + self.tag, in_specs=[ANY_SPEC] * n, out_specs=[ANY_SPEC] * n,
            out_shape=[jax.ShapeDtypeStruct(a.shape, a.dtype) for a in lands],
            scratch_shapes=[pltpu.SemaphoreType.DMA((3 * n,)), pltpu.SemaphoreType.DMA((3 * n,))],
            input_output_aliases={i: i for i in range(n)},
        )(*lands)
        return list(outs)


def _allreduce_small(vec, *, name):
    R, L = vec.shape

    def body(v_ref, o_ref, buf, send, recv, lsem):
        x, y, c, others = _mesh_place()
        me = 4 * x + 2 * y + c
        sibling = (x, y, 1 - c)

        def copy(k, slot, to, src=None):
            return pltpu.make_async_remote_copy(
                src_ref=buf.at[slot] if src is None else src, dst_ref=buf.at[slot], send_sem=send.at[k],
                recv_sem=recv.at[k], device_id=to, device_id_type=MESH)

        def slot_of(chip, core):
            return 4 * chip[0] + 2 * chip[1] + core

        mine = pltpu.make_async_copy(v_ref, buf.at[me], lsem)
        mine.start()
        first = [copy(0, me, sibling, src=v_ref)]
        first += [copy(1 + j, me, (*chip, c), src=v_ref) for j, chip in enumerate(others)]
        for cp in first:
            cp.start()
        passed = [copy(4 + j, slot_of(chip, c), sibling) for j, chip in enumerate(others)]
        for j, chip in enumerate(others):
            copy(1 + j, slot_of(chip, c), (*chip, c)).wait_recv()
            passed[j].start()
        copy(0, slot_of((x, y), 1 - c), sibling).wait_recv()
        for j, chip in enumerate(others):
            copy(4 + j, slot_of(chip, 1 - c), sibling).wait_recv()
        for cp in first + passed:
            cp.wait_send()
        mine.wait()
        total = buf[0]
        for k in range(1, N_DEV):
            total = total + buf[k]
        o_ref[...] = total

    return pl.pallas_call(
        body, name=name, in_specs=[pl.BlockSpec(memory_space=pltpu.VMEM)],
        out_specs=pl.BlockSpec(memory_space=pltpu.VMEM), out_shape=jax.ShapeDtypeStruct((R, L), F32),
        scratch_shapes=[pltpu.VMEM((N_DEV, R, L), F32), pltpu.SemaphoreType.DMA((7,)), pltpu.SemaphoreType.DMA((7,)),
                        pltpu.SemaphoreType.DMA],
        compiler_params=pltpu.CompilerParams(vmem_limit_bytes=VMEM_LIMIT),
    )(vec)


PACK_ALIGN = 1024


def _pack(arrs, row_multiple):
    flat = []
    for a in arrs:
        f = a.reshape(-1).astype(F32)
        flat.append(jnp.pad(f, (0, (-f.shape[0]) % PACK_ALIGN)))
    v = jnp.concatenate(flat)
    v = jnp.pad(v, (0, (-v.shape[0]) % (LANE * row_multiple)))
    return v.reshape(-1, LANE)


def _unpack(v, shapes):
    flat = v.reshape(-1)
    out, off = [], 0
    for s in shapes:
        size = math.prod(s)
        out.append(flat[off:off + size].reshape(s))
        off += size + (-size) % PACK_ALIGN
    return out


def _tile(dim, target):
    for cand in (1024, 512, 256, 128):
        if cand <= target and dim % cand == 0:
            return cand
    return dim


def _div_tile(dim, cap, mult=LANE):
    best = None
    for cand in range(mult, min(cap, dim) + 1, mult):
        if dim % cand == 0:
            best = cand
    return dim if best is None else best


WEIGHT_NAMES = ('norm1_g', 'w_in', 'ret_g', 'rg_conv_w', 'rg_conv_b', 'rg_wa', 'rg_ba', 'rg_wx', 'rg_bx', 'rg_lambda',
                'w_out', 'norm2_g', 'norm_mem_g', 'xa_wq', 'xa_wk', 'xa_wv', 'xa_wo', 'norm3_g', 'ffn_w_up',
                'ffn_conv_w', 'ffn_conv_b', 'ffn_w_down', 'final_g')
BIG_AXIS = {'w_in': 1, 'w_out': 0, 'xa_wq': 0, 'xa_wk': 0, 'xa_wv': 0, 'xa_wo': 0, 'ffn_w_up': 1, 'ffn_w_down': 0}
SMALL_SHARDED = ('rg_conv_w', 'ffn_conv_w')


def _step(x, mem, positions, loss_target, W, Mo, Vo):
    S, D = x.shape[1], x.shape[2]
    xs, mems, tgt = x[0], mem[0], loss_target[0]
    n_mem = mems.shape[0]
    pos_col = positions.reshape(S, 1)
    chip = 2 * lax.axis_index("x") + lax.axis_index("y")

    big = list(BIG_AXIS)
    shards = {n: W[n][0] for n in big}
    G = {}
    gather_groups = (('w_in', 'rg_conv_w'), ('w_out', 'xa_wq', 'xa_wk', 'xa_wv', 'xa_wo'),
                     ('ffn_w_up', 'ffn_conv_w'), ('ffn_w_down',))
    gathers, tok = [], None
    chip1 = jnp.reshape(chip, (1,)).astype(jnp.int32)
    for gi, names in enumerate(gather_groups):
        placed = []
        for n in names:
            if n in BIG_AXIS:
                placed.append(_cast_place(shards[n], chip1, BIG_AXIS[n], tok, name="place_" + n))
            else:
                s = W[n][0] if tok is None else W[n][0] + tok[0, 0]
                full = lax.empty((s.shape[0], s.shape[1] * N_CHIP), s.dtype)
                placed.append(lax.dynamic_update_slice(full, s, (0, chip * s.shape[1])))
        ag = _WeightGather(placed, [W[n][0].shape for n in names], [BIG_AXIS.get(n, 1) for n in names],
                           [n in BIG_AXIS for n in names], "g%d" % gi)
        tok = ag.start()
        gathers.append(ag)

    def finish_gather(gi, after):
        G.update(zip(gather_groups[gi], gathers[gi].finish(after)))

    finish_gather(0, tok)
    R = W['ret_g'].shape[1]
    Wl = W['rg_lambda'].shape[1]
    IN = W['w_in'].shape[2] * N_CHIP
    F2 = W['ffn_w_up'].shape[2] * N_CHIP
    F = F2 // 2

    norm1_g, norm2_g, norm3_g = W['norm1_g'] + tok[0, 0], W['norm2_g'], W['norm3_g']
    norm_mem_g, final_g, ret_g = W['norm_mem_g'], W['final_g'].reshape(1, D), W['ret_g']
    rg_cw, rg_cb = G['rg_conv_w'], W['rg_conv_b']
    wa, wx = W['rg_wa'][0], W['rg_wx'][0]
    ba, bx = W['rg_ba'].reshape(1, Wl), W['rg_bx'].reshape(1, Wl)
    lam = W['rg_lambda']
    ffn_cb = W['ffn_conv_b']

    def fwd_mm(a, wname, N, K, **kw):
        return _mm(a, G[wname], mode="nn", M=a.shape[0], N=N, K=K, tm=_tile(a.shape[0], 1024), tn=1024,
                   tk=_div_tile(K, 3072), **kw)

    def fwd_mm_norm(a, wname, res, g, name):
        return _mm(a, G[wname], mode="nn", M=a.shape[0], N=D, K=a.shape[1], tm=512, tn=D, tk=_div_tile(a.shape[1], 2048),
                   out_dtype=F32, res=res, norm_g=g, name=name)

    def bwd_x_mm(d, wname, N, K, **kw):
        return _mm(d, G[wname], mode="nt", M=d.shape[0], N=N, K=K, tm=_tile(d.shape[0], 1024),
                   tn=_div_tile(N, 1024, 256), tk=_div_tile(K, 3072), **kw)

    def bwd_w_mm(a, d, M, N, **kw):
        Ks = a.shape[0]
        return _mm(a, d, mode="tn", M=M, N=N, K=Ks, out_dtype=BF16, tm=_div_tile(M, 1024, 256),
                   tn=_div_tile(N, 1024, 256), tk=_div_tile(Ks, 2048 if d.dtype == BF16 else 1024), **kw)

    xn1 = _rmsnorm_fwd(xs, norm1_g, name="norm1_fwd")
    h = fwd_mm(xn1, 'w_in', IN, D, out_dtype=F32, name="mm_in")
    half = (R // RET_HEADS) // 2
    inv = (ROPE_BASE ** (-jnp.arange(half, dtype=F32) / half)).reshape(1, half)
    cos, sin = _rope_table(pos_col, inv, name="rope_table")
    hl, mix = _lru_fwd(h, rg_cw, rg_cb, wa, ba, wx, bx, lam, name="lru_fwd")
    ret_raw, states, mix = _ret_fwd(h, cos, sin, ret_g, mix, name="ret_fwd")
    finish_gather(1, mix)
    x1, xn2 = fwd_mm_norm(mix, 'w_out', xs, norm2_g, "mm_out")
    memn = _rmsnorm_fwd(mems, norm_mem_g, name="norm_mem_fwd")
    km = fwd_mm(memn, 'xa_wk', D, D, out_dtype=BF16, name="mm_k")
    vm = fwd_mm(memn, 'xa_wv', D, D, out_dtype=BF16, name="mm_v")
    q = fwd_mm(xn2, 'xa_wq', D, D, out_dtype=BF16, name="mm_q")
    o = _xattn_fwd(q, km, vm, name="xattn_fwd")
    x2, xn3 = fwd_mm_norm(o, 'xa_wo', x1, norm3_g, "mm_o")
    finish_gather(2, xn3)
    ffn_cw = G['ffn_conv_w']
    act, hh_a, hh_b = _ffn_up_gate(xn3, G['ffn_w_up'], ffn_cw, ffn_cb, name="ffn_up_gate")
    finish_gather(3, act)
    x3 = fwd_mm(act, 'ffn_w_down', D, F, out_dtype=F32, res=x2, name="mm_down")
    dx3, d_final, loss8, dx3h = _final_loss(x3, tgt, final_g, name="final_loss")

    gw = {}
    grad_groups = []

    def start_grads(names, tag):
        gg = _GradGather([gw[n] for n in names], [BIG_AXIS[n] for n in names], tag)
        grad_groups.append((names, gg))
        return gg.start()

    gw['ffn_w_down'] = bwd_w_mm(act, dx3h, F, D, name="mm_dw_down")
    dhh_a, dhh_b, gcw_a, gcw_b = _ffn_gate_bwd(dx3h, G['ffn_w_down'], hh_a, hh_b, ffn_cw, ffn_cb, name="ffn_gate_bwd")
    gw_up = bwd_w_mm(xn3, dhh_a, D, F, out_full=(D, F2), name="mm_dw_up_a")
    gw['ffn_w_up'] = bwd_w_mm(xn3, dhh_b, D, F, out=gw_up, out_off=(0, F), name="mm_dw_up_b")
    tok_a = start_grads(('ffn_w_down', 'ffn_w_up'), "a")
    dxn3 = bwd_x_mm(dhh_a, 'ffn_w_up', D, F, out_dtype=F32, after=tok_a, name="mm_dxn3_a")
    dxn3 = bwd_x_mm(dhh_b, 'ffn_w_up', D, F, out_dtype=F32, b_off=(0, F), res=dxn3, name="mm_dxn3_b")
    dx2, d_norm3, dx2h = _rmsnorm_bwd(x2, dxn3, norm3_g, dx3, name="norm3_bwd", emit_bf16=True)
    Kc = ffn_cw.shape[0]
    d_ffn_cw = jnp.concatenate([gcw_a[:Kc], gcw_b[:Kc]], axis=1)
    d_ffn_cb = jnp.concatenate([gcw_a[Kc:Kc + 1], gcw_b[Kc:Kc + 1]], axis=1)

    d_o = bwd_x_mm(dx2h, 'xa_wo', D, D, out_dtype=BF16, name="mm_do")
    gw['xa_wo'] = bwd_w_mm(o, dx2h, D, D, name="mm_dw_o")
    dq, dk, dv = _xattn_bwd(q, km, vm, d_o, name="xattn_bwd")
    gw['xa_wq'] = bwd_w_mm(xn2, dq, D, D, name="mm_dw_q")
    dxn2 = bwd_x_mm(dq, 'xa_wq', D, D, out_dtype=F32, name="mm_dxn2")
    gw['xa_wk'] = bwd_w_mm(memn, dk, D, D, name="mm_dw_k")
    gw['xa_wv'] = bwd_w_mm(memn, dv, D, D, name="mm_dw_v")
    dmemn = bwd_x_mm(dk, 'xa_wk', D, D, out_dtype=F32, name="mm_dmem_k")
    dmemn = bwd_x_mm(dv, 'xa_wv', D, D, out_dtype=F32, res=dmemn, name="mm_dmem_v")
    _, d_norm_mem = _rmsnorm_bwd(mems, dmemn, norm_mem_g, None, name="norm_mem_bwd")
    dx1, d_norm2, dx1h = _rmsnorm_bwd(x1, dxn2, norm2_g, dx2, name="norm2_bwd", emit_bf16=True)

    gw['w_out'] = bwd_w_mm(mix, dx1h, D, D, name="mm_dw_out")
    tok_b = start_grads(('xa_wo', 'xa_wq', 'xa_wk', 'xa_wv', 'w_out'), "b")
    dmix = bwd_x_mm(dx1h, 'w_out', D, D, out_dtype=F32, after=tok_b, name="mm_dmix")
    dh, d_ret_g = _ret_bwd(h, cos, sin, ret_g, states, ret_raw, dmix, name="ret_bwd")
    dh, d_rcw, d_rcb, d_wa, d_ba, d_wx, d_bx, d_lam = _lru_bwd(
        h, hl, dmix, dh, rg_cw, rg_cb, wa, ba, wx, bx, lam, name="lru_bwd")
    gw['w_in'] = bwd_w_mm(xn1, dh, D, IN, name="mm_dw_in")
    tok_c = start_grads(('w_in',), "c")
    dxn1 = bwd_x_mm(dh, 'w_in', D, IN, out_dtype=F32, after=tok_c, name="mm_dxn1")
    grad_x, d_norm1 = _rmsnorm_bwd(xs, dxn1, norm1_g, dx1, name="norm1_bwd")

    small_parts = {
        'norm1_g': d_norm1, 'ret_g': d_ret_g, 'rg_conv_w': d_rcw[:rg_cw.shape[0]], 'rg_conv_b': d_rcb,
        'rg_wa': d_wa, 'rg_ba': d_ba, 'rg_wx': d_wx, 'rg_bx': d_bx, 'rg_lambda': d_lam, 'norm2_g': d_norm2,
        'norm_mem_g': d_norm_mem, 'norm3_g': d_norm3, 'ffn_conv_w': d_ffn_cw, 'ffn_conv_b': d_ffn_cb,
        'final_g': d_final}
    small = [n for n in WEIGHT_NAMES if n not in BIG_AXIS]
    red_shapes = [(1,)] + [tuple(small_parts[n].shape) for n in small]
    reduced = _allreduce_small(_pack([loss8[0:1, 0:1]] + [small_parts[n] for n in small], SUB), name="allreduce_small")
    red = _unpack(reduced, red_shapes)
    loss = red[0][0]
    g_small = dict(zip(small, red[1:]))
    for n in SMALL_SHARDED:
        w_local = W[n].shape[-1]
        g_small[n] = lax.dynamic_slice_in_dim(g_small[n], chip * w_local, w_local, axis=1)

    out_g, out_d, out_m, out_v = {}, {}, {}, {}
    for names, gg in grad_groups:
        for n, land in zip(names, gg.finish(reduced)):
            g, d, m_new, v_new = _adamw(shards[n], Mo[n][0], Vo[n][0], land, name="adamw_" + n)
            out_g[n], out_d[n], out_m[n], out_v[n] = (t.reshape(W[n].shape) for t in (g, d, m_new, v_new))
    rows = 512
    pk = lambda d: _pack([d[n] for n in small], rows)
    g_pack = _pack([g_small[n] for n in small], rows)
    res_small = _adamw(pk(W), pk(Mo), pk(Vo), g_pack[None], name="adamw_small")
    shapes_small = [tuple(W[n].shape) for n in small]
    for dst, packed in zip((out_g, out_d, out_m, out_v), res_small):
        for n, val in zip(small, _unpack(packed, shapes_small)):
            dst[n] = val
    return (loss, grad_x[None], *[out_g[n] for n in WEIGHT_NAMES], *[out_d[n] for n in WEIGHT_NAMES],
            *[out_m[n] for n in WEIGHT_NAMES], *[out_v[n] for n in WEIGHT_NAMES])


def kernel(x, mem, positions, norm1_g, w_in, ret_g, rg_conv_w, rg_conv_b, rg_wa, rg_ba, rg_wx, rg_bx, rg_lambda, w_out, norm2_g, norm_mem_g, xa_wq, xa_wk, xa_wv, xa_wo, norm3_g, ffn_w_up, ffn_conv_w, ffn_conv_b, ffn_w_down, final_g, loss_target, m_norm1_g, m_w_in, m_ret_g, m_rg_conv_w, m_rg_conv_b, m_rg_wa, m_rg_ba, m_rg_wx, m_rg_bx, m_rg_lambda, m_w_out, m_norm2_g, m_norm_mem_g, m_xa_wq, m_xa_wk, m_xa_wv, m_xa_wo, m_norm3_g, m_ffn_w_up, m_ffn_conv_w, m_ffn_conv_b, m_ffn_w_down, m_final_g, v_norm1_g, v_w_in, v_ret_g, v_rg_conv_w, v_rg_conv_b, v_rg_wa, v_rg_ba, v_rg_wx, v_rg_bx, v_rg_lambda, v_w_out, v_norm2_g, v_norm_mem_g, v_xa_wq, v_xa_wk, v_xa_wv, v_xa_wo, v_norm3_g, v_ffn_w_up, v_ffn_conv_w, v_ffn_conv_b, v_ffn_w_down, v_final_g):
    W = dict(zip(WEIGHT_NAMES, (norm1_g, w_in, ret_g, rg_conv_w, rg_conv_b, rg_wa, rg_ba, rg_wx, rg_bx, rg_lambda, w_out,
                                norm2_g, norm_mem_g, xa_wq, xa_wk, xa_wv, xa_wo, norm3_g, ffn_w_up, ffn_conv_w,
                                ffn_conv_b, ffn_w_down, final_g)))
    Mo = dict(zip(WEIGHT_NAMES, (m_norm1_g, m_w_in, m_ret_g, m_rg_conv_w, m_rg_conv_b, m_rg_wa, m_rg_ba, m_rg_wx, m_rg_bx,
                                 m_rg_lambda, m_w_out, m_norm2_g, m_norm_mem_g, m_xa_wq, m_xa_wk, m_xa_wv, m_xa_wo,
                                 m_norm3_g, m_ffn_w_up, m_ffn_conv_w, m_ffn_conv_b, m_ffn_w_down, m_final_g)))
    Vo = dict(zip(WEIGHT_NAMES, (v_norm1_g, v_w_in, v_ret_g, v_rg_conv_w, v_rg_conv_b, v_rg_wa, v_rg_ba, v_rg_wx, v_rg_bx,
                                 v_rg_lambda, v_w_out, v_norm2_g, v_norm_mem_g, v_xa_wq, v_xa_wk, v_xa_wv, v_xa_wo,
                                 v_norm3_g, v_ffn_w_up, v_ffn_conv_w, v_ffn_conv_b, v_ffn_w_down, v_final_g)))
    return _step(x, mem, positions, loss_target, W, Mo, Vo)
```

```python
import functools
import math

import jax
import jax.numpy as jnp
from jax import lax
from jax.experimental import pallas as pl
from jax.experimental.pallas import tpu as pltpu

F32 = jnp.float32
BF16 = jnp.bfloat16

EPS = 1e-6
RET_HEADS = 4
RET_CHUNK = 128
ROPE_BASE = 10000.0
LRU_BLOCKS = 8
LRU_C = 8.0
XA_HEADS = 4

ADAM_LR = 0.001
ADAM_B1 = 0.9
ADAM_B2 = 0.999
ADAM_EPS = 1e-08
ADAM_WD = 0.01
ADAM_STEP = 10

N_DEV = 8
N_CHIP = 4
MESH = pl.DeviceIdType.MESH
SUB = 8
LANE = 128
VMEM_LIMIT = 56 * 1024 * 1024

NN = ((1,), (0,))
NT = ((1,), (1,))
TN = ((0,), (0,))


def _cparams(sem):
    return pltpu.CompilerParams(dimension_semantics=sem, vmem_limit_bytes=VMEM_LIMIT)


def _sigmoid(v):
    return 1.0 / (1.0 + jnp.exp(-v))


def _bdot(a, b, dims):
    return lax.dot_general(a.astype(BF16), b.astype(BF16), (dims, ((), ())), preferred_element_type=F32)


def _row_iota(shape):
    return lax.broadcasted_iota(jnp.int32, shape, 0)


def _shift_down(v, tail, k):
    if k == 0:
        return v
    r = pltpu.roll(v, k, 0)
    rt = pltpu.roll(tail, k, 0)
    first = jnp.where(_row_iota(rt.shape) < k, rt, r[0:SUB])
    return jnp.concatenate([first, r[SUB:]], axis=0)


def _shift_up(v, head, k):
    if k == 0:
        return v
    n = v.shape[0]
    r = pltpu.roll(v, n - k, 0)
    rh = pltpu.roll(head, SUB - k, 0)
    last = jnp.where(_row_iota(rh.shape) >= SUB - k, rh, r[n - SUB:n])
    return jnp.concatenate([r[:n - SUB], last], axis=0)


def _mm(a, b, *, mode, M, N, K, out_dtype, name, tm=512, tn=512, tk=512, a_off=(0, 0), b_off=(0, 0),
        res=None, out=None, out_off=(0, 0), out_full=None, norm_g=None, after=None):
    tm, tn, tk = min(tm, M), min(tn, N), min(tk, K)
    assert M % tm == 0 and N % tn == 0 and K % tk == 0, (name, M, N, K, tm, tn, tk)
    nk = K // tk
    if mode == "nn":
        a_blk, b_blk, dims = (tm, tk), (tk, tn), NN
        a_map = lambda i, j, k: (i + a_off[0] // tm, k + a_off[1] // tk)
        b_map = lambda i, j, k: (k + b_off[0] // tk, j + b_off[1] // tn)
    elif mode == "nt":
        a_blk, b_blk, dims = (tm, tk), (tn, tk), NT
        a_map = lambda i, j, k: (i + a_off[0] // tm, k + a_off[1] // tk)
        b_map = lambda i, j, k: (j + b_off[0] // tn, k + b_off[1] // tk)
    else:
        a_blk, b_blk, dims = (tk, tm), (tk, tn), TN
        a_map = lambda i, j, k: (k + a_off[0] // tk, i + a_off[1] // tm)
        b_map = lambda i, j, k: (k + b_off[0] // tk, j + b_off[1] // tn)
    for off, blk in ((a_off, a_blk), (b_off, b_blk), (out_off, (tm, tn))):
        assert off[0] % blk[0] == 0 and off[1] % blk[1] == 0, (name, off, blk)
    o_map = lambda i, j, k: (i + out_off[0] // tm, j + out_off[1] // tn)
    has_res, has_out, has_norm, has_after = res is not None, out is not None, norm_g is not None, after is not None
    assert not has_norm or (tn == N and not has_out)

    def body(*refs):
        refs = list(refs)
        a_ref, b_ref = refs[0], refs[1]
        pos = 2
        r_ref = g_ref = n_ref = None
        if has_res:
            r_ref = refs[pos]
            pos += 1
        if has_norm:
            g_ref = refs[pos]
            pos += 1
        pos += has_out + has_after
        o_ref = refs[pos]
        pos += 1
        if has_norm:
            n_ref = refs[pos]
            pos += 1
        acc = refs[pos] if nk > 1 else None
        k = pl.program_id(2)
        part = _bdot(a_ref[...], b_ref[...], dims)

        def finish(total):
            if has_res:
                total = total + r_ref[...].astype(F32)
            o_ref[...] = total.astype(o_ref.dtype)
            if has_norm:
                r = lax.rsqrt(jnp.mean(total * total, axis=-1, keepdims=True) + EPS)
                n_ref[...] = (total * r * g_ref[...]).astype(n_ref.dtype)

        if nk == 1:
            finish(part)
        else:
            @pl.when(k == 0)
            def _():
                acc[...] = part

            @pl.when(k > 0)
            def _():
                acc[...] += part

            @pl.when(k == nk - 1)
            def _():
                finish(acc[...])

    in_specs = [pl.BlockSpec(a_blk, a_map), pl.BlockSpec(b_blk, b_map)]
    args = [a, b]
    if has_res:
        in_specs.append(pl.BlockSpec((tm, tn), lambda i, j, k: (i, j)))
        args.append(res)
    if has_norm:
        in_specs.append(pl.BlockSpec((1, N), lambda i, j, k: (0, 0)))
        args.append(norm_g)
    aliases = {}
    if has_out:
        in_specs.append(pl.BlockSpec(memory_space=pl.ANY))
        aliases = {len(args): 0}
        args.append(out)
        out_shape = jax.ShapeDtypeStruct(out.shape, out.dtype)
    else:
        out_shape = jax.ShapeDtypeStruct((M, N) if out_full is None else out_full, out_dtype)
    if has_after:
        in_specs.append(pl.BlockSpec(memory_space=pl.ANY))
        args.append(after)
    out_specs = pl.BlockSpec((tm, tn), o_map)
    if has_norm:
        out_shape = [out_shape, jax.ShapeDtypeStruct((M, N), BF16)]
        out_specs = [out_specs, pl.BlockSpec((tm, tn), lambda i, j, k: (i, j))]
    return pl.pallas_call(
        body, name=name, grid=(M // tm, N // tn, nk), in_specs=in_specs,
        out_specs=out_specs, out_shape=out_shape,
        scratch_shapes=[pltpu.VMEM((tm, tn), F32)] if nk > 1 else [],
        input_output_aliases=aliases,
        compiler_params=_cparams(("parallel", "parallel", "arbitrary")),
    )(*args)


def _rmsnorm_fwd(x, g, *, name, ts=512):
    S, D = x.shape
    ts = min(ts, S)

    def body(x_ref, g_ref, o_ref):
        xv = x_ref[...]
        r = lax.rsqrt(jnp.mean(xv * xv, axis=-1, keepdims=True) + EPS)
        o_ref[...] = (xv * r * g_ref[...]).astype(o_ref.dtype)

    return pl.pallas_call(
        body, name=name, grid=(S // ts,),
        in_specs=[pl.BlockSpec((ts, D), lambda i: (i, 0)), pl.BlockSpec((1, D), lambda i: (0, 0))],
        out_specs=pl.BlockSpec((ts, D), lambda i: (i, 0)),
        out_shape=jax.ShapeDtypeStruct((S, D), BF16),
        compiler_params=_cparams(("parallel",)),
    )(x, g)


def _rmsnorm_bwd(x, dxn, g, res, *, name, ts=256, emit_bf16=False):
    S, D = x.shape
    ts = min(ts, S)
    has_res = res is not None

    def body(*refs):
        refs = list(refs)
        dx16_ref = refs.pop() if emit_bf16 else None
        if has_res:
            x_ref, d_ref, g_ref, r_ref, dx_ref, dg_ref = refs
        else:
            x_ref, d_ref, g_ref, dx_ref, dg_ref = refs
        i = pl.program_id(0)
        xv = x_ref[...]
        dv = d_ref[...].astype(F32)
        r = lax.rsqrt(jnp.mean(xv * xv, axis=-1, keepdims=True) + EPS)
        gd = dv * g_ref[...]
        proj = jnp.mean(xv * gd, axis=-1, keepdims=True)
        dx = r * gd - xv * (r * r * r) * proj
        if has_res:
            dx = dx + r_ref[...]
        dx_ref[...] = dx
        if emit_bf16:
            dx16_ref[...] = dx.astype(BF16)
        part = jnp.sum(dv * xv * r, axis=0, keepdims=True)

        @pl.when(i == 0)
        def _():
            dg_ref[...] = part

        @pl.when(i > 0)
        def _():
            dg_ref[...] += part

    row = pl.BlockSpec((ts, D), lambda i: (i, 0))
    vec = pl.BlockSpec((1, D), lambda i: (0, 0))
    in_specs = [row, row, vec] + ([row] if has_res else [])
    args = [x, dxn, g] + ([res] if has_res else [])
    extra = emit_bf16 * [jax.ShapeDtypeStruct((S, D), BF16)]
    return pl.pallas_call(
        body, name=name, grid=(S // ts,), in_specs=in_specs, out_specs=[row, vec] + emit_bf16 * [row],
        out_shape=[jax.ShapeDtypeStruct((S, D), F32), jax.ShapeDtypeStruct((1, D), F32)] + extra,
        compiler_params=_cparams(("arbitrary",)),
    )(*args)


def _final_loss(x, target, g, *, name, ts=256):
    S, D = x.shape
    ts = min(ts, S)

    def body(x_ref, t_ref, g_ref, dx_ref, dg_ref, loss_ref, dx16_ref):
        i = pl.program_id(0)
        xv = x_ref[...]
        gv = g_ref[...]
        r = lax.rsqrt(jnp.mean(xv * xv, axis=-1, keepdims=True) + EPS)
        y = xv * r * gv
        err = y - t_ref[...]
        row_loss = jnp.mean(err * err, axis=-1, keepdims=True)
        lpart = 0.5 * jnp.sum(row_loss, axis=0, keepdims=True)
        dy = err * (1.0 / D)
        gd = dy * gv
        proj = jnp.mean(xv * gd, axis=-1, keepdims=True)
        dx = r * gd - xv * (r * r * r) * proj
        dx_ref[...] = dx
        dx16_ref[...] = dx.astype(BF16)
        part = jnp.sum(dy * xv * r, axis=0, keepdims=True)
        lfull = jnp.broadcast_to(lpart, loss_ref.shape)

        @pl.when(i == 0)
        def _():
            dg_ref[...] = part
            loss_ref[...] = lfull

        @pl.when(i > 0)
        def _():
            dg_ref[...] += part
            loss_ref[...] += lfull

    row = pl.BlockSpec((ts, D), lambda i: (i, 0))
    vec = pl.BlockSpec((1, D), lambda i: (0, 0))
    return pl.pallas_call(
        body, name=name, grid=(S // ts,), in_specs=[row, row, vec],
        out_specs=[row, vec, pl.BlockSpec((SUB, LANE), lambda i: (0, 0)), row],
        out_shape=[jax.ShapeDtypeStruct((S, D), F32), jax.ShapeDtypeStruct((1, D), F32),
                   jax.ShapeDtypeStruct((SUB, LANE), F32), jax.ShapeDtypeStruct((S, D), BF16)],
        compiler_params=_cparams(("arbitrary",)),
    )(x, target, g)


def _rope_table(pos_col, inv, *, name, ts=1024):
    S = pos_col.shape[0]
    ts = min(ts, S)
    half = inv.shape[1]

    def body(p_ref, inv_ref, c_ref, s_ref):
        ang = p_ref[...].astype(F32) * inv_ref[...]
        c_ref[...] = jnp.cos(ang)
        s_ref[...] = jnp.sin(ang)

    tab = pl.BlockSpec((ts, half), lambda i: (i, 0))
    return pl.pallas_call(
        body, name=name, grid=(S // ts,),
        in_specs=[pl.BlockSpec((ts, 1), lambda i: (i, 0)), pl.BlockSpec((1, half), lambda i: (0, 0))],
        out_specs=[tab, tab],
        out_shape=[jax.ShapeDtypeStruct((S, half), F32), jax.ShapeDtypeStruct((S, half), F32)],
        compiler_params=_cparams(("parallel",)),
    )(pos_col, inv)


def _ret_consts(C, log_g):
    ii = lax.broadcasted_iota(jnp.int32, (C, C), 0)
    jj = lax.broadcasted_iota(jnp.int32, (C, C), 1)
    diff = (ii - jj).astype(F32)
    intra = jnp.where(ii >= jj, jnp.exp(log_g * jnp.maximum(diff, 0.0)), 0.0)
    idx = lax.broadcasted_iota(jnp.int32, (C, 1), 0).astype(F32)
    qd = jnp.exp(log_g * (idx + 1.0))
    kd = jnp.exp(log_g * (C - 1.0 - idx))
    cd = math.exp(log_g * C)
    return intra, qd, kd, cd


def _rot(t, cs, sn):
    half = t.shape[-1] // 2
    t1, t2 = t[:, :half], t[:, half:]
    return jnp.concatenate([t1 * cs - t2 * sn, t1 * sn + t2 * cs], axis=-1)


def _unrot(d, cs, sn):
    half = d.shape[-1] // 2
    d1, d2 = d[:, :half], d[:, half:]
    return jnp.concatenate([d1 * cs + d2 * sn, d2 * cs - d1 * sn], axis=-1)


def _ret_fwd(h, cos, sin, ret_g, mix, *, name, ch=2):
    S = h.shape[0]
    R = ret_g.shape[1]
    H, C = RET_HEADS, RET_CHUNK
    Dh = R // H
    ts = ch * C
    assert S % ts == 0
    log_gs = [math.log(1.0 - 2.0 ** (-5.0 - hd)) for hd in range(H)]
    scale = Dh ** -0.5

    def body(x_ref, c_ref, s_ref, rg_ref, mix_in, ret_ref, st_ref, mix_ref, state):
        i = pl.program_id(0)

        @pl.when(i == 0)
        def _():
            state[...] = jnp.zeros_like(state)

        for c in range(ch):
            rows = pl.ds(c * C, C)
            cs, sn = c_ref[rows, :], s_ref[rows, :]
            for hd in range(H):
                intra, qd, kd, cd = _ret_consts(C, log_gs[hd])
                q = x_ref[rows, pl.ds(hd * Dh, Dh)]
                k = x_ref[rows, pl.ds(R + hd * Dh, Dh)]
                v = x_ref[rows, pl.ds(2 * R + hd * Dh, Dh)]
                g = x_ref[rows, pl.ds(3 * R + hd * Dh, Dh)]
                rq = _rot(q, cs, sn)
                rk = _rot(k, cs, sn) * scale
                st = state[hd]
                st_ref[c, hd] = st.astype(BF16)
                s_ = _bdot(rq, rk, NT) * intra
                ret = _bdot(s_, v, NN) + _bdot(rq * qd, st, NN)
                state[hd] = st * cd + _bdot(rk * kd, v, TN)
                ret_ref[rows, pl.ds(hd * Dh, Dh)] = ret
                rr = lax.rsqrt(jnp.mean(ret * ret, axis=-1, keepdims=True) + EPS)
                out = ret * rr * rg_ref[:, pl.ds(hd * Dh, Dh)] * (g * _sigmoid(g))
                mix_ref[rows, pl.ds(hd * Dh, Dh)] = out.astype(BF16)

    n_chunks = S // C
    return pl.pallas_call(
        body, name=name, grid=(S // ts,),
        in_specs=[pl.BlockSpec((ts, 4 * R), lambda i: (i, 0)),
                  pl.BlockSpec((ts, Dh // 2), lambda i: (i, 0)), pl.BlockSpec((ts, Dh // 2), lambda i: (i, 0)),
                  pl.BlockSpec((1, R), lambda i: (0, 0)), pl.BlockSpec(memory_space=pl.ANY)],
        out_specs=[pl.BlockSpec((ts, R), lambda i: (i, 0)),
                   pl.BlockSpec((ch, H, Dh, Dh), lambda i: (i, 0, 0, 0)),
                   pl.BlockSpec((ts, R), lambda i: (i, 0))],
        out_shape=[jax.ShapeDtypeStruct((S, R), F32), jax.ShapeDtypeStruct((n_chunks, H, Dh, Dh), BF16),
                   jax.ShapeDtypeStruct(mix.shape, mix.dtype)],
        scratch_shapes=[pltpu.VMEM((H, Dh, Dh), F32)],
        input_output_aliases={4: 2},
        compiler_params=_cparams(("arbitrary",)),
    )(h, cos, sin, ret_g, mix)


def _ret_bwd(h, cos, sin, ret_g, states, ret_raw, dmix, *, name, ch=2):
    S = h.shape[0]
    R = ret_g.shape[1]
    H, C = RET_HEADS, RET_CHUNK
    Dh = R // H
    ts = ch * C
    nb = S // ts
    log_gs = [math.log(1.0 - 2.0 ** (-5.0 - hd)) for hd in range(H)]
    scale = Dh ** -0.5

    def body(x_ref, c_ref, s_ref, rg_ref, st_ref, ret_ref, dm_ref, dh_ref, drg_ref, dstate):
        i = pl.program_id(0)

        @pl.when(i == 0)
        def _():
            dstate[...] = jnp.zeros_like(dstate)
            drg_ref[...] = jnp.zeros_like(drg_ref)

        for c in reversed(range(ch)):
            rows = pl.ds(c * C, C)
            cs, sn = c_ref[rows, :], s_ref[rows, :]
            for hd in range(H):
                intra, qd, kd, cd = _ret_consts(C, log_gs[hd])
                cols = pl.ds(hd * Dh, Dh)
                q = x_ref[rows, pl.ds(hd * Dh, Dh)]
                k = x_ref[rows, pl.ds(R + hd * Dh, Dh)]
                v = x_ref[rows, pl.ds(2 * R + hd * Dh, Dh)]
                g = x_ref[rows, pl.ds(3 * R + hd * Dh, Dh)]
                rq = _rot(q, cs, sn)
                rk = _rot(k, cs, sn) * scale
                ret = ret_ref[rows, cols]
                dm = dm_ref[rows, cols]
                rgv = rg_ref[:, cols]
                rr = lax.rsqrt(jnp.mean(ret * ret, axis=-1, keepdims=True) + EPS)
                retn = ret * rr
                sg = _sigmoid(g)
                silu = g * sg
                drg_ref[:, cols] += jnp.sum(dm * retn * silu, axis=0, keepdims=True)
                dg = dm * retn * rgv * (sg * (1.0 + g * (1.0 - sg)))
                dretn = dm * rgv * silu
                d_o = rr * dretn - ret * (rr * rr * rr) * jnp.mean(ret * dretn, axis=-1, keepdims=True)
                st = st_ref[c, hd]
                d_s = dstate[hd]
                a_ = _bdot(rq, rk, NT) * intra
                d_a = _bdot(d_o, v, NT) * intra
                d_qr = _bdot(d_a, rk, NN) + _bdot(d_o, st, NT) * qd
                d_kr = _bdot(d_a, rq, TN) + _bdot(v, d_s, NT) * kd
                d_v = _bdot(a_, d_o, TN) + _bdot(rk * kd, d_s, NN)
                dstate[hd] = d_s * cd + _bdot(rq * qd, d_o, TN)
                dh_ref[rows, pl.ds(hd * Dh, Dh)] = _unrot(d_qr, cs, sn).astype(BF16)
                dh_ref[rows, pl.ds(R + hd * Dh, Dh)] = (_unrot(d_kr, cs, sn) * scale).astype(BF16)
                dh_ref[rows, pl.ds(2 * R + hd * Dh, Dh)] = d_v.astype(BF16)
                dh_ref[rows, pl.ds(3 * R + hd * Dh, Dh)] = dg.astype(BF16)

    rb = lambda i: nb - 1 - i
    return pl.pallas_call(
        body, name=name, grid=(nb,),
        in_specs=[pl.BlockSpec((ts, 4 * R), lambda i: (rb(i), 0)),
                  pl.BlockSpec((ts, Dh // 2), lambda i: (rb(i), 0)), pl.BlockSpec((ts, Dh // 2), lambda i: (rb(i), 0)),
                  pl.BlockSpec((1, R), lambda i: (0, 0)),
                  pl.BlockSpec((ch, H, Dh, Dh), lambda i: (rb(i), 0, 0, 0)),
                  pl.BlockSpec((ts, R), lambda i: (rb(i), 0)),
                  pl.BlockSpec((ts, R), lambda i: (rb(i), 0))],
        out_specs=[pl.BlockSpec((ts, 4 * R), lambda i: (rb(i), 0)), pl.BlockSpec((1, R), lambda i: (0, 0))],
        out_shape=[jax.ShapeDtypeStruct((S, 6 * R), BF16), jax.ShapeDtypeStruct((1, R), F32)],
        scratch_shapes=[pltpu.VMEM((H, Dh, Dh), F32)],
        compiler_params=_cparams(("arbitrary",)),
    )(h, cos, sin, ret_g, states, ret_raw, dmix)


GELU_C = math.sqrt(2.0 / math.pi)
GELU_A = 0.044715


def _gelu_parts(y):
    t = jnp.tanh(GELU_C * (y + GELU_A * y * y * y))
    val = 0.5 * y * (1.0 + t)
    grad = 0.5 * (1.0 + t) + 0.5 * y * (1.0 - t * t) * GELU_C * (1.0 + 3.0 * GELU_A * y * y)
    return val, grad


def _neg_expm1(x):
    series = -x * (1.0 + x * (1.0 / 2.0) * (1.0 + x * (1.0 / 3.0) * (1.0 + x * (1.0 / 4.0) * (
        1.0 + x * (1.0 / 5.0) * (1.0 + x * (1.0 / 6.0) * (1.0 + x * (1.0 / 7.0)))))))
    return jnp.where(x > -0.35, series, 1.0 - jnp.exp(x))


def _log_sigmoid(x):
    return jnp.minimum(x, 0.0) - jnp.log1p(jnp.exp(-jnp.abs(x)))


def _lru_gates(uc, wa_ref, ba_ref, wx_ref, bx_ref):
    nbk = wa_ref.shape[0]
    bd = wa_ref.shape[1]
    rs, gs = [], []
    for n in range(nbk):
        ucn = uc[:, n * bd:(n + 1) * bd]
        rs.append(_sigmoid(_bdot(ucn, wa_ref[n], NN) + ba_ref[:, pl.ds(n * bd, bd)]))
        gs.append(_sigmoid(_bdot(ucn, wx_ref[n], NN) + bx_ref[:, pl.ds(n * bd, bd)]))
    return jnp.concatenate(rs, axis=-1), jnp.concatenate(gs, axis=-1)


def _lru_fwd(h, conv_w, conv_b, wa, ba, wx, bx, lam, *, name, ts=256):
    S = h.shape[0]
    W = lam.shape[1]
    K = conv_w.shape[0]
    ts = min(ts, S)

    def body(u_ref, y_ref, cw_ref, cb_ref, wa_ref, ba_ref, wx_ref, bx_ref, lam_ref, hl_ref, mix_ref, tail, hlast):
        i = pl.program_id(0)

        @pl.when(i == 0)
        def _():
            tail[...] = jnp.zeros_like(tail)
            hlast[...] = jnp.zeros_like(hlast)

        u = u_ref[...]
        tl = tail[...]
        uc = cb_ref[...] + cw_ref[K - 1:K, :] * u
        for k in range(K - 1):
            uc = uc + cw_ref[k:k + 1, :] * _shift_down(u, tl, K - 1 - k)
        tail[...] = u[ts - SUB:ts]
        r, ig = _lru_gates(uc, wa_ref, ba_ref, wx_ref, bx_ref)
        log_a = LRU_C * r * _log_sigmoid(lam_ref[...])
        a = jnp.exp(log_a)
        b = jnp.sqrt(_neg_expm1(2.0 * log_a)) * (ig * uc)
        rid = _row_iota((ts, W))
        d = 1
        while d < ts:
            a_s = jnp.where(rid < d, 1.0, pltpu.roll(a, d, 0))
            b_s = jnp.where(rid < d, 0.0, pltpu.roll(b, d, 0))
            b = a * b_s + b
            a = a * a_s
            d *= 2
        hcur = a * hlast[SUB - 1:SUB, :] + b
        hlast[...] = hcur[ts - SUB:ts]
        hl_ref[...] = hcur
        gy, _ = _gelu_parts(y_ref[...])
        mix_ref[...] = (hcur * gy).astype(BF16)

    full = lambda shape: pl.BlockSpec(shape, lambda i: tuple(0 for _ in shape))
    return pl.pallas_call(
        body, name=name, grid=(S // ts,),
        in_specs=[pl.BlockSpec((ts, W), lambda i: (i, 4)), pl.BlockSpec((ts, W), lambda i: (i, 5)),
                  full(conv_w.shape), full(conv_b.shape), full(wa.shape), full(ba.shape), full(wx.shape),
                  full(bx.shape), full(lam.shape)],
        out_specs=[pl.BlockSpec((ts, W), lambda i: (i, 0)), pl.BlockSpec((ts, W), lambda i: (i, 1))],
        out_shape=[jax.ShapeDtypeStruct((S, W), F32), jax.ShapeDtypeStruct((S, 2 * W), BF16)],
        scratch_shapes=[pltpu.VMEM((SUB, W), F32), pltpu.VMEM((SUB, W), F32)],
        compiler_params=_cparams(("arbitrary",)),
    )(h, h, conv_w, conv_b, wa, ba, wx, bx, lam)


def _lru_bwd(h, hl, dmix, dh, conv_w, conv_b, wa, ba, wx, bx, lam, *, name, ts=256):
    S = h.shape[0]
    W = lam.shape[1]
    K = conv_w.shape[0]
    nbk, bd = wa.shape[0], wa.shape[1]
    ts = min(ts, S)
    nb = S // ts
    t8 = ts // SUB

    def body(u_ref, y_ref, uh_ref, hl_ref, hh_ref, dm_ref, cw_ref, cb_ref, wa_ref, ba_ref, wx_ref, bx_ref, lam_ref,
             dh_in, dh_ref, dcw_ref, dcb_ref, dwa_ref, dba_ref, dwx_ref, dbx_ref, dlam_ref, carry, head):
        i = pl.program_id(0)
        blk = nb - 1 - i

        @pl.when(i == 0)
        def _():
            carry[...] = jnp.zeros_like(carry)
            head[...] = jnp.zeros_like(head)
            for ref in (dcw_ref, dcb_ref, dwa_ref, dba_ref, dwx_ref, dbx_ref, dlam_ref):
                ref[...] = jnp.zeros_like(ref)

        inside = (blk > 0).astype(F32)
        u = u_ref[...]
        tl = uh_ref[...] * inside
        sh = [_shift_down(u, tl, K - 1 - k) for k in range(K)]
        uc = cb_ref[...]
        for k in range(K):
            uc = uc + cw_ref[k:k + 1, :] * sh[k]
        r, ig = _lru_gates(uc, wa_ref, ba_ref, wx_ref, bx_ref)
        lam_v = lam_ref[...]
        ls = _log_sigmoid(lam_v)
        log_a = LRU_C * r * ls
        a = jnp.exp(log_a)
        mult = jnp.sqrt(_neg_expm1(2.0 * log_a))
        hcur = hl_ref[...]
        hprev = _shift_down(hcur, hh_ref[...] * inside, 1)
        gy, dgy = _gelu_parts(y_ref[...])
        dm = dm_ref[...]
        d_y = dm * hcur * dgy
        rid = _row_iota((ts, W))
        bq = dm * gy + jnp.where(rid == ts - 1, carry[0:1, :], 0.0)
        aq = jnp.where(rid == ts - 1, 0.0, pltpu.roll(a, ts - 1, 0))
        d = 1
        while d < ts:
            a_s = jnp.where(rid >= ts - d, 0.0, pltpu.roll(aq, ts - d, 0))
            b_s = jnp.where(rid >= ts - d, 0.0, pltpu.roll(bq, ts - d, 0))
            bq = bq + aq * b_s
            aq = aq * a_s
            d *= 2
        lam_t = bq
        carry[...] = (a * lam_t)[0:SUB]
        d_a = lam_t * hprev
        d_mult = lam_t * (ig * uc)
        d_i = lam_t * mult * uc
        d_uc = lam_t * mult * ig
        d_log_a = d_a * a - d_mult * (a * a) / mult
        d_r = d_log_a * (LRU_C * ls)
        dlam_ref[...] += jnp.sum(d_log_a * (LRU_C * r), axis=0, keepdims=True) * _sigmoid(-lam_v)
        d_pr = d_r * r * (1.0 - r)
        d_pi = d_i * ig * (1.0 - ig)
        dba_ref[...] += jnp.sum(d_pr, axis=0, keepdims=True)
        dbx_ref[...] += jnp.sum(d_pi, axis=0, keepdims=True)
        extra = []
        for n in range(nbk):
            sl = slice(n * bd, (n + 1) * bd)
            ucn = uc[:, sl]
            dwa_ref[n] += _bdot(ucn, d_pr[:, sl], TN)
            dwx_ref[n] += _bdot(ucn, d_pi[:, sl], TN)
            extra.append(_bdot(d_pr[:, sl], wa_ref[n], NT) + _bdot(d_pi[:, sl], wx_ref[n], NT))
        d_uc = d_uc + jnp.concatenate(extra, axis=-1)
        dcb_ref[...] += jnp.sum(d_uc, axis=0, keepdims=True)
        rid8 = _row_iota((SUB, W))
        dcw = jnp.zeros((SUB, W), F32)
        for k in range(K):
            dcw = dcw + jnp.where(rid8 == k, jnp.sum(d_uc * sh[k], axis=0, keepdims=True), 0.0)
        dcw_ref[...] += dcw
        hd = head[...]
        d_u = cw_ref[K - 1:K, :] * d_uc
        for j in range(1, K):
            d_u = d_u + cw_ref[K - 1 - j:K - j, :] * _shift_up(d_uc, hd, j)
        head[...] = d_uc[0:SUB]
        dh_ref[:, 0:W] = d_u.astype(BF16)
        dh_ref[:, W:2 * W] = d_y.astype(BF16)

    rb = lambda i: nb - 1 - i
    prev8 = lambda i: jnp.maximum(rb(i) * t8 - 1, 0)
    full = lambda shape: pl.BlockSpec(shape, lambda i: tuple(0 for _ in shape))
    small = [jax.ShapeDtypeStruct((SUB, W), F32), jax.ShapeDtypeStruct((1, W), F32),
             jax.ShapeDtypeStruct(wa.shape, F32), jax.ShapeDtypeStruct((1, W), F32),
             jax.ShapeDtypeStruct(wx.shape, F32), jax.ShapeDtypeStruct((1, W), F32),
             jax.ShapeDtypeStruct((1, W), F32)]
    return pl.pallas_call(
        body, name=name, grid=(nb,),
        in_specs=[pl.BlockSpec((ts, W), lambda i: (rb(i), 4)), pl.BlockSpec((ts, W), lambda i: (rb(i), 5)),
                  pl.BlockSpec((SUB, W), lambda i: (prev8(i), 4)),
                  pl.BlockSpec((ts, W), lambda i: (rb(i), 0)), pl.BlockSpec((SUB, W), lambda i: (prev8(i), 0)),
                  pl.BlockSpec((ts, W), lambda i: (rb(i), 1)),
                  full(conv_w.shape), full(conv_b.shape), full(wa.shape), full(ba.shape), full(wx.shape),
                  full(bx.shape), full(lam.shape), pl.BlockSpec(memory_space=pl.ANY)],
        out_specs=[pl.BlockSpec((ts, 2 * W), lambda i: (rb(i), 2))] + [full(s.shape) for s in small],
        out_shape=[jax.ShapeDtypeStruct(dh.shape, dh.dtype)] + small,
        scratch_shapes=[pltpu.VMEM((SUB, W), F32), pltpu.VMEM((SUB, W), F32)],
        input_output_aliases={13: 0},
        compiler_params=_cparams(("arbitrary",)),
    )(h, h, h, hl, hl, dmix, conv_w, conv_b, wa, ba, wx, bx, lam, dh)


def _xattn_fwd(q, km, vm, *, name, ts=512):
    S, D = q.shape
    M = km.shape[0]
    H = XA_HEADS
    Dh = D // H
    ts = min(ts, S)
    scale = Dh ** -0.5

    def body(q_ref, k_ref, v_ref, o_ref):
        for hd in range(H):
            cols = pl.ds(hd * Dh, Dh)
            s = _bdot(q_ref[:, cols], k_ref[:, cols], NT) * scale
            s = s - jnp.max(s, axis=-1, keepdims=True)
            e = jnp.exp(s)
            p = e / jnp.sum(e, axis=-1, keepdims=True)
            o_ref[:, cols] = _bdot(p, v_ref[:, cols], NN).astype(o_ref.dtype)

    return pl.pallas_call(
        body, name=name, grid=(S // ts,),
        in_specs=[pl.BlockSpec((ts, D), lambda i: (i, 0)), pl.BlockSpec((M, D), lambda i: (0, 0)),
                  pl.BlockSpec((M, D), lambda i: (0, 0))],
        out_specs=pl.BlockSpec((ts, D), lambda i: (i, 0)),
        out_shape=jax.ShapeDtypeStruct((S, D), BF16),
        compiler_params=_cparams(("parallel",)),
    )(q, km, vm)


def _xattn_bwd(q, km, vm, d_o, *, name, ts=512):
    S, D = q.shape
    M = km.shape[0]
    H = XA_HEADS
    Dh = D // H
    ts = min(ts, S)
    scale = Dh ** -0.5

    def body(q_ref, k_ref, v_ref, do_ref, dq_ref, dk_ref, dv_ref):
        i = pl.program_id(0)

        @pl.when(i == 0)
        def _():
            dk_ref[...] = jnp.zeros_like(dk_ref)
            dv_ref[...] = jnp.zeros_like(dv_ref)

        for hd in range(H):
            cols = pl.ds(hd * Dh, Dh)
            qh, kh, vh, doh = q_ref[:, cols], k_ref[:, cols], v_ref[:, cols], do_ref[:, cols]
            s = _bdot(qh, kh, NT) * scale
            s = s - jnp.max(s, axis=-1, keepdims=True)
            e = jnp.exp(s)
            p = e / jnp.sum(e, axis=-1, keepdims=True)
            dp = _bdot(doh, vh, NT)
            ds = p * (dp - jnp.sum(dp * p, axis=-1, keepdims=True)) * scale
            dq_ref[:, cols] = _bdot(ds, kh, NN).astype(dq_ref.dtype)
            dk_ref[:, cols] += _bdot(ds, qh, TN)
            dv_ref[:, cols] += _bdot(p, doh, TN)

    row = pl.BlockSpec((ts, D), lambda i: (i, 0))
    mem = pl.BlockSpec((M, D), lambda i: (0, 0))
    return pl.pallas_call(
        body, name=name, grid=(S // ts,), in_specs=[row, mem, mem, row], out_specs=[row, mem, mem],
        out_shape=[jax.ShapeDtypeStruct((S, D), BF16), jax.ShapeDtypeStruct((M, D), F32),
                   jax.ShapeDtypeStruct((M, D), F32)],
        compiler_params=_cparams(("arbitrary",)),
    )(q, km, vm, d_o)


def _conv_rows(v, tail, cw_ref, cb_ref):
    K = cw_ref.shape[0]
    sh = [_shift_down(v, tail, K - 1 - k) for k in range(K)]
    out = cb_ref[...]
    for k in range(K):
        out = out + cw_ref[k:k + 1, :] * sh[k]
    return out, sh


FFN_SUB = 256


def _ffn_up_gate(xn, w_up, cw, cb, *, name, tm=1024, tn=512):
    S, D = xn.shape
    F2 = w_up.shape[1]
    F = F2 // 2
    tm, tn = min(tm, S), min(tn, F)
    sub = min(FFN_SUB, tm)
    nj = F // tn
    K = cw.shape[0]

    def body(x_ref, wa_ref, wb_ref, cwa_ref, cwb_ref, cba_ref, cbb_ref, act_ref, ha_ref, hb_ref, ta, tb):
        i = pl.program_id(1)

        @pl.when(i == 0)
        def _():
            ta[...] = jnp.zeros_like(ta)
            tb[...] = jnp.zeros_like(tb)

        tail_a, tail_b = ta[...], tb[...]
        for s in range(tm // sub):
            rows = pl.ds(s * sub, sub)
            xs = x_ref[rows, :]
            ha = _bdot(xs, wa_ref[...], NN)
            hb = _bdot(xs, wb_ref[...], NN)
            ac, _ = _conv_rows(ha, tail_a, cwa_ref, cba_ref)
            bc, _ = _conv_rows(hb, tail_b, cwb_ref, cbb_ref)
            tail_a, tail_b = ha[sub - SUB:sub], hb[sub - SUB:sub]
            ha_ref[rows, :] = ha
            hb_ref[rows, :] = hb
            act_ref[rows, :] = (ac * _sigmoid(ac) * bc).astype(act_ref.dtype)
        ta[...] = tail_a
        tb[...] = tail_b

    blk = pl.BlockSpec((tm, tn), lambda j, i: (i, j))
    return pl.pallas_call(
        body, name=name, grid=(nj, S // tm),
        in_specs=[pl.BlockSpec((tm, D), lambda j, i: (i, 0)),
                  pl.BlockSpec((D, tn), lambda j, i: (0, j)), pl.BlockSpec((D, tn), lambda j, i: (0, j + nj)),
                  pl.BlockSpec((K, tn), lambda j, i: (0, j)), pl.BlockSpec((K, tn), lambda j, i: (0, j + nj)),
                  pl.BlockSpec((1, tn), lambda j, i: (0, j)), pl.BlockSpec((1, tn), lambda j, i: (0, j + nj))],
        out_specs=[blk, blk, blk],
        out_shape=[jax.ShapeDtypeStruct((S, F), BF16), jax.ShapeDtypeStruct((S, F), F32),
                   jax.ShapeDtypeStruct((S, F), F32)],
        scratch_shapes=[pltpu.VMEM((SUB, tn), F32), pltpu.VMEM((SUB, tn), F32)],
        compiler_params=_cparams(("parallel", "arbitrary")),
    )(xn, w_up, w_up, cw, cw, cb, cb)


def _ffn_bwd(dx, w_down, hh_a, hh_b, act, xn, cw, cb, *, name, tm=512, tn=512):
    S, D = dx.shape
    F = hh_a.shape[1]
    tm, tn = min(tm, S), min(tn, F)
    sub = min(FFN_SUB, tm)
    nj = F // tn
    nb = S // tm
    t8 = tm // SUB
    K = cw.shape[0]

    def body(dx_ref, wd_ref, a_ref, ah_ref, b_ref, bh_ref, act_ref, xn_ref, cwa_ref, cwb_ref, cba_ref, cbb_ref,
             da_ref, db_ref, ga_ref, gb_ref, dwd_ref, dwa_ref, dwb_ref, ha, hb, acc_d, acc_a, acc_b):
        i = pl.program_id(1)
        blk = nb - 1 - i

        @pl.when(i == 0)
        def _():
            for ref in (ha, hb, ga_ref, gb_ref, acc_d, acc_a, acc_b):
                ref[...] = jnp.zeros_like(ref)

        inside = (blk > 0).astype(F32)
        rid8 = _row_iota((SUB, tn))
        heads = [ha[...], hb[...]]
        gsums = [jnp.zeros((SUB, tn), F32), jnp.zeros((SUB, tn), F32)]
        for s in reversed(range(tm // sub)):
            rows = pl.ds(s * sub, sub)
            before = pl.ds(s * sub - SUB, SUB)
            dxs = dx_ref[rows, :]
            dv = _bdot(dxs, wd_ref[...], NT)
            acc_d[...] += _bdot(act_ref[rows, :], dxs, TN)
            tail_a = a_ref[before, :] if s > 0 else ah_ref[...] * inside
            tail_b = b_ref[before, :] if s > 0 else bh_ref[...] * inside
            ac, sha = _conv_rows(a_ref[rows, :], tail_a, cwa_ref, cba_ref)
            bc, shb = _conv_rows(b_ref[rows, :], tail_b, cwb_ref, cbb_ref)
            sg = _sigmoid(ac)
            d_bc = dv * ac * sg
            d_ac = dv * bc * sg * (1.0 + ac * (1.0 - sg))
            for which, (d_c, sh, cw_ref, o_ref, acc_u) in enumerate(((d_ac, sha, cwa_ref, da_ref, acc_a),
                                                                      (d_bc, shb, cwb_ref, db_ref, acc_b))):
                d_in = cw_ref[K - 1:K, :] * d_c
                for j in range(1, K):
                    d_in = d_in + cw_ref[K - 1 - j:K - j, :] * _shift_up(d_c, heads[which], j)
                heads[which] = d_c[0:SUB]
                d_in = d_in.astype(o_ref.dtype)
                o_ref[rows, :] = d_in
                acc_u[...] += _bdot(xn_ref[rows, :], d_in, TN)
                gsum = gsums[which] + jnp.where(rid8 == K, jnp.sum(d_c, axis=0, keepdims=True), 0.0)
                for k in range(K):
                    gsum = gsum + jnp.where(rid8 == k, jnp.sum(d_c * sh[k], axis=0, keepdims=True), 0.0)
                gsums[which] = gsum
        ha[...], hb[...] = heads
        ga_ref[...] += gsums[0]
        gb_ref[...] += gsums[1]

        @pl.when(i == nb - 1)
        def _():
            dwd_ref[...] = acc_d[...].astype(dwd_ref.dtype)
            dwa_ref[...] = acc_a[...].astype(dwa_ref.dtype)
            dwb_ref[...] = acc_b[...].astype(dwb_ref.dtype)

    rb = lambda i: nb - 1 - i
    prev8 = lambda i: jnp.maximum(rb(i) * t8 - 1, 0)
    blk = pl.BlockSpec((tm, tn), lambda j, i: (rb(i), j))
    halo = pl.BlockSpec((SUB, tn), lambda j, i: (prev8(i), j))
    acc = pl.BlockSpec((SUB, tn), lambda j, i: (0, j))
    rows_d = pl.BlockSpec((tm, D), lambda j, i: (rb(i), 0))
    up_blk = pl.BlockSpec((D, tn), lambda j, i: (0, j))
    return pl.pallas_call(
        body, name=name, grid=(nj, nb),
        in_specs=[rows_d, pl.BlockSpec((tn, D), lambda j, i: (j, 0)), blk, halo, blk, halo, blk, rows_d,
                  pl.BlockSpec((K, tn), lambda j, i: (0, j)), pl.BlockSpec((K, tn), lambda j, i: (0, j + nj)),
                  pl.BlockSpec((1, tn), lambda j, i: (0, j)), pl.BlockSpec((1, tn), lambda j, i: (0, j + nj))],
        out_specs=[blk, blk, acc, acc, pl.BlockSpec((tn, D), lambda j, i: (j, 0)), up_blk, up_blk],
        out_shape=[jax.ShapeDtypeStruct((S, F), BF16), jax.ShapeDtypeStruct((S, F), BF16),
                   jax.ShapeDtypeStruct((SUB, F), F32), jax.ShapeDtypeStruct((SUB, F), F32),
                   jax.ShapeDtypeStruct((F, D), BF16), jax.ShapeDtypeStruct((D, F), BF16),
                   jax.ShapeDtypeStruct((D, F), BF16)],
        scratch_shapes=[pltpu.VMEM((SUB, tn), F32), pltpu.VMEM((SUB, tn), F32), pltpu.VMEM((tn, D), F32),
                        pltpu.VMEM((D, tn), F32), pltpu.VMEM((D, tn), F32)],
        compiler_params=_cparams(("parallel", "arbitrary")),
    )(dx, w_down, hh_a, hh_a, hh_b, hh_b, act, xn, cw, cw, cb, cb)


ADAM_BLOCK_ELEMS = 128 * 1024


def _adamw(w, m, v, parts, *, name):
    R, C = w.shape
    n = parts.shape[0]
    tr = R
    for cand in (1024, 512, 256, 128, 64, 32, 16):
        if R % cand == 0 and cand * C <= ADAM_BLOCK_ELEMS:
            tr = cand
            break
    c1 = 1.0 - ADAM_B1 ** ADAM_STEP
    c2 = 1.0 - ADAM_B2 ** ADAM_STEP

    def body(w_ref, m_ref, v_ref, p_ref, g_ref, d_ref, nm_ref, nv_ref):
        g = p_ref[0].astype(F32)
        for k in range(1, n):
            g = g + p_ref[k].astype(F32)
        m_new = ADAM_B1 * m_ref[...] + (1.0 - ADAM_B1) * g
        v_new = ADAM_B2 * v_ref[...] + (1.0 - ADAM_B2) * (g * g)
        m_hat = m_new / c1
        v_hat = v_new / c2
        g_ref[...] = g
        d_ref[...] = -ADAM_LR * (m_hat / (jnp.sqrt(v_hat) + ADAM_EPS) + ADAM_WD * w_ref[...])
        nm_ref[...] = m_new
        nv_ref[...] = v_new

    blk = pl.BlockSpec((tr, C), lambda i: (i, 0))
    sds = jax.ShapeDtypeStruct((R, C), F32)
    return pl.pallas_call(
        body, name=name, grid=(R // tr,),
        in_specs=[blk, blk, blk, pl.BlockSpec((n, tr, C), lambda i: (0, i, 0))],
        out_specs=[blk, blk, blk, blk], out_shape=[sds, sds, sds, sds],
        compiler_params=_cparams(("parallel",)),
    )(w, m, v, parts)


def _mesh_place():
    x, y, c = lax.axis_index("x"), lax.axis_index("y"), lax.axis_index("c")
    others = [(1 - x, y), (x, 1 - y), (1 - x, 1 - y)]
    return x, y, c, others


HBM_SPEC = pl.BlockSpec(memory_space=pltpu.HBM)
SEM_SPEC = pl.BlockSpec(memory_space=pltpu.SEMAPHORE)
ANY_SPEC = pl.BlockSpec(memory_space=pl.ANY)
EFFECT = pltpu.SideEffectType.DATAFLOW_SIDE_EFFECTING


def _in_hbm(a):
    return pltpu.with_memory_space_constraint(a, pltpu.HBM)


def _split_start(srcs, lands, copies, n_cp, *, name):
    n_s, n_l = len(srcs), len(lands)

    def body(*refs):
        src_refs, land_refs = refs[:n_s], refs[n_s:n_s + n_l]
        ssem, rsem = refs[n_s + n_l], refs[n_s + n_l + 1]
        token = refs[-1]
        for outgoing, _ in copies(src_refs, land_refs, ssem, rsem):
            outgoing.start()
        token[...] = jnp.zeros_like(token)

    outs = pl.pallas_call(
        body, name=name,
        out_shape=(pltpu.SemaphoreType.DMA((n_cp,)), pltpu.SemaphoreType.DMA((n_cp,)),
                   *[pltpu.HBM(a.shape, a.dtype) for a in srcs], *[pltpu.HBM(a.shape, a.dtype) for a in lands],
                   jax.ShapeDtypeStruct((SUB, LANE), F32)),
        in_specs=[HBM_SPEC] * (n_s + n_l),
        out_specs=(SEM_SPEC, SEM_SPEC, *[HBM_SPEC] * (n_s + n_l), pl.BlockSpec(memory_space=pltpu.VMEM)),
        input_output_aliases={i: 2 + i for i in range(n_s + n_l)},
        compiler_params=pltpu.CompilerParams(has_side_effects=EFFECT),
    )(*[_in_hbm(a) for a in srcs], *[_in_hbm(a) for a in lands])
    ssem, rsem = outs[0], outs[1]
    return ssem, rsem, list(outs[2:2 + n_s]), list(outs[2 + n_s:2 + n_s + n_l]), outs[-1]


def _split_wait(srcs, lands, ssem, rsem, after, copies, *, name):
    n_s, n_l = len(srcs), len(lands)

    def body(*refs):
        src_refs, land_refs = refs[:n_s], refs[n_s:n_s + n_l]
        s_ref, r_ref = refs[n_s + n_l], refs[n_s + n_l + 1]
        for outgoing, incoming in copies(src_refs, land_refs, s_ref, r_ref):
            outgoing.wait_send()
            incoming.wait_recv()

    outs = pl.pallas_call(
        body, name=name,
        out_shape=(*[pltpu.HBM(a.shape, a.dtype) for a in srcs], *[pltpu.HBM(a.shape, a.dtype) for a in lands]),
        in_specs=[HBM_SPEC] * (n_s + n_l) + [SEM_SPEC, SEM_SPEC, ANY_SPEC], out_specs=[HBM_SPEC] * (n_s + n_l),
        input_output_aliases={i: i for i in range(n_s + n_l)},
        compiler_params=pltpu.CompilerParams(has_side_effects=EFFECT),
    )(*srcs, *lands, ssem, rsem, after)
    return list(outs[:n_s]), list(outs[n_s:])


PLACE_BLOCK_ELEMS = 512 * 1024


def _place_rows(r, w):
    return _div_tile(r, max(16, PLACE_BLOCK_ELEMS // w), 16)


def _cast_place(shard, chip, axis, after, *, name):
    r, w = shard.shape
    tr = _place_rows(r, w)
    nb = r // tr
    full = (r * N_CHIP, w) if axis == 0 else (r, w * N_CHIP)
    has_after = after is not None

    def body(chip_ref, s_ref, *rest):
        rest[-1][...] = s_ref[...].astype(BF16)

    out_map = (lambda i, ch: (ch[0] * nb + i, 0)) if axis == 0 else (lambda i, ch: (i, ch[0]))
    grid_spec = pltpu.PrefetchScalarGridSpec(
        num_scalar_prefetch=1, grid=(nb,),
        in_specs=[pl.BlockSpec((tr, w), lambda i, ch: (i, 0))] + has_after * [ANY_SPEC],
        out_specs=pl.BlockSpec((tr, w), out_map))
    return pl.pallas_call(body, name=name, grid_spec=grid_spec, out_shape=jax.ShapeDtypeStruct(full, BF16),
                          compiler_params=_cparams(("parallel",)))(chip, shard, *(has_after * [after]))


def _slot_place(g, ids, axis, *, name):
    r, w = (g.shape[0] // N_CHIP, g.shape[1]) if axis == 0 else (g.shape[0], g.shape[1] // N_CHIP)
    tr = _place_rows(r, w)
    nb = r // tr

    def body(ids_ref, g_ref, o_ref):
        o_ref[...] = g_ref[...]

    in_map = (lambda i, ids_: (ids_[0] * nb + i, 0)) if axis == 0 else (lambda i, ids_: (i, ids_[0]))
    grid_spec = pltpu.PrefetchScalarGridSpec(
        num_scalar_prefetch=1, grid=(nb,), in_specs=[pl.BlockSpec((tr, w), in_map)],
        out_specs=pl.BlockSpec((None, tr, w), lambda i, ids_: (ids_[1], i, 0)))
    return pl.pallas_call(body, name=name, grid_spec=grid_spec, out_shape=jax.ShapeDtypeStruct((N_DEV, r, w), g.dtype),
                          compiler_params=_cparams(("parallel",)))(ids, g)


class _WeightGather:
    def __init__(self, placed, shard_shapes, axes, splits, tag):
        self.placed, self.shard_shapes, self.axes, self.splits, self.tag = list(placed), shard_shapes, axes, splits, tag
        self.n = len(placed)

    def _region(self, land_refs, it, chip, half):
        r, w = self.shard_shapes[it]
        by_rows = self.axes[it] == 0
        if self.splits[it] and half is not None:
            rows = pl.ds(pl.multiple_of(half * (r // 2) + (chip * r if by_rows else 0), 16), r // 2)
        else:
            rows = pl.ds(chip * r if by_rows else 0, r)
        cols = pl.ds(0, w) if by_rows else pl.ds(pl.multiple_of(chip * w, LANE), w)
        return land_refs[it].at[rows, cols]

    def _ici(self, src_refs, land_refs, ssem, rsem):
        x, y, c, others = _mesh_place()
        pairs = []
        for it in range(self.n):
            for j, chip in enumerate(others):
                def mk(chip_from, it=it, j=j, chip=chip):
                    return pltpu.make_async_remote_copy(
                        src_ref=self._region(land_refs, it, 2 * x + y, c), dst_ref=self._region(land_refs, it, chip_from, c),
                        send_sem=ssem.at[3 * it + j], recv_sem=rsem.at[3 * it + j], device_id=(*chip, c),
                        device_id_type=MESH)
                pairs.append((mk(2 * x + y), mk(2 * chip[0] + chip[1])))
        return pairs

    def start(self):
        self.ssem, self.rsem, _, self.lands, token = _split_start(
            [], self.placed, self._ici, 3 * self.n, name="gather_start_" + self.tag)
        return token

    def finish(self, after):
        _, lands = _split_wait([], self.lands, self.ssem, self.rsem, after, self._ici,
                               name="gather_wait_" + self.tag)
        n = self.n
        n_fwd = 3 * sum(self.splits)
        if n_fwd == 0:
            return lands

        def body(*refs):
            out_refs = refs[n:2 * n]
            fsend, frecv = refs[2 * n:]
            x, y, c, others = _mesh_place()
            sibling = (x, y, 1 - c)

            def fwd(it, slot, chip, half):
                reg = self._region(out_refs, it, 2 * chip[0] + chip[1], half)
                return pltpu.make_async_remote_copy(src_ref=reg, dst_ref=reg, send_sem=fsend.at[slot],
                                                    recv_sem=frecv.at[slot], device_id=sibling, device_id_type=MESH)

            sends, recvs = [], []
            for it in range(n):
                if self.splits[it]:
                    for chip in others:
                        sends.append(fwd(it, len(sends), chip, c))
                        recvs.append(fwd(it, len(recvs), chip, 1 - c))
            for cp in sends:
                cp.start()
            for cp in recvs:
                cp.wait_recv()
            for cp in sends:
                cp.wait_send()

        fulls = pl.pallas_call(
            body, name="gather_d2d_" + self.tag, in_specs=[ANY_SPEC] * n, out_specs=[ANY_SPEC] * n,
            out_shape=[jax.ShapeDtypeStruct(a.shape, a.dtype) for a in lands],
            scratch_shapes=[pltpu.SemaphoreType.DMA((n_fwd,)), pltpu.SemaphoreType.DMA((n_fwd,))],
            input_output_aliases={i: i for i in range(n)},
        )(*lands)
        return list(fulls)


class _GradGather:
    def __init__(self, grads, axes, tag):
        self.grads, self.axes, self.tag = list(grads), axes, tag
        self.n = len(grads)
        self.shard_shapes = [(g.shape[0] // N_CHIP, g.shape[1]) if ax == 0 else (g.shape[0], g.shape[1] // N_CHIP)
                             for g, ax in zip(grads, axes)]

    def _piece(self, src_refs, it, chip):
        r, w = self.shard_shapes[it]
        if self.axes[it] == 0:
            return src_refs[it].at[pl.ds(pl.multiple_of(chip * r, 16), r), :]
        return src_refs[it].at[:, pl.ds(pl.multiple_of(chip * w, LANE), w)]

    PER_ITEM = 4

    def _remote(self, src_refs, land_refs, ssem, rsem):
        x, y, c, others = _mesh_place()
        me = 4 * x + 2 * y + c
        pairs = []
        for it in range(self.n):
            def mk(k, piece_chip, slot, to, it=it):
                return pltpu.make_async_remote_copy(
                    src_ref=self._piece(src_refs, it, piece_chip), dst_ref=land_refs[it].at[slot],
                    send_sem=ssem.at[self.PER_ITEM * it + k], recv_sem=rsem.at[self.PER_ITEM * it + k], device_id=to,
                    device_id_type=MESH)
            for j, chip in enumerate(others):
                chip_id = 2 * chip[0] + chip[1]
                pairs.append((mk(j, chip_id, me, (*chip, c)), mk(j, chip_id, 2 * chip_id + c, (*chip, c))))
            sibling = (x, y, 1 - c)
            pairs.append((mk(3, 2 * x + y, me, sibling), mk(3, 2 * x + y, 4 * x + 2 * y + 1 - c, sibling)))
        return pairs

    def start(self):
        x, y, c = lax.axis_index("x"), lax.axis_index("y"), lax.axis_index("c")
        ids = jnp.stack([2 * x + y, 4 * x + 2 * y + c]).astype(jnp.int32)
        lands = [_slot_place(g, ids, ax, name="grads_own_%s%d" % (self.tag, it))
                 for it, (g, ax) in enumerate(zip(self.grads, self.axes))]
        self.ssem, self.rsem, self.srcs, self.lands, token = _split_start(
            self.grads, lands, self._remote, self.PER_ITEM * self.n, name="grads_start_" + self.tag)
        return token

    def finish(self, after):
        _, lands = _split_wait(self.srcs, self.lands, self.ssem, self.rsem, after, self._remote,
                               name="grads_wait_" + self.tag)
        n = self.n

        def body(*refs):
            out_refs = refs[n:2 * n]
            fsend, frecv = refs[2 * n:]
            x, y, c, others = _mesh_place()
            sibling = (x, y, 1 - c)

            def fwd(it, j, slot):
                return pltpu.make_async_remote_copy(
                    src_ref=out_refs[it].at[slot], dst_ref=out_refs[it].at[slot], send_sem=fsend.at[3 * it + j],
                    recv_sem=frecv.at[3 * it + j], device_id=sibling, device_id_type=MESH)

            sends = [fwd(it, j, 4 * ch[0] + 2 * ch[1] + c) for it in range(n) for j, ch in enumerate(others)]
            recvs = [fwd(it, j, 4 * ch[0] + 2 * ch[1] + 1 - c) for it in range(n) for j, ch in enumerate(others)]
            for cp in sends:
                cp.start()
            for cp in recvs:
                cp.wait_recv()
            for cp in sends:
                cp.wait_send()

        outs = pl.pallas_call(
            body, name="grads_d2d_" + self.tag, in_specs=[ANY_SPEC] * n, out_specs=[ANY_SPEC] * n,
            out_shape=[jax.ShapeDtypeStruct(a.shape, a.dtype) for a in lands],
            scratch_shapes=[pltpu.SemaphoreType.DMA((3 * n,)), pltpu.SemaphoreType.DMA((3 * n,))],
            input_output_aliases={i: i for i in range(n)},
        )(*lands)
        return list(outs)


def _allreduce_small(vec, *, name):
    R, L = vec.shape

    def body(v_ref, o_ref, buf, send, recv, lsem):
        x, y, c, others = _mesh_place()
        me = 4 * x + 2 * y + c
        sibling = (x, y, 1 - c)

        def copy(k, slot, to, src=None):
            return pltpu.make_async_remote_copy(
                src_ref=buf.at[slot] if src is None else src, dst_ref=buf.at[slot], send_sem=send.at[k],
                recv_sem=recv.at[k], device_id=to, device_id_type=MESH)

        def slot_of(chip, core):
            return 4 * chip[0] + 2 * chip[1] + core

        mine = pltpu.make_async_copy(v_ref, buf.at[me], lsem)
        mine.start()
        first = [copy(0, me, sibling, src=v_ref)]
        first += [copy(1 + j, me, (*chip, c), src=v_ref) for j, chip in enumerate(others)]
        for cp in first:
            cp.start()
        passed = [copy(4 + j, slot_of(chip, c), sibling) for j, chip in enumerate(others)]
        for j, chip in enumerate(others):
            copy(1 + j, slot_of(chip, c), (*chip, c)).wait_recv()
            passed[j].start()
        copy(0, slot_of((x, y), 1 - c), sibling).wait_recv()
        for j, chip in enumerate(others):
            copy(4 + j, slot_of(chip, 1 - c), sibling).wait_recv()
        for cp in first + passed:
            cp.wait_send()
        mine.wait()
        total = buf[0]
        for k in range(1, N_DEV):
            total = total + buf[k]
        o_ref[...] = total

    return pl.pallas_call(
        body, name=name, in_specs=[pl.BlockSpec(memory_space=pltpu.VMEM)],
        out_specs=pl.BlockSpec(memory_space=pltpu.VMEM), out_shape=jax.ShapeDtypeStruct((R, L), F32),
        scratch_shapes=[pltpu.VMEM((N_DEV, R, L), F32), pltpu.SemaphoreType.DMA((7,)), pltpu.SemaphoreType.DMA((7,)),
                        pltpu.SemaphoreType.DMA],
        compiler_params=pltpu.CompilerParams(vmem_limit_bytes=VMEM_LIMIT),
    )(vec)


PACK_ALIGN = 1024


def _pack(arrs, row_multiple):
    flat = []
    for a in arrs:
        f = a.reshape(-1).astype(F32)
        flat.append(jnp.pad(f, (0, (-f.shape[0]) % PACK_ALIGN)))
    v = jnp.concatenate(flat)
    v = jnp.pad(v, (0, (-v.shape[0]) % (LANE * row_multiple)))
    return v.reshape(-1, LANE)


def _unpack(v, shapes):
    flat = v.reshape(-1)
    out, off = [], 0
    for s in shapes:
        size = math.prod(s)
        out.append(flat[off:off + size].reshape(s))
        off += size + (-size) % PACK_ALIGN
    return out


def _tile(dim, target):
    for cand in (1024, 512, 256, 128):
        if cand <= target and dim % cand == 0:
            return cand
    return dim


def _div_tile(dim, cap, mult=LANE):
    best = None
    for cand in range(mult, min(cap, dim) + 1, mult):
        if dim % cand == 0:
            best = cand
    return dim if best is None else best


WEIGHT_NAMES = ('norm1_g', 'w_in', 'ret_g', 'rg_conv_w', 'rg_conv_b', 'rg_wa', 'rg_ba', 'rg_wx', 'rg_bx', 'rg_lambda',
                'w_out', 'norm2_g', 'norm_mem_g', 'xa_wq', 'xa_wk', 'xa_wv', 'xa_wo', 'norm3_g', 'ffn_w_up',
                'ffn_conv_w', 'ffn_conv_b', 'ffn_w_down', 'final_g')
BIG_AXIS = {'w_in': 1, 'w_out': 0, 'xa_wq': 0, 'xa_wk': 0, 'xa_wv': 0, 'xa_wo': 0, 'ffn_w_up': 1, 'ffn_w_down': 0}
SMALL_SHARDED = ('rg_conv_w', 'ffn_conv_w')


def _step(x, mem, positions, loss_target, W, Mo, Vo):
    S, D = x.shape[1], x.shape[2]
    xs, mems, tgt = x[0], mem[0], loss_target[0]
    n_mem = mems.shape[0]
    pos_col = positions.reshape(S, 1)
    chip = 2 * lax.axis_index("x") + lax.axis_index("y")

    big = list(BIG_AXIS)
    shards = {n: W[n][0] for n in big}
    G = {}
    gather_groups = (('w_in', 'rg_conv_w'), ('w_out', 'xa_wq', 'xa_wk', 'xa_wv', 'xa_wo'),
                     ('ffn_w_up', 'ffn_conv_w'), ('ffn_w_down',))
    gathers, tok = [], None
    chip1 = jnp.reshape(chip, (1,)).astype(jnp.int32)
    for gi, names in enumerate(gather_groups):
        placed = []
        for n in names:
            if n in BIG_AXIS:
                placed.append(_cast_place(shards[n], chip1, BIG_AXIS[n], tok, name="place_" + n))
            else:
                s = W[n][0] if tok is None else W[n][0] + tok[0, 0]
                full = lax.empty((s.shape[0], s.shape[1] * N_CHIP), s.dtype)
                placed.append(lax.dynamic_update_slice(full, s, (0, chip * s.shape[1])))
        ag = _WeightGather(placed, [W[n][0].shape for n in names], [BIG_AXIS.get(n, 1) for n in names],
                           [n in BIG_AXIS for n in names], "g%d" % gi)
        tok = ag.start()
        gathers.append(ag)

    def finish_gather(gi, after):
        G.update(zip(gather_groups[gi], gathers[gi].finish(after)))

    finish_gather(0, tok)
    R = W['ret_g'].shape[1]
    Wl = W['rg_lambda'].shape[1]
    IN = W['w_in'].shape[2] * N_CHIP
    F2 = W['ffn_w_up'].shape[2] * N_CHIP
    F = F2 // 2

    norm1_g, norm2_g, norm3_g = W['norm1_g'] + tok[0, 0], W['norm2_g'], W['norm3_g']
    norm_mem_g, final_g, ret_g = W['norm_mem_g'], W['final_g'].reshape(1, D), W['ret_g']
    rg_cw, rg_cb = G['rg_conv_w'], W['rg_conv_b']
    wa, wx = W['rg_wa'][0], W['rg_wx'][0]
    ba, bx = W['rg_ba'].reshape(1, Wl), W['rg_bx'].reshape(1, Wl)
    lam = W['rg_lambda']
    ffn_cb = W['ffn_conv_b']

    def fwd_mm(a, wname, N, K, **kw):
        return _mm(a, G[wname], mode="nn", M=a.shape[0], N=N, K=K, tm=_tile(a.shape[0], 1024), tn=1024,
                   tk=_div_tile(K, 3072), **kw)

    def fwd_mm_norm(a, wname, res, g, name):
        return _mm(a, G[wname], mode="nn", M=a.shape[0], N=D, K=a.shape[1], tm=512, tn=D, tk=_div_tile(a.shape[1], 2048),
                   out_dtype=F32, res=res, norm_g=g, name=name)

    def bwd_x_mm(d, wname, N, K, **kw):
        return _mm(d, G[wname], mode="nt", M=d.shape[0], N=N, K=K, tm=_tile(d.shape[0], 1024),
                   tn=_div_tile(N, 1024, 256), tk=_div_tile(K, 3072), **kw)

    def bwd_w_mm(a, d, M, N, **kw):
        Ks = a.shape[0]
        return _mm(a, d, mode="tn", M=M, N=N, K=Ks, out_dtype=BF16, tm=_div_tile(M, 1024, 256),
                   tn=_div_tile(N, 1024, 256), tk=_div_tile(Ks, 2048 if d.dtype == BF16 else 1024), **kw)

    xn1 = _rmsnorm_fwd(xs, norm1_g, name="norm1_fwd")
    h = fwd_mm(xn1, 'w_in', IN, D, out_dtype=F32, name="mm_in")
    half = (R // RET_HEADS) // 2
    inv = (ROPE_BASE ** (-jnp.arange(half, dtype=F32) / half)).reshape(1, half)
    cos, sin = _rope_table(pos_col, inv, name="rope_table")
    hl, mix = _lru_fwd(h, rg_cw, rg_cb, wa, ba, wx, bx, lam, name="lru_fwd")
    ret_raw, states, mix = _ret_fwd(h, cos, sin, ret_g, mix, name="ret_fwd")
    finish_gather(1, mix)
    x1, xn2 = fwd_mm_norm(mix, 'w_out', xs, norm2_g, "mm_out")
    memn = _rmsnorm_fwd(mems, norm_mem_g, name="norm_mem_fwd")
    km = fwd_mm(memn, 'xa_wk', D, D, out_dtype=BF16, name="mm_k")
    vm = fwd_mm(memn, 'xa_wv', D, D, out_dtype=BF16, name="mm_v")
    q = fwd_mm(xn2, 'xa_wq', D, D, out_dtype=BF16, name="mm_q")
    o = _xattn_fwd(q, km, vm, name="xattn_fwd")
    x2, xn3 = fwd_mm_norm(o, 'xa_wo', x1, norm3_g, "mm_o")
    finish_gather(2, xn3)
    ffn_cw = G['ffn_conv_w']
    act, hh_a, hh_b = _ffn_up_gate(xn3, G['ffn_w_up'], ffn_cw, ffn_cb, name="ffn_up_gate")
    finish_gather(3, act)
    x3 = fwd_mm(act, 'ffn_w_down', D, F, out_dtype=F32, res=x2, name="mm_down")
    dx3, d_final, loss8, dx3h = _final_loss(x3, tgt, final_g, name="final_loss")

    gw = {}
    grad_groups = []

    def start_grads(names, tag):
        gg = _GradGather([gw[n] for n in names], [BIG_AXIS[n] for n in names], tag)
        grad_groups.append((names, gg))
        return gg.start()

    dhh_a, dhh_b, gcw_a, gcw_b, gw['ffn_w_down'], gw_up_a, gw_up_b = _ffn_bwd(
        dx3h, G['ffn_w_down'], hh_a, hh_b, act, xn3, ffn_cw, ffn_cb, name="ffn_bwd")
    gw['ffn_w_up'] = jnp.concatenate([gw_up_a, gw_up_b], axis=1)
    tok_a = start_grads(('ffn_w_down', 'ffn_w_up'), "a")
    dxn3 = bwd_x_mm(dhh_a, 'ffn_w_up', D, F, out_dtype=F32, after=tok_a, name="mm_dxn3_a")
    dxn3 = bwd_x_mm(dhh_b, 'ffn_w_up', D, F, out_dtype=F32, b_off=(0, F), res=dxn3, name="mm_dxn3_b")
    dx2, d_norm3, dx2h = _rmsnorm_bwd(x2, dxn3, norm3_g, dx3, name="norm3_bwd", emit_bf16=True)
    Kc = ffn_cw.shape[0]
    d_ffn_cw = jnp.concatenate([gcw_a[:Kc], gcw_b[:Kc]], axis=1)
    d_ffn_cb = jnp.concatenate([gcw_a[Kc:Kc + 1], gcw_b[Kc:Kc + 1]], axis=1)

    d_o = bwd_x_mm(dx2h, 'xa_wo', D, D, out_dtype=BF16, name="mm_do")
    gw['xa_wo'] = bwd_w_mm(o, dx2h, D, D, name="mm_dw_o")
    dq, dk, dv = _xattn_bwd(q, km, vm, d_o, name="xattn_bwd")
    gw['xa_wq'] = bwd_w_mm(xn2, dq, D, D, name="mm_dw_q")
    dxn2 = bwd_x_mm(dq, 'xa_wq', D, D, out_dtype=F32, name="mm_dxn2")
    gw['xa_wk'] = bwd_w_mm(memn, dk, D, D, name="mm_dw_k")
    gw['xa_wv'] = bwd_w_mm(memn, dv, D, D, name="mm_dw_v")
    dmemn = bwd_x_mm(dk, 'xa_wk', D, D, out_dtype=F32, name="mm_dmem_k")
    dmemn = bwd_x_mm(dv, 'xa_wv', D, D, out_dtype=F32, res=dmemn, name="mm_dmem_v")
    _, d_norm_mem = _rmsnorm_bwd(mems, dmemn, norm_mem_g, None, name="norm_mem_bwd")
    dx1, d_norm2, dx1h = _rmsnorm_bwd(x1, dxn2, norm2_g, dx2, name="norm2_bwd", emit_bf16=True)

    gw['w_out'] = bwd_w_mm(mix, dx1h, D, D, name="mm_dw_out")
    tok_b = start_grads(('xa_wo', 'xa_wq', 'xa_wk', 'xa_wv', 'w_out'), "b")
    dmix = bwd_x_mm(dx1h, 'w_out', D, D, out_dtype=F32, after=tok_b, name="mm_dmix")
    dh, d_ret_g = _ret_bwd(h, cos, sin, ret_g, states, ret_raw, dmix, name="ret_bwd")
    dh, d_rcw, d_rcb, d_wa, d_ba, d_wx, d_bx, d_lam = _lru_bwd(
        h, hl, dmix, dh, rg_cw, rg_cb, wa, ba, wx, bx, lam, name="lru_bwd")
    gw['w_in'] = bwd_w_mm(xn1, dh, D, IN, name="mm_dw_in")
    tok_c = start_grads(('w_in',), "c")
    dxn1 = bwd_x_mm(dh, 'w_in', D, IN, out_dtype=F32, after=tok_c, name="mm_dxn1")
    grad_x, d_norm1 = _rmsnorm_bwd(xs, dxn1, norm1_g, dx1, name="norm1_bwd")

    small_parts = {
        'norm1_g': d_norm1, 'ret_g': d_ret_g, 'rg_conv_w': d_rcw[:rg_cw.shape[0]], 'rg_conv_b': d_rcb,
        'rg_wa': d_wa, 'rg_ba': d_ba, 'rg_wx': d_wx, 'rg_bx': d_bx, 'rg_lambda': d_lam, 'norm2_g': d_norm2,
        'norm_mem_g': d_norm_mem, 'norm3_g': d_norm3, 'ffn_conv_w': d_ffn_cw, 'ffn_conv_b': d_ffn_cb,
        'final_g': d_final}
    small = [n for n in WEIGHT_NAMES if n not in BIG_AXIS]
    red_shapes = [(1,)] + [tuple(small_parts[n].shape) for n in small]
    reduced = _allreduce_small(_pack([loss8[0:1, 0:1]] + [small_parts[n] for n in small], SUB), name="allreduce_small")
    red = _unpack(reduced, red_shapes)
    loss = red[0][0]
    g_small = dict(zip(small, red[1:]))
    for n in SMALL_SHARDED:
        w_local = W[n].shape[-1]
        g_small[n] = lax.dynamic_slice_in_dim(g_small[n], chip * w_local, w_local, axis=1)

    out_g, out_d, out_m, out_v = {}, {}, {}, {}
    for names, gg in grad_groups:
        for n, land in zip(names, gg.finish(reduced)):
            g, d, m_new, v_new = _adamw(shards[n], Mo[n][0], Vo[n][0], land, name="adamw_" + n)
            out_g[n], out_d[n], out_m[n], out_v[n] = (t.reshape(W[n].shape) for t in (g, d, m_new, v_new))
    rows = 512
    pk = lambda d: _pack([d[n] for n in small], rows)
    g_pack = _pack([g_small[n] for n in small], rows)
    res_small = _adamw(pk(W), pk(Mo), pk(Vo), g_pack[None], name="adamw_small")
    shapes_small = [tuple(W[n].shape) for n in small]
    for dst, packed in zip((out_g, out_d, out_m, out_v), res_small):
        for n, val in zip(small, _unpack(packed, shapes_small)):
            dst[n] = val
    return (loss, grad_x[None], *[out_g[n] for n in WEIGHT_NAMES], *[out_d[n] for n in WEIGHT_NAMES],
            *[out_m[n] for n in WEIGHT_NAMES], *[out_v[n] for n in WEIGHT_NAMES])


def kernel(x, mem, positions, norm1_g, w_in, ret_g, rg_conv_w, rg_conv_b, rg_wa, rg_ba, rg_wx, rg_bx, rg_lambda, w_out, norm2_g, norm_mem_g, xa_wq, xa_wk, xa_wv, xa_wo, norm3_g, ffn_w_up, ffn_conv_w, ffn_conv_b, ffn_w_down, final_g, loss_target, m_norm1_g, m_w_in, m_ret_g, m_rg_conv_w, m_rg_conv_b, m_rg_wa, m_rg_ba, m_rg_wx, m_rg_bx, m_rg_lambda, m_w_out, m_norm2_g, m_norm_mem_g, m_xa_wq, m_xa_wk, m_xa_wv, m_xa_wo, m_norm3_g, m_ffn_w_up, m_ffn_conv_w, m_ffn_conv_b, m_ffn_w_down, m_final_g, v_norm1_g, v_w_in, v_ret_g, v_rg_conv_w, v_rg_conv_b, v_rg_wa, v_rg_ba, v_rg_wx, v_rg_bx, v_rg_lambda, v_w_out, v_norm2_g, v_norm_mem_g, v_xa_wq, v_xa_wk, v_xa_wv, v_xa_wo, v_norm3_g, v_ffn_w_up, v_ffn_conv_w, v_ffn_conv_b, v_ffn_w_down, v_final_g):
    W = dict(zip(WEIGHT_NAMES, (norm1_g, w_in, ret_g, rg_conv_w, rg_conv_b, rg_wa, rg_ba, rg_wx, rg_bx, rg_lambda, w_out,
                                norm2_g, norm_mem_g, xa_wq, xa_wk, xa_wv, xa_wo, norm3_g, ffn_w_up, ffn_conv_w,
                                ffn_conv_b, ffn_w_down, final_g)))
    Mo = dict(zip(WEIGHT_NAMES, (m_norm1_g, m_w_in, m_ret_g, m_rg_conv_w, m_rg_conv_b, m_rg_wa, m_rg_ba, m_rg_wx, m_rg_bx,
                                 m_rg_lambda, m_w_out, m_norm2_g, m_norm_mem_g, m_xa_wq, m_xa_wk, m_xa_wv, m_xa_wo,
                                 m_norm3_g, m_ffn_w_up, m_ffn_conv_w, m_ffn_conv_b, m_ffn_w_down, m_final_g)))
    Vo = dict(zip(WEIGHT_NAMES, (v_norm1_g, v_w_in, v_ret_g, v_rg_conv_w, v_rg_conv_b, v_rg_wa, v_rg_ba, v_rg_wx, v_rg_bx,
                                 v_rg_lambda, v_w_out, v_norm2_g, v_norm_mem_g, v_xa_wq, v_xa_wk, v_xa_wv, v_xa_wo,
                                 v_norm3_g, v_ffn_w_up, v_ffn_conv_w, v_ffn_conv_b, v_ffn_w_down, v_final_g)))
    return _step(x, mem, positions, loss_target, W, Mo, Vo)
```

```python
import functools
import math

import jax
import jax.numpy as jnp
from jax import lax
from jax.experimental import pallas as pl
from jax.experimental.pallas import tpu as pltpu

F32 = jnp.float32
BF16 = jnp.bfloat16

EPS = 1e-6
RET_HEADS = 4
RET_CHUNK = 128
ROPE_BASE = 10000.0
LRU_BLOCKS = 8
LRU_C = 8.0
XA_HEADS = 4

ADAM_LR = 0.001
ADAM_B1 = 0.9
ADAM_B2 = 0.999
ADAM_EPS = 1e-08
ADAM_WD = 0.01
ADAM_STEP = 10

N_DEV = 8
N_CHIP = 4
MESH = pl.DeviceIdType.MESH
SUB = 8
LANE = 128
VMEM_LIMIT = 56 * 1024 * 1024

NN = ((1,), (0,))
NT = ((1,), (1,))
TN = ((0,), (0,))


def _cparams(sem):
    return pltpu.CompilerParams(dimension_semantics=sem, vmem_limit_bytes=VMEM_LIMIT)


def _sigmoid(v):
    return 1.0 / (1.0 + jnp.exp(-v))


def _bdot(a, b, dims):
    return lax.dot_general(a.astype(BF16), b.astype(BF16), (dims, ((), ())), preferred_element_type=F32)


def _row_iota(shape):
    return lax.broadcasted_iota(jnp.int32, shape, 0)


def _shift_down(v, tail, k):
    if k == 0:
        return v
    r = pltpu.roll(v, k, 0)
    rt = pltpu.roll(tail, k, 0)
    first = jnp.where(_row_iota(rt.shape) < k, rt, r[0:SUB])
    return jnp.concatenate([first, r[SUB:]], axis=0)


def _shift_up(v, head, k):
    if k == 0:
        return v
    n = v.shape[0]
    r = pltpu.roll(v, n - k, 0)
    rh = pltpu.roll(head, SUB - k, 0)
    last = jnp.where(_row_iota(rh.shape) >= SUB - k, rh, r[n - SUB:n])
    return jnp.concatenate([r[:n - SUB], last], axis=0)


def _mm(a, b, *, mode, M, N, K, out_dtype, name, tm=512, tn=512, tk=512, a_off=(0, 0), b_off=(0, 0),
        res=None, out=None, out_off=(0, 0), out_full=None, norm_g=None, after=None):
    tm, tn, tk = min(tm, M), min(tn, N), min(tk, K)
    assert M % tm == 0 and N % tn == 0 and K % tk == 0, (name, M, N, K, tm, tn, tk)
    nk = K // tk
    if mode == "nn":
        a_blk, b_blk, dims = (tm, tk), (tk, tn), NN
        a_map = lambda i, j, k: (i + a_off[0] // tm, k + a_off[1] // tk)
        b_map = lambda i, j, k: (k + b_off[0] // tk, j + b_off[1] // tn)
    elif mode == "nt":
        a_blk, b_blk, dims = (tm, tk), (tn, tk), NT
        a_map = lambda i, j, k: (i + a_off[0] // tm, k + a_off[1] // tk)
        b_map = lambda i, j, k: (j + b_off[0] // tn, k + b_off[1] // tk)
    else:
        a_blk, b_blk, dims = (tk, tm), (tk, tn), TN
        a_map = lambda i, j, k: (k + a_off[0] // tk, i + a_off[1] // tm)
        b_map = lambda i, j, k: (k + b_off[0] // tk, j + b_off[1] // tn)
    for off, blk in ((a_off, a_blk), (b_off, b_blk), (out_off, (tm, tn))):
        assert off[0] % blk[0] == 0 and off[1] % blk[1] == 0, (name, off, blk)
    o_map = lambda i, j, k: (i + out_off[0] // tm, j + out_off[1] // tn)
    has_res, has_out, has_norm, has_after = res is not None, out is not None, norm_g is not None, after is not None
    assert not has_norm or (tn == N and not has_out)

    def body(*refs):
        refs = list(refs)
        a_ref, b_ref = refs[0], refs[1]
        pos = 2
        r_ref = g_ref = n_ref = None
        if has_res:
            r_ref = refs[pos]
            pos += 1
        if has_norm:
            g_ref = refs[pos]
            pos += 1
        pos += has_out + has_after
        o_ref = refs[pos]
        pos += 1
        if has_norm:
            n_ref = refs[pos]
            pos += 1
        acc = refs[pos] if nk > 1 else None
        k = pl.program_id(2)
        part = _bdot(a_ref[...], b_ref[...], dims)

        def finish(total):
            if has_res:
                total = total + r_ref[...].astype(F32)
            o_ref[...] = total.astype(o_ref.dtype)
            if has_norm:
                r = lax.rsqrt(jnp.mean(total * total, axis=-1, keepdims=True) + EPS)
                n_ref[...] = (total * r * g_ref[...]).astype(n_ref.dtype)

        if nk == 1:
            finish(part)
        else:
            @pl.when(k == 0)
            def _():
                acc[...] = part

            @pl.when(k > 0)
            def _():
                acc[...] += part

            @pl.when(k == nk - 1)
            def _():
                finish(acc[...])

    in_specs = [pl.BlockSpec(a_blk, a_map), pl.BlockSpec(b_blk, b_map)]
    args = [a, b]
    if has_res:
        in_specs.append(pl.BlockSpec((tm, tn), lambda i, j, k: (i, j)))
        args.append(res)
    if has_norm:
        in_specs.append(pl.BlockSpec((1, N), lambda i, j, k: (0, 0)))
        args.append(norm_g)
    aliases = {}
    if has_out:
        in_specs.append(pl.BlockSpec(memory_space=pl.ANY))
        aliases = {len(args): 0}
        args.append(out)
        out_shape = jax.ShapeDtypeStruct(out.shape, out.dtype)
    else:
        out_shape = jax.ShapeDtypeStruct((M, N) if out_full is None else out_full, out_dtype)
    if has_after:
        in_specs.append(pl.BlockSpec(memory_space=pl.ANY))
        args.append(after)
    out_specs = pl.BlockSpec((tm, tn), o_map)
    if has_norm:
        out_shape = [out_shape, jax.ShapeDtypeStruct((M, N), BF16)]
        out_specs = [out_specs, pl.BlockSpec((tm, tn), lambda i, j, k: (i, j))]
    return pl.pallas_call(
        body, name=name, grid=(M // tm, N // tn, nk), in_specs=in_specs,
        out_specs=out_specs, out_shape=out_shape,
        scratch_shapes=[pltpu.VMEM((tm, tn), F32)] if nk > 1 else [],
        input_output_aliases=aliases,
        compiler_params=_cparams(("parallel", "parallel", "arbitrary")),
    )(*args)


def _rmsnorm_fwd(x, g, *, name, ts=512):
    S, D = x.shape
    ts = min(ts, S)

    def body(x_ref, g_ref, o_ref):
        xv = x_ref[...]
        r = lax.rsqrt(jnp.mean(xv * xv, axis=-1, keepdims=True) + EPS)
        o_ref[...] = (xv * r * g_ref[...]).astype(o_ref.dtype)

    return pl.pallas_call(
        body, name=name, grid=(S // ts,),
        in_specs=[pl.BlockSpec((ts, D), lambda i: (i, 0)), pl.BlockSpec((1, D), lambda i: (0, 0))],
        out_specs=pl.BlockSpec((ts, D), lambda i: (i, 0)),
        out_shape=jax.ShapeDtypeStruct((S, D), BF16),
        compiler_params=_cparams(("parallel",)),
    )(x, g)


def _rmsnorm_bwd(x, dxn, g, res, *, name, ts=256, emit_bf16=False):
    S, D = x.shape
    ts = min(ts, S)
    has_res = res is not None

    def body(*refs):
        refs = list(refs)
        dx16_ref = refs.pop() if emit_bf16 else None
        if has_res:
            x_ref, d_ref, g_ref, r_ref, dx_ref, dg_ref = refs
        else:
            x_ref, d_ref, g_ref, dx_ref, dg_ref = refs
        i = pl.program_id(0)
        xv = x_ref[...]
        dv = d_ref[...].astype(F32)
        r = lax.rsqrt(jnp.mean(xv * xv, axis=-1, keepdims=True) + EPS)
        gd = dv * g_ref[...]
        proj = jnp.mean(xv * gd, axis=-1, keepdims=True)
        dx = r * gd - xv * (r * r * r) * proj
        if has_res:
            dx = dx + r_ref[...]
        dx_ref[...] = dx
        if emit_bf16:
            dx16_ref[...] = dx.astype(BF16)
        part = jnp.sum(dv * xv * r, axis=0, keepdims=True)

        @pl.when(i == 0)
        def _():
            dg_ref[...] = part

        @pl.when(i > 0)
        def _():
            dg_ref[...] += part

    row = pl.BlockSpec((ts, D), lambda i: (i, 0))
    vec = pl.BlockSpec((1, D), lambda i: (0, 0))
    in_specs = [row, row, vec] + ([row] if has_res else [])
    args = [x, dxn, g] + ([res] if has_res else [])
    extra = emit_bf16 * [jax.ShapeDtypeStruct((S, D), BF16)]
    return pl.pallas_call(
        body, name=name, grid=(S // ts,), in_specs=in_specs, out_specs=[row, vec] + emit_bf16 * [row],
        out_shape=[jax.ShapeDtypeStruct((S, D), F32), jax.ShapeDtypeStruct((1, D), F32)] + extra,
        compiler_params=_cparams(("arbitrary",)),
    )(*args)


def _final_loss(x, target, g, *, name, ts=256):
    S, D = x.shape
    ts = min(ts, S)

    def body(x_ref, t_ref, g_ref, dx_ref, dg_ref, loss_ref, dx16_ref):
        i = pl.program_id(0)
        xv = x_ref[...]
        gv = g_ref[...]
        r = lax.rsqrt(jnp.mean(xv * xv, axis=-1, keepdims=True) + EPS)
        y = xv * r * gv
        err = y - t_ref[...]
        row_loss = jnp.mean(err * err, axis=-1, keepdims=True)
        lpart = 0.5 * jnp.sum(row_loss, axis=0, keepdims=True)
        dy = err * (1.0 / D)
        gd = dy * gv
        proj = jnp.mean(xv * gd, axis=-1, keepdims=True)
        dx = r * gd - xv * (r * r * r) * proj
        dx_ref[...] = dx
        dx16_ref[...] = dx.astype(BF16)
        part = jnp.sum(dy * xv * r, axis=0, keepdims=True)
        lfull = jnp.broadcast_to(lpart, loss_ref.shape)

        @pl.when(i == 0)
        def _():
            dg_ref[...] = part
            loss_ref[...] = lfull

        @pl.when(i > 0)
        def _():
            dg_ref[...] += part
            loss_ref[...] += lfull

    row = pl.BlockSpec((ts, D), lambda i: (i, 0))
    vec = pl.BlockSpec((1, D), lambda i: (0, 0))
    return pl.pallas_call(
        body, name=name, grid=(S // ts,), in_specs=[row, row, vec],
        out_specs=[row, vec, pl.BlockSpec((SUB, LANE), lambda i: (0, 0)), row],
        out_shape=[jax.ShapeDtypeStruct((S, D), F32), jax.ShapeDtypeStruct((1, D), F32),
                   jax.ShapeDtypeStruct((SUB, LANE), F32), jax.ShapeDtypeStruct((S, D), BF16)],
        compiler_params=_cparams(("arbitrary",)),
    )(x, target, g)


def _rope_table(pos_col, inv, *, name, ts=1024):
    S = pos_col.shape[0]
    ts = min(ts, S)
    half = inv.shape[1]

    def body(p_ref, inv_ref, c_ref, s_ref):
        ang = p_ref[...].astype(F32) * inv_ref[...]
        c_ref[...] = jnp.cos(ang)
        s_ref[...] = jnp.sin(ang)

    tab = pl.BlockSpec((ts, half), lambda i: (i, 0))
    return pl.pallas_call(
        body, name=name, grid=(S // ts,),
        in_specs=[pl.BlockSpec((ts, 1), lambda i: (i, 0)), pl.BlockSpec((1, half), lambda i: (0, 0))],
        out_specs=[tab, tab],
        out_shape=[jax.ShapeDtypeStruct((S, half), F32), jax.ShapeDtypeStruct((S, half), F32)],
        compiler_params=_cparams(("parallel",)),
    )(pos_col, inv)


def _ret_consts(C, log_g):
    ii = lax.broadcasted_iota(jnp.int32, (C, C), 0)
    jj = lax.broadcasted_iota(jnp.int32, (C, C), 1)
    diff = (ii - jj).astype(F32)
    intra = jnp.where(ii >= jj, jnp.exp(log_g * jnp.maximum(diff, 0.0)), 0.0)
    idx = lax.broadcasted_iota(jnp.int32, (C, 1), 0).astype(F32)
    qd = jnp.exp(log_g * (idx + 1.0))
    kd = jnp.exp(log_g * (C - 1.0 - idx))
    cd = math.exp(log_g * C)
    return intra, qd, kd, cd


def _rot(t, cs, sn):
    half = t.shape[-1] // 2
    t1, t2 = t[:, :half], t[:, half:]
    return jnp.concatenate([t1 * cs - t2 * sn, t1 * sn + t2 * cs], axis=-1)


def _unrot(d, cs, sn):
    half = d.shape[-1] // 2
    d1, d2 = d[:, :half], d[:, half:]
    return jnp.concatenate([d1 * cs + d2 * sn, d2 * cs - d1 * sn], axis=-1)


def _ret_fwd(h, cos, sin, ret_g, mix, *, name, ch=2):
    S = h.shape[0]
    R = ret_g.shape[1]
    H, C = RET_HEADS, RET_CHUNK
    Dh = R // H
    ts = ch * C
    assert S % ts == 0
    log_gs = [math.log(1.0 - 2.0 ** (-5.0 - hd)) for hd in range(H)]
    scale = Dh ** -0.5

    def body(x_ref, c_ref, s_ref, rg_ref, mix_in, ret_ref, st_ref, mix_ref, state):
        i = pl.program_id(0)

        @pl.when(i == 0)
        def _():
            state[...] = jnp.zeros_like(state)

        for c in range(ch):
            rows = pl.ds(c * C, C)
            cs, sn = c_ref[rows, :], s_ref[rows, :]
            for hd in range(H):
                intra, qd, kd, cd = _ret_consts(C, log_gs[hd])
                q = x_ref[rows, pl.ds(hd * Dh, Dh)]
                k = x_ref[rows, pl.ds(R + hd * Dh, Dh)]
                v = x_ref[rows, pl.ds(2 * R + hd * Dh, Dh)]
                g = x_ref[rows, pl.ds(3 * R + hd * Dh, Dh)]
                rq = _rot(q, cs, sn)
                rk = _rot(k, cs, sn) * scale
                st = state[hd]
                st_ref[c, hd] = st.astype(BF16)
                s_ = _bdot(rq, rk, NT) * intra
                ret = _bdot(s_, v, NN) + _bdot(rq * qd, st, NN)
                state[hd] = st * cd + _bdot(rk * kd, v, TN)
                ret_ref[rows, pl.ds(hd * Dh, Dh)] = ret
                rr = lax.rsqrt(jnp.mean(ret * ret, axis=-1, keepdims=True) + EPS)
                out = ret * rr * rg_ref[:, pl.ds(hd * Dh, Dh)] * (g * _sigmoid(g))
                mix_ref[rows, pl.ds(hd * Dh, Dh)] = out.astype(BF16)

    n_chunks = S // C
    return pl.pallas_call(
        body, name=name, grid=(S // ts,),
        in_specs=[pl.BlockSpec((ts, 4 * R), lambda i: (i, 0)),
                  pl.BlockSpec((ts, Dh // 2), lambda i: (i, 0)), pl.BlockSpec((ts, Dh // 2), lambda i: (i, 0)),
                  pl.BlockSpec((1, R), lambda i: (0, 0)), pl.BlockSpec(memory_space=pl.ANY)],
        out_specs=[pl.BlockSpec((ts, R), lambda i: (i, 0)),
                   pl.BlockSpec((ch, H, Dh, Dh), lambda i: (i, 0, 0, 0)),
                   pl.BlockSpec((ts, R), lambda i: (i, 0))],
        out_shape=[jax.ShapeDtypeStruct((S, R), F32), jax.ShapeDtypeStruct((n_chunks, H, Dh, Dh), BF16),
                   jax.ShapeDtypeStruct(mix.shape, mix.dtype)],
        scratch_shapes=[pltpu.VMEM((H, Dh, Dh), F32)],
        input_output_aliases={4: 2},
        compiler_params=_cparams(("arbitrary",)),
    )(h, cos, sin, ret_g, mix)


def _ret_bwd(h, cos, sin, ret_g, states, ret_raw, dmix, *, name, ch=2):
    S = h.shape[0]
    R = ret_g.shape[1]
    H, C = RET_HEADS, RET_CHUNK
    Dh = R // H
    ts = ch * C
    nb = S // ts
    log_gs = [math.log(1.0 - 2.0 ** (-5.0 - hd)) for hd in range(H)]
    scale = Dh ** -0.5

    def body(x_ref, c_ref, s_ref, rg_ref, st_ref, ret_ref, dm_ref, dh_ref, drg_ref, dstate):
        i = pl.program_id(0)

        @pl.when(i == 0)
        def _():
            dstate[...] = jnp.zeros_like(dstate)
            drg_ref[...] = jnp.zeros_like(drg_ref)

        for c in reversed(range(ch)):
            rows = pl.ds(c * C, C)
            cs, sn = c_ref[rows, :], s_ref[rows, :]
            for hd in range(H):
                intra, qd, kd, cd = _ret_consts(C, log_gs[hd])
                cols = pl.ds(hd * Dh, Dh)
                q = x_ref[rows, pl.ds(hd * Dh, Dh)]
                k = x_ref[rows, pl.ds(R + hd * Dh, Dh)]
                v = x_ref[rows, pl.ds(2 * R + hd * Dh, Dh)]
                g = x_ref[rows, pl.ds(3 * R + hd * Dh, Dh)]
                rq = _rot(q, cs, sn)
                rk = _rot(k, cs, sn) * scale
                ret = ret_ref[rows, cols]
                dm = dm_ref[rows, cols]
                rgv = rg_ref[:, cols]
                rr = lax.rsqrt(jnp.mean(ret * ret, axis=-1, keepdims=True) + EPS)
                retn = ret * rr
                sg = _sigmoid(g)
                silu = g * sg
                drg_ref[:, cols] += jnp.sum(dm * retn * silu, axis=0, keepdims=True)
                dg = dm * retn * rgv * (sg * (1.0 + g * (1.0 - sg)))
                dretn = dm * rgv * silu
                d_o = rr * dretn - ret * (rr * rr * rr) * jnp.mean(ret * dretn, axis=-1, keepdims=True)
                st = st_ref[c, hd]
                d_s = dstate[hd]
                a_ = _bdot(rq, rk, NT) * intra
                d_a = _bdot(d_o, v, NT) * intra
                d_qr = _bdot(d_a, rk, NN) + _bdot(d_o, st, NT) * qd
                d_kr = _bdot(d_a, rq, TN) + _bdot(v, d_s, NT) * kd
                d_v = _bdot(a_, d_o, TN) + _bdot(rk * kd, d_s, NN)
                dstate[hd] = d_s * cd + _bdot(rq * qd, d_o, TN)
                dh_ref[rows, pl.ds(hd * Dh, Dh)] = _unrot(d_qr, cs, sn).astype(BF16)
                dh_ref[rows, pl.ds(R + hd * Dh, Dh)] = (_unrot(d_kr, cs, sn) * scale).astype(BF16)
                dh_ref[rows, pl.ds(2 * R + hd * Dh, Dh)] = d_v.astype(BF16)
                dh_ref[rows, pl.ds(3 * R + hd * Dh, Dh)] = dg.astype(BF16)

    rb = lambda i: nb - 1 - i
    return pl.pallas_call(
        body, name=name, grid=(nb,),
        in_specs=[pl.BlockSpec((ts, 4 * R), lambda i: (rb(i), 0)),
                  pl.BlockSpec((ts, Dh // 2), lambda i: (rb(i), 0)), pl.BlockSpec((ts, Dh // 2), lambda i: (rb(i), 0)),
                  pl.BlockSpec((1, R), lambda i: (0, 0)),
                  pl.BlockSpec((ch, H, Dh, Dh), lambda i: (rb(i), 0, 0, 0)),
                  pl.BlockSpec((ts, R), lambda i: (rb(i), 0)),
                  pl.BlockSpec((ts, R), lambda i: (rb(i), 0))],
        out_specs=[pl.BlockSpec((ts, 4 * R), lambda i: (rb(i), 0)), pl.BlockSpec((1, R), lambda i: (0, 0))],
        out_shape=[jax.ShapeDtypeStruct((S, 6 * R), BF16), jax.ShapeDtypeStruct((1, R), F32)],
        scratch_shapes=[pltpu.VMEM((H, Dh, Dh), F32)],
        compiler_params=_cparams(("arbitrary",)),
    )(h, cos, sin, ret_g, states, ret_raw, dmix)


GELU_C = math.sqrt(2.0 / math.pi)
GELU_A = 0.044715


def _gelu_parts(y):
    t = jnp.tanh(GELU_C * (y + GELU_A * y * y * y))
    val = 0.5 * y * (1.0 + t)
    grad = 0.5 * (1.0 + t) + 0.5 * y * (1.0 - t * t) * GELU_C * (1.0 + 3.0 * GELU_A * y * y)
    return val, grad


def _neg_expm1(x):
    series = -x * (1.0 + x * (1.0 / 2.0) * (1.0 + x * (1.0 / 3.0) * (1.0 + x * (1.0 / 4.0) * (
        1.0 + x * (1.0 / 5.0) * (1.0 + x * (1.0 / 6.0) * (1.0 + x * (1.0 / 7.0)))))))
    return jnp.where(x > -0.35, series, 1.0 - jnp.exp(x))


def _log_sigmoid(x):
    return jnp.minimum(x, 0.0) - jnp.log1p(jnp.exp(-jnp.abs(x)))


def _lru_gates(uc, wa_ref, ba_ref, wx_ref, bx_ref):
    nbk = wa_ref.shape[0]
    bd = wa_ref.shape[1]
    rs, gs = [], []
    for n in range(nbk):
        ucn = uc[:, n * bd:(n + 1) * bd]
        rs.append(_sigmoid(_bdot(ucn, wa_ref[n], NN) + ba_ref[:, pl.ds(n * bd, bd)]))
        gs.append(_sigmoid(_bdot(ucn, wx_ref[n], NN) + bx_ref[:, pl.ds(n * bd, bd)]))
    return jnp.concatenate(rs, axis=-1), jnp.concatenate(gs, axis=-1)


def _lru_fwd(h, conv_w, conv_b, wa, ba, wx, bx, lam, *, name, ts=256):
    S = h.shape[0]
    W = lam.shape[1]
    K = conv_w.shape[0]
    ts = min(ts, S)

    def body(u_ref, y_ref, cw_ref, cb_ref, wa_ref, ba_ref, wx_ref, bx_ref, lam_ref, hl_ref, mix_ref, tail, hlast):
        i = pl.program_id(0)

        @pl.when(i == 0)
        def _():
            tail[...] = jnp.zeros_like(tail)
            hlast[...] = jnp.zeros_like(hlast)

        u = u_ref[...]
        tl = tail[...]
        uc = cb_ref[...] + cw_ref[K - 1:K, :] * u
        for k in range(K - 1):
            uc = uc + cw_ref[k:k + 1, :] * _shift_down(u, tl, K - 1 - k)
        tail[...] = u[ts - SUB:ts]
        r, ig = _lru_gates(uc, wa_ref, ba_ref, wx_ref, bx_ref)
        log_a = LRU_C * r * _log_sigmoid(lam_ref[...])
        a = jnp.exp(log_a)
        b = jnp.sqrt(_neg_expm1(2.0 * log_a)) * (ig * uc)
        rid = _row_iota((ts, W))
        d = 1
        while d < ts:
            a_s = jnp.where(rid < d, 1.0, pltpu.roll(a, d, 0))
            b_s = jnp.where(rid < d, 0.0, pltpu.roll(b, d, 0))
            b = a * b_s + b
            a = a * a_s
            d *= 2
        hcur = a * hlast[SUB - 1:SUB, :] + b
        hlast[...] = hcur[ts - SUB:ts]
        hl_ref[...] = hcur
        gy, _ = _gelu_parts(y_ref[...])
        mix_ref[...] = (hcur * gy).astype(BF16)

    full = lambda shape: pl.BlockSpec(shape, lambda i: tuple(0 for _ in shape))
    return pl.pallas_call(
        body, name=name, grid=(S // ts,),
        in_specs=[pl.BlockSpec((ts, W), lambda i: (i, 4)), pl.BlockSpec((ts, W), lambda i: (i, 5)),
                  full(conv_w.shape), full(conv_b.shape), full(wa.shape), full(ba.shape), full(wx.shape),
                  full(bx.shape), full(lam.shape)],
        out_specs=[pl.BlockSpec((ts, W), lambda i: (i, 0)), pl.BlockSpec((ts, W), lambda i: (i, 1))],
        out_shape=[jax.ShapeDtypeStruct((S, W), F32), jax.ShapeDtypeStruct((S, 2 * W), BF16)],
        scratch_shapes=[pltpu.VMEM((SUB, W), F32), pltpu.VMEM((SUB, W), F32)],
        compiler_params=_cparams(("arbitrary",)),
    )(h, h, conv_w, conv_b, wa, ba, wx, bx, lam)


def _lru_bwd(h, hl, dmix, dh, conv_w, conv_b, wa, ba, wx, bx, lam, *, name, ts=256):
    S = h.shape[0]
    W = lam.shape[1]
    K = conv_w.shape[0]
    nbk, bd = wa.shape[0], wa.shape[1]
    ts = min(ts, S)
    nb = S // ts
    t8 = ts // SUB

    def body(u_ref, y_ref, uh_ref, hl_ref, hh_ref, dm_ref, cw_ref, cb_ref, wa_ref, ba_ref, wx_ref, bx_ref, lam_ref,
             dh_in, dh_ref, dcw_ref, dcb_ref, dwa_ref, dba_ref, dwx_ref, dbx_ref, dlam_ref, carry, head):
        i = pl.program_id(0)
        blk = nb - 1 - i

        @pl.when(i == 0)
        def _():
            carry[...] = jnp.zeros_like(carry)
            head[...] = jnp.zeros_like(head)
            for ref in (dcw_ref, dcb_ref, dwa_ref, dba_ref, dwx_ref, dbx_ref, dlam_ref):
                ref[...] = jnp.zeros_like(ref)

        inside = (blk > 0).astype(F32)
        u = u_ref[...]
        tl = uh_ref[...] * inside
        sh = [_shift_down(u, tl, K - 1 - k) for k in range(K)]
        uc = cb_ref[...]
        for k in range(K):
            uc = uc + cw_ref[k:k + 1, :] * sh[k]
        r, ig = _lru_gates(uc, wa_ref, ba_ref, wx_ref, bx_ref)
        lam_v = lam_ref[...]
        ls = _log_sigmoid(lam_v)
        log_a = LRU_C * r * ls
        a = jnp.exp(log_a)
        mult = jnp.sqrt(_neg_expm1(2.0 * log_a))
        hcur = hl_ref[...]
        hprev = _shift_down(hcur, hh_ref[...] * inside, 1)
        gy, dgy = _gelu_parts(y_ref[...])
        dm = dm_ref[...]
        d_y = dm * hcur * dgy
        rid = _row_iota((ts, W))
        bq = dm * gy + jnp.where(rid == ts - 1, carry[0:1, :], 0.0)
        aq = jnp.where(rid == ts - 1, 0.0, pltpu.roll(a, ts - 1, 0))
        d = 1
        while d < ts:
            a_s = jnp.where(rid >= ts - d, 0.0, pltpu.roll(aq, ts - d, 0))
            b_s = jnp.where(rid >= ts - d, 0.0, pltpu.roll(bq, ts - d, 0))
            bq = bq + aq * b_s
            aq = aq * a_s
            d *= 2
        lam_t = bq
        carry[...] = (a * lam_t)[0:SUB]
        d_a = lam_t * hprev
        d_mult = lam_t * (ig * uc)
        d_i = lam_t * mult * uc
        d_uc = lam_t * mult * ig
        d_log_a = d_a * a - d_mult * (a * a) / mult
        d_r = d_log_a * (LRU_C * ls)
        dlam_ref[...] += jnp.sum(d_log_a * (LRU_C * r), axis=0, keepdims=True) * _sigmoid(-lam_v)
        d_pr = d_r * r * (1.0 - r)
        d_pi = d_i * ig * (1.0 - ig)
        dba_ref[...] += jnp.sum(d_pr, axis=0, keepdims=True)
        dbx_ref[...] += jnp.sum(d_pi, axis=0, keepdims=True)
        extra = []
        for n in range(nbk):
            sl = slice(n * bd, (n + 1) * bd)
            ucn = uc[:, sl]
            dwa_ref[n] += _bdot(ucn, d_pr[:, sl], TN)
            dwx_ref[n] += _bdot(ucn, d_pi[:, sl], TN)
            extra.append(_bdot(d_pr[:, sl], wa_ref[n], NT) + _bdot(d_pi[:, sl], wx_ref[n], NT))
        d_uc = d_uc + jnp.concatenate(extra, axis=-1)
        dcb_ref[...] += jnp.sum(d_uc, axis=0, keepdims=True)
        rid8 = _row_iota((SUB, W))
        dcw = jnp.zeros((SUB, W), F32)
        for k in range(K):
            dcw = dcw + jnp.where(rid8 == k, jnp.sum(d_uc * sh[k], axis=0, keepdims=True), 0.0)
        dcw_ref[...] += dcw
        hd = head[...]
        d_u = cw_ref[K - 1:K, :] * d_uc
        for j in range(1, K):
            d_u = d_u + cw_ref[K - 1 - j:K - j, :] * _shift_up(d_uc, hd, j)
        head[...] = d_uc[0:SUB]
        dh_ref[:, 0:W] = d_u.astype(BF16)
        dh_ref[:, W:2 * W] = d_y.astype(BF16)

    rb = lambda i: nb - 1 - i
    prev8 = lambda i: jnp.maximum(rb(i) * t8 - 1, 0)
    full = lambda shape: pl.BlockSpec(shape, lambda i: tuple(0 for _ in shape))
    small = [jax.ShapeDtypeStruct((SUB, W), F32), jax.ShapeDtypeStruct((1, W), F32),
             jax.ShapeDtypeStruct(wa.shape, F32), jax.ShapeDtypeStruct((1, W), F32),
             jax.ShapeDtypeStruct(wx.shape, F32), jax.ShapeDtypeStruct((1, W), F32),
             jax.ShapeDtypeStruct((1, W), F32)]
    return pl.pallas_call(
        body, name=name, grid=(nb,),
        in_specs=[pl.BlockSpec((ts, W), lambda i: (rb(i), 4)), pl.BlockSpec((ts, W), lambda i: (rb(i), 5)),
                  pl.BlockSpec((SUB, W), lambda i: (prev8(i), 4)),
                  pl.BlockSpec((ts, W), lambda i: (rb(i), 0)), pl.BlockSpec((SUB, W), lambda i: (prev8(i), 0)),
                  pl.BlockSpec((ts, W), lambda i: (rb(i), 1)),
                  full(conv_w.shape), full(conv_b.shape), full(wa.shape), full(ba.shape), full(wx.shape),
                  full(bx.shape), full(lam.shape), pl.BlockSpec(memory_space=pl.ANY)],
        out_specs=[pl.BlockSpec((ts, 2 * W), lambda i: (rb(i), 2))] + [full(s.shape) for s in small],
        out_shape=[jax.ShapeDtypeStruct(dh.shape, dh.dtype)] + small,
        scratch_shapes=[pltpu.VMEM((SUB, W), F32), pltpu.VMEM((SUB, W), F32)],
        input_output_aliases={13: 0},
        compiler_params=_cparams(("arbitrary",)),
    )(h, h, h, hl, hl, dmix, conv_w, conv_b, wa, ba, wx, bx, lam, dh)


def _xattn_fwd(q, km, vm, *, name, ts=512):
    S, D = q.shape
    M = km.shape[0]
    H = XA_HEADS
    Dh = D // H
    ts = min(ts, S)
    scale = Dh ** -0.5

    def body(q_ref, k_ref, v_ref, o_ref):
        for hd in range(H):
            cols = pl.ds(hd * Dh, Dh)
            s = _bdot(q_ref[:, cols], k_ref[:, cols], NT) * scale
            s = s - jnp.max(s, axis=-1, keepdims=True)
            e = jnp.exp(s)
            p = e / jnp.sum(e, axis=-1, keepdims=True)
            o_ref[:, cols] = _bdot(p, v_ref[:, cols], NN).astype(o_ref.dtype)

    return pl.pallas_call(
        body, name=name, grid=(S // ts,),
        in_specs=[pl.BlockSpec((ts, D), lambda i: (i, 0)), pl.BlockSpec((M, D), lambda i: (0, 0)),
                  pl.BlockSpec((M, D), lambda i: (0, 0))],
        out_specs=pl.BlockSpec((ts, D), lambda i: (i, 0)),
        out_shape=jax.ShapeDtypeStruct((S, D), BF16),
        compiler_params=_cparams(("parallel",)),
    )(q, km, vm)


def _xattn_bwd(q, km, vm, d_o, *, name, ts=512):
    S, D = q.shape
    M = km.shape[0]
    H = XA_HEADS
    Dh = D // H
    ts = min(ts, S)
    scale = Dh ** -0.5

    def body(q_ref, k_ref, v_ref, do_ref, dq_ref, dk_ref, dv_ref):
        i = pl.program_id(0)

        @pl.when(i == 0)
        def _():
            dk_ref[...] = jnp.zeros_like(dk_ref)
            dv_ref[...] = jnp.zeros_like(dv_ref)

        for hd in range(H):
            cols = pl.ds(hd * Dh, Dh)
            qh, kh, vh, doh = q_ref[:, cols], k_ref[:, cols], v_ref[:, cols], do_ref[:, cols]
            s = _bdot(qh, kh, NT) * scale
            s = s - jnp.max(s, axis=-1, keepdims=True)
            e = jnp.exp(s)
            p = e / jnp.sum(e, axis=-1, keepdims=True)
            dp = _bdot(doh, vh, NT)
            ds = p * (dp - jnp.sum(dp * p, axis=-1, keepdims=True)) * scale
            dq_ref[:, cols] = _bdot(ds, kh, NN).astype(dq_ref.dtype)
            dk_ref[:, cols] += _bdot(ds, qh, TN)
            dv_ref[:, cols] += _bdot(p, doh, TN)

    row = pl.BlockSpec((ts, D), lambda i: (i, 0))
    mem = pl.BlockSpec((M, D), lambda i: (0, 0))
    return pl.pallas_call(
        body, name=name, grid=(S // ts,), in_specs=[row, mem, mem, row], out_specs=[row, mem, mem],
        out_shape=[jax.ShapeDtypeStruct((S, D), BF16), jax.ShapeDtypeStruct((M, D), F32),
                   jax.ShapeDtypeStruct((M, D), F32)],
        compiler_params=_cparams(("arbitrary",)),
    )(q, km, vm, d_o)


def _conv_rows(v, tail, cw_ref, cb_ref):
    K = cw_ref.shape[0]
    sh = [_shift_down(v, tail, K - 1 - k) for k in range(K)]
    out = cb_ref[...]
    for k in range(K):
        out = out + cw_ref[k:k + 1, :] * sh[k]
    return out, sh


FFN_SUB = 256


def _ffn_up_gate(xn, w_up, cw, cb, *, name, tm=1024, tn=512):
    S, D = xn.shape
    F2 = w_up.shape[1]
    F = F2 // 2
    tm, tn = min(tm, S), min(tn, F)
    sub = min(FFN_SUB, tm)
    nj = F // tn
    K = cw.shape[0]

    def body(x_ref, wa_ref, wb_ref, cwa_ref, cwb_ref, cba_ref, cbb_ref, act_ref, ha_ref, hb_ref, ac_ref, bc_ref, ta, tb):
        i = pl.program_id(1)

        @pl.when(i == 0)
        def _():
            ta[...] = jnp.zeros_like(ta)
            tb[...] = jnp.zeros_like(tb)

        tail_a, tail_b = ta[...], tb[...]
        for s in range(tm // sub):
            rows = pl.ds(s * sub, sub)
            xs = x_ref[rows, :]
            ha = _bdot(xs, wa_ref[...], NN)
            hb = _bdot(xs, wb_ref[...], NN)
            ac, _ = _conv_rows(ha, tail_a, cwa_ref, cba_ref)
            bc, _ = _conv_rows(hb, tail_b, cwb_ref, cbb_ref)
            tail_a, tail_b = ha[sub - SUB:sub], hb[sub - SUB:sub]
            ha_ref[rows, :] = ha
            hb_ref[rows, :] = hb
            ac_ref[rows, :] = ac
            bc_ref[rows, :] = bc
            act_ref[rows, :] = (ac * _sigmoid(ac) * bc).astype(act_ref.dtype)
        ta[...] = tail_a
        tb[...] = tail_b

    blk = pl.BlockSpec((tm, tn), lambda j, i: (i, j))
    return pl.pallas_call(
        body, name=name, grid=(nj, S // tm),
        in_specs=[pl.BlockSpec((tm, D), lambda j, i: (i, 0)),
                  pl.BlockSpec((D, tn), lambda j, i: (0, j)), pl.BlockSpec((D, tn), lambda j, i: (0, j + nj)),
                  pl.BlockSpec((K, tn), lambda j, i: (0, j)), pl.BlockSpec((K, tn), lambda j, i: (0, j + nj)),
                  pl.BlockSpec((1, tn), lambda j, i: (0, j)), pl.BlockSpec((1, tn), lambda j, i: (0, j + nj))],
        out_specs=[blk] * 5,
        out_shape=[jax.ShapeDtypeStruct((S, F), BF16)] + [jax.ShapeDtypeStruct((S, F), F32)] * 4,
        scratch_shapes=[pltpu.VMEM((SUB, tn), F32), pltpu.VMEM((SUB, tn), F32)],
        compiler_params=_cparams(("parallel", "arbitrary")),
    )(xn, w_up, w_up, cw, cw, cb, cb)


def _ffn_bwd(dx, w_down, hh_a, hh_b, c_a, c_b, act, xn, cw, *, name, tm=1024, tn=256):
    S, D = dx.shape
    F = hh_a.shape[1]
    tm, tn = min(tm, S), min(tn, F)
    sub = min(FFN_SUB, tm)
    nj = F // tn
    nb = S // tm
    K = cw.shape[0]

    def body(dx_ref, wd_ref, a_ref, b_ref, ac_ref, bc_ref, act_ref, xn_ref, cwa_ref, cwb_ref,
             da_ref, db_ref, ga_ref, gb_ref, dwd_ref, dwa_ref, dwb_ref, ha, hb, acc_d, acc_a, acc_b):
        i = pl.program_id(1)

        @pl.when(i == 0)
        def _():
            for ref in (ha, hb, ga_ref, gb_ref, acc_d, acc_a, acc_b):
                ref[...] = jnp.zeros_like(ref)

        rid8 = _row_iota((SUB, tn))
        heads = [ha[...], hb[...]]
        gsums = [jnp.zeros((SUB, tn), F32), jnp.zeros((SUB, tn), F32)]
        for s in reversed(range(tm // sub)):
            rows = pl.ds(s * sub, sub)
            dv = _bdot(dx_ref[rows, :], wd_ref[...], NT)
            ac, bc = ac_ref[rows, :], bc_ref[rows, :]
            sg = _sigmoid(ac)
            d_bc = dv * ac * sg
            d_ac = dv * bc * sg * (1.0 + ac * (1.0 - sg))
            for which, (d_c, h_ref, cw_ref, o_ref) in enumerate(((d_ac, a_ref, cwa_ref, da_ref),
                                                                 (d_bc, b_ref, cwb_ref, db_ref))):
                ahead = [d_c] + [_shift_up(d_c, heads[which], j) for j in range(1, K)]
                heads[which] = d_c[0:SUB]
                d_in = cw_ref[K - 1:K, :] * d_c
                for j in range(1, K):
                    d_in = d_in + cw_ref[K - 1 - j:K - j, :] * ahead[j]
                o_ref[rows, :] = d_in.astype(o_ref.dtype)
                hv = h_ref[rows, :]
                gsum = gsums[which] + jnp.where(rid8 == K, jnp.sum(d_c, axis=0, keepdims=True), 0.0)
                for k in range(K):
                    gsum = gsum + jnp.where(rid8 == k, jnp.sum(ahead[K - 1 - k] * hv, axis=0, keepdims=True), 0.0)
                gsums[which] = gsum
        ha[...], hb[...] = heads
        ga_ref[...] += gsums[0]
        gb_ref[...] += gsums[1]
        acc_d[...] += _bdot(act_ref[...], dx_ref[...], TN)
        acc_a[...] += _bdot(xn_ref[...], da_ref[...], TN)
        acc_b[...] += _bdot(xn_ref[...], db_ref[...], TN)

        @pl.when(i == nb - 1)
        def _():
            dwd_ref[...] = acc_d[...].astype(dwd_ref.dtype)
            dwa_ref[...] = acc_a[...].astype(dwa_ref.dtype)
            dwb_ref[...] = acc_b[...].astype(dwb_ref.dtype)

    rb = lambda i: nb - 1 - i
    blk = pl.BlockSpec((tm, tn), lambda j, i: (rb(i), j))
    acc = pl.BlockSpec((SUB, tn), lambda j, i: (0, j))
    rows_d = pl.BlockSpec((tm, D), lambda j, i: (rb(i), 0))
    up_blk = pl.BlockSpec((D, tn), lambda j, i: (0, j))
    return pl.pallas_call(
        body, name=name, grid=(nj, nb),
        in_specs=[rows_d, pl.BlockSpec((tn, D), lambda j, i: (j, 0)), blk, blk, blk, blk, blk, rows_d,
                  pl.BlockSpec((K, tn), lambda j, i: (0, j)), pl.BlockSpec((K, tn), lambda j, i: (0, j + nj))],
        out_specs=[blk, blk, acc, acc, pl.BlockSpec((tn, D), lambda j, i: (j, 0)), up_blk, up_blk],
        out_shape=[jax.ShapeDtypeStruct((S, F), BF16), jax.ShapeDtypeStruct((S, F), BF16),
                   jax.ShapeDtypeStruct((SUB, F), F32), jax.ShapeDtypeStruct((SUB, F), F32),
                   jax.ShapeDtypeStruct((F, D), BF16), jax.ShapeDtypeStruct((D, F), BF16),
                   jax.ShapeDtypeStruct((D, F), BF16)],
        scratch_shapes=[pltpu.VMEM((SUB, tn), F32), pltpu.VMEM((SUB, tn), F32), pltpu.VMEM((tn, D), F32),
                        pltpu.VMEM((D, tn), F32), pltpu.VMEM((D, tn), F32)],
        compiler_params=_cparams(("parallel", "arbitrary")),
    )(dx, w_down, hh_a, hh_b, c_a, c_b, act, xn, cw, cw)


ADAM_BLOCK_ELEMS = 128 * 1024


def _adamw(w, m, v, parts, *, name):
    R, C = w.shape
    n = parts.shape[0]
    tr = R
    for cand in (1024, 512, 256, 128, 64, 32, 16):
        if R % cand == 0 and cand * C <= ADAM_BLOCK_ELEMS:
            tr = cand
            break
    c1 = 1.0 - ADAM_B1 ** ADAM_STEP
    c2 = 1.0 - ADAM_B2 ** ADAM_STEP

    def body(w_ref, m_ref, v_ref, p_ref, g_ref, d_ref, nm_ref, nv_ref):
        g = p_ref[0].astype(F32)
        for k in range(1, n):
            g = g + p_ref[k].astype(F32)
        m_new = ADAM_B1 * m_ref[...] + (1.0 - ADAM_B1) * g
        v_new = ADAM_B2 * v_ref[...] + (1.0 - ADAM_B2) * (g * g)
        m_hat = m_new / c1
        v_hat = v_new / c2
        g_ref[...] = g
        d_ref[...] = -ADAM_LR * (m_hat / (jnp.sqrt(v_hat) + ADAM_EPS) + ADAM_WD * w_ref[...])
        nm_ref[...] = m_new
        nv_ref[...] = v_new

    blk = pl.BlockSpec((tr, C), lambda i: (i, 0))
    sds = jax.ShapeDtypeStruct((R, C), F32)
    return pl.pallas_call(
        body, name=name, grid=(R // tr,),
        in_specs=[blk, blk, blk, pl.BlockSpec((n, tr, C), lambda i: (0, i, 0))],
        out_specs=[blk, blk, blk, blk], out_shape=[sds, sds, sds, sds],
        compiler_params=_cparams(("parallel",)),
    )(w, m, v, parts)


def _mesh_place():
    x, y, c = lax.axis_index("x"), lax.axis_index("y"), lax.axis_index("c")
    others = [(1 - x, y), (x, 1 - y), (1 - x, 1 - y)]
    return x, y, c, others


HBM_SPEC = pl.BlockSpec(memory_space=pltpu.HBM)
SEM_SPEC = pl.BlockSpec(memory_space=pltpu.SEMAPHORE)
ANY_SPEC = pl.BlockSpec(memory_space=pl.ANY)
EFFECT = pltpu.SideEffectType.DATAFLOW_SIDE_EFFECTING


def _in_hbm(a):
    return pltpu.with_memory_space_constraint(a, pltpu.HBM)


def _split_start(srcs, lands, copies, n_cp, *, name):
    n_s, n_l = len(srcs), len(lands)

    def body(*refs):
        src_refs, land_refs = refs[:n_s], refs[n_s:n_s + n_l]
        ssem, rsem = refs[n_s + n_l], refs[n_s + n_l + 1]
        token = refs[-1]
        for outgoing, _ in copies(src_refs, land_refs, ssem, rsem):
            outgoing.start()
        token[...] = jnp.zeros_like(token)

    outs = pl.pallas_call(
        body, name=name,
        out_shape=(pltpu.SemaphoreType.DMA((n_cp,)), pltpu.SemaphoreType.DMA((n_cp,)),
                   *[pltpu.HBM(a.shape, a.dtype) for a in srcs], *[pltpu.HBM(a.shape, a.dtype) for a in lands],
                   jax.ShapeDtypeStruct((SUB, LANE), F32)),
        in_specs=[HBM_SPEC] * (n_s + n_l),
        out_specs=(SEM_SPEC, SEM_SPEC, *[HBM_SPEC] * (n_s + n_l), pl.BlockSpec(memory_space=pltpu.VMEM)),
        input_output_aliases={i: 2 + i for i in range(n_s + n_l)},
        compiler_params=pltpu.CompilerParams(has_side_effects=EFFECT),
    )(*[_in_hbm(a) for a in srcs], *[_in_hbm(a) for a in lands])
    ssem, rsem = outs[0], outs[1]
    return ssem, rsem, list(outs[2:2 + n_s]), list(outs[2 + n_s:2 + n_s + n_l]), outs[-1]


def _split_wait(srcs, lands, ssem, rsem, after, copies, *, name):
    n_s, n_l = len(srcs), len(lands)

    def body(*refs):
        src_refs, land_refs = refs[:n_s], refs[n_s:n_s + n_l]
        s_ref, r_ref = refs[n_s + n_l], refs[n_s + n_l + 1]
        for outgoing, incoming in copies(src_refs, land_refs, s_ref, r_ref):
            outgoing.wait_send()
            incoming.wait_recv()

    outs = pl.pallas_call(
        body, name=name,
        out_shape=(*[pltpu.HBM(a.shape, a.dtype) for a in srcs], *[pltpu.HBM(a.shape, a.dtype) for a in lands]),
        in_specs=[HBM_SPEC] * (n_s + n_l) + [SEM_SPEC, SEM_SPEC, ANY_SPEC], out_specs=[HBM_SPEC] * (n_s + n_l),
        input_output_aliases={i: i for i in range(n_s + n_l)},
        compiler_params=pltpu.CompilerParams(has_side_effects=EFFECT),
    )(*srcs, *lands, ssem, rsem, after)
    return list(outs[:n_s]), list(outs[n_s:])


PLACE_BLOCK_ELEMS = 512 * 1024


def _place_rows(r, w):
    return _div_tile(r, max(16, PLACE_BLOCK_ELEMS // w), 16)


def _cast_place(shard, chip, axis, after, *, name):
    r, w = shard.shape
    tr = _place_rows(r, w)
    nb = r // tr
    full = (r * N_CHIP, w) if axis == 0 else (r, w * N_CHIP)
    has_after = after is not None

    def body(chip_ref, s_ref, *rest):
        rest[-1][...] = s_ref[...].astype(BF16)

    out_map = (lambda i, ch: (ch[0] * nb + i, 0)) if axis == 0 else (lambda i, ch: (i, ch[0]))
    grid_spec = pltpu.PrefetchScalarGridSpec(
        num_scalar_prefetch=1, grid=(nb,),
        in_specs=[pl.BlockSpec((tr, w), lambda i, ch: (i, 0))] + has_after * [ANY_SPEC],
        out_specs=pl.BlockSpec((tr, w), out_map))
    return pl.pallas_call(body, name=name, grid_spec=grid_spec, out_shape=jax.ShapeDtypeStruct(full, BF16),
                          compiler_params=_cparams(("parallel",)))(chip, shard, *(has_after * [after]))


def _slot_place(g, ids, axis, *, name):
    r, w = (g.shape[0] // N_CHIP, g.shape[1]) if axis == 0 else (g.shape[0], g.shape[1] // N_CHIP)
    tr = _place_rows(r, w)
    nb = r // tr

    def body(ids_ref, g_ref, o_ref):
        o_ref[...] = g_ref[...]

    in_map = (lambda i, ids_: (ids_[0] * nb + i, 0)) if axis == 0 else (lambda i, ids_: (i, ids_[0]))
    grid_spec = pltpu.PrefetchScalarGridSpec(
        num_scalar_prefetch=1, grid=(nb,), in_specs=[pl.BlockSpec((tr, w), in_map)],
        out_specs=pl.BlockSpec((None, tr, w), lambda i, ids_: (ids_[1], i, 0)))
    return pl.pallas_call(body, name=name, grid_spec=grid_spec, out_shape=jax.ShapeDtypeStruct((N_DEV, r, w), g.dtype),
                          compiler_params=_cparams(("parallel",)))(ids, g)


class _WeightGather:
    def __init__(self, placed, shard_shapes, axes, splits, tag):
        self.placed, self.shard_shapes, self.axes, self.splits, self.tag = list(placed), shard_shapes, axes, splits, tag
        self.n = len(placed)

    def _region(self, land_refs, it, chip, half):
        r, w = self.shard_shapes[it]
        by_rows = self.axes[it] == 0
        if self.splits[it] and half is not None:
            rows = pl.ds(pl.multiple_of(half * (r // 2) + (chip * r if by_rows else 0), 16), r // 2)
        else:
            rows = pl.ds(chip * r if by_rows else 0, r)
        cols = pl.ds(0, w) if by_rows else pl.ds(pl.multiple_of(chip * w, LANE), w)
        return land_refs[it].at[rows, cols]

    def _ici(self, src_refs, land_refs, ssem, rsem):
        x, y, c, others = _mesh_place()
        pairs = []
        for it in range(self.n):
            for j, chip in enumerate(others):
                def mk(chip_from, it=it, j=j, chip=chip):
                    return pltpu.make_async_remote_copy(
                        src_ref=self._region(land_refs, it, 2 * x + y, c), dst_ref=self._region(land_refs, it, chip_from, c),
                        send_sem=ssem.at[3 * it + j], recv_sem=rsem.at[3 * it + j], device_id=(*chip, c),
                        device_id_type=MESH)
                pairs.append((mk(2 * x + y), mk(2 * chip[0] + chip[1])))
        return pairs

    def start(self):
        self.ssem, self.rsem, _, self.lands, token = _split_start(
            [], self.placed, self._ici, 3 * self.n, name="gather_start_" + self.tag)
        return token

    def finish(self, after):
        _, lands = _split_wait([], self.lands, self.ssem, self.rsem, after, self._ici,
                               name="gather_wait_" + self.tag)
        n = self.n
        n_fwd = 3 * sum(self.splits)
        if n_fwd == 0:
            return lands

        def body(*refs):
            out_refs = refs[n:2 * n]
            fsend, frecv = refs[2 * n:]
            x, y, c, others = _mesh_place()
            sibling = (x, y, 1 - c)

            def fwd(it, slot, chip, half):
                reg = self._region(out_refs, it, 2 * chip[0] + chip[1], half)
                return pltpu.make_async_remote_copy(src_ref=reg, dst_ref=reg, send_sem=fsend.at[slot],
                                                    recv_sem=frecv.at[slot], device_id=sibling, device_id_type=MESH)

            sends, recvs = [], []
            for it in range(n):
                if self.splits[it]:
                    for chip in others:
                        sends.append(fwd(it, len(sends), chip, c))
                        recvs.append(fwd(it, len(recvs), chip, 1 - c))
            for cp in sends:
                cp.start()
            for cp in recvs:
                cp.wait_recv()
            for cp in sends:
                cp.wait_send()

        fulls = pl.pallas_call(
            body, name="gather_d2d_" + self.tag, in_specs=[ANY_SPEC] * n, out_specs=[ANY_SPEC] * n,
            out_shape=[jax.ShapeDtypeStruct(a.shape, a.dtype) for a in lands],
            scratch_shapes=[pltpu.SemaphoreType.DMA((n_fwd,)), pltpu.SemaphoreType.DMA((n_fwd,))],
            input_output_aliases={i: i for i in range(n)},
        )(*lands)
        return list(fulls)


class _GradGather:
    def __init__(self, grads, axes, tag):
        self.grads, self.axes, self.tag = list(grads), axes, tag
        self.n = len(grads)
        self.shard_shapes = [(g.shape[0] // N_CHIP, g.shape[1]) if ax == 0 else (g.shape[0], g.shape[1] // N_CHIP)
                             for g, ax in zip(grads, axes)]

    def _piece(self, src_refs, it, chip):
        r, w = self.shard_shapes[it]
        if self.axes[it] == 0:
            return src_refs[it].at[pl.ds(pl.multiple_of(chip * r, 16), r), :]
        return src_refs[it].at[:, pl.ds(pl.multiple_of(chip * w, LANE), w)]

    PER_ITEM = 4

    def _remote(self, src_refs, land_refs, ssem, rsem):
        x, y, c, others = _mesh_place()
        me = 4 * x + 2 * y + c
        pairs = []
        for it in range(self.n):
            def mk(k, piece_chip, slot, to, it=it):
                return pltpu.make_async_remote_copy(
                    src_ref=self._piece(src_refs, it, piece_chip), dst_ref=land_refs[it].at[slot],
                    send_sem=ssem.at[self.PER_ITEM * it + k], recv_sem=rsem.at[self.PER_ITEM * it + k], device_id=to,
                    device_id_type=MESH)
            for j, chip in enumerate(others):
                chip_id = 2 * chip[0] + chip[1]
                pairs.append((mk(j, chip_id, me, (*chip, c)), mk(j, chip_id, 2 * chip_id + c, (*chip, c))))
            sibling = (x, y, 1 - c)
            pairs.append((mk(3, 2 * x + y, me, sibling), mk(3, 2 * x + y, 4 * x + 2 * y + 1 - c, sibling)))
        return pairs

    def start(self):
        x, y, c = lax.axis_index("x"), lax.axis_index("y"), lax.axis_index("c")
        ids = jnp.stack([2 * x + y, 4 * x + 2 * y + c]).astype(jnp.int32)
        lands = [_slot_place(g, ids, ax, name="grads_own_%s%d" % (self.tag, it))
                 for it, (g, ax) in enumerate(zip(self.grads, self.axes))]
        self.ssem, self.rsem, self.srcs, self.lands, token = _split_start(
            self.grads, lands, self._remote, self.PER_ITEM * self.n, name="grads_start_" + self.tag)
        return token

    def _forward(self, src_refs, land_refs, ssem, rsem):
        x, y, c, others = _mesh_place()
        pairs = []
        for it in range(self.n):
            for j, ch in enumerate(others):
                def mk(slot, it=it, j=j):
                    return pltpu.make_async_remote_copy(
                        src_ref=land_refs[it].at[slot], dst_ref=land_refs[it].at[slot], send_sem=ssem.at[3 * it + j],
                        recv_sem=rsem.at[3 * it + j], device_id=(x, y, 1 - c), device_id_type=MESH)
                pairs.append((mk(4 * ch[0] + 2 * ch[1] + c), mk(4 * ch[0] + 2 * ch[1] + 1 - c)))
        return pairs

    def forward(self, after):
        _, lands = _split_wait(self.srcs, self.lands, self.ssem, self.rsem, after, self._remote,
                               name="grads_wait_" + self.tag)
        self.fsem, self.frsem, _, self.lands, token = _split_start(
            [], lands, self._forward, 3 * self.n, name="grads_fwd_" + self.tag)
        return token

    def finish(self, after):
        _, lands = _split_wait([], self.lands, self.fsem, self.frsem, after, self._forward,
                               name="grads_fwd_wait_" + self.tag)
        return lands


def _allreduce_small(vec, *, name):
    R, L = vec.shape

    def body(v_ref, o_ref, buf, send, recv, lsem):
        x, y, c, others = _mesh_place()
        me = 4 * x + 2 * y + c
        sibling = (x, y, 1 - c)

        def copy(k, slot, to, src=None):
            return pltpu.make_async_remote_copy(
                src_ref=buf.at[slot] if src is None else src, dst_ref=buf.at[slot], send_sem=send.at[k],
                recv_sem=recv.at[k], device_id=to, device_id_type=MESH)

        def slot_of(chip, core):
            return 4 * chip[0] + 2 * chip[1] + core

        mine = pltpu.make_async_copy(v_ref, buf.at[me], lsem)
        mine.start()
        first = [copy(0, me, sibling, src=v_ref)]
        first += [copy(1 + j, me, (*chip, c), src=v_ref) for j, chip in enumerate(others)]
        for cp in first:
            cp.start()
        passed = [copy(4 + j, slot_of(chip, c), sibling) for j, chip in enumerate(others)]
        for j, chip in enumerate(others):
            copy(1 + j, slot_of(chip, c), (*chip, c)).wait_recv()
            passed[j].start()
        copy(0, slot_of((x, y), 1 - c), sibling).wait_recv()
        for j, chip in enumerate(others):
            copy(4 + j, slot_of(chip, 1 - c), sibling).wait_recv()
        for cp in first + passed:
            cp.wait_send()
        mine.wait()
        total = buf[0]
        for k in range(1, N_DEV):
            total = total + buf[k]
        o_ref[...] = total

    return pl.pallas_call(
        body, name=name, in_specs=[pl.BlockSpec(memory_space=pltpu.VMEM)],
        out_specs=pl.BlockSpec(memory_space=pltpu.VMEM), out_shape=jax.ShapeDtypeStruct((R, L), F32),
        scratch_shapes=[pltpu.VMEM((N_DEV, R, L), F32), pltpu.SemaphoreType.DMA((7,)), pltpu.SemaphoreType.DMA((7,)),
                        pltpu.SemaphoreType.DMA],
        compiler_params=pltpu.CompilerParams(vmem_limit_bytes=VMEM_LIMIT),
    )(vec)


PACK_ALIGN = 1024


def _pack(arrs, row_multiple):
    flat = []
    for a in arrs:
        f = a.reshape(-1).astype(F32)
        flat.append(jnp.pad(f, (0, (-f.shape[0]) % PACK_ALIGN)))
    v = jnp.concatenate(flat)
    v = jnp.pad(v, (0, (-v.shape[0]) % (LANE * row_multiple)))
    return v.reshape(-1, LANE)


def _unpack(v, shapes):
    flat = v.reshape(-1)
    out, off = [], 0
    for s in shapes:
        size = math.prod(s)
        out.append(flat[off:off + size].reshape(s))
        off += size + (-size) % PACK_ALIGN
    return out


def _tile(dim, target):
    for cand in (1024, 512, 256, 128):
        if cand <= target and dim % cand == 0:
            return cand
    return dim


def _div_tile(dim, cap, mult=LANE):
    best = None
    for cand in range(mult, min(cap, dim) + 1, mult):
        if dim % cand == 0:
            best = cand
    return dim if best is None else best


WEIGHT_NAMES = ('norm1_g', 'w_in', 'ret_g', 'rg_conv_w', 'rg_conv_b', 'rg_wa', 'rg_ba', 'rg_wx', 'rg_bx', 'rg_lambda',
                'w_out', 'norm2_g', 'norm_mem_g', 'xa_wq', 'xa_wk', 'xa_wv', 'xa_wo', 'norm3_g', 'ffn_w_up',
                'ffn_conv_w', 'ffn_conv_b', 'ffn_w_down', 'final_g')
BIG_AXIS = {'w_in': 1, 'w_out': 0, 'xa_wq': 0, 'xa_wk': 0, 'xa_wv': 0, 'xa_wo': 0, 'ffn_w_up': 1, 'ffn_w_down': 0}
SMALL_SHARDED = ('rg_conv_w', 'ffn_conv_w')


def _step(x, mem, positions, loss_target, W, Mo, Vo):
    S, D = x.shape[1], x.shape[2]
    xs, mems, tgt = x[0], mem[0], loss_target[0]
    n_mem = mems.shape[0]
    pos_col = positions.reshape(S, 1)
    chip = 2 * lax.axis_index("x") + lax.axis_index("y")

    big = list(BIG_AXIS)
    shards = {n: W[n][0] for n in big}
    G = {}
    gather_groups = (('w_in', 'rg_conv_w'), ('w_out', 'xa_wq', 'xa_wk', 'xa_wv', 'xa_wo'),
                     ('ffn_w_up', 'ffn_conv_w'), ('ffn_w_down',))
    gathers, tok = [], None
    chip1 = jnp.reshape(chip, (1,)).astype(jnp.int32)
    for gi, names in enumerate(gather_groups):
        placed = []
        for n in names:
            if n in BIG_AXIS:
                placed.append(_cast_place(shards[n], chip1, BIG_AXIS[n], tok, name="place_" + n))
            else:
                s = W[n][0] if tok is None else W[n][0] + tok[0, 0]
                full = lax.empty((s.shape[0], s.shape[1] * N_CHIP), s.dtype)
                placed.append(lax.dynamic_update_slice(full, s, (0, chip * s.shape[1])))
        ag = _WeightGather(placed, [W[n][0].shape for n in names], [BIG_AXIS.get(n, 1) for n in names],
                           [n in BIG_AXIS for n in names], "g%d" % gi)
        tok = ag.start()
        gathers.append(ag)

    def finish_gather(gi, after):
        G.update(zip(gather_groups[gi], gathers[gi].finish(after)))

    R = W['ret_g'].shape[1]
    Wl = W['rg_lambda'].shape[1]
    IN = W['w_in'].shape[2] * N_CHIP
    F2 = W['ffn_w_up'].shape[2] * N_CHIP
    F = F2 // 2

    norm1_g, norm2_g, norm3_g = W['norm1_g'] + tok[0, 0], W['norm2_g'], W['norm3_g']
    norm_mem_g, final_g, ret_g = W['norm_mem_g'], W['final_g'].reshape(1, D), W['ret_g']
    rg_cb = W['rg_conv_b']
    wa, wx = W['rg_wa'][0], W['rg_wx'][0]
    ba, bx = W['rg_ba'].reshape(1, Wl), W['rg_bx'].reshape(1, Wl)
    lam = W['rg_lambda']
    ffn_cb = W['ffn_conv_b']

    def fwd_mm(a, wname, N, K, **kw):
        return _mm(a, G[wname], mode="nn", M=a.shape[0], N=N, K=K, tm=_tile(a.shape[0], 1024), tn=1024,
                   tk=_div_tile(K, 3072), **kw)

    def fwd_mm_norm(a, wname, res, g, name):
        return _mm(a, G[wname], mode="nn", M=a.shape[0], N=D, K=a.shape[1], tm=512, tn=D, tk=_div_tile(a.shape[1], 2048),
                   out_dtype=F32, res=res, norm_g=g, name=name)

    def bwd_x_mm(d, wname, N, K, **kw):
        return _mm(d, G[wname], mode="nt", M=d.shape[0], N=N, K=K, tm=_tile(d.shape[0], 1024),
                   tn=_div_tile(N, 1024, 256), tk=_div_tile(K, 3072), **kw)

    def bwd_w_mm(a, d, M, N, **kw):
        Ks = a.shape[0]
        return _mm(a, d, mode="tn", M=M, N=N, K=Ks, out_dtype=BF16, tm=_div_tile(M, 1024, 256),
                   tn=_div_tile(N, 1024, 256), tk=_div_tile(Ks, 2048 if d.dtype == BF16 else 1024), **kw)

    xn1 = _rmsnorm_fwd(xs, norm1_g, name="norm1_fwd")
    finish_gather(0, xn1)
    rg_cw = G['rg_conv_w']
    h = fwd_mm(xn1, 'w_in', IN, D, out_dtype=F32, name="mm_in")
    half = (R // RET_HEADS) // 2
    inv = (ROPE_BASE ** (-jnp.arange(half, dtype=F32) / half)).reshape(1, half)
    cos, sin = _rope_table(pos_col, inv, name="rope_table")
    hl, mix = _lru_fwd(h, rg_cw, rg_cb, wa, ba, wx, bx, lam, name="lru_fwd")
    ret_raw, states, mix = _ret_fwd(h, cos, sin, ret_g, mix, name="ret_fwd")
    finish_gather(1, mix)
    x1, xn2 = fwd_mm_norm(mix, 'w_out', xs, norm2_g, "mm_out")
    memn = _rmsnorm_fwd(mems, norm_mem_g, name="norm_mem_fwd")
    km = fwd_mm(memn, 'xa_wk', D, D, out_dtype=BF16, name="mm_k")
    vm = fwd_mm(memn, 'xa_wv', D, D, out_dtype=BF16, name="mm_v")
    q = fwd_mm(xn2, 'xa_wq', D, D, out_dtype=BF16, name="mm_q")
    o = _xattn_fwd(q, km, vm, name="xattn_fwd")
    x2, xn3 = fwd_mm_norm(o, 'xa_wo', x1, norm3_g, "mm_o")
    finish_gather(2, xn3)
    ffn_cw = G['ffn_conv_w']
    act, hh_a, hh_b, hc_a, hc_b = _ffn_up_gate(xn3, G['ffn_w_up'], ffn_cw, ffn_cb, name="ffn_up_gate")
    finish_gather(3, act)
    x3 = fwd_mm(act, 'ffn_w_down', D, F, out_dtype=F32, res=x2, name="mm_down")
    dx3, d_final, loss8, dx3h = _final_loss(x3, tgt, final_g, name="final_loss")

    gw = {}
    grad_groups = []

    def start_grads(names, tag):
        gg = _GradGather([gw[n] for n in names], [BIG_AXIS[n] for n in names], tag)
        grad_groups.append((names, gg))
        return gg.start()

    dhh_a, dhh_b, gcw_a, gcw_b, gw['ffn_w_down'], gw_up_a, gw_up_b = _ffn_bwd(
        dx3h, G['ffn_w_down'], hh_a, hh_b, hc_a, hc_b, act, xn3, ffn_cw, name="ffn_bwd")
    gw['ffn_w_up'] = jnp.concatenate([gw_up_a, gw_up_b], axis=1)
    tok_a = start_grads(('ffn_w_down', 'ffn_w_up'), "a")
    dxn3 = bwd_x_mm(dhh_a, 'ffn_w_up', D, F, out_dtype=F32, after=tok_a, name="mm_dxn3_a")
    dxn3 = bwd_x_mm(dhh_b, 'ffn_w_up', D, F, out_dtype=F32, b_off=(0, F), res=dxn3, name="mm_dxn3_b")
    dx2, d_norm3, dx2h = _rmsnorm_bwd(x2, dxn3, norm3_g, dx3, name="norm3_bwd", emit_bf16=True)
    Kc = ffn_cw.shape[0]
    d_ffn_cw = jnp.concatenate([gcw_a[:Kc], gcw_b[:Kc]], axis=1)
    d_ffn_cb = jnp.concatenate([gcw_a[Kc:Kc + 1], gcw_b[Kc:Kc + 1]], axis=1)

    d_o = bwd_x_mm(dx2h, 'xa_wo', D, D, out_dtype=BF16, name="mm_do")
    gw['xa_wo'] = bwd_w_mm(o, dx2h, D, D, name="mm_dw_o")
    dq, dk, dv = _xattn_bwd(q, km, vm, d_o, name="xattn_bwd")
    gw['xa_wq'] = bwd_w_mm(xn2, dq, D, D, name="mm_dw_q")
    dxn2 = bwd_x_mm(dq, 'xa_wq', D, D, out_dtype=F32, name="mm_dxn2")
    gw['xa_wk'] = bwd_w_mm(memn, dk, D, D, name="mm_dw_k")
    gw['xa_wv'] = bwd_w_mm(memn, dv, D, D, name="mm_dw_v")
    dmemn = bwd_x_mm(dk, 'xa_wk', D, D, out_dtype=F32, name="mm_dmem_k")
    dmemn = bwd_x_mm(dv, 'xa_wv', D, D, out_dtype=F32, res=dmemn, name="mm_dmem_v")
    _, d_norm_mem = _rmsnorm_bwd(mems, dmemn, norm_mem_g, None, name="norm_mem_bwd")
    dx1, d_norm2, dx1h = _rmsnorm_bwd(x1, dxn2, norm2_g, dx2, name="norm2_bwd", emit_bf16=True)

    gw['w_out'] = bwd_w_mm(mix, dx1h, D, D, name="mm_dw_out")
    tok_b = start_grads(('xa_wo', 'xa_wq', 'xa_wk', 'xa_wv', 'w_out'), "b")
    dmix = bwd_x_mm(dx1h, 'w_out', D, D, out_dtype=F32, after=tok_b, name="mm_dmix")
    dh, d_ret_g = _ret_bwd(h, cos, sin, ret_g, states, ret_raw, dmix, name="ret_bwd")
    dh, d_rcw, d_rcb, d_wa, d_ba, d_wx, d_bx, d_lam = _lru_bwd(
        h, hl, dmix, dh, rg_cw, rg_cb, wa, ba, wx, bx, lam, name="lru_bwd")
    gw['w_in'] = bwd_w_mm(xn1, dh, D, IN, name="mm_dw_in")
    tok_c = start_grads(('w_in',), "c")
    dxn1 = bwd_x_mm(dh, 'w_in', D, IN, out_dtype=F32, after=tok_c, name="mm_dxn1")
    grad_x, d_norm1 = _rmsnorm_bwd(xs, dxn1, norm1_g, dx1, name="norm1_bwd")

    small_parts = {
        'norm1_g': d_norm1, 'ret_g': d_ret_g, 'rg_conv_w': d_rcw[:rg_cw.shape[0]], 'rg_conv_b': d_rcb,
        'rg_wa': d_wa, 'rg_ba': d_ba, 'rg_wx': d_wx, 'rg_bx': d_bx, 'rg_lambda': d_lam, 'norm2_g': d_norm2,
        'norm_mem_g': d_norm_mem, 'norm3_g': d_norm3, 'ffn_conv_w': d_ffn_cw, 'ffn_conv_b': d_ffn_cb,
        'final_g': d_final}
    small = [n for n in WEIGHT_NAMES if n not in BIG_AXIS]
    red_shapes = [(1,)] + [tuple(small_parts[n].shape) for n in small]
    fwd_tok = sum(gg.forward(d_norm1)[0:1, 0:1] for _, gg in grad_groups)
    reduced = _allreduce_small(_pack([loss8[0:1, 0:1] + fwd_tok] + [small_parts[n] for n in small], SUB),
                               name="allreduce_small")
    red = _unpack(reduced, red_shapes)
    loss = red[0][0]
    g_small = dict(zip(small, red[1:]))
    for n in SMALL_SHARDED:
        w_local = W[n].shape[-1]
        g_small[n] = lax.dynamic_slice_in_dim(g_small[n], chip * w_local, w_local, axis=1)

    out_g, out_d, out_m, out_v = {}, {}, {}, {}
    rows = 512
    pk = lambda d: _pack([d[n] for n in small], rows)
    g_pack = _pack([g_small[n] for n in small], rows)
    res_small = _adamw(pk(W), pk(Mo), pk(Vo), g_pack[None], name="adamw_small")
    shapes_small = [tuple(W[n].shape) for n in small]
    for dst, packed in zip((out_g, out_d, out_m, out_v), res_small):
        for n, val in zip(small, _unpack(packed, shapes_small)):
            dst[n] = val
    last = res_small[0]
    for names, gg in grad_groups:
        for n, land in zip(names, gg.finish(last)):
            g, d, m_new, v_new = _adamw(shards[n], Mo[n][0], Vo[n][0], land, name="adamw_" + n)
            out_g[n], out_d[n], out_m[n], out_v[n] = (t.reshape(W[n].shape) for t in (g, d, m_new, v_new))
            last = g
    return (loss, grad_x[None], *[out_g[n] for n in WEIGHT_NAMES], *[out_d[n] for n in WEIGHT_NAMES],
            *[out_m[n] for n in WEIGHT_NAMES], *[out_v[n] for n in WEIGHT_NAMES])


def kernel(x, mem, positions, norm1_g, w_in, ret_g, rg_conv_w, rg_conv_b, rg_wa, rg_ba, rg_wx, rg_bx, rg_lambda, w_out, norm2_g, norm_mem_g, xa_wq, xa_wk, xa_wv, xa_wo, norm3_g, ffn_w_up, ffn_conv_w, ffn_conv_b, ffn_w_down, final_g, loss_target, m_norm1_g, m_w_in, m_ret_g, m_rg_conv_w, m_rg_conv_b, m_rg_wa, m_rg_ba, m_rg_wx, m_rg_bx, m_rg_lambda, m_w_out, m_norm2_g, m_norm_mem_g, m_xa_wq, m_xa_wk, m_xa_wv, m_xa_wo, m_norm3_g, m_ffn_w_up, m_ffn_conv_w, m_ffn_conv_b, m_ffn_w_down, m_final_g, v_norm1_g, v_w_in, v_ret_g, v_rg_conv_w, v_rg_conv_b, v_rg_wa, v_rg_ba, v_rg_wx, v_rg_bx, v_rg_lambda, v_w_out, v_norm2_g, v_norm_mem_g, v_xa_wq, v_xa_wk, v_xa_wv, v_xa_wo, v_norm3_g, v_ffn_w_up, v_ffn_conv_w, v_ffn_conv_b, v_ffn_w_down, v_final_g):
    W = dict(zip(WEIGHT_NAMES, (norm1_g, w_in, ret_g, rg_conv_w, rg_conv_b, rg_wa, rg_ba, rg_wx, rg_bx, rg_lambda, w_out,
                                norm2_g, norm_mem_g, xa_wq, xa_wk, xa_wv, xa_wo, norm3_g, ffn_w_up, ffn_conv_w,
                                ffn_conv_b, ffn_w_down, final_g)))
    Mo = dict(zip(WEIGHT_NAMES, (m_norm1_g, m_w_in, m_ret_g, m_rg_conv_w, m_rg_conv_b, m_rg_wa, m_rg_ba, m_rg_wx, m_rg_bx,
                                 m_rg_lambda, m_w_out, m_norm2_g, m_norm_mem_g, m_xa_wq, m_xa_wk, m_xa_wv, m_xa_wo,
                                 m_norm3_g, m_ffn_w_up, m_ffn_conv_w, m_ffn_conv_b, m_ffn_w_down, m_final_g)))
    Vo = dict(zip(WEIGHT_NAMES, (v_norm1_g, v_w_in, v_ret_g, v_rg_conv_w, v_rg_conv_b, v_rg_wa, v_rg_ba, v_rg_wx, v_rg_bx,
                                 v_rg_lambda, v_w_out, v_norm2_g, v_norm_mem_g, v_xa_wq, v_xa_wk, v_xa_wv, v_xa_wo,
                                 v_norm3_g, v_ffn_w_up, v_ffn_conv_w, v_ffn_conv_b, v_ffn_w_down, v_final_g)))
    return _step(x, mem, positions, loss_target, W, Mo, Vo)
```

```python
import functools
import math

import jax
import jax.numpy as jnp
from jax import lax
from jax.experimental import pallas as pl
from jax.experimental.pallas import tpu as pltpu

F32 = jnp.float32
BF16 = jnp.bfloat16

EPS = 1e-6
RET_HEADS = 4
RET_CHUNK = 128
ROPE_BASE = 10000.0
LRU_BLOCKS = 8
LRU_C = 8.0
XA_HEADS = 4

ADAM_LR = 0.001
ADAM_B1 = 0.9
ADAM_B2 = 0.999
ADAM_EPS = 1e-08
ADAM_WD = 0.01
ADAM_STEP = 10

N_DEV = 8
N_CHIP = 4
MESH = pl.DeviceIdType.MESH
SUB = 8
LANE = 128
VMEM_LIMIT = 56 * 1024 * 1024

NN = ((1,), (0,))
NT = ((1,), (1,))
TN = ((0,), (0,))


def _cparams(sem):
    return pltpu.CompilerParams(dimension_semantics=sem, vmem_limit_bytes=VMEM_LIMIT)


def _sigmoid(v):
    return 1.0 / (1.0 + jnp.exp(-v))


def _bdot(a, b, dims):
    return lax.dot_general(a.astype(BF16), b.astype(BF16), (dims, ((), ())), preferred_element_type=F32)


def _row_iota(shape):
    return lax.broadcasted_iota(jnp.int32, shape, 0)


def _shift_down(v, tail, k):
    if k == 0:
        return v
    r = pltpu.roll(v, k, 0)
    rt = pltpu.roll(tail, k, 0)
    first = jnp.where(_row_iota(rt.shape) < k, rt, r[0:SUB])
    return jnp.concatenate([first, r[SUB:]], axis=0)


def _shift_up(v, head, k):
    if k == 0:
        return v
    n = v.shape[0]
    r = pltpu.roll(v, n - k, 0)
    rh = pltpu.roll(head, SUB - k, 0)
    last = jnp.where(_row_iota(rh.shape) >= SUB - k, rh, r[n - SUB:n])
    return jnp.concatenate([r[:n - SUB], last], axis=0)


def _mm(a, b, *, mode, M, N, K, out_dtype, name, tm=512, tn=512, tk=512, a_off=(0, 0), b_off=(0, 0),
        res=None, out=None, out_off=(0, 0), out_full=None, norm_g=None, after=None):
    tm, tn, tk = min(tm, M), min(tn, N), min(tk, K)
    assert M % tm == 0 and N % tn == 0 and K % tk == 0, (name, M, N, K, tm, tn, tk)
    nk = K // tk
    if mode == "nn":
        a_blk, b_blk, dims = (tm, tk), (tk, tn), NN
        a_map = lambda i, j, k: (i + a_off[0] // tm, k + a_off[1] // tk)
        b_map = lambda i, j, k: (k + b_off[0] // tk, j + b_off[1] // tn)
    elif mode == "nt":
        a_blk, b_blk, dims = (tm, tk), (tn, tk), NT
        a_map = lambda i, j, k: (i + a_off[0] // tm, k + a_off[1] // tk)
        b_map = lambda i, j, k: (j + b_off[0] // tn, k + b_off[1] // tk)
    else:
        a_blk, b_blk, dims = (tk, tm), (tk, tn), TN
        a_map = lambda i, j, k: (k + a_off[0] // tk, i + a_off[1] // tm)
        b_map = lambda i, j, k: (k + b_off[0] // tk, j + b_off[1] // tn)
    for off, blk in ((a_off, a_blk), (b_off, b_blk), (out_off, (tm, tn))):
        assert off[0] % blk[0] == 0 and off[1] % blk[1] == 0, (name, off, blk)
    o_map = lambda i, j, k: (i + out_off[0] // tm, j + out_off[1] // tn)
    has_res, has_out, has_norm, has_after = res is not None, out is not None, norm_g is not None, after is not None
    assert not has_norm or (tn == N and not has_out)

    def body(*refs):
        refs = list(refs)
        a_ref, b_ref = refs[0], refs[1]
        pos = 2
        r_ref = g_ref = n_ref = None
        if has_res:
            r_ref = refs[pos]
            pos += 1
        if has_norm:
            g_ref = refs[pos]
            pos += 1
        pos += has_out + has_after
        o_ref = refs[pos]
        pos += 1
        if has_norm:
            n_ref = refs[pos]
            pos += 1
        acc = refs[pos] if nk > 1 else None
        k = pl.program_id(2)
        part = _bdot(a_ref[...], b_ref[...], dims)

        def finish(total):
            if has_res:
                total = total + r_ref[...].astype(F32)
            o_ref[...] = total.astype(o_ref.dtype)
            if has_norm:
                r = lax.rsqrt(jnp.mean(total * total, axis=-1, keepdims=True) + EPS)
                n_ref[...] = (total * r * g_ref[...]).astype(n_ref.dtype)

        if nk == 1:
            finish(part)
        else:
            @pl.when(k == 0)
            def _():
                acc[...] = part

            @pl.when(k > 0)
            def _():
                acc[...] += part

            @pl.when(k == nk - 1)
            def _():
                finish(acc[...])

    in_specs = [pl.BlockSpec(a_blk, a_map), pl.BlockSpec(b_blk, b_map)]
    args = [a, b]
    if has_res:
        in_specs.append(pl.BlockSpec((tm, tn), lambda i, j, k: (i, j)))
        args.append(res)
    if has_norm:
        in_specs.append(pl.BlockSpec((1, N), lambda i, j, k: (0, 0)))
        args.append(norm_g)
    aliases = {}
    if has_out:
        in_specs.append(pl.BlockSpec(memory_space=pl.ANY))
        aliases = {len(args): 0}
        args.append(out)
        out_shape = jax.ShapeDtypeStruct(out.shape, out.dtype)
    else:
        out_shape = jax.ShapeDtypeStruct((M, N) if out_full is None else out_full, out_dtype)
    if has_after:
        in_specs.append(pl.BlockSpec(memory_space=pl.ANY))
        args.append(after)
    out_specs = pl.BlockSpec((tm, tn), o_map)
    if has_norm:
        out_shape = [out_shape, jax.ShapeDtypeStruct((M, N), BF16)]
        out_specs = [out_specs, pl.BlockSpec((tm, tn), lambda i, j, k: (i, j))]
    return pl.pallas_call(
        body, name=name, grid=(M // tm, N // tn, nk), in_specs=in_specs,
        out_specs=out_specs, out_shape=out_shape,
        scratch_shapes=[pltpu.VMEM((tm, tn), F32)] if nk > 1 else [],
        input_output_aliases=aliases,
        compiler_params=_cparams(("parallel", "parallel", "arbitrary")),
    )(*args)


def _rmsnorm_fwd(x, g, *, name, ts=512):
    S, D = x.shape
    ts = min(ts, S)

    def body(x_ref, g_ref, o_ref):
        xv = x_ref[...]
        r = lax.rsqrt(jnp.mean(xv * xv, axis=-1, keepdims=True) + EPS)
        o_ref[...] = (xv * r * g_ref[...]).astype(o_ref.dtype)

    return pl.pallas_call(
        body, name=name, grid=(S // ts,),
        in_specs=[pl.BlockSpec((ts, D), lambda i: (i, 0)), pl.BlockSpec((1, D), lambda i: (0, 0))],
        out_specs=pl.BlockSpec((ts, D), lambda i: (i, 0)),
        out_shape=jax.ShapeDtypeStruct((S, D), BF16),
        compiler_params=_cparams(("parallel",)),
    )(x, g)


def _rmsnorm_bwd(x, dxn, g, res, *, name, ts=512, emit_bf16=False):
    S, D = x.shape
    ts = min(ts, S)
    has_res = res is not None

    def body(*refs):
        refs = list(refs)
        dx16_ref = refs.pop() if emit_bf16 else None
        if has_res:
            x_ref, d_ref, g_ref, r_ref, dx_ref, dg_ref = refs
        else:
            x_ref, d_ref, g_ref, dx_ref, dg_ref = refs
        i = pl.program_id(0)
        xv = x_ref[...]
        dv = d_ref[...].astype(F32)
        r = lax.rsqrt(jnp.mean(xv * xv, axis=-1, keepdims=True) + EPS)
        gd = dv * g_ref[...]
        proj = jnp.mean(xv * gd, axis=-1, keepdims=True)
        dx = r * gd - xv * (r * r * r) * proj
        if has_res:
            dx = dx + r_ref[...]
        dx_ref[...] = dx
        if emit_bf16:
            dx16_ref[...] = dx.astype(BF16)
        part = jnp.sum(dv * xv * r, axis=0, keepdims=True)

        @pl.when(i == 0)
        def _():
            dg_ref[...] = part

        @pl.when(i > 0)
        def _():
            dg_ref[...] += part

    row = pl.BlockSpec((ts, D), lambda i: (i, 0))
    vec = pl.BlockSpec((1, D), lambda i: (0, 0))
    in_specs = [row, row, vec] + ([row] if has_res else [])
    args = [x, dxn, g] + ([res] if has_res else [])
    extra = emit_bf16 * [jax.ShapeDtypeStruct((S, D), BF16)]
    return pl.pallas_call(
        body, name=name, grid=(S // ts,), in_specs=in_specs, out_specs=[row, vec] + emit_bf16 * [row],
        out_shape=[jax.ShapeDtypeStruct((S, D), F32), jax.ShapeDtypeStruct((1, D), F32)] + extra,
        compiler_params=_cparams(("arbitrary",)),
    )(*args)


def _final_loss(x, target, g, *, name, ts=512):
    S, D = x.shape
    ts = min(ts, S)

    def body(x_ref, t_ref, g_ref, dx_ref, dg_ref, loss_ref, dx16_ref):
        i = pl.program_id(0)
        xv = x_ref[...]
        gv = g_ref[...]
        r = lax.rsqrt(jnp.mean(xv * xv, axis=-1, keepdims=True) + EPS)
        y = xv * r * gv
        err = y - t_ref[...]
        row_loss = jnp.mean(err * err, axis=-1, keepdims=True)
        lpart = 0.5 * jnp.sum(row_loss, axis=0, keepdims=True)
        dy = err * (1.0 / D)
        gd = dy * gv
        proj = jnp.mean(xv * gd, axis=-1, keepdims=True)
        dx = r * gd - xv * (r * r * r) * proj
        dx_ref[...] = dx
        dx16_ref[...] = dx.astype(BF16)
        part = jnp.sum(dy * xv * r, axis=0, keepdims=True)
        lfull = jnp.broadcast_to(lpart, loss_ref.shape)

        @pl.when(i == 0)
        def _():
            dg_ref[...] = part
            loss_ref[...] = lfull

        @pl.when(i > 0)
        def _():
            dg_ref[...] += part
            loss_ref[...] += lfull

    row = pl.BlockSpec((ts, D), lambda i: (i, 0))
    vec = pl.BlockSpec((1, D), lambda i: (0, 0))
    return pl.pallas_call(
        body, name=name, grid=(S // ts,), in_specs=[row, row, vec],
        out_specs=[row, vec, pl.BlockSpec((SUB, LANE), lambda i: (0, 0)), row],
        out_shape=[jax.ShapeDtypeStruct((S, D), F32), jax.ShapeDtypeStruct((1, D), F32),
                   jax.ShapeDtypeStruct((SUB, LANE), F32), jax.ShapeDtypeStruct((S, D), BF16)],
        compiler_params=_cparams(("arbitrary",)),
    )(x, target, g)


def _rope_table(pos_col, inv, *, name, ts=1024):
    S = pos_col.shape[0]
    ts = min(ts, S)
    half = inv.shape[1]

    def body(p_ref, inv_ref, c_ref, s_ref):
        ang = p_ref[...].astype(F32) * inv_ref[...]
        c_ref[...] = jnp.cos(ang)
        s_ref[...] = jnp.sin(ang)

    tab = pl.BlockSpec((ts, half), lambda i: (i, 0))
    return pl.pallas_call(
        body, name=name, grid=(S // ts,),
        in_specs=[pl.BlockSpec((ts, 1), lambda i: (i, 0)), pl.BlockSpec((1, half), lambda i: (0, 0))],
        out_specs=[tab, tab],
        out_shape=[jax.ShapeDtypeStruct((S, half), F32), jax.ShapeDtypeStruct((S, half), F32)],
        compiler_params=_cparams(("parallel",)),
    )(pos_col, inv)


def _ret_consts(C, log_g):
    ii = lax.broadcasted_iota(jnp.int32, (C, C), 0)
    jj = lax.broadcasted_iota(jnp.int32, (C, C), 1)
    diff = (ii - jj).astype(F32)
    intra = jnp.where(ii >= jj, jnp.exp(log_g * jnp.maximum(diff, 0.0)), 0.0)
    idx = lax.broadcasted_iota(jnp.int32, (C, 1), 0).astype(F32)
    qd = jnp.exp(log_g * (idx + 1.0))
    kd = jnp.exp(log_g * (C - 1.0 - idx))
    cd = math.exp(log_g * C)
    return intra, qd, kd, cd


def _rot(t, cs, sn):
    half = t.shape[-1] // 2
    t1, t2 = t[:, :half], t[:, half:]
    return jnp.concatenate([t1 * cs - t2 * sn, t1 * sn + t2 * cs], axis=-1)


def _unrot(d, cs, sn):
    half = d.shape[-1] // 2
    d1, d2 = d[:, :half], d[:, half:]
    return jnp.concatenate([d1 * cs + d2 * sn, d2 * cs - d1 * sn], axis=-1)


def _ret_fwd(h, cos, sin, ret_g, mix, *, name, ch=2):
    S = h.shape[0]
    R = ret_g.shape[1]
    H, C = RET_HEADS, RET_CHUNK
    Dh = R // H
    ts = ch * C
    assert S % ts == 0
    log_gs = [math.log(1.0 - 2.0 ** (-5.0 - hd)) for hd in range(H)]
    scale = Dh ** -0.5

    def body(x_ref, c_ref, s_ref, rg_ref, mix_in, ret_ref, st_ref, mix_ref, state):
        i = pl.program_id(0)

        @pl.when(i == 0)
        def _():
            state[...] = jnp.zeros_like(state)

        for c in range(ch):
            rows = pl.ds(c * C, C)
            cs, sn = c_ref[rows, :], s_ref[rows, :]
            for hd in range(H):
                intra, qd, kd, cd = _ret_consts(C, log_gs[hd])
                q = x_ref[rows, pl.ds(hd * Dh, Dh)]
                k = x_ref[rows, pl.ds(R + hd * Dh, Dh)]
                v = x_ref[rows, pl.ds(2 * R + hd * Dh, Dh)]
                g = x_ref[rows, pl.ds(3 * R + hd * Dh, Dh)]
                rq = _rot(q, cs, sn)
                rk = _rot(k, cs, sn) * scale
                st = state[hd]
                st_ref[c, hd] = st.astype(BF16)
                s_ = _bdot(rq, rk, NT) * intra
                ret = _bdot(s_, v, NN) + _bdot(rq * qd, st, NN)
                state[hd] = st * cd + _bdot(rk * kd, v, TN)
                ret_ref[rows, pl.ds(hd * Dh, Dh)] = ret
                rr = lax.rsqrt(jnp.mean(ret * ret, axis=-1, keepdims=True) + EPS)
                out = ret * rr * rg_ref[:, pl.ds(hd * Dh, Dh)] * (g * _sigmoid(g))
                mix_ref[rows, pl.ds(hd * Dh, Dh)] = out.astype(BF16)

    n_chunks = S // C
    return pl.pallas_call(
        body, name=name, grid=(S // ts,),
        in_specs=[pl.BlockSpec((ts, 4 * R), lambda i: (i, 0)),
                  pl.BlockSpec((ts, Dh // 2), lambda i: (i, 0)), pl.BlockSpec((ts, Dh // 2), lambda i: (i, 0)),
                  pl.BlockSpec((1, R), lambda i: (0, 0)), pl.BlockSpec(memory_space=pl.ANY)],
        out_specs=[pl.BlockSpec((ts, R), lambda i: (i, 0)),
                   pl.BlockSpec((ch, H, Dh, Dh), lambda i: (i, 0, 0, 0)),
                   pl.BlockSpec((ts, R), lambda i: (i, 0))],
        out_shape=[jax.ShapeDtypeStruct((S, R), F32), jax.ShapeDtypeStruct((n_chunks, H, Dh, Dh), BF16),
                   jax.ShapeDtypeStruct(mix.shape, mix.dtype)],
        scratch_shapes=[pltpu.VMEM((H, Dh, Dh), F32)],
        input_output_aliases={4: 2},
        compiler_params=_cparams(("arbitrary",)),
    )(h, cos, sin, ret_g, mix)


def _ret_bwd(h, cos, sin, ret_g, states, ret_raw, dmix, *, name, ch=2):
    S = h.shape[0]
    R = ret_g.shape[1]
    H, C = RET_HEADS, RET_CHUNK
    Dh = R // H
    ts = ch * C
    nb = S // ts
    log_gs = [math.log(1.0 - 2.0 ** (-5.0 - hd)) for hd in range(H)]
    scale = Dh ** -0.5

    def body(x_ref, c_ref, s_ref, rg_ref, st_ref, ret_ref, dm_ref, dh_ref, drg_ref, dstate):
        i = pl.program_id(0)

        @pl.when(i == 0)
        def _():
            dstate[...] = jnp.zeros_like(dstate)
            drg_ref[...] = jnp.zeros_like(drg_ref)

        for c in reversed(range(ch)):
            rows = pl.ds(c * C, C)
            cs, sn = c_ref[rows, :], s_ref[rows, :]
            for hd in range(H):
                intra, qd, kd, cd = _ret_consts(C, log_gs[hd])
                cols = pl.ds(hd * Dh, Dh)
                q = x_ref[rows, pl.ds(hd * Dh, Dh)]
                k = x_ref[rows, pl.ds(R + hd * Dh, Dh)]
                v = x_ref[rows, pl.ds(2 * R + hd * Dh, Dh)]
                g = x_ref[rows, pl.ds(3 * R + hd * Dh, Dh)]
                rq = _rot(q, cs, sn)
                rk = _rot(k, cs, sn) * scale
                ret = ret_ref[rows, cols]
                dm = dm_ref[rows, cols]
                rgv = rg_ref[:, cols]
                rr = lax.rsqrt(jnp.mean(ret * ret, axis=-1, keepdims=True) + EPS)
                retn = ret * rr
                sg = _sigmoid(g)
                silu = g * sg
                drg_ref[:, cols] += jnp.sum(dm * retn * silu, axis=0, keepdims=True)
                dg = dm * retn * rgv * (sg * (1.0 + g * (1.0 - sg)))
                dretn = dm * rgv * silu
                d_o = rr * dretn - ret * (rr * rr * rr) * jnp.mean(ret * dretn, axis=-1, keepdims=True)
                st = st_ref[c, hd]
                d_s = dstate[hd]
                a_ = _bdot(rq, rk, NT) * intra
                d_a = _bdot(d_o, v, NT) * intra
                d_qr = _bdot(d_a, rk, NN) + _bdot(d_o, st, NT) * qd
                d_kr = _bdot(d_a, rq, TN) + _bdot(v, d_s, NT) * kd
                d_v = _bdot(a_, d_o, TN) + _bdot(rk * kd, d_s, NN)
                dstate[hd] = d_s * cd + _bdot(rq * qd, d_o, TN)
                dh_ref[rows, pl.ds(hd * Dh, Dh)] = _unrot(d_qr, cs, sn).astype(BF16)
                dh_ref[rows, pl.ds(R + hd * Dh, Dh)] = (_unrot(d_kr, cs, sn) * scale).astype(BF16)
                dh_ref[rows, pl.ds(2 * R + hd * Dh, Dh)] = d_v.astype(BF16)
                dh_ref[rows, pl.ds(3 * R + hd * Dh, Dh)] = dg.astype(BF16)

    rb = lambda i: nb - 1 - i
    return pl.pallas_call(
        body, name=name, grid=(nb,),
        in_specs=[pl.BlockSpec((ts, 4 * R), lambda i: (rb(i), 0)),
                  pl.BlockSpec((ts, Dh // 2), lambda i: (rb(i), 0)), pl.BlockSpec((ts, Dh // 2), lambda i: (rb(i), 0)),
                  pl.BlockSpec((1, R), lambda i: (0, 0)),
                  pl.BlockSpec((ch, H, Dh, Dh), lambda i: (rb(i), 0, 0, 0)),
                  pl.BlockSpec((ts, R), lambda i: (rb(i), 0)),
                  pl.BlockSpec((ts, R), lambda i: (rb(i), 0))],
        out_specs=[pl.BlockSpec((ts, 4 * R), lambda i: (rb(i), 0)), pl.BlockSpec((1, R), lambda i: (0, 0))],
        out_shape=[jax.ShapeDtypeStruct((S, 6 * R), BF16), jax.ShapeDtypeStruct((1, R), F32)],
        scratch_shapes=[pltpu.VMEM((H, Dh, Dh), F32)],
        compiler_params=_cparams(("arbitrary",)),
    )(h, cos, sin, ret_g, states, ret_raw, dmix)


GELU_C = math.sqrt(2.0 / math.pi)
GELU_A = 0.044715


def _gelu_parts(y):
    t = jnp.tanh(GELU_C * (y + GELU_A * y * y * y))
    val = 0.5 * y * (1.0 + t)
    grad = 0.5 * (1.0 + t) + 0.5 * y * (1.0 - t * t) * GELU_C * (1.0 + 3.0 * GELU_A * y * y)
    return val, grad


def _neg_expm1(x):
    series = -x * (1.0 + x * (1.0 / 2.0) * (1.0 + x * (1.0 / 3.0) * (1.0 + x * (1.0 / 4.0) * (
        1.0 + x * (1.0 / 5.0) * (1.0 + x * (1.0 / 6.0) * (1.0 + x * (1.0 / 7.0)))))))
    return jnp.where(x > -0.35, series, 1.0 - jnp.exp(x))


def _log_sigmoid(x):
    return jnp.minimum(x, 0.0) - jnp.log1p(jnp.exp(-jnp.abs(x)))


def _lru_gates(uc, wa_ref, ba_ref, wx_ref, bx_ref):
    nbk = wa_ref.shape[0]
    bd = wa_ref.shape[1]
    rs, gs = [], []
    for n in range(nbk):
        ucn = uc[:, n * bd:(n + 1) * bd]
        rs.append(_sigmoid(_bdot(ucn, wa_ref[n], NN) + ba_ref[:, pl.ds(n * bd, bd)]))
        gs.append(_sigmoid(_bdot(ucn, wx_ref[n], NN) + bx_ref[:, pl.ds(n * bd, bd)]))
    return jnp.concatenate(rs, axis=-1), jnp.concatenate(gs, axis=-1)


def _lru_fwd(h, conv_w, conv_b, wa, ba, wx, bx, lam, *, name, ts=256):
    S = h.shape[0]
    W = lam.shape[1]
    K = conv_w.shape[0]
    ts = min(ts, S)

    def body(u_ref, y_ref, cw_ref, cb_ref, wa_ref, ba_ref, wx_ref, bx_ref, lam_ref, hl_ref, mix_ref, tail, hlast):
        i = pl.program_id(0)

        @pl.when(i == 0)
        def _():
            tail[...] = jnp.zeros_like(tail)
            hlast[...] = jnp.zeros_like(hlast)

        u = u_ref[...]
        tl = tail[...]
        uc = cb_ref[...] + cw_ref[K - 1:K, :] * u
        for k in range(K - 1):
            uc = uc + cw_ref[k:k + 1, :] * _shift_down(u, tl, K - 1 - k)
        tail[...] = u[ts - SUB:ts]
        r, ig = _lru_gates(uc, wa_ref, ba_ref, wx_ref, bx_ref)
        log_a = LRU_C * r * _log_sigmoid(lam_ref[...])
        a = jnp.exp(log_a)
        b = jnp.sqrt(_neg_expm1(2.0 * log_a)) * (ig * uc)
        rid = _row_iota((ts, W))
        d = 1
        while d < ts:
            a_s = jnp.where(rid < d, 1.0, pltpu.roll(a, d, 0))
            b_s = jnp.where(rid < d, 0.0, pltpu.roll(b, d, 0))
            b = a * b_s + b
            a = a * a_s
            d *= 2
        hcur = a * hlast[SUB - 1:SUB, :] + b
        hlast[...] = hcur[ts - SUB:ts]
        hl_ref[...] = hcur
        gy, _ = _gelu_parts(y_ref[...])
        mix_ref[...] = (hcur * gy).astype(BF16)

    full = lambda shape: pl.BlockSpec(shape, lambda i: tuple(0 for _ in shape))
    return pl.pallas_call(
        body, name=name, grid=(S // ts,),
        in_specs=[pl.BlockSpec((ts, W), lambda i: (i, 4)), pl.BlockSpec((ts, W), lambda i: (i, 5)),
                  full(conv_w.shape), full(conv_b.shape), full(wa.shape), full(ba.shape), full(wx.shape),
                  full(bx.shape), full(lam.shape)],
        out_specs=[pl.BlockSpec((ts, W), lambda i: (i, 0)), pl.BlockSpec((ts, W), lambda i: (i, 1))],
        out_shape=[jax.ShapeDtypeStruct((S, W), F32), jax.ShapeDtypeStruct((S, 2 * W), BF16)],
        scratch_shapes=[pltpu.VMEM((SUB, W), F32), pltpu.VMEM((SUB, W), F32)],
        compiler_params=_cparams(("arbitrary",)),
    )(h, h, conv_w, conv_b, wa, ba, wx, bx, lam)


def _lru_bwd(h, hl, dmix, dh, conv_w, conv_b, wa, ba, wx, bx, lam, *, name, ts=256):
    S = h.shape[0]
    W = lam.shape[1]
    K = conv_w.shape[0]
    nbk, bd = wa.shape[0], wa.shape[1]
    ts = min(ts, S)
    nb = S // ts
    t8 = ts // SUB

    def body(u_ref, y_ref, uh_ref, hl_ref, hh_ref, dm_ref, cw_ref, cb_ref, wa_ref, ba_ref, wx_ref, bx_ref, lam_ref,
             dh_in, dh_ref, dcw_ref, dcb_ref, dwa_ref, dba_ref, dwx_ref, dbx_ref, dlam_ref, carry, head):
        i = pl.program_id(0)
        blk = nb - 1 - i

        @pl.when(i == 0)
        def _():
            carry[...] = jnp.zeros_like(carry)
            head[...] = jnp.zeros_like(head)
            for ref in (dcw_ref, dcb_ref, dwa_ref, dba_ref, dwx_ref, dbx_ref, dlam_ref):
                ref[...] = jnp.zeros_like(ref)

        inside = (blk > 0).astype(F32)
        u = u_ref[...]
        tl = uh_ref[...] * inside
        sh = [_shift_down(u, tl, K - 1 - k) for k in range(K)]
        uc = cb_ref[...]
        for k in range(K):
            uc = uc + cw_ref[k:k + 1, :] * sh[k]
        r, ig = _lru_gates(uc, wa_ref, ba_ref, wx_ref, bx_ref)
        lam_v = lam_ref[...]
        ls = _log_sigmoid(lam_v)
        log_a = LRU_C * r * ls
        a = jnp.exp(log_a)
        mult = jnp.sqrt(_neg_expm1(2.0 * log_a))
        hcur = hl_ref[...]
        hprev = _shift_down(hcur, hh_ref[...] * inside, 1)
        gy, dgy = _gelu_parts(y_ref[...])
        dm = dm_ref[...]
        d_y = dm * hcur * dgy
        rid = _row_iota((ts, W))
        bq = dm * gy + jnp.where(rid == ts - 1, carry[0:1, :], 0.0)
        aq = jnp.where(rid == ts - 1, 0.0, pltpu.roll(a, ts - 1, 0))
        d = 1
        while d < ts:
            a_s = jnp.where(rid >= ts - d, 0.0, pltpu.roll(aq, ts - d, 0))
            b_s = jnp.where(rid >= ts - d, 0.0, pltpu.roll(bq, ts - d, 0))
            bq = bq + aq * b_s
            aq = aq * a_s
            d *= 2
        lam_t = bq
        carry[...] = (a * lam_t)[0:SUB]
        d_a = lam_t * hprev
        d_mult = lam_t * (ig * uc)
        d_i = lam_t * mult * uc
        d_uc = lam_t * mult * ig
        d_log_a = d_a * a - d_mult * (a * a) / mult
        d_r = d_log_a * (LRU_C * ls)
        dlam_ref[...] += jnp.sum(d_log_a * (LRU_C * r), axis=0, keepdims=True) * _sigmoid(-lam_v)
        d_pr = d_r * r * (1.0 - r)
        d_pi = d_i * ig * (1.0 - ig)
        dba_ref[...] += jnp.sum(d_pr, axis=0, keepdims=True)
        dbx_ref[...] += jnp.sum(d_pi, axis=0, keepdims=True)
        extra = []
        for n in range(nbk):
            sl = slice(n * bd, (n + 1) * bd)
            ucn = uc[:, sl]
            dwa_ref[n] += _bdot(ucn, d_pr[:, sl], TN)
            dwx_ref[n] += _bdot(ucn, d_pi[:, sl], TN)
            extra.append(_bdot(d_pr[:, sl], wa_ref[n], NT) + _bdot(d_pi[:, sl], wx_ref[n], NT))
        d_uc = d_uc + jnp.concatenate(extra, axis=-1)
        dcb_ref[...] += jnp.sum(d_uc, axis=0, keepdims=True)
        rid8 = _row_iota((SUB, W))
        dcw = jnp.zeros((SUB, W), F32)
        for k in range(K):
            dcw = dcw + jnp.where(rid8 == k, jnp.sum(d_uc * sh[k], axis=0, keepdims=True), 0.0)
        dcw_ref[...] += dcw
        hd = head[...]
        d_u = cw_ref[K - 1:K, :] * d_uc
        for j in range(1, K):
            d_u = d_u + cw_ref[K - 1 - j:K - j, :] * _shift_up(d_uc, hd, j)
        head[...] = d_uc[0:SUB]
        dh_ref[:, 0:W] = d_u.astype(BF16)
        dh_ref[:, W:2 * W] = d_y.astype(BF16)

    rb = lambda i: nb - 1 - i
    prev8 = lambda i: jnp.maximum(rb(i) * t8 - 1, 0)
    full = lambda shape: pl.BlockSpec(shape, lambda i: tuple(0 for _ in shape))
    small = [jax.ShapeDtypeStruct((SUB, W), F32), jax.ShapeDtypeStruct((1, W), F32),
             jax.ShapeDtypeStruct(wa.shape, F32), jax.ShapeDtypeStruct((1, W), F32),
             jax.ShapeDtypeStruct(wx.shape, F32), jax.ShapeDtypeStruct((1, W), F32),
             jax.ShapeDtypeStruct((1, W), F32)]
    return pl.pallas_call(
        body, name=name, grid=(nb,),
        in_specs=[pl.BlockSpec((ts, W), lambda i: (rb(i), 4)), pl.BlockSpec((ts, W), lambda i: (rb(i), 5)),
                  pl.BlockSpec((SUB, W), lambda i: (prev8(i), 4)),
                  pl.BlockSpec((ts, W), lambda i: (rb(i), 0)), pl.BlockSpec((SUB, W), lambda i: (prev8(i), 0)),
                  pl.BlockSpec((ts, W), lambda i: (rb(i), 1)),
                  full(conv_w.shape), full(conv_b.shape), full(wa.shape), full(ba.shape), full(wx.shape),
                  full(bx.shape), full(lam.shape), pl.BlockSpec(memory_space=pl.ANY)],
        out_specs=[pl.BlockSpec((ts, 2 * W), lambda i: (rb(i), 2))] + [full(s.shape) for s in small],
        out_shape=[jax.ShapeDtypeStruct(dh.shape, dh.dtype)] + small,
        scratch_shapes=[pltpu.VMEM((SUB, W), F32), pltpu.VMEM((SUB, W), F32)],
        input_output_aliases={13: 0},
        compiler_params=_cparams(("arbitrary",)),
    )(h, h, h, hl, hl, dmix, conv_w, conv_b, wa, ba, wx, bx, lam, dh)


def _xattn_fwd(q, km, vm, *, name, ts=512):
    S, D = q.shape
    M = km.shape[0]
    H = XA_HEADS
    Dh = D // H
    ts = min(ts, S)
    scale = Dh ** -0.5

    def body(q_ref, k_ref, v_ref, o_ref):
        for hd in range(H):
            cols = pl.ds(hd * Dh, Dh)
            s = _bdot(q_ref[:, cols], k_ref[:, cols], NT) * scale
            s = s - jnp.max(s, axis=-1, keepdims=True)
            e = jnp.exp(s)
            p = e / jnp.sum(e, axis=-1, keepdims=True)
            o_ref[:, cols] = _bdot(p, v_ref[:, cols], NN).astype(o_ref.dtype)

    return pl.pallas_call(
        body, name=name, grid=(S // ts,),
        in_specs=[pl.BlockSpec((ts, D), lambda i: (i, 0)), pl.BlockSpec((M, D), lambda i: (0, 0)),
                  pl.BlockSpec((M, D), lambda i: (0, 0))],
        out_specs=pl.BlockSpec((ts, D), lambda i: (i, 0)),
        out_shape=jax.ShapeDtypeStruct((S, D), BF16),
        compiler_params=_cparams(("parallel",)),
    )(q, km, vm)


def _xattn_bwd(q, km, vm, d_o, *, name, ts=512):
    S, D = q.shape
    M = km.shape[0]
    H = XA_HEADS
    Dh = D // H
    ts = min(ts, S)
    scale = Dh ** -0.5

    def body(q_ref, k_ref, v_ref, do_ref, dq_ref, dk_ref, dv_ref):
        i = pl.program_id(0)

        @pl.when(i == 0)
        def _():
            dk_ref[...] = jnp.zeros_like(dk_ref)
            dv_ref[...] = jnp.zeros_like(dv_ref)

        for hd in range(H):
            cols = pl.ds(hd * Dh, Dh)
            qh, kh, vh, doh = q_ref[:, cols], k_ref[:, cols], v_ref[:, cols], do_ref[:, cols]
            s = _bdot(qh, kh, NT) * scale
            s = s - jnp.max(s, axis=-1, keepdims=True)
            e = jnp.exp(s)
            p = e / jnp.sum(e, axis=-1, keepdims=True)
            dp = _bdot(doh, vh, NT)
            ds = p * (dp - jnp.sum(dp * p, axis=-1, keepdims=True)) * scale
            dq_ref[:, cols] = _bdot(ds, kh, NN).astype(dq_ref.dtype)
            dk_ref[:, cols] += _bdot(ds, qh, TN)
            dv_ref[:, cols] += _bdot(p, doh, TN)

    row = pl.BlockSpec((ts, D), lambda i: (i, 0))
    mem = pl.BlockSpec((M, D), lambda i: (0, 0))
    return pl.pallas_call(
        body, name=name, grid=(S // ts,), in_specs=[row, mem, mem, row], out_specs=[row, mem, mem],
        out_shape=[jax.ShapeDtypeStruct((S, D), BF16), jax.ShapeDtypeStruct((M, D), F32),
                   jax.ShapeDtypeStruct((M, D), F32)],
        compiler_params=_cparams(("arbitrary",)),
    )(q, km, vm, d_o)


def _conv_rows(v, tail, cw_ref, cb_ref):
    K = cw_ref.shape[0]
    sh = [_shift_down(v, tail, K - 1 - k) for k in range(K)]
    out = cb_ref[...]
    for k in range(K):
        out = out + cw_ref[k:k + 1, :] * sh[k]
    return out, sh


FFN_SUB = 256


def _ffn_up_gate(xn, w_up, cw, cb, *, name, tm=1024, tn=512):
    S, D = xn.shape
    F2 = w_up.shape[1]
    F = F2 // 2
    tm, tn = min(tm, S), min(tn, F)
    sub = min(FFN_SUB, tm)
    nj = F // tn
    K = cw.shape[0]

    def body(x_ref, wa_ref, wb_ref, cwa_ref, cwb_ref, cba_ref, cbb_ref, act_ref, ha_ref, hb_ref, ac_ref, bc_ref, ta, tb):
        i = pl.program_id(1)

        @pl.when(i == 0)
        def _():
            ta[...] = jnp.zeros_like(ta)
            tb[...] = jnp.zeros_like(tb)

        tail_a, tail_b = ta[...], tb[...]
        for s in range(tm // sub):
            rows = pl.ds(s * sub, sub)
            xs = x_ref[rows, :]
            ha = _bdot(xs, wa_ref[...], NN)
            hb = _bdot(xs, wb_ref[...], NN)
            ac, _ = _conv_rows(ha, tail_a, cwa_ref, cba_ref)
            bc, _ = _conv_rows(hb, tail_b, cwb_ref, cbb_ref)
            tail_a, tail_b = ha[sub - SUB:sub], hb[sub - SUB:sub]
            ha_ref[rows, :] = ha
            hb_ref[rows, :] = hb
            ac_ref[rows, :] = ac
            bc_ref[rows, :] = bc
            act_ref[rows, :] = (ac * _sigmoid(ac) * bc).astype(act_ref.dtype)
        ta[...] = tail_a
        tb[...] = tail_b

    blk = pl.BlockSpec((tm, tn), lambda j, i: (i, j))
    return pl.pallas_call(
        body, name=name, grid=(nj, S // tm),
        in_specs=[pl.BlockSpec((tm, D), lambda j, i: (i, 0)),
                  pl.BlockSpec((D, tn), lambda j, i: (0, j)), pl.BlockSpec((D, tn), lambda j, i: (0, j + nj)),
                  pl.BlockSpec((K, tn), lambda j, i: (0, j)), pl.BlockSpec((K, tn), lambda j, i: (0, j + nj)),
                  pl.BlockSpec((1, tn), lambda j, i: (0, j)), pl.BlockSpec((1, tn), lambda j, i: (0, j + nj))],
        out_specs=[blk] * 5,
        out_shape=[jax.ShapeDtypeStruct((S, F), BF16)] + [jax.ShapeDtypeStruct((S, F), F32)] * 4,
        scratch_shapes=[pltpu.VMEM((SUB, tn), F32), pltpu.VMEM((SUB, tn), F32)],
        compiler_params=_cparams(("parallel", "arbitrary")),
    )(xn, w_up, w_up, cw, cw, cb, cb)


def _ffn_bwd(dx, w_down, hh_a, hh_b, c_a, c_b, act, xn, cw, *, name, tm=1024, tn=256):
    S, D = dx.shape
    F = hh_a.shape[1]
    tm, tn = min(tm, S), min(tn, F)
    sub = min(FFN_SUB, tm)
    nj = F // tn
    nb = S // tm
    K = cw.shape[0]

    def body(dx_ref, wd_ref, a_ref, b_ref, ac_ref, bc_ref, act_ref, xn_ref, cwa_ref, cwb_ref,
             da_ref, db_ref, ga_ref, gb_ref, dwd_ref, dwa_ref, dwb_ref, ha, hb, acc_d, acc_a, acc_b):
        i = pl.program_id(1)

        @pl.when(i == 0)
        def _():
            for ref in (ha, hb, ga_ref, gb_ref, acc_d, acc_a, acc_b):
                ref[...] = jnp.zeros_like(ref)

        rid8 = _row_iota((SUB, tn))
        heads = [ha[...], hb[...]]
        gsums = [jnp.zeros((SUB, tn), F32), jnp.zeros((SUB, tn), F32)]
        for s in reversed(range(tm // sub)):
            rows = pl.ds(s * sub, sub)
            dv = _bdot(dx_ref[rows, :], wd_ref[...], NT)
            ac, bc = ac_ref[rows, :], bc_ref[rows, :]
            sg = _sigmoid(ac)
            d_bc = dv * ac * sg
            d_ac = dv * bc * sg * (1.0 + ac * (1.0 - sg))
            for which, (d_c, h_ref, cw_ref, o_ref) in enumerate(((d_ac, a_ref, cwa_ref, da_ref),
                                                                 (d_bc, b_ref, cwb_ref, db_ref))):
                ahead = [d_c] + [_shift_up(d_c, heads[which], j) for j in range(1, K)]
                heads[which] = d_c[0:SUB]
                d_in = cw_ref[K - 1:K, :] * d_c
                for j in range(1, K):
                    d_in = d_in + cw_ref[K - 1 - j:K - j, :] * ahead[j]
                o_ref[rows, :] = d_in.astype(o_ref.dtype)
                hv = h_ref[rows, :]
                gsum = gsums[which] + jnp.where(rid8 == K, jnp.sum(d_c, axis=0, keepdims=True), 0.0)
                for k in range(K):
                    gsum = gsum + jnp.where(rid8 == k, jnp.sum(ahead[K - 1 - k] * hv, axis=0, keepdims=True), 0.0)
                gsums[which] = gsum
        ha[...], hb[...] = heads
        ga_ref[...] += gsums[0]
        gb_ref[...] += gsums[1]
        acc_d[...] += _bdot(act_ref[...], dx_ref[...], TN)
        acc_a[...] += _bdot(xn_ref[...], da_ref[...], TN)
        acc_b[...] += _bdot(xn_ref[...], db_ref[...], TN)

        @pl.when(i == nb - 1)
        def _():
            dwd_ref[...] = acc_d[...].astype(dwd_ref.dtype)
            dwa_ref[...] = acc_a[...].astype(dwa_ref.dtype)
            dwb_ref[...] = acc_b[...].astype(dwb_ref.dtype)

    rb = lambda i: nb - 1 - i
    blk = pl.BlockSpec((tm, tn), lambda j, i: (rb(i), j))
    acc = pl.BlockSpec((SUB, tn), lambda j, i: (0, j))
    rows_d = pl.BlockSpec((tm, D), lambda j, i: (rb(i), 0))
    up_blk = pl.BlockSpec((D, tn), lambda j, i: (0, j))
    return pl.pallas_call(
        body, name=name, grid=(nj, nb),
        in_specs=[rows_d, pl.BlockSpec((tn, D), lambda j, i: (j, 0)), blk, blk, blk, blk, blk, rows_d,
                  pl.BlockSpec((K, tn), lambda j, i: (0, j)), pl.BlockSpec((K, tn), lambda j, i: (0, j + nj))],
        out_specs=[blk, blk, acc, acc, pl.BlockSpec((tn, D), lambda j, i: (j, 0)), up_blk, up_blk],
        out_shape=[jax.ShapeDtypeStruct((S, F), BF16), jax.ShapeDtypeStruct((S, F), BF16),
                   jax.ShapeDtypeStruct((SUB, F), F32), jax.ShapeDtypeStruct((SUB, F), F32),
                   jax.ShapeDtypeStruct((F, D), BF16), jax.ShapeDtypeStruct((D, F), BF16),
                   jax.ShapeDtypeStruct((D, F), BF16)],
        scratch_shapes=[pltpu.VMEM((SUB, tn), F32), pltpu.VMEM((SUB, tn), F32), pltpu.VMEM((tn, D), F32),
                        pltpu.VMEM((D, tn), F32), pltpu.VMEM((D, tn), F32)],
        compiler_params=_cparams(("parallel", "arbitrary")),
    )(dx, w_down, hh_a, hh_b, c_a, c_b, act, xn, cw, cw)


ADAM_BLOCK_ELEMS = 256 * 1024


def _adamw(w, m, v, parts, *, name):
    R, C = w.shape
    n = parts.shape[0]
    tr = R
    for cand in (1024, 512, 256, 128, 64, 32, 16):
        if R % cand == 0 and cand * C <= ADAM_BLOCK_ELEMS:
            tr = cand
            break
    c1 = 1.0 - ADAM_B1 ** ADAM_STEP
    c2 = 1.0 - ADAM_B2 ** ADAM_STEP

    def body(w_ref, m_ref, v_ref, p_ref, g_ref, d_ref, nm_ref, nv_ref):
        g = p_ref[0].astype(F32)
        for k in range(1, n):
            g = g + p_ref[k].astype(F32)
        m_new = ADAM_B1 * m_ref[...] + (1.0 - ADAM_B1) * g
        v_new = ADAM_B2 * v_ref[...] + (1.0 - ADAM_B2) * (g * g)
        m_hat = m_new / c1
        v_hat = v_new / c2
        g_ref[...] = g
        d_ref[...] = -ADAM_LR * (m_hat / (jnp.sqrt(v_hat) + ADAM_EPS) + ADAM_WD * w_ref[...])
        nm_ref[...] = m_new
        nv_ref[...] = v_new

    blk = pl.BlockSpec((tr, C), lambda i: (i, 0))
    sds = jax.ShapeDtypeStruct((R, C), F32)
    return pl.pallas_call(
        body, name=name, grid=(R // tr,),
        in_specs=[blk, blk, blk, pl.BlockSpec((n, tr, C), lambda i: (0, i, 0))],
        out_specs=[blk, blk, blk, blk], out_shape=[sds, sds, sds, sds],
        compiler_params=_cparams(("parallel",)),
    )(w, m, v, parts)


def _mesh_place():
    x, y, c = lax.axis_index("x"), lax.axis_index("y"), lax.axis_index("c")
    others = [(1 - x, y), (x, 1 - y), (1 - x, 1 - y)]
    return x, y, c, others


HBM_SPEC = pl.BlockSpec(memory_space=pltpu.HBM)
SEM_SPEC = pl.BlockSpec(memory_space=pltpu.SEMAPHORE)
ANY_SPEC = pl.BlockSpec(memory_space=pl.ANY)
EFFECT = pltpu.SideEffectType.DATAFLOW_SIDE_EFFECTING


def _in_hbm(a):
    return pltpu.with_memory_space_constraint(a, pltpu.HBM)


def _split_start(srcs, lands, copies, n_cp, *, name):
    n_s, n_l = len(srcs), len(lands)

    def body(*refs):
        src_refs, land_refs = refs[:n_s], refs[n_s:n_s + n_l]
        ssem, rsem = refs[n_s + n_l], refs[n_s + n_l + 1]
        token = refs[-1]
        for outgoing, _ in copies(src_refs, land_refs, ssem, rsem):
            outgoing.start()
        token[...] = jnp.zeros_like(token)

    outs = pl.pallas_call(
        body, name=name,
        out_shape=(pltpu.SemaphoreType.DMA((n_cp,)), pltpu.SemaphoreType.DMA((n_cp,)),
                   *[pltpu.HBM(a.shape, a.dtype) for a in srcs], *[pltpu.HBM(a.shape, a.dtype) for a in lands],
                   jax.ShapeDtypeStruct((SUB, LANE), F32)),
        in_specs=[HBM_SPEC] * (n_s + n_l),
        out_specs=(SEM_SPEC, SEM_SPEC, *[HBM_SPEC] * (n_s + n_l), pl.BlockSpec(memory_space=pltpu.VMEM)),
        input_output_aliases={i: 2 + i for i in range(n_s + n_l)},
        compiler_params=pltpu.CompilerParams(has_side_effects=EFFECT),
    )(*[_in_hbm(a) for a in srcs], *[_in_hbm(a) for a in lands])
    ssem, rsem = outs[0], outs[1]
    return ssem, rsem, list(outs[2:2 + n_s]), list(outs[2 + n_s:2 + n_s + n_l]), outs[-1]


def _split_wait(srcs, lands, ssem, rsem, after, copies, *, name):
    n_s, n_l = len(srcs), len(lands)

    def body(*refs):
        src_refs, land_refs = refs[:n_s], refs[n_s:n_s + n_l]
        s_ref, r_ref = refs[n_s + n_l], refs[n_s + n_l + 1]
        for outgoing, incoming in copies(src_refs, land_refs, s_ref, r_ref):
            outgoing.wait_send()
            incoming.wait_recv()

    outs = pl.pallas_call(
        body, name=name,
        out_shape=(*[pltpu.HBM(a.shape, a.dtype) for a in srcs], *[pltpu.HBM(a.shape, a.dtype) for a in lands]),
        in_specs=[HBM_SPEC] * (n_s + n_l) + [SEM_SPEC, SEM_SPEC, ANY_SPEC], out_specs=[HBM_SPEC] * (n_s + n_l),
        input_output_aliases={i: i for i in range(n_s + n_l)},
        compiler_params=pltpu.CompilerParams(has_side_effects=EFFECT),
    )(*srcs, *lands, ssem, rsem, after)
    return list(outs[:n_s]), list(outs[n_s:])


PLACE_BLOCK_ELEMS = 512 * 1024


def _place_rows(r, w):
    return _div_tile(r, max(16, PLACE_BLOCK_ELEMS // w), 16)


def _cast_place(shard, chip, axis, after, *, name):
    r, w = shard.shape
    tr = _place_rows(r, w)
    nb = r // tr
    full = (r * N_CHIP, w) if axis == 0 else (r, w * N_CHIP)
    has_after = after is not None

    def body(chip_ref, s_ref, *rest):
        rest[-1][...] = s_ref[...].astype(BF16)

    out_map = (lambda i, ch: (ch[0] * nb + i, 0)) if axis == 0 else (lambda i, ch: (i, ch[0]))
    grid_spec = pltpu.PrefetchScalarGridSpec(
        num_scalar_prefetch=1, grid=(nb,),
        in_specs=[pl.BlockSpec((tr, w), lambda i, ch: (i, 0))] + has_after * [ANY_SPEC],
        out_specs=pl.BlockSpec((tr, w), out_map))
    return pl.pallas_call(body, name=name, grid_spec=grid_spec, out_shape=jax.ShapeDtypeStruct(full, BF16),
                          compiler_params=_cparams(("parallel",)))(chip, shard, *(has_after * [after]))


def _slot_place(g, ids, axis, *, name):
    r, w = (g.shape[0] // N_CHIP, g.shape[1]) if axis == 0 else (g.shape[0], g.shape[1] // N_CHIP)
    tr = _place_rows(r, w)
    nb = r // tr

    def body(ids_ref, g_ref, o_ref):
        o_ref[...] = g_ref[...]

    in_map = (lambda i, ids_: (ids_[0] * nb + i, 0)) if axis == 0 else (lambda i, ids_: (i, ids_[0]))
    grid_spec = pltpu.PrefetchScalarGridSpec(
        num_scalar_prefetch=1, grid=(nb,), in_specs=[pl.BlockSpec((tr, w), in_map)],
        out_specs=pl.BlockSpec((None, tr, w), lambda i, ids_: (ids_[1], i, 0)))
    return pl.pallas_call(body, name=name, grid_spec=grid_spec, out_shape=jax.ShapeDtypeStruct((N_DEV, r, w), g.dtype),
                          compiler_params=_cparams(("parallel",)))(ids, g)


class _WeightGather:
    def __init__(self, placed, shard_shapes, axes, splits, tag):
        self.placed, self.shard_shapes, self.axes, self.splits, self.tag = list(placed), shard_shapes, axes, splits, tag
        self.n = len(placed)

    def _region(self, land_refs, it, chip, half):
        r, w = self.shard_shapes[it]
        by_rows = self.axes[it] == 0
        if self.splits[it] and half is not None:
            rows = pl.ds(pl.multiple_of(half * (r // 2) + (chip * r if by_rows else 0), 16), r // 2)
        else:
            rows = pl.ds(chip * r if by_rows else 0, r)
        cols = pl.ds(0, w) if by_rows else pl.ds(pl.multiple_of(chip * w, LANE), w)
        return land_refs[it].at[rows, cols]

    def _ici(self, src_refs, land_refs, ssem, rsem):
        x, y, c, others = _mesh_place()
        pairs = []
        for it in range(self.n):
            for j, chip in enumerate(others):
                def mk(chip_from, it=it, j=j, chip=chip):
                    return pltpu.make_async_remote_copy(
                        src_ref=self._region(land_refs, it, 2 * x + y, c), dst_ref=self._region(land_refs, it, chip_from, c),
                        send_sem=ssem.at[3 * it + j], recv_sem=rsem.at[3 * it + j], device_id=(*chip, c),
                        device_id_type=MESH)
                pairs.append((mk(2 * x + y), mk(2 * chip[0] + chip[1])))
        return pairs

    def start(self):
        self.ssem, self.rsem, _, self.lands, token = _split_start(
            [], self.placed, self._ici, 3 * self.n, name="gather_start_" + self.tag)
        return token

    def _d2d(self, src_refs, land_refs, ssem, rsem):
        x, y, c, others = _mesh_place()
        pairs = []
        for it in range(self.n):
            if self.splits[it]:
                for chip in others:
                    def mk(half, it=it, chip=chip, k=len(pairs)):
                        reg = self._region(land_refs, it, 2 * chip[0] + chip[1], half)
                        return pltpu.make_async_remote_copy(src_ref=reg, dst_ref=reg, send_sem=ssem.at[k], recv_sem=rsem.at[k],
                                                            device_id=(x, y, 1 - c), device_id_type=MESH)
                    pairs.append((mk(c), mk(1 - c)))
        return pairs

    def forward(self, after):
        _, lands = _split_wait([], self.lands, self.ssem, self.rsem, after, self._ici,
                               name="gather_wait_" + self.tag)
        self.fsem, self.frsem, _, self.lands, token = _split_start(
            [], lands, self._d2d, 3 * sum(self.splits), name="gather_fwd_" + self.tag)
        return token

    def finish_forward(self, after):
        _, lands = _split_wait([], self.lands, self.fsem, self.frsem, after, self._d2d,
                               name="gather_fwd_wait_" + self.tag)
        return lands

    def finish(self, after):
        _, lands = _split_wait([], self.lands, self.ssem, self.rsem, after, self._ici,
                               name="gather_wait_" + self.tag)
        n = self.n
        n_fwd = 3 * sum(self.splits)
        if n_fwd == 0:
            return lands

        def body(*refs):
            out_refs = refs[n:2 * n]
            fsend, frecv = refs[2 * n:]
            x, y, c, others = _mesh_place()
            sibling = (x, y, 1 - c)

            def fwd(it, slot, chip, half):
                reg = self._region(out_refs, it, 2 * chip[0] + chip[1], half)
                return pltpu.make_async_remote_copy(src_ref=reg, dst_ref=reg, send_sem=fsend.at[slot],
                                                    recv_sem=frecv.at[slot], device_id=sibling, device_id_type=MESH)

            sends, recvs = [], []
            for it in range(n):
                if self.splits[it]:
                    for chip in others:
                        sends.append(fwd(it, len(sends), chip, c))
                        recvs.append(fwd(it, len(recvs), chip, 1 - c))
            for cp in sends:
                cp.start()
            for cp in recvs:
                cp.wait_recv()
            for cp in sends:
                cp.wait_send()

        fulls = pl.pallas_call(
            body, name="gather_d2d_" + self.tag, in_specs=[ANY_SPEC] * n, out_specs=[ANY_SPEC] * n,
            out_shape=[jax.ShapeDtypeStruct(a.shape, a.dtype) for a in lands],
            scratch_shapes=[pltpu.SemaphoreType.DMA((n_fwd,)), pltpu.SemaphoreType.DMA((n_fwd,))],
            input_output_aliases={i: i for i in range(n)},
        )(*lands)
        return list(fulls)


class _GradGather:
    def __init__(self, grads, axes, tag):
        self.grads, self.axes, self.tag = list(grads), axes, tag
        self.n = len(grads)
        self.shard_shapes = [(g.shape[0] // N_CHIP, g.shape[1]) if ax == 0 else (g.shape[0], g.shape[1] // N_CHIP)
                             for g, ax in zip(grads, axes)]

    def _piece(self, src_refs, it, chip):
        r, w = self.shard_shapes[it]
        if self.axes[it] == 0:
            return src_refs[it].at[pl.ds(pl.multiple_of(chip * r, 16), r), :]
        return src_refs[it].at[:, pl.ds(pl.multiple_of(chip * w, LANE), w)]

    PER_ITEM = 4

    def _remote(self, src_refs, land_refs, ssem, rsem):
        x, y, c, others = _mesh_place()
        me = 4 * x + 2 * y + c
        pairs = []
        for it in range(self.n):
            def mk(k, piece_chip, slot, to, it=it):
                return pltpu.make_async_remote_copy(
                    src_ref=self._piece(src_refs, it, piece_chip), dst_ref=land_refs[it].at[slot],
                    send_sem=ssem.at[self.PER_ITEM * it + k], recv_sem=rsem.at[self.PER_ITEM * it + k], device_id=to,
                    device_id_type=MESH)
            for j, chip in enumerate(others):
                chip_id = 2 * chip[0] + chip[1]
                pairs.append((mk(j, chip_id, me, (*chip, c)), mk(j, chip_id, 2 * chip_id + c, (*chip, c))))
            sibling = (x, y, 1 - c)
            pairs.append((mk(3, 2 * x + y, me, sibling), mk(3, 2 * x + y, 4 * x + 2 * y + 1 - c, sibling)))
        return pairs

    def start(self):
        x, y, c = lax.axis_index("x"), lax.axis_index("y"), lax.axis_index("c")
        ids = jnp.stack([2 * x + y, 4 * x + 2 * y + c]).astype(jnp.int32)
        lands = [_slot_place(g, ids, ax, name="grads_own_%s%d" % (self.tag, it))
                 for it, (g, ax) in enumerate(zip(self.grads, self.axes))]
        self.ssem, self.rsem, self.srcs, self.lands, token = _split_start(
            self.grads, lands, self._remote, self.PER_ITEM * self.n, name="grads_start_" + self.tag)
        return token

    def _forward(self, src_refs, land_refs, ssem, rsem):
        x, y, c, others = _mesh_place()
        pairs = []
        for it in range(self.n):
            for j, ch in enumerate(others):
                def mk(slot, it=it, j=j):
                    return pltpu.make_async_remote_copy(
                        src_ref=land_refs[it].at[slot], dst_ref=land_refs[it].at[slot], send_sem=ssem.at[3 * it + j],
                        recv_sem=rsem.at[3 * it + j], device_id=(x, y, 1 - c), device_id_type=MESH)
                pairs.append((mk(4 * ch[0] + 2 * ch[1] + c), mk(4 * ch[0] + 2 * ch[1] + 1 - c)))
        return pairs

    def forward(self, after):
        _, lands = _split_wait(self.srcs, self.lands, self.ssem, self.rsem, after, self._remote,
                               name="grads_wait_" + self.tag)
        self.fsem, self.frsem, _, self.lands, token = _split_start(
            [], lands, self._forward, 3 * self.n, name="grads_fwd_" + self.tag)
        return token

    def finish(self, after):
        _, lands = _split_wait([], self.lands, self.fsem, self.frsem, after, self._forward,
                               name="grads_fwd_wait_" + self.tag)
        return lands


def _allreduce_small(vec, *, name):
    R, L = vec.shape

    def body(v_ref, o_ref, buf, send, recv, lsem):
        x, y, c, others = _mesh_place()
        me = 4 * x + 2 * y + c
        sibling = (x, y, 1 - c)

        def copy(k, slot, to, src=None):
            return pltpu.make_async_remote_copy(
                src_ref=buf.at[slot] if src is None else src, dst_ref=buf.at[slot], send_sem=send.at[k],
                recv_sem=recv.at[k], device_id=to, device_id_type=MESH)

        def slot_of(chip, core):
            return 4 * chip[0] + 2 * chip[1] + core

        mine = pltpu.make_async_copy(v_ref, buf.at[me], lsem)
        mine.start()
        first = [copy(0, me, sibling, src=v_ref)]
        first += [copy(1 + j, me, (*chip, c), src=v_ref) for j, chip in enumerate(others)]
        for cp in first:
            cp.start()
        passed = [copy(4 + j, slot_of(chip, c), sibling) for j, chip in enumerate(others)]
        for j, chip in enumerate(others):
            copy(1 + j, slot_of(chip, c), (*chip, c)).wait_recv()
            passed[j].start()
        copy(0, slot_of((x, y), 1 - c), sibling).wait_recv()
        for j, chip in enumerate(others):
            copy(4 + j, slot_of(chip, 1 - c), sibling).wait_recv()
        for cp in first + passed:
            cp.wait_send()
        mine.wait()
        total = buf[0]
        for k in range(1, N_DEV):
            total = total + buf[k]
        o_ref[...] = total

    return pl.pallas_call(
        body, name=name, in_specs=[pl.BlockSpec(memory_space=pltpu.VMEM)],
        out_specs=pl.BlockSpec(memory_space=pltpu.VMEM), out_shape=jax.ShapeDtypeStruct((R, L), F32),
        scratch_shapes=[pltpu.VMEM((N_DEV, R, L), F32), pltpu.SemaphoreType.DMA((7,)), pltpu.SemaphoreType.DMA((7,)),
                        pltpu.SemaphoreType.DMA],
        compiler_params=pltpu.CompilerParams(vmem_limit_bytes=VMEM_LIMIT),
    )(vec)


PACK_ALIGN = 1024


def _pack(arrs, row_multiple):
    flat = []
    for a in arrs:
        f = a.reshape(-1).astype(F32)
        flat.append(jnp.pad(f, (0, (-f.shape[0]) % PACK_ALIGN)))
    v = jnp.concatenate(flat)
    v = jnp.pad(v, (0, (-v.shape[0]) % (LANE * row_multiple)))
    return v.reshape(-1, LANE)


def _unpack(v, shapes):
    flat = v.reshape(-1)
    out, off = [], 0
    for s in shapes:
        size = math.prod(s)
        out.append(flat[off:off + size].reshape(s))
        off += size + (-size) % PACK_ALIGN
    return out


def _tile(dim, target):
    for cand in (1024, 512, 256, 128):
        if cand <= target and dim % cand == 0:
            return cand
    return dim


def _div_tile(dim, cap, mult=LANE):
    best = None
    for cand in range(mult, min(cap, dim) + 1, mult):
        if dim % cand == 0:
            best = cand
    return dim if best is None else best


WEIGHT_NAMES = ('norm1_g', 'w_in', 'ret_g', 'rg_conv_w', 'rg_conv_b', 'rg_wa', 'rg_ba', 'rg_wx', 'rg_bx', 'rg_lambda',
                'w_out', 'norm2_g', 'norm_mem_g', 'xa_wq', 'xa_wk', 'xa_wv', 'xa_wo', 'norm3_g', 'ffn_w_up',
                'ffn_conv_w', 'ffn_conv_b', 'ffn_w_down', 'final_g')
BIG_AXIS = {'w_in': 1, 'w_out': 0, 'xa_wq': 0, 'xa_wk': 0, 'xa_wv': 0, 'xa_wo': 0, 'ffn_w_up': 1, 'ffn_w_down': 0}
SMALL_SHARDED = ('rg_conv_w', 'ffn_conv_w')


def _step(x, mem, positions, loss_target, W, Mo, Vo):
    S, D = x.shape[1], x.shape[2]
    xs, mems, tgt = x[0], mem[0], loss_target[0]
    n_mem = mems.shape[0]
    pos_col = positions.reshape(S, 1)
    chip = 2 * lax.axis_index("x") + lax.axis_index("y")

    big = list(BIG_AXIS)
    shards = {n: W[n][0] for n in big}
    G = {}
    gather_groups = (('w_in', 'rg_conv_w'), ('w_out', 'xa_wq', 'xa_wk', 'xa_wv', 'xa_wo'),
                     ('ffn_w_up', 'ffn_conv_w'), ('ffn_w_down',))
    gathers, tok = [], None
    chip1 = jnp.reshape(chip, (1,)).astype(jnp.int32)
    for gi, names in enumerate(gather_groups):
        placed = []
        for n in names:
            if n in BIG_AXIS:
                placed.append(_cast_place(shards[n], chip1, BIG_AXIS[n], tok, name="place_" + n))
            else:
                s = W[n][0] if tok is None else W[n][0] + tok[0, 0]
                full = lax.empty((s.shape[0], s.shape[1] * N_CHIP), s.dtype)
                placed.append(lax.dynamic_update_slice(full, s, (0, chip * s.shape[1])))
        ag = _WeightGather(placed, [W[n][0].shape for n in names], [BIG_AXIS.get(n, 1) for n in names],
                           [n in BIG_AXIS for n in names], "g%d" % gi)
        tok = ag.start()
        gathers.append(ag)

    def finish_gather(gi, after):
        G.update(zip(gather_groups[gi], gathers[gi].finish(after)))

    def finish_forward(gi, after):
        G.update(zip(gather_groups[gi], gathers[gi].finish_forward(after)))

    R = W['ret_g'].shape[1]
    Wl = W['rg_lambda'].shape[1]
    IN = W['w_in'].shape[2] * N_CHIP
    F2 = W['ffn_w_up'].shape[2] * N_CHIP
    F = F2 // 2

    norm1_g, norm2_g, norm3_g = W['norm1_g'] + tok[0, 0], W['norm2_g'], W['norm3_g']
    norm_mem_g, final_g, ret_g = W['norm_mem_g'], W['final_g'].reshape(1, D), W['ret_g']
    rg_cb = W['rg_conv_b']
    wa, wx = W['rg_wa'][0], W['rg_wx'][0]
    ba, bx = W['rg_ba'].reshape(1, Wl), W['rg_bx'].reshape(1, Wl)
    lam = W['rg_lambda']
    ffn_cb = W['ffn_conv_b']

    def fwd_mm(a, wname, N, K, **kw):
        return _mm(a, G[wname], mode="nn", M=a.shape[0], N=N, K=K, tm=_tile(a.shape[0], 1024), tn=1024,
                   tk=_div_tile(K, 3072), **kw)

    def fwd_mm_norm(a, wname, res, g, name):
        return _mm(a, G[wname], mode="nn", M=a.shape[0], N=D, K=a.shape[1], tm=512, tn=D, tk=_div_tile(a.shape[1], 2048),
                   out_dtype=F32, res=res, norm_g=g, name=name)

    def bwd_x_mm(d, wname, N, K, **kw):
        return _mm(d, G[wname], mode="nt", M=d.shape[0], N=N, K=K, tm=_tile(d.shape[0], 1024),
                   tn=_div_tile(N, 1024, 256), tk=_div_tile(K, 3072), **kw)

    def bwd_w_mm(a, d, M, N, **kw):
        Ks = a.shape[0]
        return _mm(a, d, mode="tn", M=M, N=N, K=Ks, out_dtype=BF16, tm=_div_tile(M, 1024, 256),
                   tn=_div_tile(N, 1024, 256), tk=_div_tile(Ks, 2048 if d.dtype == BF16 else 1024), **kw)

    xn1 = _rmsnorm_fwd(xs, norm1_g, name="norm1_fwd")
    finish_gather(0, xn1)
    rg_cw = G['rg_conv_w']
    h = fwd_mm(xn1, 'w_in', IN, D, out_dtype=F32, name="mm_in")
    half = (R // RET_HEADS) // 2
    inv = (ROPE_BASE ** (-jnp.arange(half, dtype=F32) / half)).reshape(1, half)
    cos, sin = _rope_table(pos_col, inv, name="rope_table")
    hl, mix = _lru_fwd(h, rg_cw, rg_cb, wa, ba, wx, bx, lam, name="lru_fwd")
    t1 = gathers[1].forward(hl)
    ret_raw, states, mix = _ret_fwd(h, cos, sin, ret_g + t1[0, 0], mix, name="ret_fwd")
    finish_forward(1, mix)
    x1, xn2 = fwd_mm_norm(mix, 'w_out', xs, norm2_g, "mm_out")
    memn = _rmsnorm_fwd(mems, norm_mem_g, name="norm_mem_fwd")
    km = fwd_mm(memn, 'xa_wk', D, D, out_dtype=BF16, name="mm_k")
    vm = fwd_mm(memn, 'xa_wv', D, D, out_dtype=BF16, name="mm_v")
    t2 = gathers[2].forward(x1)
    q = fwd_mm(xn2, 'xa_wq', D, D, out_dtype=BF16, after=t2, name="mm_q")
    o = _xattn_fwd(q, km, vm, name="xattn_fwd")
    x2, xn3 = fwd_mm_norm(o, 'xa_wo', x1, norm3_g, "mm_o")
    finish_forward(2, xn3)
    t3 = gathers[3].forward(xn3)
    ffn_cw = G['ffn_conv_w']
    act, hh_a, hh_b, hc_a, hc_b = _ffn_up_gate(xn3, G['ffn_w_up'], ffn_cw, ffn_cb + t3[0, 0], name="ffn_up_gate")
    finish_forward(3, act)
    x3 = fwd_mm(act, 'ffn_w_down', D, F, out_dtype=F32, res=x2, name="mm_down")
    dx3, d_final, loss8, dx3h = _final_loss(x3, tgt, final_g, name="final_loss")

    gw = {}
    grad_groups = []

    def start_grads(names, tag):
        gg = _GradGather([gw[n] for n in names], [BIG_AXIS[n] for n in names], tag)
        grad_groups.append((names, gg))
        return gg.start()

    dhh_a, dhh_b, gcw_a, gcw_b, gw['ffn_w_down'], gw_up_a, gw_up_b = _ffn_bwd(
        dx3h, G['ffn_w_down'], hh_a, hh_b, hc_a, hc_b, act, xn3, ffn_cw, name="ffn_bwd")
    gw['ffn_w_up'] = jnp.concatenate([gw_up_a, gw_up_b], axis=1)
    tok_a = start_grads(('ffn_w_down', 'ffn_w_up'), "a")
    dxn3 = bwd_x_mm(dhh_a, 'ffn_w_up', D, F, out_dtype=F32, after=tok_a, name="mm_dxn3_a")
    dxn3 = bwd_x_mm(dhh_b, 'ffn_w_up', D, F, out_dtype=F32, b_off=(0, F), res=dxn3, name="mm_dxn3_b")
    dx2, d_norm3, dx2h = _rmsnorm_bwd(x2, dxn3, norm3_g, dx3, name="norm3_bwd", emit_bf16=True)
    Kc = ffn_cw.shape[0]
    d_ffn_cw = jnp.concatenate([gcw_a[:Kc], gcw_b[:Kc]], axis=1)
    d_ffn_cb = jnp.concatenate([gcw_a[Kc:Kc + 1], gcw_b[Kc:Kc + 1]], axis=1)

    d_o = bwd_x_mm(dx2h, 'xa_wo', D, D, out_dtype=BF16, name="mm_do")
    gw['xa_wo'] = bwd_w_mm(o, dx2h, D, D, name="mm_dw_o")
    dq, dk, dv = _xattn_bwd(q, km, vm, d_o, name="xattn_bwd")
    gw['xa_wq'] = bwd_w_mm(xn2, dq, D, D, name="mm_dw_q")
    dxn2 = bwd_x_mm(dq, 'xa_wq', D, D, out_dtype=F32, name="mm_dxn2")
    gw['xa_wk'] = bwd_w_mm(memn, dk, D, D, name="mm_dw_k")
    gw['xa_wv'] = bwd_w_mm(memn, dv, D, D, name="mm_dw_v")
    dmemn = bwd_x_mm(dk, 'xa_wk', D, D, out_dtype=F32, name="mm_dmem_k")
    dmemn = bwd_x_mm(dv, 'xa_wv', D, D, out_dtype=F32, res=dmemn, name="mm_dmem_v")
    _, d_norm_mem = _rmsnorm_bwd(mems, dmemn, norm_mem_g, None, name="norm_mem_bwd")
    dx1, d_norm2, dx1h = _rmsnorm_bwd(x1, dxn2, norm2_g, dx2, name="norm2_bwd", emit_bf16=True)

    gw['w_out'] = bwd_w_mm(mix, dx1h, D, D, name="mm_dw_out")
    tok_b = start_grads(('xa_wo', 'xa_wq', 'xa_wk', 'xa_wv', 'w_out'), "b")
    dmix = bwd_x_mm(dx1h, 'w_out', D, D, out_dtype=F32, after=tok_b, name="mm_dmix")
    dh, d_ret_g = _ret_bwd(h, cos, sin, ret_g, states, ret_raw, dmix, name="ret_bwd")
    dh, d_rcw, d_rcb, d_wa, d_ba, d_wx, d_bx, d_lam = _lru_bwd(
        h, hl, dmix, dh, rg_cw, rg_cb, wa, ba, wx, bx, lam, name="lru_bwd")
    gw['w_in'] = bwd_w_mm(xn1, dh, D, IN, name="mm_dw_in")
    tok_c = start_grads(('w_in',), "c")
    dxn1 = bwd_x_mm(dh, 'w_in', D, IN, out_dtype=F32, after=tok_c, name="mm_dxn1")
    grad_x, d_norm1 = _rmsnorm_bwd(xs, dxn1, norm1_g, dx1, name="norm1_bwd")

    small_parts = {
        'norm1_g': d_norm1, 'ret_g': d_ret_g, 'rg_conv_w': d_rcw[:rg_cw.shape[0]], 'rg_conv_b': d_rcb,
        'rg_wa': d_wa, 'rg_ba': d_ba, 'rg_wx': d_wx, 'rg_bx': d_bx, 'rg_lambda': d_lam, 'norm2_g': d_norm2,
        'norm_mem_g': d_norm_mem, 'norm3_g': d_norm3, 'ffn_conv_w': d_ffn_cw, 'ffn_conv_b': d_ffn_cb,
        'final_g': d_final}
    small = [n for n in WEIGHT_NAMES if n not in BIG_AXIS]
    red_shapes = [(1,)] + [tuple(small_parts[n].shape) for n in small]
    fwd_tok = sum(gg.forward(d_norm1)[0:1, 0:1] for _, gg in grad_groups)
    reduced = _allreduce_small(_pack([loss8[0:1, 0:1] + fwd_tok] + [small_parts[n] for n in small], SUB),
                               name="allreduce_small")
    red = _unpack(reduced, red_shapes)
    loss = red[0][0]
    g_small = dict(zip(small, red[1:]))
    for n in SMALL_SHARDED:
        w_local = W[n].shape[-1]
        g_small[n] = lax.dynamic_slice_in_dim(g_small[n], chip * w_local, w_local, axis=1)

    out_g, out_d, out_m, out_v = {}, {}, {}, {}
    rows = 512
    pk = lambda d: _pack([d[n] for n in small], rows)
    g_pack = _pack([g_small[n] for n in small], rows)
    res_small = _adamw(pk(W), pk(Mo), pk(Vo), g_pack[None], name="adamw_small")
    shapes_small = [tuple(W[n].shape) for n in small]
    for dst, packed in zip((out_g, out_d, out_m, out_v), res_small):
        for n, val in zip(small, _unpack(packed, shapes_small)):
            dst[n] = val
    last = res_small[0]
    for names, gg in grad_groups:
        for n, land in zip(names, gg.finish(last)):
            g, d, m_new, v_new = _adamw(shards[n], Mo[n][0], Vo[n][0], land, name="adamw_" + n)
            out_g[n], out_d[n], out_m[n], out_v[n] = (t.reshape(W[n].shape) for t in (g, d, m_new, v_new))
            last = g
    return (loss, grad_x[None], *[out_g[n] for n in WEIGHT_NAMES], *[out_d[n] for n in WEIGHT_NAMES],
            *[out_m[n] for n in WEIGHT_NAMES], *[out_v[n] for n in WEIGHT_NAMES])


def kernel(x, mem, positions, norm1_g, w_in, ret_g, rg_conv_w, rg_conv_b, rg_wa, rg_ba, rg_wx, rg_bx, rg_lambda, w_out, norm2_g, norm_mem_g, xa_wq, xa_wk, xa_wv, xa_wo, norm3_g, ffn_w_up, ffn_conv_w, ffn_conv_b, ffn_w_down, final_g, loss_target, m_norm1_g, m_w_in, m_ret_g, m_rg_conv_w, m_rg_conv_b, m_rg_wa, m_rg_ba, m_rg_wx, m_rg_bx, m_rg_lambda, m_w_out, m_norm2_g, m_norm_mem_g, m_xa_wq, m_xa_wk, m_xa_wv, m_xa_wo, m_norm3_g, m_ffn_w_up, m_ffn_conv_w, m_ffn_conv_b, m_ffn_w_down, m_final_g, v_norm1_g, v_w_in, v_ret_g, v_rg_conv_w, v_rg_conv_b, v_rg_wa, v_rg_ba, v_rg_wx, v_rg_bx, v_rg_lambda, v_w_out, v_norm2_g, v_norm_mem_g, v_xa_wq, v_xa_wk, v_xa_wv, v_xa_wo, v_norm3_g, v_ffn_w_up, v_ffn_conv_w, v_ffn_conv_b, v_ffn_w_down, v_final_g):
    W = dict(zip(WEIGHT_NAMES, (norm1_g, w_in, ret_g, rg_conv_w, rg_conv_b, rg_wa, rg_ba, rg_wx, rg_bx, rg_lambda, w_out,
                                norm2_g, norm_mem_g, xa_wq, xa_wk, xa_wv, xa_wo, norm3_g, ffn_w_up, ffn_conv_w,
                                ffn_conv_b, ffn_w_down, final_g)))
    Mo = dict(zip(WEIGHT_NAMES, (m_norm1_g, m_w_in, m_ret_g, m_rg_conv_w, m_rg_conv_b, m_rg_wa, m_rg_ba, m_rg_wx, m_rg_bx,
                                 m_rg_lambda, m_w_out, m_norm2_g, m_norm_mem_g, m_xa_wq, m_xa_wk, m_xa_wv, m_xa_wo,
                                 m_norm3_g, m_ffn_w_up, m_ffn_conv_w, m_ffn_conv_b, m_ffn_w_down, m_final_g)))
    Vo = dict(zip(WEIGHT_NAMES, (v_norm1_g, v_w_in, v_ret_g, v_rg_conv_w, v_rg_conv_b, v_rg_wa, v_rg_ba, v_rg_wx, v_rg_bx,
                                 v_rg_lambda, v_w_out, v_norm2_g, v_norm_mem_g, v_xa_wq, v_xa_wk, v_xa_wv, v_xa_wo,
                                 v_norm3_g, v_ffn_w_up, v_ffn_conv_w, v_ffn_conv_b, v_ffn_w_down, v_final_g)))
    return _step(x, mem, positions, loss_target, W, Mo, Vo)
```

```python
import functools
import math

import jax
import jax.numpy as jnp
from jax import lax
from jax.experimental import pallas as pl
from jax.experimental.pallas import tpu as pltpu

F32 = jnp.float32
BF16 = jnp.bfloat16

EPS = 1e-6
RET_HEADS = 4
RET_CHUNK = 128
ROPE_BASE = 10000.0
LRU_BLOCKS = 8
LRU_C = 8.0
XA_HEADS = 4

ADAM_LR = 0.001
ADAM_B1 = 0.9
ADAM_B2 = 0.999
ADAM_EPS = 1e-08
ADAM_WD = 0.01
ADAM_STEP = 10

N_DEV = 8
N_CHIP = 4
MESH = pl.DeviceIdType.MESH
SUB = 8
LANE = 128
VMEM_LIMIT = 56 * 1024 * 1024

NN = ((1,), (0,))
NT = ((1,), (1,))
TN = ((0,), (0,))


def _cparams(sem):
    return pltpu.CompilerParams(dimension_semantics=sem, vmem_limit_bytes=VMEM_LIMIT)


def _sigmoid(v):
    return 1.0 / (1.0 + jnp.exp(-v))


def _bdot(a, b, dims):
    return lax.dot_general(a.astype(BF16), b.astype(BF16), (dims, ((), ())), preferred_element_type=F32)


def _row_iota(shape):
    return lax.broadcasted_iota(jnp.int32, shape, 0)


def _shift_down(v, tail, k):
    if k == 0:
        return v
    r = pltpu.roll(v, k, 0)
    rt = pltpu.roll(tail, k, 0)
    first = jnp.where(_row_iota(rt.shape) < k, rt, r[0:SUB])
    return jnp.concatenate([first, r[SUB:]], axis=0)


def _shift_up(v, head, k):
    if k == 0:
        return v
    n = v.shape[0]
    r = pltpu.roll(v, n - k, 0)
    rh = pltpu.roll(head, SUB - k, 0)
    last = jnp.where(_row_iota(rh.shape) >= SUB - k, rh, r[n - SUB:n])
    return jnp.concatenate([r[:n - SUB], last], axis=0)


def _mm(a, b, *, mode, M, N, K, out_dtype, name, tm=512, tn=512, tk=512, a_off=(0, 0), b_off=(0, 0),
        res=None, out=None, out_off=(0, 0), out_full=None, norm_g=None, after=None):
    tm, tn, tk = min(tm, M), min(tn, N), min(tk, K)
    assert M % tm == 0 and N % tn == 0 and K % tk == 0, (name, M, N, K, tm, tn, tk)
    nk = K // tk
    if mode == "nn":
        a_blk, b_blk, dims = (tm, tk), (tk, tn), NN
        a_map = lambda i, j, k: (i + a_off[0] // tm, k + a_off[1] // tk)
        b_map = lambda i, j, k: (k + b_off[0] // tk, j + b_off[1] // tn)
    elif mode == "nt":
        a_blk, b_blk, dims = (tm, tk), (tn, tk), NT
        a_map = lambda i, j, k: (i + a_off[0] // tm, k + a_off[1] // tk)
        b_map = lambda i, j, k: (j + b_off[0] // tn, k + b_off[1] // tk)
    else:
        a_blk, b_blk, dims = (tk, tm), (tk, tn), TN
        a_map = lambda i, j, k: (k + a_off[0] // tk, i + a_off[1] // tm)
        b_map = lambda i, j, k: (k + b_off[0] // tk, j + b_off[1] // tn)
    for off, blk in ((a_off, a_blk), (b_off, b_blk), (out_off, (tm, tn))):
        assert off[0] % blk[0] == 0 and off[1] % blk[1] == 0, (name, off, blk)
    o_map = lambda i, j, k: (i + out_off[0] // tm, j + out_off[1] // tn)
    has_res, has_out, has_norm, has_after = res is not None, out is not None, norm_g is not None, after is not None
    assert not has_norm or (tn == N and not has_out)

    def body(*refs):
        refs = list(refs)
        a_ref, b_ref = refs[0], refs[1]
        pos = 2
        r_ref = g_ref = n_ref = None
        if has_res:
            r_ref = refs[pos]
            pos += 1
        if has_norm:
            g_ref = refs[pos]
            pos += 1
        pos += has_out + has_after
        o_ref = refs[pos]
        pos += 1
        if has_norm:
            n_ref = refs[pos]
            pos += 1
        acc = refs[pos] if nk > 1 else None
        k = pl.program_id(2)
        part = _bdot(a_ref[...], b_ref[...], dims)

        def finish(total):
            if has_res:
                total = total + r_ref[...].astype(F32)
            o_ref[...] = total.astype(o_ref.dtype)
            if has_norm:
                r = lax.rsqrt(jnp.mean(total * total, axis=-1, keepdims=True) + EPS)
                n_ref[...] = (total * r * g_ref[...]).astype(n_ref.dtype)

        if nk == 1:
            finish(part)
        else:
            @pl.when(k == 0)
            def _():
                acc[...] = part

            @pl.when(k > 0)
            def _():
                acc[...] += part

            @pl.when(k == nk - 1)
            def _():
                finish(acc[...])

    in_specs = [pl.BlockSpec(a_blk, a_map), pl.BlockSpec(b_blk, b_map)]
    args = [a, b]
    if has_res:
        in_specs.append(pl.BlockSpec((tm, tn), lambda i, j, k: (i, j)))
        args.append(res)
    if has_norm:
        in_specs.append(pl.BlockSpec((1, N), lambda i, j, k: (0, 0)))
        args.append(norm_g)
    aliases = {}
    if has_out:
        in_specs.append(pl.BlockSpec(memory_space=pl.ANY))
        aliases = {len(args): 0}
        args.append(out)
        out_shape = jax.ShapeDtypeStruct(out.shape, out.dtype)
    else:
        out_shape = jax.ShapeDtypeStruct((M, N) if out_full is None else out_full, out_dtype)
    if has_after:
        in_specs.append(pl.BlockSpec(memory_space=pl.ANY))
        args.append(after)
    out_specs = pl.BlockSpec((tm, tn), o_map)
    if has_norm:
        out_shape = [out_shape, jax.ShapeDtypeStruct((M, N), BF16)]
        out_specs = [out_specs, pl.BlockSpec((tm, tn), lambda i, j, k: (i, j))]
    return pl.pallas_call(
        body, name=name, grid=(M // tm, N // tn, nk), in_specs=in_specs,
        out_specs=out_specs, out_shape=out_shape,
        scratch_shapes=[pltpu.VMEM((tm, tn), F32)] if nk > 1 else [],
        input_output_aliases=aliases,
        compiler_params=_cparams(("parallel", "parallel", "arbitrary")),
    )(*args)


def _rmsnorm_fwd(x, g, *, name, ts=512):
    S, D = x.shape
    ts = min(ts, S)

    def body(x_ref, g_ref, o_ref):
        xv = x_ref[...]
        r = lax.rsqrt(jnp.mean(xv * xv, axis=-1, keepdims=True) + EPS)
        o_ref[...] = (xv * r * g_ref[...]).astype(o_ref.dtype)

    return pl.pallas_call(
        body, name=name, grid=(S // ts,),
        in_specs=[pl.BlockSpec((ts, D), lambda i: (i, 0)), pl.BlockSpec((1, D), lambda i: (0, 0))],
        out_specs=pl.BlockSpec((ts, D), lambda i: (i, 0)),
        out_shape=jax.ShapeDtypeStruct((S, D), BF16),
        compiler_params=_cparams(("parallel",)),
    )(x, g)


def _rmsnorm_bwd(x, dxn, g, res, *, name, ts=512, emit_bf16=False):
    S, D = x.shape
    ts = min(ts, S)
    has_res = res is not None

    def body(*refs):
        refs = list(refs)
        dx16_ref = refs.pop() if emit_bf16 else None
        if has_res:
            x_ref, d_ref, g_ref, r_ref, dx_ref, dg_ref = refs
        else:
            x_ref, d_ref, g_ref, dx_ref, dg_ref = refs
        i = pl.program_id(0)
        xv = x_ref[...]
        dv = d_ref[...].astype(F32)
        r = lax.rsqrt(jnp.mean(xv * xv, axis=-1, keepdims=True) + EPS)
        gd = dv * g_ref[...]
        proj = jnp.mean(xv * gd, axis=-1, keepdims=True)
        dx = r * gd - xv * (r * r * r) * proj
        if has_res:
            dx = dx + r_ref[...]
        dx_ref[...] = dx
        if emit_bf16:
            dx16_ref[...] = dx.astype(BF16)
        part = jnp.sum(dv * xv * r, axis=0, keepdims=True)

        @pl.when(i == 0)
        def _():
            dg_ref[...] = part

        @pl.when(i > 0)
        def _():
            dg_ref[...] += part

    row = pl.BlockSpec((ts, D), lambda i: (i, 0))
    vec = pl.BlockSpec((1, D), lambda i: (0, 0))
    in_specs = [row, row, vec] + ([row] if has_res else [])
    args = [x, dxn, g] + ([res] if has_res else [])
    extra = emit_bf16 * [jax.ShapeDtypeStruct((S, D), BF16)]
    return pl.pallas_call(
        body, name=name, grid=(S // ts,), in_specs=in_specs, out_specs=[row, vec] + emit_bf16 * [row],
        out_shape=[jax.ShapeDtypeStruct((S, D), F32), jax.ShapeDtypeStruct((1, D), F32)] + extra,
        compiler_params=_cparams(("arbitrary",)),
    )(*args)


def _final_loss(x, target, g, *, name, ts=512):
    S, D = x.shape
    ts = min(ts, S)

    def body(x_ref, t_ref, g_ref, dx_ref, dg_ref, loss_ref, dx16_ref):
        i = pl.program_id(0)
        xv = x_ref[...]
        gv = g_ref[...]
        r = lax.rsqrt(jnp.mean(xv * xv, axis=-1, keepdims=True) + EPS)
        y = xv * r * gv
        err = y - t_ref[...]
        row_loss = jnp.mean(err * err, axis=-1, keepdims=True)
        lpart = 0.5 * jnp.sum(row_loss, axis=0, keepdims=True)
        dy = err * (1.0 / D)
        gd = dy * gv
        proj = jnp.mean(xv * gd, axis=-1, keepdims=True)
        dx = r * gd - xv * (r * r * r) * proj
        dx_ref[...] = dx
        dx16_ref[...] = dx.astype(BF16)
        part = jnp.sum(dy * xv * r, axis=0, keepdims=True)
        lfull = jnp.broadcast_to(lpart, loss_ref.shape)

        @pl.when(i == 0)
        def _():
            dg_ref[...] = part
            loss_ref[...] = lfull

        @pl.when(i > 0)
        def _():
            dg_ref[...] += part
            loss_ref[...] += lfull

    row = pl.BlockSpec((ts, D), lambda i: (i, 0))
    vec = pl.BlockSpec((1, D), lambda i: (0, 0))
    return pl.pallas_call(
        body, name=name, grid=(S // ts,), in_specs=[row, row, vec],
        out_specs=[row, vec, pl.BlockSpec((SUB, LANE), lambda i: (0, 0)), row],
        out_shape=[jax.ShapeDtypeStruct((S, D), F32), jax.ShapeDtypeStruct((1, D), F32),
                   jax.ShapeDtypeStruct((SUB, LANE), F32), jax.ShapeDtypeStruct((S, D), BF16)],
        compiler_params=_cparams(("arbitrary",)),
    )(x, target, g)


def _rope_table(pos_col, inv, *, name, ts=1024):
    S = pos_col.shape[0]
    ts = min(ts, S)
    half = inv.shape[1]

    def body(p_ref, inv_ref, c_ref, s_ref):
        ang = p_ref[...].astype(F32) * inv_ref[...]
        c_ref[...] = jnp.cos(ang)
        s_ref[...] = jnp.sin(ang)

    tab = pl.BlockSpec((ts, half), lambda i: (i, 0))
    return pl.pallas_call(
        body, name=name, grid=(S // ts,),
        in_specs=[pl.BlockSpec((ts, 1), lambda i: (i, 0)), pl.BlockSpec((1, half), lambda i: (0, 0))],
        out_specs=[tab, tab],
        out_shape=[jax.ShapeDtypeStruct((S, half), F32), jax.ShapeDtypeStruct((S, half), F32)],
        compiler_params=_cparams(("parallel",)),
    )(pos_col, inv)


def _ret_consts(C, log_g):
    ii = lax.broadcasted_iota(jnp.int32, (C, C), 0)
    jj = lax.broadcasted_iota(jnp.int32, (C, C), 1)
    diff = (ii - jj).astype(F32)
    intra = jnp.where(ii >= jj, jnp.exp(log_g * jnp.maximum(diff, 0.0)), 0.0)
    idx = lax.broadcasted_iota(jnp.int32, (C, 1), 0).astype(F32)
    qd = jnp.exp(log_g * (idx + 1.0))
    kd = jnp.exp(log_g * (C - 1.0 - idx))
    cd = math.exp(log_g * C)
    return intra, qd, kd, cd


def _rot(t, cs, sn):
    half = t.shape[-1] // 2
    t1, t2 = t[:, :half], t[:, half:]
    return jnp.concatenate([t1 * cs - t2 * sn, t1 * sn + t2 * cs], axis=-1)


def _unrot(d, cs, sn):
    half = d.shape[-1] // 2
    d1, d2 = d[:, :half], d[:, half:]
    return jnp.concatenate([d1 * cs + d2 * sn, d2 * cs - d1 * sn], axis=-1)


def _ret_fwd(h, cos, sin, ret_g, mix, *, name, ch=2):
    S = h.shape[0]
    R = ret_g.shape[1]
    H, C = RET_HEADS, RET_CHUNK
    Dh = R // H
    ts = ch * C
    assert S % ts == 0
    log_gs = [math.log(1.0 - 2.0 ** (-5.0 - hd)) for hd in range(H)]
    scale = Dh ** -0.5

    def body(x_ref, c_ref, s_ref, rg_ref, mix_in, ret_ref, st_ref, mix_ref, state):
        i = pl.program_id(0)

        @pl.when(i == 0)
        def _():
            state[...] = jnp.zeros_like(state)

        for c in range(ch):
            rows = pl.ds(c * C, C)
            cs, sn = c_ref[rows, :], s_ref[rows, :]
            for hd in range(H):
                intra, qd, kd, cd = _ret_consts(C, log_gs[hd])
                q = x_ref[rows, pl.ds(hd * Dh, Dh)]
                k = x_ref[rows, pl.ds(R + hd * Dh, Dh)]
                v = x_ref[rows, pl.ds(2 * R + hd * Dh, Dh)]
                g = x_ref[rows, pl.ds(3 * R + hd * Dh, Dh)]
                rq = _rot(q, cs, sn)
                rk = _rot(k, cs, sn) * scale
                st = state[hd]
                st_ref[c, hd] = st.astype(BF16)
                s_ = _bdot(rq, rk, NT) * intra
                ret = _bdot(s_, v, NN) + _bdot(rq * qd, st, NN)
                state[hd] = st * cd + _bdot(rk * kd, v, TN)
                ret_ref[rows, pl.ds(hd * Dh, Dh)] = ret
                rr = lax.rsqrt(jnp.mean(ret * ret, axis=-1, keepdims=True) + EPS)
                out = ret * rr * rg_ref[:, pl.ds(hd * Dh, Dh)] * (g * _sigmoid(g))
                mix_ref[rows, pl.ds(hd * Dh, Dh)] = out.astype(BF16)

    n_chunks = S // C
    return pl.pallas_call(
        body, name=name, grid=(S // ts,),
        in_specs=[pl.BlockSpec((ts, 4 * R), lambda i: (i, 0)),
                  pl.BlockSpec((ts, Dh // 2), lambda i: (i, 0)), pl.BlockSpec((ts, Dh // 2), lambda i: (i, 0)),
                  pl.BlockSpec((1, R), lambda i: (0, 0)), pl.BlockSpec(memory_space=pl.ANY)],
        out_specs=[pl.BlockSpec((ts, R), lambda i: (i, 0)),
                   pl.BlockSpec((ch, H, Dh, Dh), lambda i: (i, 0, 0, 0)),
                   pl.BlockSpec((ts, R), lambda i: (i, 0))],
        out_shape=[jax.ShapeDtypeStruct((S, R), F32), jax.ShapeDtypeStruct((n_chunks, H, Dh, Dh), BF16),
                   jax.ShapeDtypeStruct(mix.shape, mix.dtype)],
        scratch_shapes=[pltpu.VMEM((H, Dh, Dh), F32)],
        input_output_aliases={4: 2},
        compiler_params=_cparams(("arbitrary",)),
    )(h, cos, sin, ret_g, mix)


def _ret_bwd(h, cos, sin, ret_g, states, ret_raw, dmix, *, name, ch=2):
    S = h.shape[0]
    R = ret_g.shape[1]
    H, C = RET_HEADS, RET_CHUNK
    Dh = R // H
    ts = ch * C
    nb = S // ts
    log_gs = [math.log(1.0 - 2.0 ** (-5.0 - hd)) for hd in range(H)]
    scale = Dh ** -0.5

    def body(x_ref, c_ref, s_ref, rg_ref, st_ref, ret_ref, dm_ref, dh_ref, drg_ref, dstate):
        i = pl.program_id(0)

        @pl.when(i == 0)
        def _():
            dstate[...] = jnp.zeros_like(dstate)
            drg_ref[...] = jnp.zeros_like(drg_ref)

        for c in reversed(range(ch)):
            rows = pl.ds(c * C, C)
            cs, sn = c_ref[rows, :], s_ref[rows, :]
            for hd in range(H):
                intra, qd, kd, cd = _ret_consts(C, log_gs[hd])
                cols = pl.ds(hd * Dh, Dh)
                q = x_ref[rows, pl.ds(hd * Dh, Dh)]
                k = x_ref[rows, pl.ds(R + hd * Dh, Dh)]
                v = x_ref[rows, pl.ds(2 * R + hd * Dh, Dh)]
                g = x_ref[rows, pl.ds(3 * R + hd * Dh, Dh)]
                rq = _rot(q, cs, sn)
                rk = _rot(k, cs, sn) * scale
                ret = ret_ref[rows, cols]
                dm = dm_ref[rows, cols]
                rgv = rg_ref[:, cols]
                rr = lax.rsqrt(jnp.mean(ret * ret, axis=-1, keepdims=True) + EPS)
                retn = ret * rr
                sg = _sigmoid(g)
                silu = g * sg
                drg_ref[:, cols] += jnp.sum(dm * retn * silu, axis=0, keepdims=True)
                dg = dm * retn * rgv * (sg * (1.0 + g * (1.0 - sg)))
                dretn = dm * rgv * silu
                d_o = rr * dretn - ret * (rr * rr * rr) * jnp.mean(ret * dretn, axis=-1, keepdims=True)
                st = st_ref[c, hd]
                d_s = dstate[hd]
                a_ = _bdot(rq, rk, NT) * intra
                d_a = _bdot(d_o, v, NT) * intra
                d_qr = _bdot(d_a, rk, NN) + _bdot(d_o, st, NT) * qd
                d_kr = _bdot(d_a, rq, TN) + _bdot(v, d_s, NT) * kd
                d_v = _bdot(a_, d_o, TN) + _bdot(rk * kd, d_s, NN)
                dstate[hd] = d_s * cd + _bdot(rq * qd, d_o, TN)
                dh_ref[rows, pl.ds(hd * Dh, Dh)] = _unrot(d_qr, cs, sn).astype(BF16)
                dh_ref[rows, pl.ds(R + hd * Dh, Dh)] = (_unrot(d_kr, cs, sn) * scale).astype(BF16)
                dh_ref[rows, pl.ds(2 * R + hd * Dh, Dh)] = d_v.astype(BF16)
                dh_ref[rows, pl.ds(3 * R + hd * Dh, Dh)] = dg.astype(BF16)

    rb = lambda i: nb - 1 - i
    return pl.pallas_call(
        body, name=name, grid=(nb,),
        in_specs=[pl.BlockSpec((ts, 4 * R), lambda i: (rb(i), 0)),
                  pl.BlockSpec((ts, Dh // 2), lambda i: (rb(i), 0)), pl.BlockSpec((ts, Dh // 2), lambda i: (rb(i), 0)),
                  pl.BlockSpec((1, R), lambda i: (0, 0)),
                  pl.BlockSpec((ch, H, Dh, Dh), lambda i: (rb(i), 0, 0, 0)),
                  pl.BlockSpec((ts, R), lambda i: (rb(i), 0)),
                  pl.BlockSpec((ts, R), lambda i: (rb(i), 0))],
        out_specs=[pl.BlockSpec((ts, 4 * R), lambda i: (rb(i), 0)), pl.BlockSpec((1, R), lambda i: (0, 0))],
        out_shape=[jax.ShapeDtypeStruct((S, 6 * R), BF16), jax.ShapeDtypeStruct((1, R), F32)],
        scratch_shapes=[pltpu.VMEM((H, Dh, Dh), F32)],
        compiler_params=_cparams(("arbitrary",)),
    )(h, cos, sin, ret_g, states, ret_raw, dmix)


GELU_C = math.sqrt(2.0 / math.pi)
GELU_A = 0.044715


def _gelu_parts(y):
    t = jnp.tanh(GELU_C * (y + GELU_A * y * y * y))
    val = 0.5 * y * (1.0 + t)
    grad = 0.5 * (1.0 + t) + 0.5 * y * (1.0 - t * t) * GELU_C * (1.0 + 3.0 * GELU_A * y * y)
    return val, grad


def _neg_expm1(x):
    series = -x * (1.0 + x * (1.0 / 2.0) * (1.0 + x * (1.0 / 3.0) * (1.0 + x * (1.0 / 4.0) * (
        1.0 + x * (1.0 / 5.0) * (1.0 + x * (1.0 / 6.0) * (1.0 + x * (1.0 / 7.0)))))))
    return jnp.where(x > -0.35, series, 1.0 - jnp.exp(x))


def _log_sigmoid(x):
    return jnp.minimum(x, 0.0) - jnp.log1p(jnp.exp(-jnp.abs(x)))


def _lru_gates(uc, wa_ref, ba_ref, wx_ref, bx_ref):
    nbk = wa_ref.shape[0]
    bd = wa_ref.shape[1]
    rs, gs = [], []
    for n in range(nbk):
        ucn = uc[:, n * bd:(n + 1) * bd]
        rs.append(_sigmoid(_bdot(ucn, wa_ref[n], NN) + ba_ref[:, pl.ds(n * bd, bd)]))
        gs.append(_sigmoid(_bdot(ucn, wx_ref[n], NN) + bx_ref[:, pl.ds(n * bd, bd)]))
    return jnp.concatenate(rs, axis=-1), jnp.concatenate(gs, axis=-1)


def _lru_fwd(h, conv_w, conv_b, wa, ba, wx, bx, lam, *, name, ts=256):
    S = h.shape[0]
    W = lam.shape[1]
    K = conv_w.shape[0]
    ts = min(ts, S)

    def body(u_ref, y_ref, cw_ref, cb_ref, wa_ref, ba_ref, wx_ref, bx_ref, lam_ref, hl_ref, mix_ref, tail, hlast):
        i = pl.program_id(0)

        @pl.when(i == 0)
        def _():
            tail[...] = jnp.zeros_like(tail)
            hlast[...] = jnp.zeros_like(hlast)

        u = u_ref[...]
        tl = tail[...]
        uc = cb_ref[...] + cw_ref[K - 1:K, :] * u
        for k in range(K - 1):
            uc = uc + cw_ref[k:k + 1, :] * _shift_down(u, tl, K - 1 - k)
        tail[...] = u[ts - SUB:ts]
        r, ig = _lru_gates(uc, wa_ref, ba_ref, wx_ref, bx_ref)
        log_a = LRU_C * r * _log_sigmoid(lam_ref[...])
        a = jnp.exp(log_a)
        b = jnp.sqrt(_neg_expm1(2.0 * log_a)) * (ig * uc)
        in_tile = _row_iota((ts, W)) & (SUB - 1)
        d = 1
        while d < SUB:
            a_s = jnp.where(in_tile < d, 1.0, pltpu.roll(a, d, 0))
            b_s = jnp.where(in_tile < d, 0.0, pltpu.roll(b, d, 0))
            b = a * b_s + b
            a = a * a_s
            d *= 2
        before = hlast[SUB - 1:SUB, :]
        for k in range(ts // SUB):
            tile = slice(k * SUB, (k + 1) * SUB)
            h_tile = a[tile] * before + b[tile]
            hl_ref[tile, :] = h_tile
            before = h_tile[SUB - 1:SUB, :]
        hlast[...] = hl_ref[ts - SUB:ts, :]
        gy, _ = _gelu_parts(y_ref[...])
        mix_ref[...] = (hl_ref[...] * gy).astype(BF16)

    full = lambda shape: pl.BlockSpec(shape, lambda i: tuple(0 for _ in shape))
    return pl.pallas_call(
        body, name=name, grid=(S // ts,),
        in_specs=[pl.BlockSpec((ts, W), lambda i: (i, 4)), pl.BlockSpec((ts, W), lambda i: (i, 5)),
                  full(conv_w.shape), full(conv_b.shape), full(wa.shape), full(ba.shape), full(wx.shape),
                  full(bx.shape), full(lam.shape)],
        out_specs=[pl.BlockSpec((ts, W), lambda i: (i, 0)), pl.BlockSpec((ts, W), lambda i: (i, 1))],
        out_shape=[jax.ShapeDtypeStruct((S, W), F32), jax.ShapeDtypeStruct((S, 2 * W), BF16)],
        scratch_shapes=[pltpu.VMEM((SUB, W), F32), pltpu.VMEM((SUB, W), F32)],
        compiler_params=_cparams(("arbitrary",)),
    )(h, h, conv_w, conv_b, wa, ba, wx, bx, lam)


def _lru_bwd(h, hl, dmix, dh, conv_w, conv_b, wa, ba, wx, bx, lam, *, name, ts=256):
    S = h.shape[0]
    W = lam.shape[1]
    K = conv_w.shape[0]
    nbk, bd = wa.shape[0], wa.shape[1]
    ts = min(ts, S)
    nb = S // ts
    t8 = ts // SUB

    def body(u_ref, y_ref, uh_ref, hl_ref, hh_ref, dm_ref, cw_ref, cb_ref, wa_ref, ba_ref, wx_ref, bx_ref, lam_ref,
             dh_in, dh_ref, dcw_ref, dcb_ref, dwa_ref, dba_ref, dwx_ref, dbx_ref, dlam_ref, carry, head, lam_buf):
        i = pl.program_id(0)
        blk = nb - 1 - i

        @pl.when(i == 0)
        def _():
            carry[...] = jnp.zeros_like(carry)
            head[...] = jnp.zeros_like(head)
            for ref in (dcw_ref, dcb_ref, dwa_ref, dba_ref, dwx_ref, dbx_ref, dlam_ref):
                ref[...] = jnp.zeros_like(ref)

        inside = (blk > 0).astype(F32)
        u = u_ref[...]
        tl = uh_ref[...] * inside
        sh = [_shift_down(u, tl, K - 1 - k) for k in range(K)]
        uc = cb_ref[...]
        for k in range(K):
            uc = uc + cw_ref[k:k + 1, :] * sh[k]
        r, ig = _lru_gates(uc, wa_ref, ba_ref, wx_ref, bx_ref)
        lam_v = lam_ref[...]
        ls = _log_sigmoid(lam_v)
        log_a = LRU_C * r * ls
        a = jnp.exp(log_a)
        mult = jnp.sqrt(_neg_expm1(2.0 * log_a))
        hcur = hl_ref[...]
        hprev = _shift_down(hcur, hh_ref[...] * inside, 1)
        gy, dgy = _gelu_parts(y_ref[...])
        dm = dm_ref[...]
        d_y = dm * hcur * dgy
        rid = _row_iota((ts, W))
        bq = dm * gy + jnp.where(rid == ts - 1, carry[0:1, :], 0.0)
        aq = jnp.where(rid == ts - 1, 0.0, pltpu.roll(a, ts - 1, 0))
        in_tile = rid & (SUB - 1)
        d = 1
        while d < SUB:
            a_s = jnp.where(in_tile >= SUB - d, 1.0, pltpu.roll(aq, ts - d, 0))
            b_s = jnp.where(in_tile >= SUB - d, 0.0, pltpu.roll(bq, ts - d, 0))
            bq = bq + aq * b_s
            aq = aq * a_s
            d *= 2
        after_row = jnp.zeros((1, W), F32)
        for k in reversed(range(ts // SUB)):
            tile = slice(k * SUB, (k + 1) * SUB)
            lam_tile = aq[tile] * after_row + bq[tile]
            lam_buf[tile, :] = lam_tile
            after_row = lam_tile[0:1, :]
        lam_t = lam_buf[...]
        carry[...] = (a * lam_t)[0:SUB]
        d_a = lam_t * hprev
        d_mult = lam_t * (ig * uc)
        d_i = lam_t * mult * uc
        d_uc = lam_t * mult * ig
        d_log_a = d_a * a - d_mult * (a * a) / mult
        d_r = d_log_a * (LRU_C * ls)
        dlam_ref[...] += jnp.sum(d_log_a * (LRU_C * r), axis=0, keepdims=True) * _sigmoid(-lam_v)
        d_pr = d_r * r * (1.0 - r)
        d_pi = d_i * ig * (1.0 - ig)
        dba_ref[...] += jnp.sum(d_pr, axis=0, keepdims=True)
        dbx_ref[...] += jnp.sum(d_pi, axis=0, keepdims=True)
        extra = []
        for n in range(nbk):
            sl = slice(n * bd, (n + 1) * bd)
            ucn = uc[:, sl]
            dwa_ref[n] += _bdot(ucn, d_pr[:, sl], TN)
            dwx_ref[n] += _bdot(ucn, d_pi[:, sl], TN)
            extra.append(_bdot(d_pr[:, sl], wa_ref[n], NT) + _bdot(d_pi[:, sl], wx_ref[n], NT))
        d_uc = d_uc + jnp.concatenate(extra, axis=-1)
        dcb_ref[...] += jnp.sum(d_uc, axis=0, keepdims=True)
        rid8 = _row_iota((SUB, W))
        dcw = jnp.zeros((SUB, W), F32)
        for k in range(K):
            dcw = dcw + jnp.where(rid8 == k, jnp.sum(d_uc * sh[k], axis=0, keepdims=True), 0.0)
        dcw_ref[...] += dcw
        hd = head[...]
        d_u = cw_ref[K - 1:K, :] * d_uc
        for j in range(1, K):
            d_u = d_u + cw_ref[K - 1 - j:K - j, :] * _shift_up(d_uc, hd, j)
        head[...] = d_uc[0:SUB]
        dh_ref[:, 0:W] = d_u.astype(BF16)
        dh_ref[:, W:2 * W] = d_y.astype(BF16)

    rb = lambda i: nb - 1 - i
    prev8 = lambda i: jnp.maximum(rb(i) * t8 - 1, 0)
    full = lambda shape: pl.BlockSpec(shape, lambda i: tuple(0 for _ in shape))
    small = [jax.ShapeDtypeStruct((SUB, W), F32), jax.ShapeDtypeStruct((1, W), F32),
             jax.ShapeDtypeStruct(wa.shape, F32), jax.ShapeDtypeStruct((1, W), F32),
             jax.ShapeDtypeStruct(wx.shape, F32), jax.ShapeDtypeStruct((1, W), F32),
             jax.ShapeDtypeStruct((1, W), F32)]
    return pl.pallas_call(
        body, name=name, grid=(nb,),
        in_specs=[pl.BlockSpec((ts, W), lambda i: (rb(i), 4)), pl.BlockSpec((ts, W), lambda i: (rb(i), 5)),
                  pl.BlockSpec((SUB, W), lambda i: (prev8(i), 4)),
                  pl.BlockSpec((ts, W), lambda i: (rb(i), 0)), pl.BlockSpec((SUB, W), lambda i: (prev8(i), 0)),
                  pl.BlockSpec((ts, W), lambda i: (rb(i), 1)),
                  full(conv_w.shape), full(conv_b.shape), full(wa.shape), full(ba.shape), full(wx.shape),
                  full(bx.shape), full(lam.shape), pl.BlockSpec(memory_space=pl.ANY)],
        out_specs=[pl.BlockSpec((ts, 2 * W), lambda i: (rb(i), 2))] + [full(s.shape) for s in small],
        out_shape=[jax.ShapeDtypeStruct(dh.shape, dh.dtype)] + small,
        scratch_shapes=[pltpu.VMEM((SUB, W), F32), pltpu.VMEM((SUB, W), F32), pltpu.VMEM((ts, W), F32)],
        input_output_aliases={13: 0},
        compiler_params=_cparams(("arbitrary",)),
    )(h, h, h, hl, hl, dmix, conv_w, conv_b, wa, ba, wx, bx, lam, dh)


def _xattn_fwd(q, km, vm, *, name, ts=512):
    S, D = q.shape
    M = km.shape[0]
    H = XA_HEADS
    Dh = D // H
    ts = min(ts, S)
    scale = Dh ** -0.5

    def body(q_ref, k_ref, v_ref, o_ref):
        for hd in range(H):
            cols = pl.ds(hd * Dh, Dh)
            s = _bdot(q_ref[:, cols], k_ref[:, cols], NT) * scale
            s = s - jnp.max(s, axis=-1, keepdims=True)
            e = jnp.exp(s)
            p = e / jnp.sum(e, axis=-1, keepdims=True)
            o_ref[:, cols] = _bdot(p, v_ref[:, cols], NN).astype(o_ref.dtype)

    return pl.pallas_call(
        body, name=name, grid=(S // ts,),
        in_specs=[pl.BlockSpec((ts, D), lambda i: (i, 0)), pl.BlockSpec((M, D), lambda i: (0, 0)),
                  pl.BlockSpec((M, D), lambda i: (0, 0))],
        out_specs=pl.BlockSpec((ts, D), lambda i: (i, 0)),
        out_shape=jax.ShapeDtypeStruct((S, D), BF16),
        compiler_params=_cparams(("parallel",)),
    )(q, km, vm)


def _xattn_bwd(q, km, vm, d_o, *, name, ts=512):
    S, D = q.shape
    M = km.shape[0]
    H = XA_HEADS
    Dh = D // H
    ts = min(ts, S)
    scale = Dh ** -0.5

    def body(q_ref, k_ref, v_ref, do_ref, dq_ref, dk_ref, dv_ref):
        i = pl.program_id(0)

        @pl.when(i == 0)
        def _():
            dk_ref[...] = jnp.zeros_like(dk_ref)
            dv_ref[...] = jnp.zeros_like(dv_ref)

        for hd in range(H):
            cols = pl.ds(hd * Dh, Dh)
            qh, kh, vh, doh = q_ref[:, cols], k_ref[:, cols], v_ref[:, cols], do_ref[:, cols]
            s = _bdot(qh, kh, NT) * scale
            s = s - jnp.max(s, axis=-1, keepdims=True)
            e = jnp.exp(s)
            p = e / jnp.sum(e, axis=-1, keepdims=True)
            dp = _bdot(doh, vh, NT)
            ds = p * (dp - jnp.sum(dp * p, axis=-1, keepdims=True)) * scale
            dq_ref[:, cols] = _bdot(ds, kh, NN).astype(dq_ref.dtype)
            dk_ref[:, cols] += _bdot(ds, qh, TN)
            dv_ref[:, cols] += _bdot(p, doh, TN)

    row = pl.BlockSpec((ts, D), lambda i: (i, 0))
    mem = pl.BlockSpec((M, D), lambda i: (0, 0))
    return pl.pallas_call(
        body, name=name, grid=(S // ts,), in_specs=[row, mem, mem, row], out_specs=[row, mem, mem],
        out_shape=[jax.ShapeDtypeStruct((S, D), BF16), jax.ShapeDtypeStruct((M, D), F32),
                   jax.ShapeDtypeStruct((M, D), F32)],
        compiler_params=_cparams(("arbitrary",)),
    )(q, km, vm, d_o)


def _conv_rows(v, tail, cw_ref, cb_ref):
    K = cw_ref.shape[0]
    sh = [_shift_down(v, tail, K - 1 - k) for k in range(K)]
    out = cb_ref[...]
    for k in range(K):
        out = out + cw_ref[k:k + 1, :] * sh[k]
    return out, sh


FFN_SUB = 256


def _ffn_up_gate(xn, w_up, cw, cb, *, name, tm=1024, tn=512):
    S, D = xn.shape
    F2 = w_up.shape[1]
    F = F2 // 2
    tm, tn = min(tm, S), min(tn, F)
    sub = min(FFN_SUB, tm)
    nj = F // tn
    K = cw.shape[0]

    def body(x_ref, wa_ref, wb_ref, cwa_ref, cwb_ref, cba_ref, cbb_ref, act_ref, ha_ref, hb_ref, ac_ref, bc_ref, ta, tb):
        i = pl.program_id(1)

        @pl.when(i == 0)
        def _():
            ta[...] = jnp.zeros_like(ta)
            tb[...] = jnp.zeros_like(tb)

        tail_a, tail_b = ta[...], tb[...]
        for s in range(tm // sub):
            rows = pl.ds(s * sub, sub)
            xs = x_ref[rows, :]
            ha = _bdot(xs, wa_ref[...], NN)
            hb = _bdot(xs, wb_ref[...], NN)
            ac, _ = _conv_rows(ha, tail_a, cwa_ref, cba_ref)
            bc, _ = _conv_rows(hb, tail_b, cwb_ref, cbb_ref)
            tail_a, tail_b = ha[sub - SUB:sub], hb[sub - SUB:sub]
            ha_ref[rows, :] = ha
            hb_ref[rows, :] = hb
            ac_ref[rows, :] = ac
            bc_ref[rows, :] = bc
            act_ref[rows, :] = (ac * _sigmoid(ac) * bc).astype(act_ref.dtype)
        ta[...] = tail_a
        tb[...] = tail_b

    blk = pl.BlockSpec((tm, tn), lambda j, i: (i, j))
    return pl.pallas_call(
        body, name=name, grid=(nj, S // tm),
        in_specs=[pl.BlockSpec((tm, D), lambda j, i: (i, 0)),
                  pl.BlockSpec((D, tn), lambda j, i: (0, j)), pl.BlockSpec((D, tn), lambda j, i: (0, j + nj)),
                  pl.BlockSpec((K, tn), lambda j, i: (0, j)), pl.BlockSpec((K, tn), lambda j, i: (0, j + nj)),
                  pl.BlockSpec((1, tn), lambda j, i: (0, j)), pl.BlockSpec((1, tn), lambda j, i: (0, j + nj))],
        out_specs=[blk] * 5,
        out_shape=[jax.ShapeDtypeStruct((S, F), BF16)] + [jax.ShapeDtypeStruct((S, F), F32)] * 4,
        scratch_shapes=[pltpu.VMEM((SUB, tn), F32), pltpu.VMEM((SUB, tn), F32)],
        compiler_params=_cparams(("parallel", "arbitrary")),
    )(xn, w_up, w_up, cw, cw, cb, cb)


def _ffn_bwd(dx, w_down, hh_a, hh_b, c_a, c_b, act, xn, cw, *, name, tm=1024, tn=256):
    S, D = dx.shape
    F = hh_a.shape[1]
    tm, tn = min(tm, S), min(tn, F)
    sub = min(FFN_SUB, tm)
    nj = F // tn
    nb = S // tm
    K = cw.shape[0]

    def body(dx_ref, wd_ref, a_ref, b_ref, ac_ref, bc_ref, act_ref, xn_ref, cwa_ref, cwb_ref,
             da_ref, db_ref, ga_ref, gb_ref, dwd_ref, dwa_ref, dwb_ref, ha, hb, acc_d, acc_a, acc_b):
        i = pl.program_id(1)

        @pl.when(i == 0)
        def _():
            for ref in (ha, hb, ga_ref, gb_ref, acc_d, acc_a, acc_b):
                ref[...] = jnp.zeros_like(ref)

        rid8 = _row_iota((SUB, tn))
        heads = [ha[...], hb[...]]
        gsums = [jnp.zeros((SUB, tn), F32), jnp.zeros((SUB, tn), F32)]
        for s in reversed(range(tm // sub)):
            rows = pl.ds(s * sub, sub)
            dv = _bdot(dx_ref[rows, :], wd_ref[...], NT)
            ac, bc = ac_ref[rows, :], bc_ref[rows, :]
            sg = _sigmoid(ac)
            d_bc = dv * ac * sg
            d_ac = dv * bc * sg * (1.0 + ac * (1.0 - sg))
            for which, (d_c, h_ref, cw_ref, o_ref) in enumerate(((d_ac, a_ref, cwa_ref, da_ref),
                                                                 (d_bc, b_ref, cwb_ref, db_ref))):
                ahead = [d_c] + [_shift_up(d_c, heads[which], j) for j in range(1, K)]
                heads[which] = d_c[0:SUB]
                d_in = cw_ref[K - 1:K, :] * d_c
                for j in range(1, K):
                    d_in = d_in + cw_ref[K - 1 - j:K - j, :] * ahead[j]
                o_ref[rows, :] = d_in.astype(o_ref.dtype)
                hv = h_ref[rows, :]
                gsum = gsums[which] + jnp.where(rid8 == K, jnp.sum(d_c, axis=0, keepdims=True), 0.0)
                for k in range(K):
                    gsum = gsum + jnp.where(rid8 == k, jnp.sum(ahead[K - 1 - k] * hv, axis=0, keepdims=True), 0.0)
                gsums[which] = gsum
        ha[...], hb[...] = heads
        ga_ref[...] += gsums[0]
        gb_ref[...] += gsums[1]
        acc_d[...] += _bdot(act_ref[...], dx_ref[...], TN)
        acc_a[...] += _bdot(xn_ref[...], da_ref[...], TN)
        acc_b[...] += _bdot(xn_ref[...], db_ref[...], TN)

        @pl.when(i == nb - 1)
        def _():
            dwd_ref[...] = acc_d[...].astype(dwd_ref.dtype)
            dwa_ref[...] = acc_a[...].astype(dwa_ref.dtype)
            dwb_ref[...] = acc_b[...].astype(dwb_ref.dtype)

    rb = lambda i: nb - 1 - i
    blk = pl.BlockSpec((tm, tn), lambda j, i: (rb(i), j))
    acc = pl.BlockSpec((SUB, tn), lambda j, i: (0, j))
    rows_d = pl.BlockSpec((tm, D), lambda j, i: (rb(i), 0))
    up_blk = pl.BlockSpec((D, tn), lambda j, i: (0, j))
    return pl.pallas_call(
        body, name=name, grid=(nj, nb),
        in_specs=[rows_d, pl.BlockSpec((tn, D), lambda j, i: (j, 0)), blk, blk, blk, blk, blk, rows_d,
                  pl.BlockSpec((K, tn), lambda j, i: (0, j)), pl.BlockSpec((K, tn), lambda j, i: (0, j + nj))],
        out_specs=[blk, blk, acc, acc, pl.BlockSpec((tn, D), lambda j, i: (j, 0)), up_blk, up_blk],
        out_shape=[jax.ShapeDtypeStruct((S, F), BF16), jax.ShapeDtypeStruct((S, F), BF16),
                   jax.ShapeDtypeStruct((SUB, F), F32), jax.ShapeDtypeStruct((SUB, F), F32),
                   jax.ShapeDtypeStruct((F, D), BF16), jax.ShapeDtypeStruct((D, F), BF16),
                   jax.ShapeDtypeStruct((D, F), BF16)],
        scratch_shapes=[pltpu.VMEM((SUB, tn), F32), pltpu.VMEM((SUB, tn), F32), pltpu.VMEM((tn, D), F32),
                        pltpu.VMEM((D, tn), F32), pltpu.VMEM((D, tn), F32)],
        compiler_params=_cparams(("parallel", "arbitrary")),
    )(dx, w_down, hh_a, hh_b, c_a, c_b, act, xn, cw, cw)


ADAM_BLOCK_ELEMS = 256 * 1024


def _adamw(w, m, v, parts, *, name):
    R, C = w.shape
    n = parts.shape[0]
    tr = R
    for cand in (1024, 512, 256, 128, 64, 32, 16):
        if R % cand == 0 and cand * C <= ADAM_BLOCK_ELEMS:
            tr = cand
            break
    c1 = 1.0 - ADAM_B1 ** ADAM_STEP
    c2 = 1.0 - ADAM_B2 ** ADAM_STEP

    def body(w_ref, m_ref, v_ref, p_ref, g_ref, d_ref, nm_ref, nv_ref):
        g = p_ref[0].astype(F32)
        for k in range(1, n):
            g = g + p_ref[k].astype(F32)
        m_new = ADAM_B1 * m_ref[...] + (1.0 - ADAM_B1) * g
        v_new = ADAM_B2 * v_ref[...] + (1.0 - ADAM_B2) * (g * g)
        m_hat = m_new / c1
        v_hat = v_new / c2
        g_ref[...] = g
        d_ref[...] = -ADAM_LR * (m_hat / (jnp.sqrt(v_hat) + ADAM_EPS) + ADAM_WD * w_ref[...])
        nm_ref[...] = m_new
        nv_ref[...] = v_new

    blk = pl.BlockSpec((tr, C), lambda i: (i, 0))
    sds = jax.ShapeDtypeStruct((R, C), F32)
    return pl.pallas_call(
        body, name=name, grid=(R // tr,),
        in_specs=[blk, blk, blk, pl.BlockSpec((n, tr, C), lambda i: (0, i, 0))],
        out_specs=[blk, blk, blk, blk], out_shape=[sds, sds, sds, sds],
        compiler_params=_cparams(("parallel",)),
    )(w, m, v, parts)


def _mesh_place():
    x, y, c = lax.axis_index("x"), lax.axis_index("y"), lax.axis_index("c")
    others = [(1 - x, y), (x, 1 - y), (1 - x, 1 - y)]
    return x, y, c, others


HBM_SPEC = pl.BlockSpec(memory_space=pltpu.HBM)
SEM_SPEC = pl.BlockSpec(memory_space=pltpu.SEMAPHORE)
ANY_SPEC = pl.BlockSpec(memory_space=pl.ANY)
EFFECT = pltpu.SideEffectType.DATAFLOW_SIDE_EFFECTING


def _in_hbm(a):
    return pltpu.with_memory_space_constraint(a, pltpu.HBM)


def _split_start(srcs, lands, copies, n_cp, *, name):
    n_s, n_l = len(srcs), len(lands)

    def body(*refs):
        src_refs, land_refs = refs[:n_s], refs[n_s:n_s + n_l]
        ssem, rsem = refs[n_s + n_l], refs[n_s + n_l + 1]
        token = refs[-1]
        for outgoing, _ in copies(src_refs, land_refs, ssem, rsem):
            outgoing.start()
        token[...] = jnp.zeros_like(token)

    outs = pl.pallas_call(
        body, name=name,
        out_shape=(pltpu.SemaphoreType.DMA((n_cp,)), pltpu.SemaphoreType.DMA((n_cp,)),
                   *[pltpu.HBM(a.shape, a.dtype) for a in srcs], *[pltpu.HBM(a.shape, a.dtype) for a in lands],
                   jax.ShapeDtypeStruct((SUB, LANE), F32)),
        in_specs=[HBM_SPEC] * (n_s + n_l),
        out_specs=(SEM_SPEC, SEM_SPEC, *[HBM_SPEC] * (n_s + n_l), pl.BlockSpec(memory_space=pltpu.VMEM)),
        input_output_aliases={i: 2 + i for i in range(n_s + n_l)},
        compiler_params=pltpu.CompilerParams(has_side_effects=EFFECT),
    )(*[_in_hbm(a) for a in srcs], *[_in_hbm(a) for a in lands])
    ssem, rsem = outs[0], outs[1]
    return ssem, rsem, list(outs[2:2 + n_s]), list(outs[2 + n_s:2 + n_s + n_l]), outs[-1]


def _split_wait(srcs, lands, ssem, rsem, after, copies, *, name):
    n_s, n_l = len(srcs), len(lands)

    def body(*refs):
        src_refs, land_refs = refs[:n_s], refs[n_s:n_s + n_l]
        s_ref, r_ref = refs[n_s + n_l], refs[n_s + n_l + 1]
        for outgoing, incoming in copies(src_refs, land_refs, s_ref, r_ref):
            outgoing.wait_send()
            incoming.wait_recv()

    outs = pl.pallas_call(
        body, name=name,
        out_shape=(*[pltpu.HBM(a.shape, a.dtype) for a in srcs], *[pltpu.HBM(a.shape, a.dtype) for a in lands]),
        in_specs=[HBM_SPEC] * (n_s + n_l) + [SEM_SPEC, SEM_SPEC, ANY_SPEC], out_specs=[HBM_SPEC] * (n_s + n_l),
        input_output_aliases={i: i for i in range(n_s + n_l)},
        compiler_params=pltpu.CompilerParams(has_side_effects=EFFECT),
    )(*srcs, *lands, ssem, rsem, after)
    return list(outs[:n_s]), list(outs[n_s:])


PLACE_BLOCK_ELEMS = 512 * 1024


def _place_rows(r, w):
    return _div_tile(r, max(16, PLACE_BLOCK_ELEMS // w), 16)


def _cast_place(shard, chip, axis, after, *, name):
    r, w = shard.shape
    tr = _place_rows(r, w)
    nb = r // tr
    full = (r * N_CHIP, w) if axis == 0 else (r, w * N_CHIP)
    has_after = after is not None

    def body(chip_ref, s_ref, *rest):
        rest[-1][...] = s_ref[...].astype(BF16)

    out_map = (lambda i, ch: (ch[0] * nb + i, 0)) if axis == 0 else (lambda i, ch: (i, ch[0]))
    grid_spec = pltpu.PrefetchScalarGridSpec(
        num_scalar_prefetch=1, grid=(nb,),
        in_specs=[pl.BlockSpec((tr, w), lambda i, ch: (i, 0))] + has_after * [ANY_SPEC],
        out_specs=pl.BlockSpec((tr, w), out_map))
    return pl.pallas_call(body, name=name, grid_spec=grid_spec, out_shape=jax.ShapeDtypeStruct(full, BF16),
                          compiler_params=_cparams(("parallel",)))(chip, shard, *(has_after * [after]))


def _slot_place(g, ids, axis, *, name):
    r, w = (g.shape[0] // N_CHIP, g.shape[1]) if axis == 0 else (g.shape[0], g.shape[1] // N_CHIP)
    tr = _place_rows(r, w)
    nb = r // tr

    def body(ids_ref, g_ref, o_ref):
        o_ref[...] = g_ref[...]

    in_map = (lambda i, ids_: (ids_[0] * nb + i, 0)) if axis == 0 else (lambda i, ids_: (i, ids_[0]))
    grid_spec = pltpu.PrefetchScalarGridSpec(
        num_scalar_prefetch=1, grid=(nb,), in_specs=[pl.BlockSpec((tr, w), in_map)],
        out_specs=pl.BlockSpec((None, tr, w), lambda i, ids_: (ids_[1], i, 0)))
    return pl.pallas_call(body, name=name, grid_spec=grid_spec, out_shape=jax.ShapeDtypeStruct((N_DEV, r, w), g.dtype),
                          compiler_params=_cparams(("parallel",)))(ids, g)


class _WeightGather:
    def __init__(self, placed, shard_shapes, axes, splits, tag):
        self.placed, self.shard_shapes, self.axes, self.splits, self.tag = list(placed), shard_shapes, axes, splits, tag
        self.n = len(placed)

    def _region(self, land_refs, it, chip, half):
        r, w = self.shard_shapes[it]
        by_rows = self.axes[it] == 0
        if self.splits[it] and half is not None:
            rows = pl.ds(pl.multiple_of(half * (r // 2) + (chip * r if by_rows else 0), 16), r // 2)
        else:
            rows = pl.ds(chip * r if by_rows else 0, r)
        cols = pl.ds(0, w) if by_rows else pl.ds(pl.multiple_of(chip * w, LANE), w)
        return land_refs[it].at[rows, cols]

    def _ici(self, src_refs, land_refs, ssem, rsem):
        x, y, c, others = _mesh_place()
        pairs = []
        for it in range(self.n):
            for j, chip in enumerate(others):
                def mk(chip_from, it=it, j=j, chip=chip):
                    return pltpu.make_async_remote_copy(
                        src_ref=self._region(land_refs, it, 2 * x + y, c), dst_ref=self._region(land_refs, it, chip_from, c),
                        send_sem=ssem.at[3 * it + j], recv_sem=rsem.at[3 * it + j], device_id=(*chip, c),
                        device_id_type=MESH)
                pairs.append((mk(2 * x + y), mk(2 * chip[0] + chip[1])))
        return pairs

    def start(self):
        self.ssem, self.rsem, _, self.lands, token = _split_start(
            [], self.placed, self._ici, 3 * self.n, name="gather_start_" + self.tag)
        return token

    def _d2d(self, src_refs, land_refs, ssem, rsem):
        x, y, c, others = _mesh_place()
        pairs = []
        for it in range(self.n):
            if self.splits[it]:
                for chip in others:
                    def mk(half, it=it, chip=chip, k=len(pairs)):
                        reg = self._region(land_refs, it, 2 * chip[0] + chip[1], half)
                        return pltpu.make_async_remote_copy(src_ref=reg, dst_ref=reg, send_sem=ssem.at[k], recv_sem=rsem.at[k],
                                                            device_id=(x, y, 1 - c), device_id_type=MESH)
                    pairs.append((mk(c), mk(1 - c)))
        return pairs

    def forward(self, after):
        _, lands = _split_wait([], self.lands, self.ssem, self.rsem, after, self._ici,
                               name="gather_wait_" + self.tag)
        self.fsem, self.frsem, _, self.lands, token = _split_start(
            [], lands, self._d2d, 3 * sum(self.splits), name="gather_fwd_" + self.tag)
        return token

    def finish_forward(self, after):
        _, lands = _split_wait([], self.lands, self.fsem, self.frsem, after, self._d2d,
                               name="gather_fwd_wait_" + self.tag)
        return lands

    def finish(self, after):
        _, lands = _split_wait([], self.lands, self.ssem, self.rsem, after, self._ici,
                               name="gather_wait_" + self.tag)
        n = self.n
        n_fwd = 3 * sum(self.splits)
        if n_fwd == 0:
            return lands

        def body(*refs):
            out_refs = refs[n:2 * n]
            fsend, frecv = refs[2 * n:]
            x, y, c, others = _mesh_place()
            sibling = (x, y, 1 - c)

            def fwd(it, slot, chip, half):
                reg = self._region(out_refs, it, 2 * chip[0] + chip[1], half)
                return pltpu.make_async_remote_copy(src_ref=reg, dst_ref=reg, send_sem=fsend.at[slot],
                                                    recv_sem=frecv.at[slot], device_id=sibling, device_id_type=MESH)

            sends, recvs = [], []
            for it in range(n):
                if self.splits[it]:
                    for chip in others:
                        sends.append(fwd(it, len(sends), chip, c))
                        recvs.append(fwd(it, len(recvs), chip, 1 - c))
            for cp in sends:
                cp.start()
            for cp in recvs:
                cp.wait_recv()
            for cp in sends:
                cp.wait_send()

        fulls = pl.pallas_call(
            body, name="gather_d2d_" + self.tag, in_specs=[ANY_SPEC] * n, out_specs=[ANY_SPEC] * n,
            out_shape=[jax.ShapeDtypeStruct(a.shape, a.dtype) for a in lands],
            scratch_shapes=[pltpu.SemaphoreType.DMA((n_fwd,)), pltpu.SemaphoreType.DMA((n_fwd,))],
            input_output_aliases={i: i for i in range(n)},
        )(*lands)
        return list(fulls)


class _GradGather:
    def __init__(self, grads, axes, tag):
        self.grads, self.axes, self.tag = list(grads), axes, tag
        self.n = len(grads)
        self.shard_shapes = [(g.shape[0] // N_CHIP, g.shape[1]) if ax == 0 else (g.shape[0], g.shape[1] // N_CHIP)
                             for g, ax in zip(grads, axes)]

    def _piece(self, src_refs, it, chip):
        r, w = self.shard_shapes[it]
        if self.axes[it] == 0:
            return src_refs[it].at[pl.ds(pl.multiple_of(chip * r, 16), r), :]
        return src_refs[it].at[:, pl.ds(pl.multiple_of(chip * w, LANE), w)]

    PER_ITEM = 4

    def _remote(self, src_refs, land_refs, ssem, rsem):
        x, y, c, others = _mesh_place()
        me = 4 * x + 2 * y + c
        pairs = []
        for it in range(self.n):
            def mk(k, piece_chip, slot, to, it=it):
                return pltpu.make_async_remote_copy(
                    src_ref=self._piece(src_refs, it, piece_chip), dst_ref=land_refs[it].at[slot],
                    send_sem=ssem.at[self.PER_ITEM * it + k], recv_sem=rsem.at[self.PER_ITEM * it + k], device_id=to,
                    device_id_type=MESH)
            for j, chip in enumerate(others):
                chip_id = 2 * chip[0] + chip[1]
                pairs.append((mk(j, chip_id, me, (*chip, c)), mk(j, chip_id, 2 * chip_id + c, (*chip, c))))
            sibling = (x, y, 1 - c)
            pairs.append((mk(3, 2 * x + y, me, sibling), mk(3, 2 * x + y, 4 * x + 2 * y + 1 - c, sibling)))
        return pairs

    def start(self):
        x, y, c = lax.axis_index("x"), lax.axis_index("y"), lax.axis_index("c")
        ids = jnp.stack([2 * x + y, 4 * x + 2 * y + c]).astype(jnp.int32)
        lands = [_slot_place(g, ids, ax, name="grads_own_%s%d" % (self.tag, it))
                 for it, (g, ax) in enumerate(zip(self.grads, self.axes))]
        self.ssem, self.rsem, self.srcs, self.lands, token = _split_start(
            self.grads, lands, self._remote, self.PER_ITEM * self.n, name="grads_start_" + self.tag)
        return token

    def _forward(self, src_refs, land_refs, ssem, rsem):
        x, y, c, others = _mesh_place()
        pairs = []
        for it in range(self.n):
            for j, ch in enumerate(others):
                def mk(slot, it=it, j=j):
                    return pltpu.make_async_remote_copy(
                        src_ref=land_refs[it].at[slot], dst_ref=land_refs[it].at[slot], send_sem=ssem.at[3 * it + j],
                        recv_sem=rsem.at[3 * it + j], device_id=(x, y, 1 - c), device_id_type=MESH)
                pairs.append((mk(4 * ch[0] + 2 * ch[1] + c), mk(4 * ch[0] + 2 * ch[1] + 1 - c)))
        return pairs

    def forward(self, after):
        _, lands = _split_wait(self.srcs, self.lands, self.ssem, self.rsem, after, self._remote,
                               name="grads_wait_" + self.tag)
        self.fsem, self.frsem, _, self.lands, token = _split_start(
            [], lands, self._forward, 3 * self.n, name="grads_fwd_" + self.tag)
        return token

    def finish(self, after):
        _, lands = _split_wait([], self.lands, self.fsem, self.frsem, after, self._forward,
                               name="grads_fwd_wait_" + self.tag)
        return lands


def _allreduce_small(vec, *, name):
    R, L = vec.shape

    def body(v_ref, o_ref, buf, send, recv, lsem):
        x, y, c, others = _mesh_place()
        me = 4 * x + 2 * y + c
        sibling = (x, y, 1 - c)

        def copy(k, slot, to, src=None):
            return pltpu.make_async_remote_copy(
                src_ref=buf.at[slot] if src is None else src, dst_ref=buf.at[slot], send_sem=send.at[k],
                recv_sem=recv.at[k], device_id=to, device_id_type=MESH)

        def slot_of(chip, core):
            return 4 * chip[0] + 2 * chip[1] + core

        mine = pltpu.make_async_copy(v_ref, buf.at[me], lsem)
        mine.start()
        first = [copy(0, me, sibling, src=v_ref)]
        first += [copy(1 + j, me, (*chip, c), src=v_ref) for j, chip in enumerate(others)]
        for cp in first:
            cp.start()
        passed = [copy(4 + j, slot_of(chip, c), sibling) for j, chip in enumerate(others)]
        for j, chip in enumerate(others):
            copy(1 + j, slot_of(chip, c), (*chip, c)).wait_recv()
            passed[j].start()
        copy(0, slot_of((x, y), 1 - c), sibling).wait_recv()
        for j, chip in enumerate(others):
            copy(4 + j, slot_of(chip, 1 - c), sibling).wait_recv()
        for cp in first + passed:
            cp.wait_send()
        mine.wait()
        total = buf[0]
        for k in range(1, N_DEV):
            total = total + buf[k]
        o_ref[...] = total

    return pl.pallas_call(
        body, name=name, in_specs=[pl.BlockSpec(memory_space=pltpu.VMEM)],
        out_specs=pl.BlockSpec(memory_space=pltpu.VMEM), out_shape=jax.ShapeDtypeStruct((R, L), F32),
        scratch_shapes=[pltpu.VMEM((N_DEV, R, L), F32), pltpu.SemaphoreType.DMA((7,)), pltpu.SemaphoreType.DMA((7,)),
                        pltpu.SemaphoreType.DMA],
        compiler_params=pltpu.CompilerParams(vmem_limit_bytes=VMEM_LIMIT),
    )(vec)


PACK_ALIGN = 1024


def _pack(arrs, row_multiple):
    flat = []
    for a in arrs:
        f = a.reshape(-1).astype(F32)
        flat.append(jnp.pad(f, (0, (-f.shape[0]) % PACK_ALIGN)))
    v = jnp.concatenate(flat)
    v = jnp.pad(v, (0, (-v.shape[0]) % (LANE * row_multiple)))
    return v.reshape(-1, LANE)


def _unpack(v, shapes):
    flat = v.reshape(-1)
    out, off = [], 0
    for s in shapes:
        size = math.prod(s)
        out.append(flat[off:off + size].reshape(s))
        off += size + (-size) % PACK_ALIGN
    return out


def _tile(dim, target):
    for cand in (1024, 512, 256, 128):
        if cand <= target and dim % cand == 0:
            return cand
    return dim


def _div_tile(dim, cap, mult=LANE):
    best = None
    for cand in range(mult, min(cap, dim) + 1, mult):
        if dim % cand == 0:
            best = cand
    return dim if best is None else best


WEIGHT_NAMES = ('norm1_g', 'w_in', 'ret_g', 'rg_conv_w', 'rg_conv_b', 'rg_wa', 'rg_ba', 'rg_wx', 'rg_bx', 'rg_lambda',
                'w_out', 'norm2_g', 'norm_mem_g', 'xa_wq', 'xa_wk', 'xa_wv', 'xa_wo', 'norm3_g', 'ffn_w_up',
                'ffn_conv_w', 'ffn_conv_b', 'ffn_w_down', 'final_g')
BIG_AXIS = {'w_in': 1, 'w_out': 0, 'xa_wq': 0, 'xa_wk': 0, 'xa_wv': 0, 'xa_wo': 0, 'ffn_w_up': 1, 'ffn_w_down': 0}
SMALL_SHARDED = ('rg_conv_w', 'ffn_conv_w')


def _step(x, mem, positions, loss_target, W, Mo, Vo):
    S, D = x.shape[1], x.shape[2]
    xs, mems, tgt = x[0], mem[0], loss_target[0]
    n_mem = mems.shape[0]
    pos_col = positions.reshape(S, 1)
    chip = 2 * lax.axis_index("x") + lax.axis_index("y")

    big = list(BIG_AXIS)
    shards = {n: W[n][0] for n in big}
    G = {}
    gather_groups = (('w_in', 'rg_conv_w'), ('w_out', 'xa_wq', 'xa_wk', 'xa_wv', 'xa_wo'),
                     ('ffn_w_up', 'ffn_conv_w'), ('ffn_w_down',))
    gathers, tok = [], None
    chip1 = jnp.reshape(chip, (1,)).astype(jnp.int32)
    for gi, names in enumerate(gather_groups):
        placed = []
        for n in names:
            if n in BIG_AXIS:
                placed.append(_cast_place(shards[n], chip1, BIG_AXIS[n], tok, name="place_" + n))
            else:
                s = W[n][0] if tok is None else W[n][0] + tok[0, 0]
                full = lax.empty((s.shape[0], s.shape[1] * N_CHIP), s.dtype)
                placed.append(lax.dynamic_update_slice(full, s, (0, chip * s.shape[1])))
        ag = _WeightGather(placed, [W[n][0].shape for n in names], [BIG_AXIS.get(n, 1) for n in names],
                           [n in BIG_AXIS for n in names], "g%d" % gi)
        tok = ag.start()
        gathers.append(ag)

    def finish_gather(gi, after):
        G.update(zip(gather_groups[gi], gathers[gi].finish(after)))

    def finish_forward(gi, after):
        G.update(zip(gather_groups[gi], gathers[gi].finish_forward(after)))

    R = W['ret_g'].shape[1]
    Wl = W['rg_lambda'].shape[1]
    IN = W['w_in'].shape[2] * N_CHIP
    F2 = W['ffn_w_up'].shape[2] * N_CHIP
    F = F2 // 2

    norm1_g, norm2_g, norm3_g = W['norm1_g'] + tok[0, 0], W['norm2_g'], W['norm3_g']
    norm_mem_g, final_g, ret_g = W['norm_mem_g'], W['final_g'].reshape(1, D), W['ret_g']
    rg_cb = W['rg_conv_b']
    wa, wx = W['rg_wa'][0], W['rg_wx'][0]
    ba, bx = W['rg_ba'].reshape(1, Wl), W['rg_bx'].reshape(1, Wl)
    lam = W['rg_lambda']
    ffn_cb = W['ffn_conv_b']

    def fwd_mm(a, wname, N, K, **kw):
        return _mm(a, G[wname], mode="nn", M=a.shape[0], N=N, K=K, tm=_tile(a.shape[0], 1024), tn=1024,
                   tk=_div_tile(K, 3072), **kw)

    def fwd_mm_norm(a, wname, res, g, name):
        return _mm(a, G[wname], mode="nn", M=a.shape[0], N=D, K=a.shape[1], tm=512, tn=D, tk=_div_tile(a.shape[1], 2048),
                   out_dtype=F32, res=res, norm_g=g, name=name)

    def bwd_x_mm(d, wname, N, K, **kw):
        return _mm(d, G[wname], mode="nt", M=d.shape[0], N=N, K=K, tm=_tile(d.shape[0], 1024),
                   tn=_div_tile(N, 1024, 256), tk=_div_tile(K, 3072), **kw)

    def bwd_w_mm(a, d, M, N, **kw):
        Ks = a.shape[0]
        return _mm(a, d, mode="tn", M=M, N=N, K=Ks, out_dtype=BF16, tm=_div_tile(M, 1024, 256),
                   tn=_div_tile(N, 1024, 256), tk=_div_tile(Ks, 4096 if d.dtype == BF16 else 1024), **kw)

    xn1 = _rmsnorm_fwd(xs, norm1_g, name="norm1_fwd")
    finish_gather(0, xn1)
    rg_cw = G['rg_conv_w']
    h = fwd_mm(xn1, 'w_in', IN, D, out_dtype=F32, name="mm_in")
    half = (R // RET_HEADS) // 2
    inv = (ROPE_BASE ** (-jnp.arange(half, dtype=F32) / half)).reshape(1, half)
    cos, sin = _rope_table(pos_col, inv, name="rope_table")
    hl, mix = _lru_fwd(h, rg_cw, rg_cb, wa, ba, wx, bx, lam, name="lru_fwd")
    t1 = gathers[1].forward(hl)
    ret_raw, states, mix = _ret_fwd(h, cos, sin, ret_g + t1[0, 0], mix, name="ret_fwd")
    finish_forward(1, mix)
    x1, xn2 = fwd_mm_norm(mix, 'w_out', xs, norm2_g, "mm_out")
    memn = _rmsnorm_fwd(mems, norm_mem_g, name="norm_mem_fwd")
    km = fwd_mm(memn, 'xa_wk', D, D, out_dtype=BF16, name="mm_k")
    vm = fwd_mm(memn, 'xa_wv', D, D, out_dtype=BF16, name="mm_v")
    t2 = gathers[2].forward(x1)
    q = fwd_mm(xn2, 'xa_wq', D, D, out_dtype=BF16, after=t2, name="mm_q")
    o = _xattn_fwd(q, km, vm, name="xattn_fwd")
    x2, xn3 = fwd_mm_norm(o, 'xa_wo', x1, norm3_g, "mm_o")
    finish_forward(2, xn3)
    t3 = gathers[3].forward(xn3)
    ffn_cw = G['ffn_conv_w']
    act, hh_a, hh_b, hc_a, hc_b = _ffn_up_gate(xn3, G['ffn_w_up'], ffn_cw, ffn_cb + t3[0, 0], name="ffn_up_gate")
    finish_forward(3, act)
    x3 = fwd_mm(act, 'ffn_w_down', D, F, out_dtype=F32, res=x2, name="mm_down")
    dx3, d_final, loss8, dx3h = _final_loss(x3, tgt, final_g, name="final_loss")

    gw = {}
    grad_groups = []

    def start_grads(names, tag):
        gg = _GradGather([gw[n] for n in names], [BIG_AXIS[n] for n in names], tag)
        grad_groups.append((names, gg))
        return gg.start()

    dhh_a, dhh_b, gcw_a, gcw_b, gw['ffn_w_down'], gw_up_a, gw_up_b = _ffn_bwd(
        dx3h, G['ffn_w_down'], hh_a, hh_b, hc_a, hc_b, act, xn3, ffn_cw, name="ffn_bwd")
    gw['ffn_w_up'] = jnp.concatenate([gw_up_a, gw_up_b], axis=1)
    tok_a = start_grads(('ffn_w_down', 'ffn_w_up'), "a")
    dxn3 = bwd_x_mm(dhh_a, 'ffn_w_up', D, F, out_dtype=F32, after=tok_a, name="mm_dxn3_a")
    dxn3 = bwd_x_mm(dhh_b, 'ffn_w_up', D, F, out_dtype=F32, b_off=(0, F), res=dxn3, name="mm_dxn3_b")
    dx2, d_norm3, dx2h = _rmsnorm_bwd(x2, dxn3, norm3_g, dx3, name="norm3_bwd", emit_bf16=True)
    Kc = ffn_cw.shape[0]
    d_ffn_cw = jnp.concatenate([gcw_a[:Kc], gcw_b[:Kc]], axis=1)
    d_ffn_cb = jnp.concatenate([gcw_a[Kc:Kc + 1], gcw_b[Kc:Kc + 1]], axis=1)

    d_o = bwd_x_mm(dx2h, 'xa_wo', D, D, out_dtype=BF16, name="mm_do")
    gw['xa_wo'] = bwd_w_mm(o, dx2h, D, D, name="mm_dw_o")
    dq, dk, dv = _xattn_bwd(q, km, vm, d_o, name="xattn_bwd")
    gw['xa_wq'] = bwd_w_mm(xn2, dq, D, D, name="mm_dw_q")
    dxn2 = bwd_x_mm(dq, 'xa_wq', D, D, out_dtype=F32, name="mm_dxn2")
    gw['xa_wk'] = bwd_w_mm(memn, dk, D, D, name="mm_dw_k")
    gw['xa_wv'] = bwd_w_mm(memn, dv, D, D, name="mm_dw_v")
    dmemn = bwd_x_mm(dk, 'xa_wk', D, D, out_dtype=F32, name="mm_dmem_k")
    dmemn = bwd_x_mm(dv, 'xa_wv', D, D, out_dtype=F32, res=dmemn, name="mm_dmem_v")
    _, d_norm_mem = _rmsnorm_bwd(mems, dmemn, norm_mem_g, None, name="norm_mem_bwd")
    dx1, d_norm2, dx1h = _rmsnorm_bwd(x1, dxn2, norm2_g, dx2, name="norm2_bwd", emit_bf16=True)

    gw['w_out'] = bwd_w_mm(mix, dx1h, D, D, name="mm_dw_out")
    tok_b = start_grads(('xa_wo', 'xa_wq', 'xa_wk', 'xa_wv', 'w_out'), "b")
    dmix = bwd_x_mm(dx1h, 'w_out', D, D, out_dtype=F32, after=tok_b, name="mm_dmix")
    dh, d_ret_g = _ret_bwd(h, cos, sin, ret_g, states, ret_raw, dmix, name="ret_bwd")
    dh, d_rcw, d_rcb, d_wa, d_ba, d_wx, d_bx, d_lam = _lru_bwd(
        h, hl, dmix, dh, rg_cw, rg_cb, wa, ba, wx, bx, lam, name="lru_bwd")
    gw['w_in'] = bwd_w_mm(xn1, dh, D, IN, name="mm_dw_in")
    tok_c = start_grads(('w_in',), "c")
    dxn1 = bwd_x_mm(dh, 'w_in', D, IN, out_dtype=F32, after=tok_c, name="mm_dxn1")
    grad_x, d_norm1 = _rmsnorm_bwd(xs, dxn1, norm1_g, dx1, name="norm1_bwd")

    small_parts = {
        'norm1_g': d_norm1, 'ret_g': d_ret_g, 'rg_conv_w': d_rcw[:rg_cw.shape[0]], 'rg_conv_b': d_rcb,
        'rg_wa': d_wa, 'rg_ba': d_ba, 'rg_wx': d_wx, 'rg_bx': d_bx, 'rg_lambda': d_lam, 'norm2_g': d_norm2,
        'norm_mem_g': d_norm_mem, 'norm3_g': d_norm3, 'ffn_conv_w': d_ffn_cw, 'ffn_conv_b': d_ffn_cb,
        'final_g': d_final}
    small = [n for n in WEIGHT_NAMES if n not in BIG_AXIS]
    red_shapes = [(1,)] + [tuple(small_parts[n].shape) for n in small]
    fwd_tok = sum(gg.forward(d_norm1)[0:1, 0:1] for _, gg in grad_groups)
    reduced = _allreduce_small(_pack([loss8[0:1, 0:1] + fwd_tok] + [small_parts[n] for n in small], SUB),
                               name="allreduce_small")
    red = _unpack(reduced, red_shapes)
    loss = red[0][0]
    g_small = dict(zip(small, red[1:]))
    for n in SMALL_SHARDED:
        w_local = W[n].shape[-1]
        g_small[n] = lax.dynamic_slice_in_dim(g_small[n], chip * w_local, w_local, axis=1)

    out_g, out_d, out_m, out_v = {}, {}, {}, {}
    rows = 512
    pk = lambda d: _pack([d[n] for n in small], rows)
    g_pack = _pack([g_small[n] for n in small], rows)
    res_small = _adamw(pk(W), pk(Mo), pk(Vo), g_pack[None], name="adamw_small")
    shapes_small = [tuple(W[n].shape) for n in small]
    for dst, packed in zip((out_g, out_d, out_m, out_v), res_small):
        for n, val in zip(small, _unpack(packed, shapes_small)):
            dst[n] = val
    last = res_small[0]
    for names, gg in grad_groups:
        for n, land in zip(names, gg.finish(last)):
            g, d, m_new, v_new = _adamw(shards[n], Mo[n][0], Vo[n][0], land, name="adamw_" + n)
            out_g[n], out_d[n], out_m[n], out_v[n] = (t.reshape(W[n].shape) for t in (g, d, m_new, v_new))
            last = g
    return (loss, grad_x[None], *[out_g[n] for n in WEIGHT_NAMES], *[out_d[n] for n in WEIGHT_NAMES],
            *[out_m[n] for n in WEIGHT_NAMES], *[out_v[n] for n in WEIGHT_NAMES])


def kernel(x, mem, positions, norm1_g, w_in, ret_g, rg_conv_w, rg_conv_b, rg_wa, rg_ba, rg_wx, rg_bx, rg_lambda, w_out, norm2_g, norm_mem_g, xa_wq, xa_wk, xa_wv, xa_wo, norm3_g, ffn_w_up, ffn_conv_w, ffn_conv_b, ffn_w_down, final_g, loss_target, m_norm1_g, m_w_in, m_ret_g, m_rg_conv_w, m_rg_conv_b, m_rg_wa, m_rg_ba, m_rg_wx, m_rg_bx, m_rg_lambda, m_w_out, m_norm2_g, m_norm_mem_g, m_xa_wq, m_xa_wk, m_xa_wv, m_xa_wo, m_norm3_g, m_ffn_w_up, m_ffn_conv_w, m_ffn_conv_b, m_ffn_w_down, m_final_g, v_norm1_g, v_w_in, v_ret_g, v_rg_conv_w, v_rg_conv_b, v_rg_wa, v_rg_ba, v_rg_wx, v_rg_bx, v_rg_lambda, v_w_out, v_norm2_g, v_norm_mem_g, v_xa_wq, v_xa_wk, v_xa_wv, v_xa_wo, v_norm3_g, v_ffn_w_up, v_ffn_conv_w, v_ffn_conv_b, v_ffn_w_down, v_final_g):
    W = dict(zip(WEIGHT_NAMES, (norm1_g, w_in, ret_g, rg_conv_w, rg_conv_b, rg_wa, rg_ba, rg_wx, rg_bx, rg_lambda, w_out,
                                norm2_g, norm_mem_g, xa_wq, xa_wk, xa_wv, xa_wo, norm3_g, ffn_w_up, ffn_conv_w,
                                ffn_conv_b, ffn_w_down, final_g)))
    Mo = dict(zip(WEIGHT_NAMES, (m_norm1_g, m_w_in, m_ret_g, m_rg_conv_w, m_rg_conv_b, m_rg_wa, m_rg_ba, m_rg_wx, m_rg_bx,
                                 m_rg_lambda, m_w_out, m_norm2_g, m_norm_mem_g, m_xa_wq, m_xa_wk, m_xa_wv, m_xa_wo,
                                 m_norm3_g, m_ffn_w_up, m_ffn_conv_w, m_ffn_conv_b, m_ffn_w_down, m_final_g)))
    Vo = dict(zip(WEIGHT_NAMES, (v_norm1_g, v_w_in, v_ret_g, v_rg_conv_w, v_rg_conv_b, v_rg_wa, v_rg_ba, v_rg_wx, v_rg_bx,
                                 v_rg_lambda, v_w_out, v_norm2_g, v_norm_mem_g, v_xa_wq, v_xa_wk, v_xa_wv, v_xa_wo,
                                 v_norm3_g, v_ffn_w_up, v_ffn_conv_w, v_ffn_conv_b, v_ffn_w_down, v_final_g)))
    return _step(x, mem, positions, loss_target, W, Mo, Vo)
```

```python
import math

import jax
import jax.numpy as jnp
from jax import lax
from jax.experimental import pallas as pl
from jax.experimental.pallas import tpu as pltpu

F32 = jnp.float32
BF16 = jnp.bfloat16

EPS = 1e-6
RET_HEADS = 4
RET_CHUNK = 128
ROPE_BASE = 10000.0
LRU_BLOCKS = 8
LRU_C = 8.0
XA_HEADS = 4

ADAM_LR = 0.001
ADAM_B1 = 0.9
ADAM_B2 = 0.999
ADAM_EPS = 1e-08
ADAM_WD = 0.01
ADAM_STEP = 10

N_DEV = 8
N_CHIP = 4
MESH = pl.DeviceIdType.MESH
SUB = 8
LANE = 128
VMEM_LIMIT = 56 * 1024 * 1024

NN = ((1,), (0,))
NT = ((1,), (1,))
TN = ((0,), (0,))


def _cparams(sem):
    return pltpu.CompilerParams(dimension_semantics=sem, vmem_limit_bytes=VMEM_LIMIT)


def _sigmoid(v):
    return 1.0 / (1.0 + jnp.exp(-v))


def _bdot(a, b, dims):
    return lax.dot_general(a.astype(BF16), b.astype(BF16), (dims, ((), ())), preferred_element_type=F32)


def _row_iota(shape):
    return lax.broadcasted_iota(jnp.int32, shape, 0)


def _shift_down(v, tail, k):
    if k == 0:
        return v
    r = pltpu.roll(v, k, 0)
    rt = pltpu.roll(tail, k, 0)
    first = jnp.where(_row_iota(rt.shape) < k, rt, r[0:SUB])
    return jnp.concatenate([first, r[SUB:]], axis=0)


def _shift_up(v, head, k):
    if k == 0:
        return v
    n = v.shape[0]
    r = pltpu.roll(v, n - k, 0)
    rh = pltpu.roll(head, SUB - k, 0)
    last = jnp.where(_row_iota(rh.shape) >= SUB - k, rh, r[n - SUB:n])
    return jnp.concatenate([r[:n - SUB], last], axis=0)


def _mm(a, b, *, mode, M, N, K, out_dtype, name, tm=512, tn=512, tk=512, b_off=(0, 0), res=None, norm_g=None,
        after=None):
    tm, tn, tk = min(tm, M), min(tn, N), min(tk, K)
    assert M % tm == 0 and N % tn == 0 and K % tk == 0, (name, M, N, K, tm, tn, tk)
    nk = K // tk
    if mode == "nn":
        a_blk, b_blk, dims = (tm, tk), (tk, tn), NN
        a_map = lambda i, j, k: (i, k)
        b_map = lambda i, j, k: (k + b_off[0] // tk, j + b_off[1] // tn)
    elif mode == "nt":
        a_blk, b_blk, dims = (tm, tk), (tn, tk), NT
        a_map = lambda i, j, k: (i, k)
        b_map = lambda i, j, k: (j + b_off[0] // tn, k + b_off[1] // tk)
    else:
        a_blk, b_blk, dims = (tk, tm), (tk, tn), TN
        a_map = lambda i, j, k: (k, i)
        b_map = lambda i, j, k: (k + b_off[0] // tk, j + b_off[1] // tn)
    assert b_off[0] % b_blk[0] == 0 and b_off[1] % b_blk[1] == 0, (name, b_off, b_blk)
    has_res, has_norm, has_after = res is not None, norm_g is not None, after is not None
    assert not has_norm or tn == N

    def body(*refs):
        refs = list(refs)
        a_ref, b_ref = refs[0], refs[1]
        pos = 2
        r_ref = g_ref = n_ref = None
        if has_res:
            r_ref = refs[pos]
            pos += 1
        if has_norm:
            g_ref = refs[pos]
            pos += 1
        pos += has_after
        o_ref = refs[pos]
        pos += 1
        if has_norm:
            n_ref = refs[pos]
            pos += 1
        acc = refs[pos] if nk > 1 else None
        k = pl.program_id(2)
        part = _bdot(a_ref[...], b_ref[...], dims)

        def finish(total):
            if has_res:
                total = total + r_ref[...].astype(F32)
            o_ref[...] = total.astype(o_ref.dtype)
            if has_norm:
                r = lax.rsqrt(jnp.mean(total * total, axis=-1, keepdims=True) + EPS)
                n_ref[...] = (total * r * g_ref[...]).astype(n_ref.dtype)

        if nk == 1:
            finish(part)
        else:
            @pl.when(k == 0)
            def _():
                acc[...] = part

            @pl.when(k > 0)
            def _():
                acc[...] += part

            @pl.when(k == nk - 1)
            def _():
                finish(acc[...])

    in_specs = [pl.BlockSpec(a_blk, a_map), pl.BlockSpec(b_blk, b_map)]
    args = [a, b]
    if has_res:
        in_specs.append(pl.BlockSpec((tm, tn), lambda i, j, k: (i, j)))
        args.append(res)
    if has_norm:
        in_specs.append(pl.BlockSpec((1, N), lambda i, j, k: (0, 0)))
        args.append(norm_g)
    if has_after:
        in_specs.append(pl.BlockSpec(memory_space=pl.ANY))
        args.append(after)
    out_shape = jax.ShapeDtypeStruct((M, N), out_dtype)
    out_specs = pl.BlockSpec((tm, tn), lambda i, j, k: (i, j))
    if has_norm:
        out_shape = [out_shape, jax.ShapeDtypeStruct((M, N), BF16)]
        out_specs = [out_specs, pl.BlockSpec((tm, tn), lambda i, j, k: (i, j))]
    return pl.pallas_call(
        body, name=name, grid=(M // tm, N // tn, nk), in_specs=in_specs,
        out_specs=out_specs, out_shape=out_shape,
        scratch_shapes=[pltpu.VMEM((tm, tn), F32)] if nk > 1 else [],
        compiler_params=_cparams(("parallel", "parallel", "arbitrary")),
    )(*args)


def _rmsnorm_fwd(x, g, *, name, ts=512):
    S, D = x.shape
    ts = min(ts, S)

    def body(x_ref, g_ref, o_ref):
        xv = x_ref[...]
        r = lax.rsqrt(jnp.mean(xv * xv, axis=-1, keepdims=True) + EPS)
        o_ref[...] = (xv * r * g_ref[...]).astype(o_ref.dtype)

    return pl.pallas_call(
        body, name=name, grid=(S // ts,),
        in_specs=[pl.BlockSpec((ts, D), lambda i: (i, 0)), pl.BlockSpec((1, D), lambda i: (0, 0))],
        out_specs=pl.BlockSpec((ts, D), lambda i: (i, 0)),
        out_shape=jax.ShapeDtypeStruct((S, D), BF16),
        compiler_params=_cparams(("parallel",)),
    )(x, g)


def _rmsnorm_bwd(x, dxn, g, res, *, name, ts=512, emit_bf16=False):
    S, D = x.shape
    ts = min(ts, S)
    has_res = res is not None

    def body(*refs):
        refs = list(refs)
        dx16_ref = refs.pop() if emit_bf16 else None
        if has_res:
            x_ref, d_ref, g_ref, r_ref, dx_ref, dg_ref = refs
        else:
            x_ref, d_ref, g_ref, dx_ref, dg_ref = refs
        i = pl.program_id(0)
        xv = x_ref[...]
        dv = d_ref[...].astype(F32)
        r = lax.rsqrt(jnp.mean(xv * xv, axis=-1, keepdims=True) + EPS)
        gd = dv * g_ref[...]
        proj = jnp.mean(xv * gd, axis=-1, keepdims=True)
        dx = r * gd - xv * (r * r * r) * proj
        if has_res:
            dx = dx + r_ref[...]
        dx_ref[...] = dx
        if emit_bf16:
            dx16_ref[...] = dx.astype(BF16)
        part = jnp.sum(dv * xv * r, axis=0, keepdims=True)

        @pl.when(i == 0)
        def _():
            dg_ref[...] = part

        @pl.when(i > 0)
        def _():
            dg_ref[...] += part

    row = pl.BlockSpec((ts, D), lambda i: (i, 0))
    vec = pl.BlockSpec((1, D), lambda i: (0, 0))
    in_specs = [row, row, vec] + ([row] if has_res else [])
    args = [x, dxn, g] + ([res] if has_res else [])
    extra = emit_bf16 * [jax.ShapeDtypeStruct((S, D), BF16)]
    return pl.pallas_call(
        body, name=name, grid=(S // ts,), in_specs=in_specs, out_specs=[row, vec] + emit_bf16 * [row],
        out_shape=[jax.ShapeDtypeStruct((S, D), F32), jax.ShapeDtypeStruct((1, D), F32)] + extra,
        compiler_params=_cparams(("arbitrary",)),
    )(*args)


def _final_loss(x, target, g, *, name, ts=512):
    S, D = x.shape
    ts = min(ts, S)

    def body(x_ref, t_ref, g_ref, dx_ref, dg_ref, loss_ref, dx16_ref):
        i = pl.program_id(0)
        xv = x_ref[...]
        gv = g_ref[...]
        r = lax.rsqrt(jnp.mean(xv * xv, axis=-1, keepdims=True) + EPS)
        y = xv * r * gv
        err = y - t_ref[...]
        row_loss = jnp.mean(err * err, axis=-1, keepdims=True)
        lpart = 0.5 * jnp.sum(row_loss, axis=0, keepdims=True)
        dy = err * (1.0 / D)
        gd = dy * gv
        proj = jnp.mean(xv * gd, axis=-1, keepdims=True)
        dx = r * gd - xv * (r * r * r) * proj
        dx_ref[...] = dx
        dx16_ref[...] = dx.astype(BF16)
        part = jnp.sum(dy * xv * r, axis=0, keepdims=True)
        lfull = jnp.broadcast_to(lpart, loss_ref.shape)

        @pl.when(i == 0)
        def _():
            dg_ref[...] = part
            loss_ref[...] = lfull

        @pl.when(i > 0)
        def _():
            dg_ref[...] += part
            loss_ref[...] += lfull

    row = pl.BlockSpec((ts, D), lambda i: (i, 0))
    vec = pl.BlockSpec((1, D), lambda i: (0, 0))
    return pl.pallas_call(
        body, name=name, grid=(S // ts,), in_specs=[row, row, vec],
        out_specs=[row, vec, pl.BlockSpec((SUB, LANE), lambda i: (0, 0)), row],
        out_shape=[jax.ShapeDtypeStruct((S, D), F32), jax.ShapeDtypeStruct((1, D), F32),
                   jax.ShapeDtypeStruct((SUB, LANE), F32), jax.ShapeDtypeStruct((S, D), BF16)],
        compiler_params=_cparams(("arbitrary",)),
    )(x, target, g)


def _rope_table(pos_col, inv, *, name, ts=1024):
    S = pos_col.shape[0]
    ts = min(ts, S)
    half = inv.shape[1]

    def body(p_ref, inv_ref, c_ref, s_ref):
        ang = p_ref[...].astype(F32) * inv_ref[...]
        c_ref[...] = jnp.cos(ang)
        s_ref[...] = jnp.sin(ang)

    tab = pl.BlockSpec((ts, half), lambda i: (i, 0))
    return pl.pallas_call(
        body, name=name, grid=(S // ts,),
        in_specs=[pl.BlockSpec((ts, 1), lambda i: (i, 0)), pl.BlockSpec((1, half), lambda i: (0, 0))],
        out_specs=[tab, tab],
        out_shape=[jax.ShapeDtypeStruct((S, half), F32), jax.ShapeDtypeStruct((S, half), F32)],
        compiler_params=_cparams(("parallel",)),
    )(pos_col, inv)


def _ret_consts(C, log_g):
    ii = lax.broadcasted_iota(jnp.int32, (C, C), 0)
    jj = lax.broadcasted_iota(jnp.int32, (C, C), 1)
    diff = (ii - jj).astype(F32)
    intra = jnp.where(ii >= jj, jnp.exp(log_g * jnp.maximum(diff, 0.0)), 0.0)
    idx = lax.broadcasted_iota(jnp.int32, (C, 1), 0).astype(F32)
    qd = jnp.exp(log_g * (idx + 1.0))
    kd = jnp.exp(log_g * (C - 1.0 - idx))
    cd = math.exp(log_g * C)
    return intra, qd, kd, cd


def _rot(t, cs, sn):
    half = t.shape[-1] // 2
    t1, t2 = t[:, :half], t[:, half:]
    return jnp.concatenate([t1 * cs - t2 * sn, t1 * sn + t2 * cs], axis=-1)


def _unrot(d, cs, sn):
    half = d.shape[-1] // 2
    d1, d2 = d[:, :half], d[:, half:]
    return jnp.concatenate([d1 * cs + d2 * sn, d2 * cs - d1 * sn], axis=-1)


def _ret_fwd(h, cos, sin, ret_g, mix, *, name, ch=2):
    S = h.shape[0]
    R = ret_g.shape[1]
    H, C = RET_HEADS, RET_CHUNK
    Dh = R // H
    ts = ch * C
    assert S % ts == 0
    log_gs = [math.log(1.0 - 2.0 ** (-5.0 - hd)) for hd in range(H)]
    scale = Dh ** -0.5

    def body(x_ref, c_ref, s_ref, rg_ref, mix_in, ret_ref, st_ref, mix_ref, state):
        i = pl.program_id(0)

        @pl.when(i == 0)
        def _():
            state[...] = jnp.zeros_like(state)

        for c in range(ch):
            rows = pl.ds(c * C, C)
            cs, sn = c_ref[rows, :], s_ref[rows, :]
            for hd in range(H):
                intra, qd, kd, cd = _ret_consts(C, log_gs[hd])
                q = x_ref[rows, pl.ds(hd * Dh, Dh)]
                k = x_ref[rows, pl.ds(R + hd * Dh, Dh)]
                v = x_ref[rows, pl.ds(2 * R + hd * Dh, Dh)]
                g = x_ref[rows, pl.ds(3 * R + hd * Dh, Dh)]
                rq = _rot(q, cs, sn)
                rk = _rot(k, cs, sn) * scale
                st = state[hd]
                st_ref[c, hd] = st.astype(BF16)
                s_ = _bdot(rq, rk, NT) * intra
                ret = _bdot(s_, v, NN) + _bdot(rq * qd, st, NN)
                state[hd] = st * cd + _bdot(rk * kd, v, TN)
                ret_ref[rows, pl.ds(hd * Dh, Dh)] = ret
                rr = lax.rsqrt(jnp.mean(ret * ret, axis=-1, keepdims=True) + EPS)
                out = ret * rr * rg_ref[:, pl.ds(hd * Dh, Dh)] * (g * _sigmoid(g))
                mix_ref[rows, pl.ds(hd * Dh, Dh)] = out.astype(BF16)

    n_chunks = S // C
    return pl.pallas_call(
        body, name=name, grid=(S // ts,),
        in_specs=[pl.BlockSpec((ts, 4 * R), lambda i: (i, 0)),
                  pl.BlockSpec((ts, Dh // 2), lambda i: (i, 0)), pl.BlockSpec((ts, Dh // 2), lambda i: (i, 0)),
                  pl.BlockSpec((1, R), lambda i: (0, 0)), pl.BlockSpec(memory_space=pl.ANY)],
        out_specs=[pl.BlockSpec((ts, R), lambda i: (i, 0)),
                   pl.BlockSpec((ch, H, Dh, Dh), lambda i: (i, 0, 0, 0)),
                   pl.BlockSpec((ts, R), lambda i: (i, 0))],
        out_shape=[jax.ShapeDtypeStruct((S, R), F32), jax.ShapeDtypeStruct((n_chunks, H, Dh, Dh), BF16),
                   jax.ShapeDtypeStruct(mix.shape, mix.dtype)],
        scratch_shapes=[pltpu.VMEM((H, Dh, Dh), F32)],
        input_output_aliases={4: 2},
        compiler_params=_cparams(("arbitrary",)),
    )(h, cos, sin, ret_g, mix)


def _ret_bwd(h, cos, sin, ret_g, states, ret_raw, dmix, *, name, ch=2):
    S = h.shape[0]
    R = ret_g.shape[1]
    H, C = RET_HEADS, RET_CHUNK
    Dh = R // H
    ts = ch * C
    nb = S // ts
    log_gs = [math.log(1.0 - 2.0 ** (-5.0 - hd)) for hd in range(H)]
    scale = Dh ** -0.5

    def body(x_ref, c_ref, s_ref, rg_ref, st_ref, ret_ref, dm_ref, dh_ref, drg_ref, dstate):
        i = pl.program_id(0)

        @pl.when(i == 0)
        def _():
            dstate[...] = jnp.zeros_like(dstate)
            drg_ref[...] = jnp.zeros_like(drg_ref)

        for c in reversed(range(ch)):
            rows = pl.ds(c * C, C)
            cs, sn = c_ref[rows, :], s_ref[rows, :]
            for hd in range(H):
                intra, qd, kd, cd = _ret_consts(C, log_gs[hd])
                cols = pl.ds(hd * Dh, Dh)
                q = x_ref[rows, pl.ds(hd * Dh, Dh)]
                k = x_ref[rows, pl.ds(R + hd * Dh, Dh)]
                v = x_ref[rows, pl.ds(2 * R + hd * Dh, Dh)]
                g = x_ref[rows, pl.ds(3 * R + hd * Dh, Dh)]
                rq = _rot(q, cs, sn)
                rk = _rot(k, cs, sn) * scale
                ret = ret_ref[rows, cols]
                dm = dm_ref[rows, cols]
                rgv = rg_ref[:, cols]
                rr = lax.rsqrt(jnp.mean(ret * ret, axis=-1, keepdims=True) + EPS)
                retn = ret * rr
                sg = _sigmoid(g)
                silu = g * sg
                drg_ref[:, cols] += jnp.sum(dm * retn * silu, axis=0, keepdims=True)
                dg = dm * retn * rgv * (sg * (1.0 + g * (1.0 - sg)))
                dretn = dm * rgv * silu
                d_o = rr * dretn - ret * (rr * rr * rr) * jnp.mean(ret * dretn, axis=-1, keepdims=True)
                st = st_ref[c, hd]
                d_s = dstate[hd]
                a_ = _bdot(rq, rk, NT) * intra
                d_a = _bdot(d_o, v, NT) * intra
                d_qr = _bdot(d_a, rk, NN) + _bdot(d_o, st, NT) * qd
                d_kr = _bdot(d_a, rq, TN) + _bdot(v, d_s, NT) * kd
                d_v = _bdot(a_, d_o, TN) + _bdot(rk * kd, d_s, NN)
                dstate[hd] = d_s * cd + _bdot(rq * qd, d_o, TN)
                dh_ref[rows, pl.ds(hd * Dh, Dh)] = _unrot(d_qr, cs, sn).astype(BF16)
                dh_ref[rows, pl.ds(R + hd * Dh, Dh)] = (_unrot(d_kr, cs, sn) * scale).astype(BF16)
                dh_ref[rows, pl.ds(2 * R + hd * Dh, Dh)] = d_v.astype(BF16)
                dh_ref[rows, pl.ds(3 * R + hd * Dh, Dh)] = dg.astype(BF16)

    rb = lambda i: nb - 1 - i
    return pl.pallas_call(
        body, name=name, grid=(nb,),
        in_specs=[pl.BlockSpec((ts, 4 * R), lambda i: (rb(i), 0)),
                  pl.BlockSpec((ts, Dh // 2), lambda i: (rb(i), 0)), pl.BlockSpec((ts, Dh // 2), lambda i: (rb(i), 0)),
                  pl.BlockSpec((1, R), lambda i: (0, 0)),
                  pl.BlockSpec((ch, H, Dh, Dh), lambda i: (rb(i), 0, 0, 0)),
                  pl.BlockSpec((ts, R), lambda i: (rb(i), 0)),
                  pl.BlockSpec((ts, R), lambda i: (rb(i), 0))],
        out_specs=[pl.BlockSpec((ts, 4 * R), lambda i: (rb(i), 0)), pl.BlockSpec((1, R), lambda i: (0, 0))],
        out_shape=[jax.ShapeDtypeStruct((S, 6 * R), BF16), jax.ShapeDtypeStruct((1, R), F32)],
        scratch_shapes=[pltpu.VMEM((H, Dh, Dh), F32)],
        compiler_params=_cparams(("arbitrary",)),
    )(h, cos, sin, ret_g, states, ret_raw, dmix)


GELU_C = math.sqrt(2.0 / math.pi)
GELU_A = 0.044715


def _gelu_parts(y):
    t = jnp.tanh(GELU_C * (y + GELU_A * y * y * y))
    val = 0.5 * y * (1.0 + t)
    grad = 0.5 * (1.0 + t) + 0.5 * y * (1.0 - t * t) * GELU_C * (1.0 + 3.0 * GELU_A * y * y)
    return val, grad


def _neg_expm1(x):
    series = -x * (1.0 + x * (1.0 / 2.0) * (1.0 + x * (1.0 / 3.0) * (1.0 + x * (1.0 / 4.0) * (
        1.0 + x * (1.0 / 5.0) * (1.0 + x * (1.0 / 6.0) * (1.0 + x * (1.0 / 7.0)))))))
    return jnp.where(x > -0.35, series, 1.0 - jnp.exp(x))


def _log_sigmoid(x):
    return jnp.minimum(x, 0.0) - jnp.log1p(jnp.exp(-jnp.abs(x)))


def _lru_gates(uc, wa_ref, ba_ref, wx_ref, bx_ref):
    nbk = wa_ref.shape[0]
    bd = wa_ref.shape[1]
    rs, gs = [], []
    for n in range(nbk):
        ucn = uc[:, n * bd:(n + 1) * bd]
        rs.append(_sigmoid(_bdot(ucn, wa_ref[n], NN) + ba_ref[:, pl.ds(n * bd, bd)]))
        gs.append(_sigmoid(_bdot(ucn, wx_ref[n], NN) + bx_ref[:, pl.ds(n * bd, bd)]))
    return jnp.concatenate(rs, axis=-1), jnp.concatenate(gs, axis=-1)


def _lru_fwd(h, conv_w, conv_b, wa, ba, wx, bx, lam, *, name, ts=256):
    S = h.shape[0]
    W = lam.shape[1]
    K = conv_w.shape[0]
    ts = min(ts, S)

    def body(u_ref, y_ref, cw_ref, cb_ref, wa_ref, ba_ref, wx_ref, bx_ref, lam_ref, hl_ref, mix_ref, tail, hlast):
        i = pl.program_id(0)

        @pl.when(i == 0)
        def _():
            tail[...] = jnp.zeros_like(tail)
            hlast[...] = jnp.zeros_like(hlast)

        u = u_ref[...]
        tl = tail[...]
        uc = cb_ref[...] + cw_ref[K - 1:K, :] * u
        for k in range(K - 1):
            uc = uc + cw_ref[k:k + 1, :] * _shift_down(u, tl, K - 1 - k)
        tail[...] = u[ts - SUB:ts]
        r, ig = _lru_gates(uc, wa_ref, ba_ref, wx_ref, bx_ref)
        log_a = LRU_C * r * _log_sigmoid(lam_ref[...])
        a = jnp.exp(log_a)
        b = jnp.sqrt(_neg_expm1(2.0 * log_a)) * (ig * uc)
        in_tile = _row_iota((ts, W)) & (SUB - 1)
        d = 1
        while d < SUB:
            a_s = jnp.where(in_tile < d, 1.0, pltpu.roll(a, d, 0))
            b_s = jnp.where(in_tile < d, 0.0, pltpu.roll(b, d, 0))
            b = a * b_s + b
            a = a * a_s
            d *= 2
        before = hlast[SUB - 1:SUB, :]
        for k in range(ts // SUB):
            tile = slice(k * SUB, (k + 1) * SUB)
            h_tile = a[tile] * before + b[tile]
            hl_ref[tile, :] = h_tile
            before = h_tile[SUB - 1:SUB, :]
        hlast[...] = hl_ref[ts - SUB:ts, :]
        gy, _ = _gelu_parts(y_ref[...])
        mix_ref[...] = (hl_ref[...] * gy).astype(BF16)

    full = lambda shape: pl.BlockSpec(shape, lambda i: tuple(0 for _ in shape))
    return pl.pallas_call(
        body, name=name, grid=(S // ts,),
        in_specs=[pl.BlockSpec((ts, W), lambda i: (i, 4)), pl.BlockSpec((ts, W), lambda i: (i, 5)),
                  full(conv_w.shape), full(conv_b.shape), full(wa.shape), full(ba.shape), full(wx.shape),
                  full(bx.shape), full(lam.shape)],
        out_specs=[pl.BlockSpec((ts, W), lambda i: (i, 0)), pl.BlockSpec((ts, W), lambda i: (i, 1))],
        out_shape=[jax.ShapeDtypeStruct((S, W), F32), jax.ShapeDtypeStruct((S, 2 * W), BF16)],
        scratch_shapes=[pltpu.VMEM((SUB, W), F32), pltpu.VMEM((SUB, W), F32)],
        compiler_params=_cparams(("arbitrary",)),
    )(h, h, conv_w, conv_b, wa, ba, wx, bx, lam)


def _lru_bwd(h, hl, dmix, dh, conv_w, conv_b, wa, ba, wx, bx, lam, *, name, ts=256):
    S = h.shape[0]
    W = lam.shape[1]
    K = conv_w.shape[0]
    nbk, bd = wa.shape[0], wa.shape[1]
    ts = min(ts, S)
    nb = S // ts
    t8 = ts // SUB

    def body(u_ref, y_ref, uh_ref, hl_ref, hh_ref, dm_ref, cw_ref, cb_ref, wa_ref, ba_ref, wx_ref, bx_ref, lam_ref,
             dh_in, dh_ref, dcw_ref, dcb_ref, dwa_ref, dba_ref, dwx_ref, dbx_ref, dlam_ref, carry, head, lam_buf):
        i = pl.program_id(0)
        blk = nb - 1 - i

        @pl.when(i == 0)
        def _():
            carry[...] = jnp.zeros_like(carry)
            head[...] = jnp.zeros_like(head)
            for ref in (dcw_ref, dcb_ref, dwa_ref, dba_ref, dwx_ref, dbx_ref, dlam_ref):
                ref[...] = jnp.zeros_like(ref)

        inside = (blk > 0).astype(F32)
        u = u_ref[...]
        tl = uh_ref[...] * inside
        sh = [_shift_down(u, tl, K - 1 - k) for k in range(K)]
        uc = cb_ref[...]
        for k in range(K):
            uc = uc + cw_ref[k:k + 1, :] * sh[k]
        r, ig = _lru_gates(uc, wa_ref, ba_ref, wx_ref, bx_ref)
        lam_v = lam_ref[...]
        ls = _log_sigmoid(lam_v)
        log_a = LRU_C * r * ls
        a = jnp.exp(log_a)
        mult = jnp.sqrt(_neg_expm1(2.0 * log_a))
        hcur = hl_ref[...]
        hprev = _shift_down(hcur, hh_ref[...] * inside, 1)
        gy, dgy = _gelu_parts(y_ref[...])
        dm = dm_ref[...]
        d_y = dm * hcur * dgy
        rid = _row_iota((ts, W))
        bq = dm * gy + jnp.where(rid == ts - 1, carry[0:1, :], 0.0)
        aq = jnp.where(rid == ts - 1, 0.0, pltpu.roll(a, ts - 1, 0))
        in_tile = rid & (SUB - 1)
        d = 1
        while d < SUB:
            a_s = jnp.where(in_tile >= SUB - d, 1.0, pltpu.roll(aq, ts - d, 0))
            b_s = jnp.where(in_tile >= SUB - d, 0.0, pltpu.roll(bq, ts - d, 0))
            bq = bq + aq * b_s
            aq = aq * a_s
            d *= 2
        after_row = jnp.zeros((1, W), F32)
        for k in reversed(range(ts // SUB)):
            tile = slice(k * SUB, (k + 1) * SUB)
            lam_tile = aq[tile] * after_row + bq[tile]
            lam_buf[tile, :] = lam_tile
            after_row = lam_tile[0:1, :]
        lam_t = lam_buf[...]
        carry[...] = (a * lam_t)[0:SUB]
        d_a = lam_t * hprev
        d_mult = lam_t * (ig * uc)
        d_i = lam_t * mult * uc
        d_uc = lam_t * mult * ig
        d_log_a = d_a * a - d_mult * (a * a) / mult
        d_r = d_log_a * (LRU_C * ls)
        dlam_ref[...] += jnp.sum(d_log_a * (LRU_C * r), axis=0, keepdims=True) * _sigmoid(-lam_v)
        d_pr = d_r * r * (1.0 - r)
        d_pi = d_i * ig * (1.0 - ig)
        dba_ref[...] += jnp.sum(d_pr, axis=0, keepdims=True)
        dbx_ref[...] += jnp.sum(d_pi, axis=0, keepdims=True)
        extra = []
        for n in range(nbk):
            sl = slice(n * bd, (n + 1) * bd)
            ucn = uc[:, sl]
            dwa_ref[n] += _bdot(ucn, d_pr[:, sl], TN)
            dwx_ref[n] += _bdot(ucn, d_pi[:, sl], TN)
            extra.append(_bdot(d_pr[:, sl], wa_ref[n], NT) + _bdot(d_pi[:, sl], wx_ref[n], NT))
        d_uc = d_uc + jnp.concatenate(extra, axis=-1)
        dcb_ref[...] += jnp.sum(d_uc, axis=0, keepdims=True)
        rid8 = _row_iota((SUB, W))
        dcw = jnp.zeros((SUB, W), F32)
        for k in range(K):
            dcw = dcw + jnp.where(rid8 == k, jnp.sum(d_uc * sh[k], axis=0, keepdims=True), 0.0)
        dcw_ref[...] += dcw
        hd = head[...]
        d_u = cw_ref[K - 1:K, :] * d_uc
        for j in range(1, K):
            d_u = d_u + cw_ref[K - 1 - j:K - j, :] * _shift_up(d_uc, hd, j)
        head[...] = d_uc[0:SUB]
        dh_ref[:, 0:W] = d_u.astype(BF16)
        dh_ref[:, W:2 * W] = d_y.astype(BF16)

    rb = lambda i: nb - 1 - i
    prev8 = lambda i: jnp.maximum(rb(i) * t8 - 1, 0)
    full = lambda shape: pl.BlockSpec(shape, lambda i: tuple(0 for _ in shape))
    small = [jax.ShapeDtypeStruct((SUB, W), F32), jax.ShapeDtypeStruct((1, W), F32),
             jax.ShapeDtypeStruct(wa.shape, F32), jax.ShapeDtypeStruct((1, W), F32),
             jax.ShapeDtypeStruct(wx.shape, F32), jax.ShapeDtypeStruct((1, W), F32),
             jax.ShapeDtypeStruct((1, W), F32)]
    return pl.pallas_call(
        body, name=name, grid=(nb,),
        in_specs=[pl.BlockSpec((ts, W), lambda i: (rb(i), 4)), pl.BlockSpec((ts, W), lambda i: (rb(i), 5)),
                  pl.BlockSpec((SUB, W), lambda i: (prev8(i), 4)),
                  pl.BlockSpec((ts, W), lambda i: (rb(i), 0)), pl.BlockSpec((SUB, W), lambda i: (prev8(i), 0)),
                  pl.BlockSpec((ts, W), lambda i: (rb(i), 1)),
                  full(conv_w.shape), full(conv_b.shape), full(wa.shape), full(ba.shape), full(wx.shape),
                  full(bx.shape), full(lam.shape), pl.BlockSpec(memory_space=pl.ANY)],
        out_specs=[pl.BlockSpec((ts, 2 * W), lambda i: (rb(i), 2))] + [full(s.shape) for s in small],
        out_shape=[jax.ShapeDtypeStruct(dh.shape, dh.dtype)] + small,
        scratch_shapes=[pltpu.VMEM((SUB, W), F32), pltpu.VMEM((SUB, W), F32), pltpu.VMEM((ts, W), F32)],
        input_output_aliases={13: 0},
        compiler_params=_cparams(("arbitrary",)),
    )(h, h, h, hl, hl, dmix, conv_w, conv_b, wa, ba, wx, bx, lam, dh)


def _xattn_fwd(q, km, vm, *, name, ts=512):
    S, D = q.shape
    M = km.shape[0]
    H = XA_HEADS
    Dh = D // H
    ts = min(ts, S)
    scale = Dh ** -0.5

    def body(q_ref, k_ref, v_ref, o_ref):
        for hd in range(H):
            cols = pl.ds(hd * Dh, Dh)
            s = _bdot(q_ref[:, cols], k_ref[:, cols], NT) * scale
            s = s - jnp.max(s, axis=-1, keepdims=True)
            e = jnp.exp(s)
            p = e / jnp.sum(e, axis=-1, keepdims=True)
            o_ref[:, cols] = _bdot(p, v_ref[:, cols], NN).astype(o_ref.dtype)

    return pl.pallas_call(
        body, name=name, grid=(S // ts,),
        in_specs=[pl.BlockSpec((ts, D), lambda i: (i, 0)), pl.BlockSpec((M, D), lambda i: (0, 0)),
                  pl.BlockSpec((M, D), lambda i: (0, 0))],
        out_specs=pl.BlockSpec((ts, D), lambda i: (i, 0)),
        out_shape=jax.ShapeDtypeStruct((S, D), BF16),
        compiler_params=_cparams(("parallel",)),
    )(q, km, vm)


def _xattn_bwd(q, km, vm, d_o, *, name, ts=512):
    S, D = q.shape
    M = km.shape[0]
    H = XA_HEADS
    Dh = D // H
    ts = min(ts, S)
    scale = Dh ** -0.5

    def body(q_ref, k_ref, v_ref, do_ref, dq_ref, dk_ref, dv_ref):
        i = pl.program_id(0)

        @pl.when(i == 0)
        def _():
            dk_ref[...] = jnp.zeros_like(dk_ref)
            dv_ref[...] = jnp.zeros_like(dv_ref)

        for hd in range(H):
            cols = pl.ds(hd * Dh, Dh)
            qh, kh, vh, doh = q_ref[:, cols], k_ref[:, cols], v_ref[:, cols], do_ref[:, cols]
            s = _bdot(qh, kh, NT) * scale
            s = s - jnp.max(s, axis=-1, keepdims=True)
            e = jnp.exp(s)
            p = e / jnp.sum(e, axis=-1, keepdims=True)
            dp = _bdot(doh, vh, NT)
            ds = p * (dp - jnp.sum(dp * p, axis=-1, keepdims=True)) * scale
            dq_ref[:, cols] = _bdot(ds, kh, NN).astype(dq_ref.dtype)
            dk_ref[:, cols] += _bdot(ds, qh, TN)
            dv_ref[:, cols] += _bdot(p, doh, TN)

    row = pl.BlockSpec((ts, D), lambda i: (i, 0))
    mem = pl.BlockSpec((M, D), lambda i: (0, 0))
    return pl.pallas_call(
        body, name=name, grid=(S // ts,), in_specs=[row, mem, mem, row], out_specs=[row, mem, mem],
        out_shape=[jax.ShapeDtypeStruct((S, D), BF16), jax.ShapeDtypeStruct((M, D), F32),
                   jax.ShapeDtypeStruct((M, D), F32)],
        compiler_params=_cparams(("arbitrary",)),
    )(q, km, vm, d_o)


def _conv_rows(v, tail, cw_ref, cb_ref):
    K = cw_ref.shape[0]
    sh = [_shift_down(v, tail, K - 1 - k) for k in range(K)]
    out = cb_ref[...]
    for k in range(K):
        out = out + cw_ref[k:k + 1, :] * sh[k]
    return out, sh


FFN_SUB = 256


def _ffn_up_gate(xn, w_up, cw, cb, *, name, tm=1024, tn=512):
    S, D = xn.shape
    F2 = w_up.shape[1]
    F = F2 // 2
    tm, tn = min(tm, S), min(tn, F)
    sub = min(FFN_SUB, tm)
    nj = F // tn
    K = cw.shape[0]

    def body(x_ref, wa_ref, wb_ref, cwa_ref, cwb_ref, cba_ref, cbb_ref, act_ref, ha_ref, hb_ref, ac_ref, bc_ref, ta, tb):
        i = pl.program_id(1)

        @pl.when(i == 0)
        def _():
            ta[...] = jnp.zeros_like(ta)
            tb[...] = jnp.zeros_like(tb)

        tail_a, tail_b = ta[...], tb[...]
        for s in range(tm // sub):
            rows = pl.ds(s * sub, sub)
            xs = x_ref[rows, :]
            ha = _bdot(xs, wa_ref[...], NN)
            hb = _bdot(xs, wb_ref[...], NN)
            ac, _ = _conv_rows(ha, tail_a, cwa_ref, cba_ref)
            bc, _ = _conv_rows(hb, tail_b, cwb_ref, cbb_ref)
            tail_a, tail_b = ha[sub - SUB:sub], hb[sub - SUB:sub]
            ha_ref[rows, :] = ha
            hb_ref[rows, :] = hb
            ac_ref[rows, :] = ac
            bc_ref[rows, :] = bc
            act_ref[rows, :] = (ac * _sigmoid(ac) * bc).astype(act_ref.dtype)
        ta[...] = tail_a
        tb[...] = tail_b

    blk = pl.BlockSpec((tm, tn), lambda j, i: (i, j))
    return pl.pallas_call(
        body, name=name, grid=(nj, S // tm),
        in_specs=[pl.BlockSpec((tm, D), lambda j, i: (i, 0)),
                  pl.BlockSpec((D, tn), lambda j, i: (0, j)), pl.BlockSpec((D, tn), lambda j, i: (0, j + nj)),
                  pl.BlockSpec((K, tn), lambda j, i: (0, j)), pl.BlockSpec((K, tn), lambda j, i: (0, j + nj)),
                  pl.BlockSpec((1, tn), lambda j, i: (0, j)), pl.BlockSpec((1, tn), lambda j, i: (0, j + nj))],
        out_specs=[blk] * 5,
        out_shape=[jax.ShapeDtypeStruct((S, F), BF16)] + [jax.ShapeDtypeStruct((S, F), F32)] * 4,
        scratch_shapes=[pltpu.VMEM((SUB, tn), F32), pltpu.VMEM((SUB, tn), F32)],
        compiler_params=_cparams(("parallel", "arbitrary")),
    )(xn, w_up, w_up, cw, cw, cb, cb)


def _ffn_bwd(dx, w_down, hh_a, hh_b, c_a, c_b, act, xn, cw, *, name, tm=1024, tn=256):
    S, D = dx.shape
    F = hh_a.shape[1]
    tm, tn = min(tm, S), min(tn, F)
    sub = min(FFN_SUB, tm)
    nj = F // tn
    nb = S // tm
    K = cw.shape[0]

    def body(dx_ref, wd_ref, a_ref, b_ref, ac_ref, bc_ref, act_ref, xn_ref, cwa_ref, cwb_ref,
             da_ref, db_ref, ga_ref, gb_ref, dwd_ref, dwu_ref, ha, hb, acc_d, acc_a, acc_b):
        i = pl.program_id(1)

        @pl.when(i == 0)
        def _():
            for ref in (ha, hb, ga_ref, gb_ref, acc_d, acc_a, acc_b):
                ref[...] = jnp.zeros_like(ref)

        rid8 = _row_iota((SUB, tn))
        heads = [ha[...], hb[...]]
        gsums = [jnp.zeros((SUB, tn), F32), jnp.zeros((SUB, tn), F32)]
        for s in reversed(range(tm // sub)):
            rows = pl.ds(s * sub, sub)
            dv = _bdot(dx_ref[rows, :], wd_ref[...], NT)
            ac, bc = ac_ref[rows, :], bc_ref[rows, :]
            sg = _sigmoid(ac)
            d_bc = dv * ac * sg
            d_ac = dv * bc * sg * (1.0 + ac * (1.0 - sg))
            for which, (d_c, h_ref, cw_ref, o_ref) in enumerate(((d_ac, a_ref, cwa_ref, da_ref),
                                                                 (d_bc, b_ref, cwb_ref, db_ref))):
                ahead = [d_c] + [_shift_up(d_c, heads[which], j) for j in range(1, K)]
                heads[which] = d_c[0:SUB]
                d_in = cw_ref[K - 1:K, :] * d_c
                for j in range(1, K):
                    d_in = d_in + cw_ref[K - 1 - j:K - j, :] * ahead[j]
                o_ref[rows, :] = d_in.astype(o_ref.dtype)
                hv = h_ref[rows, :]
                gsum = gsums[which] + jnp.where(rid8 == K, jnp.sum(d_c, axis=0, keepdims=True), 0.0)
                for k in range(K):
                    gsum = gsum + jnp.where(rid8 == k, jnp.sum(ahead[K - 1 - k] * hv, axis=0, keepdims=True), 0.0)
                gsums[which] = gsum
        ha[...], hb[...] = heads
        ga_ref[...] += gsums[0]
        gb_ref[...] += gsums[1]
        acc_d[...] += _bdot(act_ref[...], dx_ref[...], TN)
        acc_a[...] += _bdot(xn_ref[...], da_ref[...], TN)
        acc_b[...] += _bdot(xn_ref[...], db_ref[...], TN)

        @pl.when(i == nb - 1)
        def _():
            dwd_ref[...] = acc_d[...].astype(dwd_ref.dtype)
            dwu_ref[0] = acc_a[...].astype(dwu_ref.dtype)
            dwu_ref[1] = acc_b[...].astype(dwu_ref.dtype)

    rb = lambda i: nb - 1 - i
    blk = pl.BlockSpec((tm, tn), lambda j, i: (rb(i), j))
    acc = pl.BlockSpec((SUB, tn), lambda j, i: (0, j))
    rows_d = pl.BlockSpec((tm, D), lambda j, i: (rb(i), 0))
    return pl.pallas_call(
        body, name=name, grid=(nj, nb),
        in_specs=[rows_d, pl.BlockSpec((tn, D), lambda j, i: (j, 0)), blk, blk, blk, blk, blk, rows_d,
                  pl.BlockSpec((K, tn), lambda j, i: (0, j)), pl.BlockSpec((K, tn), lambda j, i: (0, j + nj))],
        out_specs=[blk, blk, acc, acc, pl.BlockSpec((tn, D), lambda j, i: (j, 0)),
                   pl.BlockSpec((2, D, tn), lambda j, i: (0, 0, j))],
        out_shape=[jax.ShapeDtypeStruct((S, F), BF16), jax.ShapeDtypeStruct((S, F), BF16),
                   jax.ShapeDtypeStruct((SUB, F), F32), jax.ShapeDtypeStruct((SUB, F), F32),
                   jax.ShapeDtypeStruct((F, D), BF16), jax.ShapeDtypeStruct((2, D, F), BF16)],
        scratch_shapes=[pltpu.VMEM((SUB, tn), F32), pltpu.VMEM((SUB, tn), F32), pltpu.VMEM((tn, D), F32),
                        pltpu.VMEM((D, tn), F32), pltpu.VMEM((D, tn), F32)],
        compiler_params=_cparams(("parallel", "arbitrary")),
    )(dx, w_down, hh_a, hh_b, c_a, c_b, act, xn, cw, cw)


ADAM_BLOCK_ELEMS = 256 * 1024


def _adamw(w, m, v, parts, *, name):
    R, C = w.shape
    n = parts.shape[0]
    tr = R
    for cand in (1024, 512, 256, 128, 64, 32, 16):
        if R % cand == 0 and cand * C <= ADAM_BLOCK_ELEMS:
            tr = cand
            break
    c1 = 1.0 - ADAM_B1 ** ADAM_STEP
    c2 = 1.0 - ADAM_B2 ** ADAM_STEP

    def body(w_ref, m_ref, v_ref, p_ref, g_ref, d_ref, nm_ref, nv_ref):
        g = p_ref[0].astype(F32)
        for k in range(1, n):
            g = g + p_ref[k].astype(F32)
        m_new = ADAM_B1 * m_ref[...] + (1.0 - ADAM_B1) * g
        v_new = ADAM_B2 * v_ref[...] + (1.0 - ADAM_B2) * (g * g)
        m_hat = m_new / c1
        v_hat = v_new / c2
        g_ref[...] = g
        d_ref[...] = -ADAM_LR * (m_hat / (jnp.sqrt(v_hat) + ADAM_EPS) + ADAM_WD * w_ref[...])
        nm_ref[...] = m_new
        nv_ref[...] = v_new

    blk = pl.BlockSpec((tr, C), lambda i: (i, 0))
    sds = jax.ShapeDtypeStruct((R, C), F32)
    return pl.pallas_call(
        body, name=name, grid=(R // tr,),
        in_specs=[blk, blk, blk, pl.BlockSpec((n, tr, C), lambda i: (0, i, 0))],
        out_specs=[blk, blk, blk, blk], out_shape=[sds, sds, sds, sds],
        compiler_params=_cparams(("parallel",)),
    )(w, m, v, parts)


def _mesh_place():
    x, y, c = lax.axis_index("x"), lax.axis_index("y"), lax.axis_index("c")
    others = [(1 - x, y), (x, 1 - y), (1 - x, 1 - y)]
    return x, y, c, others


HBM_SPEC = pl.BlockSpec(memory_space=pltpu.HBM)
SEM_SPEC = pl.BlockSpec(memory_space=pltpu.SEMAPHORE)
ANY_SPEC = pl.BlockSpec(memory_space=pl.ANY)
EFFECT = pltpu.SideEffectType.DATAFLOW_SIDE_EFFECTING


def _in_hbm(a):
    return pltpu.with_memory_space_constraint(a, pltpu.HBM)


def _split_start(srcs, lands, copies, n_cp, *, name):
    n_s, n_l = len(srcs), len(lands)

    def body(*refs):
        src_refs, land_refs = refs[:n_s], refs[n_s:n_s + n_l]
        ssem, rsem = refs[n_s + n_l], refs[n_s + n_l + 1]
        token = refs[-1]
        for outgoing, _ in copies(src_refs, land_refs, ssem, rsem):
            outgoing.start()
        token[...] = jnp.zeros_like(token)

    outs = pl.pallas_call(
        body, name=name,
        out_shape=(pltpu.SemaphoreType.DMA((n_cp,)), pltpu.SemaphoreType.DMA((n_cp,)),
                   *[pltpu.HBM(a.shape, a.dtype) for a in srcs], *[pltpu.HBM(a.shape, a.dtype) for a in lands],
                   jax.ShapeDtypeStruct((SUB, LANE), F32)),
        in_specs=[HBM_SPEC] * (n_s + n_l),
        out_specs=(SEM_SPEC, SEM_SPEC, *[HBM_SPEC] * (n_s + n_l), pl.BlockSpec(memory_space=pltpu.VMEM)),
        input_output_aliases={i: 2 + i for i in range(n_s + n_l)},
        compiler_params=pltpu.CompilerParams(has_side_effects=EFFECT),
    )(*[_in_hbm(a) for a in srcs], *[_in_hbm(a) for a in lands])
    ssem, rsem = outs[0], outs[1]
    return ssem, rsem, list(outs[2:2 + n_s]), list(outs[2 + n_s:2 + n_s + n_l]), outs[-1]


def _split_wait(srcs, lands, ssem, rsem, after, copies, *, name):
    n_s, n_l = len(srcs), len(lands)

    def body(*refs):
        src_refs, land_refs = refs[:n_s], refs[n_s:n_s + n_l]
        s_ref, r_ref = refs[n_s + n_l], refs[n_s + n_l + 1]
        for outgoing, incoming in copies(src_refs, land_refs, s_ref, r_ref):
            outgoing.wait_send()
            incoming.wait_recv()

    outs = pl.pallas_call(
        body, name=name,
        out_shape=(*[pltpu.HBM(a.shape, a.dtype) for a in srcs], *[pltpu.HBM(a.shape, a.dtype) for a in lands]),
        in_specs=[HBM_SPEC] * (n_s + n_l) + [SEM_SPEC, SEM_SPEC, ANY_SPEC], out_specs=[HBM_SPEC] * (n_s + n_l),
        input_output_aliases={i: i for i in range(n_s + n_l)},
        compiler_params=pltpu.CompilerParams(has_side_effects=EFFECT),
    )(*srcs, *lands, ssem, rsem, after)
    return list(outs[:n_s]), list(outs[n_s:])


PLACE_BLOCK_ELEMS = 512 * 1024


def _place_rows(r, w):
    return _div_tile(r, max(16, PLACE_BLOCK_ELEMS // w), 16)


def _cast_place(shard, chip, axis, after, *, name):
    r, w = shard.shape
    tr = _place_rows(r, w)
    nb = r // tr
    full = (r * N_CHIP, w) if axis == 0 else (r, w * N_CHIP)
    has_after = after is not None

    def body(chip_ref, s_ref, *rest):
        rest[-1][...] = s_ref[...].astype(BF16)

    out_map = (lambda i, ch: (ch[0] * nb + i, 0)) if axis == 0 else (lambda i, ch: (i, ch[0]))
    grid_spec = pltpu.PrefetchScalarGridSpec(
        num_scalar_prefetch=1, grid=(nb,),
        in_specs=[pl.BlockSpec((tr, w), lambda i, ch: (i, 0))] + has_after * [ANY_SPEC],
        out_specs=pl.BlockSpec((tr, w), out_map))
    return pl.pallas_call(body, name=name, grid_spec=grid_spec, out_shape=jax.ShapeDtypeStruct(full, BF16),
                          compiler_params=_cparams(("parallel",)))(chip, shard, *(has_after * [after]))


def _grad_shard_shape(g, axis):
    if g.ndim == 3:
        return g.shape[1], 2 * g.shape[2] // N_CHIP
    return (g.shape[0] // N_CHIP, g.shape[1]) if axis == 0 else (g.shape[0], g.shape[1] // N_CHIP)


def _slot_place(g, ids, axis, *, name):
    r, w = _grad_shard_shape(g, axis)
    tr = _place_rows(r, w)
    nb = r // tr

    def body(ids_ref, g_ref, o_ref):
        o_ref[...] = g_ref[...]

    if g.ndim == 3:
        in_spec = pl.BlockSpec((None, tr, w), lambda i, ids_: (ids_[0] // 2, i, ids_[0] % 2))
    elif axis == 0:
        in_spec = pl.BlockSpec((tr, w), lambda i, ids_: (ids_[0] * nb + i, 0))
    else:
        in_spec = pl.BlockSpec((tr, w), lambda i, ids_: (i, ids_[0]))
    grid_spec = pltpu.PrefetchScalarGridSpec(
        num_scalar_prefetch=1, grid=(nb,), in_specs=[in_spec],
        out_specs=pl.BlockSpec((None, tr, w), lambda i, ids_: (ids_[1], i, 0)))
    return pl.pallas_call(body, name=name, grid_spec=grid_spec, out_shape=jax.ShapeDtypeStruct((N_DEV, r, w), g.dtype),
                          compiler_params=_cparams(("parallel",)))(ids, g)


class _WeightGather:
    def __init__(self, placed, shard_shapes, axes, splits, tag):
        self.placed, self.shard_shapes, self.axes, self.splits, self.tag = list(placed), shard_shapes, axes, splits, tag
        self.n = len(placed)

    def _region(self, land_refs, it, chip, half):
        r, w = self.shard_shapes[it]
        by_rows = self.axes[it] == 0
        if self.splits[it] and half is not None:
            rows = pl.ds(pl.multiple_of(half * (r // 2) + (chip * r if by_rows else 0), 16), r // 2)
        else:
            rows = pl.ds(chip * r if by_rows else 0, r)
        cols = pl.ds(0, w) if by_rows else pl.ds(pl.multiple_of(chip * w, LANE), w)
        return land_refs[it].at[rows, cols]

    def _ici(self, src_refs, land_refs, ssem, rsem):
        x, y, c, others = _mesh_place()
        pairs = []
        for it in range(self.n):
            for j, chip in enumerate(others):
                def mk(chip_from, it=it, j=j, chip=chip):
                    return pltpu.make_async_remote_copy(
                        src_ref=self._region(land_refs, it, 2 * x + y, c), dst_ref=self._region(land_refs, it, chip_from, c),
                        send_sem=ssem.at[3 * it + j], recv_sem=rsem.at[3 * it + j], device_id=(*chip, c),
                        device_id_type=MESH)
                pairs.append((mk(2 * x + y), mk(2 * chip[0] + chip[1])))
        return pairs

    def start(self):
        self.ssem, self.rsem, _, self.lands, token = _split_start(
            [], self.placed, self._ici, 3 * self.n, name="gather_start_" + self.tag)
        return token

    def _d2d(self, src_refs, land_refs, ssem, rsem):
        x, y, c, others = _mesh_place()
        pairs = []
        for it in range(self.n):
            if self.splits[it]:
                for chip in others:
                    def mk(half, it=it, chip=chip, k=len(pairs)):
                        reg = self._region(land_refs, it, 2 * chip[0] + chip[1], half)
                        return pltpu.make_async_remote_copy(src_ref=reg, dst_ref=reg, send_sem=ssem.at[k], recv_sem=rsem.at[k],
                                                            device_id=(x, y, 1 - c), device_id_type=MESH)
                    pairs.append((mk(c), mk(1 - c)))
        return pairs

    def forward(self, after):
        _, lands = _split_wait([], self.lands, self.ssem, self.rsem, after, self._ici,
                               name="gather_wait_" + self.tag)
        self.fsem, self.frsem, _, self.lands, token = _split_start(
            [], lands, self._d2d, 3 * sum(self.splits), name="gather_fwd_" + self.tag)
        return token

    def finish_forward(self, after):
        _, lands = _split_wait([], self.lands, self.fsem, self.frsem, after, self._d2d,
                               name="gather_fwd_wait_" + self.tag)
        return lands

    def finish(self, after):
        _, lands = _split_wait([], self.lands, self.ssem, self.rsem, after, self._ici,
                               name="gather_wait_" + self.tag)
        n = self.n
        n_fwd = 3 * sum(self.splits)
        if n_fwd == 0:
            return lands

        def body(*refs):
            out_refs = refs[n:2 * n]
            fsend, frecv = refs[2 * n:]
            x, y, c, others = _mesh_place()
            sibling = (x, y, 1 - c)

            def fwd(it, slot, chip, half):
                reg = self._region(out_refs, it, 2 * chip[0] + chip[1], half)
                return pltpu.make_async_remote_copy(src_ref=reg, dst_ref=reg, send_sem=fsend.at[slot],
                                                    recv_sem=frecv.at[slot], device_id=sibling, device_id_type=MESH)

            sends, recvs = [], []
            for it in range(n):
                if self.splits[it]:
                    for chip in others:
                        sends.append(fwd(it, len(sends), chip, c))
                        recvs.append(fwd(it, len(recvs), chip, 1 - c))
            for cp in sends:
                cp.start()
            for cp in recvs:
                cp.wait_recv()
            for cp in sends:
                cp.wait_send()

        fulls = pl.pallas_call(
            body, name="gather_d2d_" + self.tag, in_specs=[ANY_SPEC] * n, out_specs=[ANY_SPEC] * n,
            out_shape=[jax.ShapeDtypeStruct(a.shape, a.dtype) for a in lands],
            scratch_shapes=[pltpu.SemaphoreType.DMA((n_fwd,)), pltpu.SemaphoreType.DMA((n_fwd,))],
            input_output_aliases={i: i for i in range(n)},
        )(*lands)
        return list(fulls)


class _GradGather:
    def __init__(self, grads, axes, tag):
        self.grads, self.axes, self.tag = list(grads), axes, tag
        self.n = len(grads)
        self.shard_shapes = [_grad_shard_shape(g, ax) for g, ax in zip(grads, axes)]

    def _piece(self, src_refs, it, chip):
        r, w = self.shard_shapes[it]
        if self.grads[it].ndim == 3:
            return src_refs[it].at[chip // 2, :, pl.ds(pl.multiple_of((chip % 2) * w, LANE), w)]
        if self.axes[it] == 0:
            return src_refs[it].at[pl.ds(pl.multiple_of(chip * r, 16), r), :]
        return src_refs[it].at[:, pl.ds(pl.multiple_of(chip * w, LANE), w)]

    PER_ITEM = 4

    def _remote(self, src_refs, land_refs, ssem, rsem):
        x, y, c, others = _mesh_place()
        me = 4 * x + 2 * y + c
        pairs = []
        for it in range(self.n):
            def mk(k, piece_chip, slot, to, it=it):
                return pltpu.make_async_remote_copy(
                    src_ref=self._piece(src_refs, it, piece_chip), dst_ref=land_refs[it].at[slot],
                    send_sem=ssem.at[self.PER_ITEM * it + k], recv_sem=rsem.at[self.PER_ITEM * it + k], device_id=to,
                    device_id_type=MESH)
            for j, chip in enumerate(others):
                chip_id = 2 * chip[0] + chip[1]
                pairs.append((mk(j, chip_id, me, (*chip, c)), mk(j, chip_id, 2 * chip_id + c, (*chip, c))))
            sibling = (x, y, 1 - c)
            pairs.append((mk(3, 2 * x + y, me, sibling), mk(3, 2 * x + y, 4 * x + 2 * y + 1 - c, sibling)))
        return pairs

    def start(self):
        x, y, c = lax.axis_index("x"), lax.axis_index("y"), lax.axis_index("c")
        ids = jnp.stack([2 * x + y, 4 * x + 2 * y + c]).astype(jnp.int32)
        lands = [_slot_place(g, ids, ax, name="grads_own_%s%d" % (self.tag, it))
                 for it, (g, ax) in enumerate(zip(self.grads, self.axes))]
        self.ssem, self.rsem, self.srcs, self.lands, token = _split_start(
            self.grads, lands, self._remote, self.PER_ITEM * self.n, name="grads_start_" + self.tag)
        return token

    def _forward(self, src_refs, land_refs, ssem, rsem):
        x, y, c, others = _mesh_place()
        pairs = []
        for it in range(self.n):
            for j, ch in enumerate(others):
                def mk(slot, it=it, j=j):
                    return pltpu.make_async_remote_copy(
                        src_ref=land_refs[it].at[slot], dst_ref=land_refs[it].at[slot], send_sem=ssem.at[3 * it + j],
                        recv_sem=rsem.at[3 * it + j], device_id=(x, y, 1 - c), device_id_type=MESH)
                pairs.append((mk(4 * ch[0] + 2 * ch[1] + c), mk(4 * ch[0] + 2 * ch[1] + 1 - c)))
        return pairs

    def forward(self, after):
        _, lands = _split_wait(self.srcs, self.lands, self.ssem, self.rsem, after, self._remote,
                               name="grads_wait_" + self.tag)
        self.fsem, self.frsem, _, self.lands, token = _split_start(
            [], lands, self._forward, 3 * self.n, name="grads_fwd_" + self.tag)
        return token

    def finish(self, after):
        _, lands = _split_wait([], self.lands, self.fsem, self.frsem, after, self._forward,
                               name="grads_fwd_wait_" + self.tag)
        return lands


def _allreduce_small(vec, *, name):
    R, L = vec.shape

    def body(v_ref, o_ref, buf, send, recv, lsem):
        x, y, c, others = _mesh_place()
        me = 4 * x + 2 * y + c
        sibling = (x, y, 1 - c)

        def copy(k, slot, to, src=None):
            return pltpu.make_async_remote_copy(
                src_ref=buf.at[slot] if src is None else src, dst_ref=buf.at[slot], send_sem=send.at[k],
                recv_sem=recv.at[k], device_id=to, device_id_type=MESH)

        def slot_of(chip, core):
            return 4 * chip[0] + 2 * chip[1] + core

        mine = pltpu.make_async_copy(v_ref, buf.at[me], lsem)
        mine.start()
        first = [copy(0, me, sibling, src=v_ref)]
        first += [copy(1 + j, me, (*chip, c), src=v_ref) for j, chip in enumerate(others)]
        for cp in first:
            cp.start()
        passed = [copy(4 + j, slot_of(chip, c), sibling) for j, chip in enumerate(others)]
        for j, chip in enumerate(others):
            copy(1 + j, slot_of(chip, c), (*chip, c)).wait_recv()
            passed[j].start()
        copy(0, slot_of((x, y), 1 - c), sibling).wait_recv()
        for j, chip in enumerate(others):
            copy(4 + j, slot_of(chip, 1 - c), sibling).wait_recv()
        for cp in first + passed:
            cp.wait_send()
        mine.wait()
        total = buf[0]
        for k in range(1, N_DEV):
            total = total + buf[k]
        o_ref[...] = total

    return pl.pallas_call(
        body, name=name, in_specs=[pl.BlockSpec(memory_space=pltpu.VMEM)],
        out_specs=pl.BlockSpec(memory_space=pltpu.VMEM), out_shape=jax.ShapeDtypeStruct((R, L), F32),
        scratch_shapes=[pltpu.VMEM((N_DEV, R, L), F32), pltpu.SemaphoreType.DMA((7,)), pltpu.SemaphoreType.DMA((7,)),
                        pltpu.SemaphoreType.DMA],
        compiler_params=pltpu.CompilerParams(vmem_limit_bytes=VMEM_LIMIT),
    )(vec)


PACK_ALIGN = 1024


def _pack(arrs, row_multiple):
    flat = []
    for a in arrs:
        f = a.reshape(-1).astype(F32)
        flat.append(jnp.pad(f, (0, (-f.shape[0]) % PACK_ALIGN)))
    v = jnp.concatenate(flat)
    v = jnp.pad(v, (0, (-v.shape[0]) % (LANE * row_multiple)))
    return v.reshape(-1, LANE)


def _unpack(v, shapes):
    flat = v.reshape(-1)
    out, off = [], 0
    for s in shapes:
        size = math.prod(s)
        out.append(flat[off:off + size].reshape(s))
        off += size + (-size) % PACK_ALIGN
    return out


def _div_tile(dim, cap, mult=LANE):
    best = None
    for cand in range(mult, min(cap, dim) + 1, mult):
        if dim % cand == 0:
            best = cand
    return dim if best is None else best


WEIGHT_NAMES = ('norm1_g', 'w_in', 'ret_g', 'rg_conv_w', 'rg_conv_b', 'rg_wa', 'rg_ba', 'rg_wx', 'rg_bx', 'rg_lambda',
                'w_out', 'norm2_g', 'norm_mem_g', 'xa_wq', 'xa_wk', 'xa_wv', 'xa_wo', 'norm3_g', 'ffn_w_up',
                'ffn_conv_w', 'ffn_conv_b', 'ffn_w_down', 'final_g')
BIG_AXIS = {'w_in': 1, 'w_out': 0, 'xa_wq': 0, 'xa_wk': 0, 'xa_wv': 0, 'xa_wo': 0, 'ffn_w_up': 1, 'ffn_w_down': 0}
SMALL_SHARDED = ('rg_conv_w', 'ffn_conv_w')


def _step(x, mem, positions, loss_target, W, Mo, Vo):
    S, D = x.shape[1], x.shape[2]
    xs, mems, tgt = x[0], mem[0], loss_target[0]
    n_mem = mems.shape[0]
    pos_col = positions.reshape(S, 1)
    chip = 2 * lax.axis_index("x") + lax.axis_index("y")

    big = list(BIG_AXIS)
    shards = {n: W[n][0] for n in big}
    G = {}
    gather_groups = (('w_in', 'rg_conv_w'), ('w_out', 'xa_wq', 'xa_wk', 'xa_wv', 'xa_wo'),
                     ('ffn_w_up', 'ffn_conv_w'), ('ffn_w_down',))
    gathers, tok = [], None
    chip1 = jnp.reshape(chip, (1,)).astype(jnp.int32)
    for gi, names in enumerate(gather_groups):
        placed = []
        for n in names:
            if n in BIG_AXIS:
                placed.append(_cast_place(shards[n], chip1, BIG_AXIS[n], tok, name="place_" + n))
            else:
                s = W[n][0] if tok is None else W[n][0] + tok[0, 0]
                full = lax.empty((s.shape[0], s.shape[1] * N_CHIP), s.dtype)
                placed.append(lax.dynamic_update_slice(full, s, (0, chip * s.shape[1])))
        ag = _WeightGather(placed, [W[n][0].shape for n in names], [BIG_AXIS.get(n, 1) for n in names],
                           [n in BIG_AXIS for n in names], "g%d" % gi)
        tok = ag.start()
        gathers.append(ag)

    def finish_gather(gi, after):
        G.update(zip(gather_groups[gi], gathers[gi].finish(after)))

    def finish_forward(gi, after):
        G.update(zip(gather_groups[gi], gathers[gi].finish_forward(after)))

    R = W['ret_g'].shape[1]
    Wl = W['rg_lambda'].shape[1]
    IN = W['w_in'].shape[2] * N_CHIP
    F2 = W['ffn_w_up'].shape[2] * N_CHIP
    F = F2 // 2

    norm1_g, norm2_g, norm3_g = W['norm1_g'] + tok[0, 0], W['norm2_g'], W['norm3_g']
    norm_mem_g, final_g, ret_g = W['norm_mem_g'], W['final_g'].reshape(1, D), W['ret_g']
    rg_cb = W['rg_conv_b']
    wa, wx = W['rg_wa'][0], W['rg_wx'][0]
    ba, bx = W['rg_ba'].reshape(1, Wl), W['rg_bx'].reshape(1, Wl)
    lam = W['rg_lambda']
    ffn_cb = W['ffn_conv_b']

    def fwd_mm(a, wname, N, K, **kw):
        return _mm(a, G[wname], mode="nn", M=a.shape[0], N=N, K=K, tm=_div_tile(a.shape[0], 1024), tn=1024,
                   tk=_div_tile(K, 3072), **kw)

    def fwd_mm_norm(a, wname, res, g, name):
        return _mm(a, G[wname], mode="nn", M=a.shape[0], N=D, K=a.shape[1], tm=512, tn=D, tk=_div_tile(a.shape[1], 2048),
                   out_dtype=F32, res=res, norm_g=g, name=name)

    def bwd_x_mm(d, wname, N, K, **kw):
        return _mm(d, G[wname], mode="nt", M=d.shape[0], N=N, K=K, tm=_div_tile(d.shape[0], 1024),
                   tn=_div_tile(N, 1024, 256), tk=_div_tile(K, 3072), **kw)

    def bwd_w_mm(a, d, M, N, **kw):
        Ks = a.shape[0]
        return _mm(a, d, mode="tn", M=M, N=N, K=Ks, out_dtype=BF16, tm=_div_tile(M, 1024, 256),
                   tn=_div_tile(N, 1024, 256), tk=_div_tile(Ks, 4096 if d.dtype == BF16 else 1024), **kw)

    xn1 = _rmsnorm_fwd(xs, norm1_g, name="norm1_fwd")
    half = (R // RET_HEADS) // 2
    inv = (ROPE_BASE ** (-jnp.arange(half, dtype=F32) / half)).reshape(1, half)
    cos, sin = _rope_table(pos_col, inv + tok[0, 0], name="rope_table")
    finish_gather(0, cos)
    rg_cw = G['rg_conv_w']
    h = fwd_mm(xn1, 'w_in', IN, D, out_dtype=F32, name="mm_in")
    hl, mix = _lru_fwd(h, rg_cw, rg_cb, wa, ba, wx, bx, lam, name="lru_fwd")
    t1 = gathers[1].forward(hl)
    ret_raw, states, mix = _ret_fwd(h, cos, sin, ret_g + t1[0, 0], mix, name="ret_fwd")
    finish_forward(1, mix)
    x1, xn2 = fwd_mm_norm(mix, 'w_out', xs, norm2_g, "mm_out")
    memn = _rmsnorm_fwd(mems, norm_mem_g, name="norm_mem_fwd")
    km = fwd_mm(memn, 'xa_wk', D, D, out_dtype=BF16, name="mm_k")
    vm = fwd_mm(memn, 'xa_wv', D, D, out_dtype=BF16, name="mm_v")
    t2 = gathers[2].forward(x1)
    q = fwd_mm(xn2, 'xa_wq', D, D, out_dtype=BF16, after=t2, name="mm_q")
    o = _xattn_fwd(q, km, vm, name="xattn_fwd")
    x2, xn3 = fwd_mm_norm(o, 'xa_wo', x1, norm3_g, "mm_o")
    finish_forward(2, xn3)
    t3 = gathers[3].forward(xn3)
    ffn_cw = G['ffn_conv_w']
    act, hh_a, hh_b, hc_a, hc_b = _ffn_up_gate(xn3, G['ffn_w_up'], ffn_cw, ffn_cb + t3[0, 0], name="ffn_up_gate")
    finish_forward(3, act)
    x3 = fwd_mm(act, 'ffn_w_down', D, F, out_dtype=F32, res=x2, name="mm_down")
    dx3, d_final, loss8, dx3h = _final_loss(x3, tgt, final_g, name="final_loss")

    gw = {}
    grad_groups = []

    def start_grads(names, tag):
        gg = _GradGather([gw[n] for n in names], [BIG_AXIS[n] for n in names], tag)
        grad_groups.append((names, gg))
        return gg.start()

    dhh_a, dhh_b, gcw_a, gcw_b, gw['ffn_w_down'], gw['ffn_w_up'] = _ffn_bwd(
        dx3h, G['ffn_w_down'], hh_a, hh_b, hc_a, hc_b, act, xn3, ffn_cw, name="ffn_bwd")
    tok_a = start_grads(('ffn_w_down', 'ffn_w_up'), "a")
    dxn3 = bwd_x_mm(dhh_a, 'ffn_w_up', D, F, out_dtype=F32, after=tok_a, name="mm_dxn3_a")
    dxn3 = bwd_x_mm(dhh_b, 'ffn_w_up', D, F, out_dtype=F32, b_off=(0, F), res=dxn3, name="mm_dxn3_b")
    dx2, d_norm3, dx2h = _rmsnorm_bwd(x2, dxn3, norm3_g, dx3, name="norm3_bwd", emit_bf16=True)
    Kc = ffn_cw.shape[0]
    d_ffn_cw = jnp.concatenate([gcw_a[:Kc], gcw_b[:Kc]], axis=1)
    d_ffn_cb = jnp.concatenate([gcw_a[Kc:Kc + 1], gcw_b[Kc:Kc + 1]], axis=1)

    d_o = bwd_x_mm(dx2h, 'xa_wo', D, D, out_dtype=BF16, name="mm_do")
    gw['xa_wo'] = bwd_w_mm(o, dx2h, D, D, name="mm_dw_o")
    dq, dk, dv = _xattn_bwd(q, km, vm, d_o, name="xattn_bwd")
    gw['xa_wq'] = bwd_w_mm(xn2, dq, D, D, name="mm_dw_q")
    dxn2 = bwd_x_mm(dq, 'xa_wq', D, D, out_dtype=F32, name="mm_dxn2")
    gw['xa_wk'] = bwd_w_mm(memn, dk, D, D, name="mm_dw_k")
    gw['xa_wv'] = bwd_w_mm(memn, dv, D, D, name="mm_dw_v")
    dmemn = bwd_x_mm(dk, 'xa_wk', D, D, out_dtype=F32, name="mm_dmem_k")
    dmemn = bwd_x_mm(dv, 'xa_wv', D, D, out_dtype=F32, res=dmemn, name="mm_dmem_v")
    _, d_norm_mem = _rmsnorm_bwd(mems, dmemn, norm_mem_g, None, name="norm_mem_bwd")
    dx1, d_norm2, dx1h = _rmsnorm_bwd(x1, dxn2, norm2_g, dx2, name="norm2_bwd", emit_bf16=True)

    gw['w_out'] = bwd_w_mm(mix, dx1h, D, D, name="mm_dw_out")
    tok_b = start_grads(('xa_wo', 'xa_wq', 'xa_wk', 'xa_wv', 'w_out'), "b")
    dmix = bwd_x_mm(dx1h, 'w_out', D, D, out_dtype=F32, after=tok_b, name="mm_dmix")
    dh, d_ret_g = _ret_bwd(h, cos, sin, ret_g, states, ret_raw, dmix, name="ret_bwd")
    dh, d_rcw, d_rcb, d_wa, d_ba, d_wx, d_bx, d_lam = _lru_bwd(
        h, hl, dmix, dh, rg_cw, rg_cb, wa, ba, wx, bx, lam, name="lru_bwd")
    gw['w_in'] = bwd_w_mm(xn1, dh, D, IN, name="mm_dw_in")
    tok_c = start_grads(('w_in',), "c")
    dxn1 = bwd_x_mm(dh, 'w_in', D, IN, out_dtype=F32, after=tok_c, name="mm_dxn1")
    grad_x, d_norm1 = _rmsnorm_bwd(xs, dxn1, norm1_g, dx1, name="norm1_bwd")

    small_parts = {
        'norm1_g': d_norm1, 'ret_g': d_ret_g, 'rg_conv_w': d_rcw[:rg_cw.shape[0]], 'rg_conv_b': d_rcb,
        'rg_wa': d_wa, 'rg_ba': d_ba, 'rg_wx': d_wx, 'rg_bx': d_bx, 'rg_lambda': d_lam, 'norm2_g': d_norm2,
        'norm_mem_g': d_norm_mem, 'norm3_g': d_norm3, 'ffn_conv_w': d_ffn_cw, 'ffn_conv_b': d_ffn_cb,
        'final_g': d_final}
    small = [n for n in WEIGHT_NAMES if n not in BIG_AXIS]
    red_shapes = [(1,)] + [tuple(small_parts[n].shape) for n in small]
    fwd_tok = sum(gg.forward(d_norm1)[0:1, 0:1] for _, gg in grad_groups)
    reduced = _allreduce_small(_pack([loss8[0:1, 0:1] + fwd_tok] + [small_parts[n] for n in small], SUB),
                               name="allreduce_small")
    red = _unpack(reduced, red_shapes)
    loss = red[0][0]
    g_small = dict(zip(small, red[1:]))
    for n in SMALL_SHARDED:
        w_local = W[n].shape[-1]
        g_small[n] = lax.dynamic_slice_in_dim(g_small[n], chip * w_local, w_local, axis=1)

    out_g, out_d, out_m, out_v = {}, {}, {}, {}
    rows = 512
    pk = lambda d: _pack([d[n] for n in small], rows)
    g_pack = _pack([g_small[n] for n in small], rows)
    res_small = _adamw(pk(W), pk(Mo), pk(Vo), g_pack[None], name="adamw_small")
    shapes_small = [tuple(W[n].shape) for n in small]
    for dst, packed in zip((out_g, out_d, out_m, out_v), res_small):
        for n, val in zip(small, _unpack(packed, shapes_small)):
            dst[n] = val
    last = res_small[0]
    for names, gg in grad_groups:
        for n, land in zip(names, gg.finish(last)):
            g, d, m_new, v_new = _adamw(shards[n], Mo[n][0], Vo[n][0], land, name="adamw_" + n)
            out_g[n], out_d[n], out_m[n], out_v[n] = (t.reshape(W[n].shape) for t in (g, d, m_new, v_new))
            last = g
    return (loss, grad_x[None], *[out_g[n] for n in WEIGHT_NAMES], *[out_d[n] for n in WEIGHT_NAMES],
            *[out_m[n] for n in WEIGHT_NAMES], *[out_v[n] for n in WEIGHT_NAMES])


def kernel(x, mem, positions, norm1_g, w_in, ret_g, rg_conv_w, rg_conv_b, rg_wa, rg_ba, rg_wx, rg_bx, rg_lambda, w_out, norm2_g, norm_mem_g, xa_wq, xa_wk, xa_wv, xa_wo, norm3_g, ffn_w_up, ffn_conv_w, ffn_conv_b, ffn_w_down, final_g, loss_target, m_norm1_g, m_w_in, m_ret_g, m_rg_conv_w, m_rg_conv_b, m_rg_wa, m_rg_ba, m_rg_wx, m_rg_bx, m_rg_lambda, m_w_out, m_norm2_g, m_norm_mem_g, m_xa_wq, m_xa_wk, m_xa_wv, m_xa_wo, m_norm3_g, m_ffn_w_up, m_ffn_conv_w, m_ffn_conv_b, m_ffn_w_down, m_final_g, v_norm1_g, v_w_in, v_ret_g, v_rg_conv_w, v_rg_conv_b, v_rg_wa, v_rg_ba, v_rg_wx, v_rg_bx, v_rg_lambda, v_w_out, v_norm2_g, v_norm_mem_g, v_xa_wq, v_xa_wk, v_xa_wv, v_xa_wo, v_norm3_g, v_ffn_w_up, v_ffn_conv_w, v_ffn_conv_b, v_ffn_w_down, v_final_g):
    W = dict(zip(WEIGHT_NAMES, (norm1_g, w_in, ret_g, rg_conv_w, rg_conv_b, rg_wa, rg_ba, rg_wx, rg_bx, rg_lambda, w_out,
                                norm2_g, norm_mem_g, xa_wq, xa_wk, xa_wv, xa_wo, norm3_g, ffn_w_up, ffn_conv_w,
                                ffn_conv_b, ffn_w_down, final_g)))
    Mo = dict(zip(WEIGHT_NAMES, (m_norm1_g, m_w_in, m_ret_g, m_rg_conv_w, m_rg_conv_b, m_rg_wa, m_rg_ba, m_rg_wx, m_rg_bx,
                                 m_rg_lambda, m_w_out, m_norm2_g, m_norm_mem_g, m_xa_wq, m_xa_wk, m_xa_wv, m_xa_wo,
                                 m_norm3_g, m_ffn_w_up, m_ffn_conv_w, m_ffn_conv_b, m_ffn_w_down, m_final_g)))
    Vo = dict(zip(WEIGHT_NAMES, (v_norm1_g, v_w_in, v_ret_g, v_rg_conv_w, v_rg_conv_b, v_rg_wa, v_rg_ba, v_rg_wx, v_rg_bx,
                                 v_rg_lambda, v_w_out, v_norm2_g, v_norm_mem_g, v_xa_wq, v_xa_wk, v_xa_wv, v_xa_wo,
                                 v_norm3_g, v_ffn_w_up, v_ffn_conv_w, v_ffn_conv_b, v_ffn_w_down, v_final_g)))
    return _step(x, mem, positions, loss_target, W, Mo, Vo)
```

```python
import math

import jax
import jax.numpy as jnp
from jax import lax
from jax.experimental import pallas as pl
from jax.experimental.pallas import tpu as pltpu

F32 = jnp.float32
BF16 = jnp.bfloat16

EPS = 1e-6
RET_HEADS = 4
RET_CHUNK = 128
ROPE_BASE = 10000.0
LRU_BLOCKS = 8
LRU_C = 8.0
XA_HEADS = 4

ADAM_LR = 0.001
ADAM_B1 = 0.9
ADAM_B2 = 0.999
ADAM_EPS = 1e-08
ADAM_WD = 0.01
ADAM_STEP = 10

N_DEV = 8
N_CHIP = 4
MESH = pl.DeviceIdType.MESH
SUB = 8
LANE = 128
VMEM_LIMIT = 56 * 1024 * 1024

NN = ((1,), (0,))
NT = ((1,), (1,))
TN = ((0,), (0,))


def _cparams(sem):
    return pltpu.CompilerParams(dimension_semantics=sem, vmem_limit_bytes=VMEM_LIMIT)


def _sigmoid(v):
    return 1.0 / (1.0 + jnp.exp(-v))


def _bdot(a, b, dims):
    return lax.dot_general(a.astype(BF16), b.astype(BF16), (dims, ((), ())), preferred_element_type=F32)


def _row_iota(shape):
    return lax.broadcasted_iota(jnp.int32, shape, 0)


def _shift_down(v, tail, k):
    if k == 0:
        return v
    r = pltpu.roll(v, k, 0)
    rt = pltpu.roll(tail, k, 0)
    first = jnp.where(_row_iota(rt.shape) < k, rt, r[0:SUB])
    return jnp.concatenate([first, r[SUB:]], axis=0)


def _shift_up(v, head, k):
    if k == 0:
        return v
    n = v.shape[0]
    r = pltpu.roll(v, n - k, 0)
    rh = pltpu.roll(head, SUB - k, 0)
    last = jnp.where(_row_iota(rh.shape) >= SUB - k, rh, r[n - SUB:n])
    return jnp.concatenate([r[:n - SUB], last], axis=0)


def _mm(a, b, *, mode, M, N, K, out_dtype, name, tm=512, tn=512, tk=512, b_off=(0, 0), res=None, norm_g=None,
        after=None):
    tm, tn, tk = min(tm, M), min(tn, N), min(tk, K)
    assert M % tm == 0 and N % tn == 0 and K % tk == 0, (name, M, N, K, tm, tn, tk)
    nk = K // tk
    if mode == "nn":
        a_blk, b_blk, dims = (tm, tk), (tk, tn), NN
        a_map = lambda i, j, k: (i, k)
        b_map = lambda i, j, k: (k + b_off[0] // tk, j + b_off[1] // tn)
    elif mode == "nt":
        a_blk, b_blk, dims = (tm, tk), (tn, tk), NT
        a_map = lambda i, j, k: (i, k)
        b_map = lambda i, j, k: (j + b_off[0] // tn, k + b_off[1] // tk)
    else:
        a_blk, b_blk, dims = (tk, tm), (tk, tn), TN
        a_map = lambda i, j, k: (k, i)
        b_map = lambda i, j, k: (k + b_off[0] // tk, j + b_off[1] // tn)
    assert b_off[0] % b_blk[0] == 0 and b_off[1] % b_blk[1] == 0, (name, b_off, b_blk)
    has_res, has_norm, has_after = res is not None, norm_g is not None, after is not None
    assert not has_norm or tn == N

    def body(*refs):
        refs = list(refs)
        a_ref, b_ref = refs[0], refs[1]
        pos = 2
        r_ref = g_ref = n_ref = None
        if has_res:
            r_ref = refs[pos]
            pos += 1
        if has_norm:
            g_ref = refs[pos]
            pos += 1
        pos += has_after
        o_ref = refs[pos]
        pos += 1
        if has_norm:
            n_ref = refs[pos]
            pos += 1
        acc = refs[pos] if nk > 1 else None
        k = pl.program_id(2)
        part = _bdot(a_ref[...], b_ref[...], dims)

        def finish(total):
            if has_res:
                total = total + r_ref[...].astype(F32)
            o_ref[...] = total.astype(o_ref.dtype)
            if has_norm:
                r = lax.rsqrt(jnp.mean(total * total, axis=-1, keepdims=True) + EPS)
                n_ref[...] = (total * r * g_ref[...]).astype(n_ref.dtype)

        if nk == 1:
            finish(part)
        else:
            @pl.when(k == 0)
            def _():
                acc[...] = part

            @pl.when(k > 0)
            def _():
                acc[...] += part

            @pl.when(k == nk - 1)
            def _():
                finish(acc[...])

    in_specs = [pl.BlockSpec(a_blk, a_map), pl.BlockSpec(b_blk, b_map)]
    args = [a, b]
    if has_res:
        in_specs.append(pl.BlockSpec((tm, tn), lambda i, j, k: (i, j)))
        args.append(res)
    if has_norm:
        in_specs.append(pl.BlockSpec((1, N), lambda i, j, k: (0, 0)))
        args.append(norm_g)
    if has_after:
        in_specs.append(pl.BlockSpec(memory_space=pl.ANY))
        args.append(after)
    out_shape = jax.ShapeDtypeStruct((M, N), out_dtype)
    out_specs = pl.BlockSpec((tm, tn), lambda i, j, k: (i, j))
    if has_norm:
        out_shape = [out_shape, jax.ShapeDtypeStruct((M, N), BF16)]
        out_specs = [out_specs, pl.BlockSpec((tm, tn), lambda i, j, k: (i, j))]
    return pl.pallas_call(
        body, name=name, grid=(M // tm, N // tn, nk), in_specs=in_specs,
        out_specs=out_specs, out_shape=out_shape,
        scratch_shapes=[pltpu.VMEM((tm, tn), F32)] if nk > 1 else [],
        compiler_params=_cparams(("parallel", "parallel", "arbitrary")),
    )(*args)


def _rmsnorm_fwd(x, g, *, name, ts=512):
    S, D = x.shape
    ts = min(ts, S)

    def body(x_ref, g_ref, o_ref):
        xv = x_ref[...]
        r = lax.rsqrt(jnp.mean(xv * xv, axis=-1, keepdims=True) + EPS)
        o_ref[...] = (xv * r * g_ref[...]).astype(o_ref.dtype)

    return pl.pallas_call(
        body, name=name, grid=(S // ts,),
        in_specs=[pl.BlockSpec((ts, D), lambda i: (i, 0)), pl.BlockSpec((1, D), lambda i: (0, 0))],
        out_specs=pl.BlockSpec((ts, D), lambda i: (i, 0)),
        out_shape=jax.ShapeDtypeStruct((S, D), BF16),
        compiler_params=_cparams(("parallel",)),
    )(x, g)


def _rmsnorm_bwd(x, dxn, g, res, *, name, ts=512, emit_bf16=False):
    S, D = x.shape
    ts = min(ts, S)
    has_res = res is not None

    def body(*refs):
        refs = list(refs)
        dx16_ref = refs.pop() if emit_bf16 else None
        if has_res:
            x_ref, d_ref, g_ref, r_ref, dx_ref, dg_ref = refs
        else:
            x_ref, d_ref, g_ref, dx_ref, dg_ref = refs
        i = pl.program_id(0)
        xv = x_ref[...]
        dv = d_ref[...].astype(F32)
        r = lax.rsqrt(jnp.mean(xv * xv, axis=-1, keepdims=True) + EPS)
        gd = dv * g_ref[...]
        proj = jnp.mean(xv * gd, axis=-1, keepdims=True)
        dx = r * gd - xv * (r * r * r) * proj
        if has_res:
            dx = dx + r_ref[...]
        dx_ref[...] = dx
        if emit_bf16:
            dx16_ref[...] = dx.astype(BF16)
        part = jnp.sum(dv * xv * r, axis=0, keepdims=True)

        @pl.when(i == 0)
        def _():
            dg_ref[...] = part

        @pl.when(i > 0)
        def _():
            dg_ref[...] += part

    row = pl.BlockSpec((ts, D), lambda i: (i, 0))
    vec = pl.BlockSpec((1, D), lambda i: (0, 0))
    in_specs = [row, row, vec] + ([row] if has_res else [])
    args = [x, dxn, g] + ([res] if has_res else [])
    extra = emit_bf16 * [jax.ShapeDtypeStruct((S, D), BF16)]
    return pl.pallas_call(
        body, name=name, grid=(S // ts,), in_specs=in_specs, out_specs=[row, vec] + emit_bf16 * [row],
        out_shape=[jax.ShapeDtypeStruct((S, D), F32), jax.ShapeDtypeStruct((1, D), F32)] + extra,
        compiler_params=_cparams(("arbitrary",)),
    )(*args)


def _final_loss(x, target, g, *, name, ts=512):
    S, D = x.shape
    ts = min(ts, S)

    def body(x_ref, t_ref, g_ref, dx_ref, dg_ref, loss_ref, dx16_ref):
        i = pl.program_id(0)
        xv = x_ref[...]
        gv = g_ref[...]
        r = lax.rsqrt(jnp.mean(xv * xv, axis=-1, keepdims=True) + EPS)
        y = xv * r * gv
        err = y - t_ref[...]
        row_loss = jnp.mean(err * err, axis=-1, keepdims=True)
        lpart = 0.5 * jnp.sum(row_loss, axis=0, keepdims=True)
        dy = err * (1.0 / D)
        gd = dy * gv
        proj = jnp.mean(xv * gd, axis=-1, keepdims=True)
        dx = r * gd - xv * (r * r * r) * proj
        dx_ref[...] = dx
        dx16_ref[...] = dx.astype(BF16)
        part = jnp.sum(dy * xv * r, axis=0, keepdims=True)
        lfull = jnp.broadcast_to(lpart, loss_ref.shape)

        @pl.when(i == 0)
        def _():
            dg_ref[...] = part
            loss_ref[...] = lfull

        @pl.when(i > 0)
        def _():
            dg_ref[...] += part
            loss_ref[...] += lfull

    row = pl.BlockSpec((ts, D), lambda i: (i, 0))
    vec = pl.BlockSpec((1, D), lambda i: (0, 0))
    return pl.pallas_call(
        body, name=name, grid=(S // ts,), in_specs=[row, row, vec],
        out_specs=[row, vec, pl.BlockSpec((SUB, LANE), lambda i: (0, 0)), row],
        out_shape=[jax.ShapeDtypeStruct((S, D), F32), jax.ShapeDtypeStruct((1, D), F32),
                   jax.ShapeDtypeStruct((SUB, LANE), F32), jax.ShapeDtypeStruct((S, D), BF16)],
        compiler_params=_cparams(("arbitrary",)),
    )(x, target, g)


def _rope_table(pos_col, inv, *, name, ts=1024):
    S = pos_col.shape[0]
    ts = min(ts, S)
    half = inv.shape[1]

    def body(p_ref, inv_ref, c_ref, s_ref):
        ang = p_ref[...].astype(F32) * inv_ref[...]
        c_ref[...] = jnp.cos(ang)
        s_ref[...] = jnp.sin(ang)

    tab = pl.BlockSpec((ts, half), lambda i: (i, 0))
    return pl.pallas_call(
        body, name=name, grid=(S // ts,),
        in_specs=[pl.BlockSpec((ts, 1), lambda i: (i, 0)), pl.BlockSpec((1, half), lambda i: (0, 0))],
        out_specs=[tab, tab],
        out_shape=[jax.ShapeDtypeStruct((S, half), F32), jax.ShapeDtypeStruct((S, half), F32)],
        compiler_params=_cparams(("parallel",)),
    )(pos_col, inv)


def _ret_consts(C, log_g):
    ii = lax.broadcasted_iota(jnp.int32, (C, C), 0)
    jj = lax.broadcasted_iota(jnp.int32, (C, C), 1)
    diff = (ii - jj).astype(F32)
    intra = jnp.where(ii >= jj, jnp.exp(log_g * jnp.maximum(diff, 0.0)), 0.0)
    idx = lax.broadcasted_iota(jnp.int32, (C, 1), 0).astype(F32)
    qd = jnp.exp(log_g * (idx + 1.0))
    kd = jnp.exp(log_g * (C - 1.0 - idx))
    cd = math.exp(log_g * C)
    return intra, qd, kd, cd


def _rot(t, cs, sn):
    half = t.shape[-1] // 2
    t1, t2 = t[:, :half], t[:, half:]
    return jnp.concatenate([t1 * cs - t2 * sn, t1 * sn + t2 * cs], axis=-1)


def _unrot(d, cs, sn):
    half = d.shape[-1] // 2
    d1, d2 = d[:, :half], d[:, half:]
    return jnp.concatenate([d1 * cs + d2 * sn, d2 * cs - d1 * sn], axis=-1)


def _ret_fwd(h, cos, sin, ret_g, mix, *, name, ch=2):
    S = h.shape[0]
    R = ret_g.shape[1]
    H, C = RET_HEADS, RET_CHUNK
    Dh = R // H
    ts = ch * C
    assert S % ts == 0
    log_gs = [math.log(1.0 - 2.0 ** (-5.0 - hd)) for hd in range(H)]
    scale = Dh ** -0.5

    def body(x_ref, c_ref, s_ref, rg_ref, mix_in, ret_ref, st_ref, mix_ref, state):
        i = pl.program_id(0)

        @pl.when(i == 0)
        def _():
            state[...] = jnp.zeros_like(state)

        for c in range(ch):
            rows = pl.ds(c * C, C)
            cs, sn = c_ref[rows, :], s_ref[rows, :]
            for hd in range(H):
                intra, qd, kd, cd = _ret_consts(C, log_gs[hd])
                q = x_ref[rows, pl.ds(hd * Dh, Dh)]
                k = x_ref[rows, pl.ds(R + hd * Dh, Dh)]
                v = x_ref[rows, pl.ds(2 * R + hd * Dh, Dh)]
                g = x_ref[rows, pl.ds(3 * R + hd * Dh, Dh)]
                rq = _rot(q, cs, sn)
                rk = _rot(k, cs, sn) * scale
                st = state[hd]
                st_ref[c, hd] = st.astype(BF16)
                s_ = _bdot(rq, rk, NT) * intra
                ret = _bdot(s_, v, NN) + _bdot(rq * qd, st, NN)
                state[hd] = st * cd + _bdot(rk * kd, v, TN)
                ret_ref[rows, pl.ds(hd * Dh, Dh)] = ret
                rr = lax.rsqrt(jnp.mean(ret * ret, axis=-1, keepdims=True) + EPS)
                out = ret * rr * rg_ref[:, pl.ds(hd * Dh, Dh)] * (g * _sigmoid(g))
                mix_ref[rows, pl.ds(hd * Dh, Dh)] = out.astype(BF16)

    n_chunks = S // C
    return pl.pallas_call(
        body, name=name, grid=(S // ts,),
        in_specs=[pl.BlockSpec((ts, 4 * R), lambda i: (i, 0)),
                  pl.BlockSpec((ts, Dh // 2), lambda i: (i, 0)), pl.BlockSpec((ts, Dh // 2), lambda i: (i, 0)),
                  pl.BlockSpec((1, R), lambda i: (0, 0)), pl.BlockSpec(memory_space=pl.ANY)],
        out_specs=[pl.BlockSpec((ts, R), lambda i: (i, 0)),
                   pl.BlockSpec((ch, H, Dh, Dh), lambda i: (i, 0, 0, 0)),
                   pl.BlockSpec((ts, R), lambda i: (i, 0))],
        out_shape=[jax.ShapeDtypeStruct((S, R), F32), jax.ShapeDtypeStruct((n_chunks, H, Dh, Dh), BF16),
                   jax.ShapeDtypeStruct(mix.shape, mix.dtype)],
        scratch_shapes=[pltpu.VMEM((H, Dh, Dh), F32)],
        input_output_aliases={4: 2},
        compiler_params=_cparams(("arbitrary",)),
    )(h, cos, sin, ret_g, mix)


def _ret_bwd(h, cos, sin, ret_g, states, ret_raw, dmix, *, name, ch=2):
    S = h.shape[0]
    R = ret_g.shape[1]
    H, C = RET_HEADS, RET_CHUNK
    Dh = R // H
    ts = ch * C
    nb = S // ts
    log_gs = [math.log(1.0 - 2.0 ** (-5.0 - hd)) for hd in range(H)]
    scale = Dh ** -0.5

    def body(x_ref, c_ref, s_ref, rg_ref, st_ref, ret_ref, dm_ref, dh_ref, drg_ref, dstate):
        i = pl.program_id(0)

        @pl.when(i == 0)
        def _():
            dstate[...] = jnp.zeros_like(dstate)
            drg_ref[...] = jnp.zeros_like(drg_ref)

        for c in reversed(range(ch)):
            rows = pl.ds(c * C, C)
            cs, sn = c_ref[rows, :], s_ref[rows, :]
            for hd in range(H):
                intra, qd, kd, cd = _ret_consts(C, log_gs[hd])
                cols = pl.ds(hd * Dh, Dh)
                q = x_ref[rows, pl.ds(hd * Dh, Dh)]
                k = x_ref[rows, pl.ds(R + hd * Dh, Dh)]
                v = x_ref[rows, pl.ds(2 * R + hd * Dh, Dh)]
                g = x_ref[rows, pl.ds(3 * R + hd * Dh, Dh)]
                rq = _rot(q, cs, sn)
                rk = _rot(k, cs, sn) * scale
                ret = ret_ref[rows, cols]
                dm = dm_ref[rows, cols]
                rgv = rg_ref[:, cols]
                rr = lax.rsqrt(jnp.mean(ret * ret, axis=-1, keepdims=True) + EPS)
                retn = ret * rr
                sg = _sigmoid(g)
                silu = g * sg
                drg_ref[:, cols] += jnp.sum(dm * retn * silu, axis=0, keepdims=True)
                dg = dm * retn * rgv * (sg * (1.0 + g * (1.0 - sg)))
                dretn = dm * rgv * silu
                d_o = rr * dretn - ret * (rr * rr * rr) * jnp.mean(ret * dretn, axis=-1, keepdims=True)
                st = st_ref[c, hd]
                d_s = dstate[hd]
                a_ = _bdot(rq, rk, NT) * intra
                d_a = _bdot(d_o, v, NT) * intra
                d_qr = _bdot(d_a, rk, NN) + _bdot(d_o, st, NT) * qd
                d_kr = _bdot(d_a, rq, TN) + _bdot(v, d_s, NT) * kd
                d_v = _bdot(a_, d_o, TN) + _bdot(rk * kd, d_s, NN)
                dstate[hd] = d_s * cd + _bdot(rq * qd, d_o, TN)
                dh_ref[rows, pl.ds(hd * Dh, Dh)] = _unrot(d_qr, cs, sn).astype(BF16)
                dh_ref[rows, pl.ds(R + hd * Dh, Dh)] = (_unrot(d_kr, cs, sn) * scale).astype(BF16)
                dh_ref[rows, pl.ds(2 * R + hd * Dh, Dh)] = d_v.astype(BF16)
                dh_ref[rows, pl.ds(3 * R + hd * Dh, Dh)] = dg.astype(BF16)

    rb = lambda i: nb - 1 - i
    return pl.pallas_call(
        body, name=name, grid=(nb,),
        in_specs=[pl.BlockSpec((ts, 4 * R), lambda i: (rb(i), 0)),
                  pl.BlockSpec((ts, Dh // 2), lambda i: (rb(i), 0)), pl.BlockSpec((ts, Dh // 2), lambda i: (rb(i), 0)),
                  pl.BlockSpec((1, R), lambda i: (0, 0)),
                  pl.BlockSpec((ch, H, Dh, Dh), lambda i: (rb(i), 0, 0, 0)),
                  pl.BlockSpec((ts, R), lambda i: (rb(i), 0)),
                  pl.BlockSpec((ts, R), lambda i: (rb(i), 0))],
        out_specs=[pl.BlockSpec((ts, 4 * R), lambda i: (rb(i), 0)), pl.BlockSpec((1, R), lambda i: (0, 0))],
        out_shape=[jax.ShapeDtypeStruct((S, 6 * R), BF16), jax.ShapeDtypeStruct((1, R), F32)],
        scratch_shapes=[pltpu.VMEM((H, Dh, Dh), F32)],
        compiler_params=_cparams(("arbitrary",)),
    )(h, cos, sin, ret_g, states, ret_raw, dmix)


GELU_C = math.sqrt(2.0 / math.pi)
GELU_A = 0.044715


def _gelu_parts(y):
    t = jnp.tanh(GELU_C * (y + GELU_A * y * y * y))
    val = 0.5 * y * (1.0 + t)
    grad = 0.5 * (1.0 + t) + 0.5 * y * (1.0 - t * t) * GELU_C * (1.0 + 3.0 * GELU_A * y * y)
    return val, grad


def _neg_expm1(x):
    series = -x * (1.0 + x * (1.0 / 2.0) * (1.0 + x * (1.0 / 3.0) * (1.0 + x * (1.0 / 4.0) * (
        1.0 + x * (1.0 / 5.0) * (1.0 + x * (1.0 / 6.0) * (1.0 + x * (1.0 / 7.0)))))))
    return jnp.where(x > -0.35, series, 1.0 - jnp.exp(x))


def _log_sigmoid(x):
    return jnp.minimum(x, 0.0) - jnp.log1p(jnp.exp(-jnp.abs(x)))


def _lru_gates(uc, wa_ref, ba_ref, wx_ref, bx_ref):
    nbk = wa_ref.shape[0]
    bd = wa_ref.shape[1]
    rs, gs = [], []
    for n in range(nbk):
        ucn = uc[:, n * bd:(n + 1) * bd]
        rs.append(_sigmoid(_bdot(ucn, wa_ref[n], NN) + ba_ref[:, pl.ds(n * bd, bd)]))
        gs.append(_sigmoid(_bdot(ucn, wx_ref[n], NN) + bx_ref[:, pl.ds(n * bd, bd)]))
    return jnp.concatenate(rs, axis=-1), jnp.concatenate(gs, axis=-1)


def _lru_fwd(h, conv_w, conv_b, wa, ba, wx, bx, lam, *, name, ts=256):
    S = h.shape[0]
    W = lam.shape[1]
    K = conv_w.shape[0]
    ts = min(ts, S)

    def body(u_ref, y_ref, cw_ref, cb_ref, wa_ref, ba_ref, wx_ref, bx_ref, lam_ref, hl_ref, mix_ref, tail, hlast):
        i = pl.program_id(0)

        @pl.when(i == 0)
        def _():
            tail[...] = jnp.zeros_like(tail)
            hlast[...] = jnp.zeros_like(hlast)

        u = u_ref[...]
        tl = tail[...]
        uc = cb_ref[...] + cw_ref[K - 1:K, :] * u
        for k in range(K - 1):
            uc = uc + cw_ref[k:k + 1, :] * _shift_down(u, tl, K - 1 - k)
        tail[...] = u[ts - SUB:ts]
        r, ig = _lru_gates(uc, wa_ref, ba_ref, wx_ref, bx_ref)
        log_a = LRU_C * r * _log_sigmoid(lam_ref[...])
        a = jnp.exp(log_a)
        b = jnp.sqrt(_neg_expm1(2.0 * log_a)) * (ig * uc)
        in_tile = _row_iota((ts, W)) & (SUB - 1)
        d = 1
        while d < SUB:
            a_s = jnp.where(in_tile < d, 1.0, pltpu.roll(a, d, 0))
            b_s = jnp.where(in_tile < d, 0.0, pltpu.roll(b, d, 0))
            b = a * b_s + b
            a = a * a_s
            d *= 2
        before = hlast[SUB - 1:SUB, :]
        for k in range(ts // SUB):
            tile = slice(k * SUB, (k + 1) * SUB)
            h_tile = a[tile] * before + b[tile]
            hl_ref[tile, :] = h_tile
            before = h_tile[SUB - 1:SUB, :]
        hlast[...] = hl_ref[ts - SUB:ts, :]
        gy, _ = _gelu_parts(y_ref[...])
        mix_ref[...] = (hl_ref[...] * gy).astype(BF16)

    full = lambda shape: pl.BlockSpec(shape, lambda i: tuple(0 for _ in shape))
    return pl.pallas_call(
        body, name=name, grid=(S // ts,),
        in_specs=[pl.BlockSpec((ts, W), lambda i: (i, 4)), pl.BlockSpec((ts, W), lambda i: (i, 5)),
                  full(conv_w.shape), full(conv_b.shape), full(wa.shape), full(ba.shape), full(wx.shape),
                  full(bx.shape), full(lam.shape)],
        out_specs=[pl.BlockSpec((ts, W), lambda i: (i, 0)), pl.BlockSpec((ts, W), lambda i: (i, 1))],
        out_shape=[jax.ShapeDtypeStruct((S, W), F32), jax.ShapeDtypeStruct((S, 2 * W), BF16)],
        scratch_shapes=[pltpu.VMEM((SUB, W), F32), pltpu.VMEM((SUB, W), F32)],
        compiler_params=_cparams(("arbitrary",)),
    )(h, h, conv_w, conv_b, wa, ba, wx, bx, lam)


def _lru_bwd(h, hl, dmix, dh, conv_w, conv_b, wa, ba, wx, bx, lam, *, name, ts=256):
    S = h.shape[0]
    W = lam.shape[1]
    K = conv_w.shape[0]
    nbk, bd = wa.shape[0], wa.shape[1]
    ts = min(ts, S)
    nb = S // ts
    t8 = ts // SUB

    def body(u_ref, y_ref, uh_ref, hl_ref, hh_ref, dm_ref, cw_ref, cb_ref, wa_ref, ba_ref, wx_ref, bx_ref, lam_ref,
             dh_in, dh_ref, dcw_ref, dcb_ref, dwa_ref, dba_ref, dwx_ref, dbx_ref, dlam_ref, carry, head, lam_buf):
        i = pl.program_id(0)
        blk = nb - 1 - i

        @pl.when(i == 0)
        def _():
            carry[...] = jnp.zeros_like(carry)
            head[...] = jnp.zeros_like(head)
            for ref in (dcw_ref, dcb_ref, dwa_ref, dba_ref, dwx_ref, dbx_ref, dlam_ref):
                ref[...] = jnp.zeros_like(ref)

        inside = (blk > 0).astype(F32)
        u = u_ref[...]
        tl = uh_ref[...] * inside
        sh = [_shift_down(u, tl, K - 1 - k) for k in range(K)]
        uc = cb_ref[...]
        for k in range(K):
            uc = uc + cw_ref[k:k + 1, :] * sh[k]
        r, ig = _lru_gates(uc, wa_ref, ba_ref, wx_ref, bx_ref)
        lam_v = lam_ref[...]
        ls = _log_sigmoid(lam_v)
        log_a = LRU_C * r * ls
        a = jnp.exp(log_a)
        mult = jnp.sqrt(_neg_expm1(2.0 * log_a))
        hcur = hl_ref[...]
        hprev = _shift_down(hcur, hh_ref[...] * inside, 1)
        gy, dgy = _gelu_parts(y_ref[...])
        dm = dm_ref[...]
        d_y = dm * hcur * dgy
        rid = _row_iota((ts, W))
        bq = dm * gy + jnp.where(rid == ts - 1, carry[0:1, :], 0.0)
        aq = jnp.where(rid == ts - 1, 0.0, pltpu.roll(a, ts - 1, 0))
        in_tile = rid & (SUB - 1)
        d = 1
        while d < SUB:
            a_s = jnp.where(in_tile >= SUB - d, 1.0, pltpu.roll(aq, ts - d, 0))
            b_s = jnp.where(in_tile >= SUB - d, 0.0, pltpu.roll(bq, ts - d, 0))
            bq = bq + aq * b_s
            aq = aq * a_s
            d *= 2
        after_row = jnp.zeros((1, W), F32)
        for k in reversed(range(ts // SUB)):
            tile = slice(k * SUB, (k + 1) * SUB)
            lam_tile = aq[tile] * after_row + bq[tile]
            lam_buf[tile, :] = lam_tile
            after_row = lam_tile[0:1, :]
        lam_t = lam_buf[...]
        carry[...] = (a * lam_t)[0:SUB]
        d_a = lam_t * hprev
        d_mult = lam_t * (ig * uc)
        d_i = lam_t * mult * uc
        d_uc = lam_t * mult * ig
        d_log_a = d_a * a - d_mult * (a * a) / mult
        d_r = d_log_a * (LRU_C * ls)
        dlam_ref[...] += jnp.sum(d_log_a * (LRU_C * r), axis=0, keepdims=True) * _sigmoid(-lam_v)
        d_pr = d_r * r * (1.0 - r)
        d_pi = d_i * ig * (1.0 - ig)
        dba_ref[...] += jnp.sum(d_pr, axis=0, keepdims=True)
        dbx_ref[...] += jnp.sum(d_pi, axis=0, keepdims=True)
        extra = []
        for n in range(nbk):
            sl = slice(n * bd, (n + 1) * bd)
            ucn = uc[:, sl]
            dwa_ref[n] += _bdot(ucn, d_pr[:, sl], TN)
            dwx_ref[n] += _bdot(ucn, d_pi[:, sl], TN)
            extra.append(_bdot(d_pr[:, sl], wa_ref[n], NT) + _bdot(d_pi[:, sl], wx_ref[n], NT))
        d_uc = d_uc + jnp.concatenate(extra, axis=-1)
        dcb_ref[...] += jnp.sum(d_uc, axis=0, keepdims=True)
        rid8 = _row_iota((SUB, W))
        dcw = jnp.zeros((SUB, W), F32)
        for k in range(K):
            dcw = dcw + jnp.where(rid8 == k, jnp.sum(d_uc * sh[k], axis=0, keepdims=True), 0.0)
        dcw_ref[...] += dcw
        hd = head[...]
        d_u = cw_ref[K - 1:K, :] * d_uc
        for j in range(1, K):
            d_u = d_u + cw_ref[K - 1 - j:K - j, :] * _shift_up(d_uc, hd, j)
        head[...] = d_uc[0:SUB]
        dh_ref[:, 0:W] = d_u.astype(BF16)
        dh_ref[:, W:2 * W] = d_y.astype(BF16)

    rb = lambda i: nb - 1 - i
    prev8 = lambda i: jnp.maximum(rb(i) * t8 - 1, 0)
    full = lambda shape: pl.BlockSpec(shape, lambda i: tuple(0 for _ in shape))
    small = [jax.ShapeDtypeStruct((SUB, W), F32), jax.ShapeDtypeStruct((1, W), F32),
             jax.ShapeDtypeStruct(wa.shape, F32), jax.ShapeDtypeStruct((1, W), F32),
             jax.ShapeDtypeStruct(wx.shape, F32), jax.ShapeDtypeStruct((1, W), F32),
             jax.ShapeDtypeStruct((1, W), F32)]
    return pl.pallas_call(
        body, name=name, grid=(nb,),
        in_specs=[pl.BlockSpec((ts, W), lambda i: (rb(i), 4)), pl.BlockSpec((ts, W), lambda i: (rb(i), 5)),
                  pl.BlockSpec((SUB, W), lambda i: (prev8(i), 4)),
                  pl.BlockSpec((ts, W), lambda i: (rb(i), 0)), pl.BlockSpec((SUB, W), lambda i: (prev8(i), 0)),
                  pl.BlockSpec((ts, W), lambda i: (rb(i), 1)),
                  full(conv_w.shape), full(conv_b.shape), full(wa.shape), full(ba.shape), full(wx.shape),
                  full(bx.shape), full(lam.shape), pl.BlockSpec(memory_space=pl.ANY)],
        out_specs=[pl.BlockSpec((ts, 2 * W), lambda i: (rb(i), 2))] + [full(s.shape) for s in small],
        out_shape=[jax.ShapeDtypeStruct(dh.shape, dh.dtype)] + small,
        scratch_shapes=[pltpu.VMEM((SUB, W), F32), pltpu.VMEM((SUB, W), F32), pltpu.VMEM((ts, W), F32)],
        input_output_aliases={13: 0},
        compiler_params=_cparams(("arbitrary",)),
    )(h, h, h, hl, hl, dmix, conv_w, conv_b, wa, ba, wx, bx, lam, dh)


def _xattn_fwd(q, km, vm, *, name, ts=512):
    S, D = q.shape
    M = km.shape[0]
    H = XA_HEADS
    Dh = D // H
    ts = min(ts, S)
    scale = Dh ** -0.5

    def body(q_ref, k_ref, v_ref, o_ref):
        for hd in range(H):
            cols = pl.ds(hd * Dh, Dh)
            s = _bdot(q_ref[:, cols], k_ref[:, cols], NT) * scale
            s = s - jnp.max(s, axis=-1, keepdims=True)
            e = jnp.exp(s)
            p = e / jnp.sum(e, axis=-1, keepdims=True)
            o_ref[:, cols] = _bdot(p, v_ref[:, cols], NN).astype(o_ref.dtype)

    return pl.pallas_call(
        body, name=name, grid=(S // ts,),
        in_specs=[pl.BlockSpec((ts, D), lambda i: (i, 0)), pl.BlockSpec((M, D), lambda i: (0, 0)),
                  pl.BlockSpec((M, D), lambda i: (0, 0))],
        out_specs=pl.BlockSpec((ts, D), lambda i: (i, 0)),
        out_shape=jax.ShapeDtypeStruct((S, D), BF16),
        compiler_params=_cparams(("parallel",)),
    )(q, km, vm)


def _xattn_bwd(q, km, vm, d_o, *, name, ts=512):
    S, D = q.shape
    M = km.shape[0]
    H = XA_HEADS
    Dh = D // H
    ts = min(ts, S)
    scale = Dh ** -0.5

    def body(q_ref, k_ref, v_ref, do_ref, dq_ref, dk_ref, dv_ref):
        i = pl.program_id(0)

        @pl.when(i == 0)
        def _():
            dk_ref[...] = jnp.zeros_like(dk_ref)
            dv_ref[...] = jnp.zeros_like(dv_ref)

        for hd in range(H):
            cols = pl.ds(hd * Dh, Dh)
            qh, kh, vh, doh = q_ref[:, cols], k_ref[:, cols], v_ref[:, cols], do_ref[:, cols]
            s = _bdot(qh, kh, NT) * scale
            s = s - jnp.max(s, axis=-1, keepdims=True)
            e = jnp.exp(s)
            p = e / jnp.sum(e, axis=-1, keepdims=True)
            dp = _bdot(doh, vh, NT)
            ds = p * (dp - jnp.sum(dp * p, axis=-1, keepdims=True)) * scale
            dq_ref[:, cols] = _bdot(ds, kh, NN).astype(dq_ref.dtype)
            dk_ref[:, cols] += _bdot(ds, qh, TN)
            dv_ref[:, cols] += _bdot(p, doh, TN)

    row = pl.BlockSpec((ts, D), lambda i: (i, 0))
    mem = pl.BlockSpec((M, D), lambda i: (0, 0))
    return pl.pallas_call(
        body, name=name, grid=(S // ts,), in_specs=[row, mem, mem, row], out_specs=[row, mem, mem],
        out_shape=[jax.ShapeDtypeStruct((S, D), BF16), jax.ShapeDtypeStruct((M, D), F32),
                   jax.ShapeDtypeStruct((M, D), F32)],
        compiler_params=_cparams(("arbitrary",)),
    )(q, km, vm, d_o)


def _conv_rows(v, tail, cw_ref, cb_ref):
    K = cw_ref.shape[0]
    sh = [_shift_down(v, tail, K - 1 - k) for k in range(K)]
    out = cb_ref[...]
    for k in range(K):
        out = out + cw_ref[k:k + 1, :] * sh[k]
    return out, sh


FFN_SUB = 256


def _ffn_up_gate(xn, w_up, cw, cb, *, name, tm=1024, tn=512):
    S, D = xn.shape
    F2 = w_up.shape[1]
    F = F2 // 2
    tm, tn = min(tm, S), min(tn, F)
    sub = min(FFN_SUB, tm)
    nj = F // tn
    K = cw.shape[0]

    def body(x_ref, wa_ref, wb_ref, cwa_ref, cwb_ref, cba_ref, cbb_ref, act_ref, ha_ref, hb_ref, ac_ref, bc_ref, ta, tb):
        i = pl.program_id(1)

        @pl.when(i == 0)
        def _():
            ta[...] = jnp.zeros_like(ta)
            tb[...] = jnp.zeros_like(tb)

        tail_a, tail_b = ta[...], tb[...]
        for s in range(tm // sub):
            rows = pl.ds(s * sub, sub)
            xs = x_ref[rows, :]
            ha = _bdot(xs, wa_ref[...], NN)
            hb = _bdot(xs, wb_ref[...], NN)
            ac, _ = _conv_rows(ha, tail_a, cwa_ref, cba_ref)
            bc, _ = _conv_rows(hb, tail_b, cwb_ref, cbb_ref)
            tail_a, tail_b = ha[sub - SUB:sub], hb[sub - SUB:sub]
            ha_ref[rows, :] = ha
            hb_ref[rows, :] = hb
            ac_ref[rows, :] = ac
            bc_ref[rows, :] = bc
            act_ref[rows, :] = (ac * _sigmoid(ac) * bc).astype(act_ref.dtype)
        ta[...] = tail_a
        tb[...] = tail_b

    blk = pl.BlockSpec((tm, tn), lambda j, i: (i, j))
    return pl.pallas_call(
        body, name=name, grid=(nj, S // tm),
        in_specs=[pl.BlockSpec((tm, D), lambda j, i: (i, 0)),
                  pl.BlockSpec((D, tn), lambda j, i: (0, j)), pl.BlockSpec((D, tn), lambda j, i: (0, j + nj)),
                  pl.BlockSpec((K, tn), lambda j, i: (0, j)), pl.BlockSpec((K, tn), lambda j, i: (0, j + nj)),
                  pl.BlockSpec((1, tn), lambda j, i: (0, j)), pl.BlockSpec((1, tn), lambda j, i: (0, j + nj))],
        out_specs=[blk] * 5,
        out_shape=[jax.ShapeDtypeStruct((S, F), BF16)] + [jax.ShapeDtypeStruct((S, F), F32)] * 4,
        scratch_shapes=[pltpu.VMEM((SUB, tn), F32), pltpu.VMEM((SUB, tn), F32)],
        compiler_params=_cparams(("parallel", "arbitrary")),
    )(xn, w_up, w_up, cw, cw, cb, cb)


def _ffn_bwd(dx, w_down, hh_a, hh_b, c_a, c_b, act, xn, cw, *, name, tm=1024, tn=256):
    S, D = dx.shape
    F = hh_a.shape[1]
    tm, tn = min(tm, S), min(tn, F)
    sub = min(FFN_SUB, tm)
    nj = F // tn
    nb = S // tm
    K = cw.shape[0]

    def body(dx_ref, wd_ref, a_ref, b_ref, ac_ref, bc_ref, act_ref, xn_ref, cwa_ref, cwb_ref,
             da_ref, db_ref, ga_ref, gb_ref, dwd_ref, dwu_ref, ha, hb, acc_d, acc_a, acc_b):
        i = pl.program_id(1)

        @pl.when(i == 0)
        def _():
            for ref in (ha, hb, ga_ref, gb_ref, acc_d, acc_a, acc_b):
                ref[...] = jnp.zeros_like(ref)

        rid8 = _row_iota((SUB, tn))
        heads = [ha[...], hb[...]]
        gsums = [jnp.zeros((SUB, tn), F32), jnp.zeros((SUB, tn), F32)]
        for s in reversed(range(tm // sub)):
            rows = pl.ds(s * sub, sub)
            dv = _bdot(dx_ref[rows, :], wd_ref[...], NT)
            ac, bc = ac_ref[rows, :], bc_ref[rows, :]
            sg = _sigmoid(ac)
            d_bc = dv * ac * sg
            d_ac = dv * bc * sg * (1.0 + ac * (1.0 - sg))
            for which, (d_c, h_ref, cw_ref, o_ref) in enumerate(((d_ac, a_ref, cwa_ref, da_ref),
                                                                 (d_bc, b_ref, cwb_ref, db_ref))):
                ahead = [d_c] + [_shift_up(d_c, heads[which], j) for j in range(1, K)]
                heads[which] = d_c[0:SUB]
                d_in = cw_ref[K - 1:K, :] * d_c
                for j in range(1, K):
                    d_in = d_in + cw_ref[K - 1 - j:K - j, :] * ahead[j]
                o_ref[rows, :] = d_in.astype(o_ref.dtype)
                hv = h_ref[rows, :]
                gsum = gsums[which] + jnp.where(rid8 == K, jnp.sum(d_c, axis=0, keepdims=True), 0.0)
                for k in range(K):
                    gsum = gsum + jnp.where(rid8 == k, jnp.sum(ahead[K - 1 - k] * hv, axis=0, keepdims=True), 0.0)
                gsums[which] = gsum
        ha[...], hb[...] = heads
        ga_ref[...] += gsums[0]
        gb_ref[...] += gsums[1]
        acc_d[...] += _bdot(act_ref[...], dx_ref[...], TN)
        acc_a[...] += _bdot(xn_ref[...], da_ref[...], TN)
        acc_b[...] += _bdot(xn_ref[...], db_ref[...], TN)

        @pl.when(i == nb - 1)
        def _():
            dwd_ref[...] = acc_d[...].astype(dwd_ref.dtype)
            dwu_ref[0] = acc_a[...].astype(dwu_ref.dtype)
            dwu_ref[1] = acc_b[...].astype(dwu_ref.dtype)

    rb = lambda i: nb - 1 - i
    blk = pl.BlockSpec((tm, tn), lambda j, i: (rb(i), j))
    acc = pl.BlockSpec((SUB, tn), lambda j, i: (0, j))
    rows_d = pl.BlockSpec((tm, D), lambda j, i: (rb(i), 0))
    return pl.pallas_call(
        body, name=name, grid=(nj, nb),
        in_specs=[rows_d, pl.BlockSpec((tn, D), lambda j, i: (j, 0)), blk, blk, blk, blk, blk, rows_d,
                  pl.BlockSpec((K, tn), lambda j, i: (0, j)), pl.BlockSpec((K, tn), lambda j, i: (0, j + nj))],
        out_specs=[blk, blk, acc, acc, pl.BlockSpec((tn, D), lambda j, i: (j, 0)),
                   pl.BlockSpec((2, D, tn), lambda j, i: (0, 0, j))],
        out_shape=[jax.ShapeDtypeStruct((S, F), BF16), jax.ShapeDtypeStruct((S, F), BF16),
                   jax.ShapeDtypeStruct((SUB, F), F32), jax.ShapeDtypeStruct((SUB, F), F32),
                   jax.ShapeDtypeStruct((F, D), BF16), jax.ShapeDtypeStruct((2, D, F), BF16)],
        scratch_shapes=[pltpu.VMEM((SUB, tn), F32), pltpu.VMEM((SUB, tn), F32), pltpu.VMEM((tn, D), F32),
                        pltpu.VMEM((D, tn), F32), pltpu.VMEM((D, tn), F32)],
        compiler_params=_cparams(("parallel", "arbitrary")),
    )(dx, w_down, hh_a, hh_b, c_a, c_b, act, xn, cw, cw)


ADAM_BLOCK_ELEMS = 256 * 1024


def _adamw(w, m, v, parts, *, name):
    R, C = w.shape
    n = parts.shape[0]
    tr = R
    for cand in (1024, 512, 256, 128, 64, 32, 16):
        if R % cand == 0 and cand * C <= ADAM_BLOCK_ELEMS:
            tr = cand
            break
    c1 = 1.0 - ADAM_B1 ** ADAM_STEP
    c2 = 1.0 - ADAM_B2 ** ADAM_STEP

    def body(w_ref, m_ref, v_ref, p_ref, g_ref, d_ref, nm_ref, nv_ref):
        g = p_ref[0].astype(F32)
        for k in range(1, n):
            g = g + p_ref[k].astype(F32)
        m_new = ADAM_B1 * m_ref[...] + (1.0 - ADAM_B1) * g
        v_new = ADAM_B2 * v_ref[...] + (1.0 - ADAM_B2) * (g * g)
        m_hat = m_new / c1
        v_hat = v_new / c2
        g_ref[...] = g
        d_ref[...] = -ADAM_LR * (m_hat / (jnp.sqrt(v_hat) + ADAM_EPS) + ADAM_WD * w_ref[...])
        nm_ref[...] = m_new
        nv_ref[...] = v_new

    blk = pl.BlockSpec((tr, C), lambda i: (i, 0))
    sds = jax.ShapeDtypeStruct((R, C), F32)
    return pl.pallas_call(
        body, name=name, grid=(R // tr,),
        in_specs=[blk, blk, blk, pl.BlockSpec((n, tr, C), lambda i: (0, i, 0))],
        out_specs=[blk, blk, blk, blk], out_shape=[sds, sds, sds, sds],
        compiler_params=_cparams(("parallel",)),
    )(w, m, v, parts)


def _mesh_place():
    x, y, c = lax.axis_index("x"), lax.axis_index("y"), lax.axis_index("c")
    others = [(1 - x, y), (x, 1 - y), (1 - x, 1 - y)]
    return x, y, c, others


HBM_SPEC = pl.BlockSpec(memory_space=pltpu.HBM)
SEM_SPEC = pl.BlockSpec(memory_space=pltpu.SEMAPHORE)
ANY_SPEC = pl.BlockSpec(memory_space=pl.ANY)
EFFECT = pltpu.SideEffectType.DATAFLOW_SIDE_EFFECTING


def _in_hbm(a):
    return pltpu.with_memory_space_constraint(a, pltpu.HBM)


def _split_start(srcs, lands, copies, n_cp, *, name):
    n_s, n_l = len(srcs), len(lands)

    def body(*refs):
        src_refs, land_refs = refs[:n_s], refs[n_s:n_s + n_l]
        ssem, rsem = refs[n_s + n_l], refs[n_s + n_l + 1]
        token = refs[-1]
        for outgoing, _ in copies(src_refs, land_refs, ssem, rsem):
            outgoing.start()
        token[...] = jnp.zeros_like(token)

    outs = pl.pallas_call(
        body, name=name,
        out_shape=(pltpu.SemaphoreType.DMA((n_cp,)), pltpu.SemaphoreType.DMA((n_cp,)),
                   *[pltpu.HBM(a.shape, a.dtype) for a in srcs], *[pltpu.HBM(a.shape, a.dtype) for a in lands],
                   jax.ShapeDtypeStruct((SUB, LANE), F32)),
        in_specs=[HBM_SPEC] * (n_s + n_l),
        out_specs=(SEM_SPEC, SEM_SPEC, *[HBM_SPEC] * (n_s + n_l), pl.BlockSpec(memory_space=pltpu.VMEM)),
        input_output_aliases={i: 2 + i for i in range(n_s + n_l)},
        compiler_params=pltpu.CompilerParams(has_side_effects=EFFECT),
    )(*[_in_hbm(a) for a in srcs], *[_in_hbm(a) for a in lands])
    ssem, rsem = outs[0], outs[1]
    return ssem, rsem, list(outs[2:2 + n_s]), list(outs[2 + n_s:2 + n_s + n_l]), outs[-1]


def _split_wait(srcs, lands, ssem, rsem, after, copies, *, name):
    n_s, n_l = len(srcs), len(lands)

    def body(*refs):
        src_refs, land_refs = refs[:n_s], refs[n_s:n_s + n_l]
        s_ref, r_ref = refs[n_s + n_l], refs[n_s + n_l + 1]
        for outgoing, incoming in copies(src_refs, land_refs, s_ref, r_ref):
            outgoing.wait_send()
            incoming.wait_recv()

    outs = pl.pallas_call(
        body, name=name,
        out_shape=(*[pltpu.HBM(a.shape, a.dtype) for a in srcs], *[pltpu.HBM(a.shape, a.dtype) for a in lands]),
        in_specs=[HBM_SPEC] * (n_s + n_l) + [SEM_SPEC, SEM_SPEC, ANY_SPEC], out_specs=[HBM_SPEC] * (n_s + n_l),
        input_output_aliases={i: i for i in range(n_s + n_l)},
        compiler_params=pltpu.CompilerParams(has_side_effects=EFFECT),
    )(*srcs, *lands, ssem, rsem, after)
    return list(outs[:n_s]), list(outs[n_s:])


PLACE_BLOCK_ELEMS = 512 * 1024


def _place_rows(r, w):
    return _div_tile(r, max(16, PLACE_BLOCK_ELEMS // w), 16)


def _cast_place(shard, chip, axis, after, *, name):
    r, w = shard.shape
    tr = _place_rows(r, w)
    nb = r // tr
    full = (r * N_CHIP, w) if axis == 0 else (r, w * N_CHIP)
    has_after = after is not None

    def body(chip_ref, s_ref, *rest):
        rest[-1][...] = s_ref[...].astype(BF16)

    out_map = (lambda i, ch: (ch[0] * nb + i, 0)) if axis == 0 else (lambda i, ch: (i, ch[0]))
    grid_spec = pltpu.PrefetchScalarGridSpec(
        num_scalar_prefetch=1, grid=(nb,),
        in_specs=[pl.BlockSpec((tr, w), lambda i, ch: (i, 0))] + has_after * [ANY_SPEC],
        out_specs=pl.BlockSpec((tr, w), out_map))
    return pl.pallas_call(body, name=name, grid_spec=grid_spec, out_shape=jax.ShapeDtypeStruct(full, BF16),
                          compiler_params=_cparams(("parallel",)))(chip, shard, *(has_after * [after]))


def _grad_shard_shape(g, axis):
    if g.ndim == 3:
        return g.shape[1], 2 * g.shape[2] // N_CHIP
    return (g.shape[0] // N_CHIP, g.shape[1]) if axis == 0 else (g.shape[0], g.shape[1] // N_CHIP)


def _slot_place(g, ids, axis, *, name):
    r, w = _grad_shard_shape(g, axis)
    tr = _place_rows(r, w)
    nb = r // tr

    def body(ids_ref, g_ref, o_ref):
        o_ref[...] = g_ref[...]

    if g.ndim == 3:
        in_spec = pl.BlockSpec((None, tr, w), lambda i, ids_: (ids_[0] // 2, i, ids_[0] % 2))
    elif axis == 0:
        in_spec = pl.BlockSpec((tr, w), lambda i, ids_: (ids_[0] * nb + i, 0))
    else:
        in_spec = pl.BlockSpec((tr, w), lambda i, ids_: (i, ids_[0]))
    grid_spec = pltpu.PrefetchScalarGridSpec(
        num_scalar_prefetch=1, grid=(nb,), in_specs=[in_spec],
        out_specs=pl.BlockSpec((None, tr, w), lambda i, ids_: (ids_[1], i, 0)))
    return pl.pallas_call(body, name=name, grid_spec=grid_spec, out_shape=jax.ShapeDtypeStruct((N_DEV, r, w), g.dtype),
                          compiler_params=_cparams(("parallel",)))(ids, g)


class _WeightGather:
    def __init__(self, placed, shard_shapes, axes, splits, tag):
        self.placed, self.shard_shapes, self.axes, self.splits, self.tag = list(placed), shard_shapes, axes, splits, tag
        self.n = len(placed)

    def _region(self, land_refs, it, chip, half):
        r, w = self.shard_shapes[it]
        by_rows = self.axes[it] == 0
        if self.splits[it] and half is not None:
            rows = pl.ds(pl.multiple_of(half * (r // 2) + (chip * r if by_rows else 0), 16), r // 2)
        else:
            rows = pl.ds(chip * r if by_rows else 0, r)
        cols = pl.ds(0, w) if by_rows else pl.ds(pl.multiple_of(chip * w, LANE), w)
        return land_refs[it].at[rows, cols]

    def _ici(self, src_refs, land_refs, ssem, rsem):
        x, y, c, others = _mesh_place()
        pairs = []
        for it in range(self.n):
            for j, chip in enumerate(others):
                def mk(chip_from, it=it, j=j, chip=chip):
                    return pltpu.make_async_remote_copy(
                        src_ref=self._region(land_refs, it, 2 * x + y, c), dst_ref=self._region(land_refs, it, chip_from, c),
                        send_sem=ssem.at[3 * it + j], recv_sem=rsem.at[3 * it + j], device_id=(*chip, c),
                        device_id_type=MESH)
                pairs.append((mk(2 * x + y), mk(2 * chip[0] + chip[1])))
        return pairs

    def start(self):
        self.ssem, self.rsem, _, self.lands, token = _split_start(
            [], self.placed, self._ici, 3 * self.n, name="gather_start_" + self.tag)
        return token

    def _d2d(self, src_refs, land_refs, ssem, rsem):
        x, y, c, others = _mesh_place()
        pairs = []
        for it in range(self.n):
            if self.splits[it]:
                for chip in others:
                    def mk(half, it=it, chip=chip, k=len(pairs)):
                        reg = self._region(land_refs, it, 2 * chip[0] + chip[1], half)
                        return pltpu.make_async_remote_copy(src_ref=reg, dst_ref=reg, send_sem=ssem.at[k], recv_sem=rsem.at[k],
                                                            device_id=(x, y, 1 - c), device_id_type=MESH)
                    pairs.append((mk(c), mk(1 - c)))
        return pairs

    def forward(self, after):
        _, lands = _split_wait([], self.lands, self.ssem, self.rsem, after, self._ici,
                               name="gather_wait_" + self.tag)
        self.fsem, self.frsem, _, self.lands, token = _split_start(
            [], lands, self._d2d, 3 * sum(self.splits), name="gather_fwd_" + self.tag)
        return token

    def finish_forward(self, after):
        _, lands = _split_wait([], self.lands, self.fsem, self.frsem, after, self._d2d,
                               name="gather_fwd_wait_" + self.tag)
        return lands

    def finish(self, after):
        _, lands = _split_wait([], self.lands, self.ssem, self.rsem, after, self._ici,
                               name="gather_wait_" + self.tag)
        n = self.n
        n_fwd = 3 * sum(self.splits)
        if n_fwd == 0:
            return lands

        def body(*refs):
            out_refs = refs[n:2 * n]
            fsend, frecv = refs[2 * n:]
            x, y, c, others = _mesh_place()
            sibling = (x, y, 1 - c)

            def fwd(it, slot, chip, half):
                reg = self._region(out_refs, it, 2 * chip[0] + chip[1], half)
                return pltpu.make_async_remote_copy(src_ref=reg, dst_ref=reg, send_sem=fsend.at[slot],
                                                    recv_sem=frecv.at[slot], device_id=sibling, device_id_type=MESH)

            sends, recvs = [], []
            for it in range(n):
                if self.splits[it]:
                    for chip in others:
                        sends.append(fwd(it, len(sends), chip, c))
                        recvs.append(fwd(it, len(recvs), chip, 1 - c))
            for cp in sends:
                cp.start()
            for cp in recvs:
                cp.wait_recv()
            for cp in sends:
                cp.wait_send()

        fulls = pl.pallas_call(
            body, name="gather_d2d_" + self.tag, in_specs=[ANY_SPEC] * n, out_specs=[ANY_SPEC] * n,
            out_shape=[jax.ShapeDtypeStruct(a.shape, a.dtype) for a in lands],
            scratch_shapes=[pltpu.SemaphoreType.DMA((n_fwd,)), pltpu.SemaphoreType.DMA((n_fwd,))],
            input_output_aliases={i: i for i in range(n)},
        )(*lands)
        return list(fulls)


class _GradGather:
    def __init__(self, grads, axes, tag):
        self.grads, self.axes, self.tag = list(grads), axes, tag
        self.n = len(grads)
        self.shard_shapes = [_grad_shard_shape(g, ax) for g, ax in zip(grads, axes)]

    def _piece(self, src_refs, it, chip):
        r, w = self.shard_shapes[it]
        if self.grads[it].ndim == 3:
            return src_refs[it].at[chip // 2, :, pl.ds(pl.multiple_of((chip % 2) * w, LANE), w)]
        if self.axes[it] == 0:
            return src_refs[it].at[pl.ds(pl.multiple_of(chip * r, 16), r), :]
        return src_refs[it].at[:, pl.ds(pl.multiple_of(chip * w, LANE), w)]

    PER_ITEM = 4

    def _remote(self, src_refs, land_refs, ssem, rsem):
        x, y, c, others = _mesh_place()
        me = 4 * x + 2 * y + c
        pairs = []
        for it in range(self.n):
            def mk(k, piece_chip, slot, to, it=it):
                return pltpu.make_async_remote_copy(
                    src_ref=self._piece(src_refs, it, piece_chip), dst_ref=land_refs[it].at[slot],
                    send_sem=ssem.at[self.PER_ITEM * it + k], recv_sem=rsem.at[self.PER_ITEM * it + k], device_id=to,
                    device_id_type=MESH)
            for j, chip in enumerate(others):
                chip_id = 2 * chip[0] + chip[1]
                pairs.append((mk(j, chip_id, me, (*chip, c)), mk(j, chip_id, 2 * chip_id + c, (*chip, c))))
            sibling = (x, y, 1 - c)
            pairs.append((mk(3, 2 * x + y, me, sibling), mk(3, 2 * x + y, 4 * x + 2 * y + 1 - c, sibling)))
        return pairs

    def start(self):
        x, y, c = lax.axis_index("x"), lax.axis_index("y"), lax.axis_index("c")
        ids = jnp.stack([2 * x + y, 4 * x + 2 * y + c]).astype(jnp.int32)
        lands = [_slot_place(g, ids, ax, name="grads_own_%s%d" % (self.tag, it))
                 for it, (g, ax) in enumerate(zip(self.grads, self.axes))]
        self.ssem, self.rsem, self.srcs, self.lands, token = _split_start(
            self.grads, lands, self._remote, self.PER_ITEM * self.n, name="grads_start_" + self.tag)
        return token

    def _forward(self, src_refs, land_refs, ssem, rsem):
        x, y, c, others = _mesh_place()
        pairs = []
        for it in range(self.n):
            for j, ch in enumerate(others):
                def mk(slot, it=it, j=j):
                    return pltpu.make_async_remote_copy(
                        src_ref=land_refs[it].at[slot], dst_ref=land_refs[it].at[slot], send_sem=ssem.at[3 * it + j],
                        recv_sem=rsem.at[3 * it + j], device_id=(x, y, 1 - c), device_id_type=MESH)
                pairs.append((mk(4 * ch[0] + 2 * ch[1] + c), mk(4 * ch[0] + 2 * ch[1] + 1 - c)))
        return pairs

    def forward(self, after):
        _, lands = _split_wait(self.srcs, self.lands, self.ssem, self.rsem, after, self._remote,
                               name="grads_wait_" + self.tag)
        self.fsem, self.frsem, _, self.lands, token = _split_start(
            [], lands, self._forward, 3 * self.n, name="grads_fwd_" + self.tag)
        return token

    def finish(self, after):
        _, lands = _split_wait([], self.lands, self.fsem, self.frsem, after, self._forward,
                               name="grads_fwd_wait_" + self.tag)
        return lands


def _allreduce_small(vec, *, name):
    R, L = vec.shape

    def body(v_ref, o_ref, buf, send, recv, lsem):
        x, y, c, others = _mesh_place()
        me = 4 * x + 2 * y + c
        sibling = (x, y, 1 - c)

        def copy(k, slot, to, src=None):
            return pltpu.make_async_remote_copy(
                src_ref=buf.at[slot] if src is None else src, dst_ref=buf.at[slot], send_sem=send.at[k],
                recv_sem=recv.at[k], device_id=to, device_id_type=MESH)

        def slot_of(chip, core):
            return 4 * chip[0] + 2 * chip[1] + core

        mine = pltpu.make_async_copy(v_ref, buf.at[me], lsem)
        mine.start()
        first = [copy(0, me, sibling, src=v_ref)]
        first += [copy(1 + j, me, (*chip, c), src=v_ref) for j, chip in enumerate(others)]
        for cp in first:
            cp.start()
        passed = [copy(4 + j, slot_of(chip, c), sibling) for j, chip in enumerate(others)]
        for j, chip in enumerate(others):
            copy(1 + j, slot_of(chip, c), (*chip, c)).wait_recv()
            passed[j].start()
        copy(0, slot_of((x, y), 1 - c), sibling).wait_recv()
        for j, chip in enumerate(others):
            copy(4 + j, slot_of(chip, 1 - c), sibling).wait_recv()
        for cp in first + passed:
            cp.wait_send()
        mine.wait()
        total = buf[0]
        for k in range(1, N_DEV):
            total = total + buf[k]
        o_ref[...] = total

    return pl.pallas_call(
        body, name=name, in_specs=[pl.BlockSpec(memory_space=pltpu.VMEM)],
        out_specs=pl.BlockSpec(memory_space=pltpu.VMEM), out_shape=jax.ShapeDtypeStruct((R, L), F32),
        scratch_shapes=[pltpu.VMEM((N_DEV, R, L), F32), pltpu.SemaphoreType.DMA((7,)), pltpu.SemaphoreType.DMA((7,)),
                        pltpu.SemaphoreType.DMA],
        compiler_params=pltpu.CompilerParams(vmem_limit_bytes=VMEM_LIMIT),
    )(vec)


PACK_ALIGN = 1024


def _pack(arrs, row_multiple):
    flat = []
    for a in arrs:
        f = a.reshape(-1).astype(F32)
        flat.append(jnp.pad(f, (0, (-f.shape[0]) % PACK_ALIGN)))
    v = jnp.concatenate(flat)
    v = jnp.pad(v, (0, (-v.shape[0]) % (LANE * row_multiple)))
    return v.reshape(-1, LANE)


def _unpack(v, shapes):
    flat = v.reshape(-1)
    out, off = [], 0
    for s in shapes:
        size = math.prod(s)
        out.append(flat[off:off + size].reshape(s))
        off += size + (-size) % PACK_ALIGN
    return out


def _div_tile(dim, cap, mult=LANE):
    best = None
    for cand in range(mult, min(cap, dim) + 1, mult):
        if dim % cand == 0:
            best = cand
    return dim if best is None else best


WEIGHT_NAMES = ('norm1_g', 'w_in', 'ret_g', 'rg_conv_w', 'rg_conv_b', 'rg_wa', 'rg_ba', 'rg_wx', 'rg_bx', 'rg_lambda',
                'w_out', 'norm2_g', 'norm_mem_g', 'xa_wq', 'xa_wk', 'xa_wv', 'xa_wo', 'norm3_g', 'ffn_w_up',
                'ffn_conv_w', 'ffn_conv_b', 'ffn_w_down', 'final_g')
BIG_AXIS = {'w_in': 1, 'w_out': 0, 'xa_wq': 0, 'xa_wk': 0, 'xa_wv': 0, 'xa_wo': 0, 'ffn_w_up': 1, 'ffn_w_down': 0}
SMALL_SHARDED = ('rg_conv_w', 'ffn_conv_w')


def _step(x, mem, positions, loss_target, W, Mo, Vo):
    S, D = x.shape[1], x.shape[2]
    xs, mems, tgt = x[0], mem[0], loss_target[0]
    n_mem = mems.shape[0]
    pos_col = positions.reshape(S, 1)
    chip = 2 * lax.axis_index("x") + lax.axis_index("y")

    big = list(BIG_AXIS)
    shards = {n: W[n][0] for n in big}
    G = {}
    gather_groups = (('w_in', 'rg_conv_w'), ('w_out', 'xa_wq', 'xa_wk', 'xa_wv', 'xa_wo'),
                     ('ffn_w_up', 'ffn_conv_w'), ('ffn_w_down',))
    gathers, tok = [], None
    chip1 = jnp.reshape(chip, (1,)).astype(jnp.int32)
    for gi, names in enumerate(gather_groups):
        placed = []
        for n in names:
            if n in BIG_AXIS:
                placed.append(_cast_place(shards[n], chip1, BIG_AXIS[n], tok, name="place_" + n))
            else:
                s = W[n][0] if tok is None else W[n][0] + tok[0, 0]
                full = lax.empty((s.shape[0], s.shape[1] * N_CHIP), s.dtype)
                placed.append(lax.dynamic_update_slice(full, s, (0, chip * s.shape[1])))
        ag = _WeightGather(placed, [W[n][0].shape for n in names], [BIG_AXIS.get(n, 1) for n in names],
                           [n in BIG_AXIS for n in names], "g%d" % gi)
        tok = ag.start()
        gathers.append(ag)

    def finish_gather(gi, after):
        G.update(zip(gather_groups[gi], gathers[gi].finish(after)))

    def finish_forward(gi, after):
        G.update(zip(gather_groups[gi], gathers[gi].finish_forward(after)))

    R = W['ret_g'].shape[1]
    Wl = W['rg_lambda'].shape[1]
    IN = W['w_in'].shape[2] * N_CHIP
    F2 = W['ffn_w_up'].shape[2] * N_CHIP
    F = F2 // 2

    norm1_g, norm2_g, norm3_g = W['norm1_g'] + tok[0, 0], W['norm2_g'], W['norm3_g']
    norm_mem_g, final_g, ret_g = W['norm_mem_g'], W['final_g'].reshape(1, D), W['ret_g']
    rg_cb = W['rg_conv_b']
    wa, wx = W['rg_wa'][0], W['rg_wx'][0]
    ba, bx = W['rg_ba'].reshape(1, Wl), W['rg_bx'].reshape(1, Wl)
    lam = W['rg_lambda']
    ffn_cb = W['ffn_conv_b']

    def fwd_mm(a, wname, N, K, **kw):
        return _mm(a, G[wname], mode="nn", M=a.shape[0], N=N, K=K, tm=_div_tile(a.shape[0], 1024),
                   tn=1024 if K <= 3072 else 512, tk=K, **kw)

    def fwd_mm_norm(a, wname, res, g, name):
        return _mm(a, G[wname], mode="nn", M=a.shape[0], N=D, K=a.shape[1], tm=512, tn=D, tk=_div_tile(a.shape[1], 2048),
                   out_dtype=F32, res=res, norm_g=g, name=name)

    def bwd_x_mm(d, wname, N, K, **kw):
        return _mm(d, G[wname], mode="nt", M=d.shape[0], N=N, K=K, tm=_div_tile(d.shape[0], 1024),
                   tn=_div_tile(N, 1024 if K <= 3072 else 512, 256), tk=K, **kw)

    def bwd_w_mm(a, d, M, N, **kw):
        Ks = a.shape[0]
        return _mm(a, d, mode="tn", M=M, N=N, K=Ks, out_dtype=BF16, tm=_div_tile(M, 1024, 256),
                   tn=_div_tile(N, 1024, 256), tk=_div_tile(Ks, 4096 if d.dtype == BF16 else 1024), **kw)

    xn1 = _rmsnorm_fwd(xs, norm1_g, name="norm1_fwd")
    half = (R // RET_HEADS) // 2
    inv = (ROPE_BASE ** (-jnp.arange(half, dtype=F32) / half)).reshape(1, half)
    cos, sin = _rope_table(pos_col, inv + tok[0, 0], name="rope_table")
    finish_gather(0, cos)
    rg_cw = G['rg_conv_w']
    h = fwd_mm(xn1, 'w_in', IN, D, out_dtype=F32, name="mm_in")
    hl, mix = _lru_fwd(h, rg_cw, rg_cb, wa, ba, wx, bx, lam, name="lru_fwd")
    t1 = gathers[1].forward(hl)
    ret_raw, states, mix = _ret_fwd(h, cos, sin, ret_g + t1[0, 0], mix, name="ret_fwd")
    finish_forward(1, mix)
    x1, xn2 = fwd_mm_norm(mix, 'w_out', xs, norm2_g, "mm_out")
    memn = _rmsnorm_fwd(mems, norm_mem_g, name="norm_mem_fwd")
    km = fwd_mm(memn, 'xa_wk', D, D, out_dtype=BF16, name="mm_k")
    vm = fwd_mm(memn, 'xa_wv', D, D, out_dtype=BF16, name="mm_v")
    t2 = gathers[2].forward(x1)
    q = fwd_mm(xn2, 'xa_wq', D, D, out_dtype=BF16, after=t2, name="mm_q")
    o = _xattn_fwd(q, km, vm, name="xattn_fwd")
    x2, xn3 = fwd_mm_norm(o, 'xa_wo', x1, norm3_g, "mm_o")
    finish_forward(2, xn3)
    t3 = gathers[3].forward(xn3)
    ffn_cw = G['ffn_conv_w']
    act, hh_a, hh_b, hc_a, hc_b = _ffn_up_gate(xn3, G['ffn_w_up'], ffn_cw, ffn_cb + t3[0, 0], name="ffn_up_gate")
    finish_forward(3, act)
    x3 = fwd_mm(act, 'ffn_w_down', D, F, out_dtype=F32, res=x2, name="mm_down")
    dx3, d_final, loss8, dx3h = _final_loss(x3, tgt, final_g, name="final_loss")

    gw = {}
    grad_groups = []

    def start_grads(names, tag):
        gg = _GradGather([gw[n] for n in names], [BIG_AXIS[n] for n in names], tag)
        grad_groups.append((names, gg))
        return gg.start()

    dhh_a, dhh_b, gcw_a, gcw_b, gw['ffn_w_down'], gw['ffn_w_up'] = _ffn_bwd(
        dx3h, G['ffn_w_down'], hh_a, hh_b, hc_a, hc_b, act, xn3, ffn_cw, name="ffn_bwd")
    tok_a = start_grads(('ffn_w_down', 'ffn_w_up'), "a")
    dxn3 = bwd_x_mm(dhh_a, 'ffn_w_up', D, F, out_dtype=F32, after=tok_a, name="mm_dxn3_a")
    dxn3 = bwd_x_mm(dhh_b, 'ffn_w_up', D, F, out_dtype=F32, b_off=(0, F), res=dxn3, name="mm_dxn3_b")
    dx2, d_norm3, dx2h = _rmsnorm_bwd(x2, dxn3, norm3_g, dx3, name="norm3_bwd", emit_bf16=True)
    Kc = ffn_cw.shape[0]
    d_ffn_cw = jnp.concatenate([gcw_a[:Kc], gcw_b[:Kc]], axis=1)
    d_ffn_cb = jnp.concatenate([gcw_a[Kc:Kc + 1], gcw_b[Kc:Kc + 1]], axis=1)

    d_o = bwd_x_mm(dx2h, 'xa_wo', D, D, out_dtype=BF16, name="mm_do")
    gw['xa_wo'] = bwd_w_mm(o, dx2h, D, D, name="mm_dw_o")
    dq, dk, dv = _xattn_bwd(q, km, vm, d_o, name="xattn_bwd")
    gw['xa_wq'] = bwd_w_mm(xn2, dq, D, D, name="mm_dw_q")
    dxn2 = bwd_x_mm(dq, 'xa_wq', D, D, out_dtype=F32, name="mm_dxn2")
    gw['xa_wk'] = bwd_w_mm(memn, dk, D, D, name="mm_dw_k")
    gw['xa_wv'] = bwd_w_mm(memn, dv, D, D, name="mm_dw_v")
    dmemn = bwd_x_mm(dk, 'xa_wk', D, D, out_dtype=F32, name="mm_dmem_k")
    dmemn = bwd_x_mm(dv, 'xa_wv', D, D, out_dtype=F32, res=dmemn, name="mm_dmem_v")
    _, d_norm_mem = _rmsnorm_bwd(mems, dmemn, norm_mem_g, None, name="norm_mem_bwd")
    dx1, d_norm2, dx1h = _rmsnorm_bwd(x1, dxn2, norm2_g, dx2, name="norm2_bwd", emit_bf16=True)

    gw['w_out'] = bwd_w_mm(mix, dx1h, D, D, name="mm_dw_out")
    tok_b = start_grads(('xa_wo', 'xa_wq', 'xa_wk', 'xa_wv', 'w_out'), "b")
    dmix = bwd_x_mm(dx1h, 'w_out', D, D, out_dtype=F32, after=tok_b, name="mm_dmix")
    dh, d_ret_g = _ret_bwd(h, cos, sin, ret_g, states, ret_raw, dmix, name="ret_bwd")
    dh, d_rcw, d_rcb, d_wa, d_ba, d_wx, d_bx, d_lam = _lru_bwd(
        h, hl, dmix, dh, rg_cw, rg_cb, wa, ba, wx, bx, lam, name="lru_bwd")
    gw['w_in'] = bwd_w_mm(xn1, dh, D, IN, name="mm_dw_in")
    tok_c = start_grads(('w_in',), "c")
    dxn1 = bwd_x_mm(dh, 'w_in', D, IN, out_dtype=F32, after=tok_c, name="mm_dxn1")
    grad_x, d_norm1 = _rmsnorm_bwd(xs, dxn1, norm1_g, dx1, name="norm1_bwd")

    small_parts = {
        'norm1_g': d_norm1, 'ret_g': d_ret_g, 'rg_conv_w': d_rcw[:rg_cw.shape[0]], 'rg_conv_b': d_rcb,
        'rg_wa': d_wa, 'rg_ba': d_ba, 'rg_wx': d_wx, 'rg_bx': d_bx, 'rg_lambda': d_lam, 'norm2_g': d_norm2,
        'norm_mem_g': d_norm_mem, 'norm3_g': d_norm3, 'ffn_conv_w': d_ffn_cw, 'ffn_conv_b': d_ffn_cb,
        'final_g': d_final}
    small = [n for n in WEIGHT_NAMES if n not in BIG_AXIS]
    red_shapes = [(1,)] + [tuple(small_parts[n].shape) for n in small]
    fwd_tok = sum(gg.forward(d_norm1)[0:1, 0:1] for _, gg in grad_groups)
    reduced = _allreduce_small(_pack([loss8[0:1, 0:1] + fwd_tok] + [small_parts[n] for n in small], SUB),
                               name="allreduce_small")
    red = _unpack(reduced, red_shapes)
    loss = red[0][0]
    g_small = dict(zip(small, red[1:]))
    for n in SMALL_SHARDED:
        w_local = W[n].shape[-1]
        g_small[n] = lax.dynamic_slice_in_dim(g_small[n], chip * w_local, w_local, axis=1)

    out_g, out_d, out_m, out_v = {}, {}, {}, {}
    rows = 512
    pk = lambda d: _pack([d[n] for n in small], rows)
    g_pack = _pack([g_small[n] for n in small], rows)
    res_small = _adamw(pk(W), pk(Mo), pk(Vo), g_pack[None], name="adamw_small")
    shapes_small = [tuple(W[n].shape) for n in small]
    for dst, packed in zip((out_g, out_d, out_m, out_v), res_small):
        for n, val in zip(small, _unpack(packed, shapes_small)):
            dst[n] = val
    last = res_small[0]
    for names, gg in grad_groups:
        for n, land in zip(names, gg.finish(last)):
            g, d, m_new, v_new = _adamw(shards[n], Mo[n][0], Vo[n][0], land, name="adamw_" + n)
            out_g[n], out_d[n], out_m[n], out_v[n] = (t.reshape(W[n].shape) for t in (g, d, m_new, v_new))
            last = g
    return (loss, grad_x[None], *[out_g[n] for n in WEIGHT_NAMES], *[out_d[n] for n in WEIGHT_NAMES],
            *[out_m[n] for n in WEIGHT_NAMES], *[out_v[n] for n in WEIGHT_NAMES])


def kernel(x, mem, positions, norm1_g, w_in, ret_g, rg_conv_w, rg_conv_b, rg_wa, rg_ba, rg_wx, rg_bx, rg_lambda, w_out, norm2_g, norm_mem_g, xa_wq, xa_wk, xa_wv, xa_wo, norm3_g, ffn_w_up, ffn_conv_w, ffn_conv_b, ffn_w_down, final_g, loss_target, m_norm1_g, m_w_in, m_ret_g, m_rg_conv_w, m_rg_conv_b, m_rg_wa, m_rg_ba, m_rg_wx, m_rg_bx, m_rg_lambda, m_w_out, m_norm2_g, m_norm_mem_g, m_xa_wq, m_xa_wk, m_xa_wv, m_xa_wo, m_norm3_g, m_ffn_w_up, m_ffn_conv_w, m_ffn_conv_b, m_ffn_w_down, m_final_g, v_norm1_g, v_w_in, v_ret_g, v_rg_conv_w, v_rg_conv_b, v_rg_wa, v_rg_ba, v_rg_wx, v_rg_bx, v_rg_lambda, v_w_out, v_norm2_g, v_norm_mem_g, v_xa_wq, v_xa_wk, v_xa_wv, v_xa_wo, v_norm3_g, v_ffn_w_up, v_ffn_conv_w, v_ffn_conv_b, v_ffn_w_down, v_final_g):
    W = dict(zip(WEIGHT_NAMES, (norm1_g, w_in, ret_g, rg_conv_w, rg_conv_b, rg_wa, rg_ba, rg_wx, rg_bx, rg_lambda, w_out,
                                norm2_g, norm_mem_g, xa_wq, xa_wk, xa_wv, xa_wo, norm3_g, ffn_w_up, ffn_conv_w,
                                ffn_conv_b, ffn_w_down, final_g)))
    Mo = dict(zip(WEIGHT_NAMES, (m_norm1_g, m_w_in, m_ret_g, m_rg_conv_w, m_rg_conv_b, m_rg_wa, m_rg_ba, m_rg_wx, m_rg_bx,
                                 m_rg_lambda, m_w_out, m_norm2_g, m_norm_mem_g, m_xa_wq, m_xa_wk, m_xa_wv, m_xa_wo,
                                 m_norm3_g, m_ffn_w_up, m_ffn_conv_w, m_ffn_conv_b, m_ffn_w_down, m_final_g)))
    Vo = dict(zip(WEIGHT_NAMES, (v_norm1_g, v_w_in, v_ret_g, v_rg_conv_w, v_rg_conv_b, v_rg_wa, v_rg_ba, v_rg_wx, v_rg_bx,
                                 v_rg_lambda, v_w_out, v_norm2_g, v_norm_mem_g, v_xa_wq, v_xa_wk, v_xa_wv, v_xa_wo,
                                 v_norm3_g, v_ffn_w_up, v_ffn_conv_w, v_ffn_conv_b, v_ffn_w_down, v_final_g)))
    return _step(x, mem, positions, loss_target, W, Mo, Vo)
```

```python
import math

import jax
import jax.numpy as jnp
from jax import lax
from jax.experimental import pallas as pl
from jax.experimental.pallas import tpu as pltpu

F32 = jnp.float32
BF16 = jnp.bfloat16

EPS = 1e-6
RET_HEADS = 4
RET_CHUNK = 128
ROPE_BASE = 10000.0
LRU_BLOCKS = 8
LRU_C = 8.0
XA_HEADS = 4

ADAM_LR = 0.001
ADAM_B1 = 0.9
ADAM_B2 = 0.999
ADAM_EPS = 1e-08
ADAM_WD = 0.01
ADAM_STEP = 10

N_DEV = 8
N_CHIP = 4
MESH = pl.DeviceIdType.MESH
SUB = 8
LANE = 128
VMEM_LIMIT = 56 * 1024 * 1024

NN = ((1,), (0,))
NT = ((1,), (1,))
TN = ((0,), (0,))


def _cparams(sem):
    return pltpu.CompilerParams(dimension_semantics=sem, vmem_limit_bytes=VMEM_LIMIT)


def _sigmoid(v):
    return 1.0 / (1.0 + jnp.exp(-v))


def _bdot(a, b, dims):
    return lax.dot_general(a.astype(BF16), b.astype(BF16), (dims, ((), ())), preferred_element_type=F32)


def _row_iota(shape):
    return lax.broadcasted_iota(jnp.int32, shape, 0)


def _shift_down(v, tail, k):
    if k == 0:
        return v
    r = pltpu.roll(v, k, 0)
    rt = pltpu.roll(tail, k, 0)
    first = jnp.where(_row_iota(rt.shape) < k, rt, r[0:SUB])
    return jnp.concatenate([first, r[SUB:]], axis=0)


def _shift_up(v, head, k):
    if k == 0:
        return v
    n = v.shape[0]
    r = pltpu.roll(v, n - k, 0)
    rh = pltpu.roll(head, SUB - k, 0)
    last = jnp.where(_row_iota(rh.shape) >= SUB - k, rh, r[n - SUB:n])
    return jnp.concatenate([r[:n - SUB], last], axis=0)


def _mm(a, b, *, mode, M, N, K, out_dtype, name, tm=512, tn=512, tk=512, b_off=(0, 0), res=None, norm_g=None,
        after=None):
    tm, tn, tk = min(tm, M), min(tn, N), min(tk, K)
    assert M % tm == 0 and N % tn == 0 and K % tk == 0, (name, M, N, K, tm, tn, tk)
    nk = K // tk
    if mode == "nn":
        a_blk, b_blk, dims = (tm, tk), (tk, tn), NN
        a_map = lambda i, j, k: (i, k)
        b_map = lambda i, j, k: (k + b_off[0] // tk, j + b_off[1] // tn)
    elif mode == "nt":
        a_blk, b_blk, dims = (tm, tk), (tn, tk), NT
        a_map = lambda i, j, k: (i, k)
        b_map = lambda i, j, k: (j + b_off[0] // tn, k + b_off[1] // tk)
    else:
        a_blk, b_blk, dims = (tk, tm), (tk, tn), TN
        a_map = lambda i, j, k: (k, i)
        b_map = lambda i, j, k: (k + b_off[0] // tk, j + b_off[1] // tn)
    assert b_off[0] % b_blk[0] == 0 and b_off[1] % b_blk[1] == 0, (name, b_off, b_blk)
    has_res, has_norm, has_after = res is not None, norm_g is not None, after is not None
    assert not has_norm or tn == N

    def body(*refs):
        refs = list(refs)
        a_ref, b_ref = refs[0], refs[1]
        pos = 2
        r_ref = g_ref = n_ref = None
        if has_res:
            r_ref = refs[pos]
            pos += 1
        if has_norm:
            g_ref = refs[pos]
            pos += 1
        pos += has_after
        o_ref = refs[pos]
        pos += 1
        if has_norm:
            n_ref = refs[pos]
            pos += 1
        acc = refs[pos] if nk > 1 else None
        k = pl.program_id(2)
        part = _bdot(a_ref[...], b_ref[...], dims)

        def finish(total):
            if has_res:
                total = total + r_ref[...].astype(F32)
            o_ref[...] = total.astype(o_ref.dtype)
            if has_norm:
                r = lax.rsqrt(jnp.mean(total * total, axis=-1, keepdims=True) + EPS)
                n_ref[...] = (total * r * g_ref[...]).astype(n_ref.dtype)

        if nk == 1:
            finish(part)
        else:
            @pl.when(k == 0)
            def _():
                acc[...] = part

            @pl.when(k > 0)
            def _():
                acc[...] += part

            @pl.when(k == nk - 1)
            def _():
                finish(acc[...])

    in_specs = [pl.BlockSpec(a_blk, a_map), pl.BlockSpec(b_blk, b_map)]
    args = [a, b]
    if has_res:
        in_specs.append(pl.BlockSpec((tm, tn), lambda i, j, k: (i, j)))
        args.append(res)
    if has_norm:
        in_specs.append(pl.BlockSpec((1, N), lambda i, j, k: (0, 0)))
        args.append(norm_g)
    if has_after:
        in_specs.append(pl.BlockSpec(memory_space=pl.ANY))
        args.append(after)
    out_shape = jax.ShapeDtypeStruct((M, N), out_dtype)
    out_specs = pl.BlockSpec((tm, tn), lambda i, j, k: (i, j))
    if has_norm:
        out_shape = [out_shape, jax.ShapeDtypeStruct((M, N), BF16)]
        out_specs = [out_specs, pl.BlockSpec((tm, tn), lambda i, j, k: (i, j))]
    return pl.pallas_call(
        body, name=name, grid=(M // tm, N // tn, nk), in_specs=in_specs,
        out_specs=out_specs, out_shape=out_shape,
        scratch_shapes=[pltpu.VMEM((tm, tn), F32)] if nk > 1 else [],
        compiler_params=_cparams(("parallel", "parallel", "arbitrary")),
    )(*args)


def _rmsnorm_fwd(x, g, *, name, ts=512):
    S, D = x.shape
    ts = min(ts, S)

    def body(x_ref, g_ref, o_ref):
        xv = x_ref[...]
        r = lax.rsqrt(jnp.mean(xv * xv, axis=-1, keepdims=True) + EPS)
        o_ref[...] = (xv * r * g_ref[...]).astype(o_ref.dtype)

    return pl.pallas_call(
        body, name=name, grid=(S // ts,),
        in_specs=[pl.BlockSpec((ts, D), lambda i: (i, 0)), pl.BlockSpec((1, D), lambda i: (0, 0))],
        out_specs=pl.BlockSpec((ts, D), lambda i: (i, 0)),
        out_shape=jax.ShapeDtypeStruct((S, D), BF16),
        compiler_params=_cparams(("parallel",)),
    )(x, g)


def _rmsnorm_bwd(x, dxn, g, res, *, name, ts=512, emit_bf16=False):
    S, D = x.shape
    ts = min(ts, S)
    has_res = res is not None

    def body(*refs):
        refs = list(refs)
        dx16_ref = refs.pop() if emit_bf16 else None
        if has_res:
            x_ref, d_ref, g_ref, r_ref, dx_ref, dg_ref = refs
        else:
            x_ref, d_ref, g_ref, dx_ref, dg_ref = refs
        i = pl.program_id(0)
        xv = x_ref[...]
        dv = d_ref[...].astype(F32)
        r = lax.rsqrt(jnp.mean(xv * xv, axis=-1, keepdims=True) + EPS)
        gd = dv * g_ref[...]
        proj = jnp.mean(xv * gd, axis=-1, keepdims=True)
        dx = r * gd - xv * (r * r * r) * proj
        if has_res:
            dx = dx + r_ref[...]
        dx_ref[...] = dx
        if emit_bf16:
            dx16_ref[...] = dx.astype(BF16)
        part = jnp.sum(dv * xv * r, axis=0, keepdims=True)

        @pl.when(i == 0)
        def _():
            dg_ref[...] = part

        @pl.when(i > 0)
        def _():
            dg_ref[...] += part

    row = pl.BlockSpec((ts, D), lambda i: (i, 0))
    vec = pl.BlockSpec((1, D), lambda i: (0, 0))
    in_specs = [row, row, vec] + ([row] if has_res else [])
    args = [x, dxn, g] + ([res] if has_res else [])
    extra = emit_bf16 * [jax.ShapeDtypeStruct((S, D), BF16)]
    return pl.pallas_call(
        body, name=name, grid=(S // ts,), in_specs=in_specs, out_specs=[row, vec] + emit_bf16 * [row],
        out_shape=[jax.ShapeDtypeStruct((S, D), F32), jax.ShapeDtypeStruct((1, D), F32)] + extra,
        compiler_params=_cparams(("arbitrary",)),
    )(*args)


def _final_loss(x, target, g, *, name, ts=512):
    S, D = x.shape
    ts = min(ts, S)

    def body(x_ref, t_ref, g_ref, dx_ref, dg_ref, loss_ref, dx16_ref):
        i = pl.program_id(0)
        xv = x_ref[...]
        gv = g_ref[...]
        r = lax.rsqrt(jnp.mean(xv * xv, axis=-1, keepdims=True) + EPS)
        y = xv * r * gv
        err = y - t_ref[...]
        row_loss = jnp.mean(err * err, axis=-1, keepdims=True)
        lpart = 0.5 * jnp.sum(row_loss, axis=0, keepdims=True)
        dy = err * (1.0 / D)
        gd = dy * gv
        proj = jnp.mean(xv * gd, axis=-1, keepdims=True)
        dx = r * gd - xv * (r * r * r) * proj
        dx_ref[...] = dx
        dx16_ref[...] = dx.astype(BF16)
        part = jnp.sum(dy * xv * r, axis=0, keepdims=True)
        lfull = jnp.broadcast_to(lpart, loss_ref.shape)

        @pl.when(i == 0)
        def _():
            dg_ref[...] = part
            loss_ref[...] = lfull

        @pl.when(i > 0)
        def _():
            dg_ref[...] += part
            loss_ref[...] += lfull

    row = pl.BlockSpec((ts, D), lambda i: (i, 0))
    vec = pl.BlockSpec((1, D), lambda i: (0, 0))
    return pl.pallas_call(
        body, name=name, grid=(S // ts,), in_specs=[row, row, vec],
        out_specs=[row, vec, pl.BlockSpec((SUB, LANE), lambda i: (0, 0)), row],
        out_shape=[jax.ShapeDtypeStruct((S, D), F32), jax.ShapeDtypeStruct((1, D), F32),
                   jax.ShapeDtypeStruct((SUB, LANE), F32), jax.ShapeDtypeStruct((S, D), BF16)],
        compiler_params=_cparams(("arbitrary",)),
    )(x, target, g)


def _rope_table(pos_col, inv, *, name, ts=1024):
    S = pos_col.shape[0]
    ts = min(ts, S)
    half = inv.shape[1]

    def body(p_ref, inv_ref, c_ref, s_ref):
        ang = p_ref[...].astype(F32) * inv_ref[...]
        c_ref[...] = jnp.cos(ang)
        s_ref[...] = jnp.sin(ang)

    tab = pl.BlockSpec((ts, half), lambda i: (i, 0))
    return pl.pallas_call(
        body, name=name, grid=(S // ts,),
        in_specs=[pl.BlockSpec((ts, 1), lambda i: (i, 0)), pl.BlockSpec((1, half), lambda i: (0, 0))],
        out_specs=[tab, tab],
        out_shape=[jax.ShapeDtypeStruct((S, half), F32), jax.ShapeDtypeStruct((S, half), F32)],
        compiler_params=_cparams(("parallel",)),
    )(pos_col, inv)


def _ret_consts(C, log_g):
    ii = lax.broadcasted_iota(jnp.int32, (C, C), 0)
    jj = lax.broadcasted_iota(jnp.int32, (C, C), 1)
    diff = (ii - jj).astype(F32)
    intra = jnp.where(ii >= jj, jnp.exp(log_g * jnp.maximum(diff, 0.0)), 0.0)
    idx = lax.broadcasted_iota(jnp.int32, (C, 1), 0).astype(F32)
    qd = jnp.exp(log_g * (idx + 1.0))
    kd = jnp.exp(log_g * (C - 1.0 - idx))
    cd = math.exp(log_g * C)
    return intra, qd, kd, cd


def _rot(t, cs, sn):
    half = t.shape[-1] // 2
    t1, t2 = t[:, :half], t[:, half:]
    return jnp.concatenate([t1 * cs - t2 * sn, t1 * sn + t2 * cs], axis=-1)


def _unrot(d, cs, sn):
    half = d.shape[-1] // 2
    d1, d2 = d[:, :half], d[:, half:]
    return jnp.concatenate([d1 * cs + d2 * sn, d2 * cs - d1 * sn], axis=-1)


def _ret_fwd(h, cos, sin, ret_g, mix, *, name, ch=2):
    S = h.shape[0]
    R = ret_g.shape[1]
    H, C = RET_HEADS, RET_CHUNK
    Dh = R // H
    ts = ch * C
    assert S % ts == 0
    log_gs = [math.log(1.0 - 2.0 ** (-5.0 - hd)) for hd in range(H)]
    scale = Dh ** -0.5

    def body(x_ref, c_ref, s_ref, rg_ref, mix_in, ret_ref, st_ref, mix_ref, state):
        i = pl.program_id(0)

        @pl.when(i == 0)
        def _():
            state[...] = jnp.zeros_like(state)

        for c in range(ch):
            rows = pl.ds(c * C, C)
            cs, sn = c_ref[rows, :], s_ref[rows, :]
            for hd in range(H):
                intra, qd, kd, cd = _ret_consts(C, log_gs[hd])
                q = x_ref[rows, pl.ds(hd * Dh, Dh)]
                k = x_ref[rows, pl.ds(R + hd * Dh, Dh)]
                v = x_ref[rows, pl.ds(2 * R + hd * Dh, Dh)]
                g = x_ref[rows, pl.ds(3 * R + hd * Dh, Dh)]
                rq = _rot(q, cs, sn)
                rk = _rot(k, cs, sn) * scale
                st = state[hd]
                st_ref[c, hd] = st.astype(BF16)
                s_ = _bdot(rq, rk, NT) * intra
                ret = _bdot(s_, v, NN) + _bdot(rq * qd, st, NN)
                state[hd] = st * cd + _bdot(rk * kd, v, TN)
                ret_ref[rows, pl.ds(hd * Dh, Dh)] = ret
                rr = lax.rsqrt(jnp.mean(ret * ret, axis=-1, keepdims=True) + EPS)
                out = ret * rr * rg_ref[:, pl.ds(hd * Dh, Dh)] * (g * _sigmoid(g))
                mix_ref[rows, pl.ds(hd * Dh, Dh)] = out.astype(BF16)

    n_chunks = S // C
    return pl.pallas_call(
        body, name=name, grid=(S // ts,),
        in_specs=[pl.BlockSpec((ts, 4 * R), lambda i: (i, 0)),
                  pl.BlockSpec((ts, Dh // 2), lambda i: (i, 0)), pl.BlockSpec((ts, Dh // 2), lambda i: (i, 0)),
                  pl.BlockSpec((1, R), lambda i: (0, 0)), pl.BlockSpec(memory_space=pl.ANY)],
        out_specs=[pl.BlockSpec((ts, R), lambda i: (i, 0)),
                   pl.BlockSpec((ch, H, Dh, Dh), lambda i: (i, 0, 0, 0)),
                   pl.BlockSpec((ts, R), lambda i: (i, 0))],
        out_shape=[jax.ShapeDtypeStruct((S, R), F32), jax.ShapeDtypeStruct((n_chunks, H, Dh, Dh), BF16),
                   jax.ShapeDtypeStruct(mix.shape, mix.dtype)],
        scratch_shapes=[pltpu.VMEM((H, Dh, Dh), F32)],
        input_output_aliases={4: 2},
        compiler_params=_cparams(("arbitrary",)),
    )(h, cos, sin, ret_g, mix)


def _ret_bwd(h, cos, sin, ret_g, states, ret_raw, dmix, *, name, ch=2):
    S = h.shape[0]
    R = ret_g.shape[1]
    H, C = RET_HEADS, RET_CHUNK
    Dh = R // H
    ts = ch * C
    nb = S // ts
    log_gs = [math.log(1.0 - 2.0 ** (-5.0 - hd)) for hd in range(H)]
    scale = Dh ** -0.5

    def body(x_ref, c_ref, s_ref, rg_ref, st_ref, ret_ref, dm_ref, dh_ref, drg_ref, dstate):
        i = pl.program_id(0)

        @pl.when(i == 0)
        def _():
            dstate[...] = jnp.zeros_like(dstate)
            drg_ref[...] = jnp.zeros_like(drg_ref)

        for c in reversed(range(ch)):
            rows = pl.ds(c * C, C)
            cs, sn = c_ref[rows, :], s_ref[rows, :]
            for hd in range(H):
                intra, qd, kd, cd = _ret_consts(C, log_gs[hd])
                cols = pl.ds(hd * Dh, Dh)
                q = x_ref[rows, pl.ds(hd * Dh, Dh)]
                k = x_ref[rows, pl.ds(R + hd * Dh, Dh)]
                v = x_ref[rows, pl.ds(2 * R + hd * Dh, Dh)]
                g = x_ref[rows, pl.ds(3 * R + hd * Dh, Dh)]
                rq = _rot(q, cs, sn)
                rk = _rot(k, cs, sn) * scale
                ret = ret_ref[rows, cols]
                dm = dm_ref[rows, cols].astype(F32)
                rgv = rg_ref[:, cols]
                rr = lax.rsqrt(jnp.mean(ret * ret, axis=-1, keepdims=True) + EPS)
                retn = ret * rr
                sg = _sigmoid(g)
                silu = g * sg
                drg_ref[:, cols] += jnp.sum(dm * retn * silu, axis=0, keepdims=True)
                dg = dm * retn * rgv * (sg * (1.0 + g * (1.0 - sg)))
                dretn = dm * rgv * silu
                d_o = rr * dretn - ret * (rr * rr * rr) * jnp.mean(ret * dretn, axis=-1, keepdims=True)
                st = st_ref[c, hd]
                d_s = dstate[hd]
                a_ = _bdot(rq, rk, NT) * intra
                d_a = _bdot(d_o, v, NT) * intra
                d_qr = _bdot(d_a, rk, NN) + _bdot(d_o, st, NT) * qd
                d_kr = _bdot(d_a, rq, TN) + _bdot(v, d_s, NT) * kd
                d_v = _bdot(a_, d_o, TN) + _bdot(rk * kd, d_s, NN)
                dstate[hd] = d_s * cd + _bdot(rq * qd, d_o, TN)
                dh_ref[rows, pl.ds(hd * Dh, Dh)] = _unrot(d_qr, cs, sn).astype(BF16)
                dh_ref[rows, pl.ds(R + hd * Dh, Dh)] = (_unrot(d_kr, cs, sn) * scale).astype(BF16)
                dh_ref[rows, pl.ds(2 * R + hd * Dh, Dh)] = d_v.astype(BF16)
                dh_ref[rows, pl.ds(3 * R + hd * Dh, Dh)] = dg.astype(BF16)

    rb = lambda i: nb - 1 - i
    return pl.pallas_call(
        body, name=name, grid=(nb,),
        in_specs=[pl.BlockSpec((ts, 4 * R), lambda i: (rb(i), 0)),
                  pl.BlockSpec((ts, Dh // 2), lambda i: (rb(i), 0)), pl.BlockSpec((ts, Dh // 2), lambda i: (rb(i), 0)),
                  pl.BlockSpec((1, R), lambda i: (0, 0)),
                  pl.BlockSpec((ch, H, Dh, Dh), lambda i: (rb(i), 0, 0, 0)),
                  pl.BlockSpec((ts, R), lambda i: (rb(i), 0)),
                  pl.BlockSpec((ts, R), lambda i: (rb(i), 0))],
        out_specs=[pl.BlockSpec((ts, 4 * R), lambda i: (rb(i), 0)), pl.BlockSpec((1, R), lambda i: (0, 0))],
        out_shape=[jax.ShapeDtypeStruct((S, 6 * R), BF16), jax.ShapeDtypeStruct((1, R), F32)],
        scratch_shapes=[pltpu.VMEM((H, Dh, Dh), F32)],
        compiler_params=_cparams(("arbitrary",)),
    )(h, cos, sin, ret_g, states, ret_raw, dmix)


GELU_C = math.sqrt(2.0 / math.pi)
GELU_A = 0.044715


def _gelu_parts(y):
    t = jnp.tanh(GELU_C * (y + GELU_A * y * y * y))
    val = 0.5 * y * (1.0 + t)
    grad = 0.5 * (1.0 + t) + 0.5 * y * (1.0 - t * t) * GELU_C * (1.0 + 3.0 * GELU_A * y * y)
    return val, grad


def _neg_expm1(x):
    series = -x * (1.0 + x * (1.0 / 2.0) * (1.0 + x * (1.0 / 3.0) * (1.0 + x * (1.0 / 4.0) * (
        1.0 + x * (1.0 / 5.0) * (1.0 + x * (1.0 / 6.0) * (1.0 + x * (1.0 / 7.0)))))))
    return jnp.where(x > -0.35, series, 1.0 - jnp.exp(x))


def _log_sigmoid(x):
    return jnp.minimum(x, 0.0) - jnp.log1p(jnp.exp(-jnp.abs(x)))


def _lru_gates(uc, wa_ref, ba_ref, wx_ref, bx_ref):
    nbk = wa_ref.shape[0]
    bd = wa_ref.shape[1]
    rs, gs = [], []
    for n in range(nbk):
        ucn = uc[:, n * bd:(n + 1) * bd]
        rs.append(_sigmoid(_bdot(ucn, wa_ref[n], NN) + ba_ref[:, pl.ds(n * bd, bd)]))
        gs.append(_sigmoid(_bdot(ucn, wx_ref[n], NN) + bx_ref[:, pl.ds(n * bd, bd)]))
    return jnp.concatenate(rs, axis=-1), jnp.concatenate(gs, axis=-1)


def _lru_fwd(h, conv_w, conv_b, wa, ba, wx, bx, lam, *, name, ts=256):
    S = h.shape[0]
    W = lam.shape[1]
    K = conv_w.shape[0]
    ts = min(ts, S)

    def body(u_ref, y_ref, cw_ref, cb_ref, wa_ref, ba_ref, wx_ref, bx_ref, lam_ref, hl_ref, mix_ref, tail, hlast):
        i = pl.program_id(0)

        @pl.when(i == 0)
        def _():
            tail[...] = jnp.zeros_like(tail)
            hlast[...] = jnp.zeros_like(hlast)

        u = u_ref[...]
        tl = tail[...]
        uc = cb_ref[...] + cw_ref[K - 1:K, :] * u
        for k in range(K - 1):
            uc = uc + cw_ref[k:k + 1, :] * _shift_down(u, tl, K - 1 - k)
        tail[...] = u[ts - SUB:ts]
        r, ig = _lru_gates(uc, wa_ref, ba_ref, wx_ref, bx_ref)
        log_a = LRU_C * r * _log_sigmoid(lam_ref[...])
        a = jnp.exp(log_a)
        b = jnp.sqrt(_neg_expm1(2.0 * log_a)) * (ig * uc)
        in_tile = _row_iota((ts, W)) & (SUB - 1)
        d = 1
        while d < SUB:
            a_s = jnp.where(in_tile < d, 1.0, pltpu.roll(a, d, 0))
            b_s = jnp.where(in_tile < d, 0.0, pltpu.roll(b, d, 0))
            b = a * b_s + b
            a = a * a_s
            d *= 2
        before = hlast[SUB - 1:SUB, :]
        for k in range(ts // SUB):
            tile = slice(k * SUB, (k + 1) * SUB)
            h_tile = a[tile] * before + b[tile]
            hl_ref[tile, :] = h_tile
            before = h_tile[SUB - 1:SUB, :]
        hlast[...] = hl_ref[ts - SUB:ts, :]
        gy, _ = _gelu_parts(y_ref[...])
        mix_ref[...] = (hl_ref[...] * gy).astype(BF16)

    full = lambda shape: pl.BlockSpec(shape, lambda i: tuple(0 for _ in shape))
    return pl.pallas_call(
        body, name=name, grid=(S // ts,),
        in_specs=[pl.BlockSpec((ts, W), lambda i: (i, 4)), pl.BlockSpec((ts, W), lambda i: (i, 5)),
                  full(conv_w.shape), full(conv_b.shape), full(wa.shape), full(ba.shape), full(wx.shape),
                  full(bx.shape), full(lam.shape)],
        out_specs=[pl.BlockSpec((ts, W), lambda i: (i, 0)), pl.BlockSpec((ts, W), lambda i: (i, 1))],
        out_shape=[jax.ShapeDtypeStruct((S, W), F32), jax.ShapeDtypeStruct((S, 2 * W), BF16)],
        scratch_shapes=[pltpu.VMEM((SUB, W), F32), pltpu.VMEM((SUB, W), F32)],
        compiler_params=_cparams(("arbitrary",)),
    )(h, h, conv_w, conv_b, wa, ba, wx, bx, lam)


def _lru_bwd(h, hl, dmix, dh, conv_w, conv_b, wa, ba, wx, bx, lam, *, name, ts=256):
    S = h.shape[0]
    W = lam.shape[1]
    K = conv_w.shape[0]
    nbk, bd = wa.shape[0], wa.shape[1]
    ts = min(ts, S)
    nb = S // ts
    t8 = ts // SUB

    def body(u_ref, y_ref, uh_ref, hl_ref, hh_ref, dm_ref, cw_ref, cb_ref, wa_ref, ba_ref, wx_ref, bx_ref, lam_ref,
             dh_in, dh_ref, dcw_ref, dcb_ref, dwa_ref, dba_ref, dwx_ref, dbx_ref, dlam_ref, carry, head, lam_buf):
        i = pl.program_id(0)
        blk = nb - 1 - i

        @pl.when(i == 0)
        def _():
            carry[...] = jnp.zeros_like(carry)
            head[...] = jnp.zeros_like(head)
            for ref in (dcw_ref, dcb_ref, dwa_ref, dba_ref, dwx_ref, dbx_ref, dlam_ref):
                ref[...] = jnp.zeros_like(ref)

        inside = (blk > 0).astype(F32)
        u = u_ref[...]
        tl = uh_ref[...] * inside
        sh = [_shift_down(u, tl, K - 1 - k) for k in range(K)]
        uc = cb_ref[...]
        for k in range(K):
            uc = uc + cw_ref[k:k + 1, :] * sh[k]
        r, ig = _lru_gates(uc, wa_ref, ba_ref, wx_ref, bx_ref)
        lam_v = lam_ref[...]
        ls = _log_sigmoid(lam_v)
        log_a = LRU_C * r * ls
        a = jnp.exp(log_a)
        mult = jnp.sqrt(_neg_expm1(2.0 * log_a))
        hcur = hl_ref[...]
        hprev = _shift_down(hcur, hh_ref[...] * inside, 1)
        gy, dgy = _gelu_parts(y_ref[...])
        dm = dm_ref[...].astype(F32)
        d_y = dm * hcur * dgy
        rid = _row_iota((ts, W))
        bq = dm * gy + jnp.where(rid == ts - 1, carry[0:1, :], 0.0)
        aq = jnp.where(rid == ts - 1, 0.0, pltpu.roll(a, ts - 1, 0))
        in_tile = rid & (SUB - 1)
        d = 1
        while d < SUB:
            a_s = jnp.where(in_tile >= SUB - d, 1.0, pltpu.roll(aq, ts - d, 0))
            b_s = jnp.where(in_tile >= SUB - d, 0.0, pltpu.roll(bq, ts - d, 0))
            bq = bq + aq * b_s
            aq = aq * a_s
            d *= 2
        after_row = jnp.zeros((1, W), F32)
        for k in reversed(range(ts // SUB)):
            tile = slice(k * SUB, (k + 1) * SUB)
            lam_tile = aq[tile] * after_row + bq[tile]
            lam_buf[tile, :] = lam_tile
            after_row = lam_tile[0:1, :]
        lam_t = lam_buf[...]
        carry[...] = (a * lam_t)[0:SUB]
        d_a = lam_t * hprev
        d_mult = lam_t * (ig * uc)
        d_i = lam_t * mult * uc
        d_uc = lam_t * mult * ig
        d_log_a = d_a * a - d_mult * (a * a) / mult
        d_r = d_log_a * (LRU_C * ls)
        dlam_ref[...] += jnp.sum(d_log_a * (LRU_C * r), axis=0, keepdims=True) * _sigmoid(-lam_v)
        d_pr = d_r * r * (1.0 - r)
        d_pi = d_i * ig * (1.0 - ig)
        dba_ref[...] += jnp.sum(d_pr, axis=0, keepdims=True)
        dbx_ref[...] += jnp.sum(d_pi, axis=0, keepdims=True)
        extra = []
        for n in range(nbk):
            sl = slice(n * bd, (n + 1) * bd)
            ucn = uc[:, sl]
            dwa_ref[n] += _bdot(ucn, d_pr[:, sl], TN)
            dwx_ref[n] += _bdot(ucn, d_pi[:, sl], TN)
            extra.append(_bdot(d_pr[:, sl], wa_ref[n], NT) + _bdot(d_pi[:, sl], wx_ref[n], NT))
        d_uc = d_uc + jnp.concatenate(extra, axis=-1)
        dcb_ref[...] += jnp.sum(d_uc, axis=0, keepdims=True)
        rid8 = _row_iota((SUB, W))
        dcw = jnp.zeros((SUB, W), F32)
        for k in range(K):
            dcw = dcw + jnp.where(rid8 == k, jnp.sum(d_uc * sh[k], axis=0, keepdims=True), 0.0)
        dcw_ref[...] += dcw
        hd = head[...]
        d_u = cw_ref[K - 1:K, :] * d_uc
        for j in range(1, K):
            d_u = d_u + cw_ref[K - 1 - j:K - j, :] * _shift_up(d_uc, hd, j)
        head[...] = d_uc[0:SUB]
        dh_ref[:, 0:W] = d_u.astype(BF16)
        dh_ref[:, W:2 * W] = d_y.astype(BF16)

    rb = lambda i: nb - 1 - i
    prev8 = lambda i: jnp.maximum(rb(i) * t8 - 1, 0)
    full = lambda shape: pl.BlockSpec(shape, lambda i: tuple(0 for _ in shape))
    small = [jax.ShapeDtypeStruct((SUB, W), F32), jax.ShapeDtypeStruct((1, W), F32),
             jax.ShapeDtypeStruct(wa.shape, F32), jax.ShapeDtypeStruct((1, W), F32),
             jax.ShapeDtypeStruct(wx.shape, F32), jax.ShapeDtypeStruct((1, W), F32),
             jax.ShapeDtypeStruct((1, W), F32)]
    return pl.pallas_call(
        body, name=name, grid=(nb,),
        in_specs=[pl.BlockSpec((ts, W), lambda i: (rb(i), 4)), pl.BlockSpec((ts, W), lambda i: (rb(i), 5)),
                  pl.BlockSpec((SUB, W), lambda i: (prev8(i), 4)),
                  pl.BlockSpec((ts, W), lambda i: (rb(i), 0)), pl.BlockSpec((SUB, W), lambda i: (prev8(i), 0)),
                  pl.BlockSpec((ts, W), lambda i: (rb(i), 1)),
                  full(conv_w.shape), full(conv_b.shape), full(wa.shape), full(ba.shape), full(wx.shape),
                  full(bx.shape), full(lam.shape), pl.BlockSpec(memory_space=pl.ANY)],
        out_specs=[pl.BlockSpec((ts, 2 * W), lambda i: (rb(i), 2))] + [full(s.shape) for s in small],
        out_shape=[jax.ShapeDtypeStruct(dh.shape, dh.dtype)] + small,
        scratch_shapes=[pltpu.VMEM((SUB, W), F32), pltpu.VMEM((SUB, W), F32), pltpu.VMEM((ts, W), F32)],
        input_output_aliases={13: 0},
        compiler_params=_cparams(("arbitrary",)),
    )(h, h, h, hl, hl, dmix, conv_w, conv_b, wa, ba, wx, bx, lam, dh)


def _xattn_fwd(q, km, vm, *, name, ts=512):
    S, D = q.shape
    M = km.shape[0]
    H = XA_HEADS
    Dh = D // H
    ts = min(ts, S)
    scale = Dh ** -0.5

    def body(q_ref, k_ref, v_ref, o_ref):
        for hd in range(H):
            cols = pl.ds(hd * Dh, Dh)
            s = _bdot(q_ref[:, cols], k_ref[:, cols], NT) * scale
            s = s - jnp.max(s, axis=-1, keepdims=True)
            e = jnp.exp(s)
            p = e / jnp.sum(e, axis=-1, keepdims=True)
            o_ref[:, cols] = _bdot(p, v_ref[:, cols], NN).astype(o_ref.dtype)

    return pl.pallas_call(
        body, name=name, grid=(S // ts,),
        in_specs=[pl.BlockSpec((ts, D), lambda i: (i, 0)), pl.BlockSpec((M, D), lambda i: (0, 0)),
                  pl.BlockSpec((M, D), lambda i: (0, 0))],
        out_specs=pl.BlockSpec((ts, D), lambda i: (i, 0)),
        out_shape=jax.ShapeDtypeStruct((S, D), BF16),
        compiler_params=_cparams(("parallel",)),
    )(q, km, vm)


def _xattn_bwd(q, km, vm, d_o, *, name, ts=512):
    S, D = q.shape
    M = km.shape[0]
    H = XA_HEADS
    Dh = D // H
    ts = min(ts, S)
    scale = Dh ** -0.5

    def body(q_ref, k_ref, v_ref, do_ref, dq_ref, dk_ref, dv_ref):
        i = pl.program_id(0)

        @pl.when(i == 0)
        def _():
            dk_ref[...] = jnp.zeros_like(dk_ref)
            dv_ref[...] = jnp.zeros_like(dv_ref)

        for hd in range(H):
            cols = pl.ds(hd * Dh, Dh)
            qh, kh, vh, doh = q_ref[:, cols], k_ref[:, cols], v_ref[:, cols], do_ref[:, cols]
            s = _bdot(qh, kh, NT) * scale
            s = s - jnp.max(s, axis=-1, keepdims=True)
            e = jnp.exp(s)
            p = e / jnp.sum(e, axis=-1, keepdims=True)
            dp = _bdot(doh, vh, NT)
            ds = p * (dp - jnp.sum(dp * p, axis=-1, keepdims=True)) * scale
            dq_ref[:, cols] = _bdot(ds, kh, NN).astype(dq_ref.dtype)
            dk_ref[:, cols] += _bdot(ds, qh, TN)
            dv_ref[:, cols] += _bdot(p, doh, TN)

    row = pl.BlockSpec((ts, D), lambda i: (i, 0))
    mem = pl.BlockSpec((M, D), lambda i: (0, 0))
    return pl.pallas_call(
        body, name=name, grid=(S // ts,), in_specs=[row, mem, mem, row], out_specs=[row, mem, mem],
        out_shape=[jax.ShapeDtypeStruct((S, D), BF16), jax.ShapeDtypeStruct((M, D), F32),
                   jax.ShapeDtypeStruct((M, D), F32)],
        compiler_params=_cparams(("arbitrary",)),
    )(q, km, vm, d_o)


def _conv_rows(v, tail, cw_ref, cb_ref):
    K = cw_ref.shape[0]
    sh = [_shift_down(v, tail, K - 1 - k) for k in range(K)]
    out = cb_ref[...]
    for k in range(K):
        out = out + cw_ref[k:k + 1, :] * sh[k]
    return out, sh


FFN_SUB = 256


def _ffn_up_gate(xn, w_up, cw, cb, *, name, tm=1024, tn=512):
    S, D = xn.shape
    F2 = w_up.shape[1]
    F = F2 // 2
    tm, tn = min(tm, S), min(tn, F)
    sub = min(FFN_SUB, tm)
    nj = F // tn
    K = cw.shape[0]

    def body(x_ref, wa_ref, wb_ref, cwa_ref, cwb_ref, cba_ref, cbb_ref, act_ref, ha_ref, hb_ref, ac_ref, bc_ref, ta, tb):
        i = pl.program_id(1)

        @pl.when(i == 0)
        def _():
            ta[...] = jnp.zeros_like(ta)
            tb[...] = jnp.zeros_like(tb)

        tail_a, tail_b = ta[...], tb[...]
        for s in range(tm // sub):
            rows = pl.ds(s * sub, sub)
            xs = x_ref[rows, :]
            ha = _bdot(xs, wa_ref[...], NN)
            hb = _bdot(xs, wb_ref[...], NN)
            ac, _ = _conv_rows(ha, tail_a, cwa_ref, cba_ref)
            bc, _ = _conv_rows(hb, tail_b, cwb_ref, cbb_ref)
            tail_a, tail_b = ha[sub - SUB:sub], hb[sub - SUB:sub]
            ha_ref[rows, :] = ha
            hb_ref[rows, :] = hb
            ac_ref[rows, :] = ac
            bc_ref[rows, :] = bc
            act_ref[rows, :] = (ac * _sigmoid(ac) * bc).astype(act_ref.dtype)
        ta[...] = tail_a
        tb[...] = tail_b

    blk = pl.BlockSpec((tm, tn), lambda j, i: (i, j))
    return pl.pallas_call(
        body, name=name, grid=(nj, S // tm),
        in_specs=[pl.BlockSpec((tm, D), lambda j, i: (i, 0)),
                  pl.BlockSpec((D, tn), lambda j, i: (0, j)), pl.BlockSpec((D, tn), lambda j, i: (0, j + nj)),
                  pl.BlockSpec((K, tn), lambda j, i: (0, j)), pl.BlockSpec((K, tn), lambda j, i: (0, j + nj)),
                  pl.BlockSpec((1, tn), lambda j, i: (0, j)), pl.BlockSpec((1, tn), lambda j, i: (0, j + nj))],
        out_specs=[blk] * 5,
        out_shape=[jax.ShapeDtypeStruct((S, F), BF16)] + [jax.ShapeDtypeStruct((S, F), F32)] * 4,
        scratch_shapes=[pltpu.VMEM((SUB, tn), F32), pltpu.VMEM((SUB, tn), F32)],
        compiler_params=_cparams(("parallel", "arbitrary")),
    )(xn, w_up, w_up, cw, cw, cb, cb)


def _ffn_bwd(dx, w_down, hh_a, hh_b, c_a, c_b, act, xn, cw, *, name, tm=1024, tn=256):
    S, D = dx.shape
    F = hh_a.shape[1]
    tm, tn = min(tm, S), min(tn, F)
    sub = min(FFN_SUB, tm)
    nj = F // tn
    nb = S // tm
    K = cw.shape[0]

    def body(dx_ref, wd_ref, a_ref, b_ref, ac_ref, bc_ref, act_ref, xn_ref, cwa_ref, cwb_ref,
             da_ref, db_ref, ga_ref, gb_ref, dwd_ref, dwu_ref, ha, hb, acc_d, acc_a, acc_b):
        i = pl.program_id(1)

        @pl.when(i == 0)
        def _():
            for ref in (ha, hb, ga_ref, gb_ref, acc_d, acc_a, acc_b):
                ref[...] = jnp.zeros_like(ref)

        rid8 = _row_iota((SUB, tn))
        heads = [ha[...], hb[...]]
        gsums = [jnp.zeros((SUB, tn), F32), jnp.zeros((SUB, tn), F32)]
        for s in reversed(range(tm // sub)):
            rows = pl.ds(s * sub, sub)
            dv = _bdot(dx_ref[rows, :], wd_ref[...], NT)
            ac, bc = ac_ref[rows, :], bc_ref[rows, :]
            sg = _sigmoid(ac)
            d_bc = dv * ac * sg
            d_ac = dv * bc * sg * (1.0 + ac * (1.0 - sg))
            for which, (d_c, h_ref, cw_ref, o_ref) in enumerate(((d_ac, a_ref, cwa_ref, da_ref),
                                                                 (d_bc, b_ref, cwb_ref, db_ref))):
                ahead = [d_c] + [_shift_up(d_c, heads[which], j) for j in range(1, K)]
                heads[which] = d_c[0:SUB]
                d_in = cw_ref[K - 1:K, :] * d_c
                for j in range(1, K):
                    d_in = d_in + cw_ref[K - 1 - j:K - j, :] * ahead[j]
                o_ref[rows, :] = d_in.astype(o_ref.dtype)
                hv = h_ref[rows, :]
                gsum = gsums[which] + jnp.where(rid8 == K, jnp.sum(d_c, axis=0, keepdims=True), 0.0)
                for k in range(K):
                    gsum = gsum + jnp.where(rid8 == k, jnp.sum(ahead[K - 1 - k] * hv, axis=0, keepdims=True), 0.0)
                gsums[which] = gsum
        ha[...], hb[...] = heads
        ga_ref[...] += gsums[0]
        gb_ref[...] += gsums[1]
        acc_d[...] += _bdot(act_ref[...], dx_ref[...], TN)
        acc_a[...] += _bdot(xn_ref[...], da_ref[...], TN)
        acc_b[...] += _bdot(xn_ref[...], db_ref[...], TN)

        @pl.when(i == nb - 1)
        def _():
            dwd_ref[...] = acc_d[...].astype(dwd_ref.dtype)
            dwu_ref[0] = acc_a[...].astype(dwu_ref.dtype)
            dwu_ref[1] = acc_b[...].astype(dwu_ref.dtype)

    rb = lambda i: nb - 1 - i
    blk = pl.BlockSpec((tm, tn), lambda j, i: (rb(i), j))
    acc = pl.BlockSpec((SUB, tn), lambda j, i: (0, j))
    rows_d = pl.BlockSpec((tm, D), lambda j, i: (rb(i), 0))
    return pl.pallas_call(
        body, name=name, grid=(nj, nb),
        in_specs=[rows_d, pl.BlockSpec((tn, D), lambda j, i: (j, 0)), blk, blk, blk, blk, blk, rows_d,
                  pl.BlockSpec((K, tn), lambda j, i: (0, j)), pl.BlockSpec((K, tn), lambda j, i: (0, j + nj))],
        out_specs=[blk, blk, acc, acc, pl.BlockSpec((tn, D), lambda j, i: (j, 0)),
                   pl.BlockSpec((2, D, tn), lambda j, i: (0, 0, j))],
        out_shape=[jax.ShapeDtypeStruct((S, F), BF16), jax.ShapeDtypeStruct((S, F), BF16),
                   jax.ShapeDtypeStruct((SUB, F), F32), jax.ShapeDtypeStruct((SUB, F), F32),
                   jax.ShapeDtypeStruct((F, D), BF16), jax.ShapeDtypeStruct((2, D, F), BF16)],
        scratch_shapes=[pltpu.VMEM((SUB, tn), F32), pltpu.VMEM((SUB, tn), F32), pltpu.VMEM((tn, D), F32),
                        pltpu.VMEM((D, tn), F32), pltpu.VMEM((D, tn), F32)],
        compiler_params=_cparams(("parallel", "arbitrary")),
    )(dx, w_down, hh_a, hh_b, c_a, c_b, act, xn, cw, cw)


ADAM_BLOCK_ELEMS = 256 * 1024


def _adamw(w, m, v, parts, *, name):
    R, C = w.shape
    n = parts.shape[0]
    tr = R
    for cand in (1024, 512, 256, 128, 64, 32, 16):
        if R % cand == 0 and cand * C <= ADAM_BLOCK_ELEMS:
            tr = cand
            break
    c1 = 1.0 - ADAM_B1 ** ADAM_STEP
    c2 = 1.0 - ADAM_B2 ** ADAM_STEP

    def body(w_ref, m_ref, v_ref, p_ref, g_ref, d_ref, nm_ref, nv_ref):
        g = p_ref[0].astype(F32)
        for k in range(1, n):
            g = g + p_ref[k].astype(F32)
        m_new = ADAM_B1 * m_ref[...] + (1.0 - ADAM_B1) * g
        v_new = ADAM_B2 * v_ref[...] + (1.0 - ADAM_B2) * (g * g)
        m_hat = m_new / c1
        v_hat = v_new / c2
        g_ref[...] = g
        d_ref[...] = -ADAM_LR * (m_hat / (jnp.sqrt(v_hat) + ADAM_EPS) + ADAM_WD * w_ref[...])
        nm_ref[...] = m_new
        nv_ref[...] = v_new

    blk = pl.BlockSpec((tr, C), lambda i: (i, 0))
    sds = jax.ShapeDtypeStruct((R, C), F32)
    return pl.pallas_call(
        body, name=name, grid=(R // tr,),
        in_specs=[blk, blk, blk, pl.BlockSpec((n, tr, C), lambda i: (0, i, 0))],
        out_specs=[blk, blk, blk, blk], out_shape=[sds, sds, sds, sds],
        compiler_params=_cparams(("parallel",)),
    )(w, m, v, parts)


def _mesh_place():
    x, y, c = lax.axis_index("x"), lax.axis_index("y"), lax.axis_index("c")
    others = [(1 - x, y), (x, 1 - y), (1 - x, 1 - y)]
    return x, y, c, others


HBM_SPEC = pl.BlockSpec(memory_space=pltpu.HBM)
SEM_SPEC = pl.BlockSpec(memory_space=pltpu.SEMAPHORE)
ANY_SPEC = pl.BlockSpec(memory_space=pl.ANY)
EFFECT = pltpu.SideEffectType.DATAFLOW_SIDE_EFFECTING


def _in_hbm(a):
    return pltpu.with_memory_space_constraint(a, pltpu.HBM)


def _split_start(srcs, lands, copies, n_cp, *, name):
    n_s, n_l = len(srcs), len(lands)

    def body(*refs):
        src_refs, land_refs = refs[:n_s], refs[n_s:n_s + n_l]
        ssem, rsem = refs[n_s + n_l], refs[n_s + n_l + 1]
        token = refs[-1]
        for outgoing, _ in copies(src_refs, land_refs, ssem, rsem):
            outgoing.start()
        token[...] = jnp.zeros_like(token)

    outs = pl.pallas_call(
        body, name=name,
        out_shape=(pltpu.SemaphoreType.DMA((n_cp,)), pltpu.SemaphoreType.DMA((n_cp,)),
                   *[pltpu.HBM(a.shape, a.dtype) for a in srcs], *[pltpu.HBM(a.shape, a.dtype) for a in lands],
                   jax.ShapeDtypeStruct((SUB, LANE), F32)),
        in_specs=[HBM_SPEC] * (n_s + n_l),
        out_specs=(SEM_SPEC, SEM_SPEC, *[HBM_SPEC] * (n_s + n_l), pl.BlockSpec(memory_space=pltpu.VMEM)),
        input_output_aliases={i: 2 + i for i in range(n_s + n_l)},
        compiler_params=pltpu.CompilerParams(has_side_effects=EFFECT),
    )(*[_in_hbm(a) for a in srcs], *[_in_hbm(a) for a in lands])
    ssem, rsem = outs[0], outs[1]
    return ssem, rsem, list(outs[2:2 + n_s]), list(outs[2 + n_s:2 + n_s + n_l]), outs[-1]


def _split_wait(srcs, lands, ssem, rsem, after, copies, *, name):
    n_s, n_l = len(srcs), len(lands)

    def body(*refs):
        src_refs, land_refs = refs[:n_s], refs[n_s:n_s + n_l]
        s_ref, r_ref = refs[n_s + n_l], refs[n_s + n_l + 1]
        for outgoing, incoming in copies(src_refs, land_refs, s_ref, r_ref):
            outgoing.wait_send()
            incoming.wait_recv()

    outs = pl.pallas_call(
        body, name=name,
        out_shape=(*[pltpu.HBM(a.shape, a.dtype) for a in srcs], *[pltpu.HBM(a.shape, a.dtype) for a in lands]),
        in_specs=[HBM_SPEC] * (n_s + n_l) + [SEM_SPEC, SEM_SPEC, ANY_SPEC], out_specs=[HBM_SPEC] * (n_s + n_l),
        input_output_aliases={i: i for i in range(n_s + n_l)},
        compiler_params=pltpu.CompilerParams(has_side_effects=EFFECT),
    )(*srcs, *lands, ssem, rsem, after)
    return list(outs[:n_s]), list(outs[n_s:])


PLACE_BLOCK_ELEMS = 512 * 1024


def _place_rows(r, w):
    return _div_tile(r, max(16, PLACE_BLOCK_ELEMS // w), 16)


def _cast_place(shard, chip, axis, after, *, name):
    r, w = shard.shape
    tr = _place_rows(r, w)
    nb = r // tr
    full = (r * N_CHIP, w) if axis == 0 else (r, w * N_CHIP)
    has_after = after is not None

    def body(chip_ref, s_ref, *rest):
        rest[-1][...] = s_ref[...].astype(BF16)

    out_map = (lambda i, ch: (ch[0] * nb + i, 0)) if axis == 0 else (lambda i, ch: (i, ch[0]))
    grid_spec = pltpu.PrefetchScalarGridSpec(
        num_scalar_prefetch=1, grid=(nb,),
        in_specs=[pl.BlockSpec((tr, w), lambda i, ch: (i, 0))] + has_after * [ANY_SPEC],
        out_specs=pl.BlockSpec((tr, w), out_map))
    return pl.pallas_call(body, name=name, grid_spec=grid_spec, out_shape=jax.ShapeDtypeStruct(full, BF16),
                          compiler_params=_cparams(("parallel",)))(chip, shard, *(has_after * [after]))


def _grad_shard_shape(g, axis):
    if g.ndim == 3:
        return g.shape[1], 2 * g.shape[2] // N_CHIP
    return (g.shape[0] // N_CHIP, g.shape[1]) if axis == 0 else (g.shape[0], g.shape[1] // N_CHIP)


def _slot_place(g, ids, axis, *, name):
    r, w = _grad_shard_shape(g, axis)
    tr = _place_rows(r, w)
    nb = r // tr

    def body(ids_ref, g_ref, o_ref):
        o_ref[...] = g_ref[...]

    if g.ndim == 3:
        in_spec = pl.BlockSpec((None, tr, w), lambda i, ids_: (ids_[0] // 2, i, ids_[0] % 2))
    elif axis == 0:
        in_spec = pl.BlockSpec((tr, w), lambda i, ids_: (ids_[0] * nb + i, 0))
    else:
        in_spec = pl.BlockSpec((tr, w), lambda i, ids_: (i, ids_[0]))
    grid_spec = pltpu.PrefetchScalarGridSpec(
        num_scalar_prefetch=1, grid=(nb,), in_specs=[in_spec],
        out_specs=pl.BlockSpec((None, tr, w), lambda i, ids_: (ids_[1], i, 0)))
    return pl.pallas_call(body, name=name, grid_spec=grid_spec, out_shape=jax.ShapeDtypeStruct((N_DEV, r, w), g.dtype),
                          compiler_params=_cparams(("parallel",)))(ids, g)


class _WeightGather:
    def __init__(self, placed, shard_shapes, axes, splits, tag):
        self.placed, self.shard_shapes, self.axes, self.splits, self.tag = list(placed), shard_shapes, axes, splits, tag
        self.n = len(placed)

    def _region(self, land_refs, it, chip, half):
        r, w = self.shard_shapes[it]
        by_rows = self.axes[it] == 0
        if self.splits[it] and half is not None:
            rows = pl.ds(pl.multiple_of(half * (r // 2) + (chip * r if by_rows else 0), 16), r // 2)
        else:
            rows = pl.ds(chip * r if by_rows else 0, r)
        cols = pl.ds(0, w) if by_rows else pl.ds(pl.multiple_of(chip * w, LANE), w)
        return land_refs[it].at[rows, cols]

    def _ici(self, src_refs, land_refs, ssem, rsem):
        x, y, c, others = _mesh_place()
        pairs = []
        for it in range(self.n):
            for j, chip in enumerate(others):
                def mk(chip_from, it=it, j=j, chip=chip):
                    return pltpu.make_async_remote_copy(
                        src_ref=self._region(land_refs, it, 2 * x + y, c), dst_ref=self._region(land_refs, it, chip_from, c),
                        send_sem=ssem.at[3 * it + j], recv_sem=rsem.at[3 * it + j], device_id=(*chip, c),
                        device_id_type=MESH)
                pairs.append((mk(2 * x + y), mk(2 * chip[0] + chip[1])))
        return pairs

    def start(self):
        self.ssem, self.rsem, _, self.lands, token = _split_start(
            [], self.placed, self._ici, 3 * self.n, name="gather_start_" + self.tag)
        return token

    def _d2d(self, src_refs, land_refs, ssem, rsem):
        x, y, c, others = _mesh_place()
        pairs = []
        for it in range(self.n):
            if self.splits[it]:
                for chip in others:
                    def mk(half, it=it, chip=chip, k=len(pairs)):
                        reg = self._region(land_refs, it, 2 * chip[0] + chip[1], half)
                        return pltpu.make_async_remote_copy(src_ref=reg, dst_ref=reg, send_sem=ssem.at[k], recv_sem=rsem.at[k],
                                                            device_id=(x, y, 1 - c), device_id_type=MESH)
                    pairs.append((mk(c), mk(1 - c)))
        return pairs

    def forward(self, after):
        _, lands = _split_wait([], self.lands, self.ssem, self.rsem, after, self._ici,
                               name="gather_wait_" + self.tag)
        self.fsem, self.frsem, _, self.lands, token = _split_start(
            [], lands, self._d2d, 3 * sum(self.splits), name="gather_fwd_" + self.tag)
        return token

    def finish_forward(self, after):
        _, lands = _split_wait([], self.lands, self.fsem, self.frsem, after, self._d2d,
                               name="gather_fwd_wait_" + self.tag)
        return lands

    def finish(self, after):
        _, lands = _split_wait([], self.lands, self.ssem, self.rsem, after, self._ici,
                               name="gather_wait_" + self.tag)
        n = self.n
        n_fwd = 3 * sum(self.splits)
        if n_fwd == 0:
            return lands

        def body(*refs):
            out_refs = refs[n:2 * n]
            fsend, frecv = refs[2 * n:]
            x, y, c, others = _mesh_place()
            sibling = (x, y, 1 - c)

            def fwd(it, slot, chip, half):
                reg = self._region(out_refs, it, 2 * chip[0] + chip[1], half)
                return pltpu.make_async_remote_copy(src_ref=reg, dst_ref=reg, send_sem=fsend.at[slot],
                                                    recv_sem=frecv.at[slot], device_id=sibling, device_id_type=MESH)

            sends, recvs = [], []
            for it in range(n):
                if self.splits[it]:
                    for chip in others:
                        sends.append(fwd(it, len(sends), chip, c))
                        recvs.append(fwd(it, len(recvs), chip, 1 - c))
            for cp in sends:
                cp.start()
            for cp in recvs:
                cp.wait_recv()
            for cp in sends:
                cp.wait_send()

        fulls = pl.pallas_call(
            body, name="gather_d2d_" + self.tag, in_specs=[ANY_SPEC] * n, out_specs=[ANY_SPEC] * n,
            out_shape=[jax.ShapeDtypeStruct(a.shape, a.dtype) for a in lands],
            scratch_shapes=[pltpu.SemaphoreType.DMA((n_fwd,)), pltpu.SemaphoreType.DMA((n_fwd,))],
            input_output_aliases={i: i for i in range(n)},
        )(*lands)
        return list(fulls)


class _GradGather:
    def __init__(self, grads, axes, tag):
        self.grads, self.axes, self.tag = list(grads), axes, tag
        self.n = len(grads)
        self.shard_shapes = [_grad_shard_shape(g, ax) for g, ax in zip(grads, axes)]

    def _piece(self, src_refs, it, chip):
        r, w = self.shard_shapes[it]
        if self.grads[it].ndim == 3:
            return src_refs[it].at[chip // 2, :, pl.ds(pl.multiple_of((chip % 2) * w, LANE), w)]
        if self.axes[it] == 0:
            return src_refs[it].at[pl.ds(pl.multiple_of(chip * r, 16), r), :]
        return src_refs[it].at[:, pl.ds(pl.multiple_of(chip * w, LANE), w)]

    PER_ITEM = 4

    def _remote(self, src_refs, land_refs, ssem, rsem):
        x, y, c, others = _mesh_place()
        me = 4 * x + 2 * y + c
        pairs = []
        for it in range(self.n):
            def mk(k, piece_chip, slot, to, it=it):
                return pltpu.make_async_remote_copy(
                    src_ref=self._piece(src_refs, it, piece_chip), dst_ref=land_refs[it].at[slot],
                    send_sem=ssem.at[self.PER_ITEM * it + k], recv_sem=rsem.at[self.PER_ITEM * it + k], device_id=to,
                    device_id_type=MESH)
            for j, chip in enumerate(others):
                chip_id = 2 * chip[0] + chip[1]
                pairs.append((mk(j, chip_id, me, (*chip, c)), mk(j, chip_id, 2 * chip_id + c, (*chip, c))))
            sibling = (x, y, 1 - c)
            pairs.append((mk(3, 2 * x + y, me, sibling), mk(3, 2 * x + y, 4 * x + 2 * y + 1 - c, sibling)))
        return pairs

    def start(self):
        x, y, c = lax.axis_index("x"), lax.axis_index("y"), lax.axis_index("c")
        ids = jnp.stack([2 * x + y, 4 * x + 2 * y + c]).astype(jnp.int32)
        lands = [_slot_place(g, ids, ax, name="grads_own_%s%d" % (self.tag, it))
                 for it, (g, ax) in enumerate(zip(self.grads, self.axes))]
        self.ssem, self.rsem, self.srcs, self.lands, token = _split_start(
            self.grads, lands, self._remote, self.PER_ITEM * self.n, name="grads_start_" + self.tag)
        return token

    def _forward(self, src_refs, land_refs, ssem, rsem):
        x, y, c, others = _mesh_place()
        pairs = []
        for it in range(self.n):
            for j, ch in enumerate(others):
                def mk(slot, it=it, j=j):
                    return pltpu.make_async_remote_copy(
                        src_ref=land_refs[it].at[slot], dst_ref=land_refs[it].at[slot], send_sem=ssem.at[3 * it + j],
                        recv_sem=rsem.at[3 * it + j], device_id=(x, y, 1 - c), device_id_type=MESH)
                pairs.append((mk(4 * ch[0] + 2 * ch[1] + c), mk(4 * ch[0] + 2 * ch[1] + 1 - c)))
        return pairs

    def forward(self, after):
        _, lands = _split_wait(self.srcs, self.lands, self.ssem, self.rsem, after, self._remote,
                               name="grads_wait_" + self.tag)
        self.fsem, self.frsem, _, self.lands, token = _split_start(
            [], lands, self._forward, 3 * self.n, name="grads_fwd_" + self.tag)
        return token

    def finish(self, after):
        _, lands = _split_wait([], self.lands, self.fsem, self.frsem, after, self._forward,
                               name="grads_fwd_wait_" + self.tag)
        return lands


def _allreduce_small(vec, *, name):
    R, L = vec.shape

    def body(v_ref, o_ref, buf, send, recv, lsem):
        x, y, c, others = _mesh_place()
        me = 4 * x + 2 * y + c
        sibling = (x, y, 1 - c)

        def copy(k, slot, to, src=None):
            return pltpu.make_async_remote_copy(
                src_ref=buf.at[slot] if src is None else src, dst_ref=buf.at[slot], send_sem=send.at[k],
                recv_sem=recv.at[k], device_id=to, device_id_type=MESH)

        def slot_of(chip, core):
            return 4 * chip[0] + 2 * chip[1] + core

        mine = pltpu.make_async_copy(v_ref, buf.at[me], lsem)
        mine.start()
        first = [copy(0, me, sibling, src=v_ref)]
        first += [copy(1 + j, me, (*chip, c), src=v_ref) for j, chip in enumerate(others)]
        for cp in first:
            cp.start()
        passed = [copy(4 + j, slot_of(chip, c), sibling) for j, chip in enumerate(others)]
        for j, chip in enumerate(others):
            copy(1 + j, slot_of(chip, c), (*chip, c)).wait_recv()
            passed[j].start()
        copy(0, slot_of((x, y), 1 - c), sibling).wait_recv()
        for j, chip in enumerate(others):
            copy(4 + j, slot_of(chip, 1 - c), sibling).wait_recv()
        for cp in first + passed:
            cp.wait_send()
        mine.wait()
        total = buf[0]
        for k in range(1, N_DEV):
            total = total + buf[k]
        o_ref[...] = total

    return pl.pallas_call(
        body, name=name, in_specs=[pl.BlockSpec(memory_space=pltpu.VMEM)],
        out_specs=pl.BlockSpec(memory_space=pltpu.VMEM), out_shape=jax.ShapeDtypeStruct((R, L), F32),
        scratch_shapes=[pltpu.VMEM((N_DEV, R, L), F32), pltpu.SemaphoreType.DMA((7,)), pltpu.SemaphoreType.DMA((7,)),
                        pltpu.SemaphoreType.DMA],
        compiler_params=pltpu.CompilerParams(vmem_limit_bytes=VMEM_LIMIT),
    )(vec)


PACK_ALIGN = 1024


def _pack(arrs, row_multiple):
    flat = []
    for a in arrs:
        f = a.reshape(-1).astype(F32)
        flat.append(jnp.pad(f, (0, (-f.shape[0]) % PACK_ALIGN)))
    v = jnp.concatenate(flat)
    v = jnp.pad(v, (0, (-v.shape[0]) % (LANE * row_multiple)))
    return v.reshape(-1, LANE)


def _unpack(v, shapes):
    flat = v.reshape(-1)
    out, off = [], 0
    for s in shapes:
        size = math.prod(s)
        out.append(flat[off:off + size].reshape(s))
        off += size + (-size) % PACK_ALIGN
    return out


def _div_tile(dim, cap, mult=LANE):
    best = None
    for cand in range(mult, min(cap, dim) + 1, mult):
        if dim % cand == 0:
            best = cand
    return dim if best is None else best


WEIGHT_NAMES = ('norm1_g', 'w_in', 'ret_g', 'rg_conv_w', 'rg_conv_b', 'rg_wa', 'rg_ba', 'rg_wx', 'rg_bx', 'rg_lambda',
                'w_out', 'norm2_g', 'norm_mem_g', 'xa_wq', 'xa_wk', 'xa_wv', 'xa_wo', 'norm3_g', 'ffn_w_up',
                'ffn_conv_w', 'ffn_conv_b', 'ffn_w_down', 'final_g')
BIG_AXIS = {'w_in': 1, 'w_out': 0, 'xa_wq': 0, 'xa_wk': 0, 'xa_wv': 0, 'xa_wo': 0, 'ffn_w_up': 1, 'ffn_w_down': 0}
SMALL_SHARDED = ('rg_conv_w', 'ffn_conv_w')


def _step(x, mem, positions, loss_target, W, Mo, Vo):
    S, D = x.shape[1], x.shape[2]
    xs, mems, tgt = x[0], mem[0], loss_target[0]
    n_mem = mems.shape[0]
    pos_col = positions.reshape(S, 1)
    chip = 2 * lax.axis_index("x") + lax.axis_index("y")

    big = list(BIG_AXIS)
    shards = {n: W[n][0] for n in big}
    G = {}
    gather_groups = (('w_in', 'rg_conv_w'), ('w_out', 'xa_wq', 'xa_wk', 'xa_wv', 'xa_wo'),
                     ('ffn_w_up', 'ffn_conv_w'), ('ffn_w_down',))
    gathers, tok = [], None
    chip1 = jnp.reshape(chip, (1,)).astype(jnp.int32)
    for gi, names in enumerate(gather_groups):
        placed = []
        for n in names:
            if n in BIG_AXIS:
                placed.append(_cast_place(shards[n], chip1, BIG_AXIS[n], tok, name="place_" + n))
            else:
                s = W[n][0] if tok is None else W[n][0] + tok[0, 0]
                full = lax.empty((s.shape[0], s.shape[1] * N_CHIP), s.dtype)
                placed.append(lax.dynamic_update_slice(full, s, (0, chip * s.shape[1])))
        ag = _WeightGather(placed, [W[n][0].shape for n in names], [BIG_AXIS.get(n, 1) for n in names],
                           [n in BIG_AXIS for n in names], "g%d" % gi)
        tok = ag.start()
        gathers.append(ag)

    def finish_gather(gi, after):
        G.update(zip(gather_groups[gi], gathers[gi].finish(after)))

    def finish_forward(gi, after):
        G.update(zip(gather_groups[gi], gathers[gi].finish_forward(after)))

    R = W['ret_g'].shape[1]
    Wl = W['rg_lambda'].shape[1]
    IN = W['w_in'].shape[2] * N_CHIP
    F2 = W['ffn_w_up'].shape[2] * N_CHIP
    F = F2 // 2

    norm1_g, norm2_g, norm3_g = W['norm1_g'] + tok[0, 0], W['norm2_g'], W['norm3_g']
    norm_mem_g, final_g, ret_g = W['norm_mem_g'], W['final_g'].reshape(1, D), W['ret_g']
    rg_cb = W['rg_conv_b']
    wa, wx = W['rg_wa'][0], W['rg_wx'][0]
    ba, bx = W['rg_ba'].reshape(1, Wl), W['rg_bx'].reshape(1, Wl)
    lam = W['rg_lambda']
    ffn_cb = W['ffn_conv_b']

    def fwd_mm(a, wname, N, K, **kw):
        return _mm(a, G[wname], mode="nn", M=a.shape[0], N=N, K=K, tm=_div_tile(a.shape[0], 1024),
                   tn=1024 if K <= 3072 else 512, tk=K, **kw)

    def fwd_mm_norm(a, wname, res, g, name):
        return _mm(a, G[wname], mode="nn", M=a.shape[0], N=D, K=a.shape[1], tm=512, tn=D, tk=_div_tile(a.shape[1], 2048),
                   out_dtype=F32, res=res, norm_g=g, name=name)

    def bwd_x_mm(d, wname, N, K, **kw):
        return _mm(d, G[wname], mode="nt", M=d.shape[0], N=N, K=K, tm=_div_tile(d.shape[0], 1024),
                   tn=_div_tile(N, 1024 if K <= 3072 else 512, 256), tk=K, **kw)

    def bwd_w_mm(a, d, M, N, **kw):
        Ks = a.shape[0]
        return _mm(a, d, mode="tn", M=M, N=N, K=Ks, out_dtype=BF16, tm=_div_tile(M, 1024, 256),
                   tn=_div_tile(N, 1024, 256), tk=_div_tile(Ks, 4096 if d.dtype == BF16 else 1024), **kw)

    xn1 = _rmsnorm_fwd(xs, norm1_g, name="norm1_fwd")
    half = (R // RET_HEADS) // 2
    inv = (ROPE_BASE ** (-jnp.arange(half, dtype=F32) / half)).reshape(1, half)
    cos, sin = _rope_table(pos_col, inv + tok[0, 0], name="rope_table")
    finish_gather(0, cos)
    rg_cw = G['rg_conv_w']
    h = fwd_mm(xn1, 'w_in', IN, D, out_dtype=F32, name="mm_in")
    hl, mix = _lru_fwd(h, rg_cw, rg_cb, wa, ba, wx, bx, lam, name="lru_fwd")
    t1 = gathers[1].forward(hl)
    ret_raw, states, mix = _ret_fwd(h, cos, sin, ret_g + t1[0, 0], mix, name="ret_fwd")
    finish_forward(1, mix)
    x1, xn2 = fwd_mm_norm(mix, 'w_out', xs, norm2_g, "mm_out")
    memn = _rmsnorm_fwd(mems, norm_mem_g, name="norm_mem_fwd")
    km = fwd_mm(memn, 'xa_wk', D, D, out_dtype=BF16, name="mm_k")
    vm = fwd_mm(memn, 'xa_wv', D, D, out_dtype=BF16, name="mm_v")
    t2 = gathers[2].forward(x1)
    q = fwd_mm(xn2, 'xa_wq', D, D, out_dtype=BF16, after=t2, name="mm_q")
    o = _xattn_fwd(q, km, vm, name="xattn_fwd")
    x2, xn3 = fwd_mm_norm(o, 'xa_wo', x1, norm3_g, "mm_o")
    finish_forward(2, xn3)
    t3 = gathers[3].forward(xn3)
    ffn_cw = G['ffn_conv_w']
    act, hh_a, hh_b, hc_a, hc_b = _ffn_up_gate(xn3, G['ffn_w_up'], ffn_cw, ffn_cb + t3[0, 0], name="ffn_up_gate")
    finish_forward(3, act)
    x3 = fwd_mm(act, 'ffn_w_down', D, F, out_dtype=F32, res=x2, name="mm_down")
    dx3, d_final, loss8, dx3h = _final_loss(x3, tgt, final_g, name="final_loss")

    gw = {}
    grad_groups = []

    def start_grads(names, tag):
        gg = _GradGather([gw[n] for n in names], [BIG_AXIS[n] for n in names], tag)
        grad_groups.append((names, gg))
        return gg.start()

    dhh_a, dhh_b, gcw_a, gcw_b, gw['ffn_w_down'], gw['ffn_w_up'] = _ffn_bwd(
        dx3h, G['ffn_w_down'], hh_a, hh_b, hc_a, hc_b, act, xn3, ffn_cw, name="ffn_bwd")
    tok_a = start_grads(('ffn_w_down', 'ffn_w_up'), "a")
    dxn3 = bwd_x_mm(dhh_a, 'ffn_w_up', D, F, out_dtype=F32, after=tok_a, name="mm_dxn3_a")
    dxn3 = bwd_x_mm(dhh_b, 'ffn_w_up', D, F, out_dtype=BF16, b_off=(0, F), res=dxn3, name="mm_dxn3_b")
    dx2, d_norm3, dx2h = _rmsnorm_bwd(x2, dxn3, norm3_g, dx3, name="norm3_bwd", emit_bf16=True)
    Kc = ffn_cw.shape[0]
    d_ffn_cw = jnp.concatenate([gcw_a[:Kc], gcw_b[:Kc]], axis=1)
    d_ffn_cb = jnp.concatenate([gcw_a[Kc:Kc + 1], gcw_b[Kc:Kc + 1]], axis=1)

    d_o = bwd_x_mm(dx2h, 'xa_wo', D, D, out_dtype=BF16, name="mm_do")
    gw['xa_wo'] = bwd_w_mm(o, dx2h, D, D, name="mm_dw_o")
    dq, dk, dv = _xattn_bwd(q, km, vm, d_o, name="xattn_bwd")
    gw['xa_wq'] = bwd_w_mm(xn2, dq, D, D, name="mm_dw_q")
    dxn2 = bwd_x_mm(dq, 'xa_wq', D, D, out_dtype=BF16, name="mm_dxn2")
    gw['xa_wk'] = bwd_w_mm(memn, dk, D, D, name="mm_dw_k")
    gw['xa_wv'] = bwd_w_mm(memn, dv, D, D, name="mm_dw_v")
    dmemn = bwd_x_mm(dk, 'xa_wk', D, D, out_dtype=F32, name="mm_dmem_k")
    dmemn = bwd_x_mm(dv, 'xa_wv', D, D, out_dtype=F32, res=dmemn, name="mm_dmem_v")
    _, d_norm_mem = _rmsnorm_bwd(mems, dmemn, norm_mem_g, None, name="norm_mem_bwd")
    dx1, d_norm2, dx1h = _rmsnorm_bwd(x1, dxn2, norm2_g, dx2, name="norm2_bwd", emit_bf16=True)

    gw['w_out'] = bwd_w_mm(mix, dx1h, D, D, name="mm_dw_out")
    tok_b = start_grads(('xa_wo', 'xa_wq', 'xa_wk', 'xa_wv', 'w_out'), "b")
    dmix = bwd_x_mm(dx1h, 'w_out', D, D, out_dtype=BF16, after=tok_b, name="mm_dmix")
    dh, d_ret_g = _ret_bwd(h, cos, sin, ret_g, states, ret_raw, dmix, name="ret_bwd")
    dh, d_rcw, d_rcb, d_wa, d_ba, d_wx, d_bx, d_lam = _lru_bwd(
        h, hl, dmix, dh, rg_cw, rg_cb, wa, ba, wx, bx, lam, name="lru_bwd")
    gw['w_in'] = bwd_w_mm(xn1, dh, D, IN, name="mm_dw_in")
    tok_c = start_grads(('w_in',), "c")
    dxn1 = bwd_x_mm(dh, 'w_in', D, IN, out_dtype=BF16, after=tok_c, name="mm_dxn1")
    grad_x, d_norm1 = _rmsnorm_bwd(xs, dxn1, norm1_g, dx1, name="norm1_bwd")

    small_parts = {
        'norm1_g': d_norm1, 'ret_g': d_ret_g, 'rg_conv_w': d_rcw[:rg_cw.shape[0]], 'rg_conv_b': d_rcb,
        'rg_wa': d_wa, 'rg_ba': d_ba, 'rg_wx': d_wx, 'rg_bx': d_bx, 'rg_lambda': d_lam, 'norm2_g': d_norm2,
        'norm_mem_g': d_norm_mem, 'norm3_g': d_norm3, 'ffn_conv_w': d_ffn_cw, 'ffn_conv_b': d_ffn_cb,
        'final_g': d_final}
    small = [n for n in WEIGHT_NAMES if n not in BIG_AXIS]
    red_shapes = [(1,)] + [tuple(small_parts[n].shape) for n in small]
    fwd_tok = sum(gg.forward(d_norm1)[0:1, 0:1] for _, gg in grad_groups)
    reduced = _allreduce_small(_pack([loss8[0:1, 0:1] + fwd_tok] + [small_parts[n] for n in small], SUB),
                               name="allreduce_small")
    red = _unpack(reduced, red_shapes)
    loss = red[0][0]
    g_small = dict(zip(small, red[1:]))
    for n in SMALL_SHARDED:
        w_local = W[n].shape[-1]
        g_small[n] = lax.dynamic_slice_in_dim(g_small[n], chip * w_local, w_local, axis=1)

    out_g, out_d, out_m, out_v = {}, {}, {}, {}
    rows = 512
    pk = lambda d: _pack([d[n] for n in small], rows)
    g_pack = _pack([g_small[n] for n in small], rows)
    res_small = _adamw(pk(W), pk(Mo), pk(Vo), g_pack[None], name="adamw_small")
    shapes_small = [tuple(W[n].shape) for n in small]
    for dst, packed in zip((out_g, out_d, out_m, out_v), res_small):
        for n, val in zip(small, _unpack(packed, shapes_small)):
            dst[n] = val
    last = res_small[0]
    for names, gg in grad_groups:
        for n, land in zip(names, gg.finish(last)):
            g, d, m_new, v_new = _adamw(shards[n], Mo[n][0], Vo[n][0], land, name="adamw_" + n)
            out_g[n], out_d[n], out_m[n], out_v[n] = (t.reshape(W[n].shape) for t in (g, d, m_new, v_new))
            last = g
    return (loss, grad_x[None], *[out_g[n] for n in WEIGHT_NAMES], *[out_d[n] for n in WEIGHT_NAMES],
            *[out_m[n] for n in WEIGHT_NAMES], *[out_v[n] for n in WEIGHT_NAMES])


def kernel(x, mem, positions, norm1_g, w_in, ret_g, rg_conv_w, rg_conv_b, rg_wa, rg_ba, rg_wx, rg_bx, rg_lambda, w_out, norm2_g, norm_mem_g, xa_wq, xa_wk, xa_wv, xa_wo, norm3_g, ffn_w_up, ffn_conv_w, ffn_conv_b, ffn_w_down, final_g, loss_target, m_norm1_g, m_w_in, m_ret_g, m_rg_conv_w, m_rg_conv_b, m_rg_wa, m_rg_ba, m_rg_wx, m_rg_bx, m_rg_lambda, m_w_out, m_norm2_g, m_norm_mem_g, m_xa_wq, m_xa_wk, m_xa_wv, m_xa_wo, m_norm3_g, m_ffn_w_up, m_ffn_conv_w, m_ffn_conv_b, m_ffn_w_down, m_final_g, v_norm1_g, v_w_in, v_ret_g, v_rg_conv_w, v_rg_conv_b, v_rg_wa, v_rg_ba, v_rg_wx, v_rg_bx, v_rg_lambda, v_w_out, v_norm2_g, v_norm_mem_g, v_xa_wq, v_xa_wk, v_xa_wv, v_xa_wo, v_norm3_g, v_ffn_w_up, v_ffn_conv_w, v_ffn_conv_b, v_ffn_w_down, v_final_g):
    W = dict(zip(WEIGHT_NAMES, (norm1_g, w_in, ret_g, rg_conv_w, rg_conv_b, rg_wa, rg_ba, rg_wx, rg_bx, rg_lambda, w_out,
                                norm2_g, norm_mem_g, xa_wq, xa_wk, xa_wv, xa_wo, norm3_g, ffn_w_up, ffn_conv_w,
                                ffn_conv_b, ffn_w_down, final_g)))
    Mo = dict(zip(WEIGHT_NAMES, (m_norm1_g, m_w_in, m_ret_g, m_rg_conv_w, m_rg_conv_b, m_rg_wa, m_rg_ba, m_rg_wx, m_rg_bx,
                                 m_rg_lambda, m_w_out, m_norm2_g, m_norm_mem_g, m_xa_wq, m_xa_wk, m_xa_wv, m_xa_wo,
                                 m_norm3_g, m_ffn_w_up, m_ffn_conv_w, m_ffn_conv_b, m_ffn_w_down, m_final_g)))
    Vo = dict(zip(WEIGHT_NAMES, (v_norm1_g, v_w_in, v_ret_g, v_rg_conv_w, v_rg_conv_b, v_rg_wa, v_rg_ba, v_rg_wx, v_rg_bx,
                                 v_rg_lambda, v_w_out, v_norm2_g, v_norm_mem_g, v_xa_wq, v_xa_wk, v_xa_wv, v_xa_wo,
                                 v_norm3_g, v_ffn_w_up, v_ffn_conv_w, v_ffn_conv_b, v_ffn_w_down, v_final_g)))
    return _step(x, mem, positions, loss_target, W, Mo, Vo)
```

```python
import math

import jax
import jax.numpy as jnp
from jax import lax
from jax.experimental import pallas as pl
from jax.experimental.pallas import tpu as pltpu

F32 = jnp.float32
BF16 = jnp.bfloat16

EPS = 1e-6
RET_HEADS = 4
RET_CHUNK = 128
ROPE_BASE = 10000.0
LRU_BLOCKS = 8
LRU_C = 8.0
XA_HEADS = 4

ADAM_LR = 0.001
ADAM_B1 = 0.9
ADAM_B2 = 0.999
ADAM_EPS = 1e-08
ADAM_WD = 0.01
ADAM_STEP = 10

N_DEV = 8
N_CHIP = 4
MESH = pl.DeviceIdType.MESH
SUB = 8
LANE = 128
VMEM_LIMIT = 56 * 1024 * 1024

NN = ((1,), (0,))
NT = ((1,), (1,))
TN = ((0,), (0,))


def _cparams(sem):
    return pltpu.CompilerParams(dimension_semantics=sem, vmem_limit_bytes=VMEM_LIMIT)


def _sigmoid(v):
    return 1.0 / (1.0 + jnp.exp(-v))


def _bdot(a, b, dims):
    return lax.dot_general(a.astype(BF16), b.astype(BF16), (dims, ((), ())), preferred_element_type=F32)


def _row_iota(shape):
    return lax.broadcasted_iota(jnp.int32, shape, 0)


def _shift_down(v, tail, k):
    if k == 0:
        return v
    r = pltpu.roll(v, k, 0)
    rt = pltpu.roll(tail, k, 0)
    first = jnp.where(_row_iota(rt.shape) < k, rt, r[0:SUB])
    return jnp.concatenate([first, r[SUB:]], axis=0)


def _shift_up(v, head, k):
    if k == 0:
        return v
    n = v.shape[0]
    r = pltpu.roll(v, n - k, 0)
    rh = pltpu.roll(head, SUB - k, 0)
    last = jnp.where(_row_iota(rh.shape) >= SUB - k, rh, r[n - SUB:n])
    return jnp.concatenate([r[:n - SUB], last], axis=0)


def _mm(a, b, *, mode, M, N, K, out_dtype, name, tm=512, tn=512, tk=512, b_off=(0, 0), res=None, norm_g=None,
        after=None):
    tm, tn, tk = min(tm, M), min(tn, N), min(tk, K)
    assert M % tm == 0 and N % tn == 0 and K % tk == 0, (name, M, N, K, tm, tn, tk)
    nk = K // tk
    if mode == "nn":
        a_blk, b_blk, dims = (tm, tk), (tk, tn), NN
        a_map = lambda i, j, k: (i, k)
        b_map = lambda i, j, k: (k + b_off[0] // tk, j + b_off[1] // tn)
    elif mode == "nt":
        a_blk, b_blk, dims = (tm, tk), (tn, tk), NT
        a_map = lambda i, j, k: (i, k)
        b_map = lambda i, j, k: (j + b_off[0] // tn, k + b_off[1] // tk)
    else:
        a_blk, b_blk, dims = (tk, tm), (tk, tn), TN
        a_map = lambda i, j, k: (k, i)
        b_map = lambda i, j, k: (k + b_off[0] // tk, j + b_off[1] // tn)
    assert b_off[0] % b_blk[0] == 0 and b_off[1] % b_blk[1] == 0, (name, b_off, b_blk)
    has_res, has_norm, has_after = res is not None, norm_g is not None, after is not None
    assert not has_norm or tn == N

    def body(*refs):
        refs = list(refs)
        a_ref, b_ref = refs[0], refs[1]
        pos = 2
        r_ref = g_ref = n_ref = None
        if has_res:
            r_ref = refs[pos]
            pos += 1
        if has_norm:
            g_ref = refs[pos]
            pos += 1
        pos += has_after
        o_ref = refs[pos]
        pos += 1
        if has_norm:
            n_ref = refs[pos]
            pos += 1
        acc = refs[pos] if nk > 1 else None
        k = pl.program_id(2)
        part = _bdot(a_ref[...], b_ref[...], dims)

        def finish(total):
            if has_res:
                total = total + r_ref[...].astype(F32)
            o_ref[...] = total.astype(o_ref.dtype)
            if has_norm:
                r = lax.rsqrt(jnp.mean(total * total, axis=-1, keepdims=True) + EPS)
                n_ref[...] = (total * r * g_ref[...]).astype(n_ref.dtype)

        if nk == 1:
            finish(part)
        else:
            @pl.when(k == 0)
            def _():
                acc[...] = part

            @pl.when(k > 0)
            def _():
                acc[...] += part

            @pl.when(k == nk - 1)
            def _():
                finish(acc[...])

    in_specs = [pl.BlockSpec(a_blk, a_map), pl.BlockSpec(b_blk, b_map)]
    args = [a, b]
    if has_res:
        in_specs.append(pl.BlockSpec((tm, tn), lambda i, j, k: (i, j)))
        args.append(res)
    if has_norm:
        in_specs.append(pl.BlockSpec((1, N), lambda i, j, k: (0, 0)))
        args.append(norm_g)
    if has_after:
        in_specs.append(pl.BlockSpec(memory_space=pl.ANY))
        args.append(after)
    out_shape = jax.ShapeDtypeStruct((M, N), out_dtype)
    out_specs = pl.BlockSpec((tm, tn), lambda i, j, k: (i, j))
    if has_norm:
        out_shape = [out_shape, jax.ShapeDtypeStruct((M, N), BF16)]
        out_specs = [out_specs, pl.BlockSpec((tm, tn), lambda i, j, k: (i, j))]
    return pl.pallas_call(
        body, name=name, grid=(M // tm, N // tn, nk), in_specs=in_specs,
        out_specs=out_specs, out_shape=out_shape,
        scratch_shapes=[pltpu.VMEM((tm, tn), F32)] if nk > 1 else [],
        compiler_params=_cparams(("parallel", "parallel", "arbitrary")),
    )(*args)


def _rmsnorm_fwd(x, g, *, name, ts=512):
    S, D = x.shape
    ts = min(ts, S)

    def body(x_ref, g_ref, o_ref):
        xv = x_ref[...]
        r = lax.rsqrt(jnp.mean(xv * xv, axis=-1, keepdims=True) + EPS)
        o_ref[...] = (xv * r * g_ref[...]).astype(o_ref.dtype)

    return pl.pallas_call(
        body, name=name, grid=(S // ts,),
        in_specs=[pl.BlockSpec((ts, D), lambda i: (i, 0)), pl.BlockSpec((1, D), lambda i: (0, 0))],
        out_specs=pl.BlockSpec((ts, D), lambda i: (i, 0)),
        out_shape=jax.ShapeDtypeStruct((S, D), BF16),
        compiler_params=_cparams(("parallel",)),
    )(x, g)


def _rmsnorm_bwd(x, dxn, g, res, *, name, ts=512, emit_bf16=False):
    S, D = x.shape
    ts = min(ts, S)
    has_res = res is not None

    def body(*refs):
        refs = list(refs)
        dx16_ref = refs.pop() if emit_bf16 else None
        if has_res:
            x_ref, d_ref, g_ref, r_ref, dx_ref, dg_ref = refs
        else:
            x_ref, d_ref, g_ref, dx_ref, dg_ref = refs
        i = pl.program_id(0)
        xv = x_ref[...]
        dv = d_ref[...].astype(F32)
        r = lax.rsqrt(jnp.mean(xv * xv, axis=-1, keepdims=True) + EPS)
        gd = dv * g_ref[...]
        proj = jnp.mean(xv * gd, axis=-1, keepdims=True)
        dx = r * gd - xv * (r * r * r) * proj
        if has_res:
            dx = dx + r_ref[...]
        dx_ref[...] = dx
        if emit_bf16:
            dx16_ref[...] = dx.astype(BF16)
        part = jnp.sum(dv * xv * r, axis=0, keepdims=True)

        @pl.when(i == 0)
        def _():
            dg_ref[...] = part

        @pl.when(i > 0)
        def _():
            dg_ref[...] += part

    row = pl.BlockSpec((ts, D), lambda i: (i, 0))
    vec = pl.BlockSpec((1, D), lambda i: (0, 0))
    in_specs = [row, row, vec] + ([row] if has_res else [])
    args = [x, dxn, g] + ([res] if has_res else [])
    extra = emit_bf16 * [jax.ShapeDtypeStruct((S, D), BF16)]
    return pl.pallas_call(
        body, name=name, grid=(S // ts,), in_specs=in_specs, out_specs=[row, vec] + emit_bf16 * [row],
        out_shape=[jax.ShapeDtypeStruct((S, D), F32), jax.ShapeDtypeStruct((1, D), F32)] + extra,
        compiler_params=_cparams(("arbitrary",)),
    )(*args)


def _final_loss(x, target, g, *, name, ts=512):
    S, D = x.shape
    ts = min(ts, S)

    def body(x_ref, t_ref, g_ref, dx_ref, dg_ref, loss_ref, dx16_ref):
        i = pl.program_id(0)
        xv = x_ref[...]
        gv = g_ref[...]
        r = lax.rsqrt(jnp.mean(xv * xv, axis=-1, keepdims=True) + EPS)
        y = xv * r * gv
        err = y - t_ref[...]
        row_loss = jnp.mean(err * err, axis=-1, keepdims=True)
        lpart = 0.5 * jnp.sum(row_loss, axis=0, keepdims=True)
        dy = err * (1.0 / D)
        gd = dy * gv
        proj = jnp.mean(xv * gd, axis=-1, keepdims=True)
        dx = r * gd - xv * (r * r * r) * proj
        dx_ref[...] = dx
        dx16_ref[...] = dx.astype(BF16)
        part = jnp.sum(dy * xv * r, axis=0, keepdims=True)
        lfull = jnp.broadcast_to(lpart, loss_ref.shape)

        @pl.when(i == 0)
        def _():
            dg_ref[...] = part
            loss_ref[...] = lfull

        @pl.when(i > 0)
        def _():
            dg_ref[...] += part
            loss_ref[...] += lfull

    row = pl.BlockSpec((ts, D), lambda i: (i, 0))
    vec = pl.BlockSpec((1, D), lambda i: (0, 0))
    return pl.pallas_call(
        body, name=name, grid=(S // ts,), in_specs=[row, row, vec],
        out_specs=[row, vec, pl.BlockSpec((SUB, LANE), lambda i: (0, 0)), row],
        out_shape=[jax.ShapeDtypeStruct((S, D), F32), jax.ShapeDtypeStruct((1, D), F32),
                   jax.ShapeDtypeStruct((SUB, LANE), F32), jax.ShapeDtypeStruct((S, D), BF16)],
        compiler_params=_cparams(("arbitrary",)),
    )(x, target, g)


def _rope_table(pos_col, inv, *, name, ts=1024):
    S = pos_col.shape[0]
    ts = min(ts, S)
    half = inv.shape[1]

    def body(p_ref, inv_ref, c_ref, s_ref):
        ang = p_ref[...].astype(F32) * inv_ref[...]
        c_ref[...] = jnp.cos(ang)
        s_ref[...] = jnp.sin(ang)

    tab = pl.BlockSpec((ts, half), lambda i: (i, 0))
    return pl.pallas_call(
        body, name=name, grid=(S // ts,),
        in_specs=[pl.BlockSpec((ts, 1), lambda i: (i, 0)), pl.BlockSpec((1, half), lambda i: (0, 0))],
        out_specs=[tab, tab],
        out_shape=[jax.ShapeDtypeStruct((S, half), F32), jax.ShapeDtypeStruct((S, half), F32)],
        compiler_params=_cparams(("parallel",)),
    )(pos_col, inv)


def _ret_consts(C, log_g):
    ii = lax.broadcasted_iota(jnp.int32, (C, C), 0)
    jj = lax.broadcasted_iota(jnp.int32, (C, C), 1)
    diff = (ii - jj).astype(F32)
    intra = jnp.where(ii >= jj, jnp.exp(log_g * jnp.maximum(diff, 0.0)), 0.0)
    idx = lax.broadcasted_iota(jnp.int32, (C, 1), 0).astype(F32)
    qd = jnp.exp(log_g * (idx + 1.0))
    kd = jnp.exp(log_g * (C - 1.0 - idx))
    cd = math.exp(log_g * C)
    return intra, qd, kd, cd


def _rot(t, cs, sn):
    half = t.shape[-1] // 2
    t1, t2 = t[:, :half], t[:, half:]
    return jnp.concatenate([t1 * cs - t2 * sn, t1 * sn + t2 * cs], axis=-1)


def _unrot(d, cs, sn):
    half = d.shape[-1] // 2
    d1, d2 = d[:, :half], d[:, half:]
    return jnp.concatenate([d1 * cs + d2 * sn, d2 * cs - d1 * sn], axis=-1)


def _ret_fwd(h, cos, sin, ret_g, mix, *, name, ch=2):
    S = h.shape[0]
    R = ret_g.shape[1]
    H, C = RET_HEADS, RET_CHUNK
    Dh = R // H
    ts = ch * C
    assert S % ts == 0
    log_gs = [math.log(1.0 - 2.0 ** (-5.0 - hd)) for hd in range(H)]
    scale = Dh ** -0.5

    def body(x_ref, c_ref, s_ref, rg_ref, mix_in, ret_ref, st_ref, mix_ref, state):
        i = pl.program_id(0)

        @pl.when(i == 0)
        def _():
            state[...] = jnp.zeros_like(state)

        for c in range(ch):
            rows = pl.ds(c * C, C)
            cs, sn = c_ref[rows, :], s_ref[rows, :]
            for hd in range(H):
                intra, qd, kd, cd = _ret_consts(C, log_gs[hd])
                q = x_ref[rows, pl.ds(hd * Dh, Dh)]
                k = x_ref[rows, pl.ds(R + hd * Dh, Dh)]
                v = x_ref[rows, pl.ds(2 * R + hd * Dh, Dh)]
                g = x_ref[rows, pl.ds(3 * R + hd * Dh, Dh)]
                rq = _rot(q, cs, sn)
                rk = _rot(k, cs, sn) * scale
                st = state[hd]
                st_ref[c, hd] = st.astype(BF16)
                s_ = _bdot(rq, rk, NT) * intra
                ret = _bdot(s_, v, NN) + _bdot(rq * qd, st, NN)
                state[hd] = st * cd + _bdot(rk * kd, v, TN)
                ret_ref[rows, pl.ds(hd * Dh, Dh)] = ret
                rr = lax.rsqrt(jnp.mean(ret * ret, axis=-1, keepdims=True) + EPS)
                out = ret * rr * rg_ref[:, pl.ds(hd * Dh, Dh)] * (g * _sigmoid(g))
                mix_ref[rows, pl.ds(hd * Dh, Dh)] = out.astype(BF16)

    n_chunks = S // C
    return pl.pallas_call(
        body, name=name, grid=(S // ts,),
        in_specs=[pl.BlockSpec((ts, 4 * R), lambda i: (i, 0)),
                  pl.BlockSpec((ts, Dh // 2), lambda i: (i, 0)), pl.BlockSpec((ts, Dh // 2), lambda i: (i, 0)),
                  pl.BlockSpec((1, R), lambda i: (0, 0)), pl.BlockSpec(memory_space=pl.ANY)],
        out_specs=[pl.BlockSpec((ts, R), lambda i: (i, 0)),
                   pl.BlockSpec((ch, H, Dh, Dh), lambda i: (i, 0, 0, 0)),
                   pl.BlockSpec((ts, R), lambda i: (i, 0))],
        out_shape=[jax.ShapeDtypeStruct((S, R), F32), jax.ShapeDtypeStruct((n_chunks, H, Dh, Dh), BF16),
                   jax.ShapeDtypeStruct(mix.shape, mix.dtype)],
        scratch_shapes=[pltpu.VMEM((H, Dh, Dh), F32)],
        input_output_aliases={4: 2},
        compiler_params=_cparams(("arbitrary",)),
    )(h, cos, sin, ret_g, mix)


def _ret_bwd(h, cos, sin, ret_g, states, ret_raw, dmix, *, name, ch=2):
    S = h.shape[0]
    R = ret_g.shape[1]
    H, C = RET_HEADS, RET_CHUNK
    Dh = R // H
    ts = ch * C
    nb = S // ts
    log_gs = [math.log(1.0 - 2.0 ** (-5.0 - hd)) for hd in range(H)]
    scale = Dh ** -0.5

    def body(x_ref, c_ref, s_ref, rg_ref, st_ref, ret_ref, dm_ref, dh_ref, drg_ref, dstate):
        i = pl.program_id(0)

        @pl.when(i == 0)
        def _():
            dstate[...] = jnp.zeros_like(dstate)
            drg_ref[...] = jnp.zeros_like(drg_ref)

        for c in reversed(range(ch)):
            rows = pl.ds(c * C, C)
            cs, sn = c_ref[rows, :], s_ref[rows, :]
            for hd in range(H):
                intra, qd, kd, cd = _ret_consts(C, log_gs[hd])
                cols = pl.ds(hd * Dh, Dh)
                q = x_ref[rows, pl.ds(hd * Dh, Dh)]
                k = x_ref[rows, pl.ds(R + hd * Dh, Dh)]
                v = x_ref[rows, pl.ds(2 * R + hd * Dh, Dh)]
                g = x_ref[rows, pl.ds(3 * R + hd * Dh, Dh)]
                rq = _rot(q, cs, sn)
                rk = _rot(k, cs, sn) * scale
                ret = ret_ref[rows, cols]
                dm = dm_ref[rows, cols].astype(F32)
                rgv = rg_ref[:, cols]
                rr = lax.rsqrt(jnp.mean(ret * ret, axis=-1, keepdims=True) + EPS)
                retn = ret * rr
                sg = _sigmoid(g)
                silu = g * sg
                drg_ref[:, cols] += jnp.sum(dm * retn * silu, axis=0, keepdims=True)
                dg = dm * retn * rgv * (sg * (1.0 + g * (1.0 - sg)))
                dretn = dm * rgv * silu
                d_o = rr * dretn - ret * (rr * rr * rr) * jnp.mean(ret * dretn, axis=-1, keepdims=True)
                st = st_ref[c, hd]
                d_s = dstate[hd]
                a_ = _bdot(rq, rk, NT) * intra
                d_a = _bdot(d_o, v, NT) * intra
                d_qr = _bdot(d_a, rk, NN) + _bdot(d_o, st, NT) * qd
                d_kr = _bdot(d_a, rq, TN) + _bdot(v, d_s, NT) * kd
                d_v = _bdot(a_, d_o, TN) + _bdot(rk * kd, d_s, NN)
                dstate[hd] = d_s * cd + _bdot(rq * qd, d_o, TN)
                dh_ref[rows, pl.ds(hd * Dh, Dh)] = _unrot(d_qr, cs, sn).astype(BF16)
                dh_ref[rows, pl.ds(R + hd * Dh, Dh)] = (_unrot(d_kr, cs, sn) * scale).astype(BF16)
                dh_ref[rows, pl.ds(2 * R + hd * Dh, Dh)] = d_v.astype(BF16)
                dh_ref[rows, pl.ds(3 * R + hd * Dh, Dh)] = dg.astype(BF16)

    rb = lambda i: nb - 1 - i
    return pl.pallas_call(
        body, name=name, grid=(nb,),
        in_specs=[pl.BlockSpec((ts, 4 * R), lambda i: (rb(i), 0)),
                  pl.BlockSpec((ts, Dh // 2), lambda i: (rb(i), 0)), pl.BlockSpec((ts, Dh // 2), lambda i: (rb(i), 0)),
                  pl.BlockSpec((1, R), lambda i: (0, 0)),
                  pl.BlockSpec((ch, H, Dh, Dh), lambda i: (rb(i), 0, 0, 0)),
                  pl.BlockSpec((ts, R), lambda i: (rb(i), 0)),
                  pl.BlockSpec((ts, R), lambda i: (rb(i), 0))],
        out_specs=[pl.BlockSpec((ts, 4 * R), lambda i: (rb(i), 0)), pl.BlockSpec((1, R), lambda i: (0, 0))],
        out_shape=[jax.ShapeDtypeStruct((S, 6 * R), BF16), jax.ShapeDtypeStruct((1, R), F32)],
        scratch_shapes=[pltpu.VMEM((H, Dh, Dh), F32)],
        compiler_params=_cparams(("arbitrary",)),
    )(h, cos, sin, ret_g, states, ret_raw, dmix)


GELU_C = math.sqrt(2.0 / math.pi)
GELU_A = 0.044715


def _gelu_parts(y):
    t = jnp.tanh(GELU_C * (y + GELU_A * y * y * y))
    val = 0.5 * y * (1.0 + t)
    grad = 0.5 * (1.0 + t) + 0.5 * y * (1.0 - t * t) * GELU_C * (1.0 + 3.0 * GELU_A * y * y)
    return val, grad


def _neg_expm1(x):
    series = -x * (1.0 + x * (1.0 / 2.0) * (1.0 + x * (1.0 / 3.0) * (1.0 + x * (1.0 / 4.0) * (
        1.0 + x * (1.0 / 5.0) * (1.0 + x * (1.0 / 6.0) * (1.0 + x * (1.0 / 7.0)))))))
    return jnp.where(x > -0.35, series, 1.0 - jnp.exp(x))


def _log_sigmoid(x):
    return jnp.minimum(x, 0.0) - jnp.log1p(jnp.exp(-jnp.abs(x)))


def _lru_gates(uc, wa_ref, ba_ref, wx_ref, bx_ref):
    nbk = wa_ref.shape[0]
    bd = wa_ref.shape[1]
    rs, gs = [], []
    for n in range(nbk):
        ucn = uc[:, n * bd:(n + 1) * bd]
        rs.append(_sigmoid(_bdot(ucn, wa_ref[n], NN) + ba_ref[:, pl.ds(n * bd, bd)]))
        gs.append(_sigmoid(_bdot(ucn, wx_ref[n], NN) + bx_ref[:, pl.ds(n * bd, bd)]))
    return jnp.concatenate(rs, axis=-1), jnp.concatenate(gs, axis=-1)


def _lru_fwd(h, conv_w, conv_b, wa, ba, wx, bx, lam, *, name, ts=256):
    S = h.shape[0]
    W = lam.shape[1]
    K = conv_w.shape[0]
    ts = min(ts, S)

    def body(u_ref, y_ref, cw_ref, cb_ref, wa_ref, ba_ref, wx_ref, bx_ref, lam_ref, hl_ref, mix_ref, tail, hlast):
        i = pl.program_id(0)

        @pl.when(i == 0)
        def _():
            tail[...] = jnp.zeros_like(tail)
            hlast[...] = jnp.zeros_like(hlast)

        u = u_ref[...]
        tl = tail[...]
        uc = cb_ref[...] + cw_ref[K - 1:K, :] * u
        for k in range(K - 1):
            uc = uc + cw_ref[k:k + 1, :] * _shift_down(u, tl, K - 1 - k)
        tail[...] = u[ts - SUB:ts]
        r, ig = _lru_gates(uc, wa_ref, ba_ref, wx_ref, bx_ref)
        log_a = LRU_C * r * _log_sigmoid(lam_ref[...])
        a = jnp.exp(log_a)
        b = jnp.sqrt(_neg_expm1(2.0 * log_a)) * (ig * uc)
        in_tile = _row_iota((ts, W)) & (SUB - 1)
        d = 1
        while d < SUB:
            a_s = jnp.where(in_tile < d, 1.0, pltpu.roll(a, d, 0))
            b_s = jnp.where(in_tile < d, 0.0, pltpu.roll(b, d, 0))
            b = a * b_s + b
            a = a * a_s
            d *= 2
        before = hlast[SUB - 1:SUB, :]
        for k in range(ts // SUB):
            tile = slice(k * SUB, (k + 1) * SUB)
            h_tile = a[tile] * before + b[tile]
            hl_ref[tile, :] = h_tile
            before = h_tile[SUB - 1:SUB, :]
        hlast[...] = hl_ref[ts - SUB:ts, :]
        gy, _ = _gelu_parts(y_ref[...])
        mix_ref[...] = (hl_ref[...] * gy).astype(BF16)

    full = lambda shape: pl.BlockSpec(shape, lambda i: tuple(0 for _ in shape))
    return pl.pallas_call(
        body, name=name, grid=(S // ts,),
        in_specs=[pl.BlockSpec((ts, W), lambda i: (i, 4)), pl.BlockSpec((ts, W), lambda i: (i, 5)),
                  full(conv_w.shape), full(conv_b.shape), full(wa.shape), full(ba.shape), full(wx.shape),
                  full(bx.shape), full(lam.shape)],
        out_specs=[pl.BlockSpec((ts, W), lambda i: (i, 0)), pl.BlockSpec((ts, W), lambda i: (i, 1))],
        out_shape=[jax.ShapeDtypeStruct((S, W), F32), jax.ShapeDtypeStruct((S, 2 * W), BF16)],
        scratch_shapes=[pltpu.VMEM((SUB, W), F32), pltpu.VMEM((SUB, W), F32)],
        compiler_params=_cparams(("arbitrary",)),
    )(h, h, conv_w, conv_b, wa, ba, wx, bx, lam)


def _lru_bwd(h, hl, dmix, dh, conv_w, conv_b, wa, ba, wx, bx, lam, *, name, ts=256):
    S = h.shape[0]
    W = lam.shape[1]
    K = conv_w.shape[0]
    nbk, bd = wa.shape[0], wa.shape[1]
    ts = min(ts, S)
    nb = S // ts
    t8 = ts // SUB

    def body(u_ref, y_ref, uh_ref, hl_ref, hh_ref, dm_ref, cw_ref, cb_ref, wa_ref, ba_ref, wx_ref, bx_ref, lam_ref,
             dh_in, dh_ref, dcw_ref, dcb_ref, dwa_ref, dba_ref, dwx_ref, dbx_ref, dlam_ref, carry, head, lam_buf):
        i = pl.program_id(0)
        blk = nb - 1 - i

        @pl.when(i == 0)
        def _():
            carry[...] = jnp.zeros_like(carry)
            head[...] = jnp.zeros_like(head)
            for ref in (dcw_ref, dcb_ref, dwa_ref, dba_ref, dwx_ref, dbx_ref, dlam_ref):
                ref[...] = jnp.zeros_like(ref)

        inside = (blk > 0).astype(F32)
        u = u_ref[...]
        tl = uh_ref[...] * inside
        sh = [_shift_down(u, tl, K - 1 - k) for k in range(K)]
        uc = cb_ref[...]
        for k in range(K):
            uc = uc + cw_ref[k:k + 1, :] * sh[k]
        r, ig = _lru_gates(uc, wa_ref, ba_ref, wx_ref, bx_ref)
        lam_v = lam_ref[...]
        ls = _log_sigmoid(lam_v)
        log_a = LRU_C * r * ls
        a = jnp.exp(log_a)
        mult = jnp.sqrt(_neg_expm1(2.0 * log_a))
        hcur = hl_ref[...]
        hprev = _shift_down(hcur, hh_ref[...] * inside, 1)
        gy, dgy = _gelu_parts(y_ref[...])
        dm = dm_ref[...].astype(F32)
        d_y = dm * hcur * dgy
        rid = _row_iota((ts, W))
        bq = dm * gy + jnp.where(rid == ts - 1, carry[0:1, :], 0.0)
        aq = jnp.where(rid == ts - 1, 0.0, pltpu.roll(a, ts - 1, 0))
        in_tile = rid & (SUB - 1)
        d = 1
        while d < SUB:
            a_s = jnp.where(in_tile >= SUB - d, 1.0, pltpu.roll(aq, ts - d, 0))
            b_s = jnp.where(in_tile >= SUB - d, 0.0, pltpu.roll(bq, ts - d, 0))
            bq = bq + aq * b_s
            aq = aq * a_s
            d *= 2
        after_row = jnp.zeros((1, W), F32)
        for k in reversed(range(ts // SUB)):
            tile = slice(k * SUB, (k + 1) * SUB)
            lam_tile = aq[tile] * after_row + bq[tile]
            lam_buf[tile, :] = lam_tile
            after_row = lam_tile[0:1, :]
        lam_t = lam_buf[...]
        carry[...] = (a * lam_t)[0:SUB]
        d_a = lam_t * hprev
        d_mult = lam_t * (ig * uc)
        d_i = lam_t * mult * uc
        d_uc = lam_t * mult * ig
        d_log_a = d_a * a - d_mult * (a * a) / mult
        d_r = d_log_a * (LRU_C * ls)
        dlam_ref[...] += jnp.sum(d_log_a * (LRU_C * r), axis=0, keepdims=True) * _sigmoid(-lam_v)
        d_pr = d_r * r * (1.0 - r)
        d_pi = d_i * ig * (1.0 - ig)
        dba_ref[...] += jnp.sum(d_pr, axis=0, keepdims=True)
        dbx_ref[...] += jnp.sum(d_pi, axis=0, keepdims=True)
        extra = []
        for n in range(nbk):
            sl = slice(n * bd, (n + 1) * bd)
            ucn = uc[:, sl]
            dwa_ref[n] += _bdot(ucn, d_pr[:, sl], TN)
            dwx_ref[n] += _bdot(ucn, d_pi[:, sl], TN)
            extra.append(_bdot(d_pr[:, sl], wa_ref[n], NT) + _bdot(d_pi[:, sl], wx_ref[n], NT))
        d_uc = d_uc + jnp.concatenate(extra, axis=-1)
        dcb_ref[...] += jnp.sum(d_uc, axis=0, keepdims=True)
        rid8 = _row_iota((SUB, W))
        dcw = jnp.zeros((SUB, W), F32)
        for k in range(K):
            dcw = dcw + jnp.where(rid8 == k, jnp.sum(d_uc * sh[k], axis=0, keepdims=True), 0.0)
        dcw_ref[...] += dcw
        hd = head[...]
        d_u = cw_ref[K - 1:K, :] * d_uc
        for j in range(1, K):
            d_u = d_u + cw_ref[K - 1 - j:K - j, :] * _shift_up(d_uc, hd, j)
        head[...] = d_uc[0:SUB]
        dh_ref[:, 0:W] = d_u.astype(BF16)
        dh_ref[:, W:2 * W] = d_y.astype(BF16)

    rb = lambda i: nb - 1 - i
    prev8 = lambda i: jnp.maximum(rb(i) * t8 - 1, 0)
    full = lambda shape: pl.BlockSpec(shape, lambda i: tuple(0 for _ in shape))
    small = [jax.ShapeDtypeStruct((SUB, W), F32), jax.ShapeDtypeStruct((1, W), F32),
             jax.ShapeDtypeStruct(wa.shape, F32), jax.ShapeDtypeStruct((1, W), F32),
             jax.ShapeDtypeStruct(wx.shape, F32), jax.ShapeDtypeStruct((1, W), F32),
             jax.ShapeDtypeStruct((1, W), F32)]
    return pl.pallas_call(
        body, name=name, grid=(nb,),
        in_specs=[pl.BlockSpec((ts, W), lambda i: (rb(i), 4)), pl.BlockSpec((ts, W), lambda i: (rb(i), 5)),
                  pl.BlockSpec((SUB, W), lambda i: (prev8(i), 4)),
                  pl.BlockSpec((ts, W), lambda i: (rb(i), 0)), pl.BlockSpec((SUB, W), lambda i: (prev8(i), 0)),
                  pl.BlockSpec((ts, W), lambda i: (rb(i), 1)),
                  full(conv_w.shape), full(conv_b.shape), full(wa.shape), full(ba.shape), full(wx.shape),
                  full(bx.shape), full(lam.shape), pl.BlockSpec(memory_space=pl.ANY)],
        out_specs=[pl.BlockSpec((ts, 2 * W), lambda i: (rb(i), 2))] + [full(s.shape) for s in small],
        out_shape=[jax.ShapeDtypeStruct(dh.shape, dh.dtype)] + small,
        scratch_shapes=[pltpu.VMEM((SUB, W), F32), pltpu.VMEM((SUB, W), F32), pltpu.VMEM((ts, W), F32)],
        input_output_aliases={13: 0},
        compiler_params=_cparams(("arbitrary",)),
    )(h, h, h, hl, hl, dmix, conv_w, conv_b, wa, ba, wx, bx, lam, dh)


def _xattn_fwd(q, km, vm, *, name, ts=512):
    S, D = q.shape
    M = km.shape[0]
    H = XA_HEADS
    Dh = D // H
    ts = min(ts, S)
    scale = Dh ** -0.5

    def body(q_ref, k_ref, v_ref, o_ref):
        for hd in range(H):
            cols = pl.ds(hd * Dh, Dh)
            s = _bdot(q_ref[:, cols], k_ref[:, cols], NT) * scale
            s = s - jnp.max(s, axis=-1, keepdims=True)
            e = jnp.exp(s)
            p = e / jnp.sum(e, axis=-1, keepdims=True)
            o_ref[:, cols] = _bdot(p, v_ref[:, cols], NN).astype(o_ref.dtype)

    return pl.pallas_call(
        body, name=name, grid=(S // ts,),
        in_specs=[pl.BlockSpec((ts, D), lambda i: (i, 0)), pl.BlockSpec((M, D), lambda i: (0, 0)),
                  pl.BlockSpec((M, D), lambda i: (0, 0))],
        out_specs=pl.BlockSpec((ts, D), lambda i: (i, 0)),
        out_shape=jax.ShapeDtypeStruct((S, D), BF16),
        compiler_params=_cparams(("parallel",)),
    )(q, km, vm)


def _xattn_bwd(q, km, vm, d_o, *, name, ts=512):
    S, D = q.shape
    M = km.shape[0]
    H = XA_HEADS
    Dh = D // H
    ts = min(ts, S)
    scale = Dh ** -0.5

    def body(q_ref, k_ref, v_ref, do_ref, dq_ref, dk_ref, dv_ref):
        i = pl.program_id(0)

        @pl.when(i == 0)
        def _():
            dk_ref[...] = jnp.zeros_like(dk_ref)
            dv_ref[...] = jnp.zeros_like(dv_ref)

        for hd in range(H):
            cols = pl.ds(hd * Dh, Dh)
            qh, kh, vh, doh = q_ref[:, cols], k_ref[:, cols], v_ref[:, cols], do_ref[:, cols]
            s = _bdot(qh, kh, NT) * scale
            s = s - jnp.max(s, axis=-1, keepdims=True)
            e = jnp.exp(s)
            p = e / jnp.sum(e, axis=-1, keepdims=True)
            dp = _bdot(doh, vh, NT)
            ds = p * (dp - jnp.sum(dp * p, axis=-1, keepdims=True)) * scale
            dq_ref[:, cols] = _bdot(ds, kh, NN).astype(dq_ref.dtype)
            dk_ref[:, cols] += _bdot(ds, qh, TN)
            dv_ref[:, cols] += _bdot(p, doh, TN)

    row = pl.BlockSpec((ts, D), lambda i: (i, 0))
    mem = pl.BlockSpec((M, D), lambda i: (0, 0))
    return pl.pallas_call(
        body, name=name, grid=(S // ts,), in_specs=[row, mem, mem, row], out_specs=[row, mem, mem],
        out_shape=[jax.ShapeDtypeStruct((S, D), BF16), jax.ShapeDtypeStruct((M, D), F32),
                   jax.ShapeDtypeStruct((M, D), F32)],
        compiler_params=_cparams(("arbitrary",)),
    )(q, km, vm, d_o)


def _conv_rows(v, tail, cw_ref, cb_ref):
    K = cw_ref.shape[0]
    sh = [_shift_down(v, tail, K - 1 - k) for k in range(K)]
    out = cb_ref[...]
    for k in range(K):
        out = out + cw_ref[k:k + 1, :] * sh[k]
    return out, sh


FFN_SUB_FWD = 128
FFN_SUB_BWD = 256


def _ffn_up_gate(xn, w_up, cw, cb, *, name, tm=1024, tn=512):
    S, D = xn.shape
    F2 = w_up.shape[1]
    F = F2 // 2
    tm, tn = min(tm, S), min(tn, F)
    sub = min(FFN_SUB_FWD, tm)
    nj = F // tn
    K = cw.shape[0]

    def body(x_ref, wa_ref, wb_ref, cwa_ref, cwb_ref, cba_ref, cbb_ref, act_ref, ha_ref, hb_ref, ac_ref, bc_ref, ta, tb):
        i = pl.program_id(1)

        @pl.when(i == 0)
        def _():
            ta[...] = jnp.zeros_like(ta)
            tb[...] = jnp.zeros_like(tb)

        tail_a, tail_b = ta[...], tb[...]
        for s in range(tm // sub):
            rows = pl.ds(s * sub, sub)
            xs = x_ref[rows, :]
            ha = _bdot(xs, wa_ref[...], NN)
            hb = _bdot(xs, wb_ref[...], NN)
            ac, _ = _conv_rows(ha, tail_a, cwa_ref, cba_ref)
            bc, _ = _conv_rows(hb, tail_b, cwb_ref, cbb_ref)
            tail_a, tail_b = ha[sub - SUB:sub], hb[sub - SUB:sub]
            ha_ref[rows, :] = ha
            hb_ref[rows, :] = hb
            ac_ref[rows, :] = ac
            bc_ref[rows, :] = bc
            act_ref[rows, :] = (ac * _sigmoid(ac) * bc).astype(act_ref.dtype)
        ta[...] = tail_a
        tb[...] = tail_b

    blk = pl.BlockSpec((tm, tn), lambda j, i: (i, j))
    return pl.pallas_call(
        body, name=name, grid=(nj, S // tm),
        in_specs=[pl.BlockSpec((tm, D), lambda j, i: (i, 0)),
                  pl.BlockSpec((D, tn), lambda j, i: (0, j)), pl.BlockSpec((D, tn), lambda j, i: (0, j + nj)),
                  pl.BlockSpec((K, tn), lambda j, i: (0, j)), pl.BlockSpec((K, tn), lambda j, i: (0, j + nj)),
                  pl.BlockSpec((1, tn), lambda j, i: (0, j)), pl.BlockSpec((1, tn), lambda j, i: (0, j + nj))],
        out_specs=[blk] * 5,
        out_shape=[jax.ShapeDtypeStruct((S, F), BF16)] + [jax.ShapeDtypeStruct((S, F), F32)] * 4,
        scratch_shapes=[pltpu.VMEM((SUB, tn), F32), pltpu.VMEM((SUB, tn), F32)],
        compiler_params=_cparams(("parallel", "arbitrary")),
    )(xn, w_up, w_up, cw, cw, cb, cb)


def _ffn_bwd(dx, w_down, hh_a, hh_b, c_a, c_b, act, xn, cw, *, name, tm=1024, tn=256):
    S, D = dx.shape
    F = hh_a.shape[1]
    tm, tn = min(tm, S), min(tn, F)
    sub = min(FFN_SUB_BWD, tm)
    nj = F // tn
    nb = S // tm
    K = cw.shape[0]

    def body(dx_ref, wd_ref, a_ref, b_ref, ac_ref, bc_ref, act_ref, xn_ref, cwa_ref, cwb_ref,
             da_ref, db_ref, ga_ref, gb_ref, dwd_ref, dwu_ref, ha, hb, acc_d, acc_a, acc_b):
        i = pl.program_id(1)

        @pl.when(i == 0)
        def _():
            for ref in (ha, hb, ga_ref, gb_ref, acc_d, acc_a, acc_b):
                ref[...] = jnp.zeros_like(ref)

        rid8 = _row_iota((SUB, tn))
        heads = [ha[...], hb[...]]
        gsums = [jnp.zeros((SUB, tn), F32), jnp.zeros((SUB, tn), F32)]
        for s in reversed(range(tm // sub)):
            rows = pl.ds(s * sub, sub)
            dv = _bdot(dx_ref[rows, :], wd_ref[...], NT)
            ac, bc = ac_ref[rows, :], bc_ref[rows, :]
            sg = _sigmoid(ac)
            d_bc = dv * ac * sg
            d_ac = dv * bc * sg * (1.0 + ac * (1.0 - sg))
            for which, (d_c, h_ref, cw_ref, o_ref) in enumerate(((d_ac, a_ref, cwa_ref, da_ref),
                                                                 (d_bc, b_ref, cwb_ref, db_ref))):
                ahead = [d_c] + [_shift_up(d_c, heads[which], j) for j in range(1, K)]
                heads[which] = d_c[0:SUB]
                d_in = cw_ref[K - 1:K, :] * d_c
                for j in range(1, K):
                    d_in = d_in + cw_ref[K - 1 - j:K - j, :] * ahead[j]
                o_ref[rows, :] = d_in.astype(o_ref.dtype)
                hv = h_ref[rows, :]
                gsum = gsums[which] + jnp.where(rid8 == K, jnp.sum(d_c, axis=0, keepdims=True), 0.0)
                for k in range(K):
                    gsum = gsum + jnp.where(rid8 == k, jnp.sum(ahead[K - 1 - k] * hv, axis=0, keepdims=True), 0.0)
                gsums[which] = gsum
        ha[...], hb[...] = heads
        ga_ref[...] += gsums[0]
        gb_ref[...] += gsums[1]
        acc_d[...] += _bdot(act_ref[...], dx_ref[...], TN)
        acc_a[...] += _bdot(xn_ref[...], da_ref[...], TN)
        acc_b[...] += _bdot(xn_ref[...], db_ref[...], TN)

        @pl.when(i == nb - 1)
        def _():
            dwd_ref[...] = acc_d[...].astype(dwd_ref.dtype)
            dwu_ref[0] = acc_a[...].astype(dwu_ref.dtype)
            dwu_ref[1] = acc_b[...].astype(dwu_ref.dtype)

    rb = lambda i: nb - 1 - i
    blk = pl.BlockSpec((tm, tn), lambda j, i: (rb(i), j))
    acc = pl.BlockSpec((SUB, tn), lambda j, i: (0, j))
    rows_d = pl.BlockSpec((tm, D), lambda j, i: (rb(i), 0))
    return pl.pallas_call(
        body, name=name, grid=(nj, nb),
        in_specs=[rows_d, pl.BlockSpec((tn, D), lambda j, i: (j, 0)), blk, blk, blk, blk, blk, rows_d,
                  pl.BlockSpec((K, tn), lambda j, i: (0, j)), pl.BlockSpec((K, tn), lambda j, i: (0, j + nj))],
        out_specs=[blk, blk, acc, acc, pl.BlockSpec((tn, D), lambda j, i: (j, 0)),
                   pl.BlockSpec((2, D, tn), lambda j, i: (0, 0, j))],
        out_shape=[jax.ShapeDtypeStruct((S, F), BF16), jax.ShapeDtypeStruct((S, F), BF16),
                   jax.ShapeDtypeStruct((SUB, F), F32), jax.ShapeDtypeStruct((SUB, F), F32),
                   jax.ShapeDtypeStruct((F, D), BF16), jax.ShapeDtypeStruct((2, D, F), BF16)],
        scratch_shapes=[pltpu.VMEM((SUB, tn), F32), pltpu.VMEM((SUB, tn), F32), pltpu.VMEM((tn, D), F32),
                        pltpu.VMEM((D, tn), F32), pltpu.VMEM((D, tn), F32)],
        compiler_params=_cparams(("parallel", "arbitrary")),
    )(dx, w_down, hh_a, hh_b, c_a, c_b, act, xn, cw, cw)


ADAM_BLOCK_ELEMS = 256 * 1024


def _adamw(w, m, v, parts, *, name):
    R, C = w.shape
    n = parts.shape[0]
    tr = R
    for cand in (1024, 512, 256, 128, 64, 32, 16):
        if R % cand == 0 and cand * C <= ADAM_BLOCK_ELEMS:
            tr = cand
            break
    c1 = 1.0 - ADAM_B1 ** ADAM_STEP
    c2 = 1.0 - ADAM_B2 ** ADAM_STEP

    def body(w_ref, m_ref, v_ref, p_ref, g_ref, d_ref, nm_ref, nv_ref):
        g = p_ref[0].astype(F32)
        for k in range(1, n):
            g = g + p_ref[k].astype(F32)
        m_new = ADAM_B1 * m_ref[...] + (1.0 - ADAM_B1) * g
        v_new = ADAM_B2 * v_ref[...] + (1.0 - ADAM_B2) * (g * g)
        m_hat = m_new / c1
        v_hat = v_new / c2
        g_ref[...] = g
        d_ref[...] = -ADAM_LR * (m_hat / (jnp.sqrt(v_hat) + ADAM_EPS) + ADAM_WD * w_ref[...])
        nm_ref[...] = m_new
        nv_ref[...] = v_new

    blk = pl.BlockSpec((tr, C), lambda i: (i, 0))
    sds = jax.ShapeDtypeStruct((R, C), F32)
    return pl.pallas_call(
        body, name=name, grid=(R // tr,),
        in_specs=[blk, blk, blk, pl.BlockSpec((n, tr, C), lambda i: (0, i, 0))],
        out_specs=[blk, blk, blk, blk], out_shape=[sds, sds, sds, sds],
        compiler_params=_cparams(("parallel",)),
    )(w, m, v, parts)


def _mesh_place():
    x, y, c = lax.axis_index("x"), lax.axis_index("y"), lax.axis_index("c")
    others = [(1 - x, y), (x, 1 - y), (1 - x, 1 - y)]
    return x, y, c, others


HBM_SPEC = pl.BlockSpec(memory_space=pltpu.HBM)
SEM_SPEC = pl.BlockSpec(memory_space=pltpu.SEMAPHORE)
ANY_SPEC = pl.BlockSpec(memory_space=pl.ANY)
EFFECT = pltpu.SideEffectType.DATAFLOW_SIDE_EFFECTING


def _in_hbm(a):
    return pltpu.with_memory_space_constraint(a, pltpu.HBM)


def _split_start(srcs, lands, copies, n_cp, *, name):
    n_s, n_l = len(srcs), len(lands)

    def body(*refs):
        src_refs, land_refs = refs[:n_s], refs[n_s:n_s + n_l]
        ssem, rsem = refs[n_s + n_l], refs[n_s + n_l + 1]
        token = refs[-1]
        for outgoing, _ in copies(src_refs, land_refs, ssem, rsem):
            outgoing.start()
        token[...] = jnp.zeros_like(token)

    outs = pl.pallas_call(
        body, name=name,
        out_shape=(pltpu.SemaphoreType.DMA((n_cp,)), pltpu.SemaphoreType.DMA((n_cp,)),
                   *[pltpu.HBM(a.shape, a.dtype) for a in srcs], *[pltpu.HBM(a.shape, a.dtype) for a in lands],
                   jax.ShapeDtypeStruct((SUB, LANE), F32)),
        in_specs=[HBM_SPEC] * (n_s + n_l),
        out_specs=(SEM_SPEC, SEM_SPEC, *[HBM_SPEC] * (n_s + n_l), pl.BlockSpec(memory_space=pltpu.VMEM)),
        input_output_aliases={i: 2 + i for i in range(n_s + n_l)},
        compiler_params=pltpu.CompilerParams(has_side_effects=EFFECT),
    )(*[_in_hbm(a) for a in srcs], *[_in_hbm(a) for a in lands])
    ssem, rsem = outs[0], outs[1]
    return ssem, rsem, list(outs[2:2 + n_s]), list(outs[2 + n_s:2 + n_s + n_l]), outs[-1]


def _split_wait(srcs, lands, ssem, rsem, after, copies, *, name):
    n_s, n_l = len(srcs), len(lands)

    def body(*refs):
        src_refs, land_refs = refs[:n_s], refs[n_s:n_s + n_l]
        s_ref, r_ref = refs[n_s + n_l], refs[n_s + n_l + 1]
        for outgoing, incoming in copies(src_refs, land_refs, s_ref, r_ref):
            outgoing.wait_send()
            incoming.wait_recv()

    outs = pl.pallas_call(
        body, name=name,
        out_shape=(*[pltpu.HBM(a.shape, a.dtype) for a in srcs], *[pltpu.HBM(a.shape, a.dtype) for a in lands]),
        in_specs=[HBM_SPEC] * (n_s + n_l) + [SEM_SPEC, SEM_SPEC, ANY_SPEC], out_specs=[HBM_SPEC] * (n_s + n_l),
        input_output_aliases={i: i for i in range(n_s + n_l)},
        compiler_params=pltpu.CompilerParams(has_side_effects=EFFECT),
    )(*srcs, *lands, ssem, rsem, after)
    return list(outs[:n_s]), list(outs[n_s:])


PLACE_BLOCK_ELEMS = 512 * 1024


def _place_rows(r, w):
    return _div_tile(r, max(16, PLACE_BLOCK_ELEMS // w), 16)


def _cast_place(shard, chip, axis, after, *, name):
    r, w = shard.shape
    tr = _place_rows(r, w)
    nb = r // tr
    full = (r * N_CHIP, w) if axis == 0 else (r, w * N_CHIP)
    has_after = after is not None

    def body(chip_ref, s_ref, *rest):
        rest[-1][...] = s_ref[...].astype(BF16)

    out_map = (lambda i, ch: (ch[0] * nb + i, 0)) if axis == 0 else (lambda i, ch: (i, ch[0]))
    grid_spec = pltpu.PrefetchScalarGridSpec(
        num_scalar_prefetch=1, grid=(nb,),
        in_specs=[pl.BlockSpec((tr, w), lambda i, ch: (i, 0))] + has_after * [ANY_SPEC],
        out_specs=pl.BlockSpec((tr, w), out_map))
    return pl.pallas_call(body, name=name, grid_spec=grid_spec, out_shape=jax.ShapeDtypeStruct(full, BF16),
                          compiler_params=_cparams(("parallel",)))(chip, shard, *(has_after * [after]))


def _grad_shard_shape(g, axis):
    if g.ndim == 3:
        return g.shape[1], 2 * g.shape[2] // N_CHIP
    return (g.shape[0] // N_CHIP, g.shape[1]) if axis == 0 else (g.shape[0], g.shape[1] // N_CHIP)


def _slot_place(g, ids, axis, *, name):
    r, w = _grad_shard_shape(g, axis)
    tr = _place_rows(r, w)
    nb = r // tr

    def body(ids_ref, g_ref, o_ref):
        o_ref[...] = g_ref[...]

    if g.ndim == 3:
        in_spec = pl.BlockSpec((None, tr, w), lambda i, ids_: (ids_[0] // 2, i, ids_[0] % 2))
    elif axis == 0:
        in_spec = pl.BlockSpec((tr, w), lambda i, ids_: (ids_[0] * nb + i, 0))
    else:
        in_spec = pl.BlockSpec((tr, w), lambda i, ids_: (i, ids_[0]))
    grid_spec = pltpu.PrefetchScalarGridSpec(
        num_scalar_prefetch=1, grid=(nb,), in_specs=[in_spec],
        out_specs=pl.BlockSpec((None, tr, w), lambda i, ids_: (ids_[1], i, 0)))
    return pl.pallas_call(body, name=name, grid_spec=grid_spec, out_shape=jax.ShapeDtypeStruct((N_DEV, r, w), g.dtype),
                          compiler_params=_cparams(("parallel",)))(ids, g)


class _WeightGather:
    def __init__(self, placed, shard_shapes, axes, splits, tag):
        self.placed, self.shard_shapes, self.axes, self.splits, self.tag = list(placed), shard_shapes, axes, splits, tag
        self.n = len(placed)

    def _region(self, land_refs, it, chip, half):
        r, w = self.shard_shapes[it]
        by_rows = self.axes[it] == 0
        if self.splits[it] and half is not None:
            rows = pl.ds(pl.multiple_of(half * (r // 2) + (chip * r if by_rows else 0), 16), r // 2)
        else:
            rows = pl.ds(chip * r if by_rows else 0, r)
        cols = pl.ds(0, w) if by_rows else pl.ds(pl.multiple_of(chip * w, LANE), w)
        return land_refs[it].at[rows, cols]

    def _ici(self, src_refs, land_refs, ssem, rsem):
        x, y, c, others = _mesh_place()
        pairs = []
        for it in range(self.n):
            for j, chip in enumerate(others):
                def mk(chip_from, it=it, j=j, chip=chip):
                    return pltpu.make_async_remote_copy(
                        src_ref=self._region(land_refs, it, 2 * x + y, c), dst_ref=self._region(land_refs, it, chip_from, c),
                        send_sem=ssem.at[3 * it + j], recv_sem=rsem.at[3 * it + j], device_id=(*chip, c),
                        device_id_type=MESH)
                pairs.append((mk(2 * x + y), mk(2 * chip[0] + chip[1])))
        return pairs

    def start(self):
        self.ssem, self.rsem, _, self.lands, token = _split_start(
            [], self.placed, self._ici, 3 * self.n, name="gather_start_" + self.tag)
        return token

    def _d2d(self, src_refs, land_refs, ssem, rsem):
        x, y, c, others = _mesh_place()
        pairs = []
        for it in range(self.n):
            if self.splits[it]:
                for chip in others:
                    def mk(half, it=it, chip=chip, k=len(pairs)):
                        reg = self._region(land_refs, it, 2 * chip[0] + chip[1], half)
                        return pltpu.make_async_remote_copy(src_ref=reg, dst_ref=reg, send_sem=ssem.at[k], recv_sem=rsem.at[k],
                                                            device_id=(x, y, 1 - c), device_id_type=MESH)
                    pairs.append((mk(c), mk(1 - c)))
        return pairs

    def forward(self, after):
        _, lands = _split_wait([], self.lands, self.ssem, self.rsem, after, self._ici,
                               name="gather_wait_" + self.tag)
        self.fsem, self.frsem, _, self.lands, token = _split_start(
            [], lands, self._d2d, 3 * sum(self.splits), name="gather_fwd_" + self.tag)
        return token

    def finish_forward(self, after):
        _, lands = _split_wait([], self.lands, self.fsem, self.frsem, after, self._d2d,
                               name="gather_fwd_wait_" + self.tag)
        return lands

    def finish(self, after):
        _, lands = _split_wait([], self.lands, self.ssem, self.rsem, after, self._ici,
                               name="gather_wait_" + self.tag)
        n = self.n
        n_fwd = 3 * sum(self.splits)
        if n_fwd == 0:
            return lands

        def body(*refs):
            out_refs = refs[n:2 * n]
            fsend, frecv = refs[2 * n:]
            x, y, c, others = _mesh_place()
            sibling = (x, y, 1 - c)

            def fwd(it, slot, chip, half):
                reg = self._region(out_refs, it, 2 * chip[0] + chip[1], half)
                return pltpu.make_async_remote_copy(src_ref=reg, dst_ref=reg, send_sem=fsend.at[slot],
                                                    recv_sem=frecv.at[slot], device_id=sibling, device_id_type=MESH)

            sends, recvs = [], []
            for it in range(n):
                if self.splits[it]:
                    for chip in others:
                        sends.append(fwd(it, len(sends), chip, c))
                        recvs.append(fwd(it, len(recvs), chip, 1 - c))
            for cp in sends:
                cp.start()
            for cp in recvs:
                cp.wait_recv()
            for cp in sends:
                cp.wait_send()

        fulls = pl.pallas_call(
            body, name="gather_d2d_" + self.tag, in_specs=[ANY_SPEC] * n, out_specs=[ANY_SPEC] * n,
            out_shape=[jax.ShapeDtypeStruct(a.shape, a.dtype) for a in lands],
            scratch_shapes=[pltpu.SemaphoreType.DMA((n_fwd,)), pltpu.SemaphoreType.DMA((n_fwd,))],
            input_output_aliases={i: i for i in range(n)},
        )(*lands)
        return list(fulls)


class _GradGather:
    def __init__(self, grads, axes, tag):
        self.grads, self.axes, self.tag = list(grads), axes, tag
        self.n = len(grads)
        self.shard_shapes = [_grad_shard_shape(g, ax) for g, ax in zip(grads, axes)]

    def _piece(self, src_refs, it, chip):
        r, w = self.shard_shapes[it]
        if self.grads[it].ndim == 3:
            return src_refs[it].at[chip // 2, :, pl.ds(pl.multiple_of((chip % 2) * w, LANE), w)]
        if self.axes[it] == 0:
            return src_refs[it].at[pl.ds(pl.multiple_of(chip * r, 16), r), :]
        return src_refs[it].at[:, pl.ds(pl.multiple_of(chip * w, LANE), w)]

    PER_ITEM = 4

    def _remote(self, src_refs, land_refs, ssem, rsem):
        x, y, c, others = _mesh_place()
        me = 4 * x + 2 * y + c
        pairs = []
        for it in range(self.n):
            def mk(k, piece_chip, slot, to, it=it):
                return pltpu.make_async_remote_copy(
                    src_ref=self._piece(src_refs, it, piece_chip), dst_ref=land_refs[it].at[slot],
                    send_sem=ssem.at[self.PER_ITEM * it + k], recv_sem=rsem.at[self.PER_ITEM * it + k], device_id=to,
                    device_id_type=MESH)
            for j, chip in enumerate(others):
                chip_id = 2 * chip[0] + chip[1]
                pairs.append((mk(j, chip_id, me, (*chip, c)), mk(j, chip_id, 2 * chip_id + c, (*chip, c))))
            sibling = (x, y, 1 - c)
            pairs.append((mk(3, 2 * x + y, me, sibling), mk(3, 2 * x + y, 4 * x + 2 * y + 1 - c, sibling)))
        return pairs

    def start(self):
        x, y, c = lax.axis_index("x"), lax.axis_index("y"), lax.axis_index("c")
        ids = jnp.stack([2 * x + y, 4 * x + 2 * y + c]).astype(jnp.int32)
        lands = [_slot_place(g, ids, ax, name="grads_own_%s%d" % (self.tag, it))
                 for it, (g, ax) in enumerate(zip(self.grads, self.axes))]
        self.ssem, self.rsem, self.srcs, self.lands, token = _split_start(
            self.grads, lands, self._remote, self.PER_ITEM * self.n, name="grads_start_" + self.tag)
        return token

    def _forward(self, src_refs, land_refs, ssem, rsem):
        x, y, c, others = _mesh_place()
        pairs = []
        for it in range(self.n):
            for j, ch in enumerate(others):
                def mk(slot, it=it, j=j):
                    return pltpu.make_async_remote_copy(
                        src_ref=land_refs[it].at[slot], dst_ref=land_refs[it].at[slot], send_sem=ssem.at[3 * it + j],
                        recv_sem=rsem.at[3 * it + j], device_id=(x, y, 1 - c), device_id_type=MESH)
                pairs.append((mk(4 * ch[0] + 2 * ch[1] + c), mk(4 * ch[0] + 2 * ch[1] + 1 - c)))
        return pairs

    def forward(self, after):
        _, lands = _split_wait(self.srcs, self.lands, self.ssem, self.rsem, after, self._remote,
                               name="grads_wait_" + self.tag)
        self.fsem, self.frsem, _, self.lands, token = _split_start(
            [], lands, self._forward, 3 * self.n, name="grads_fwd_" + self.tag)
        return token

    def finish(self, after):
        _, lands = _split_wait([], self.lands, self.fsem, self.frsem, after, self._forward,
                               name="grads_fwd_wait_" + self.tag)
        return lands


def _allreduce_small(vec, *, name):
    R, L = vec.shape

    def body(v_ref, o_ref, buf, send, recv, lsem):
        x, y, c, others = _mesh_place()
        me = 4 * x + 2 * y + c
        sibling = (x, y, 1 - c)

        def copy(k, slot, to, src=None):
            return pltpu.make_async_remote_copy(
                src_ref=buf.at[slot] if src is None else src, dst_ref=buf.at[slot], send_sem=send.at[k],
                recv_sem=recv.at[k], device_id=to, device_id_type=MESH)

        def slot_of(chip, core):
            return 4 * chip[0] + 2 * chip[1] + core

        mine = pltpu.make_async_copy(v_ref, buf.at[me], lsem)
        mine.start()
        first = [copy(0, me, sibling, src=v_ref)]
        first += [copy(1 + j, me, (*chip, c), src=v_ref) for j, chip in enumerate(others)]
        for cp in first:
            cp.start()
        passed = [copy(4 + j, slot_of(chip, c), sibling) for j, chip in enumerate(others)]
        for j, chip in enumerate(others):
            copy(1 + j, slot_of(chip, c), (*chip, c)).wait_recv()
            passed[j].start()
        copy(0, slot_of((x, y), 1 - c), sibling).wait_recv()
        for j, chip in enumerate(others):
            copy(4 + j, slot_of(chip, 1 - c), sibling).wait_recv()
        for cp in first + passed:
            cp.wait_send()
        mine.wait()
        total = buf[0]
        for k in range(1, N_DEV):
            total = total + buf[k]
        o_ref[...] = total

    return pl.pallas_call(
        body, name=name, in_specs=[pl.BlockSpec(memory_space=pltpu.VMEM)],
        out_specs=pl.BlockSpec(memory_space=pltpu.VMEM), out_shape=jax.ShapeDtypeStruct((R, L), F32),
        scratch_shapes=[pltpu.VMEM((N_DEV, R, L), F32), pltpu.SemaphoreType.DMA((7,)), pltpu.SemaphoreType.DMA((7,)),
                        pltpu.SemaphoreType.DMA],
        compiler_params=pltpu.CompilerParams(vmem_limit_bytes=VMEM_LIMIT),
    )(vec)


PACK_ALIGN = 1024


def _pack(arrs, row_multiple):
    flat = []
    for a in arrs:
        f = a.reshape(-1).astype(F32)
        flat.append(jnp.pad(f, (0, (-f.shape[0]) % PACK_ALIGN)))
    v = jnp.concatenate(flat)
    v = jnp.pad(v, (0, (-v.shape[0]) % (LANE * row_multiple)))
    return v.reshape(-1, LANE)


def _unpack(v, shapes):
    flat = v.reshape(-1)
    out, off = [], 0
    for s in shapes:
        size = math.prod(s)
        out.append(flat[off:off + size].reshape(s))
        off += size + (-size) % PACK_ALIGN
    return out


def _div_tile(dim, cap, mult=LANE):
    best = None
    for cand in range(mult, min(cap, dim) + 1, mult):
        if dim % cand == 0:
            best = cand
    return dim if best is None else best


WEIGHT_NAMES = ('norm1_g', 'w_in', 'ret_g', 'rg_conv_w', 'rg_conv_b', 'rg_wa', 'rg_ba', 'rg_wx', 'rg_bx', 'rg_lambda',
                'w_out', 'norm2_g', 'norm_mem_g', 'xa_wq', 'xa_wk', 'xa_wv', 'xa_wo', 'norm3_g', 'ffn_w_up',
                'ffn_conv_w', 'ffn_conv_b', 'ffn_w_down', 'final_g')
BIG_AXIS = {'w_in': 1, 'w_out': 0, 'xa_wq': 0, 'xa_wk': 0, 'xa_wv': 0, 'xa_wo': 0, 'ffn_w_up': 1, 'ffn_w_down': 0}
SMALL_SHARDED = ('rg_conv_w', 'ffn_conv_w')


def _step(x, mem, positions, loss_target, W, Mo, Vo):
    S, D = x.shape[1], x.shape[2]
    xs, mems, tgt = x[0], mem[0], loss_target[0]
    n_mem = mems.shape[0]
    pos_col = positions.reshape(S, 1)
    chip = 2 * lax.axis_index("x") + lax.axis_index("y")

    big = list(BIG_AXIS)
    shards = {n: W[n][0] for n in big}
    G = {}
    gather_groups = (('w_in', 'rg_conv_w'), ('w_out', 'xa_wq', 'xa_wk', 'xa_wv', 'xa_wo'),
                     ('ffn_w_up', 'ffn_conv_w'), ('ffn_w_down',))
    gathers, tok = [], None
    chip1 = jnp.reshape(chip, (1,)).astype(jnp.int32)
    for gi, names in enumerate(gather_groups):
        placed = []
        for n in names:
            if n in BIG_AXIS:
                placed.append(_cast_place(shards[n], chip1, BIG_AXIS[n], tok, name="place_" + n))
            else:
                s = W[n][0] if tok is None else W[n][0] + tok[0, 0]
                full = lax.empty((s.shape[0], s.shape[1] * N_CHIP), s.dtype)
                placed.append(lax.dynamic_update_slice(full, s, (0, chip * s.shape[1])))
        ag = _WeightGather(placed, [W[n][0].shape for n in names], [BIG_AXIS.get(n, 1) for n in names],
                           [n in BIG_AXIS for n in names], "g%d" % gi)
        tok = ag.start()
        gathers.append(ag)

    def finish_gather(gi, after):
        G.update(zip(gather_groups[gi], gathers[gi].finish(after)))

    def finish_forward(gi, after):
        G.update(zip(gather_groups[gi], gathers[gi].finish_forward(after)))

    R = W['ret_g'].shape[1]
    Wl = W['rg_lambda'].shape[1]
    IN = W['w_in'].shape[2] * N_CHIP
    F2 = W['ffn_w_up'].shape[2] * N_CHIP
    F = F2 // 2

    norm1_g, norm2_g, norm3_g = W['norm1_g'] + tok[0, 0], W['norm2_g'], W['norm3_g']
    norm_mem_g, final_g, ret_g = W['norm_mem_g'], W['final_g'].reshape(1, D), W['ret_g']
    rg_cb = W['rg_conv_b']
    wa, wx = W['rg_wa'][0], W['rg_wx'][0]
    ba, bx = W['rg_ba'].reshape(1, Wl), W['rg_bx'].reshape(1, Wl)
    lam = W['rg_lambda']
    ffn_cb = W['ffn_conv_b']

    def fwd_mm(a, wname, N, K, **kw):
        return _mm(a, G[wname], mode="nn", M=a.shape[0], N=N, K=K, tm=_div_tile(a.shape[0], 1024),
                   tn=1024 if K <= 3072 else 512, tk=K, **kw)

    def fwd_mm_norm(a, wname, res, g, name):
        return _mm(a, G[wname], mode="nn", M=a.shape[0], N=D, K=a.shape[1], tm=512, tn=D, tk=_div_tile(a.shape[1], 2048),
                   out_dtype=F32, res=res, norm_g=g, name=name)

    def bwd_x_mm(d, wname, N, K, **kw):
        return _mm(d, G[wname], mode="nt", M=d.shape[0], N=N, K=K, tm=_div_tile(d.shape[0], 1024),
                   tn=_div_tile(N, 1024 if K <= 3072 else 512, 256), tk=K, **kw)

    def bwd_w_mm(a, d, M, N, **kw):
        Ks = a.shape[0]
        return _mm(a, d, mode="tn", M=M, N=N, K=Ks, out_dtype=BF16, tm=_div_tile(M, 1024, 256),
                   tn=_div_tile(N, 1024, 256), tk=_div_tile(Ks, 4096 if d.dtype == BF16 else 1024), **kw)

    xn1 = _rmsnorm_fwd(xs, norm1_g, name="norm1_fwd")
    half = (R // RET_HEADS) // 2
    inv = (ROPE_BASE ** (-jnp.arange(half, dtype=F32) / half)).reshape(1, half)
    cos, sin = _rope_table(pos_col, inv + tok[0, 0], name="rope_table")
    finish_gather(0, cos)
    rg_cw = G['rg_conv_w']
    h = fwd_mm(xn1, 'w_in', IN, D, out_dtype=F32, name="mm_in")
    hl, mix = _lru_fwd(h, rg_cw, rg_cb, wa, ba, wx, bx, lam, name="lru_fwd")
    t1 = gathers[1].forward(hl)
    ret_raw, states, mix = _ret_fwd(h, cos, sin, ret_g + t1[0, 0], mix, name="ret_fwd")
    finish_forward(1, mix)
    x1, xn2 = fwd_mm_norm(mix, 'w_out', xs, norm2_g, "mm_out")
    memn = _rmsnorm_fwd(mems, norm_mem_g, name="norm_mem_fwd")
    km = fwd_mm(memn, 'xa_wk', D, D, out_dtype=BF16, name="mm_k")
    vm = fwd_mm(memn, 'xa_wv', D, D, out_dtype=BF16, name="mm_v")
    t2 = gathers[2].forward(x1)
    q = fwd_mm(xn2, 'xa_wq', D, D, out_dtype=BF16, after=t2, name="mm_q")
    o = _xattn_fwd(q, km, vm, name="xattn_fwd")
    x2, xn3 = fwd_mm_norm(o, 'xa_wo', x1, norm3_g, "mm_o")
    finish_forward(2, xn3)
    t3 = gathers[3].forward(xn3)
    ffn_cw = G['ffn_conv_w']
    act, hh_a, hh_b, hc_a, hc_b = _ffn_up_gate(xn3, G['ffn_w_up'], ffn_cw, ffn_cb + t3[0, 0], name="ffn_up_gate")
    finish_forward(3, act)
    x3 = fwd_mm(act, 'ffn_w_down', D, F, out_dtype=F32, res=x2, name="mm_down")
    dx3, d_final, loss8, dx3h = _final_loss(x3, tgt, final_g, name="final_loss")

    gw = {}
    grad_groups = []

    def start_grads(names, tag):
        gg = _GradGather([gw[n] for n in names], [BIG_AXIS[n] for n in names], tag)
        grad_groups.append((names, gg))
        return gg.start()

    dhh_a, dhh_b, gcw_a, gcw_b, gw['ffn_w_down'], gw['ffn_w_up'] = _ffn_bwd(
        dx3h, G['ffn_w_down'], hh_a, hh_b, hc_a, hc_b, act, xn3, ffn_cw, name="ffn_bwd")
    tok_a = start_grads(('ffn_w_down', 'ffn_w_up'), "a")
    dxn3 = bwd_x_mm(dhh_a, 'ffn_w_up', D, F, out_dtype=F32, after=tok_a, name="mm_dxn3_a")
    dxn3 = bwd_x_mm(dhh_b, 'ffn_w_up', D, F, out_dtype=BF16, b_off=(0, F), res=dxn3, name="mm_dxn3_b")
    dx2, d_norm3, dx2h = _rmsnorm_bwd(x2, dxn3, norm3_g, dx3, name="norm3_bwd", emit_bf16=True)
    Kc = ffn_cw.shape[0]
    d_ffn_cw = jnp.concatenate([gcw_a[:Kc], gcw_b[:Kc]], axis=1)
    d_ffn_cb = jnp.concatenate([gcw_a[Kc:Kc + 1], gcw_b[Kc:Kc + 1]], axis=1)

    d_o = bwd_x_mm(dx2h, 'xa_wo', D, D, out_dtype=BF16, name="mm_do")
    gw['xa_wo'] = bwd_w_mm(o, dx2h, D, D, name="mm_dw_o")
    dq, dk, dv = _xattn_bwd(q, km, vm, d_o, name="xattn_bwd")
    gw['xa_wq'] = bwd_w_mm(xn2, dq, D, D, name="mm_dw_q")
    dxn2 = bwd_x_mm(dq, 'xa_wq', D, D, out_dtype=BF16, name="mm_dxn2")
    gw['xa_wk'] = bwd_w_mm(memn, dk, D, D, name="mm_dw_k")
    gw['xa_wv'] = bwd_w_mm(memn, dv, D, D, name="mm_dw_v")
    dmemn = bwd_x_mm(dk, 'xa_wk', D, D, out_dtype=F32, name="mm_dmem_k")
    dmemn = bwd_x_mm(dv, 'xa_wv', D, D, out_dtype=F32, res=dmemn, name="mm_dmem_v")
    _, d_norm_mem = _rmsnorm_bwd(mems, dmemn, norm_mem_g, None, name="norm_mem_bwd")
    dx1, d_norm2, dx1h = _rmsnorm_bwd(x1, dxn2, norm2_g, dx2, name="norm2_bwd", emit_bf16=True)

    gw['w_out'] = bwd_w_mm(mix, dx1h, D, D, name="mm_dw_out")
    tok_b = start_grads(('xa_wo', 'xa_wq', 'xa_wk', 'xa_wv', 'w_out'), "b")
    dmix = bwd_x_mm(dx1h, 'w_out', D, D, out_dtype=BF16, after=tok_b, name="mm_dmix")
    dh, d_ret_g = _ret_bwd(h, cos, sin, ret_g, states, ret_raw, dmix, name="ret_bwd")
    dh, d_rcw, d_rcb, d_wa, d_ba, d_wx, d_bx, d_lam = _lru_bwd(
        h, hl, dmix, dh, rg_cw, rg_cb, wa, ba, wx, bx, lam, name="lru_bwd")
    gw['w_in'] = bwd_w_mm(xn1, dh, D, IN, name="mm_dw_in")
    tok_c = start_grads(('w_in',), "c")
    dxn1 = bwd_x_mm(dh, 'w_in', D, IN, out_dtype=BF16, after=tok_c, name="mm_dxn1")
    grad_x, d_norm1 = _rmsnorm_bwd(xs, dxn1, norm1_g, dx1, name="norm1_bwd")

    small_parts = {
        'norm1_g': d_norm1, 'ret_g': d_ret_g, 'rg_conv_w': d_rcw[:rg_cw.shape[0]], 'rg_conv_b': d_rcb,
        'rg_wa': d_wa, 'rg_ba': d_ba, 'rg_wx': d_wx, 'rg_bx': d_bx, 'rg_lambda': d_lam, 'norm2_g': d_norm2,
        'norm_mem_g': d_norm_mem, 'norm3_g': d_norm3, 'ffn_conv_w': d_ffn_cw, 'ffn_conv_b': d_ffn_cb,
        'final_g': d_final}
    small = [n for n in WEIGHT_NAMES if n not in BIG_AXIS]
    red_shapes = [(1,)] + [tuple(small_parts[n].shape) for n in small]
    fwd_tok = sum(gg.forward(d_norm1)[0:1, 0:1] for _, gg in grad_groups)
    reduced = _allreduce_small(_pack([loss8[0:1, 0:1] + fwd_tok] + [small_parts[n] for n in small], SUB),
                               name="allreduce_small")
    red = _unpack(reduced, red_shapes)
    loss = red[0][0]
    g_small = dict(zip(small, red[1:]))
    for n in SMALL_SHARDED:
        w_local = W[n].shape[-1]
        g_small[n] = lax.dynamic_slice_in_dim(g_small[n], chip * w_local, w_local, axis=1)

    out_g, out_d, out_m, out_v = {}, {}, {}, {}
    rows = 512
    pk = lambda d: _pack([d[n] for n in small], rows)
    g_pack = _pack([g_small[n] for n in small], rows)
    res_small = _adamw(pk(W), pk(Mo), pk(Vo), g_pack[None], name="adamw_small")
    shapes_small = [tuple(W[n].shape) for n in small]
    for dst, packed in zip((out_g, out_d, out_m, out_v), res_small):
        for n, val in zip(small, _unpack(packed, shapes_small)):
            dst[n] = val
    last = res_small[0]
    for names, gg in grad_groups:
        for n, land in zip(names, gg.finish(last)):
            g, d, m_new, v_new = _adamw(shards[n], Mo[n][0], Vo[n][0], land, name="adamw_" + n)
            out_g[n], out_d[n], out_m[n], out_v[n] = (t.reshape(W[n].shape) for t in (g, d, m_new, v_new))
            last = g
    return (loss, grad_x[None], *[out_g[n] for n in WEIGHT_NAMES], *[out_d[n] for n in WEIGHT_NAMES],
            *[out_m[n] for n in WEIGHT_NAMES], *[out_v[n] for n in WEIGHT_NAMES])


def kernel(x, mem, positions, norm1_g, w_in, ret_g, rg_conv_w, rg_conv_b, rg_wa, rg_ba, rg_wx, rg_bx, rg_lambda, w_out, norm2_g, norm_mem_g, xa_wq, xa_wk, xa_wv, xa_wo, norm3_g, ffn_w_up, ffn_conv_w, ffn_conv_b, ffn_w_down, final_g, loss_target, m_norm1_g, m_w_in, m_ret_g, m_rg_conv_w, m_rg_conv_b, m_rg_wa, m_rg_ba, m_rg_wx, m_rg_bx, m_rg_lambda, m_w_out, m_norm2_g, m_norm_mem_g, m_xa_wq, m_xa_wk, m_xa_wv, m_xa_wo, m_norm3_g, m_ffn_w_up, m_ffn_conv_w, m_ffn_conv_b, m_ffn_w_down, m_final_g, v_norm1_g, v_w_in, v_ret_g, v_rg_conv_w, v_rg_conv_b, v_rg_wa, v_rg_ba, v_rg_wx, v_rg_bx, v_rg_lambda, v_w_out, v_norm2_g, v_norm_mem_g, v_xa_wq, v_xa_wk, v_xa_wv, v_xa_wo, v_norm3_g, v_ffn_w_up, v_ffn_conv_w, v_ffn_conv_b, v_ffn_w_down, v_final_g):
    W = dict(zip(WEIGHT_NAMES, (norm1_g, w_in, ret_g, rg_conv_w, rg_conv_b, rg_wa, rg_ba, rg_wx, rg_bx, rg_lambda, w_out,
                                norm2_g, norm_mem_g, xa_wq, xa_wk, xa_wv, xa_wo, norm3_g, ffn_w_up, ffn_conv_w,
                                ffn_conv_b, ffn_w_down, final_g)))
    Mo = dict(zip(WEIGHT_NAMES, (m_norm1_g, m_w_in, m_ret_g, m_rg_conv_w, m_rg_conv_b, m_rg_wa, m_rg_ba, m_rg_wx, m_rg_bx,
                                 m_rg_lambda, m_w_out, m_norm2_g, m_norm_mem_g, m_xa_wq, m_xa_wk, m_xa_wv, m_xa_wo,
                                 m_norm3_g, m_ffn_w_up, m_ffn_conv_w, m_ffn_conv_b, m_ffn_w_down, m_final_g)))
    Vo = dict(zip(WEIGHT_NAMES, (v_norm1_g, v_w_in, v_ret_g, v_rg_conv_w, v_rg_conv_b, v_rg_wa, v_rg_ba, v_rg_wx, v_rg_bx,
                                 v_rg_lambda, v_w_out, v_norm2_g, v_norm_mem_g, v_xa_wq, v_xa_wk, v_xa_wv, v_xa_wo,
                                 v_norm3_g, v_ffn_w_up, v_ffn_conv_w, v_ffn_conv_b, v_ffn_w_down, v_final_g)))
    return _step(x, mem, positions, loss_target, W, Mo, Vo)
```

```python
import math

import jax
import jax.numpy as jnp
from jax import lax
from jax.experimental import pallas as pl
from jax.experimental.pallas import tpu as pltpu

F32 = jnp.float32
BF16 = jnp.bfloat16

EPS = 1e-6
RET_HEADS = 4
RET_CHUNK = 128
ROPE_BASE = 10000.0
LRU_BLOCKS = 8
LRU_C = 8.0
XA_HEADS = 4

ADAM_LR = 0.001
ADAM_B1 = 0.9
ADAM_B2 = 0.999
ADAM_EPS = 1e-08
ADAM_WD = 0.01
ADAM_STEP = 10

N_DEV = 8
N_CHIP = 4
MESH = pl.DeviceIdType.MESH
SUB = 8
LANE = 128
VMEM_LIMIT = 56 * 1024 * 1024

NN = ((1,), (0,))
NT = ((1,), (1,))
TN = ((0,), (0,))


def _cparams(sem):
    return pltpu.CompilerParams(dimension_semantics=sem, vmem_limit_bytes=VMEM_LIMIT)


def _sigmoid(v):
    return 1.0 / (1.0 + jnp.exp(-v))


def _bdot(a, b, dims):
    return lax.dot_general(a.astype(BF16), b.astype(BF16), (dims, ((), ())), preferred_element_type=F32)


def _row_iota(shape):
    return lax.broadcasted_iota(jnp.int32, shape, 0)


def _shift_down(v, tail, k):
    if k == 0:
        return v
    r = pltpu.roll(v, k, 0)
    rt = pltpu.roll(tail, k, 0)
    first = jnp.where(_row_iota(rt.shape) < k, rt, r[0:SUB])
    return jnp.concatenate([first, r[SUB:]], axis=0)


def _shift_up(v, head, k):
    if k == 0:
        return v
    n = v.shape[0]
    r = pltpu.roll(v, n - k, 0)
    rh = pltpu.roll(head, SUB - k, 0)
    last = jnp.where(_row_iota(rh.shape) >= SUB - k, rh, r[n - SUB:n])
    return jnp.concatenate([r[:n - SUB], last], axis=0)


def _mm(a, b, *, mode, M, N, K, out_dtype, name, tm=512, tn=512, tk=512, b_off=(0, 0), res=None, norm_g=None,
        after=None):
    tm, tn, tk = min(tm, M), min(tn, N), min(tk, K)
    assert M % tm == 0 and N % tn == 0 and K % tk == 0, (name, M, N, K, tm, tn, tk)
    nk = K // tk
    if mode == "nn":
        a_blk, b_blk, dims = (tm, tk), (tk, tn), NN
        a_map = lambda i, j, k: (i, k)
        b_map = lambda i, j, k: (k + b_off[0] // tk, j + b_off[1] // tn)
    elif mode == "nt":
        a_blk, b_blk, dims = (tm, tk), (tn, tk), NT
        a_map = lambda i, j, k: (i, k)
        b_map = lambda i, j, k: (j + b_off[0] // tn, k + b_off[1] // tk)
    else:
        a_blk, b_blk, dims = (tk, tm), (tk, tn), TN
        a_map = lambda i, j, k: (k, i)
        b_map = lambda i, j, k: (k + b_off[0] // tk, j + b_off[1] // tn)
    assert b_off[0] % b_blk[0] == 0 and b_off[1] % b_blk[1] == 0, (name, b_off, b_blk)
    has_res, has_norm, has_after = res is not None, norm_g is not None, after is not None
    assert not has_norm or tn == N

    def body(*refs):
        refs = list(refs)
        a_ref, b_ref = refs[0], refs[1]
        pos = 2
        r_ref = g_ref = n_ref = None
        if has_res:
            r_ref = refs[pos]
            pos += 1
        if has_norm:
            g_ref = refs[pos]
            pos += 1
        pos += has_after
        o_ref = refs[pos]
        pos += 1
        if has_norm:
            n_ref = refs[pos]
            pos += 1
        acc = refs[pos] if nk > 1 else None
        k = pl.program_id(2)
        part = _bdot(a_ref[...], b_ref[...], dims)

        def finish(total):
            if has_res:
                total = total + r_ref[...].astype(F32)
            o_ref[...] = total.astype(o_ref.dtype)
            if has_norm:
                r = lax.rsqrt(jnp.mean(total * total, axis=-1, keepdims=True) + EPS)
                n_ref[...] = (total * r * g_ref[...]).astype(n_ref.dtype)

        if nk == 1:
            finish(part)
        else:
            @pl.when(k == 0)
            def _():
                acc[...] = part

            @pl.when(k > 0)
            def _():
                acc[...] += part

            @pl.when(k == nk - 1)
            def _():
                finish(acc[...])

    in_specs = [pl.BlockSpec(a_blk, a_map), pl.BlockSpec(b_blk, b_map)]
    args = [a, b]
    if has_res:
        in_specs.append(pl.BlockSpec((tm, tn), lambda i, j, k: (i, j)))
        args.append(res)
    if has_norm:
        in_specs.append(pl.BlockSpec((1, N), lambda i, j, k: (0, 0)))
        args.append(norm_g)
    if has_after:
        in_specs.append(pl.BlockSpec(memory_space=pl.ANY))
        args.append(after)
    out_shape = jax.ShapeDtypeStruct((M, N), out_dtype)
    out_specs = pl.BlockSpec((tm, tn), lambda i, j, k: (i, j))
    if has_norm:
        out_shape = [out_shape, jax.ShapeDtypeStruct((M, N), BF16)]
        out_specs = [out_specs, pl.BlockSpec((tm, tn), lambda i, j, k: (i, j))]
    return pl.pallas_call(
        body, name=name, grid=(M // tm, N // tn, nk), in_specs=in_specs,
        out_specs=out_specs, out_shape=out_shape,
        scratch_shapes=[pltpu.VMEM((tm, tn), F32)] if nk > 1 else [],
        compiler_params=_cparams(("parallel", "parallel", "arbitrary")),
    )(*args)


def _rmsnorm_fwd(x, g, *, name, ts=512):
    S, D = x.shape
    ts = min(ts, S)

    def body(x_ref, g_ref, o_ref):
        xv = x_ref[...]
        r = lax.rsqrt(jnp.mean(xv * xv, axis=-1, keepdims=True) + EPS)
        o_ref[...] = (xv * r * g_ref[...]).astype(o_ref.dtype)

    return pl.pallas_call(
        body, name=name, grid=(S // ts,),
        in_specs=[pl.BlockSpec((ts, D), lambda i: (i, 0)), pl.BlockSpec((1, D), lambda i: (0, 0))],
        out_specs=pl.BlockSpec((ts, D), lambda i: (i, 0)),
        out_shape=jax.ShapeDtypeStruct((S, D), BF16),
        compiler_params=_cparams(("parallel",)),
    )(x, g)


def _rmsnorm_bwd(x, dxn, g, res, *, name, ts=512, emit_bf16=False):
    S, D = x.shape
    ts = min(ts, S)
    has_res = res is not None

    def body(*refs):
        refs = list(refs)
        dx16_ref = refs.pop() if emit_bf16 else None
        if has_res:
            x_ref, d_ref, g_ref, r_ref, dx_ref, dg_ref = refs
        else:
            x_ref, d_ref, g_ref, dx_ref, dg_ref = refs
        i = pl.program_id(0)
        xv = x_ref[...]
        dv = d_ref[...].astype(F32)
        r = lax.rsqrt(jnp.mean(xv * xv, axis=-1, keepdims=True) + EPS)
        gd = dv * g_ref[...]
        proj = jnp.mean(xv * gd, axis=-1, keepdims=True)
        dx = r * gd - xv * (r * r * r) * proj
        if has_res:
            dx = dx + r_ref[...]
        dx_ref[...] = dx
        if emit_bf16:
            dx16_ref[...] = dx.astype(BF16)
        part = jnp.sum(dv * xv * r, axis=0, keepdims=True)

        @pl.when(i == 0)
        def _():
            dg_ref[...] = part

        @pl.when(i > 0)
        def _():
            dg_ref[...] += part

    row = pl.BlockSpec((ts, D), lambda i: (i, 0))
    vec = pl.BlockSpec((1, D), lambda i: (0, 0))
    in_specs = [row, row, vec] + ([row] if has_res else [])
    args = [x, dxn, g] + ([res] if has_res else [])
    extra = emit_bf16 * [jax.ShapeDtypeStruct((S, D), BF16)]
    return pl.pallas_call(
        body, name=name, grid=(S // ts,), in_specs=in_specs, out_specs=[row, vec] + emit_bf16 * [row],
        out_shape=[jax.ShapeDtypeStruct((S, D), F32), jax.ShapeDtypeStruct((1, D), F32)] + extra,
        compiler_params=_cparams(("arbitrary",)),
    )(*args)


def _final_loss(x, target, g, *, name, ts=512):
    S, D = x.shape
    ts = min(ts, S)

    def body(x_ref, t_ref, g_ref, dx_ref, dg_ref, loss_ref, dx16_ref):
        i = pl.program_id(0)
        xv = x_ref[...]
        gv = g_ref[...]
        r = lax.rsqrt(jnp.mean(xv * xv, axis=-1, keepdims=True) + EPS)
        y = xv * r * gv
        err = y - t_ref[...]
        row_loss = jnp.mean(err * err, axis=-1, keepdims=True)
        lpart = 0.5 * jnp.sum(row_loss, axis=0, keepdims=True)
        dy = err * (1.0 / D)
        gd = dy * gv
        proj = jnp.mean(xv * gd, axis=-1, keepdims=True)
        dx = r * gd - xv * (r * r * r) * proj
        dx_ref[...] = dx
        dx16_ref[...] = dx.astype(BF16)
        part = jnp.sum(dy * xv * r, axis=0, keepdims=True)
        lfull = jnp.broadcast_to(lpart, loss_ref.shape)

        @pl.when(i == 0)
        def _():
            dg_ref[...] = part
            loss_ref[...] = lfull

        @pl.when(i > 0)
        def _():
            dg_ref[...] += part
            loss_ref[...] += lfull

    row = pl.BlockSpec((ts, D), lambda i: (i, 0))
    vec = pl.BlockSpec((1, D), lambda i: (0, 0))
    return pl.pallas_call(
        body, name=name, grid=(S // ts,), in_specs=[row, row, vec],
        out_specs=[row, vec, pl.BlockSpec((SUB, LANE), lambda i: (0, 0)), row],
        out_shape=[jax.ShapeDtypeStruct((S, D), F32), jax.ShapeDtypeStruct((1, D), F32),
                   jax.ShapeDtypeStruct((SUB, LANE), F32), jax.ShapeDtypeStruct((S, D), BF16)],
        compiler_params=_cparams(("arbitrary",)),
    )(x, target, g)


def _rope_table(pos_col, inv, *, name, ts=1024):
    S = pos_col.shape[0]
    ts = min(ts, S)
    half = inv.shape[1]

    def body(p_ref, inv_ref, c_ref, s_ref):
        ang = p_ref[...].astype(F32) * inv_ref[...]
        c_ref[...] = jnp.cos(ang)
        s_ref[...] = jnp.sin(ang)

    tab = pl.BlockSpec((ts, half), lambda i: (i, 0))
    return pl.pallas_call(
        body, name=name, grid=(S // ts,),
        in_specs=[pl.BlockSpec((ts, 1), lambda i: (i, 0)), pl.BlockSpec((1, half), lambda i: (0, 0))],
        out_specs=[tab, tab],
        out_shape=[jax.ShapeDtypeStruct((S, half), F32), jax.ShapeDtypeStruct((S, half), F32)],
        compiler_params=_cparams(("parallel",)),
    )(pos_col, inv)


def _ret_consts(C, log_g):
    ii = lax.broadcasted_iota(jnp.int32, (C, C), 0)
    jj = lax.broadcasted_iota(jnp.int32, (C, C), 1)
    diff = (ii - jj).astype(F32)
    intra = jnp.where(ii >= jj, jnp.exp(log_g * jnp.maximum(diff, 0.0)), 0.0)
    idx = lax.broadcasted_iota(jnp.int32, (C, 1), 0).astype(F32)
    qd = jnp.exp(log_g * (idx + 1.0))
    kd = jnp.exp(log_g * (C - 1.0 - idx))
    cd = math.exp(log_g * C)
    return intra, qd, kd, cd


def _rot(t, cs, sn):
    half = t.shape[-1] // 2
    t1, t2 = t[:, :half], t[:, half:]
    return jnp.concatenate([t1 * cs - t2 * sn, t1 * sn + t2 * cs], axis=-1)


def _unrot(d, cs, sn):
    half = d.shape[-1] // 2
    d1, d2 = d[:, :half], d[:, half:]
    return jnp.concatenate([d1 * cs + d2 * sn, d2 * cs - d1 * sn], axis=-1)


def _ret_fwd(h, cos, sin, ret_g, mix, *, name, ch=2):
    S = h.shape[0]
    R = ret_g.shape[1]
    H, C = RET_HEADS, RET_CHUNK
    Dh = R // H
    ts = ch * C
    assert S % ts == 0
    log_gs = [math.log(1.0 - 2.0 ** (-5.0 - hd)) for hd in range(H)]
    scale = Dh ** -0.5

    def body(x_ref, c_ref, s_ref, rg_ref, mix_in, ret_ref, st_ref, mix_ref, state):
        i = pl.program_id(0)

        @pl.when(i == 0)
        def _():
            state[...] = jnp.zeros_like(state)

        for c in range(ch):
            rows = pl.ds(c * C, C)
            cs, sn = c_ref[rows, :], s_ref[rows, :]
            for hd in range(H):
                intra, qd, kd, cd = _ret_consts(C, log_gs[hd])
                q = x_ref[rows, pl.ds(hd * Dh, Dh)]
                k = x_ref[rows, pl.ds(R + hd * Dh, Dh)]
                v = x_ref[rows, pl.ds(2 * R + hd * Dh, Dh)]
                g = x_ref[rows, pl.ds(3 * R + hd * Dh, Dh)]
                rq = _rot(q, cs, sn)
                rk = _rot(k, cs, sn) * scale
                st = state[hd]
                st_ref[c, hd] = st.astype(BF16)
                s_ = _bdot(rq, rk, NT) * intra
                ret = _bdot(s_, v, NN) + _bdot(rq * qd, st, NN)
                state[hd] = st * cd + _bdot(rk * kd, v, TN)
                ret_ref[rows, pl.ds(hd * Dh, Dh)] = ret
                rr = lax.rsqrt(jnp.mean(ret * ret, axis=-1, keepdims=True) + EPS)
                out = ret * rr * rg_ref[:, pl.ds(hd * Dh, Dh)] * (g * _sigmoid(g))
                mix_ref[rows, pl.ds(hd * Dh, Dh)] = out.astype(BF16)

    n_chunks = S // C
    return pl.pallas_call(
        body, name=name, grid=(S // ts,),
        in_specs=[pl.BlockSpec((ts, 4 * R), lambda i: (i, 0)),
                  pl.BlockSpec((ts, Dh // 2), lambda i: (i, 0)), pl.BlockSpec((ts, Dh // 2), lambda i: (i, 0)),
                  pl.BlockSpec((1, R), lambda i: (0, 0)), pl.BlockSpec(memory_space=pl.ANY)],
        out_specs=[pl.BlockSpec((ts, R), lambda i: (i, 0)),
                   pl.BlockSpec((ch, H, Dh, Dh), lambda i: (i, 0, 0, 0)),
                   pl.BlockSpec((ts, R), lambda i: (i, 0))],
        out_shape=[jax.ShapeDtypeStruct((S, R), F32), jax.ShapeDtypeStruct((n_chunks, H, Dh, Dh), BF16),
                   jax.ShapeDtypeStruct(mix.shape, mix.dtype)],
        scratch_shapes=[pltpu.VMEM((H, Dh, Dh), F32)],
        input_output_aliases={4: 2},
        compiler_params=_cparams(("arbitrary",)),
    )(h, cos, sin, ret_g, mix)


def _ret_bwd(h, cos, sin, ret_g, states, ret_raw, dmix, *, name, ch=2):
    S = h.shape[0]
    R = ret_g.shape[1]
    H, C = RET_HEADS, RET_CHUNK
    Dh = R // H
    ts = ch * C
    nb = S // ts
    log_gs = [math.log(1.0 - 2.0 ** (-5.0 - hd)) for hd in range(H)]
    scale = Dh ** -0.5

    def body(x_ref, c_ref, s_ref, rg_ref, st_ref, ret_ref, dm_ref, dh_ref, drg_ref, dstate):
        i = pl.program_id(0)

        @pl.when(i == 0)
        def _():
            dstate[...] = jnp.zeros_like(dstate)
            drg_ref[...] = jnp.zeros_like(drg_ref)

        for c in reversed(range(ch)):
            rows = pl.ds(c * C, C)
            cs, sn = c_ref[rows, :], s_ref[rows, :]
            for hd in range(H):
                intra, qd, kd, cd = _ret_consts(C, log_gs[hd])
                cols = pl.ds(hd * Dh, Dh)
                q = x_ref[rows, pl.ds(hd * Dh, Dh)]
                k = x_ref[rows, pl.ds(R + hd * Dh, Dh)]
                v = x_ref[rows, pl.ds(2 * R + hd * Dh, Dh)]
                g = x_ref[rows, pl.ds(3 * R + hd * Dh, Dh)]
                rq = _rot(q, cs, sn)
                rk = _rot(k, cs, sn) * scale
                ret = ret_ref[rows, cols]
                dm = dm_ref[rows, cols].astype(F32)
                rgv = rg_ref[:, cols]
                rr = lax.rsqrt(jnp.mean(ret * ret, axis=-1, keepdims=True) + EPS)
                retn = ret * rr
                sg = _sigmoid(g)
                silu = g * sg
                drg_ref[:, cols] += jnp.sum(dm * retn * silu, axis=0, keepdims=True)
                dg = dm * retn * rgv * (sg * (1.0 + g * (1.0 - sg)))
                dretn = dm * rgv * silu
                d_o = rr * dretn - ret * (rr * rr * rr) * jnp.mean(ret * dretn, axis=-1, keepdims=True)
                st = st_ref[c, hd]
                d_s = dstate[hd]
                a_ = _bdot(rq, rk, NT) * intra
                d_a = _bdot(d_o, v, NT) * intra
                d_qr = _bdot(d_a, rk, NN) + _bdot(d_o, st, NT) * qd
                d_kr = _bdot(d_a, rq, TN) + _bdot(v, d_s, NT) * kd
                d_v = _bdot(a_, d_o, TN) + _bdot(rk * kd, d_s, NN)
                dstate[hd] = d_s * cd + _bdot(rq * qd, d_o, TN)
                dh_ref[rows, pl.ds(hd * Dh, Dh)] = _unrot(d_qr, cs, sn).astype(BF16)
                dh_ref[rows, pl.ds(R + hd * Dh, Dh)] = (_unrot(d_kr, cs, sn) * scale).astype(BF16)
                dh_ref[rows, pl.ds(2 * R + hd * Dh, Dh)] = d_v.astype(BF16)
                dh_ref[rows, pl.ds(3 * R + hd * Dh, Dh)] = dg.astype(BF16)

    rb = lambda i: nb - 1 - i
    return pl.pallas_call(
        body, name=name, grid=(nb,),
        in_specs=[pl.BlockSpec((ts, 4 * R), lambda i: (rb(i), 0)),
                  pl.BlockSpec((ts, Dh // 2), lambda i: (rb(i), 0)), pl.BlockSpec((ts, Dh // 2), lambda i: (rb(i), 0)),
                  pl.BlockSpec((1, R), lambda i: (0, 0)),
                  pl.BlockSpec((ch, H, Dh, Dh), lambda i: (rb(i), 0, 0, 0)),
                  pl.BlockSpec((ts, R), lambda i: (rb(i), 0)),
                  pl.BlockSpec((ts, R), lambda i: (rb(i), 0))],
        out_specs=[pl.BlockSpec((ts, 4 * R), lambda i: (rb(i), 0)), pl.BlockSpec((1, R), lambda i: (0, 0))],
        out_shape=[jax.ShapeDtypeStruct((S, 6 * R), BF16), jax.ShapeDtypeStruct((1, R), F32)],
        scratch_shapes=[pltpu.VMEM((H, Dh, Dh), F32)],
        compiler_params=_cparams(("arbitrary",)),
    )(h, cos, sin, ret_g, states, ret_raw, dmix)


GELU_C = math.sqrt(2.0 / math.pi)
GELU_A = 0.044715


def _gelu_parts(y):
    t = jnp.tanh(GELU_C * (y + GELU_A * y * y * y))
    val = 0.5 * y * (1.0 + t)
    grad = 0.5 * (1.0 + t) + 0.5 * y * (1.0 - t * t) * GELU_C * (1.0 + 3.0 * GELU_A * y * y)
    return val, grad


def _neg_expm1(x):
    series = -x * (1.0 + x * (1.0 / 2.0) * (1.0 + x * (1.0 / 3.0) * (1.0 + x * (1.0 / 4.0) * (
        1.0 + x * (1.0 / 5.0) * (1.0 + x * (1.0 / 6.0) * (1.0 + x * (1.0 / 7.0)))))))
    return jnp.where(x > -0.35, series, 1.0 - jnp.exp(x))


def _log_sigmoid(x):
    return jnp.minimum(x, 0.0) - jnp.log1p(jnp.exp(-jnp.abs(x)))


def _lru_gates(uc, wa_ref, ba_ref, wx_ref, bx_ref):
    nbk = wa_ref.shape[0]
    bd = wa_ref.shape[1]
    rs, gs = [], []
    for n in range(nbk):
        ucn = uc[:, n * bd:(n + 1) * bd]
        rs.append(_sigmoid(_bdot(ucn, wa_ref[n], NN) + ba_ref[:, pl.ds(n * bd, bd)]))
        gs.append(_sigmoid(_bdot(ucn, wx_ref[n], NN) + bx_ref[:, pl.ds(n * bd, bd)]))
    return jnp.concatenate(rs, axis=-1), jnp.concatenate(gs, axis=-1)


def _lru_fwd(h, conv_w, conv_b, wa, ba, wx, bx, lam, *, name, ts=256):
    S = h.shape[0]
    W = lam.shape[1]
    K = conv_w.shape[0]
    ts = min(ts, S)

    def body(u_ref, y_ref, cw_ref, cb_ref, wa_ref, ba_ref, wx_ref, bx_ref, lam_ref, hl_ref, mix_ref, tail, hlast):
        i = pl.program_id(0)

        @pl.when(i == 0)
        def _():
            tail[...] = jnp.zeros_like(tail)
            hlast[...] = jnp.zeros_like(hlast)

        u = u_ref[...]
        tl = tail[...]
        uc = cb_ref[...] + cw_ref[K - 1:K, :] * u
        for k in range(K - 1):
            uc = uc + cw_ref[k:k + 1, :] * _shift_down(u, tl, K - 1 - k)
        tail[...] = u[ts - SUB:ts]
        r, ig = _lru_gates(uc, wa_ref, ba_ref, wx_ref, bx_ref)
        log_a = LRU_C * r * _log_sigmoid(lam_ref[...])
        a = jnp.exp(log_a)
        b = jnp.sqrt(_neg_expm1(2.0 * log_a)) * (ig * uc)
        in_tile = _row_iota((ts, W)) & (SUB - 1)
        d = 1
        while d < SUB:
            a_s = jnp.where(in_tile < d, 1.0, pltpu.roll(a, d, 0))
            b_s = jnp.where(in_tile < d, 0.0, pltpu.roll(b, d, 0))
            b = a * b_s + b
            a = a * a_s
            d *= 2
        before = hlast[SUB - 1:SUB, :]
        for k in range(ts // SUB):
            tile = slice(k * SUB, (k + 1) * SUB)
            h_tile = a[tile] * before + b[tile]
            hl_ref[tile, :] = h_tile
            before = h_tile[SUB - 1:SUB, :]
        hlast[...] = hl_ref[ts - SUB:ts, :]
        gy, _ = _gelu_parts(y_ref[...])
        mix_ref[...] = (hl_ref[...] * gy).astype(BF16)

    full = lambda shape: pl.BlockSpec(shape, lambda i: tuple(0 for _ in shape))
    return pl.pallas_call(
        body, name=name, grid=(S // ts,),
        in_specs=[pl.BlockSpec((ts, W), lambda i: (i, 4)), pl.BlockSpec((ts, W), lambda i: (i, 5)),
                  full(conv_w.shape), full(conv_b.shape), full(wa.shape), full(ba.shape), full(wx.shape),
                  full(bx.shape), full(lam.shape)],
        out_specs=[pl.BlockSpec((ts, W), lambda i: (i, 0)), pl.BlockSpec((ts, W), lambda i: (i, 1))],
        out_shape=[jax.ShapeDtypeStruct((S, W), F32), jax.ShapeDtypeStruct((S, 2 * W), BF16)],
        scratch_shapes=[pltpu.VMEM((SUB, W), F32), pltpu.VMEM((SUB, W), F32)],
        compiler_params=_cparams(("arbitrary",)),
    )(h, h, conv_w, conv_b, wa, ba, wx, bx, lam)


def _lru_bwd(h, hl, dmix, dh, conv_w, conv_b, wa, ba, wx, bx, lam, *, name, ts=256):
    S = h.shape[0]
    W = lam.shape[1]
    K = conv_w.shape[0]
    nbk, bd = wa.shape[0], wa.shape[1]
    ts = min(ts, S)
    nb = S // ts
    t8 = ts // SUB

    def body(u_ref, y_ref, uh_ref, hl_ref, hh_ref, dm_ref, cw_ref, cb_ref, wa_ref, ba_ref, wx_ref, bx_ref, lam_ref,
             dh_in, dh_ref, dcw_ref, dcb_ref, dwa_ref, dba_ref, dwx_ref, dbx_ref, dlam_ref, carry, head, lam_buf):
        i = pl.program_id(0)
        blk = nb - 1 - i

        @pl.when(i == 0)
        def _():
            carry[...] = jnp.zeros_like(carry)
            head[...] = jnp.zeros_like(head)
            for ref in (dcw_ref, dcb_ref, dwa_ref, dba_ref, dwx_ref, dbx_ref, dlam_ref):
                ref[...] = jnp.zeros_like(ref)

        inside = (blk > 0).astype(F32)
        u = u_ref[...]
        tl = uh_ref[...] * inside
        sh = [_shift_down(u, tl, K - 1 - k) for k in range(K)]
        uc = cb_ref[...]
        for k in range(K):
            uc = uc + cw_ref[k:k + 1, :] * sh[k]
        r, ig = _lru_gates(uc, wa_ref, ba_ref, wx_ref, bx_ref)
        lam_v = lam_ref[...]
        ls = _log_sigmoid(lam_v)
        log_a = LRU_C * r * ls
        a = jnp.exp(log_a)
        mult = jnp.sqrt(_neg_expm1(2.0 * log_a))
        hcur = hl_ref[...]
        hprev = _shift_down(hcur, hh_ref[...] * inside, 1)
        gy, dgy = _gelu_parts(y_ref[...])
        dm = dm_ref[...].astype(F32)
        d_y = dm * hcur * dgy
        rid = _row_iota((ts, W))
        bq = dm * gy + jnp.where(rid == ts - 1, carry[0:1, :], 0.0)
        aq = jnp.where(rid == ts - 1, 0.0, pltpu.roll(a, ts - 1, 0))
        in_tile = rid & (SUB - 1)
        d = 1
        while d < SUB:
            a_s = jnp.where(in_tile >= SUB - d, 1.0, pltpu.roll(aq, ts - d, 0))
            b_s = jnp.where(in_tile >= SUB - d, 0.0, pltpu.roll(bq, ts - d, 0))
            bq = bq + aq * b_s
            aq = aq * a_s
            d *= 2
        after_row = jnp.zeros((1, W), F32)
        for k in reversed(range(ts // SUB)):
            tile = slice(k * SUB, (k + 1) * SUB)
            lam_tile = aq[tile] * after_row + bq[tile]
            lam_buf[tile, :] = lam_tile
            after_row = lam_tile[0:1, :]
        lam_t = lam_buf[...]
        carry[...] = (a * lam_t)[0:SUB]
        d_a = lam_t * hprev
        d_mult = lam_t * (ig * uc)
        d_i = lam_t * mult * uc
        d_uc = lam_t * mult * ig
        d_log_a = d_a * a - d_mult * (a * a) / mult
        d_r = d_log_a * (LRU_C * ls)
        dlam_ref[...] += jnp.sum(d_log_a * (LRU_C * r), axis=0, keepdims=True) * _sigmoid(-lam_v)
        d_pr = d_r * r * (1.0 - r)
        d_pi = d_i * ig * (1.0 - ig)
        dba_ref[...] += jnp.sum(d_pr, axis=0, keepdims=True)
        dbx_ref[...] += jnp.sum(d_pi, axis=0, keepdims=True)
        extra = []
        for n in range(nbk):
            sl = slice(n * bd, (n + 1) * bd)
            ucn = uc[:, sl]
            dwa_ref[n] += _bdot(ucn, d_pr[:, sl], TN)
            dwx_ref[n] += _bdot(ucn, d_pi[:, sl], TN)
            extra.append(_bdot(d_pr[:, sl], wa_ref[n], NT) + _bdot(d_pi[:, sl], wx_ref[n], NT))
        d_uc = d_uc + jnp.concatenate(extra, axis=-1)
        dcb_ref[...] += jnp.sum(d_uc, axis=0, keepdims=True)
        rid8 = _row_iota((SUB, W))
        dcw = jnp.zeros((SUB, W), F32)
        for k in range(K):
            dcw = dcw + jnp.where(rid8 == k, jnp.sum(d_uc * sh[k], axis=0, keepdims=True), 0.0)
        dcw_ref[...] += dcw
        hd = head[...]
        d_u = cw_ref[K - 1:K, :] * d_uc
        for j in range(1, K):
            d_u = d_u + cw_ref[K - 1 - j:K - j, :] * _shift_up(d_uc, hd, j)
        head[...] = d_uc[0:SUB]
        dh_ref[:, 0:W] = d_u.astype(BF16)
        dh_ref[:, W:2 * W] = d_y.astype(BF16)

    rb = lambda i: nb - 1 - i
    prev8 = lambda i: jnp.maximum(rb(i) * t8 - 1, 0)
    full = lambda shape: pl.BlockSpec(shape, lambda i: tuple(0 for _ in shape))
    small = [jax.ShapeDtypeStruct((SUB, W), F32), jax.ShapeDtypeStruct((1, W), F32),
             jax.ShapeDtypeStruct(wa.shape, F32), jax.ShapeDtypeStruct((1, W), F32),
             jax.ShapeDtypeStruct(wx.shape, F32), jax.ShapeDtypeStruct((1, W), F32),
             jax.ShapeDtypeStruct((1, W), F32)]
    return pl.pallas_call(
        body, name=name, grid=(nb,),
        in_specs=[pl.BlockSpec((ts, W), lambda i: (rb(i), 4)), pl.BlockSpec((ts, W), lambda i: (rb(i), 5)),
                  pl.BlockSpec((SUB, W), lambda i: (prev8(i), 4)),
                  pl.BlockSpec((ts, W), lambda i: (rb(i), 0)), pl.BlockSpec((SUB, W), lambda i: (prev8(i), 0)),
                  pl.BlockSpec((ts, W), lambda i: (rb(i), 1)),
                  full(conv_w.shape), full(conv_b.shape), full(wa.shape), full(ba.shape), full(wx.shape),
                  full(bx.shape), full(lam.shape), pl.BlockSpec(memory_space=pl.ANY)],
        out_specs=[pl.BlockSpec((ts, 2 * W), lambda i: (rb(i), 2))] + [full(s.shape) for s in small],
        out_shape=[jax.ShapeDtypeStruct(dh.shape, dh.dtype)] + small,
        scratch_shapes=[pltpu.VMEM((SUB, W), F32), pltpu.VMEM((SUB, W), F32), pltpu.VMEM((ts, W), F32)],
        input_output_aliases={13: 0},
        compiler_params=_cparams(("arbitrary",)),
    )(h, h, h, hl, hl, dmix, conv_w, conv_b, wa, ba, wx, bx, lam, dh)


def _xattn_fwd(q, km, vm, *, name, ts=512):
    S, D = q.shape
    M = km.shape[0]
    H = XA_HEADS
    Dh = D // H
    ts = min(ts, S)
    scale = Dh ** -0.5

    def body(q_ref, k_ref, v_ref, o_ref):
        for hd in range(H):
            cols = pl.ds(hd * Dh, Dh)
            s = _bdot(q_ref[:, cols], k_ref[:, cols], NT) * scale
            s = s - jnp.max(s, axis=-1, keepdims=True)
            e = jnp.exp(s)
            p = e / jnp.sum(e, axis=-1, keepdims=True)
            o_ref[:, cols] = _bdot(p, v_ref[:, cols], NN).astype(o_ref.dtype)

    return pl.pallas_call(
        body, name=name, grid=(S // ts,),
        in_specs=[pl.BlockSpec((ts, D), lambda i: (i, 0)), pl.BlockSpec((M, D), lambda i: (0, 0)),
                  pl.BlockSpec((M, D), lambda i: (0, 0))],
        out_specs=pl.BlockSpec((ts, D), lambda i: (i, 0)),
        out_shape=jax.ShapeDtypeStruct((S, D), BF16),
        compiler_params=_cparams(("parallel",)),
    )(q, km, vm)


def _xattn_bwd(q, km, vm, d_o, *, name, ts=512):
    S, D = q.shape
    M = km.shape[0]
    H = XA_HEADS
    Dh = D // H
    ts = min(ts, S)
    scale = Dh ** -0.5

    def body(q_ref, k_ref, v_ref, do_ref, dq_ref, dk_ref, dv_ref):
        i = pl.program_id(0)

        @pl.when(i == 0)
        def _():
            dk_ref[...] = jnp.zeros_like(dk_ref)
            dv_ref[...] = jnp.zeros_like(dv_ref)

        for hd in range(H):
            cols = pl.ds(hd * Dh, Dh)
            qh, kh, vh, doh = q_ref[:, cols], k_ref[:, cols], v_ref[:, cols], do_ref[:, cols]
            s = _bdot(qh, kh, NT) * scale
            s = s - jnp.max(s, axis=-1, keepdims=True)
            e = jnp.exp(s)
            p = e / jnp.sum(e, axis=-1, keepdims=True)
            dp = _bdot(doh, vh, NT)
            ds = p * (dp - jnp.sum(dp * p, axis=-1, keepdims=True)) * scale
            dq_ref[:, cols] = _bdot(ds, kh, NN).astype(dq_ref.dtype)
            dk_ref[:, cols] += _bdot(ds, qh, TN)
            dv_ref[:, cols] += _bdot(p, doh, TN)

    row = pl.BlockSpec((ts, D), lambda i: (i, 0))
    mem = pl.BlockSpec((M, D), lambda i: (0, 0))
    return pl.pallas_call(
        body, name=name, grid=(S // ts,), in_specs=[row, mem, mem, row], out_specs=[row, mem, mem],
        out_shape=[jax.ShapeDtypeStruct((S, D), BF16), jax.ShapeDtypeStruct((M, D), F32),
                   jax.ShapeDtypeStruct((M, D), F32)],
        compiler_params=_cparams(("arbitrary",)),
    )(q, km, vm, d_o)


def _conv_rows(v, tail, cw_ref, cb_ref):
    K = cw_ref.shape[0]
    sh = [_shift_down(v, tail, K - 1 - k) for k in range(K)]
    out = cb_ref[...]
    for k in range(K):
        out = out + cw_ref[k:k + 1, :] * sh[k]
    return out, sh


FFN_SUB_FWD = 128
FFN_SUB_BWD = 256


def _ffn_up_gate(xn, w_up, cw, cb, *, name, tm=1024, tn=512):
    S, D = xn.shape
    F2 = w_up.shape[1]
    F = F2 // 2
    tm, tn = min(tm, S), min(tn, F)
    sub = min(FFN_SUB_FWD, tm)
    nj = F // tn
    K = cw.shape[0]

    def body(x_ref, wa_ref, wb_ref, cwa_ref, cwb_ref, cba_ref, cbb_ref, act_ref, ha_ref, hb_ref, ac_ref, bc_ref, ta, tb):
        i = pl.program_id(1)

        @pl.when(i == 0)
        def _():
            ta[...] = jnp.zeros_like(ta)
            tb[...] = jnp.zeros_like(tb)

        tail_a, tail_b = ta[...], tb[...]
        for s in range(tm // sub):
            rows = pl.ds(s * sub, sub)
            xs = x_ref[rows, :]
            ha = _bdot(xs, wa_ref[...], NN)
            hb = _bdot(xs, wb_ref[...], NN)
            ac, _ = _conv_rows(ha, tail_a, cwa_ref, cba_ref)
            bc, _ = _conv_rows(hb, tail_b, cwb_ref, cbb_ref)
            tail_a, tail_b = ha[sub - SUB:sub], hb[sub - SUB:sub]
            ha_ref[rows, :] = ha.astype(ha_ref.dtype)
            hb_ref[rows, :] = hb.astype(hb_ref.dtype)
            ac_ref[rows, :] = ac.astype(ac_ref.dtype)
            bc_ref[rows, :] = bc.astype(bc_ref.dtype)
            act_ref[rows, :] = (ac * _sigmoid(ac) * bc).astype(act_ref.dtype)
        ta[...] = tail_a
        tb[...] = tail_b

    blk = pl.BlockSpec((tm, tn), lambda j, i: (i, j))
    return pl.pallas_call(
        body, name=name, grid=(nj, S // tm),
        in_specs=[pl.BlockSpec((tm, D), lambda j, i: (i, 0)),
                  pl.BlockSpec((D, tn), lambda j, i: (0, j)), pl.BlockSpec((D, tn), lambda j, i: (0, j + nj)),
                  pl.BlockSpec((K, tn), lambda j, i: (0, j)), pl.BlockSpec((K, tn), lambda j, i: (0, j + nj)),
                  pl.BlockSpec((1, tn), lambda j, i: (0, j)), pl.BlockSpec((1, tn), lambda j, i: (0, j + nj))],
        out_specs=[blk] * 5,
        out_shape=[jax.ShapeDtypeStruct((S, F), BF16)] * 5,
        scratch_shapes=[pltpu.VMEM((SUB, tn), F32), pltpu.VMEM((SUB, tn), F32)],
        compiler_params=_cparams(("parallel", "arbitrary")),
    )(xn, w_up, w_up, cw, cw, cb, cb)


def _ffn_bwd(dx, w_down, hh_a, hh_b, c_a, c_b, act, xn, cw, *, name, tm=1024, tn=256):
    S, D = dx.shape
    F = hh_a.shape[1]
    tm, tn = min(tm, S), min(tn, F)
    sub = min(FFN_SUB_BWD, tm)
    nj = F // tn
    nb = S // tm
    K = cw.shape[0]

    def body(dx_ref, wd_ref, a_ref, b_ref, ac_ref, bc_ref, act_ref, xn_ref, cwa_ref, cwb_ref,
             da_ref, db_ref, ga_ref, gb_ref, dwd_ref, dwu_ref, ha, hb, acc_d, acc_a, acc_b):
        i = pl.program_id(1)

        @pl.when(i == 0)
        def _():
            for ref in (ha, hb, ga_ref, gb_ref, acc_d, acc_a, acc_b):
                ref[...] = jnp.zeros_like(ref)

        rid8 = _row_iota((SUB, tn))
        heads = [ha[...], hb[...]]
        gsums = [jnp.zeros((SUB, tn), F32), jnp.zeros((SUB, tn), F32)]
        for s in reversed(range(tm // sub)):
            rows = pl.ds(s * sub, sub)
            dv = _bdot(dx_ref[rows, :], wd_ref[...], NT)
            ac, bc = ac_ref[rows, :].astype(F32), bc_ref[rows, :].astype(F32)
            sg = _sigmoid(ac)
            d_bc = dv * ac * sg
            d_ac = dv * bc * sg * (1.0 + ac * (1.0 - sg))
            for which, (d_c, h_ref, cw_ref, o_ref) in enumerate(((d_ac, a_ref, cwa_ref, da_ref),
                                                                 (d_bc, b_ref, cwb_ref, db_ref))):
                ahead = [d_c] + [_shift_up(d_c, heads[which], j) for j in range(1, K)]
                heads[which] = d_c[0:SUB]
                d_in = cw_ref[K - 1:K, :] * d_c
                for j in range(1, K):
                    d_in = d_in + cw_ref[K - 1 - j:K - j, :] * ahead[j]
                o_ref[rows, :] = d_in.astype(o_ref.dtype)
                hv = h_ref[rows, :].astype(F32)
                gsum = gsums[which] + jnp.where(rid8 == K, jnp.sum(d_c, axis=0, keepdims=True), 0.0)
                for k in range(K):
                    gsum = gsum + jnp.where(rid8 == k, jnp.sum(ahead[K - 1 - k] * hv, axis=0, keepdims=True), 0.0)
                gsums[which] = gsum
        ha[...], hb[...] = heads
        ga_ref[...] += gsums[0]
        gb_ref[...] += gsums[1]
        acc_d[...] += _bdot(act_ref[...], dx_ref[...], TN)
        acc_a[...] += _bdot(xn_ref[...], da_ref[...], TN)
        acc_b[...] += _bdot(xn_ref[...], db_ref[...], TN)

        @pl.when(i == nb - 1)
        def _():
            dwd_ref[...] = acc_d[...].astype(dwd_ref.dtype)
            dwu_ref[0] = acc_a[...].astype(dwu_ref.dtype)
            dwu_ref[1] = acc_b[...].astype(dwu_ref.dtype)

    rb = lambda i: nb - 1 - i
    blk = pl.BlockSpec((tm, tn), lambda j, i: (rb(i), j))
    acc = pl.BlockSpec((SUB, tn), lambda j, i: (0, j))
    rows_d = pl.BlockSpec((tm, D), lambda j, i: (rb(i), 0))
    return pl.pallas_call(
        body, name=name, grid=(nj, nb),
        in_specs=[rows_d, pl.BlockSpec((tn, D), lambda j, i: (j, 0)), blk, blk, blk, blk, blk, rows_d,
                  pl.BlockSpec((K, tn), lambda j, i: (0, j)), pl.BlockSpec((K, tn), lambda j, i: (0, j + nj))],
        out_specs=[blk, blk, acc, acc, pl.BlockSpec((tn, D), lambda j, i: (j, 0)),
                   pl.BlockSpec((2, D, tn), lambda j, i: (0, 0, j))],
        out_shape=[jax.ShapeDtypeStruct((S, F), BF16), jax.ShapeDtypeStruct((S, F), BF16),
                   jax.ShapeDtypeStruct((SUB, F), F32), jax.ShapeDtypeStruct((SUB, F), F32),
                   jax.ShapeDtypeStruct((F, D), BF16), jax.ShapeDtypeStruct((2, D, F), BF16)],
        scratch_shapes=[pltpu.VMEM((SUB, tn), F32), pltpu.VMEM((SUB, tn), F32), pltpu.VMEM((tn, D), F32),
                        pltpu.VMEM((D, tn), F32), pltpu.VMEM((D, tn), F32)],
        compiler_params=_cparams(("parallel", "arbitrary")),
    )(dx, w_down, hh_a, hh_b, c_a, c_b, act, xn, cw, cw)


ADAM_BLOCK_ELEMS = 256 * 1024


def _adamw(w, m, v, parts, *, name):
    R, C = w.shape
    n = parts.shape[0]
    tr = R
    for cand in (1024, 512, 256, 128, 64, 32, 16):
        if R % cand == 0 and cand * C <= ADAM_BLOCK_ELEMS:
            tr = cand
            break
    c1 = 1.0 - ADAM_B1 ** ADAM_STEP
    c2 = 1.0 - ADAM_B2 ** ADAM_STEP

    def body(w_ref, m_ref, v_ref, p_ref, g_ref, d_ref, nm_ref, nv_ref):
        g = p_ref[0].astype(F32)
        for k in range(1, n):
            g = g + p_ref[k].astype(F32)
        m_new = ADAM_B1 * m_ref[...] + (1.0 - ADAM_B1) * g
        v_new = ADAM_B2 * v_ref[...] + (1.0 - ADAM_B2) * (g * g)
        m_hat = m_new / c1
        v_hat = v_new / c2
        g_ref[...] = g
        d_ref[...] = -ADAM_LR * (m_hat / (jnp.sqrt(v_hat) + ADAM_EPS) + ADAM_WD * w_ref[...])
        nm_ref[...] = m_new
        nv_ref[...] = v_new

    blk = pl.BlockSpec((tr, C), lambda i: (i, 0))
    sds = jax.ShapeDtypeStruct((R, C), F32)
    return pl.pallas_call(
        body, name=name, grid=(R // tr,),
        in_specs=[blk, blk, blk, pl.BlockSpec((n, tr, C), lambda i: (0, i, 0))],
        out_specs=[blk, blk, blk, blk], out_shape=[sds, sds, sds, sds],
        compiler_params=_cparams(("parallel",)),
    )(w, m, v, parts)


def _mesh_place():
    x, y, c = lax.axis_index("x"), lax.axis_index("y"), lax.axis_index("c")
    others = [(1 - x, y), (x, 1 - y), (1 - x, 1 - y)]
    return x, y, c, others


HBM_SPEC = pl.BlockSpec(memory_space=pltpu.HBM)
SEM_SPEC = pl.BlockSpec(memory_space=pltpu.SEMAPHORE)
ANY_SPEC = pl.BlockSpec(memory_space=pl.ANY)
EFFECT = pltpu.SideEffectType.DATAFLOW_SIDE_EFFECTING


def _in_hbm(a):
    return pltpu.with_memory_space_constraint(a, pltpu.HBM)


def _split_start(srcs, lands, copies, n_cp, *, name):
    n_s, n_l = len(srcs), len(lands)

    def body(*refs):
        src_refs, land_refs = refs[:n_s], refs[n_s:n_s + n_l]
        ssem, rsem = refs[n_s + n_l], refs[n_s + n_l + 1]
        token = refs[-1]
        for outgoing, _ in copies(src_refs, land_refs, ssem, rsem):
            outgoing.start()
        token[...] = jnp.zeros_like(token)

    outs = pl.pallas_call(
        body, name=name,
        out_shape=(pltpu.SemaphoreType.DMA((n_cp,)), pltpu.SemaphoreType.DMA((n_cp,)),
                   *[pltpu.HBM(a.shape, a.dtype) for a in srcs], *[pltpu.HBM(a.shape, a.dtype) for a in lands],
                   jax.ShapeDtypeStruct((SUB, LANE), F32)),
        in_specs=[HBM_SPEC] * (n_s + n_l),
        out_specs=(SEM_SPEC, SEM_SPEC, *[HBM_SPEC] * (n_s + n_l), pl.BlockSpec(memory_space=pltpu.VMEM)),
        input_output_aliases={i: 2 + i for i in range(n_s + n_l)},
        compiler_params=pltpu.CompilerParams(has_side_effects=EFFECT),
    )(*[_in_hbm(a) for a in srcs], *[_in_hbm(a) for a in lands])
    ssem, rsem = outs[0], outs[1]
    return ssem, rsem, list(outs[2:2 + n_s]), list(outs[2 + n_s:2 + n_s + n_l]), outs[-1]


def _split_wait(srcs, lands, ssem, rsem, after, copies, *, name):
    n_s, n_l = len(srcs), len(lands)

    def body(*refs):
        src_refs, land_refs = refs[:n_s], refs[n_s:n_s + n_l]
        s_ref, r_ref = refs[n_s + n_l], refs[n_s + n_l + 1]
        for outgoing, incoming in copies(src_refs, land_refs, s_ref, r_ref):
            outgoing.wait_send()
            incoming.wait_recv()

    outs = pl.pallas_call(
        body, name=name,
        out_shape=(*[pltpu.HBM(a.shape, a.dtype) for a in srcs], *[pltpu.HBM(a.shape, a.dtype) for a in lands]),
        in_specs=[HBM_SPEC] * (n_s + n_l) + [SEM_SPEC, SEM_SPEC, ANY_SPEC], out_specs=[HBM_SPEC] * (n_s + n_l),
        input_output_aliases={i: i for i in range(n_s + n_l)},
        compiler_params=pltpu.CompilerParams(has_side_effects=EFFECT),
    )(*srcs, *lands, ssem, rsem, after)
    return list(outs[:n_s]), list(outs[n_s:])


PLACE_BLOCK_ELEMS = 512 * 1024


def _place_rows(r, w):
    return _div_tile(r, max(16, PLACE_BLOCK_ELEMS // w), 16)


def _cast_place(shard, chip, axis, after, *, name):
    r, w = shard.shape
    tr = _place_rows(r, w)
    nb = r // tr
    full = (r * N_CHIP, w) if axis == 0 else (r, w * N_CHIP)
    has_after = after is not None

    def body(chip_ref, s_ref, *rest):
        rest[-1][...] = s_ref[...].astype(BF16)

    out_map = (lambda i, ch: (ch[0] * nb + i, 0)) if axis == 0 else (lambda i, ch: (i, ch[0]))
    grid_spec = pltpu.PrefetchScalarGridSpec(
        num_scalar_prefetch=1, grid=(nb,),
        in_specs=[pl.BlockSpec((tr, w), lambda i, ch: (i, 0))] + has_after * [ANY_SPEC],
        out_specs=pl.BlockSpec((tr, w), out_map))
    return pl.pallas_call(body, name=name, grid_spec=grid_spec, out_shape=jax.ShapeDtypeStruct(full, BF16),
                          compiler_params=_cparams(("parallel",)))(chip, shard, *(has_after * [after]))


def _grad_shard_shape(g, axis):
    if g.ndim == 3:
        return g.shape[1], 2 * g.shape[2] // N_CHIP
    return (g.shape[0] // N_CHIP, g.shape[1]) if axis == 0 else (g.shape[0], g.shape[1] // N_CHIP)


def _slot_place(g, ids, axis, *, name):
    r, w = _grad_shard_shape(g, axis)
    tr = _place_rows(r, w)
    nb = r // tr

    def body(ids_ref, g_ref, o_ref):
        o_ref[...] = g_ref[...]

    if g.ndim == 3:
        in_spec = pl.BlockSpec((None, tr, w), lambda i, ids_: (ids_[0] // 2, i, ids_[0] % 2))
    elif axis == 0:
        in_spec = pl.BlockSpec((tr, w), lambda i, ids_: (ids_[0] * nb + i, 0))
    else:
        in_spec = pl.BlockSpec((tr, w), lambda i, ids_: (i, ids_[0]))
    grid_spec = pltpu.PrefetchScalarGridSpec(
        num_scalar_prefetch=1, grid=(nb,), in_specs=[in_spec],
        out_specs=pl.BlockSpec((None, tr, w), lambda i, ids_: (ids_[1], i, 0)))
    return pl.pallas_call(body, name=name, grid_spec=grid_spec, out_shape=jax.ShapeDtypeStruct((N_DEV, r, w), g.dtype),
                          compiler_params=_cparams(("parallel",)))(ids, g)


class _WeightGather:
    def __init__(self, placed, shard_shapes, axes, splits, tag):
        self.placed, self.shard_shapes, self.axes, self.splits, self.tag = list(placed), shard_shapes, axes, splits, tag
        self.n = len(placed)

    def _region(self, land_refs, it, chip, half):
        r, w = self.shard_shapes[it]
        by_rows = self.axes[it] == 0
        if self.splits[it] and half is not None:
            rows = pl.ds(pl.multiple_of(half * (r // 2) + (chip * r if by_rows else 0), 16), r // 2)
        else:
            rows = pl.ds(chip * r if by_rows else 0, r)
        cols = pl.ds(0, w) if by_rows else pl.ds(pl.multiple_of(chip * w, LANE), w)
        return land_refs[it].at[rows, cols]

    def _ici(self, src_refs, land_refs, ssem, rsem):
        x, y, c, others = _mesh_place()
        pairs = []
        for it in range(self.n):
            for j, chip in enumerate(others):
                def mk(chip_from, it=it, j=j, chip=chip):
                    return pltpu.make_async_remote_copy(
                        src_ref=self._region(land_refs, it, 2 * x + y, c), dst_ref=self._region(land_refs, it, chip_from, c),
                        send_sem=ssem.at[3 * it + j], recv_sem=rsem.at[3 * it + j], device_id=(*chip, c),
                        device_id_type=MESH)
                pairs.append((mk(2 * x + y), mk(2 * chip[0] + chip[1])))
        return pairs

    def start(self):
        self.ssem, self.rsem, _, self.lands, token = _split_start(
            [], self.placed, self._ici, 3 * self.n, name="gather_start_" + self.tag)
        return token

    def _d2d(self, src_refs, land_refs, ssem, rsem):
        x, y, c, others = _mesh_place()
        pairs = []
        for it in range(self.n):
            if self.splits[it]:
                for chip in others:
                    def mk(half, it=it, chip=chip, k=len(pairs)):
                        reg = self._region(land_refs, it, 2 * chip[0] + chip[1], half)
                        return pltpu.make_async_remote_copy(src_ref=reg, dst_ref=reg, send_sem=ssem.at[k], recv_sem=rsem.at[k],
                                                            device_id=(x, y, 1 - c), device_id_type=MESH)
                    pairs.append((mk(c), mk(1 - c)))
        return pairs

    def forward(self, after):
        _, lands = _split_wait([], self.lands, self.ssem, self.rsem, after, self._ici,
                               name="gather_wait_" + self.tag)
        self.fsem, self.frsem, _, self.lands, token = _split_start(
            [], lands, self._d2d, 3 * sum(self.splits), name="gather_fwd_" + self.tag)
        return token

    def finish_forward(self, after):
        _, lands = _split_wait([], self.lands, self.fsem, self.frsem, after, self._d2d,
                               name="gather_fwd_wait_" + self.tag)
        return lands

    def finish(self, after):
        _, lands = _split_wait([], self.lands, self.ssem, self.rsem, after, self._ici,
                               name="gather_wait_" + self.tag)
        n = self.n
        n_fwd = 3 * sum(self.splits)
        if n_fwd == 0:
            return lands

        def body(*refs):
            out_refs = refs[n:2 * n]
            fsend, frecv = refs[2 * n:]
            x, y, c, others = _mesh_place()
            sibling = (x, y, 1 - c)

            def fwd(it, slot, chip, half):
                reg = self._region(out_refs, it, 2 * chip[0] + chip[1], half)
                return pltpu.make_async_remote_copy(src_ref=reg, dst_ref=reg, send_sem=fsend.at[slot],
                                                    recv_sem=frecv.at[slot], device_id=sibling, device_id_type=MESH)

            sends, recvs = [], []
            for it in range(n):
                if self.splits[it]:
                    for chip in others:
                        sends.append(fwd(it, len(sends), chip, c))
                        recvs.append(fwd(it, len(recvs), chip, 1 - c))
            for cp in sends:
                cp.start()
            for cp in recvs:
                cp.wait_recv()
            for cp in sends:
                cp.wait_send()

        fulls = pl.pallas_call(
            body, name="gather_d2d_" + self.tag, in_specs=[ANY_SPEC] * n, out_specs=[ANY_SPEC] * n,
            out_shape=[jax.ShapeDtypeStruct(a.shape, a.dtype) for a in lands],
            scratch_shapes=[pltpu.SemaphoreType.DMA((n_fwd,)), pltpu.SemaphoreType.DMA((n_fwd,))],
            input_output_aliases={i: i for i in range(n)},
        )(*lands)
        return list(fulls)


class _GradGather:
    def __init__(self, grads, axes, tag):
        self.grads, self.axes, self.tag = list(grads), axes, tag
        self.n = len(grads)
        self.shard_shapes = [_grad_shard_shape(g, ax) for g, ax in zip(grads, axes)]

    def _piece(self, src_refs, it, chip):
        r, w = self.shard_shapes[it]
        if self.grads[it].ndim == 3:
            return src_refs[it].at[chip // 2, :, pl.ds(pl.multiple_of((chip % 2) * w, LANE), w)]
        if self.axes[it] == 0:
            return src_refs[it].at[pl.ds(pl.multiple_of(chip * r, 16), r), :]
        return src_refs[it].at[:, pl.ds(pl.multiple_of(chip * w, LANE), w)]

    PER_ITEM = 4

    def _remote(self, src_refs, land_refs, ssem, rsem):
        x, y, c, others = _mesh_place()
        me = 4 * x + 2 * y + c
        pairs = []
        for it in range(self.n):
            def mk(k, piece_chip, slot, to, it=it):
                return pltpu.make_async_remote_copy(
                    src_ref=self._piece(src_refs, it, piece_chip), dst_ref=land_refs[it].at[slot],
                    send_sem=ssem.at[self.PER_ITEM * it + k], recv_sem=rsem.at[self.PER_ITEM * it + k], device_id=to,
                    device_id_type=MESH)
            for j, chip in enumerate(others):
                chip_id = 2 * chip[0] + chip[1]
                pairs.append((mk(j, chip_id, me, (*chip, c)), mk(j, chip_id, 2 * chip_id + c, (*chip, c))))
            sibling = (x, y, 1 - c)
            pairs.append((mk(3, 2 * x + y, me, sibling), mk(3, 2 * x + y, 4 * x + 2 * y + 1 - c, sibling)))
        return pairs

    def start(self):
        x, y, c = lax.axis_index("x"), lax.axis_index("y"), lax.axis_index("c")
        ids = jnp.stack([2 * x + y, 4 * x + 2 * y + c]).astype(jnp.int32)
        lands = [_slot_place(g, ids, ax, name="grads_own_%s%d" % (self.tag, it))
                 for it, (g, ax) in enumerate(zip(self.grads, self.axes))]
        self.ssem, self.rsem, self.srcs, self.lands, token = _split_start(
            self.grads, lands, self._remote, self.PER_ITEM * self.n, name="grads_start_" + self.tag)
        return token

    def _forward(self, src_refs, land_refs, ssem, rsem):
        x, y, c, others = _mesh_place()
        pairs = []
        for it in range(self.n):
            for j, ch in enumerate(others):
                def mk(slot, it=it, j=j):
                    return pltpu.make_async_remote_copy(
                        src_ref=land_refs[it].at[slot], dst_ref=land_refs[it].at[slot], send_sem=ssem.at[3 * it + j],
                        recv_sem=rsem.at[3 * it + j], device_id=(x, y, 1 - c), device_id_type=MESH)
                pairs.append((mk(4 * ch[0] + 2 * ch[1] + c), mk(4 * ch[0] + 2 * ch[1] + 1 - c)))
        return pairs

    def forward(self, after):
        _, lands = _split_wait(self.srcs, self.lands, self.ssem, self.rsem, after, self._remote,
                               name="grads_wait_" + self.tag)
        self.fsem, self.frsem, _, self.lands, token = _split_start(
            [], lands, self._forward, 3 * self.n, name="grads_fwd_" + self.tag)
        return token

    def finish(self, after):
        _, lands = _split_wait([], self.lands, self.fsem, self.frsem, after, self._forward,
                               name="grads_fwd_wait_" + self.tag)
        return lands


def _allreduce_small(vec, *, name):
    R, L = vec.shape

    def body(v_ref, o_ref, buf, send, recv, lsem):
        x, y, c, others = _mesh_place()
        me = 4 * x + 2 * y + c
        sibling = (x, y, 1 - c)

        def copy(k, slot, to, src=None):
            return pltpu.make_async_remote_copy(
                src_ref=buf.at[slot] if src is None else src, dst_ref=buf.at[slot], send_sem=send.at[k],
                recv_sem=recv.at[k], device_id=to, device_id_type=MESH)

        def slot_of(chip, core):
            return 4 * chip[0] + 2 * chip[1] + core

        mine = pltpu.make_async_copy(v_ref, buf.at[me], lsem)
        mine.start()
        first = [copy(0, me, sibling, src=v_ref)]
        first += [copy(1 + j, me, (*chip, c), src=v_ref) for j, chip in enumerate(others)]
        for cp in first:
            cp.start()
        passed = [copy(4 + j, slot_of(chip, c), sibling) for j, chip in enumerate(others)]
        for j, chip in enumerate(others):
            copy(1 + j, slot_of(chip, c), (*chip, c)).wait_recv()
            passed[j].start()
        copy(0, slot_of((x, y), 1 - c), sibling).wait_recv()
        for j, chip in enumerate(others):
            copy(4 + j, slot_of(chip, 1 - c), sibling).wait_recv()
        for cp in first + passed:
            cp.wait_send()
        mine.wait()
        total = buf[0]
        for k in range(1, N_DEV):
            total = total + buf[k]
        o_ref[...] = total

    return pl.pallas_call(
        body, name=name, in_specs=[pl.BlockSpec(memory_space=pltpu.VMEM)],
        out_specs=pl.BlockSpec(memory_space=pltpu.VMEM), out_shape=jax.ShapeDtypeStruct((R, L), F32),
        scratch_shapes=[pltpu.VMEM((N_DEV, R, L), F32), pltpu.SemaphoreType.DMA((7,)), pltpu.SemaphoreType.DMA((7,)),
                        pltpu.SemaphoreType.DMA],
        compiler_params=pltpu.CompilerParams(vmem_limit_bytes=VMEM_LIMIT),
    )(vec)


PACK_ALIGN = 1024


def _pack(arrs, row_multiple):
    flat = []
    for a in arrs:
        f = a.reshape(-1).astype(F32)
        flat.append(jnp.pad(f, (0, (-f.shape[0]) % PACK_ALIGN)))
    v = jnp.concatenate(flat)
    v = jnp.pad(v, (0, (-v.shape[0]) % (LANE * row_multiple)))
    return v.reshape(-1, LANE)


def _unpack(v, shapes):
    flat = v.reshape(-1)
    out, off = [], 0
    for s in shapes:
        size = math.prod(s)
        out.append(flat[off:off + size].reshape(s))
        off += size + (-size) % PACK_ALIGN
    return out


def _div_tile(dim, cap, mult=LANE):
    best = None
    for cand in range(mult, min(cap, dim) + 1, mult):
        if dim % cand == 0:
            best = cand
    return dim if best is None else best


WEIGHT_NAMES = ('norm1_g', 'w_in', 'ret_g', 'rg_conv_w', 'rg_conv_b', 'rg_wa', 'rg_ba', 'rg_wx', 'rg_bx', 'rg_lambda',
                'w_out', 'norm2_g', 'norm_mem_g', 'xa_wq', 'xa_wk', 'xa_wv', 'xa_wo', 'norm3_g', 'ffn_w_up',
                'ffn_conv_w', 'ffn_conv_b', 'ffn_w_down', 'final_g')
BIG_AXIS = {'w_in': 1, 'w_out': 0, 'xa_wq': 0, 'xa_wk': 0, 'xa_wv': 0, 'xa_wo': 0, 'ffn_w_up': 1, 'ffn_w_down': 0}
SMALL_SHARDED = ('rg_conv_w', 'ffn_conv_w')


def _step(x, mem, positions, loss_target, W, Mo, Vo):
    S, D = x.shape[1], x.shape[2]
    xs, mems, tgt = x[0], mem[0], loss_target[0]
    n_mem = mems.shape[0]
    pos_col = positions.reshape(S, 1)
    chip = 2 * lax.axis_index("x") + lax.axis_index("y")

    big = list(BIG_AXIS)
    shards = {n: W[n][0] for n in big}
    G = {}
    gather_groups = (('w_in', 'rg_conv_w'), ('w_out', 'xa_wq', 'xa_wk', 'xa_wv', 'xa_wo'),
                     ('ffn_w_up', 'ffn_conv_w'), ('ffn_w_down',))
    gathers, tok = [], None
    chip1 = jnp.reshape(chip, (1,)).astype(jnp.int32)
    for gi, names in enumerate(gather_groups):
        placed = []
        for n in names:
            if n in BIG_AXIS:
                placed.append(_cast_place(shards[n], chip1, BIG_AXIS[n], tok, name="place_" + n))
            else:
                s = W[n][0] if tok is None else W[n][0] + tok[0, 0]
                full = lax.empty((s.shape[0], s.shape[1] * N_CHIP), s.dtype)
                placed.append(lax.dynamic_update_slice(full, s, (0, chip * s.shape[1])))
        ag = _WeightGather(placed, [W[n][0].shape for n in names], [BIG_AXIS.get(n, 1) for n in names],
                           [n in BIG_AXIS for n in names], "g%d" % gi)
        tok = ag.start()
        gathers.append(ag)

    def finish_gather(gi, after):
        G.update(zip(gather_groups[gi], gathers[gi].finish(after)))

    def finish_forward(gi, after):
        G.update(zip(gather_groups[gi], gathers[gi].finish_forward(after)))

    R = W['ret_g'].shape[1]
    Wl = W['rg_lambda'].shape[1]
    IN = W['w_in'].shape[2] * N_CHIP
    F2 = W['ffn_w_up'].shape[2] * N_CHIP
    F = F2 // 2

    norm1_g, norm2_g, norm3_g = W['norm1_g'] + tok[0, 0], W['norm2_g'], W['norm3_g']
    norm_mem_g, final_g, ret_g = W['norm_mem_g'], W['final_g'].reshape(1, D), W['ret_g']
    rg_cb = W['rg_conv_b']
    wa, wx = W['rg_wa'][0], W['rg_wx'][0]
    ba, bx = W['rg_ba'].reshape(1, Wl), W['rg_bx'].reshape(1, Wl)
    lam = W['rg_lambda']
    ffn_cb = W['ffn_conv_b']

    def fwd_mm(a, wname, N, K, **kw):
        return _mm(a, G[wname], mode="nn", M=a.shape[0], N=N, K=K, tm=_div_tile(a.shape[0], 1024),
                   tn=1024 if K <= 3072 else 512, tk=K, **kw)

    def fwd_mm_norm(a, wname, res, g, name):
        return _mm(a, G[wname], mode="nn", M=a.shape[0], N=D, K=a.shape[1], tm=512, tn=D, tk=_div_tile(a.shape[1], 2048),
                   out_dtype=F32, res=res, norm_g=g, name=name)

    def bwd_x_mm(d, wname, N, K, **kw):
        return _mm(d, G[wname], mode="nt", M=d.shape[0], N=N, K=K, tm=_div_tile(d.shape[0], 1024),
                   tn=_div_tile(N, 1024 if K <= 3072 else 512, 256), tk=K, **kw)

    def bwd_w_mm(a, d, M, N, **kw):
        Ks = a.shape[0]
        return _mm(a, d, mode="tn", M=M, N=N, K=Ks, out_dtype=BF16, tm=_div_tile(M, 1024, 256),
                   tn=_div_tile(N, 1024, 256), tk=_div_tile(Ks, 4096 if d.dtype == BF16 else 1024), **kw)

    xn1 = _rmsnorm_fwd(xs, norm1_g, name="norm1_fwd")
    half = (R // RET_HEADS) // 2
    inv = (ROPE_BASE ** (-jnp.arange(half, dtype=F32) / half)).reshape(1, half)
    cos, sin = _rope_table(pos_col, inv + tok[0, 0], name="rope_table")
    finish_gather(0, cos)
    rg_cw = G['rg_conv_w']
    h = fwd_mm(xn1, 'w_in', IN, D, out_dtype=F32, name="mm_in")
    hl, mix = _lru_fwd(h, rg_cw, rg_cb, wa, ba, wx, bx, lam, name="lru_fwd")
    t1 = gathers[1].forward(hl)
    ret_raw, states, mix = _ret_fwd(h, cos, sin, ret_g + t1[0, 0], mix, name="ret_fwd")
    finish_forward(1, mix)
    x1, xn2 = fwd_mm_norm(mix, 'w_out', xs, norm2_g, "mm_out")
    memn = _rmsnorm_fwd(mems, norm_mem_g, name="norm_mem_fwd")
    km = fwd_mm(memn, 'xa_wk', D, D, out_dtype=BF16, name="mm_k")
    vm = fwd_mm(memn, 'xa_wv', D, D, out_dtype=BF16, name="mm_v")
    t2 = gathers[2].forward(x1)
    q = fwd_mm(xn2, 'xa_wq', D, D, out_dtype=BF16, after=t2, name="mm_q")
    o = _xattn_fwd(q, km, vm, name="xattn_fwd")
    x2, xn3 = fwd_mm_norm(o, 'xa_wo', x1, norm3_g, "mm_o")
    finish_forward(2, xn3)
    t3 = gathers[3].forward(xn3)
    ffn_cw = G['ffn_conv_w']
    act, hh_a, hh_b, hc_a, hc_b = _ffn_up_gate(xn3, G['ffn_w_up'], ffn_cw, ffn_cb + t3[0, 0], name="ffn_up_gate")
    finish_forward(3, act)
    x3 = fwd_mm(act, 'ffn_w_down', D, F, out_dtype=F32, res=x2, name="mm_down")
    dx3, d_final, loss8, dx3h = _final_loss(x3, tgt, final_g, name="final_loss")

    gw = {}
    grad_groups = []

    def start_grads(names, tag):
        gg = _GradGather([gw[n] for n in names], [BIG_AXIS[n] for n in names], tag)
        grad_groups.append((names, gg))
        return gg.start()

    dhh_a, dhh_b, gcw_a, gcw_b, gw['ffn_w_down'], gw['ffn_w_up'] = _ffn_bwd(
        dx3h, G['ffn_w_down'], hh_a, hh_b, hc_a, hc_b, act, xn3, ffn_cw, name="ffn_bwd")
    tok_a = start_grads(('ffn_w_down', 'ffn_w_up'), "a")
    dxn3 = bwd_x_mm(dhh_a, 'ffn_w_up', D, F, out_dtype=F32, after=tok_a, name="mm_dxn3_a")
    dxn3 = bwd_x_mm(dhh_b, 'ffn_w_up', D, F, out_dtype=BF16, b_off=(0, F), res=dxn3, name="mm_dxn3_b")
    dx2, d_norm3, dx2h = _rmsnorm_bwd(x2, dxn3, norm3_g, dx3, name="norm3_bwd", emit_bf16=True)
    Kc = ffn_cw.shape[0]
    d_ffn_cw = jnp.concatenate([gcw_a[:Kc], gcw_b[:Kc]], axis=1)
    d_ffn_cb = jnp.concatenate([gcw_a[Kc:Kc + 1], gcw_b[Kc:Kc + 1]], axis=1)

    d_o = bwd_x_mm(dx2h, 'xa_wo', D, D, out_dtype=BF16, name="mm_do")
    gw['xa_wo'] = bwd_w_mm(o, dx2h, D, D, name="mm_dw_o")
    dq, dk, dv = _xattn_bwd(q, km, vm, d_o, name="xattn_bwd")
    gw['xa_wq'] = bwd_w_mm(xn2, dq, D, D, name="mm_dw_q")
    dxn2 = bwd_x_mm(dq, 'xa_wq', D, D, out_dtype=BF16, name="mm_dxn2")
    gw['xa_wk'] = bwd_w_mm(memn, dk, D, D, name="mm_dw_k")
    gw['xa_wv'] = bwd_w_mm(memn, dv, D, D, name="mm_dw_v")
    dmemn = bwd_x_mm(dk, 'xa_wk', D, D, out_dtype=F32, name="mm_dmem_k")
    dmemn = bwd_x_mm(dv, 'xa_wv', D, D, out_dtype=F32, res=dmemn, name="mm_dmem_v")
    _, d_norm_mem = _rmsnorm_bwd(mems, dmemn, norm_mem_g, None, name="norm_mem_bwd")
    dx1, d_norm2, dx1h = _rmsnorm_bwd(x1, dxn2, norm2_g, dx2, name="norm2_bwd", emit_bf16=True)

    gw['w_out'] = bwd_w_mm(mix, dx1h, D, D, name="mm_dw_out")
    tok_b = start_grads(('xa_wo', 'xa_wq', 'xa_wk', 'xa_wv', 'w_out'), "b")
    dmix = bwd_x_mm(dx1h, 'w_out', D, D, out_dtype=BF16, after=tok_b, name="mm_dmix")
    dh, d_ret_g = _ret_bwd(h, cos, sin, ret_g, states, ret_raw, dmix, name="ret_bwd")
    dh, d_rcw, d_rcb, d_wa, d_ba, d_wx, d_bx, d_lam = _lru_bwd(
        h, hl, dmix, dh, rg_cw, rg_cb, wa, ba, wx, bx, lam, name="lru_bwd")
    gw['w_in'] = bwd_w_mm(xn1, dh, D, IN, name="mm_dw_in")
    tok_c = start_grads(('w_in',), "c")
    dxn1 = bwd_x_mm(dh, 'w_in', D, IN, out_dtype=BF16, after=tok_c, name="mm_dxn1")
    grad_x, d_norm1 = _rmsnorm_bwd(xs, dxn1, norm1_g, dx1, name="norm1_bwd")

    small_parts = {
        'norm1_g': d_norm1, 'ret_g': d_ret_g, 'rg_conv_w': d_rcw[:rg_cw.shape[0]], 'rg_conv_b': d_rcb,
        'rg_wa': d_wa, 'rg_ba': d_ba, 'rg_wx': d_wx, 'rg_bx': d_bx, 'rg_lambda': d_lam, 'norm2_g': d_norm2,
        'norm_mem_g': d_norm_mem, 'norm3_g': d_norm3, 'ffn_conv_w': d_ffn_cw, 'ffn_conv_b': d_ffn_cb,
        'final_g': d_final}
    small = [n for n in WEIGHT_NAMES if n not in BIG_AXIS]
    red_shapes = [(1,)] + [tuple(small_parts[n].shape) for n in small]
    fwd_tok = sum(gg.forward(d_norm1)[0:1, 0:1] for _, gg in grad_groups)
    reduced = _allreduce_small(_pack([loss8[0:1, 0:1] + fwd_tok] + [small_parts[n] for n in small], SUB),
                               name="allreduce_small")
    red = _unpack(reduced, red_shapes)
    loss = red[0][0]
    g_small = dict(zip(small, red[1:]))
    for n in SMALL_SHARDED:
        w_local = W[n].shape[-1]
        g_small[n] = lax.dynamic_slice_in_dim(g_small[n], chip * w_local, w_local, axis=1)

    out_g, out_d, out_m, out_v = {}, {}, {}, {}
    rows = 512
    pk = lambda d: _pack([d[n] for n in small], rows)
    g_pack = _pack([g_small[n] for n in small], rows)
    res_small = _adamw(pk(W), pk(Mo), pk(Vo), g_pack[None], name="adamw_small")
    shapes_small = [tuple(W[n].shape) for n in small]
    for dst, packed in zip((out_g, out_d, out_m, out_v), res_small):
        for n, val in zip(small, _unpack(packed, shapes_small)):
            dst[n] = val
    last = res_small[0]
    for names, gg in grad_groups:
        for n, land in zip(names, gg.finish(last)):
            g, d, m_new, v_new = _adamw(shards[n], Mo[n][0], Vo[n][0], land, name="adamw_" + n)
            out_g[n], out_d[n], out_m[n], out_v[n] = (t.reshape(W[n].shape) for t in (g, d, m_new, v_new))
            last = g
    return (loss, grad_x[None], *[out_g[n] for n in WEIGHT_NAMES], *[out_d[n] for n in WEIGHT_NAMES],
            *[out_m[n] for n in WEIGHT_NAMES], *[out_v[n] for n in WEIGHT_NAMES])


def kernel(x, mem, positions, norm1_g, w_in, ret_g, rg_conv_w, rg_conv_b, rg_wa, rg_ba, rg_wx, rg_bx, rg_lambda, w_out, norm2_g, norm_mem_g, xa_wq, xa_wk, xa_wv, xa_wo, norm3_g, ffn_w_up, ffn_conv_w, ffn_conv_b, ffn_w_down, final_g, loss_target, m_norm1_g, m_w_in, m_ret_g, m_rg_conv_w, m_rg_conv_b, m_rg_wa, m_rg_ba, m_rg_wx, m_rg_bx, m_rg_lambda, m_w_out, m_norm2_g, m_norm_mem_g, m_xa_wq, m_xa_wk, m_xa_wv, m_xa_wo, m_norm3_g, m_ffn_w_up, m_ffn_conv_w, m_ffn_conv_b, m_ffn_w_down, m_final_g, v_norm1_g, v_w_in, v_ret_g, v_rg_conv_w, v_rg_conv_b, v_rg_wa, v_rg_ba, v_rg_wx, v_rg_bx, v_rg_lambda, v_w_out, v_norm2_g, v_norm_mem_g, v_xa_wq, v_xa_wk, v_xa_wv, v_xa_wo, v_norm3_g, v_ffn_w_up, v_ffn_conv_w, v_ffn_conv_b, v_ffn_w_down, v_final_g):
    W = dict(zip(WEIGHT_NAMES, (norm1_g, w_in, ret_g, rg_conv_w, rg_conv_b, rg_wa, rg_ba, rg_wx, rg_bx, rg_lambda, w_out,
                                norm2_g, norm_mem_g, xa_wq, xa_wk, xa_wv, xa_wo, norm3_g, ffn_w_up, ffn_conv_w,
                                ffn_conv_b, ffn_w_down, final_g)))
    Mo = dict(zip(WEIGHT_NAMES, (m_norm1_g, m_w_in, m_ret_g, m_rg_conv_w, m_rg_conv_b, m_rg_wa, m_rg_ba, m_rg_wx, m_rg_bx,
                                 m_rg_lambda, m_w_out, m_norm2_g, m_norm_mem_g, m_xa_wq, m_xa_wk, m_xa_wv, m_xa_wo,
                                 m_norm3_g, m_ffn_w_up, m_ffn_conv_w, m_ffn_conv_b, m_ffn_w_down, m_final_g)))
    Vo = dict(zip(WEIGHT_NAMES, (v_norm1_g, v_w_in, v_ret_g, v_rg_conv_w, v_rg_conv_b, v_rg_wa, v_rg_ba, v_rg_wx, v_rg_bx,
                                 v_rg_lambda, v_w_out, v_norm2_g, v_norm_mem_g, v_xa_wq, v_xa_wk, v_xa_wv, v_xa_wo,
                                 v_norm3_g, v_ffn_w_up, v_ffn_conv_w, v_ffn_conv_b, v_ffn_w_down, v_final_g)))
    return _step(x, mem, positions, loss_target, W, Mo, Vo)
```

```python
import math

import jax
import jax.numpy as jnp
from jax import lax
from jax.experimental import pallas as pl
from jax.experimental.pallas import tpu as pltpu

F32 = jnp.float32
BF16 = jnp.bfloat16

EPS = 1e-6
RET_HEADS = 4
RET_CHUNK = 128
ROPE_BASE = 10000.0
LRU_BLOCKS = 8
LRU_C = 8.0
XA_HEADS = 4

ADAM_LR = 0.001
ADAM_B1 = 0.9
ADAM_B2 = 0.999
ADAM_EPS = 1e-08
ADAM_WD = 0.01
ADAM_STEP = 10

N_DEV = 8
N_CHIP = 4
MESH = pl.DeviceIdType.MESH
SUB = 8
LANE = 128
VMEM_LIMIT = 56 * 1024 * 1024

NN = ((1,), (0,))
NT = ((1,), (1,))
TN = ((0,), (0,))


def _cparams(sem):
    return pltpu.CompilerParams(dimension_semantics=sem, vmem_limit_bytes=VMEM_LIMIT)


def _sigmoid(v):
    return 1.0 / (1.0 + jnp.exp(-v))


def _bdot(a, b, dims):
    return lax.dot_general(a.astype(BF16), b.astype(BF16), (dims, ((), ())), preferred_element_type=F32)


def _row_iota(shape):
    return lax.broadcasted_iota(jnp.int32, shape, 0)


def _shift_down(v, tail, k):
    if k == 0:
        return v
    r = pltpu.roll(v, k, 0)
    rt = pltpu.roll(tail, k, 0)
    first = jnp.where(_row_iota(rt.shape) < k, rt, r[0:SUB])
    return jnp.concatenate([first, r[SUB:]], axis=0)


def _shift_up(v, head, k):
    if k == 0:
        return v
    n = v.shape[0]
    r = pltpu.roll(v, n - k, 0)
    rh = pltpu.roll(head, SUB - k, 0)
    last = jnp.where(_row_iota(rh.shape) >= SUB - k, rh, r[n - SUB:n])
    return jnp.concatenate([r[:n - SUB], last], axis=0)


def _mm(a, b, *, mode, M, N, K, out_dtype, name, tm=512, tn=512, tk=512, b_off=(0, 0), res=None, norm_g=None,
        after=None):
    tm, tn, tk = min(tm, M), min(tn, N), min(tk, K)
    assert M % tm == 0 and N % tn == 0 and K % tk == 0, (name, M, N, K, tm, tn, tk)
    nk = K // tk
    if mode == "nn":
        a_blk, b_blk, dims = (tm, tk), (tk, tn), NN
        a_map = lambda i, j, k: (i, k)
        b_map = lambda i, j, k: (k + b_off[0] // tk, j + b_off[1] // tn)
    elif mode == "nt":
        a_blk, b_blk, dims = (tm, tk), (tn, tk), NT
        a_map = lambda i, j, k: (i, k)
        b_map = lambda i, j, k: (j + b_off[0] // tn, k + b_off[1] // tk)
    else:
        a_blk, b_blk, dims = (tk, tm), (tk, tn), TN
        a_map = lambda i, j, k: (k, i)
        b_map = lambda i, j, k: (k + b_off[0] // tk, j + b_off[1] // tn)
    assert b_off[0] % b_blk[0] == 0 and b_off[1] % b_blk[1] == 0, (name, b_off, b_blk)
    has_res, has_norm, has_after = res is not None, norm_g is not None, after is not None
    assert not has_norm or tn == N

    def body(*refs):
        refs = list(refs)
        a_ref, b_ref = refs[0], refs[1]
        pos = 2
        r_ref = g_ref = n_ref = None
        if has_res:
            r_ref = refs[pos]
            pos += 1
        if has_norm:
            g_ref = refs[pos]
            pos += 1
        pos += has_after
        o_ref = refs[pos]
        pos += 1
        if has_norm:
            n_ref = refs[pos]
            pos += 1
        acc = refs[pos] if nk > 1 else None
        k = pl.program_id(2)
        part = _bdot(a_ref[...], b_ref[...], dims)

        def finish(total):
            if has_res:
                total = total + r_ref[...].astype(F32)
            o_ref[...] = total.astype(o_ref.dtype)
            if has_norm:
                r = lax.rsqrt(jnp.mean(total * total, axis=-1, keepdims=True) + EPS)
                n_ref[...] = (total * r * g_ref[...]).astype(n_ref.dtype)

        if nk == 1:
            finish(part)
        else:
            @pl.when(k == 0)
            def _():
                acc[...] = part

            @pl.when(k > 0)
            def _():
                acc[...] += part

            @pl.when(k == nk - 1)
            def _():
                finish(acc[...])

    in_specs = [pl.BlockSpec(a_blk, a_map), pl.BlockSpec(b_blk, b_map)]
    args = [a, b]
    if has_res:
        in_specs.append(pl.BlockSpec((tm, tn), lambda i, j, k: (i, j)))
        args.append(res)
    if has_norm:
        in_specs.append(pl.BlockSpec((1, N), lambda i, j, k: (0, 0)))
        args.append(norm_g)
    if has_after:
        in_specs.append(pl.BlockSpec(memory_space=pl.ANY))
        args.append(after)
    out_shape = jax.ShapeDtypeStruct((M, N), out_dtype)
    out_specs = pl.BlockSpec((tm, tn), lambda i, j, k: (i, j))
    if has_norm:
        out_shape = [out_shape, jax.ShapeDtypeStruct((M, N), BF16)]
        out_specs = [out_specs, pl.BlockSpec((tm, tn), lambda i, j, k: (i, j))]
    return pl.pallas_call(
        body, name=name, grid=(M // tm, N // tn, nk), in_specs=in_specs,
        out_specs=out_specs, out_shape=out_shape,
        scratch_shapes=[pltpu.VMEM((tm, tn), F32)] if nk > 1 else [],
        compiler_params=_cparams(("parallel", "parallel", "arbitrary")),
    )(*args)


def _rmsnorm_fwd(x, g, *, name, ts=512):
    S, D = x.shape
    ts = min(ts, S)

    def body(x_ref, g_ref, o_ref):
        xv = x_ref[...]
        r = lax.rsqrt(jnp.mean(xv * xv, axis=-1, keepdims=True) + EPS)
        o_ref[...] = (xv * r * g_ref[...]).astype(o_ref.dtype)

    return pl.pallas_call(
        body, name=name, grid=(S // ts,),
        in_specs=[pl.BlockSpec((ts, D), lambda i: (i, 0)), pl.BlockSpec((1, D), lambda i: (0, 0))],
        out_specs=pl.BlockSpec((ts, D), lambda i: (i, 0)),
        out_shape=jax.ShapeDtypeStruct((S, D), BF16),
        compiler_params=_cparams(("parallel",)),
    )(x, g)


def _rmsnorm_bwd(x, dxn, g, res, *, name, ts=512, emit_bf16=False):
    S, D = x.shape
    ts = min(ts, S)
    has_res = res is not None

    def body(*refs):
        refs = list(refs)
        dx16_ref = refs.pop() if emit_bf16 else None
        if has_res:
            x_ref, d_ref, g_ref, r_ref, dx_ref, dg_ref = refs
        else:
            x_ref, d_ref, g_ref, dx_ref, dg_ref = refs
        i = pl.program_id(0)
        xv = x_ref[...]
        dv = d_ref[...].astype(F32)
        r = lax.rsqrt(jnp.mean(xv * xv, axis=-1, keepdims=True) + EPS)
        gd = dv * g_ref[...]
        proj = jnp.mean(xv * gd, axis=-1, keepdims=True)
        dx = r * gd - xv * (r * r * r) * proj
        if has_res:
            dx = dx + r_ref[...]
        dx_ref[...] = dx
        if emit_bf16:
            dx16_ref[...] = dx.astype(BF16)
        part = jnp.sum(dv * xv * r, axis=0, keepdims=True)

        @pl.when(i == 0)
        def _():
            dg_ref[...] = part

        @pl.when(i > 0)
        def _():
            dg_ref[...] += part

    row = pl.BlockSpec((ts, D), lambda i: (i, 0))
    vec = pl.BlockSpec((1, D), lambda i: (0, 0))
    in_specs = [row, row, vec] + ([row] if has_res else [])
    args = [x, dxn, g] + ([res] if has_res else [])
    extra = emit_bf16 * [jax.ShapeDtypeStruct((S, D), BF16)]
    return pl.pallas_call(
        body, name=name, grid=(S // ts,), in_specs=in_specs, out_specs=[row, vec] + emit_bf16 * [row],
        out_shape=[jax.ShapeDtypeStruct((S, D), F32), jax.ShapeDtypeStruct((1, D), F32)] + extra,
        compiler_params=_cparams(("arbitrary",)),
    )(*args)


def _final_loss(x, target, g, *, name, ts=512):
    S, D = x.shape
    ts = min(ts, S)

    def body(x_ref, t_ref, g_ref, dx_ref, dg_ref, loss_ref, dx16_ref):
        i = pl.program_id(0)
        xv = x_ref[...]
        gv = g_ref[...]
        r = lax.rsqrt(jnp.mean(xv * xv, axis=-1, keepdims=True) + EPS)
        y = xv * r * gv
        err = y - t_ref[...]
        row_loss = jnp.mean(err * err, axis=-1, keepdims=True)
        lpart = 0.5 * jnp.sum(row_loss, axis=0, keepdims=True)
        dy = err * (1.0 / D)
        gd = dy * gv
        proj = jnp.mean(xv * gd, axis=-1, keepdims=True)
        dx = r * gd - xv * (r * r * r) * proj
        dx_ref[...] = dx
        dx16_ref[...] = dx.astype(BF16)
        part = jnp.sum(dy * xv * r, axis=0, keepdims=True)
        lfull = jnp.broadcast_to(lpart, loss_ref.shape)

        @pl.when(i == 0)
        def _():
            dg_ref[...] = part
            loss_ref[...] = lfull

        @pl.when(i > 0)
        def _():
            dg_ref[...] += part
            loss_ref[...] += lfull

    row = pl.BlockSpec((ts, D), lambda i: (i, 0))
    vec = pl.BlockSpec((1, D), lambda i: (0, 0))
    return pl.pallas_call(
        body, name=name, grid=(S // ts,), in_specs=[row, row, vec],
        out_specs=[row, vec, pl.BlockSpec((SUB, LANE), lambda i: (0, 0)), row],
        out_shape=[jax.ShapeDtypeStruct((S, D), F32), jax.ShapeDtypeStruct((1, D), F32),
                   jax.ShapeDtypeStruct((SUB, LANE), F32), jax.ShapeDtypeStruct((S, D), BF16)],
        compiler_params=_cparams(("arbitrary",)),
    )(x, target, g)


def _rope_table(pos_col, inv, *, name, ts=1024):
    S = pos_col.shape[0]
    ts = min(ts, S)
    half = inv.shape[1]

    def body(p_ref, inv_ref, c_ref, s_ref):
        ang = p_ref[...].astype(F32) * inv_ref[...]
        c_ref[...] = jnp.cos(ang)
        s_ref[...] = jnp.sin(ang)

    tab = pl.BlockSpec((ts, half), lambda i: (i, 0))
    return pl.pallas_call(
        body, name=name, grid=(S // ts,),
        in_specs=[pl.BlockSpec((ts, 1), lambda i: (i, 0)), pl.BlockSpec((1, half), lambda i: (0, 0))],
        out_specs=[tab, tab],
        out_shape=[jax.ShapeDtypeStruct((S, half), F32), jax.ShapeDtypeStruct((S, half), F32)],
        compiler_params=_cparams(("parallel",)),
    )(pos_col, inv)


def _ret_consts(C, log_g):
    ii = lax.broadcasted_iota(jnp.int32, (C, C), 0)
    jj = lax.broadcasted_iota(jnp.int32, (C, C), 1)
    diff = (ii - jj).astype(F32)
    intra = jnp.where(ii >= jj, jnp.exp(log_g * jnp.maximum(diff, 0.0)), 0.0)
    idx = lax.broadcasted_iota(jnp.int32, (C, 1), 0).astype(F32)
    qd = jnp.exp(log_g * (idx + 1.0))
    kd = jnp.exp(log_g * (C - 1.0 - idx))
    cd = math.exp(log_g * C)
    return intra, qd, kd, cd


def _rot(t, cs, sn):
    half = t.shape[-1] // 2
    t1, t2 = t[:, :half], t[:, half:]
    return jnp.concatenate([t1 * cs - t2 * sn, t1 * sn + t2 * cs], axis=-1)


def _unrot(d, cs, sn):
    half = d.shape[-1] // 2
    d1, d2 = d[:, :half], d[:, half:]
    return jnp.concatenate([d1 * cs + d2 * sn, d2 * cs - d1 * sn], axis=-1)


def _ret_fwd(h, cos, sin, ret_g, mix, *, name, ch=4):
    S = h.shape[0]
    R = ret_g.shape[1]
    H, C = RET_HEADS, RET_CHUNK
    Dh = R // H
    ts = ch * C
    assert S % ts == 0
    log_gs = [math.log(1.0 - 2.0 ** (-5.0 - hd)) for hd in range(H)]
    scale = Dh ** -0.5

    def body(x_ref, c_ref, s_ref, rg_ref, mix_in, ret_ref, st_ref, mix_ref, state):
        i = pl.program_id(0)

        @pl.when(i == 0)
        def _():
            state[...] = jnp.zeros_like(state)

        for c in range(ch):
            rows = pl.ds(c * C, C)
            cs, sn = c_ref[rows, :], s_ref[rows, :]
            for hd in range(H):
                intra, qd, kd, cd = _ret_consts(C, log_gs[hd])
                q = x_ref[rows, pl.ds(hd * Dh, Dh)]
                k = x_ref[rows, pl.ds(R + hd * Dh, Dh)]
                v = x_ref[rows, pl.ds(2 * R + hd * Dh, Dh)]
                g = x_ref[rows, pl.ds(3 * R + hd * Dh, Dh)]
                rq = _rot(q, cs, sn)
                rk = _rot(k, cs, sn) * scale
                st = state[hd]
                st_ref[c, hd] = st.astype(BF16)
                s_ = _bdot(rq, rk, NT) * intra
                ret = _bdot(s_, v, NN) + _bdot(rq * qd, st, NN)
                state[hd] = st * cd + _bdot(rk * kd, v, TN)
                ret_ref[rows, pl.ds(hd * Dh, Dh)] = ret
                rr = lax.rsqrt(jnp.mean(ret * ret, axis=-1, keepdims=True) + EPS)
                out = ret * rr * rg_ref[:, pl.ds(hd * Dh, Dh)] * (g * _sigmoid(g))
                mix_ref[rows, pl.ds(hd * Dh, Dh)] = out.astype(BF16)

    n_chunks = S // C
    return pl.pallas_call(
        body, name=name, grid=(S // ts,),
        in_specs=[pl.BlockSpec((ts, 4 * R), lambda i: (i, 0)),
                  pl.BlockSpec((ts, Dh // 2), lambda i: (i, 0)), pl.BlockSpec((ts, Dh // 2), lambda i: (i, 0)),
                  pl.BlockSpec((1, R), lambda i: (0, 0)), pl.BlockSpec(memory_space=pl.ANY)],
        out_specs=[pl.BlockSpec((ts, R), lambda i: (i, 0)),
                   pl.BlockSpec((ch, H, Dh, Dh), lambda i: (i, 0, 0, 0)),
                   pl.BlockSpec((ts, R), lambda i: (i, 0))],
        out_shape=[jax.ShapeDtypeStruct((S, R), F32), jax.ShapeDtypeStruct((n_chunks, H, Dh, Dh), BF16),
                   jax.ShapeDtypeStruct(mix.shape, mix.dtype)],
        scratch_shapes=[pltpu.VMEM((H, Dh, Dh), F32)],
        input_output_aliases={4: 2},
        compiler_params=_cparams(("arbitrary",)),
    )(h, cos, sin, ret_g, mix)


def _ret_bwd(h, cos, sin, ret_g, states, ret_raw, dmix, *, name, ch=4):
    S = h.shape[0]
    R = ret_g.shape[1]
    H, C = RET_HEADS, RET_CHUNK
    Dh = R // H
    ts = ch * C
    nb = S // ts
    log_gs = [math.log(1.0 - 2.0 ** (-5.0 - hd)) for hd in range(H)]
    scale = Dh ** -0.5

    def body(x_ref, c_ref, s_ref, rg_ref, st_ref, ret_ref, dm_ref, dh_ref, drg_ref, dstate):
        i = pl.program_id(0)

        @pl.when(i == 0)
        def _():
            dstate[...] = jnp.zeros_like(dstate)
            drg_ref[...] = jnp.zeros_like(drg_ref)

        for c in reversed(range(ch)):
            rows = pl.ds(c * C, C)
            cs, sn = c_ref[rows, :], s_ref[rows, :]
            for hd in range(H):
                intra, qd, kd, cd = _ret_consts(C, log_gs[hd])
                cols = pl.ds(hd * Dh, Dh)
                q = x_ref[rows, pl.ds(hd * Dh, Dh)]
                k = x_ref[rows, pl.ds(R + hd * Dh, Dh)]
                v = x_ref[rows, pl.ds(2 * R + hd * Dh, Dh)]
                g = x_ref[rows, pl.ds(3 * R + hd * Dh, Dh)]
                rq = _rot(q, cs, sn)
                rk = _rot(k, cs, sn) * scale
                ret = ret_ref[rows, cols]
                dm = dm_ref[rows, cols].astype(F32)
                rgv = rg_ref[:, cols]
                rr = lax.rsqrt(jnp.mean(ret * ret, axis=-1, keepdims=True) + EPS)
                retn = ret * rr
                sg = _sigmoid(g)
                silu = g * sg
                drg_ref[:, cols] += jnp.sum(dm * retn * silu, axis=0, keepdims=True)
                dg = dm * retn * rgv * (sg * (1.0 + g * (1.0 - sg)))
                dretn = dm * rgv * silu
                d_o = rr * dretn - ret * (rr * rr * rr) * jnp.mean(ret * dretn, axis=-1, keepdims=True)
                st = st_ref[c, hd]
                d_s = dstate[hd]
                a_ = _bdot(rq, rk, NT) * intra
                d_a = _bdot(d_o, v, NT) * intra
                d_qr = _bdot(d_a, rk, NN) + _bdot(d_o, st, NT) * qd
                d_kr = _bdot(d_a, rq, TN) + _bdot(v, d_s, NT) * kd
                d_v = _bdot(a_, d_o, TN) + _bdot(rk * kd, d_s, NN)
                dstate[hd] = d_s * cd + _bdot(rq * qd, d_o, TN)
                dh_ref[rows, pl.ds(hd * Dh, Dh)] = _unrot(d_qr, cs, sn).astype(BF16)
                dh_ref[rows, pl.ds(R + hd * Dh, Dh)] = (_unrot(d_kr, cs, sn) * scale).astype(BF16)
                dh_ref[rows, pl.ds(2 * R + hd * Dh, Dh)] = d_v.astype(BF16)
                dh_ref[rows, pl.ds(3 * R + hd * Dh, Dh)] = dg.astype(BF16)

    rb = lambda i: nb - 1 - i
    return pl.pallas_call(
        body, name=name, grid=(nb,),
        in_specs=[pl.BlockSpec((ts, 4 * R), lambda i: (rb(i), 0)),
                  pl.BlockSpec((ts, Dh // 2), lambda i: (rb(i), 0)), pl.BlockSpec((ts, Dh // 2), lambda i: (rb(i), 0)),
                  pl.BlockSpec((1, R), lambda i: (0, 0)),
                  pl.BlockSpec((ch, H, Dh, Dh), lambda i: (rb(i), 0, 0, 0)),
                  pl.BlockSpec((ts, R), lambda i: (rb(i), 0)),
                  pl.BlockSpec((ts, R), lambda i: (rb(i), 0))],
        out_specs=[pl.BlockSpec((ts, 4 * R), lambda i: (rb(i), 0)), pl.BlockSpec((1, R), lambda i: (0, 0))],
        out_shape=[jax.ShapeDtypeStruct((S, 6 * R), BF16), jax.ShapeDtypeStruct((1, R), F32)],
        scratch_shapes=[pltpu.VMEM((H, Dh, Dh), F32)],
        compiler_params=_cparams(("arbitrary",)),
    )(h, cos, sin, ret_g, states, ret_raw, dmix)


GELU_C = math.sqrt(2.0 / math.pi)
GELU_A = 0.044715


def _gelu_parts(y):
    t = jnp.tanh(GELU_C * (y + GELU_A * y * y * y))
    val = 0.5 * y * (1.0 + t)
    grad = 0.5 * (1.0 + t) + 0.5 * y * (1.0 - t * t) * GELU_C * (1.0 + 3.0 * GELU_A * y * y)
    return val, grad


def _neg_expm1(x):
    series = -x * (1.0 + x * (1.0 / 2.0) * (1.0 + x * (1.0 / 3.0) * (1.0 + x * (1.0 / 4.0) * (
        1.0 + x * (1.0 / 5.0) * (1.0 + x * (1.0 / 6.0) * (1.0 + x * (1.0 / 7.0)))))))
    return jnp.where(x > -0.35, series, 1.0 - jnp.exp(x))


def _log_sigmoid(x):
    return jnp.minimum(x, 0.0) - jnp.log1p(jnp.exp(-jnp.abs(x)))


def _lru_gates(uc, wa_ref, ba_ref, wx_ref, bx_ref):
    nbk = wa_ref.shape[0]
    bd = wa_ref.shape[1]
    rs, gs = [], []
    for n in range(nbk):
        ucn = uc[:, n * bd:(n + 1) * bd]
        rs.append(_sigmoid(_bdot(ucn, wa_ref[n], NN) + ba_ref[:, pl.ds(n * bd, bd)]))
        gs.append(_sigmoid(_bdot(ucn, wx_ref[n], NN) + bx_ref[:, pl.ds(n * bd, bd)]))
    return jnp.concatenate(rs, axis=-1), jnp.concatenate(gs, axis=-1)


def _lru_fwd(h, conv_w, conv_b, wa, ba, wx, bx, lam, *, name, ts=256):
    S = h.shape[0]
    W = lam.shape[1]
    K = conv_w.shape[0]
    ts = min(ts, S)

    def body(u_ref, y_ref, cw_ref, cb_ref, wa_ref, ba_ref, wx_ref, bx_ref, lam_ref, hl_ref, mix_ref, tail, hlast):
        i = pl.program_id(0)

        @pl.when(i == 0)
        def _():
            tail[...] = jnp.zeros_like(tail)
            hlast[...] = jnp.zeros_like(hlast)

        u = u_ref[...]
        tl = tail[...]
        uc = cb_ref[...] + cw_ref[K - 1:K, :] * u
        for k in range(K - 1):
            uc = uc + cw_ref[k:k + 1, :] * _shift_down(u, tl, K - 1 - k)
        tail[...] = u[ts - SUB:ts]
        r, ig = _lru_gates(uc, wa_ref, ba_ref, wx_ref, bx_ref)
        log_a = LRU_C * r * _log_sigmoid(lam_ref[...])
        a = jnp.exp(log_a)
        b = jnp.sqrt(_neg_expm1(2.0 * log_a)) * (ig * uc)
        in_tile = _row_iota((ts, W)) & (SUB - 1)
        d = 1
        while d < SUB:
            a_s = jnp.where(in_tile < d, 1.0, pltpu.roll(a, d, 0))
            b_s = jnp.where(in_tile < d, 0.0, pltpu.roll(b, d, 0))
            b = a * b_s + b
            a = a * a_s
            d *= 2
        before = hlast[SUB - 1:SUB, :]
        for k in range(ts // SUB):
            tile = slice(k * SUB, (k + 1) * SUB)
            h_tile = a[tile] * before + b[tile]
            hl_ref[tile, :] = h_tile
            before = h_tile[SUB - 1:SUB, :]
        hlast[...] = hl_ref[ts - SUB:ts, :]
        gy, _ = _gelu_parts(y_ref[...])
        mix_ref[...] = (hl_ref[...] * gy).astype(BF16)

    full = lambda shape: pl.BlockSpec(shape, lambda i: tuple(0 for _ in shape))
    return pl.pallas_call(
        body, name=name, grid=(S // ts,),
        in_specs=[pl.BlockSpec((ts, W), lambda i: (i, 4)), pl.BlockSpec((ts, W), lambda i: (i, 5)),
                  full(conv_w.shape), full(conv_b.shape), full(wa.shape), full(ba.shape), full(wx.shape),
                  full(bx.shape), full(lam.shape)],
        out_specs=[pl.BlockSpec((ts, W), lambda i: (i, 0)), pl.BlockSpec((ts, W), lambda i: (i, 1))],
        out_shape=[jax.ShapeDtypeStruct((S, W), F32), jax.ShapeDtypeStruct((S, 2 * W), BF16)],
        scratch_shapes=[pltpu.VMEM((SUB, W), F32), pltpu.VMEM((SUB, W), F32)],
        compiler_params=_cparams(("arbitrary",)),
    )(h, h, conv_w, conv_b, wa, ba, wx, bx, lam)


def _lru_bwd(h, hl, dmix, dh, conv_w, conv_b, wa, ba, wx, bx, lam, *, name, ts=256):
    S = h.shape[0]
    W = lam.shape[1]
    K = conv_w.shape[0]
    nbk, bd = wa.shape[0], wa.shape[1]
    ts = min(ts, S)
    nb = S // ts
    t8 = ts // SUB

    def body(u_ref, y_ref, uh_ref, hl_ref, hh_ref, dm_ref, cw_ref, cb_ref, wa_ref, ba_ref, wx_ref, bx_ref, lam_ref,
             dh_in, dh_ref, dcw_ref, dcb_ref, dwa_ref, dba_ref, dwx_ref, dbx_ref, dlam_ref, carry, head, lam_buf):
        i = pl.program_id(0)
        blk = nb - 1 - i

        @pl.when(i == 0)
        def _():
            carry[...] = jnp.zeros_like(carry)
            head[...] = jnp.zeros_like(head)
            for ref in (dcw_ref, dcb_ref, dwa_ref, dba_ref, dwx_ref, dbx_ref, dlam_ref):
                ref[...] = jnp.zeros_like(ref)

        inside = (blk > 0).astype(F32)
        u = u_ref[...]
        tl = uh_ref[...] * inside
        sh = [_shift_down(u, tl, K - 1 - k) for k in range(K)]
        uc = cb_ref[...]
        for k in range(K):
            uc = uc + cw_ref[k:k + 1, :] * sh[k]
        r, ig = _lru_gates(uc, wa_ref, ba_ref, wx_ref, bx_ref)
        lam_v = lam_ref[...]
        ls = _log_sigmoid(lam_v)
        log_a = LRU_C * r * ls
        a = jnp.exp(log_a)
        mult = jnp.sqrt(_neg_expm1(2.0 * log_a))
        hcur = hl_ref[...]
        hprev = _shift_down(hcur, hh_ref[...] * inside, 1)
        gy, dgy = _gelu_parts(y_ref[...])
        dm = dm_ref[...].astype(F32)
        d_y = dm * hcur * dgy
        rid = _row_iota((ts, W))
        bq = dm * gy + jnp.where(rid == ts - 1, carry[0:1, :], 0.0)
        aq = jnp.where(rid == ts - 1, 0.0, pltpu.roll(a, ts - 1, 0))
        in_tile = rid & (SUB - 1)
        d = 1
        while d < SUB:
            a_s = jnp.where(in_tile >= SUB - d, 1.0, pltpu.roll(aq, ts - d, 0))
            b_s = jnp.where(in_tile >= SUB - d, 0.0, pltpu.roll(bq, ts - d, 0))
            bq = bq + aq * b_s
            aq = aq * a_s
            d *= 2
        after_row = jnp.zeros((1, W), F32)
        for k in reversed(range(ts // SUB)):
            tile = slice(k * SUB, (k + 1) * SUB)
            lam_tile = aq[tile] * after_row + bq[tile]
            lam_buf[tile, :] = lam_tile
            after_row = lam_tile[0:1, :]
        lam_t = lam_buf[...]
        carry[...] = (a * lam_t)[0:SUB]
        d_a = lam_t * hprev
        d_mult = lam_t * (ig * uc)
        d_i = lam_t * mult * uc
        d_uc = lam_t * mult * ig
        d_log_a = d_a * a - d_mult * (a * a) / mult
        d_r = d_log_a * (LRU_C * ls)
        dlam_ref[...] += jnp.sum(d_log_a * (LRU_C * r), axis=0, keepdims=True) * _sigmoid(-lam_v)
        d_pr = d_r * r * (1.0 - r)
        d_pi = d_i * ig * (1.0 - ig)
        dba_ref[...] += jnp.sum(d_pr, axis=0, keepdims=True)
        dbx_ref[...] += jnp.sum(d_pi, axis=0, keepdims=True)
        extra = []
        for n in range(nbk):
            sl = slice(n * bd, (n + 1) * bd)
            ucn = uc[:, sl]
            dwa_ref[n] += _bdot(ucn, d_pr[:, sl], TN)
            dwx_ref[n] += _bdot(ucn, d_pi[:, sl], TN)
            extra.append(_bdot(d_pr[:, sl], wa_ref[n], NT) + _bdot(d_pi[:, sl], wx_ref[n], NT))
        d_uc = d_uc + jnp.concatenate(extra, axis=-1)
        dcb_ref[...] += jnp.sum(d_uc, axis=0, keepdims=True)
        rid8 = _row_iota((SUB, W))
        dcw = jnp.zeros((SUB, W), F32)
        for k in range(K):
            dcw = dcw + jnp.where(rid8 == k, jnp.sum(d_uc * sh[k], axis=0, keepdims=True), 0.0)
        dcw_ref[...] += dcw
        hd = head[...]
        d_u = cw_ref[K - 1:K, :] * d_uc
        for j in range(1, K):
            d_u = d_u + cw_ref[K - 1 - j:K - j, :] * _shift_up(d_uc, hd, j)
        head[...] = d_uc[0:SUB]
        dh_ref[:, 0:W] = d_u.astype(BF16)
        dh_ref[:, W:2 * W] = d_y.astype(BF16)

    rb = lambda i: nb - 1 - i
    prev8 = lambda i: jnp.maximum(rb(i) * t8 - 1, 0)
    full = lambda shape: pl.BlockSpec(shape, lambda i: tuple(0 for _ in shape))
    small = [jax.ShapeDtypeStruct((SUB, W), F32), jax.ShapeDtypeStruct((1, W), F32),
             jax.ShapeDtypeStruct(wa.shape, F32), jax.ShapeDtypeStruct((1, W), F32),
             jax.ShapeDtypeStruct(wx.shape, F32), jax.ShapeDtypeStruct((1, W), F32),
             jax.ShapeDtypeStruct((1, W), F32)]
    return pl.pallas_call(
        body, name=name, grid=(nb,),
        in_specs=[pl.BlockSpec((ts, W), lambda i: (rb(i), 4)), pl.BlockSpec((ts, W), lambda i: (rb(i), 5)),
                  pl.BlockSpec((SUB, W), lambda i: (prev8(i), 4)),
                  pl.BlockSpec((ts, W), lambda i: (rb(i), 0)), pl.BlockSpec((SUB, W), lambda i: (prev8(i), 0)),
                  pl.BlockSpec((ts, W), lambda i: (rb(i), 1)),
                  full(conv_w.shape), full(conv_b.shape), full(wa.shape), full(ba.shape), full(wx.shape),
                  full(bx.shape), full(lam.shape), pl.BlockSpec(memory_space=pl.ANY)],
        out_specs=[pl.BlockSpec((ts, 2 * W), lambda i: (rb(i), 2))] + [full(s.shape) for s in small],
        out_shape=[jax.ShapeDtypeStruct(dh.shape, dh.dtype)] + small,
        scratch_shapes=[pltpu.VMEM((SUB, W), F32), pltpu.VMEM((SUB, W), F32), pltpu.VMEM((ts, W), F32)],
        input_output_aliases={13: 0},
        compiler_params=_cparams(("arbitrary",)),
    )(h, h, h, hl, hl, dmix, conv_w, conv_b, wa, ba, wx, bx, lam, dh)


def _xattn_fwd(q, km, vm, *, name, ts=1024):
    S, D = q.shape
    M = km.shape[0]
    H = XA_HEADS
    Dh = D // H
    ts = min(ts, S)
    scale = Dh ** -0.5

    def body(q_ref, k_ref, v_ref, o_ref):
        for hd in range(H):
            cols = pl.ds(hd * Dh, Dh)
            s = _bdot(q_ref[:, cols], k_ref[:, cols], NT) * scale
            s = s - jnp.max(s, axis=-1, keepdims=True)
            e = jnp.exp(s)
            p = e / jnp.sum(e, axis=-1, keepdims=True)
            o_ref[:, cols] = _bdot(p, v_ref[:, cols], NN).astype(o_ref.dtype)

    return pl.pallas_call(
        body, name=name, grid=(S // ts,),
        in_specs=[pl.BlockSpec((ts, D), lambda i: (i, 0)), pl.BlockSpec((M, D), lambda i: (0, 0)),
                  pl.BlockSpec((M, D), lambda i: (0, 0))],
        out_specs=pl.BlockSpec((ts, D), lambda i: (i, 0)),
        out_shape=jax.ShapeDtypeStruct((S, D), BF16),
        compiler_params=_cparams(("parallel",)),
    )(q, km, vm)


def _xattn_bwd(q, km, vm, d_o, *, name, ts=1024):
    S, D = q.shape
    M = km.shape[0]
    H = XA_HEADS
    Dh = D // H
    ts = min(ts, S)
    scale = Dh ** -0.5

    def body(q_ref, k_ref, v_ref, do_ref, dq_ref, dk_ref, dv_ref):
        i = pl.program_id(0)

        @pl.when(i == 0)
        def _():
            dk_ref[...] = jnp.zeros_like(dk_ref)
            dv_ref[...] = jnp.zeros_like(dv_ref)

        for hd in range(H):
            cols = pl.ds(hd * Dh, Dh)
            qh, kh, vh, doh = q_ref[:, cols], k_ref[:, cols], v_ref[:, cols], do_ref[:, cols]
            s = _bdot(qh, kh, NT) * scale
            s = s - jnp.max(s, axis=-1, keepdims=True)
            e = jnp.exp(s)
            p = e / jnp.sum(e, axis=-1, keepdims=True)
            dp = _bdot(doh, vh, NT)
            ds = p * (dp - jnp.sum(dp * p, axis=-1, keepdims=True)) * scale
            dq_ref[:, cols] = _bdot(ds, kh, NN).astype(dq_ref.dtype)
            dk_ref[:, cols] += _bdot(ds, qh, TN)
            dv_ref[:, cols] += _bdot(p, doh, TN)

    row = pl.BlockSpec((ts, D), lambda i: (i, 0))
    mem = pl.BlockSpec((M, D), lambda i: (0, 0))
    return pl.pallas_call(
        body, name=name, grid=(S // ts,), in_specs=[row, mem, mem, row], out_specs=[row, mem, mem],
        out_shape=[jax.ShapeDtypeStruct((S, D), BF16), jax.ShapeDtypeStruct((M, D), F32),
                   jax.ShapeDtypeStruct((M, D), F32)],
        compiler_params=_cparams(("arbitrary",)),
    )(q, km, vm, d_o)


def _conv_rows(v, tail, cw_ref, cb_ref):
    K = cw_ref.shape[0]
    sh = [_shift_down(v, tail, K - 1 - k) for k in range(K)]
    out = cb_ref[...]
    for k in range(K):
        out = out + cw_ref[k:k + 1, :] * sh[k]
    return out, sh


FFN_SUB = 256


def _ffn_up_gate(xn, w_up, cw, cb, *, name, tm=1024, tn=512):
    S, D = xn.shape
    F2 = w_up.shape[1]
    F = F2 // 2
    tm, tn = min(tm, S), min(tn, F)
    sub = min(FFN_SUB, tm)
    nj = F // tn
    K = cw.shape[0]

    def body(x_ref, wa_ref, wb_ref, cwa_ref, cwb_ref, cba_ref, cbb_ref, act_ref, ha_ref, hb_ref, ac_ref, bc_ref, ta, tb):
        i = pl.program_id(1)

        @pl.when(i == 0)
        def _():
            ta[...] = jnp.zeros_like(ta)
            tb[...] = jnp.zeros_like(tb)

        tail_a, tail_b = ta[...], tb[...]
        for s in range(tm // sub):
            rows = pl.ds(s * sub, sub)
            xs = x_ref[rows, :]
            ha = _bdot(xs, wa_ref[...], NN)
            hb = _bdot(xs, wb_ref[...], NN)
            ac, _ = _conv_rows(ha, tail_a, cwa_ref, cba_ref)
            bc, _ = _conv_rows(hb, tail_b, cwb_ref, cbb_ref)
            tail_a, tail_b = ha[sub - SUB:sub], hb[sub - SUB:sub]
            ha_ref[rows, :] = ha
            hb_ref[rows, :] = hb
            ac_ref[rows, :] = ac
            bc_ref[rows, :] = bc
            act_ref[rows, :] = (ac * _sigmoid(ac) * bc).astype(act_ref.dtype)
        ta[...] = tail_a
        tb[...] = tail_b

    blk = pl.BlockSpec((tm, tn), lambda j, i: (i, j))
    return pl.pallas_call(
        body, name=name, grid=(nj, S // tm),
        in_specs=[pl.BlockSpec((tm, D), lambda j, i: (i, 0)),
                  pl.BlockSpec((D, tn), lambda j, i: (0, j)), pl.BlockSpec((D, tn), lambda j, i: (0, j + nj)),
                  pl.BlockSpec((K, tn), lambda j, i: (0, j)), pl.BlockSpec((K, tn), lambda j, i: (0, j + nj)),
                  pl.BlockSpec((1, tn), lambda j, i: (0, j)), pl.BlockSpec((1, tn), lambda j, i: (0, j + nj))],
        out_specs=[blk] * 5,
        out_shape=[jax.ShapeDtypeStruct((S, F), BF16)] + [jax.ShapeDtypeStruct((S, F), F32)] * 4,
        scratch_shapes=[pltpu.VMEM((SUB, tn), F32), pltpu.VMEM((SUB, tn), F32)],
        compiler_params=_cparams(("parallel", "arbitrary")),
    )(xn, w_up, w_up, cw, cw, cb, cb)


def _ffn_bwd(dx, w_down, hh_a, hh_b, c_a, c_b, act, xn, cw, *, name, tm=1024, tn=256):
    S, D = dx.shape
    F = hh_a.shape[1]
    tm, tn = min(tm, S), min(tn, F)
    sub = min(FFN_SUB, tm)
    nj = F // tn
    nb = S // tm
    K = cw.shape[0]

    def body(dx_ref, wd_ref, a_ref, b_ref, ac_ref, bc_ref, act_ref, xn_ref, cwa_ref, cwb_ref,
             da_ref, db_ref, ga_ref, gb_ref, dwd_ref, dwu_ref, ha, hb, acc_d, acc_a, acc_b):
        i = pl.program_id(1)

        @pl.when(i == 0)
        def _():
            for ref in (ha, hb, ga_ref, gb_ref, acc_d, acc_a, acc_b):
                ref[...] = jnp.zeros_like(ref)

        rid8 = _row_iota((SUB, tn))
        heads = [ha[...], hb[...]]
        gsums = [jnp.zeros((SUB, tn), F32), jnp.zeros((SUB, tn), F32)]
        for s in reversed(range(tm // sub)):
            rows = pl.ds(s * sub, sub)
            dv = _bdot(dx_ref[rows, :], wd_ref[...], NT)
            ac, bc = ac_ref[rows, :], bc_ref[rows, :]
            sg = _sigmoid(ac)
            d_bc = dv * ac * sg
            d_ac = dv * bc * sg * (1.0 + ac * (1.0 - sg))
            for which, (d_c, h_ref, cw_ref, o_ref) in enumerate(((d_ac, a_ref, cwa_ref, da_ref),
                                                                 (d_bc, b_ref, cwb_ref, db_ref))):
                ahead = [d_c] + [_shift_up(d_c, heads[which], j) for j in range(1, K)]
                heads[which] = d_c[0:SUB]
                d_in = cw_ref[K - 1:K, :] * d_c
                for j in range(1, K):
                    d_in = d_in + cw_ref[K - 1 - j:K - j, :] * ahead[j]
                o_ref[rows, :] = d_in.astype(o_ref.dtype)
                hv = h_ref[rows, :]
                gsum = gsums[which] + jnp.where(rid8 == K, jnp.sum(d_c, axis=0, keepdims=True), 0.0)
                for k in range(K):
                    gsum = gsum + jnp.where(rid8 == k, jnp.sum(ahead[K - 1 - k] * hv, axis=0, keepdims=True), 0.0)
                gsums[which] = gsum
        ha[...], hb[...] = heads
        ga_ref[...] += gsums[0]
        gb_ref[...] += gsums[1]
        acc_d[...] += _bdot(act_ref[...], dx_ref[...], TN)
        acc_a[...] += _bdot(xn_ref[...], da_ref[...], TN)
        acc_b[...] += _bdot(xn_ref[...], db_ref[...], TN)

        @pl.when(i == nb - 1)
        def _():
            dwd_ref[...] = acc_d[...].astype(dwd_ref.dtype)
            dwu_ref[0] = acc_a[...].astype(dwu_ref.dtype)
            dwu_ref[1] = acc_b[...].astype(dwu_ref.dtype)

    rb = lambda i: nb - 1 - i
    blk = pl.BlockSpec((tm, tn), lambda j, i: (rb(i), j))
    acc = pl.BlockSpec((SUB, tn), lambda j, i: (0, j))
    rows_d = pl.BlockSpec((tm, D), lambda j, i: (rb(i), 0))
    return pl.pallas_call(
        body, name=name, grid=(nj, nb),
        in_specs=[rows_d, pl.BlockSpec((tn, D), lambda j, i: (j, 0)), blk, blk, blk, blk, blk, rows_d,
                  pl.BlockSpec((K, tn), lambda j, i: (0, j)), pl.BlockSpec((K, tn), lambda j, i: (0, j + nj))],
        out_specs=[blk, blk, acc, acc, pl.BlockSpec((tn, D), lambda j, i: (j, 0)),
                   pl.BlockSpec((2, D, tn), lambda j, i: (0, 0, j))],
        out_shape=[jax.ShapeDtypeStruct((S, F), BF16), jax.ShapeDtypeStruct((S, F), BF16),
                   jax.ShapeDtypeStruct((SUB, F), F32), jax.ShapeDtypeStruct((SUB, F), F32),
                   jax.ShapeDtypeStruct((F, D), BF16), jax.ShapeDtypeStruct((2, D, F), BF16)],
        scratch_shapes=[pltpu.VMEM((SUB, tn), F32), pltpu.VMEM((SUB, tn), F32), pltpu.VMEM((tn, D), F32),
                        pltpu.VMEM((D, tn), F32), pltpu.VMEM((D, tn), F32)],
        compiler_params=_cparams(("parallel", "arbitrary")),
    )(dx, w_down, hh_a, hh_b, c_a, c_b, act, xn, cw, cw)


ADAM_BLOCK_ELEMS = 256 * 1024


def _adamw(w, m, v, parts, *, name):
    R, C = w.shape
    n = parts.shape[0]
    tr = R
    for cand in (1024, 512, 256, 128, 64, 32, 16):
        if R % cand == 0 and cand * C <= ADAM_BLOCK_ELEMS:
            tr = cand
            break
    c1 = 1.0 - ADAM_B1 ** ADAM_STEP
    c2 = 1.0 - ADAM_B2 ** ADAM_STEP

    def body(w_ref, m_ref, v_ref, p_ref, g_ref, d_ref, nm_ref, nv_ref):
        g = p_ref[0].astype(F32)
        for k in range(1, n):
            g = g + p_ref[k].astype(F32)
        m_new = ADAM_B1 * m_ref[...] + (1.0 - ADAM_B1) * g
        v_new = ADAM_B2 * v_ref[...] + (1.0 - ADAM_B2) * (g * g)
        m_hat = m_new / c1
        v_hat = v_new / c2
        g_ref[...] = g
        d_ref[...] = -ADAM_LR * (m_hat / (jnp.sqrt(v_hat) + ADAM_EPS) + ADAM_WD * w_ref[...])
        nm_ref[...] = m_new
        nv_ref[...] = v_new

    blk = pl.BlockSpec((tr, C), lambda i: (i, 0))
    sds = jax.ShapeDtypeStruct((R, C), F32)
    return pl.pallas_call(
        body, name=name, grid=(R // tr,),
        in_specs=[blk, blk, blk, pl.BlockSpec((n, tr, C), lambda i: (0, i, 0))],
        out_specs=[blk, blk, blk, blk], out_shape=[sds, sds, sds, sds],
        compiler_params=_cparams(("parallel",)),
    )(w, m, v, parts)


def _mesh_place():
    x, y, c = lax.axis_index("x"), lax.axis_index("y"), lax.axis_index("c")
    others = [(1 - x, y), (x, 1 - y), (1 - x, 1 - y)]
    return x, y, c, others


HBM_SPEC = pl.BlockSpec(memory_space=pltpu.HBM)
SEM_SPEC = pl.BlockSpec(memory_space=pltpu.SEMAPHORE)
ANY_SPEC = pl.BlockSpec(memory_space=pl.ANY)
EFFECT = pltpu.SideEffectType.DATAFLOW_SIDE_EFFECTING


def _in_hbm(a):
    return pltpu.with_memory_space_constraint(a, pltpu.HBM)


def _split_start(srcs, lands, copies, n_cp, *, name):
    n_s, n_l = len(srcs), len(lands)

    def body(*refs):
        src_refs, land_refs = refs[:n_s], refs[n_s:n_s + n_l]
        ssem, rsem = refs[n_s + n_l], refs[n_s + n_l + 1]
        token = refs[-1]
        for outgoing, _ in copies(src_refs, land_refs, ssem, rsem):
            outgoing.start()
        token[...] = jnp.zeros_like(token)

    outs = pl.pallas_call(
        body, name=name,
        out_shape=(pltpu.SemaphoreType.DMA((n_cp,)), pltpu.SemaphoreType.DMA((n_cp,)),
                   *[pltpu.HBM(a.shape, a.dtype) for a in srcs], *[pltpu.HBM(a.shape, a.dtype) for a in lands],
                   jax.ShapeDtypeStruct((SUB, LANE), F32)),
        in_specs=[HBM_SPEC] * (n_s + n_l),
        out_specs=(SEM_SPEC, SEM_SPEC, *[HBM_SPEC] * (n_s + n_l), pl.BlockSpec(memory_space=pltpu.VMEM)),
        input_output_aliases={i: 2 + i for i in range(n_s + n_l)},
        compiler_params=pltpu.CompilerParams(has_side_effects=EFFECT),
    )(*[_in_hbm(a) for a in srcs], *[_in_hbm(a) for a in lands])
    ssem, rsem = outs[0], outs[1]
    return ssem, rsem, list(outs[2:2 + n_s]), list(outs[2 + n_s:2 + n_s + n_l]), outs[-1]


def _split_wait(srcs, lands, ssem, rsem, after, copies, *, name):
    n_s, n_l = len(srcs), len(lands)

    def body(*refs):
        src_refs, land_refs = refs[:n_s], refs[n_s:n_s + n_l]
        s_ref, r_ref = refs[n_s + n_l], refs[n_s + n_l + 1]
        for outgoing, incoming in copies(src_refs, land_refs, s_ref, r_ref):
            outgoing.wait_send()
            incoming.wait_recv()

    outs = pl.pallas_call(
        body, name=name,
        out_shape=(*[pltpu.HBM(a.shape, a.dtype) for a in srcs], *[pltpu.HBM(a.shape, a.dtype) for a in lands]),
        in_specs=[HBM_SPEC] * (n_s + n_l) + [SEM_SPEC, SEM_SPEC, ANY_SPEC], out_specs=[HBM_SPEC] * (n_s + n_l),
        input_output_aliases={i: i for i in range(n_s + n_l)},
        compiler_params=pltpu.CompilerParams(has_side_effects=EFFECT),
    )(*srcs, *lands, ssem, rsem, after)
    return list(outs[:n_s]), list(outs[n_s:])


PLACE_BLOCK_ELEMS = 512 * 1024


def _place_rows(r, w):
    return _div_tile(r, max(16, PLACE_BLOCK_ELEMS // w), 16)


def _cast_place(shard, chip, axis, after, *, name):
    r, w = shard.shape
    tr = _place_rows(r, w)
    nb = r // tr
    full = (r * N_CHIP, w) if axis == 0 else (r, w * N_CHIP)
    has_after = after is not None

    def body(chip_ref, s_ref, *rest):
        rest[-1][...] = s_ref[...].astype(BF16)

    out_map = (lambda i, ch: (ch[0] * nb + i, 0)) if axis == 0 else (lambda i, ch: (i, ch[0]))
    grid_spec = pltpu.PrefetchScalarGridSpec(
        num_scalar_prefetch=1, grid=(nb,),
        in_specs=[pl.BlockSpec((tr, w), lambda i, ch: (i, 0))] + has_after * [ANY_SPEC],
        out_specs=pl.BlockSpec((tr, w), out_map))
    return pl.pallas_call(body, name=name, grid_spec=grid_spec, out_shape=jax.ShapeDtypeStruct(full, BF16),
                          compiler_params=_cparams(("parallel",)))(chip, shard, *(has_after * [after]))


def _grad_shard_shape(g, axis):
    if g.ndim == 3:
        return g.shape[1], 2 * g.shape[2] // N_CHIP
    return (g.shape[0] // N_CHIP, g.shape[1]) if axis == 0 else (g.shape[0], g.shape[1] // N_CHIP)


def _slot_place(g, ids, axis, *, name):
    r, w = _grad_shard_shape(g, axis)
    tr = _place_rows(r, w)
    nb = r // tr

    def body(ids_ref, g_ref, o_ref):
        o_ref[...] = g_ref[...]

    if g.ndim == 3:
        in_spec = pl.BlockSpec((None, tr, w), lambda i, ids_: (ids_[0] // 2, i, ids_[0] % 2))
    elif axis == 0:
        in_spec = pl.BlockSpec((tr, w), lambda i, ids_: (ids_[0] * nb + i, 0))
    else:
        in_spec = pl.BlockSpec((tr, w), lambda i, ids_: (i, ids_[0]))
    grid_spec = pltpu.PrefetchScalarGridSpec(
        num_scalar_prefetch=1, grid=(nb,), in_specs=[in_spec],
        out_specs=pl.BlockSpec((None, tr, w), lambda i, ids_: (ids_[1], i, 0)))
    return pl.pallas_call(body, name=name, grid_spec=grid_spec, out_shape=jax.ShapeDtypeStruct((N_DEV, r, w), g.dtype),
                          compiler_params=_cparams(("parallel",)))(ids, g)


class _WeightGather:
    def __init__(self, placed, shard_shapes, axes, splits, tag):
        self.placed, self.shard_shapes, self.axes, self.splits, self.tag = list(placed), shard_shapes, axes, splits, tag
        self.n = len(placed)

    def _region(self, land_refs, it, chip, half):
        r, w = self.shard_shapes[it]
        by_rows = self.axes[it] == 0
        if self.splits[it] and half is not None:
            rows = pl.ds(pl.multiple_of(half * (r // 2) + (chip * r if by_rows else 0), 16), r // 2)
        else:
            rows = pl.ds(chip * r if by_rows else 0, r)
        cols = pl.ds(0, w) if by_rows else pl.ds(pl.multiple_of(chip * w, LANE), w)
        return land_refs[it].at[rows, cols]

    def _ici(self, src_refs, land_refs, ssem, rsem):
        x, y, c, others = _mesh_place()
        pairs = []
        for it in range(self.n):
            for j, chip in enumerate(others):
                def mk(chip_from, it=it, j=j, chip=chip):
                    return pltpu.make_async_remote_copy(
                        src_ref=self._region(land_refs, it, 2 * x + y, c), dst_ref=self._region(land_refs, it, chip_from, c),
                        send_sem=ssem.at[3 * it + j], recv_sem=rsem.at[3 * it + j], device_id=(*chip, c),
                        device_id_type=MESH)
                pairs.append((mk(2 * x + y), mk(2 * chip[0] + chip[1])))
        return pairs

    def start(self):
        self.ssem, self.rsem, _, self.lands, token = _split_start(
            [], self.placed, self._ici, 3 * self.n, name="gather_start_" + self.tag)
        return token

    def _d2d(self, src_refs, land_refs, ssem, rsem):
        x, y, c, others = _mesh_place()
        pairs = []
        for it in range(self.n):
            if self.splits[it]:
                for chip in others:
                    def mk(half, it=it, chip=chip, k=len(pairs)):
                        reg = self._region(land_refs, it, 2 * chip[0] + chip[1], half)
                        return pltpu.make_async_remote_copy(src_ref=reg, dst_ref=reg, send_sem=ssem.at[k], recv_sem=rsem.at[k],
                                                            device_id=(x, y, 1 - c), device_id_type=MESH)
                    pairs.append((mk(c), mk(1 - c)))
        return pairs

    def forward(self, after):
        _, lands = _split_wait([], self.lands, self.ssem, self.rsem, after, self._ici,
                               name="gather_wait_" + self.tag)
        self.fsem, self.frsem, _, self.lands, token = _split_start(
            [], lands, self._d2d, 3 * sum(self.splits), name="gather_fwd_" + self.tag)
        return token

    def finish_forward(self, after):
        _, lands = _split_wait([], self.lands, self.fsem, self.frsem, after, self._d2d,
                               name="gather_fwd_wait_" + self.tag)
        return lands

    def finish(self, after):
        _, lands = _split_wait([], self.lands, self.ssem, self.rsem, after, self._ici,
                               name="gather_wait_" + self.tag)
        n = self.n
        n_fwd = 3 * sum(self.splits)
        if n_fwd == 0:
            return lands

        def body(*refs):
            out_refs = refs[n:2 * n]
            fsend, frecv = refs[2 * n:]
            x, y, c, others = _mesh_place()
            sibling = (x, y, 1 - c)

            def fwd(it, slot, chip, half):
                reg = self._region(out_refs, it, 2 * chip[0] + chip[1], half)
                return pltpu.make_async_remote_copy(src_ref=reg, dst_ref=reg, send_sem=fsend.at[slot],
                                                    recv_sem=frecv.at[slot], device_id=sibling, device_id_type=MESH)

            sends, recvs = [], []
            for it in range(n):
                if self.splits[it]:
                    for chip in others:
                        sends.append(fwd(it, len(sends), chip, c))
                        recvs.append(fwd(it, len(recvs), chip, 1 - c))
            for cp in sends:
                cp.start()
            for cp in recvs:
                cp.wait_recv()
            for cp in sends:
                cp.wait_send()

        fulls = pl.pallas_call(
            body, name="gather_d2d_" + self.tag, in_specs=[ANY_SPEC] * n, out_specs=[ANY_SPEC] * n,
            out_shape=[jax.ShapeDtypeStruct(a.shape, a.dtype) for a in lands],
            scratch_shapes=[pltpu.SemaphoreType.DMA((n_fwd,)), pltpu.SemaphoreType.DMA((n_fwd,))],
            input_output_aliases={i: i for i in range(n)},
        )(*lands)
        return list(fulls)


class _GradGather:
    def __init__(self, grads, axes, tag):
        self.grads, self.axes, self.tag = list(grads), axes, tag
        self.n = len(grads)
        self.shard_shapes = [_grad_shard_shape(g, ax) for g, ax in zip(grads, axes)]

    def _piece(self, src_refs, it, chip):
        r, w = self.shard_shapes[it]
        if self.grads[it].ndim == 3:
            return src_refs[it].at[chip // 2, :, pl.ds(pl.multiple_of((chip % 2) * w, LANE), w)]
        if self.axes[it] == 0:
            return src_refs[it].at[pl.ds(pl.multiple_of(chip * r, 16), r), :]
        return src_refs[it].at[:, pl.ds(pl.multiple_of(chip * w, LANE), w)]

    PER_ITEM = 4

    def _remote(self, src_refs, land_refs, ssem, rsem):
        x, y, c, others = _mesh_place()
        me = 4 * x + 2 * y + c
        pairs = []
        for it in range(self.n):
            def mk(k, piece_chip, slot, to, it=it):
                return pltpu.make_async_remote_copy(
                    src_ref=self._piece(src_refs, it, piece_chip), dst_ref=land_refs[it].at[slot],
                    send_sem=ssem.at[self.PER_ITEM * it + k], recv_sem=rsem.at[self.PER_ITEM * it + k], device_id=to,
                    device_id_type=MESH)
            for j, chip in enumerate(others):
                chip_id = 2 * chip[0] + chip[1]
                pairs.append((mk(j, chip_id, me, (*chip, c)), mk(j, chip_id, 2 * chip_id + c, (*chip, c))))
            sibling = (x, y, 1 - c)
            pairs.append((mk(3, 2 * x + y, me, sibling), mk(3, 2 * x + y, 4 * x + 2 * y + 1 - c, sibling)))
        return pairs

    def start(self):
        x, y, c = lax.axis_index("x"), lax.axis_index("y"), lax.axis_index("c")
        ids = jnp.stack([2 * x + y, 4 * x + 2 * y + c]).astype(jnp.int32)
        lands = [_slot_place(g, ids, ax, name="grads_own_%s%d" % (self.tag, it))
                 for it, (g, ax) in enumerate(zip(self.grads, self.axes))]
        self.ssem, self.rsem, self.srcs, self.lands, token = _split_start(
            self.grads, lands, self._remote, self.PER_ITEM * self.n, name="grads_start_" + self.tag)
        return token

    def _forward(self, src_refs, land_refs, ssem, rsem):
        x, y, c, others = _mesh_place()
        pairs = []
        for it in range(self.n):
            for j, ch in enumerate(others):
                def mk(slot, it=it, j=j):
                    return pltpu.make_async_remote_copy(
                        src_ref=land_refs[it].at[slot], dst_ref=land_refs[it].at[slot], send_sem=ssem.at[3 * it + j],
                        recv_sem=rsem.at[3 * it + j], device_id=(x, y, 1 - c), device_id_type=MESH)
                pairs.append((mk(4 * ch[0] + 2 * ch[1] + c), mk(4 * ch[0] + 2 * ch[1] + 1 - c)))
        return pairs

    def forward(self, after):
        _, lands = _split_wait(self.srcs, self.lands, self.ssem, self.rsem, after, self._remote,
                               name="grads_wait_" + self.tag)
        self.fsem, self.frsem, _, self.lands, token = _split_start(
            [], lands, self._forward, 3 * self.n, name="grads_fwd_" + self.tag)
        return token

    def finish(self, after):
        _, lands = _split_wait([], self.lands, self.fsem, self.frsem, after, self._forward,
                               name="grads_fwd_wait_" + self.tag)
        return lands


def _allreduce_small(vec, *, name):
    R, L = vec.shape

    def body(v_ref, o_ref, buf, send, recv, lsem):
        x, y, c, others = _mesh_place()
        me = 4 * x + 2 * y + c
        sibling = (x, y, 1 - c)

        def copy(k, slot, to, src=None):
            return pltpu.make_async_remote_copy(
                src_ref=buf.at[slot] if src is None else src, dst_ref=buf.at[slot], send_sem=send.at[k],
                recv_sem=recv.at[k], device_id=to, device_id_type=MESH)

        def slot_of(chip, core):
            return 4 * chip[0] + 2 * chip[1] + core

        mine = pltpu.make_async_copy(v_ref, buf.at[me], lsem)
        mine.start()
        first = [copy(0, me, sibling, src=v_ref)]
        first += [copy(1 + j, me, (*chip, c), src=v_ref) for j, chip in enumerate(others)]
        for cp in first:
            cp.start()
        passed = [copy(4 + j, slot_of(chip, c), sibling) for j, chip in enumerate(others)]
        for j, chip in enumerate(others):
            copy(1 + j, slot_of(chip, c), (*chip, c)).wait_recv()
            passed[j].start()
        copy(0, slot_of((x, y), 1 - c), sibling).wait_recv()
        for j, chip in enumerate(others):
            copy(4 + j, slot_of(chip, 1 - c), sibling).wait_recv()
        for cp in first + passed:
            cp.wait_send()
        mine.wait()
        total = buf[0]
        for k in range(1, N_DEV):
            total = total + buf[k]
        o_ref[...] = total

    return pl.pallas_call(
        body, name=name, in_specs=[pl.BlockSpec(memory_space=pltpu.VMEM)],
        out_specs=pl.BlockSpec(memory_space=pltpu.VMEM), out_shape=jax.ShapeDtypeStruct((R, L), F32),
        scratch_shapes=[pltpu.VMEM((N_DEV, R, L), F32), pltpu.SemaphoreType.DMA((7,)), pltpu.SemaphoreType.DMA((7,)),
                        pltpu.SemaphoreType.DMA],
        compiler_params=pltpu.CompilerParams(vmem_limit_bytes=VMEM_LIMIT),
    )(vec)


PACK_ALIGN = 1024


def _pack(arrs, row_multiple):
    flat = []
    for a in arrs:
        f = a.reshape(-1).astype(F32)
        flat.append(jnp.pad(f, (0, (-f.shape[0]) % PACK_ALIGN)))
    v = jnp.concatenate(flat)
    v = jnp.pad(v, (0, (-v.shape[0]) % (LANE * row_multiple)))
    return v.reshape(-1, LANE)


def _unpack(v, shapes):
    flat = v.reshape(-1)
    out, off = [], 0
    for s in shapes:
        size = math.prod(s)
        out.append(flat[off:off + size].reshape(s))
        off += size + (-size) % PACK_ALIGN
    return out


def _div_tile(dim, cap, mult=LANE):
    best = None
    for cand in range(mult, min(cap, dim) + 1, mult):
        if dim % cand == 0:
            best = cand
    return dim if best is None else best


WEIGHT_NAMES = ('norm1_g', 'w_in', 'ret_g', 'rg_conv_w', 'rg_conv_b', 'rg_wa', 'rg_ba', 'rg_wx', 'rg_bx', 'rg_lambda',
                'w_out', 'norm2_g', 'norm_mem_g', 'xa_wq', 'xa_wk', 'xa_wv', 'xa_wo', 'norm3_g', 'ffn_w_up',
                'ffn_conv_w', 'ffn_conv_b', 'ffn_w_down', 'final_g')
BIG_AXIS = {'w_in': 1, 'w_out': 0, 'xa_wq': 0, 'xa_wk': 0, 'xa_wv': 0, 'xa_wo': 0, 'ffn_w_up': 1, 'ffn_w_down': 0}
SMALL_SHARDED = ('rg_conv_w', 'ffn_conv_w')


def _step(x, mem, positions, loss_target, W, Mo, Vo):
    S, D = x.shape[1], x.shape[2]
    xs, mems, tgt = x[0], mem[0], loss_target[0]
    n_mem = mems.shape[0]
    pos_col = positions.reshape(S, 1)
    chip = 2 * lax.axis_index("x") + lax.axis_index("y")

    big = list(BIG_AXIS)
    shards = {n: W[n][0] for n in big}
    G = {}
    gather_groups = (('w_in', 'rg_conv_w'), ('w_out', 'xa_wq', 'xa_wk', 'xa_wv', 'xa_wo'),
                     ('ffn_w_up', 'ffn_conv_w'), ('ffn_w_down',))
    gathers, tok = [], None
    chip1 = jnp.reshape(chip, (1,)).astype(jnp.int32)
    for gi, names in enumerate(gather_groups):
        placed = []
        for n in names:
            if n in BIG_AXIS:
                placed.append(_cast_place(shards[n], chip1, BIG_AXIS[n], tok, name="place_" + n))
            else:
                s = W[n][0] if tok is None else W[n][0] + tok[0, 0]
                full = lax.empty((s.shape[0], s.shape[1] * N_CHIP), s.dtype)
                placed.append(lax.dynamic_update_slice(full, s, (0, chip * s.shape[1])))
        ag = _WeightGather(placed, [W[n][0].shape for n in names], [BIG_AXIS.get(n, 1) for n in names],
                           [n in BIG_AXIS for n in names], "g%d" % gi)
        tok = ag.start()
        gathers.append(ag)

    def finish_gather(gi, after):
        G.update(zip(gather_groups[gi], gathers[gi].finish(after)))

    def finish_forward(gi, after):
        G.update(zip(gather_groups[gi], gathers[gi].finish_forward(after)))

    R = W['ret_g'].shape[1]
    Wl = W['rg_lambda'].shape[1]
    IN = W['w_in'].shape[2] * N_CHIP
    F2 = W['ffn_w_up'].shape[2] * N_CHIP
    F = F2 // 2

    norm1_g, norm2_g, norm3_g = W['norm1_g'] + tok[0, 0], W['norm2_g'], W['norm3_g']
    norm_mem_g, final_g, ret_g = W['norm_mem_g'], W['final_g'].reshape(1, D), W['ret_g']
    rg_cb = W['rg_conv_b']
    wa, wx = W['rg_wa'][0], W['rg_wx'][0]
    ba, bx = W['rg_ba'].reshape(1, Wl), W['rg_bx'].reshape(1, Wl)
    lam = W['rg_lambda']
    ffn_cb = W['ffn_conv_b']

    def fwd_mm(a, wname, N, K, **kw):
        return _mm(a, G[wname], mode="nn", M=a.shape[0], N=N, K=K, tm=_div_tile(a.shape[0], 1024),
                   tn=_div_tile(N, 1536, 512) if K <= 3072 else 512, tk=K, **kw)

    def fwd_mm_norm(a, wname, res, g, name):
        return _mm(a, G[wname], mode="nn", M=a.shape[0], N=D, K=a.shape[1], tm=512, tn=D, tk=_div_tile(a.shape[1], 2048),
                   out_dtype=F32, res=res, norm_g=g, name=name)

    def bwd_x_mm(d, wname, N, K, **kw):
        return _mm(d, G[wname], mode="nt", M=d.shape[0], N=N, K=K, tm=_div_tile(d.shape[0], 1024),
                   tn=_div_tile(N, 1024 if K <= 3072 else 512, 256), tk=K, **kw)

    def bwd_w_mm(a, d, M, N, **kw):
        Ks = a.shape[0]
        return _mm(a, d, mode="tn", M=M, N=N, K=Ks, out_dtype=BF16, tm=_div_tile(M, 1024, 256),
                   tn=_div_tile(N, 1024, 256), tk=_div_tile(Ks, 4096 if d.dtype == BF16 else 1024), **kw)

    xn1 = _rmsnorm_fwd(xs, norm1_g, name="norm1_fwd")
    half = (R // RET_HEADS) // 2
    inv = (ROPE_BASE ** (-jnp.arange(half, dtype=F32) / half)).reshape(1, half)
    cos, sin = _rope_table(pos_col, inv + tok[0, 0], name="rope_table")
    finish_gather(0, cos)
    rg_cw = G['rg_conv_w']
    h = fwd_mm(xn1, 'w_in', IN, D, out_dtype=F32, name="mm_in")
    hl, mix = _lru_fwd(h, rg_cw, rg_cb, wa, ba, wx, bx, lam, name="lru_fwd")
    t1 = gathers[1].forward(hl)
    ret_raw, states, mix = _ret_fwd(h, cos, sin, ret_g + t1[0, 0], mix, name="ret_fwd")
    finish_forward(1, mix)
    x1, xn2 = fwd_mm_norm(mix, 'w_out', xs, norm2_g, "mm_out")
    memn = _rmsnorm_fwd(mems, norm_mem_g, name="norm_mem_fwd")
    km = fwd_mm(memn, 'xa_wk', D, D, out_dtype=BF16, name="mm_k")
    vm = fwd_mm(memn, 'xa_wv', D, D, out_dtype=BF16, name="mm_v")
    t2 = gathers[2].forward(x1)
    q = fwd_mm(xn2, 'xa_wq', D, D, out_dtype=BF16, after=t2, name="mm_q")
    o = _xattn_fwd(q, km, vm, name="xattn_fwd")
    x2, xn3 = fwd_mm_norm(o, 'xa_wo', x1, norm3_g, "mm_o")
    finish_forward(2, xn3)
    t3 = gathers[3].forward(xn3)
    ffn_cw = G['ffn_conv_w']
    act, hh_a, hh_b, hc_a, hc_b = _ffn_up_gate(xn3, G['ffn_w_up'], ffn_cw, ffn_cb + t3[0, 0], name="ffn_up_gate")
    finish_forward(3, act)
    x3 = fwd_mm(act, 'ffn_w_down', D, F, out_dtype=F32, res=x2, name="mm_down")
    dx3, d_final, loss8, dx3h = _final_loss(x3, tgt, final_g, name="final_loss")

    gw = {}
    grad_groups = []

    def start_grads(names, tag):
        gg = _GradGather([gw[n] for n in names], [BIG_AXIS[n] for n in names], tag)
        grad_groups.append((names, gg))
        return gg.start()

    dhh_a, dhh_b, gcw_a, gcw_b, gw['ffn_w_down'], gw['ffn_w_up'] = _ffn_bwd(
        dx3h, G['ffn_w_down'], hh_a, hh_b, hc_a, hc_b, act, xn3, ffn_cw, name="ffn_bwd")
    tok_a = start_grads(('ffn_w_down', 'ffn_w_up'), "a")
    dxn3 = bwd_x_mm(dhh_a, 'ffn_w_up', D, F, out_dtype=F32, after=tok_a, name="mm_dxn3_a")
    dxn3 = bwd_x_mm(dhh_b, 'ffn_w_up', D, F, out_dtype=BF16, b_off=(0, F), res=dxn3, name="mm_dxn3_b")
    dx2, d_norm3, dx2h = _rmsnorm_bwd(x2, dxn3, norm3_g, dx3, name="norm3_bwd", emit_bf16=True)
    Kc = ffn_cw.shape[0]
    d_ffn_cw = jnp.concatenate([gcw_a[:Kc], gcw_b[:Kc]], axis=1)
    d_ffn_cb = jnp.concatenate([gcw_a[Kc:Kc + 1], gcw_b[Kc:Kc + 1]], axis=1)

    d_o = bwd_x_mm(dx2h, 'xa_wo', D, D, out_dtype=BF16, name="mm_do")
    gw['xa_wo'] = bwd_w_mm(o, dx2h, D, D, name="mm_dw_o")
    dq, dk, dv = _xattn_bwd(q, km, vm, d_o, name="xattn_bwd")
    gw['xa_wq'] = bwd_w_mm(xn2, dq, D, D, name="mm_dw_q")
    dxn2 = bwd_x_mm(dq, 'xa_wq', D, D, out_dtype=BF16, name="mm_dxn2")
    gw['xa_wk'] = bwd_w_mm(memn, dk, D, D, name="mm_dw_k")
    gw['xa_wv'] = bwd_w_mm(memn, dv, D, D, name="mm_dw_v")
    dmemn = bwd_x_mm(dk, 'xa_wk', D, D, out_dtype=F32, name="mm_dmem_k")
    dmemn = bwd_x_mm(dv, 'xa_wv', D, D, out_dtype=F32, res=dmemn, name="mm_dmem_v")
    _, d_norm_mem = _rmsnorm_bwd(mems, dmemn, norm_mem_g, None, name="norm_mem_bwd")
    dx1, d_norm2, dx1h = _rmsnorm_bwd(x1, dxn2, norm2_g, dx2, name="norm2_bwd", emit_bf16=True)

    gw['w_out'] = bwd_w_mm(mix, dx1h, D, D, name="mm_dw_out")
    tok_b = start_grads(('xa_wo', 'xa_wq', 'xa_wk', 'xa_wv', 'w_out'), "b")
    dmix = bwd_x_mm(dx1h, 'w_out', D, D, out_dtype=BF16, after=tok_b, name="mm_dmix")
    dh, d_ret_g = _ret_bwd(h, cos, sin, ret_g, states, ret_raw, dmix, name="ret_bwd")
    dh, d_rcw, d_rcb, d_wa, d_ba, d_wx, d_bx, d_lam = _lru_bwd(
        h, hl, dmix, dh, rg_cw, rg_cb, wa, ba, wx, bx, lam, name="lru_bwd")
    gw['w_in'] = bwd_w_mm(xn1, dh, D, IN, name="mm_dw_in")
    tok_c = start_grads(('w_in',), "c")
    dxn1 = bwd_x_mm(dh, 'w_in', D, IN, out_dtype=BF16, after=tok_c, name="mm_dxn1")
    grad_x, d_norm1 = _rmsnorm_bwd(xs, dxn1, norm1_g, dx1, name="norm1_bwd")

    small_parts = {
        'norm1_g': d_norm1, 'ret_g': d_ret_g, 'rg_conv_w': d_rcw[:rg_cw.shape[0]], 'rg_conv_b': d_rcb,
        'rg_wa': d_wa, 'rg_ba': d_ba, 'rg_wx': d_wx, 'rg_bx': d_bx, 'rg_lambda': d_lam, 'norm2_g': d_norm2,
        'norm_mem_g': d_norm_mem, 'norm3_g': d_norm3, 'ffn_conv_w': d_ffn_cw, 'ffn_conv_b': d_ffn_cb,
        'final_g': d_final}
    small = [n for n in WEIGHT_NAMES if n not in BIG_AXIS]
    red_shapes = [(1,)] + [tuple(small_parts[n].shape) for n in small]
    fwd_tok = sum(gg.forward(d_norm1)[0:1, 0:1] for _, gg in grad_groups)
    reduced = _allreduce_small(_pack([loss8[0:1, 0:1] + fwd_tok] + [small_parts[n] for n in small], SUB),
                               name="allreduce_small")
    red = _unpack(reduced, red_shapes)
    loss = red[0][0]
    g_small = dict(zip(small, red[1:]))
    for n in SMALL_SHARDED:
        w_local = W[n].shape[-1]
        g_small[n] = lax.dynamic_slice_in_dim(g_small[n], chip * w_local, w_local, axis=1)

    out_g, out_d, out_m, out_v = {}, {}, {}, {}
    rows = 512
    pk = lambda d: _pack([d[n] for n in small], rows)
    g_pack = _pack([g_small[n] for n in small], rows)
    res_small = _adamw(pk(W), pk(Mo), pk(Vo), g_pack[None], name="adamw_small")
    shapes_small = [tuple(W[n].shape) for n in small]
    for dst, packed in zip((out_g, out_d, out_m, out_v), res_small):
        for n, val in zip(small, _unpack(packed, shapes_small)):
            dst[n] = val
    last = res_small[0]
    for names, gg in grad_groups:
        for n, land in zip(names, gg.finish(last)):
            g, d, m_new, v_new = _adamw(shards[n], Mo[n][0], Vo[n][0], land, name="adamw_" + n)
            out_g[n], out_d[n], out_m[n], out_v[n] = (t.reshape(W[n].shape) for t in (g, d, m_new, v_new))
            last = g
    return (loss, grad_x[None], *[out_g[n] for n in WEIGHT_NAMES], *[out_d[n] for n in WEIGHT_NAMES],
            *[out_m[n] for n in WEIGHT_NAMES], *[out_v[n] for n in WEIGHT_NAMES])


def kernel(x, mem, positions, norm1_g, w_in, ret_g, rg_conv_w, rg_conv_b, rg_wa, rg_ba, rg_wx, rg_bx, rg_lambda, w_out, norm2_g, norm_mem_g, xa_wq, xa_wk, xa_wv, xa_wo, norm3_g, ffn_w_up, ffn_conv_w, ffn_conv_b, ffn_w_down, final_g, loss_target, m_norm1_g, m_w_in, m_ret_g, m_rg_conv_w, m_rg_conv_b, m_rg_wa, m_rg_ba, m_rg_wx, m_rg_bx, m_rg_lambda, m_w_out, m_norm2_g, m_norm_mem_g, m_xa_wq, m_xa_wk, m_xa_wv, m_xa_wo, m_norm3_g, m_ffn_w_up, m_ffn_conv_w, m_ffn_conv_b, m_ffn_w_down, m_final_g, v_norm1_g, v_w_in, v_ret_g, v_rg_conv_w, v_rg_conv_b, v_rg_wa, v_rg_ba, v_rg_wx, v_rg_bx, v_rg_lambda, v_w_out, v_norm2_g, v_norm_mem_g, v_xa_wq, v_xa_wk, v_xa_wv, v_xa_wo, v_norm3_g, v_ffn_w_up, v_ffn_conv_w, v_ffn_conv_b, v_ffn_w_down, v_final_g):
    W = dict(zip(WEIGHT_NAMES, (norm1_g, w_in, ret_g, rg_conv_w, rg_conv_b, rg_wa, rg_ba, rg_wx, rg_bx, rg_lambda, w_out,
                                norm2_g, norm_mem_g, xa_wq, xa_wk, xa_wv, xa_wo, norm3_g, ffn_w_up, ffn_conv_w,
                                ffn_conv_b, ffn_w_down, final_g)))
    Mo = dict(zip(WEIGHT_NAMES, (m_norm1_g, m_w_in, m_ret_g, m_rg_conv_w, m_rg_conv_b, m_rg_wa, m_rg_ba, m_rg_wx, m_rg_bx,
                                 m_rg_lambda, m_w_out, m_norm2_g, m_norm_mem_g, m_xa_wq, m_xa_wk, m_xa_wv, m_xa_wo,
                                 m_norm3_g, m_ffn_w_up, m_ffn_conv_w, m_ffn_conv_b, m_ffn_w_down, m_final_g)))
    Vo = dict(zip(WEIGHT_NAMES, (v_norm1_g, v_w_in, v_ret_g, v_rg_conv_w, v_rg_conv_b, v_rg_wa, v_rg_ba, v_rg_wx, v_rg_bx,
                                 v_rg_lambda, v_w_out, v_norm2_g, v_norm_mem_g, v_xa_wq, v_xa_wk, v_xa_wv, v_xa_wo,
                                 v_norm3_g, v_ffn_w_up, v_ffn_conv_w, v_ffn_conv_b, v_ffn_w_down, v_final_g)))
    return _step(x, mem, positions, loss_target, W, Mo, Vo)
```

```python
import math

import jax
import jax.numpy as jnp
from jax import lax
from jax.experimental import pallas as pl
from jax.experimental.pallas import tpu as pltpu

F32 = jnp.float32
BF16 = jnp.bfloat16

EPS = 1e-6
RET_HEADS = 4
RET_CHUNK = 128
ROPE_BASE = 10000.0
LRU_BLOCKS = 8
LRU_C = 8.0
XA_HEADS = 4

ADAM_LR = 0.001
ADAM_B1 = 0.9
ADAM_B2 = 0.999
ADAM_EPS = 1e-08
ADAM_WD = 0.01
ADAM_STEP = 10

N_DEV = 8
N_CHIP = 4
MESH = pl.DeviceIdType.MESH
SUB = 8
LANE = 128
VMEM_LIMIT = 56 * 1024 * 1024

NN = ((1,), (0,))
NT = ((1,), (1,))
TN = ((0,), (0,))


def _cparams(sem):
    return pltpu.CompilerParams(dimension_semantics=sem, vmem_limit_bytes=VMEM_LIMIT)


def _sigmoid(v):
    return 1.0 / (1.0 + jnp.exp(-v))


def _bdot(a, b, dims):
    return lax.dot_general(a.astype(BF16), b.astype(BF16), (dims, ((), ())), preferred_element_type=F32)


def _row_iota(shape):
    return lax.broadcasted_iota(jnp.int32, shape, 0)


def _shift_down(v, tail, k):
    if k == 0:
        return v
    r = pltpu.roll(v, k, 0)
    rt = pltpu.roll(tail, k, 0)
    first = jnp.where(_row_iota(rt.shape) < k, rt, r[0:SUB])
    return jnp.concatenate([first, r[SUB:]], axis=0)


def _shift_up(v, head, k):
    if k == 0:
        return v
    n = v.shape[0]
    r = pltpu.roll(v, n - k, 0)
    rh = pltpu.roll(head, SUB - k, 0)
    last = jnp.where(_row_iota(rh.shape) >= SUB - k, rh, r[n - SUB:n])
    return jnp.concatenate([r[:n - SUB], last], axis=0)


def _mm(a, b, *, mode, M, N, K, out_dtype, name, tm=512, tn=512, tk=512, b_off=(0, 0), res=None, norm_g=None,
        after=None):
    tm, tn, tk = min(tm, M), min(tn, N), min(tk, K)
    assert M % tm == 0 and N % tn == 0 and K % tk == 0, (name, M, N, K, tm, tn, tk)
    nk = K // tk
    if mode == "nn":
        a_blk, b_blk, dims = (tm, tk), (tk, tn), NN
        a_map = lambda i, j, k: (i, k)
        b_map = lambda i, j, k: (k + b_off[0] // tk, j + b_off[1] // tn)
    elif mode == "nt":
        a_blk, b_blk, dims = (tm, tk), (tn, tk), NT
        a_map = lambda i, j, k: (i, k)
        b_map = lambda i, j, k: (j + b_off[0] // tn, k + b_off[1] // tk)
    else:
        a_blk, b_blk, dims = (tk, tm), (tk, tn), TN
        a_map = lambda i, j, k: (k, i)
        b_map = lambda i, j, k: (k + b_off[0] // tk, j + b_off[1] // tn)
    assert b_off[0] % b_blk[0] == 0 and b_off[1] % b_blk[1] == 0, (name, b_off, b_blk)
    has_res, has_norm, has_after = res is not None, norm_g is not None, after is not None
    assert not has_norm or tn == N

    def body(*refs):
        refs = list(refs)
        a_ref, b_ref = refs[0], refs[1]
        pos = 2
        r_ref = g_ref = n_ref = None
        if has_res:
            r_ref = refs[pos]
            pos += 1
        if has_norm:
            g_ref = refs[pos]
            pos += 1
        pos += has_after
        o_ref = refs[pos]
        pos += 1
        if has_norm:
            n_ref = refs[pos]
            pos += 1
        acc = refs[pos] if nk > 1 else None
        k = pl.program_id(2)
        part = _bdot(a_ref[...], b_ref[...], dims)

        def finish(total):
            if has_res:
                total = total + r_ref[...].astype(F32)
            o_ref[...] = total.astype(o_ref.dtype)
            if has_norm:
                r = lax.rsqrt(jnp.mean(total * total, axis=-1, keepdims=True) + EPS)
                n_ref[...] = (total * r * g_ref[...]).astype(n_ref.dtype)

        if nk == 1:
            finish(part)
        else:
            @pl.when(k == 0)
            def _():
                acc[...] = part

            @pl.when(k > 0)
            def _():
                acc[...] += part

            @pl.when(k == nk - 1)
            def _():
                finish(acc[...])

    in_specs = [pl.BlockSpec(a_blk, a_map), pl.BlockSpec(b_blk, b_map)]
    args = [a, b]
    if has_res:
        in_specs.append(pl.BlockSpec((tm, tn), lambda i, j, k: (i, j)))
        args.append(res)
    if has_norm:
        in_specs.append(pl.BlockSpec((1, N), lambda i, j, k: (0, 0)))
        args.append(norm_g)
    if has_after:
        in_specs.append(pl.BlockSpec(memory_space=pl.ANY))
        args.append(after)
    out_shape = jax.ShapeDtypeStruct((M, N), out_dtype)
    out_specs = pl.BlockSpec((tm, tn), lambda i, j, k: (i, j))
    if has_norm:
        out_shape = [out_shape, jax.ShapeDtypeStruct((M, N), BF16)]
        out_specs = [out_specs, pl.BlockSpec((tm, tn), lambda i, j, k: (i, j))]
    return pl.pallas_call(
        body, name=name, grid=(M // tm, N // tn, nk), in_specs=in_specs,
        out_specs=out_specs, out_shape=out_shape,
        scratch_shapes=[pltpu.VMEM((tm, tn), F32)] if nk > 1 else [],
        compiler_params=_cparams(("parallel", "parallel", "arbitrary")),
    )(*args)


def _rmsnorm_fwd(x, g, *, name, ts=512):
    S, D = x.shape
    ts = min(ts, S)

    def body(x_ref, g_ref, o_ref):
        xv = x_ref[...]
        r = lax.rsqrt(jnp.mean(xv * xv, axis=-1, keepdims=True) + EPS)
        o_ref[...] = (xv * r * g_ref[...]).astype(o_ref.dtype)

    return pl.pallas_call(
        body, name=name, grid=(S // ts,),
        in_specs=[pl.BlockSpec((ts, D), lambda i: (i, 0)), pl.BlockSpec((1, D), lambda i: (0, 0))],
        out_specs=pl.BlockSpec((ts, D), lambda i: (i, 0)),
        out_shape=jax.ShapeDtypeStruct((S, D), BF16),
        compiler_params=_cparams(("parallel",)),
    )(x, g)


def _rmsnorm_bwd(x, dxn, g, res, *, name, ts=512, emit_bf16=False):
    S, D = x.shape
    ts = min(ts, S)
    has_res = res is not None

    def body(*refs):
        refs = list(refs)
        dx16_ref = refs.pop() if emit_bf16 else None
        if has_res:
            x_ref, d_ref, g_ref, r_ref, dx_ref, dg_ref = refs
        else:
            x_ref, d_ref, g_ref, dx_ref, dg_ref = refs
        i = pl.program_id(0)
        xv = x_ref[...]
        dv = d_ref[...].astype(F32)
        r = lax.rsqrt(jnp.mean(xv * xv, axis=-1, keepdims=True) + EPS)
        gd = dv * g_ref[...]
        proj = jnp.mean(xv * gd, axis=-1, keepdims=True)
        dx = r * gd - xv * (r * r * r) * proj
        if has_res:
            dx = dx + r_ref[...]
        dx_ref[...] = dx
        if emit_bf16:
            dx16_ref[...] = dx.astype(BF16)
        part = jnp.sum(dv * xv * r, axis=0, keepdims=True)

        @pl.when(i == 0)
        def _():
            dg_ref[...] = part

        @pl.when(i > 0)
        def _():
            dg_ref[...] += part

    row = pl.BlockSpec((ts, D), lambda i: (i, 0))
    vec = pl.BlockSpec((1, D), lambda i: (0, 0))
    in_specs = [row, row, vec] + ([row] if has_res else [])
    args = [x, dxn, g] + ([res] if has_res else [])
    extra = emit_bf16 * [jax.ShapeDtypeStruct((S, D), BF16)]
    return pl.pallas_call(
        body, name=name, grid=(S // ts,), in_specs=in_specs, out_specs=[row, vec] + emit_bf16 * [row],
        out_shape=[jax.ShapeDtypeStruct((S, D), F32), jax.ShapeDtypeStruct((1, D), F32)] + extra,
        compiler_params=_cparams(("arbitrary",)),
    )(*args)


def _final_loss(x, target, g, *, name, ts=512):
    S, D = x.shape
    ts = min(ts, S)

    def body(x_ref, t_ref, g_ref, dx_ref, dg_ref, loss_ref, dx16_ref):
        i = pl.program_id(0)
        xv = x_ref[...]
        gv = g_ref[...]
        r = lax.rsqrt(jnp.mean(xv * xv, axis=-1, keepdims=True) + EPS)
        y = xv * r * gv
        err = y - t_ref[...]
        row_loss = jnp.mean(err * err, axis=-1, keepdims=True)
        lpart = 0.5 * jnp.sum(row_loss, axis=0, keepdims=True)
        dy = err * (1.0 / D)
        gd = dy * gv
        proj = jnp.mean(xv * gd, axis=-1, keepdims=True)
        dx = r * gd - xv * (r * r * r) * proj
        dx_ref[...] = dx
        dx16_ref[...] = dx.astype(BF16)
        part = jnp.sum(dy * xv * r, axis=0, keepdims=True)
        lfull = jnp.broadcast_to(lpart, loss_ref.shape)

        @pl.when(i == 0)
        def _():
            dg_ref[...] = part
            loss_ref[...] = lfull

        @pl.when(i > 0)
        def _():
            dg_ref[...] += part
            loss_ref[...] += lfull

    row = pl.BlockSpec((ts, D), lambda i: (i, 0))
    vec = pl.BlockSpec((1, D), lambda i: (0, 0))
    return pl.pallas_call(
        body, name=name, grid=(S // ts,), in_specs=[row, row, vec],
        out_specs=[row, vec, pl.BlockSpec((SUB, LANE), lambda i: (0, 0)), row],
        out_shape=[jax.ShapeDtypeStruct((S, D), F32), jax.ShapeDtypeStruct((1, D), F32),
                   jax.ShapeDtypeStruct((SUB, LANE), F32), jax.ShapeDtypeStruct((S, D), BF16)],
        compiler_params=_cparams(("arbitrary",)),
    )(x, target, g)


def _rope_table(pos_col, inv, *, name, ts=1024):
    S = pos_col.shape[0]
    ts = min(ts, S)
    half = inv.shape[1]

    def body(p_ref, inv_ref, c_ref, s_ref):
        ang = p_ref[...].astype(F32) * inv_ref[...]
        c_ref[...] = jnp.cos(ang)
        s_ref[...] = jnp.sin(ang)

    tab = pl.BlockSpec((ts, half), lambda i: (i, 0))
    return pl.pallas_call(
        body, name=name, grid=(S // ts,),
        in_specs=[pl.BlockSpec((ts, 1), lambda i: (i, 0)), pl.BlockSpec((1, half), lambda i: (0, 0))],
        out_specs=[tab, tab],
        out_shape=[jax.ShapeDtypeStruct((S, half), F32), jax.ShapeDtypeStruct((S, half), F32)],
        compiler_params=_cparams(("parallel",)),
    )(pos_col, inv)


def _ret_consts(C, log_g):
    ii = lax.broadcasted_iota(jnp.int32, (C, C), 0)
    jj = lax.broadcasted_iota(jnp.int32, (C, C), 1)
    diff = (ii - jj).astype(F32)
    intra = jnp.where(ii >= jj, jnp.exp(log_g * jnp.maximum(diff, 0.0)), 0.0)
    idx = lax.broadcasted_iota(jnp.int32, (C, 1), 0).astype(F32)
    qd = jnp.exp(log_g * (idx + 1.0))
    kd = jnp.exp(log_g * (C - 1.0 - idx))
    cd = math.exp(log_g * C)
    return intra, qd, kd, cd


def _rot(t, cs, sn):
    half = t.shape[-1] // 2
    t1, t2 = t[:, :half], t[:, half:]
    return jnp.concatenate([t1 * cs - t2 * sn, t1 * sn + t2 * cs], axis=-1)


def _unrot(d, cs, sn):
    half = d.shape[-1] // 2
    d1, d2 = d[:, :half], d[:, half:]
    return jnp.concatenate([d1 * cs + d2 * sn, d2 * cs - d1 * sn], axis=-1)


def _ret_fwd(h, cos, sin, ret_g, mix, *, name, ch=4):
    S = h.shape[0]
    R = ret_g.shape[1]
    H, C = RET_HEADS, RET_CHUNK
    Dh = R // H
    ts = ch * C
    assert S % ts == 0
    log_gs = [math.log(1.0 - 2.0 ** (-5.0 - hd)) for hd in range(H)]
    scale = Dh ** -0.5

    def body(x_ref, c_ref, s_ref, rg_ref, mix_in, ret_ref, st_ref, mix_ref, state):
        i = pl.program_id(0)

        @pl.when(i == 0)
        def _():
            state[...] = jnp.zeros_like(state)

        for c in range(ch):
            rows = pl.ds(c * C, C)
            cs, sn = c_ref[rows, :], s_ref[rows, :]
            for hd in range(H):
                intra, qd, kd, cd = _ret_consts(C, log_gs[hd])
                q = x_ref[rows, pl.ds(hd * Dh, Dh)]
                k = x_ref[rows, pl.ds(R + hd * Dh, Dh)]
                v = x_ref[rows, pl.ds(2 * R + hd * Dh, Dh)]
                g = x_ref[rows, pl.ds(3 * R + hd * Dh, Dh)]
                rq = _rot(q, cs, sn)
                rk = _rot(k, cs, sn) * scale
                st = state[hd]
                st_ref[c, hd] = st.astype(BF16)
                s_ = _bdot(rq, rk, NT) * intra
                ret = _bdot(s_, v, NN) + _bdot(rq * qd, st, NN)
                state[hd] = st * cd + _bdot(rk * kd, v, TN)
                ret_ref[rows, pl.ds(hd * Dh, Dh)] = ret
                rr = lax.rsqrt(jnp.mean(ret * ret, axis=-1, keepdims=True) + EPS)
                out = ret * rr * rg_ref[:, pl.ds(hd * Dh, Dh)] * (g * _sigmoid(g))
                mix_ref[rows, pl.ds(hd * Dh, Dh)] = out.astype(BF16)

    n_chunks = S // C
    return pl.pallas_call(
        body, name=name, grid=(S // ts,),
        in_specs=[pl.BlockSpec((ts, 4 * R), lambda i: (i, 0)),
                  pl.BlockSpec((ts, Dh // 2), lambda i: (i, 0)), pl.BlockSpec((ts, Dh // 2), lambda i: (i, 0)),
                  pl.BlockSpec((1, R), lambda i: (0, 0)), pl.BlockSpec(memory_space=pl.ANY)],
        out_specs=[pl.BlockSpec((ts, R), lambda i: (i, 0)),
                   pl.BlockSpec((ch, H, Dh, Dh), lambda i: (i, 0, 0, 0)),
                   pl.BlockSpec((ts, R), lambda i: (i, 0))],
        out_shape=[jax.ShapeDtypeStruct((S, R), F32), jax.ShapeDtypeStruct((n_chunks, H, Dh, Dh), BF16),
                   jax.ShapeDtypeStruct(mix.shape, mix.dtype)],
        scratch_shapes=[pltpu.VMEM((H, Dh, Dh), F32)],
        input_output_aliases={4: 2},
        compiler_params=_cparams(("arbitrary",)),
    )(h, cos, sin, ret_g, mix)


def _ret_bwd(h, cos, sin, ret_g, states, ret_raw, dmix, *, name, ch=4):
    S = h.shape[0]
    R = ret_g.shape[1]
    H, C = RET_HEADS, RET_CHUNK
    Dh = R // H
    ts = ch * C
    nb = S // ts
    log_gs = [math.log(1.0 - 2.0 ** (-5.0 - hd)) for hd in range(H)]
    scale = Dh ** -0.5

    def body(x_ref, c_ref, s_ref, rg_ref, st_ref, ret_ref, dm_ref, dh_ref, drg_ref, dstate):
        i = pl.program_id(0)

        @pl.when(i == 0)
        def _():
            dstate[...] = jnp.zeros_like(dstate)
            drg_ref[...] = jnp.zeros_like(drg_ref)

        for c in reversed(range(ch)):
            rows = pl.ds(c * C, C)
            cs, sn = c_ref[rows, :], s_ref[rows, :]
            for hd in range(H):
                intra, qd, kd, cd = _ret_consts(C, log_gs[hd])
                cols = pl.ds(hd * Dh, Dh)
                q = x_ref[rows, pl.ds(hd * Dh, Dh)]
                k = x_ref[rows, pl.ds(R + hd * Dh, Dh)]
                v = x_ref[rows, pl.ds(2 * R + hd * Dh, Dh)]
                g = x_ref[rows, pl.ds(3 * R + hd * Dh, Dh)]
                rq = _rot(q, cs, sn)
                rk = _rot(k, cs, sn) * scale
                ret = ret_ref[rows, cols]
                dm = dm_ref[rows, cols].astype(F32)
                rgv = rg_ref[:, cols]
                rr = lax.rsqrt(jnp.mean(ret * ret, axis=-1, keepdims=True) + EPS)
                retn = ret * rr
                sg = _sigmoid(g)
                silu = g * sg
                drg_ref[:, cols] += jnp.sum(dm * retn * silu, axis=0, keepdims=True)
                dg = dm * retn * rgv * (sg * (1.0 + g * (1.0 - sg)))
                dretn = dm * rgv * silu
                d_o = rr * dretn - ret * (rr * rr * rr) * jnp.mean(ret * dretn, axis=-1, keepdims=True)
                st = st_ref[c, hd]
                d_s = dstate[hd]
                a_ = _bdot(rq, rk, NT) * intra
                d_a = _bdot(d_o, v, NT) * intra
                d_qr = _bdot(d_a, rk, NN) + _bdot(d_o, st, NT) * qd
                d_kr = _bdot(d_a, rq, TN) + _bdot(v, d_s, NT) * kd
                d_v = _bdot(a_, d_o, TN) + _bdot(rk * kd, d_s, NN)
                dstate[hd] = d_s * cd + _bdot(rq * qd, d_o, TN)
                dh_ref[rows, pl.ds(hd * Dh, Dh)] = _unrot(d_qr, cs, sn).astype(BF16)
                dh_ref[rows, pl.ds(R + hd * Dh, Dh)] = (_unrot(d_kr, cs, sn) * scale).astype(BF16)
                dh_ref[rows, pl.ds(2 * R + hd * Dh, Dh)] = d_v.astype(BF16)
                dh_ref[rows, pl.ds(3 * R + hd * Dh, Dh)] = dg.astype(BF16)

    rb = lambda i: nb - 1 - i
    return pl.pallas_call(
        body, name=name, grid=(nb,),
        in_specs=[pl.BlockSpec((ts, 4 * R), lambda i: (rb(i), 0)),
                  pl.BlockSpec((ts, Dh // 2), lambda i: (rb(i), 0)), pl.BlockSpec((ts, Dh // 2), lambda i: (rb(i), 0)),
                  pl.BlockSpec((1, R), lambda i: (0, 0)),
                  pl.BlockSpec((ch, H, Dh, Dh), lambda i: (rb(i), 0, 0, 0)),
                  pl.BlockSpec((ts, R), lambda i: (rb(i), 0)),
                  pl.BlockSpec((ts, R), lambda i: (rb(i), 0))],
        out_specs=[pl.BlockSpec((ts, 4 * R), lambda i: (rb(i), 0)), pl.BlockSpec((1, R), lambda i: (0, 0))],
        out_shape=[jax.ShapeDtypeStruct((S, 6 * R), BF16), jax.ShapeDtypeStruct((1, R), F32)],
        scratch_shapes=[pltpu.VMEM((H, Dh, Dh), F32)],
        compiler_params=_cparams(("arbitrary",)),
    )(h, cos, sin, ret_g, states, ret_raw, dmix)


GELU_C = math.sqrt(2.0 / math.pi)
GELU_A = 0.044715


def _gelu_parts(y):
    t = jnp.tanh(GELU_C * (y + GELU_A * y * y * y))
    val = 0.5 * y * (1.0 + t)
    grad = 0.5 * (1.0 + t) + 0.5 * y * (1.0 - t * t) * GELU_C * (1.0 + 3.0 * GELU_A * y * y)
    return val, grad


def _neg_expm1(x):
    series = -x * (1.0 + x * (1.0 / 2.0) * (1.0 + x * (1.0 / 3.0) * (1.0 + x * (1.0 / 4.0) * (
        1.0 + x * (1.0 / 5.0) * (1.0 + x * (1.0 / 6.0) * (1.0 + x * (1.0 / 7.0)))))))
    return jnp.where(x > -0.35, series, 1.0 - jnp.exp(x))


def _log_sigmoid(x):
    return jnp.minimum(x, 0.0) - jnp.log1p(jnp.exp(-jnp.abs(x)))


def _lru_gates(uc, wa_ref, ba_ref, wx_ref, bx_ref):
    nbk = wa_ref.shape[0]
    bd = wa_ref.shape[1]
    rs, gs = [], []
    for n in range(nbk):
        ucn = uc[:, n * bd:(n + 1) * bd]
        rs.append(_sigmoid(_bdot(ucn, wa_ref[n], NN) + ba_ref[:, pl.ds(n * bd, bd)]))
        gs.append(_sigmoid(_bdot(ucn, wx_ref[n], NN) + bx_ref[:, pl.ds(n * bd, bd)]))
    return jnp.concatenate(rs, axis=-1), jnp.concatenate(gs, axis=-1)


def _lru_fwd(h, conv_w, conv_b, wa, ba, wx, bx, lam, *, name, ts=256):
    S = h.shape[0]
    W = lam.shape[1]
    K = conv_w.shape[0]
    ts = min(ts, S)

    def body(u_ref, y_ref, cw_ref, cb_ref, wa_ref, ba_ref, wx_ref, bx_ref, lam_ref, hl_ref, mix_ref, tail, hlast):
        i = pl.program_id(0)

        @pl.when(i == 0)
        def _():
            tail[...] = jnp.zeros_like(tail)
            hlast[...] = jnp.zeros_like(hlast)

        u = u_ref[...]
        tl = tail[...]
        uc = cb_ref[...] + cw_ref[K - 1:K, :] * u
        for k in range(K - 1):
            uc = uc + cw_ref[k:k + 1, :] * _shift_down(u, tl, K - 1 - k)
        tail[...] = u[ts - SUB:ts]
        r, ig = _lru_gates(uc, wa_ref, ba_ref, wx_ref, bx_ref)
        log_a = LRU_C * r * _log_sigmoid(lam_ref[...])
        a = jnp.exp(log_a)
        b = jnp.sqrt(_neg_expm1(2.0 * log_a)) * (ig * uc)
        in_tile = _row_iota((ts, W)) & (SUB - 1)
        d = 1
        while d < SUB:
            a_s = jnp.where(in_tile < d, 1.0, pltpu.roll(a, d, 0))
            b_s = jnp.where(in_tile < d, 0.0, pltpu.roll(b, d, 0))
            b = a * b_s + b
            a = a * a_s
            d *= 2
        before = hlast[SUB - 1:SUB, :]
        for k in range(ts // SUB):
            tile = slice(k * SUB, (k + 1) * SUB)
            h_tile = a[tile] * before + b[tile]
            hl_ref[tile, :] = h_tile
            before = h_tile[SUB - 1:SUB, :]
        hlast[...] = hl_ref[ts - SUB:ts, :]
        gy, _ = _gelu_parts(y_ref[...])
        mix_ref[...] = (hl_ref[...] * gy).astype(BF16)

    full = lambda shape: pl.BlockSpec(shape, lambda i: tuple(0 for _ in shape))
    return pl.pallas_call(
        body, name=name, grid=(S // ts,),
        in_specs=[pl.BlockSpec((ts, W), lambda i: (i, 4)), pl.BlockSpec((ts, W), lambda i: (i, 5)),
                  full(conv_w.shape), full(conv_b.shape), full(wa.shape), full(ba.shape), full(wx.shape),
                  full(bx.shape), full(lam.shape)],
        out_specs=[pl.BlockSpec((ts, W), lambda i: (i, 0)), pl.BlockSpec((ts, W), lambda i: (i, 1))],
        out_shape=[jax.ShapeDtypeStruct((S, W), F32), jax.ShapeDtypeStruct((S, 2 * W), BF16)],
        scratch_shapes=[pltpu.VMEM((SUB, W), F32), pltpu.VMEM((SUB, W), F32)],
        compiler_params=_cparams(("arbitrary",)),
    )(h, h, conv_w, conv_b, wa, ba, wx, bx, lam)


def _lru_bwd(h, hl, dmix, dh, conv_w, conv_b, wa, ba, wx, bx, lam, *, name, ts=256):
    S = h.shape[0]
    W = lam.shape[1]
    K = conv_w.shape[0]
    nbk, bd = wa.shape[0], wa.shape[1]
    ts = min(ts, S)
    nb = S // ts
    t8 = ts // SUB

    def body(u_ref, y_ref, uh_ref, hl_ref, hh_ref, dm_ref, cw_ref, cb_ref, wa_ref, ba_ref, wx_ref, bx_ref, lam_ref,
             dh_in, dh_ref, dcw_ref, dcb_ref, dwa_ref, dba_ref, dwx_ref, dbx_ref, dlam_ref, carry, head, lam_buf):
        i = pl.program_id(0)
        blk = nb - 1 - i

        @pl.when(i == 0)
        def _():
            carry[...] = jnp.zeros_like(carry)
            head[...] = jnp.zeros_like(head)
            for ref in (dcw_ref, dcb_ref, dwa_ref, dba_ref, dwx_ref, dbx_ref, dlam_ref):
                ref[...] = jnp.zeros_like(ref)

        inside = (blk > 0).astype(F32)
        u = u_ref[...]
        tl = uh_ref[...] * inside
        sh = [_shift_down(u, tl, K - 1 - k) for k in range(K)]
        uc = cb_ref[...]
        for k in range(K):
            uc = uc + cw_ref[k:k + 1, :] * sh[k]
        r, ig = _lru_gates(uc, wa_ref, ba_ref, wx_ref, bx_ref)
        lam_v = lam_ref[...]
        ls = _log_sigmoid(lam_v)
        log_a = LRU_C * r * ls
        a = jnp.exp(log_a)
        mult = jnp.sqrt(_neg_expm1(2.0 * log_a))
        hcur = hl_ref[...]
        hprev = _shift_down(hcur, hh_ref[...] * inside, 1)
        gy, dgy = _gelu_parts(y_ref[...])
        dm = dm_ref[...].astype(F32)
        d_y = dm * hcur * dgy
        rid = _row_iota((ts, W))
        bq = dm * gy + jnp.where(rid == ts - 1, carry[0:1, :], 0.0)
        aq = jnp.where(rid == ts - 1, 0.0, pltpu.roll(a, ts - 1, 0))
        in_tile = rid & (SUB - 1)
        d = 1
        while d < SUB:
            a_s = jnp.where(in_tile >= SUB - d, 1.0, pltpu.roll(aq, ts - d, 0))
            b_s = jnp.where(in_tile >= SUB - d, 0.0, pltpu.roll(bq, ts - d, 0))
            bq = bq + aq * b_s
            aq = aq * a_s
            d *= 2
        after_row = jnp.zeros((1, W), F32)
        for k in reversed(range(ts // SUB)):
            tile = slice(k * SUB, (k + 1) * SUB)
            lam_tile = aq[tile] * after_row + bq[tile]
            lam_buf[tile, :] = lam_tile
            after_row = lam_tile[0:1, :]
        lam_t = lam_buf[...]
        carry[...] = (a * lam_t)[0:SUB]
        d_a = lam_t * hprev
        d_mult = lam_t * (ig * uc)
        d_i = lam_t * mult * uc
        d_uc = lam_t * mult * ig
        d_log_a = d_a * a - d_mult * (a * a) / mult
        d_r = d_log_a * (LRU_C * ls)
        dlam_ref[...] += jnp.sum(d_log_a * (LRU_C * r), axis=0, keepdims=True) * _sigmoid(-lam_v)
        d_pr = d_r * r * (1.0 - r)
        d_pi = d_i * ig * (1.0 - ig)
        dba_ref[...] += jnp.sum(d_pr, axis=0, keepdims=True)
        dbx_ref[...] += jnp.sum(d_pi, axis=0, keepdims=True)
        extra = []
        for n in range(nbk):
            sl = slice(n * bd, (n + 1) * bd)
            ucn = uc[:, sl]
            dwa_ref[n] += _bdot(ucn, d_pr[:, sl], TN)
            dwx_ref[n] += _bdot(ucn, d_pi[:, sl], TN)
            extra.append(_bdot(d_pr[:, sl], wa_ref[n], NT) + _bdot(d_pi[:, sl], wx_ref[n], NT))
        d_uc = d_uc + jnp.concatenate(extra, axis=-1)
        dcb_ref[...] += jnp.sum(d_uc, axis=0, keepdims=True)
        rid8 = _row_iota((SUB, W))
        dcw = jnp.zeros((SUB, W), F32)
        for k in range(K):
            dcw = dcw + jnp.where(rid8 == k, jnp.sum(d_uc * sh[k], axis=0, keepdims=True), 0.0)
        dcw_ref[...] += dcw
        hd = head[...]
        d_u = cw_ref[K - 1:K, :] * d_uc
        for j in range(1, K):
            d_u = d_u + cw_ref[K - 1 - j:K - j, :] * _shift_up(d_uc, hd, j)
        head[...] = d_uc[0:SUB]
        dh_ref[:, 0:W] = d_u.astype(BF16)
        dh_ref[:, W:2 * W] = d_y.astype(BF16)

    rb = lambda i: nb - 1 - i
    prev8 = lambda i: jnp.maximum(rb(i) * t8 - 1, 0)
    full = lambda shape: pl.BlockSpec(shape, lambda i: tuple(0 for _ in shape))
    small = [jax.ShapeDtypeStruct((SUB, W), F32), jax.ShapeDtypeStruct((1, W), F32),
             jax.ShapeDtypeStruct(wa.shape, F32), jax.ShapeDtypeStruct((1, W), F32),
             jax.ShapeDtypeStruct(wx.shape, F32), jax.ShapeDtypeStruct((1, W), F32),
             jax.ShapeDtypeStruct((1, W), F32)]
    return pl.pallas_call(
        body, name=name, grid=(nb,),
        in_specs=[pl.BlockSpec((ts, W), lambda i: (rb(i), 4)), pl.BlockSpec((ts, W), lambda i: (rb(i), 5)),
                  pl.BlockSpec((SUB, W), lambda i: (prev8(i), 4)),
                  pl.BlockSpec((ts, W), lambda i: (rb(i), 0)), pl.BlockSpec((SUB, W), lambda i: (prev8(i), 0)),
                  pl.BlockSpec((ts, W), lambda i: (rb(i), 1)),
                  full(conv_w.shape), full(conv_b.shape), full(wa.shape), full(ba.shape), full(wx.shape),
                  full(bx.shape), full(lam.shape), pl.BlockSpec(memory_space=pl.ANY)],
        out_specs=[pl.BlockSpec((ts, 2 * W), lambda i: (rb(i), 2))] + [full(s.shape) for s in small],
        out_shape=[jax.ShapeDtypeStruct(dh.shape, dh.dtype)] + small,
        scratch_shapes=[pltpu.VMEM((SUB, W), F32), pltpu.VMEM((SUB, W), F32), pltpu.VMEM((ts, W), F32)],
        input_output_aliases={13: 0},
        compiler_params=_cparams(("arbitrary",)),
    )(h, h, h, hl, hl, dmix, conv_w, conv_b, wa, ba, wx, bx, lam, dh)


def _xattn_fwd(q, km, vm, *, name, ts=1024):
    S, D = q.shape
    M = km.shape[0]
    H = XA_HEADS
    Dh = D // H
    ts = min(ts, S)
    scale = Dh ** -0.5

    def body(q_ref, k_ref, v_ref, o_ref):
        for hd in range(H):
            cols = pl.ds(hd * Dh, Dh)
            s = _bdot(q_ref[:, cols], k_ref[:, cols], NT) * scale
            s = s - jnp.max(s, axis=-1, keepdims=True)
            e = jnp.exp(s)
            p = e / jnp.sum(e, axis=-1, keepdims=True)
            o_ref[:, cols] = _bdot(p, v_ref[:, cols], NN).astype(o_ref.dtype)

    return pl.pallas_call(
        body, name=name, grid=(S // ts,),
        in_specs=[pl.BlockSpec((ts, D), lambda i: (i, 0)), pl.BlockSpec((M, D), lambda i: (0, 0)),
                  pl.BlockSpec((M, D), lambda i: (0, 0))],
        out_specs=pl.BlockSpec((ts, D), lambda i: (i, 0)),
        out_shape=jax.ShapeDtypeStruct((S, D), BF16),
        compiler_params=_cparams(("parallel",)),
    )(q, km, vm)


def _xattn_bwd(q, km, vm, d_o, *, name, ts=1024):
    S, D = q.shape
    M = km.shape[0]
    H = XA_HEADS
    Dh = D // H
    ts = min(ts, S)
    scale = Dh ** -0.5

    def body(q_ref, k_ref, v_ref, do_ref, dq_ref, dk_ref, dv_ref):
        i = pl.program_id(0)

        @pl.when(i == 0)
        def _():
            dk_ref[...] = jnp.zeros_like(dk_ref)
            dv_ref[...] = jnp.zeros_like(dv_ref)

        for hd in range(H):
            cols = pl.ds(hd * Dh, Dh)
            qh, kh, vh, doh = q_ref[:, cols], k_ref[:, cols], v_ref[:, cols], do_ref[:, cols]
            s = _bdot(qh, kh, NT) * scale
            s = s - jnp.max(s, axis=-1, keepdims=True)
            e = jnp.exp(s)
            p = e / jnp.sum(e, axis=-1, keepdims=True)
            dp = _bdot(doh, vh, NT)
            ds = p * (dp - jnp.sum(dp * p, axis=-1, keepdims=True)) * scale
            dq_ref[:, cols] = _bdot(ds, kh, NN).astype(dq_ref.dtype)
            dk_ref[:, cols] += _bdot(ds, qh, TN)
            dv_ref[:, cols] += _bdot(p, doh, TN)

    row = pl.BlockSpec((ts, D), lambda i: (i, 0))
    mem = pl.BlockSpec((M, D), lambda i: (0, 0))
    return pl.pallas_call(
        body, name=name, grid=(S // ts,), in_specs=[row, mem, mem, row], out_specs=[row, mem, mem],
        out_shape=[jax.ShapeDtypeStruct((S, D), BF16), jax.ShapeDtypeStruct((M, D), F32),
                   jax.ShapeDtypeStruct((M, D), F32)],
        compiler_params=_cparams(("arbitrary",)),
    )(q, km, vm, d_o)


def _conv_rows(v, tail, cw_ref, cb_ref):
    K = cw_ref.shape[0]
    sh = [_shift_down(v, tail, K - 1 - k) for k in range(K)]
    out = cb_ref[...]
    for k in range(K):
        out = out + cw_ref[k:k + 1, :] * sh[k]
    return out, sh


FFN_SUB = 256


def _ffn_up_gate(xn, w_up, cw, cb, *, name, tm=1024, tn=512):
    S, D = xn.shape
    F2 = w_up.shape[1]
    F = F2 // 2
    tm, tn = min(tm, S), min(tn, F)
    sub = min(FFN_SUB, tm)
    nj = F // tn
    K = cw.shape[0]

    def body(x_ref, wa_ref, wb_ref, cwa_ref, cwb_ref, cba_ref, cbb_ref, act_ref, ha_ref, hb_ref, ac_ref, bc_ref, ta, tb):
        i = pl.program_id(1)

        @pl.when(i == 0)
        def _():
            ta[...] = jnp.zeros_like(ta)
            tb[...] = jnp.zeros_like(tb)

        tail_a, tail_b = ta[...], tb[...]
        for s in range(tm // sub):
            rows = pl.ds(s * sub, sub)
            xs = x_ref[rows, :]
            ha = _bdot(xs, wa_ref[...], NN)
            hb = _bdot(xs, wb_ref[...], NN)
            ac, _ = _conv_rows(ha, tail_a, cwa_ref, cba_ref)
            bc, _ = _conv_rows(hb, tail_b, cwb_ref, cbb_ref)
            tail_a, tail_b = ha[sub - SUB:sub], hb[sub - SUB:sub]
            ha_ref[rows, :] = ha
            hb_ref[rows, :] = hb
            ac_ref[rows, :] = ac
            bc_ref[rows, :] = bc
            act_ref[rows, :] = (ac * _sigmoid(ac) * bc).astype(act_ref.dtype)
        ta[...] = tail_a
        tb[...] = tail_b

    blk = pl.BlockSpec((tm, tn), lambda j, i: (i, j))
    return pl.pallas_call(
        body, name=name, grid=(nj, S // tm),
        in_specs=[pl.BlockSpec((tm, D), lambda j, i: (i, 0)),
                  pl.BlockSpec((D, tn), lambda j, i: (0, j)), pl.BlockSpec((D, tn), lambda j, i: (0, j + nj)),
                  pl.BlockSpec((K, tn), lambda j, i: (0, j)), pl.BlockSpec((K, tn), lambda j, i: (0, j + nj)),
                  pl.BlockSpec((1, tn), lambda j, i: (0, j)), pl.BlockSpec((1, tn), lambda j, i: (0, j + nj))],
        out_specs=[blk] * 5,
        out_shape=[jax.ShapeDtypeStruct((S, F), BF16)] + [jax.ShapeDtypeStruct((S, F), F32)] * 4,
        scratch_shapes=[pltpu.VMEM((SUB, tn), F32), pltpu.VMEM((SUB, tn), F32)],
        compiler_params=_cparams(("parallel", "arbitrary")),
    )(xn, w_up, w_up, cw, cw, cb, cb)


def _ffn_bwd(dx, w_down, hh_a, hh_b, c_a, c_b, act, xn, cw, *, name, tm=1024, tn=256):
    S, D = dx.shape
    F = hh_a.shape[1]
    tm, tn = min(tm, S), min(tn, F)
    sub = min(FFN_SUB, tm)
    nj = F // tn
    nb = S // tm
    K = cw.shape[0]

    def body(dx_ref, wd_ref, a_ref, b_ref, ac_ref, bc_ref, act_ref, xn_ref, cwa_ref, cwb_ref,
             da_ref, db_ref, ga_ref, gb_ref, dwd_ref, dwu_ref, ha, hb, acc_d, acc_a, acc_b):
        i = pl.program_id(1)

        @pl.when(i == 0)
        def _():
            for ref in (ha, hb, ga_ref, gb_ref, acc_d, acc_a, acc_b):
                ref[...] = jnp.zeros_like(ref)

        rid8 = _row_iota((SUB, tn))
        heads = [ha[...], hb[...]]
        gsums = [jnp.zeros((SUB, tn), F32), jnp.zeros((SUB, tn), F32)]
        for s in reversed(range(tm // sub)):
            rows = pl.ds(s * sub, sub)
            dv = _bdot(dx_ref[rows, :], wd_ref[...], NT)
            ac, bc = ac_ref[rows, :], bc_ref[rows, :]
            sg = _sigmoid(ac)
            d_bc = dv * ac * sg
            d_ac = dv * bc * sg * (1.0 + ac * (1.0 - sg))
            for which, (d_c, h_ref, cw_ref, o_ref) in enumerate(((d_ac, a_ref, cwa_ref, da_ref),
                                                                 (d_bc, b_ref, cwb_ref, db_ref))):
                ahead = [d_c] + [_shift_up(d_c, heads[which], j) for j in range(1, K)]
                heads[which] = d_c[0:SUB]
                d_in = cw_ref[K - 1:K, :] * d_c
                for j in range(1, K):
                    d_in = d_in + cw_ref[K - 1 - j:K - j, :] * ahead[j]
                o_ref[rows, :] = d_in.astype(o_ref.dtype)
                hv = h_ref[rows, :]
                gsum = gsums[which] + jnp.where(rid8 == K, jnp.sum(d_c, axis=0, keepdims=True), 0.0)
                for k in range(K):
                    gsum = gsum + jnp.where(rid8 == k, jnp.sum(ahead[K - 1 - k] * hv, axis=0, keepdims=True), 0.0)
                gsums[which] = gsum
        ha[...], hb[...] = heads
        ga_ref[...] += gsums[0]
        gb_ref[...] += gsums[1]
        acc_d[...] += _bdot(act_ref[...], dx_ref[...], TN)
        acc_a[...] += _bdot(xn_ref[...], da_ref[...], TN)
        acc_b[...] += _bdot(xn_ref[...], db_ref[...], TN)

        @pl.when(i == nb - 1)
        def _():
            dwd_ref[...] = acc_d[...].astype(dwd_ref.dtype)
            dwu_ref[0] = acc_a[...].astype(dwu_ref.dtype)
            dwu_ref[1] = acc_b[...].astype(dwu_ref.dtype)

    rb = lambda i: nb - 1 - i
    blk = pl.BlockSpec((tm, tn), lambda j, i: (rb(i), j))
    acc = pl.BlockSpec((SUB, tn), lambda j, i: (0, j))
    rows_d = pl.BlockSpec((tm, D), lambda j, i: (rb(i), 0))
    return pl.pallas_call(
        body, name=name, grid=(nj, nb),
        in_specs=[rows_d, pl.BlockSpec((tn, D), lambda j, i: (j, 0)), blk, blk, blk, blk, blk, rows_d,
                  pl.BlockSpec((K, tn), lambda j, i: (0, j)), pl.BlockSpec((K, tn), lambda j, i: (0, j + nj))],
        out_specs=[blk, blk, acc, acc, pl.BlockSpec((tn, D), lambda j, i: (j, 0)),
                   pl.BlockSpec((2, D, tn), lambda j, i: (0, 0, j))],
        out_shape=[jax.ShapeDtypeStruct((S, F), BF16), jax.ShapeDtypeStruct((S, F), BF16),
                   jax.ShapeDtypeStruct((SUB, F), F32), jax.ShapeDtypeStruct((SUB, F), F32),
                   jax.ShapeDtypeStruct((F, D), BF16), jax.ShapeDtypeStruct((2, D, F), BF16)],
        scratch_shapes=[pltpu.VMEM((SUB, tn), F32), pltpu.VMEM((SUB, tn), F32), pltpu.VMEM((tn, D), F32),
                        pltpu.VMEM((D, tn), F32), pltpu.VMEM((D, tn), F32)],
        compiler_params=_cparams(("parallel", "arbitrary")),
    )(dx, w_down, hh_a, hh_b, c_a, c_b, act, xn, cw, cw)


ADAM_BLOCK_ELEMS = 256 * 1024


def _adamw(w, m, v, parts, *, name):
    R, C = w.shape
    n = parts.shape[0]
    tr = R
    for cand in (1024, 512, 256, 128, 64, 32, 16):
        if R % cand == 0 and cand * C <= ADAM_BLOCK_ELEMS:
            tr = cand
            break
    c1 = 1.0 - ADAM_B1 ** ADAM_STEP
    c2 = 1.0 - ADAM_B2 ** ADAM_STEP

    def body(w_ref, m_ref, v_ref, p_ref, g_ref, d_ref, nm_ref, nv_ref):
        g = p_ref[0].astype(F32)
        for k in range(1, n):
            g = g + p_ref[k].astype(F32)
        m_new = ADAM_B1 * m_ref[...] + (1.0 - ADAM_B1) * g
        v_new = ADAM_B2 * v_ref[...] + (1.0 - ADAM_B2) * (g * g)
        m_hat = m_new / c1
        v_hat = v_new / c2
        g_ref[...] = g
        d_ref[...] = -ADAM_LR * (m_hat / (jnp.sqrt(v_hat) + ADAM_EPS) + ADAM_WD * w_ref[...])
        nm_ref[...] = m_new
        nv_ref[...] = v_new

    blk = pl.BlockSpec((tr, C), lambda i: (i, 0))
    sds = jax.ShapeDtypeStruct((R, C), F32)
    return pl.pallas_call(
        body, name=name, grid=(R // tr,),
        in_specs=[blk, blk, blk, pl.BlockSpec((n, tr, C), lambda i: (0, i, 0))],
        out_specs=[blk, blk, blk, blk], out_shape=[sds, sds, sds, sds],
        compiler_params=_cparams(("parallel",)),
    )(w, m, v, parts)


def _mesh_place():
    x, y, c = lax.axis_index("x"), lax.axis_index("y"), lax.axis_index("c")
    others = [(1 - x, y), (x, 1 - y), (1 - x, 1 - y)]
    return x, y, c, others


HBM_SPEC = pl.BlockSpec(memory_space=pltpu.HBM)
SEM_SPEC = pl.BlockSpec(memory_space=pltpu.SEMAPHORE)
ANY_SPEC = pl.BlockSpec(memory_space=pl.ANY)
EFFECT = pltpu.SideEffectType.DATAFLOW_SIDE_EFFECTING


def _in_hbm(a):
    return pltpu.with_memory_space_constraint(a, pltpu.HBM)


def _split_start(srcs, lands, copies, n_cp, *, name):
    n_s, n_l = len(srcs), len(lands)

    def body(*refs):
        src_refs, land_refs = refs[:n_s], refs[n_s:n_s + n_l]
        ssem, rsem = refs[n_s + n_l], refs[n_s + n_l + 1]
        token = refs[-1]
        for outgoing, _ in copies(src_refs, land_refs, ssem, rsem):
            outgoing.start()
        token[...] = jnp.zeros_like(token)

    outs = pl.pallas_call(
        body, name=name,
        out_shape=(pltpu.SemaphoreType.DMA((n_cp,)), pltpu.SemaphoreType.DMA((n_cp,)),
                   *[pltpu.HBM(a.shape, a.dtype) for a in srcs], *[pltpu.HBM(a.shape, a.dtype) for a in lands],
                   jax.ShapeDtypeStruct((SUB, LANE), F32)),
        in_specs=[HBM_SPEC] * (n_s + n_l),
        out_specs=(SEM_SPEC, SEM_SPEC, *[HBM_SPEC] * (n_s + n_l), pl.BlockSpec(memory_space=pltpu.VMEM)),
        input_output_aliases={i: 2 + i for i in range(n_s + n_l)},
        compiler_params=pltpu.CompilerParams(has_side_effects=EFFECT),
    )(*[_in_hbm(a) for a in srcs], *[_in_hbm(a) for a in lands])
    ssem, rsem = outs[0], outs[1]
    return ssem, rsem, list(outs[2:2 + n_s]), list(outs[2 + n_s:2 + n_s + n_l]), outs[-1]


def _split_wait(srcs, lands, ssem, rsem, after, copies, *, name):
    n_s, n_l = len(srcs), len(lands)

    def body(*refs):
        src_refs, land_refs = refs[:n_s], refs[n_s:n_s + n_l]
        s_ref, r_ref = refs[n_s + n_l], refs[n_s + n_l + 1]
        for outgoing, incoming in copies(src_refs, land_refs, s_ref, r_ref):
            outgoing.wait_send()
            incoming.wait_recv()

    outs = pl.pallas_call(
        body, name=name,
        out_shape=(*[pltpu.HBM(a.shape, a.dtype) for a in srcs], *[pltpu.HBM(a.shape, a.dtype) for a in lands]),
        in_specs=[HBM_SPEC] * (n_s + n_l) + [SEM_SPEC, SEM_SPEC, ANY_SPEC], out_specs=[HBM_SPEC] * (n_s + n_l),
        input_output_aliases={i: i for i in range(n_s + n_l)},
        compiler_params=pltpu.CompilerParams(has_side_effects=EFFECT),
    )(*srcs, *lands, ssem, rsem, after)
    return list(outs[:n_s]), list(outs[n_s:])


PLACE_BLOCK_ELEMS = 512 * 1024


def _place_rows(r, w):
    return _div_tile(r, max(16, PLACE_BLOCK_ELEMS // w), 16)


def _cast_place(shard, chip, axis, after, *, name):
    r, w = shard.shape
    tr = _place_rows(r, w)
    nb = r // tr
    full = (r * N_CHIP, w) if axis == 0 else (r, w * N_CHIP)
    has_after = after is not None

    def body(chip_ref, s_ref, *rest):
        rest[-1][...] = s_ref[...].astype(BF16)

    out_map = (lambda i, ch: (ch[0] * nb + i, 0)) if axis == 0 else (lambda i, ch: (i, ch[0]))
    grid_spec = pltpu.PrefetchScalarGridSpec(
        num_scalar_prefetch=1, grid=(nb,),
        in_specs=[pl.BlockSpec((tr, w), lambda i, ch: (i, 0))] + has_after * [ANY_SPEC],
        out_specs=pl.BlockSpec((tr, w), out_map))
    return pl.pallas_call(body, name=name, grid_spec=grid_spec, out_shape=jax.ShapeDtypeStruct(full, BF16),
                          compiler_params=_cparams(("parallel",)))(chip, shard, *(has_after * [after]))


def _grad_shard_shape(g, axis):
    if g.ndim == 3:
        return g.shape[1], 2 * g.shape[2] // N_CHIP
    return (g.shape[0] // N_CHIP, g.shape[1]) if axis == 0 else (g.shape[0], g.shape[1] // N_CHIP)


def _slot_place(g, ids, axis, *, name):
    r, w = _grad_shard_shape(g, axis)
    tr = _place_rows(r, w)
    nb = r // tr

    def body(ids_ref, g_ref, o_ref):
        o_ref[...] = g_ref[...]

    if g.ndim == 3:
        in_spec = pl.BlockSpec((None, tr, w), lambda i, ids_: (ids_[0] // 2, i, ids_[0] % 2))
    elif axis == 0:
        in_spec = pl.BlockSpec((tr, w), lambda i, ids_: (ids_[0] * nb + i, 0))
    else:
        in_spec = pl.BlockSpec((tr, w), lambda i, ids_: (i, ids_[0]))
    grid_spec = pltpu.PrefetchScalarGridSpec(
        num_scalar_prefetch=1, grid=(nb,), in_specs=[in_spec],
        out_specs=pl.BlockSpec((None, tr, w), lambda i, ids_: (ids_[1], i, 0)))
    return pl.pallas_call(body, name=name, grid_spec=grid_spec, out_shape=jax.ShapeDtypeStruct((N_DEV, r, w), g.dtype),
                          compiler_params=_cparams(("parallel",)))(ids, g)


class _WeightGather:
    def __init__(self, placed, shard_shapes, axes, splits, tag):
        self.placed, self.shard_shapes, self.axes, self.splits, self.tag = list(placed), shard_shapes, axes, splits, tag
        self.n = len(placed)

    def _region(self, land_refs, it, chip, half):
        r, w = self.shard_shapes[it]
        by_rows = self.axes[it] == 0
        if self.splits[it] and half is not None:
            rows = pl.ds(pl.multiple_of(half * (r // 2) + (chip * r if by_rows else 0), 16), r // 2)
        else:
            rows = pl.ds(chip * r if by_rows else 0, r)
        cols = pl.ds(0, w) if by_rows else pl.ds(pl.multiple_of(chip * w, LANE), w)
        return land_refs[it].at[rows, cols]

    def _ici(self, src_refs, land_refs, ssem, rsem):
        x, y, c, others = _mesh_place()
        pairs = []
        for it in range(self.n):
            for j, chip in enumerate(others):
                def mk(chip_from, it=it, j=j, chip=chip):
                    return pltpu.make_async_remote_copy(
                        src_ref=self._region(land_refs, it, 2 * x + y, c), dst_ref=self._region(land_refs, it, chip_from, c),
                        send_sem=ssem.at[3 * it + j], recv_sem=rsem.at[3 * it + j], device_id=(*chip, c),
                        device_id_type=MESH)
                pairs.append((mk(2 * x + y), mk(2 * chip[0] + chip[1])))
        return pairs

    def start(self):
        self.ssem, self.rsem, _, self.lands, token = _split_start(
            [], self.placed, self._ici, 3 * self.n, name="gather_start_" + self.tag)
        return token

    def _d2d(self, src_refs, land_refs, ssem, rsem):
        x, y, c, others = _mesh_place()
        pairs = []
        for it in range(self.n):
            if self.splits[it]:
                for chip in others:
                    def mk(half, it=it, chip=chip, k=len(pairs)):
                        reg = self._region(land_refs, it, 2 * chip[0] + chip[1], half)
                        return pltpu.make_async_remote_copy(src_ref=reg, dst_ref=reg, send_sem=ssem.at[k], recv_sem=rsem.at[k],
                                                            device_id=(x, y, 1 - c), device_id_type=MESH)
                    pairs.append((mk(c), mk(1 - c)))
        return pairs

    def forward(self, after):
        _, lands = _split_wait([], self.lands, self.ssem, self.rsem, after, self._ici,
                               name="gather_wait_" + self.tag)
        self.fsem, self.frsem, _, self.lands, token = _split_start(
            [], lands, self._d2d, 3 * sum(self.splits), name="gather_fwd_" + self.tag)
        return token

    def finish_forward(self, after):
        _, lands = _split_wait([], self.lands, self.fsem, self.frsem, after, self._d2d,
                               name="gather_fwd_wait_" + self.tag)
        return lands

    def finish(self, after):
        _, lands = _split_wait([], self.lands, self.ssem, self.rsem, after, self._ici,
                               name="gather_wait_" + self.tag)
        n = self.n
        n_fwd = 3 * sum(self.splits)
        if n_fwd == 0:
            return lands

        def body(*refs):
            out_refs = refs[n:2 * n]
            fsend, frecv = refs[2 * n:]
            x, y, c, others = _mesh_place()
            sibling = (x, y, 1 - c)

            def fwd(it, slot, chip, half):
                reg = self._region(out_refs, it, 2 * chip[0] + chip[1], half)
                return pltpu.make_async_remote_copy(src_ref=reg, dst_ref=reg, send_sem=fsend.at[slot],
                                                    recv_sem=frecv.at[slot], device_id=sibling, device_id_type=MESH)

            sends, recvs = [], []
            for it in range(n):
                if self.splits[it]:
                    for chip in others:
                        sends.append(fwd(it, len(sends), chip, c))
                        recvs.append(fwd(it, len(recvs), chip, 1 - c))
            for cp in sends:
                cp.start()
            for cp in recvs:
                cp.wait_recv()
            for cp in sends:
                cp.wait_send()

        fulls = pl.pallas_call(
            body, name="gather_d2d_" + self.tag, in_specs=[ANY_SPEC] * n, out_specs=[ANY_SPEC] * n,
            out_shape=[jax.ShapeDtypeStruct(a.shape, a.dtype) for a in lands],
            scratch_shapes=[pltpu.SemaphoreType.DMA((n_fwd,)), pltpu.SemaphoreType.DMA((n_fwd,))],
            input_output_aliases={i: i for i in range(n)},
        )(*lands)
        return list(fulls)


class _GradGather:
    def __init__(self, grads, axes, tag):
        self.grads, self.axes, self.tag = list(grads), axes, tag
        self.n = len(grads)
        self.shard_shapes = [_grad_shard_shape(g, ax) for g, ax in zip(grads, axes)]

    def _piece(self, src_refs, it, chip):
        r, w = self.shard_shapes[it]
        if self.grads[it].ndim == 3:
            return src_refs[it].at[chip // 2, :, pl.ds(pl.multiple_of((chip % 2) * w, LANE), w)]
        if self.axes[it] == 0:
            return src_refs[it].at[pl.ds(pl.multiple_of(chip * r, 16), r), :]
        return src_refs[it].at[:, pl.ds(pl.multiple_of(chip * w, LANE), w)]

    PER_ITEM = 4

    def _remote(self, src_refs, land_refs, ssem, rsem):
        x, y, c, others = _mesh_place()
        me = 4 * x + 2 * y + c
        pairs = []
        for it in range(self.n):
            def mk(k, piece_chip, slot, to, it=it):
                return pltpu.make_async_remote_copy(
                    src_ref=self._piece(src_refs, it, piece_chip), dst_ref=land_refs[it].at[slot],
                    send_sem=ssem.at[self.PER_ITEM * it + k], recv_sem=rsem.at[self.PER_ITEM * it + k], device_id=to,
                    device_id_type=MESH)
            for j, chip in enumerate(others):
                chip_id = 2 * chip[0] + chip[1]
                pairs.append((mk(j, chip_id, me, (*chip, c)), mk(j, chip_id, 2 * chip_id + c, (*chip, c))))
            sibling = (x, y, 1 - c)
            pairs.append((mk(3, 2 * x + y, me, sibling), mk(3, 2 * x + y, 4 * x + 2 * y + 1 - c, sibling)))
        return pairs

    def start(self):
        x, y, c = lax.axis_index("x"), lax.axis_index("y"), lax.axis_index("c")
        ids = jnp.stack([2 * x + y, 4 * x + 2 * y + c]).astype(jnp.int32)
        lands = [_slot_place(g, ids, ax, name="grads_own_%s%d" % (self.tag, it))
                 for it, (g, ax) in enumerate(zip(self.grads, self.axes))]
        self.ssem, self.rsem, self.srcs, self.lands, token = _split_start(
            self.grads, lands, self._remote, self.PER_ITEM * self.n, name="grads_start_" + self.tag)
        return token

    def _forward(self, src_refs, land_refs, ssem, rsem):
        x, y, c, others = _mesh_place()
        pairs = []
        for it in range(self.n):
            for j, ch in enumerate(others):
                def mk(slot, it=it, j=j):
                    return pltpu.make_async_remote_copy(
                        src_ref=land_refs[it].at[slot], dst_ref=land_refs[it].at[slot], send_sem=ssem.at[3 * it + j],
                        recv_sem=rsem.at[3 * it + j], device_id=(x, y, 1 - c), device_id_type=MESH)
                pairs.append((mk(4 * ch[0] + 2 * ch[1] + c), mk(4 * ch[0] + 2 * ch[1] + 1 - c)))
        return pairs

    def forward(self, after):
        _, lands = _split_wait(self.srcs, self.lands, self.ssem, self.rsem, after, self._remote,
                               name="grads_wait_" + self.tag)
        self.fsem, self.frsem, _, self.lands, token = _split_start(
            [], lands, self._forward, 3 * self.n, name="grads_fwd_" + self.tag)
        return token

    def finish(self, after):
        _, lands = _split_wait([], self.lands, self.fsem, self.frsem, after, self._forward,
                               name="grads_fwd_wait_" + self.tag)
        return lands


def _allreduce_small(vec, *, name):
    R, L = vec.shape

    def body(v_ref, o_ref, buf, send, recv, lsem):
        x, y, c, others = _mesh_place()
        me = 4 * x + 2 * y + c
        sibling = (x, y, 1 - c)

        def copy(k, slot, to, src=None):
            return pltpu.make_async_remote_copy(
                src_ref=buf.at[slot] if src is None else src, dst_ref=buf.at[slot], send_sem=send.at[k],
                recv_sem=recv.at[k], device_id=to, device_id_type=MESH)

        def slot_of(chip, core):
            return 4 * chip[0] + 2 * chip[1] + core

        mine = pltpu.make_async_copy(v_ref, buf.at[me], lsem)
        mine.start()
        first = [copy(0, me, sibling, src=v_ref)]
        first += [copy(1 + j, me, (*chip, c), src=v_ref) for j, chip in enumerate(others)]
        for cp in first:
            cp.start()
        passed = [copy(4 + j, slot_of(chip, c), sibling) for j, chip in enumerate(others)]
        for j, chip in enumerate(others):
            copy(1 + j, slot_of(chip, c), (*chip, c)).wait_recv()
            passed[j].start()
        copy(0, slot_of((x, y), 1 - c), sibling).wait_recv()
        for j, chip in enumerate(others):
            copy(4 + j, slot_of(chip, 1 - c), sibling).wait_recv()
        for cp in first + passed:
            cp.wait_send()
        mine.wait()
        total = buf[0]
        for k in range(1, N_DEV):
            total = total + buf[k]
        o_ref[...] = total

    return pl.pallas_call(
        body, name=name, in_specs=[pl.BlockSpec(memory_space=pltpu.VMEM)],
        out_specs=pl.BlockSpec(memory_space=pltpu.VMEM), out_shape=jax.ShapeDtypeStruct((R, L), F32),
        scratch_shapes=[pltpu.VMEM((N_DEV, R, L), F32), pltpu.SemaphoreType.DMA((7,)), pltpu.SemaphoreType.DMA((7,)),
                        pltpu.SemaphoreType.DMA],
        compiler_params=pltpu.CompilerParams(vmem_limit_bytes=VMEM_LIMIT),
    )(vec)


PACK_ALIGN = 1024


def _pack(arrs, row_multiple):
    flat = []
    for a in arrs:
        f = a.reshape(-1).astype(F32)
        flat.append(jnp.pad(f, (0, (-f.shape[0]) % PACK_ALIGN)))
    v = jnp.concatenate(flat)
    v = jnp.pad(v, (0, (-v.shape[0]) % (LANE * row_multiple)))
    return v.reshape(-1, LANE)


def _unpack(v, shapes):
    flat = v.reshape(-1)
    out, off = [], 0
    for s in shapes:
        size = math.prod(s)
        out.append(flat[off:off + size].reshape(s))
        off += size + (-size) % PACK_ALIGN
    return out


def _div_tile(dim, cap, mult=LANE):
    best = None
    for cand in range(mult, min(cap, dim) + 1, mult):
        if dim % cand == 0:
            best = cand
    return dim if best is None else best


WEIGHT_NAMES = ('norm1_g', 'w_in', 'ret_g', 'rg_conv_w', 'rg_conv_b', 'rg_wa', 'rg_ba', 'rg_wx', 'rg_bx', 'rg_lambda',
                'w_out', 'norm2_g', 'norm_mem_g', 'xa_wq', 'xa_wk', 'xa_wv', 'xa_wo', 'norm3_g', 'ffn_w_up',
                'ffn_conv_w', 'ffn_conv_b', 'ffn_w_down', 'final_g')
BIG_AXIS = {'w_in': 1, 'w_out': 0, 'xa_wq': 0, 'xa_wk': 0, 'xa_wv': 0, 'xa_wo': 0, 'ffn_w_up': 1, 'ffn_w_down': 0}
SMALL_SHARDED = ('rg_conv_w', 'ffn_conv_w')


def _step(x, mem, positions, loss_target, W, Mo, Vo):
    S, D = x.shape[1], x.shape[2]
    xs, mems, tgt = x[0], mem[0], loss_target[0]
    n_mem = mems.shape[0]
    pos_col = positions.reshape(S, 1)
    chip = 2 * lax.axis_index("x") + lax.axis_index("y")

    big = list(BIG_AXIS)
    shards = {n: W[n][0] for n in big}
    G = {}
    gather_groups = (('w_in', 'rg_conv_w'), ('w_out', 'xa_wq', 'xa_wk', 'xa_wv', 'xa_wo'),
                     ('ffn_w_up', 'ffn_conv_w'), ('ffn_w_down',))
    gathers, tok = [], None
    chip1 = jnp.reshape(chip, (1,)).astype(jnp.int32)
    for gi, names in enumerate(gather_groups):
        placed = []
        for n in names:
            if n in BIG_AXIS:
                placed.append(_cast_place(shards[n], chip1, BIG_AXIS[n], tok, name="place_" + n))
            else:
                s = W[n][0] if tok is None else W[n][0] + tok[0, 0]
                full = lax.empty((s.shape[0], s.shape[1] * N_CHIP), s.dtype)
                placed.append(lax.dynamic_update_slice(full, s, (0, chip * s.shape[1])))
        ag = _WeightGather(placed, [W[n][0].shape for n in names], [BIG_AXIS.get(n, 1) for n in names],
                           [n in BIG_AXIS for n in names], "g%d" % gi)
        tok = ag.start()
        gathers.append(ag)

    def finish_gather(gi, after):
        G.update(zip(gather_groups[gi], gathers[gi].finish(after)))

    def finish_forward(gi, after):
        G.update(zip(gather_groups[gi], gathers[gi].finish_forward(after)))

    R = W['ret_g'].shape[1]
    Wl = W['rg_lambda'].shape[1]
    IN = W['w_in'].shape[2] * N_CHIP
    F2 = W['ffn_w_up'].shape[2] * N_CHIP
    F = F2 // 2

    norm1_g, norm2_g, norm3_g = W['norm1_g'] + tok[0, 0], W['norm2_g'], W['norm3_g']
    norm_mem_g, final_g, ret_g = W['norm_mem_g'], W['final_g'].reshape(1, D), W['ret_g']
    rg_cb = W['rg_conv_b']
    wa, wx = W['rg_wa'][0], W['rg_wx'][0]
    ba, bx = W['rg_ba'].reshape(1, Wl), W['rg_bx'].reshape(1, Wl)
    lam = W['rg_lambda']
    ffn_cb = W['ffn_conv_b']

    def fwd_mm(a, wname, N, K, **kw):
        return _mm(a, G[wname], mode="nn", M=a.shape[0], N=N, K=K, tm=_div_tile(a.shape[0], 1024),
                   tn=_div_tile(N, 2048, 512) if K <= 3072 else 512, tk=K, **kw)

    def fwd_mm_norm(a, wname, res, g, name):
        return _mm(a, G[wname], mode="nn", M=a.shape[0], N=D, K=a.shape[1], tm=512, tn=D, tk=_div_tile(a.shape[1], 2048),
                   out_dtype=F32, res=res, norm_g=g, name=name)

    def bwd_x_mm(d, wname, N, K, **kw):
        return _mm(d, G[wname], mode="nt", M=d.shape[0], N=N, K=K, tm=_div_tile(d.shape[0], 1024),
                   tn=_div_tile(N, 2048 if K <= 3072 else 512, 256), tk=K, **kw)

    def bwd_w_mm(a, d, M, N, **kw):
        Ks = a.shape[0]
        return _mm(a, d, mode="tn", M=M, N=N, K=Ks, out_dtype=BF16, tm=_div_tile(M, 1024, 256),
                   tn=_div_tile(N, 1024, 256), tk=_div_tile(Ks, 4096 if d.dtype == BF16 else 1024), **kw)

    xn1 = _rmsnorm_fwd(xs, norm1_g, name="norm1_fwd")
    half = (R // RET_HEADS) // 2
    inv = (ROPE_BASE ** (-jnp.arange(half, dtype=F32) / half)).reshape(1, half)
    cos, sin = _rope_table(pos_col, inv + tok[0, 0], name="rope_table")
    finish_gather(0, cos)
    rg_cw = G['rg_conv_w']
    h = fwd_mm(xn1, 'w_in', IN, D, out_dtype=F32, name="mm_in")
    hl, mix = _lru_fwd(h, rg_cw, rg_cb, wa, ba, wx, bx, lam, name="lru_fwd")
    t1 = gathers[1].forward(hl)
    ret_raw, states, mix = _ret_fwd(h, cos, sin, ret_g + t1[0, 0], mix, name="ret_fwd")
    finish_forward(1, mix)
    x1, xn2 = fwd_mm_norm(mix, 'w_out', xs, norm2_g, "mm_out")
    memn = _rmsnorm_fwd(mems, norm_mem_g, name="norm_mem_fwd")
    km = fwd_mm(memn, 'xa_wk', D, D, out_dtype=BF16, name="mm_k")
    vm = fwd_mm(memn, 'xa_wv', D, D, out_dtype=BF16, name="mm_v")
    t2 = gathers[2].forward(x1)
    q = fwd_mm(xn2, 'xa_wq', D, D, out_dtype=BF16, after=t2, name="mm_q")
    o = _xattn_fwd(q, km, vm, name="xattn_fwd")
    x2, xn3 = fwd_mm_norm(o, 'xa_wo', x1, norm3_g, "mm_o")
    finish_forward(2, xn3)
    t3 = gathers[3].forward(xn3)
    ffn_cw = G['ffn_conv_w']
    act, hh_a, hh_b, hc_a, hc_b = _ffn_up_gate(xn3, G['ffn_w_up'], ffn_cw, ffn_cb + t3[0, 0], name="ffn_up_gate")
    finish_forward(3, act)
    x3 = fwd_mm(act, 'ffn_w_down', D, F, out_dtype=F32, res=x2, name="mm_down")
    dx3, d_final, loss8, dx3h = _final_loss(x3, tgt, final_g, name="final_loss")

    gw = {}
    grad_groups = []

    def start_grads(names, tag):
        gg = _GradGather([gw[n] for n in names], [BIG_AXIS[n] for n in names], tag)
        grad_groups.append((names, gg))
        return gg.start()

    dhh_a, dhh_b, gcw_a, gcw_b, gw['ffn_w_down'], gw['ffn_w_up'] = _ffn_bwd(
        dx3h, G['ffn_w_down'], hh_a, hh_b, hc_a, hc_b, act, xn3, ffn_cw, name="ffn_bwd")
    tok_a = start_grads(('ffn_w_down', 'ffn_w_up'), "a")
    dxn3 = bwd_x_mm(dhh_a, 'ffn_w_up', D, F, out_dtype=F32, after=tok_a, name="mm_dxn3_a")
    dxn3 = bwd_x_mm(dhh_b, 'ffn_w_up', D, F, out_dtype=BF16, b_off=(0, F), res=dxn3, name="mm_dxn3_b")
    dx2, d_norm3, dx2h = _rmsnorm_bwd(x2, dxn3, norm3_g, dx3, name="norm3_bwd", emit_bf16=True)
    Kc = ffn_cw.shape[0]
    d_ffn_cw = jnp.concatenate([gcw_a[:Kc], gcw_b[:Kc]], axis=1)
    d_ffn_cb = jnp.concatenate([gcw_a[Kc:Kc + 1], gcw_b[Kc:Kc + 1]], axis=1)

    d_o = bwd_x_mm(dx2h, 'xa_wo', D, D, out_dtype=BF16, name="mm_do")
    gw['xa_wo'] = bwd_w_mm(o, dx2h, D, D, name="mm_dw_o")
    dq, dk, dv = _xattn_bwd(q, km, vm, d_o, name="xattn_bwd")
    gw['xa_wq'] = bwd_w_mm(xn2, dq, D, D, name="mm_dw_q")
    dxn2 = bwd_x_mm(dq, 'xa_wq', D, D, out_dtype=BF16, name="mm_dxn2")
    gw['xa_wk'] = bwd_w_mm(memn, dk, D, D, name="mm_dw_k")
    gw['xa_wv'] = bwd_w_mm(memn, dv, D, D, name="mm_dw_v")
    dmemn = bwd_x_mm(dk, 'xa_wk', D, D, out_dtype=F32, name="mm_dmem_k")
    dmemn = bwd_x_mm(dv, 'xa_wv', D, D, out_dtype=F32, res=dmemn, name="mm_dmem_v")
    _, d_norm_mem = _rmsnorm_bwd(mems, dmemn, norm_mem_g, None, name="norm_mem_bwd")
    dx1, d_norm2, dx1h = _rmsnorm_bwd(x1, dxn2, norm2_g, dx2, name="norm2_bwd", emit_bf16=True)

    gw['w_out'] = bwd_w_mm(mix, dx1h, D, D, name="mm_dw_out")
    tok_b = start_grads(('xa_wo', 'xa_wq', 'xa_wk', 'xa_wv', 'w_out'), "b")
    dmix = bwd_x_mm(dx1h, 'w_out', D, D, out_dtype=BF16, after=tok_b, name="mm_dmix")
    dh, d_ret_g = _ret_bwd(h, cos, sin, ret_g, states, ret_raw, dmix, name="ret_bwd")
    dh, d_rcw, d_rcb, d_wa, d_ba, d_wx, d_bx, d_lam = _lru_bwd(
        h, hl, dmix, dh, rg_cw, rg_cb, wa, ba, wx, bx, lam, name="lru_bwd")
    gw['w_in'] = bwd_w_mm(xn1, dh, D, IN, name="mm_dw_in")
    tok_c = start_grads(('w_in',), "c")
    dxn1 = bwd_x_mm(dh, 'w_in', D, IN, out_dtype=BF16, after=tok_c, name="mm_dxn1")
    grad_x, d_norm1 = _rmsnorm_bwd(xs, dxn1, norm1_g, dx1, name="norm1_bwd")

    small_parts = {
        'norm1_g': d_norm1, 'ret_g': d_ret_g, 'rg_conv_w': d_rcw[:rg_cw.shape[0]], 'rg_conv_b': d_rcb,
        'rg_wa': d_wa, 'rg_ba': d_ba, 'rg_wx': d_wx, 'rg_bx': d_bx, 'rg_lambda': d_lam, 'norm2_g': d_norm2,
        'norm_mem_g': d_norm_mem, 'norm3_g': d_norm3, 'ffn_conv_w': d_ffn_cw, 'ffn_conv_b': d_ffn_cb,
        'final_g': d_final}
    small = [n for n in WEIGHT_NAMES if n not in BIG_AXIS]
    red_shapes = [(1,)] + [tuple(small_parts[n].shape) for n in small]
    fwd_tok = sum(gg.forward(d_norm1)[0:1, 0:1] for _, gg in grad_groups)
    reduced = _allreduce_small(_pack([loss8[0:1, 0:1] + fwd_tok] + [small_parts[n] for n in small], SUB),
                               name="allreduce_small")
    red = _unpack(reduced, red_shapes)
    loss = red[0][0]
    g_small = dict(zip(small, red[1:]))
    for n in SMALL_SHARDED:
        w_local = W[n].shape[-1]
        g_small[n] = lax.dynamic_slice_in_dim(g_small[n], chip * w_local, w_local, axis=1)

    out_g, out_d, out_m, out_v = {}, {}, {}, {}
    rows = 512
    pk = lambda d: _pack([d[n] for n in small], rows)
    g_pack = _pack([g_small[n] for n in small], rows)
    res_small = _adamw(pk(W), pk(Mo), pk(Vo), g_pack[None], name="adamw_small")
    shapes_small = [tuple(W[n].shape) for n in small]
    for dst, packed in zip((out_g, out_d, out_m, out_v), res_small):
        for n, val in zip(small, _unpack(packed, shapes_small)):
            dst[n] = val
    last = res_small[0]
    for names, gg in grad_groups:
        for n, land in zip(names, gg.finish(last)):
            g, d, m_new, v_new = _adamw(shards[n], Mo[n][0], Vo[n][0], land, name="adamw_" + n)
            out_g[n], out_d[n], out_m[n], out_v[n] = (t.reshape(W[n].shape) for t in (g, d, m_new, v_new))
            last = g
    return (loss, grad_x[None], *[out_g[n] for n in WEIGHT_NAMES], *[out_d[n] for n in WEIGHT_NAMES],
            *[out_m[n] for n in WEIGHT_NAMES], *[out_v[n] for n in WEIGHT_NAMES])


def kernel(x, mem, positions, norm1_g, w_in, ret_g, rg_conv_w, rg_conv_b, rg_wa, rg_ba, rg_wx, rg_bx, rg_lambda, w_out, norm2_g, norm_mem_g, xa_wq, xa_wk, xa_wv, xa_wo, norm3_g, ffn_w_up, ffn_conv_w, ffn_conv_b, ffn_w_down, final_g, loss_target, m_norm1_g, m_w_in, m_ret_g, m_rg_conv_w, m_rg_conv_b, m_rg_wa, m_rg_ba, m_rg_wx, m_rg_bx, m_rg_lambda, m_w_out, m_norm2_g, m_norm_mem_g, m_xa_wq, m_xa_wk, m_xa_wv, m_xa_wo, m_norm3_g, m_ffn_w_up, m_ffn_conv_w, m_ffn_conv_b, m_ffn_w_down, m_final_g, v_norm1_g, v_w_in, v_ret_g, v_rg_conv_w, v_rg_conv_b, v_rg_wa, v_rg_ba, v_rg_wx, v_rg_bx, v_rg_lambda, v_w_out, v_norm2_g, v_norm_mem_g, v_xa_wq, v_xa_wk, v_xa_wv, v_xa_wo, v_norm3_g, v_ffn_w_up, v_ffn_conv_w, v_ffn_conv_b, v_ffn_w_down, v_final_g):
    W = dict(zip(WEIGHT_NAMES, (norm1_g, w_in, ret_g, rg_conv_w, rg_conv_b, rg_wa, rg_ba, rg_wx, rg_bx, rg_lambda, w_out,
                                norm2_g, norm_mem_g, xa_wq, xa_wk, xa_wv, xa_wo, norm3_g, ffn_w_up, ffn_conv_w,
                                ffn_conv_b, ffn_w_down, final_g)))
    Mo = dict(zip(WEIGHT_NAMES, (m_norm1_g, m_w_in, m_ret_g, m_rg_conv_w, m_rg_conv_b, m_rg_wa, m_rg_ba, m_rg_wx, m_rg_bx,
                                 m_rg_lambda, m_w_out, m_norm2_g, m_norm_mem_g, m_xa_wq, m_xa_wk, m_xa_wv, m_xa_wo,
                                 m_norm3_g, m_ffn_w_up, m_ffn_conv_w, m_ffn_conv_b, m_ffn_w_down, m_final_g)))
    Vo = dict(zip(WEIGHT_NAMES, (v_norm1_g, v_w_in, v_ret_g, v_rg_conv_w, v_rg_conv_b, v_rg_wa, v_rg_ba, v_rg_wx, v_rg_bx,
                                 v_rg_lambda, v_w_out, v_norm2_g, v_norm_mem_g, v_xa_wq, v_xa_wk, v_xa_wv, v_xa_wo,
                                 v_norm3_g, v_ffn_w_up, v_ffn_conv_w, v_ffn_conv_b, v_ffn_w_down, v_final_g)))
    return _step(x, mem, positions, loss_target, W, Mo, Vo)
```

```python
import math

import jax
import jax.numpy as jnp
from jax import lax
from jax.experimental import pallas as pl
from jax.experimental.pallas import tpu as pltpu

F32 = jnp.float32
BF16 = jnp.bfloat16

EPS = 1e-6
RET_HEADS = 4
RET_CHUNK = 128
ROPE_BASE = 10000.0
LRU_BLOCKS = 8
LRU_C = 8.0
XA_HEADS = 4

ADAM_LR = 0.001
ADAM_B1 = 0.9
ADAM_B2 = 0.999
ADAM_EPS = 1e-08
ADAM_WD = 0.01
ADAM_STEP = 10

N_DEV = 8
N_CHIP = 4
MESH = pl.DeviceIdType.MESH
SUB = 8
LANE = 128
VMEM_LIMIT = 56 * 1024 * 1024

NN = ((1,), (0,))
NT = ((1,), (1,))
TN = ((0,), (0,))


def _cparams(sem):
    return pltpu.CompilerParams(dimension_semantics=sem, vmem_limit_bytes=VMEM_LIMIT)


def _sigmoid(v):
    return 1.0 / (1.0 + jnp.exp(-v))


def _bdot(a, b, dims):
    return lax.dot_general(a.astype(BF16), b.astype(BF16), (dims, ((), ())), preferred_element_type=F32)


def _row_iota(shape):
    return lax.broadcasted_iota(jnp.int32, shape, 0)


def _shift_down(v, tail, k):
    if k == 0:
        return v
    r = pltpu.roll(v, k, 0)
    rt = pltpu.roll(tail, k, 0)
    first = jnp.where(_row_iota(rt.shape) < k, rt, r[0:SUB])
    return jnp.concatenate([first, r[SUB:]], axis=0)


def _shift_up(v, head, k):
    if k == 0:
        return v
    n = v.shape[0]
    r = pltpu.roll(v, n - k, 0)
    rh = pltpu.roll(head, SUB - k, 0)
    last = jnp.where(_row_iota(rh.shape) >= SUB - k, rh, r[n - SUB:n])
    return jnp.concatenate([r[:n - SUB], last], axis=0)


def _mm(a, b, *, mode, M, N, K, out_dtype, name, tm=512, tn=512, tk=512, b_off=(0, 0), res=None, norm_g=None,
        after=None):
    tm, tn, tk = min(tm, M), min(tn, N), min(tk, K)
    assert M % tm == 0 and N % tn == 0 and K % tk == 0, (name, M, N, K, tm, tn, tk)
    nk = K // tk
    if mode == "nn":
        a_blk, b_blk, dims = (tm, tk), (tk, tn), NN
        a_map = lambda i, j, k: (i, k)
        b_map = lambda i, j, k: (k + b_off[0] // tk, j + b_off[1] // tn)
    elif mode == "nt":
        a_blk, b_blk, dims = (tm, tk), (tn, tk), NT
        a_map = lambda i, j, k: (i, k)
        b_map = lambda i, j, k: (j + b_off[0] // tn, k + b_off[1] // tk)
    else:
        a_blk, b_blk, dims = (tk, tm), (tk, tn), TN
        a_map = lambda i, j, k: (k, i)
        b_map = lambda i, j, k: (k + b_off[0] // tk, j + b_off[1] // tn)
    assert b_off[0] % b_blk[0] == 0 and b_off[1] % b_blk[1] == 0, (name, b_off, b_blk)
    has_res, has_norm, has_after = res is not None, norm_g is not None, after is not None
    assert not has_norm or tn == N

    def body(*refs):
        refs = list(refs)
        a_ref, b_ref = refs[0], refs[1]
        pos = 2
        r_ref = g_ref = n_ref = None
        if has_res:
            r_ref = refs[pos]
            pos += 1
        if has_norm:
            g_ref = refs[pos]
            pos += 1
        pos += has_after
        o_ref = refs[pos]
        pos += 1
        if has_norm:
            n_ref = refs[pos]
            pos += 1
        acc = refs[pos] if nk > 1 else None
        k = pl.program_id(2)
        part = _bdot(a_ref[...], b_ref[...], dims)

        def finish(total):
            if has_res:
                total = total + r_ref[...].astype(F32)
            o_ref[...] = total.astype(o_ref.dtype)
            if has_norm:
                r = lax.rsqrt(jnp.mean(total * total, axis=-1, keepdims=True) + EPS)
                n_ref[...] = (total * r * g_ref[...]).astype(n_ref.dtype)

        if nk == 1:
            finish(part)
        else:
            @pl.when(k == 0)
            def _():
                acc[...] = part

            @pl.when(k > 0)
            def _():
                acc[...] += part

            @pl.when(k == nk - 1)
            def _():
                finish(acc[...])

    in_specs = [pl.BlockSpec(a_blk, a_map), pl.BlockSpec(b_blk, b_map)]
    args = [a, b]
    if has_res:
        in_specs.append(pl.BlockSpec((tm, tn), lambda i, j, k: (i, j)))
        args.append(res)
    if has_norm:
        in_specs.append(pl.BlockSpec((1, N), lambda i, j, k: (0, 0)))
        args.append(norm_g)
    if has_after:
        in_specs.append(pl.BlockSpec(memory_space=pl.ANY))
        args.append(after)
    out_shape = jax.ShapeDtypeStruct((M, N), out_dtype)
    out_specs = pl.BlockSpec((tm, tn), lambda i, j, k: (i, j))
    if has_norm:
        out_shape = [out_shape, jax.ShapeDtypeStruct((M, N), BF16)]
        out_specs = [out_specs, pl.BlockSpec((tm, tn), lambda i, j, k: (i, j))]
    return pl.pallas_call(
        body, name=name, grid=(M // tm, N // tn, nk), in_specs=in_specs,
        out_specs=out_specs, out_shape=out_shape,
        scratch_shapes=[pltpu.VMEM((tm, tn), F32)] if nk > 1 else [],
        compiler_params=_cparams(("parallel", "parallel", "arbitrary")),
    )(*args)


def _rmsnorm_fwd(x, g, *, name, ts=1024):
    S, D = x.shape
    ts = min(ts, S)

    def body(x_ref, g_ref, o_ref):
        xv = x_ref[...]
        r = lax.rsqrt(jnp.mean(xv * xv, axis=-1, keepdims=True) + EPS)
        o_ref[...] = (xv * r * g_ref[...]).astype(o_ref.dtype)

    return pl.pallas_call(
        body, name=name, grid=(S // ts,),
        in_specs=[pl.BlockSpec((ts, D), lambda i: (i, 0)), pl.BlockSpec((1, D), lambda i: (0, 0))],
        out_specs=pl.BlockSpec((ts, D), lambda i: (i, 0)),
        out_shape=jax.ShapeDtypeStruct((S, D), BF16),
        compiler_params=_cparams(("parallel",)),
    )(x, g)


def _rmsnorm_bwd(x, dxn, g, res, *, name, ts=512, emit_bf16=False):
    S, D = x.shape
    ts = min(ts, S)
    has_res = res is not None

    def body(*refs):
        refs = list(refs)
        dx16_ref = refs.pop() if emit_bf16 else None
        if has_res:
            x_ref, d_ref, g_ref, r_ref, dx_ref, dg_ref = refs
        else:
            x_ref, d_ref, g_ref, dx_ref, dg_ref = refs
        i = pl.program_id(0)
        xv = x_ref[...]
        dv = d_ref[...].astype(F32)
        r = lax.rsqrt(jnp.mean(xv * xv, axis=-1, keepdims=True) + EPS)
        gd = dv * g_ref[...]
        proj = jnp.mean(xv * gd, axis=-1, keepdims=True)
        dx = r * gd - xv * (r * r * r) * proj
        if has_res:
            dx = dx + r_ref[...]
        dx_ref[...] = dx
        if emit_bf16:
            dx16_ref[...] = dx.astype(BF16)
        part = jnp.sum(dv * xv * r, axis=0, keepdims=True)

        @pl.when(i == 0)
        def _():
            dg_ref[...] = part

        @pl.when(i > 0)
        def _():
            dg_ref[...] += part

    row = pl.BlockSpec((ts, D), lambda i: (i, 0))
    vec = pl.BlockSpec((1, D), lambda i: (0, 0))
    in_specs = [row, row, vec] + ([row] if has_res else [])
    args = [x, dxn, g] + ([res] if has_res else [])
    extra = emit_bf16 * [jax.ShapeDtypeStruct((S, D), BF16)]
    return pl.pallas_call(
        body, name=name, grid=(S // ts,), in_specs=in_specs, out_specs=[row, vec] + emit_bf16 * [row],
        out_shape=[jax.ShapeDtypeStruct((S, D), F32), jax.ShapeDtypeStruct((1, D), F32)] + extra,
        compiler_params=_cparams(("arbitrary",)),
    )(*args)


def _final_loss(x, target, g, *, name, ts=512):
    S, D = x.shape
    ts = min(ts, S)

    def body(x_ref, t_ref, g_ref, dx_ref, dg_ref, loss_ref, dx16_ref):
        i = pl.program_id(0)
        xv = x_ref[...]
        gv = g_ref[...]
        r = lax.rsqrt(jnp.mean(xv * xv, axis=-1, keepdims=True) + EPS)
        y = xv * r * gv
        err = y - t_ref[...]
        row_loss = jnp.mean(err * err, axis=-1, keepdims=True)
        lpart = 0.5 * jnp.sum(row_loss, axis=0, keepdims=True)
        dy = err * (1.0 / D)
        gd = dy * gv
        proj = jnp.mean(xv * gd, axis=-1, keepdims=True)
        dx = r * gd - xv * (r * r * r) * proj
        dx_ref[...] = dx
        dx16_ref[...] = dx.astype(BF16)
        part = jnp.sum(dy * xv * r, axis=0, keepdims=True)
        lfull = jnp.broadcast_to(lpart, loss_ref.shape)

        @pl.when(i == 0)
        def _():
            dg_ref[...] = part
            loss_ref[...] = lfull

        @pl.when(i > 0)
        def _():
            dg_ref[...] += part
            loss_ref[...] += lfull

    row = pl.BlockSpec((ts, D), lambda i: (i, 0))
    vec = pl.BlockSpec((1, D), lambda i: (0, 0))
    return pl.pallas_call(
        body, name=name, grid=(S // ts,), in_specs=[row, row, vec],
        out_specs=[row, vec, pl.BlockSpec((SUB, LANE), lambda i: (0, 0)), row],
        out_shape=[jax.ShapeDtypeStruct((S, D), F32), jax.ShapeDtypeStruct((1, D), F32),
                   jax.ShapeDtypeStruct((SUB, LANE), F32), jax.ShapeDtypeStruct((S, D), BF16)],
        compiler_params=_cparams(("arbitrary",)),
    )(x, target, g)


def _rope_table(pos_col, inv, *, name, ts=1024):
    S = pos_col.shape[0]
    ts = min(ts, S)
    half = inv.shape[1]

    def body(p_ref, inv_ref, c_ref, s_ref):
        ang = p_ref[...].astype(F32) * inv_ref[...]
        c_ref[...] = jnp.cos(ang)
        s_ref[...] = jnp.sin(ang)

    tab = pl.BlockSpec((ts, half), lambda i: (i, 0))
    return pl.pallas_call(
        body, name=name, grid=(S // ts,),
        in_specs=[pl.BlockSpec((ts, 1), lambda i: (i, 0)), pl.BlockSpec((1, half), lambda i: (0, 0))],
        out_specs=[tab, tab],
        out_shape=[jax.ShapeDtypeStruct((S, half), F32), jax.ShapeDtypeStruct((S, half), F32)],
        compiler_params=_cparams(("parallel",)),
    )(pos_col, inv)


def _ret_consts(C, log_g):
    ii = lax.broadcasted_iota(jnp.int32, (C, C), 0)
    jj = lax.broadcasted_iota(jnp.int32, (C, C), 1)
    diff = (ii - jj).astype(F32)
    intra = jnp.where(ii >= jj, jnp.exp(log_g * jnp.maximum(diff, 0.0)), 0.0)
    idx = lax.broadcasted_iota(jnp.int32, (C, 1), 0).astype(F32)
    qd = jnp.exp(log_g * (idx + 1.0))
    kd = jnp.exp(log_g * (C - 1.0 - idx))
    cd = math.exp(log_g * C)
    return intra, qd, kd, cd


def _rot(t, cs, sn):
    half = t.shape[-1] // 2
    t1, t2 = t[:, :half], t[:, half:]
    return jnp.concatenate([t1 * cs - t2 * sn, t1 * sn + t2 * cs], axis=-1)


def _unrot(d, cs, sn):
    half = d.shape[-1] // 2
    d1, d2 = d[:, :half], d[:, half:]
    return jnp.concatenate([d1 * cs + d2 * sn, d2 * cs - d1 * sn], axis=-1)


def _ret_fwd(h, cos, sin, ret_g, mix, *, name, ch=4):
    S = h.shape[0]
    R = ret_g.shape[1]
    H, C = RET_HEADS, RET_CHUNK
    Dh = R // H
    ts = ch * C
    assert S % ts == 0
    log_gs = [math.log(1.0 - 2.0 ** (-5.0 - hd)) for hd in range(H)]
    scale = Dh ** -0.5

    def body(x_ref, c_ref, s_ref, rg_ref, mix_in, ret_ref, st_ref, mix_ref, state):
        i = pl.program_id(0)

        @pl.when(i == 0)
        def _():
            state[...] = jnp.zeros_like(state)

        for c in range(ch):
            rows = pl.ds(c * C, C)
            cs, sn = c_ref[rows, :], s_ref[rows, :]
            for hd in range(H):
                intra, qd, kd, cd = _ret_consts(C, log_gs[hd])
                q = x_ref[rows, pl.ds(hd * Dh, Dh)]
                k = x_ref[rows, pl.ds(R + hd * Dh, Dh)]
                v = x_ref[rows, pl.ds(2 * R + hd * Dh, Dh)]
                g = x_ref[rows, pl.ds(3 * R + hd * Dh, Dh)]
                rq = _rot(q, cs, sn)
                rk = _rot(k, cs, sn) * scale
                st = state[hd]
                st_ref[c, hd] = st.astype(BF16)
                s_ = _bdot(rq, rk, NT) * intra
                ret = _bdot(s_, v, NN) + _bdot(rq * qd, st, NN)
                state[hd] = st * cd + _bdot(rk * kd, v, TN)
                ret_ref[rows, pl.ds(hd * Dh, Dh)] = ret
                rr = lax.rsqrt(jnp.mean(ret * ret, axis=-1, keepdims=True) + EPS)
                out = ret * rr * rg_ref[:, pl.ds(hd * Dh, Dh)] * (g * _sigmoid(g))
                mix_ref[rows, pl.ds(hd * Dh, Dh)] = out.astype(BF16)

    n_chunks = S // C
    return pl.pallas_call(
        body, name=name, grid=(S // ts,),
        in_specs=[pl.BlockSpec((ts, 4 * R), lambda i: (i, 0)),
                  pl.BlockSpec((ts, Dh // 2), lambda i: (i, 0)), pl.BlockSpec((ts, Dh // 2), lambda i: (i, 0)),
                  pl.BlockSpec((1, R), lambda i: (0, 0)), pl.BlockSpec(memory_space=pl.ANY)],
        out_specs=[pl.BlockSpec((ts, R), lambda i: (i, 0)),
                   pl.BlockSpec((ch, H, Dh, Dh), lambda i: (i, 0, 0, 0)),
                   pl.BlockSpec((ts, R), lambda i: (i, 0))],
        out_shape=[jax.ShapeDtypeStruct((S, R), F32), jax.ShapeDtypeStruct((n_chunks, H, Dh, Dh), BF16),
                   jax.ShapeDtypeStruct(mix.shape, mix.dtype)],
        scratch_shapes=[pltpu.VMEM((H, Dh, Dh), F32)],
        input_output_aliases={4: 2},
        compiler_params=_cparams(("arbitrary",)),
    )(h, cos, sin, ret_g, mix)


def _ret_bwd(h, cos, sin, ret_g, states, ret_raw, dmix, *, name, ch=4):
    S = h.shape[0]
    R = ret_g.shape[1]
    H, C = RET_HEADS, RET_CHUNK
    Dh = R // H
    ts = ch * C
    nb = S // ts
    log_gs = [math.log(1.0 - 2.0 ** (-5.0 - hd)) for hd in range(H)]
    scale = Dh ** -0.5

    def body(x_ref, c_ref, s_ref, rg_ref, st_ref, ret_ref, dm_ref, dh_ref, drg_ref, dstate):
        i = pl.program_id(0)

        @pl.when(i == 0)
        def _():
            dstate[...] = jnp.zeros_like(dstate)
            drg_ref[...] = jnp.zeros_like(drg_ref)

        for c in reversed(range(ch)):
            rows = pl.ds(c * C, C)
            cs, sn = c_ref[rows, :], s_ref[rows, :]
            for hd in range(H):
                intra, qd, kd, cd = _ret_consts(C, log_gs[hd])
                cols = pl.ds(hd * Dh, Dh)
                q = x_ref[rows, pl.ds(hd * Dh, Dh)]
                k = x_ref[rows, pl.ds(R + hd * Dh, Dh)]
                v = x_ref[rows, pl.ds(2 * R + hd * Dh, Dh)]
                g = x_ref[rows, pl.ds(3 * R + hd * Dh, Dh)]
                rq = _rot(q, cs, sn)
                rk = _rot(k, cs, sn) * scale
                ret = ret_ref[rows, cols]
                dm = dm_ref[rows, cols].astype(F32)
                rgv = rg_ref[:, cols]
                rr = lax.rsqrt(jnp.mean(ret * ret, axis=-1, keepdims=True) + EPS)
                retn = ret * rr
                sg = _sigmoid(g)
                silu = g * sg
                drg_ref[:, cols] += jnp.sum(dm * retn * silu, axis=0, keepdims=True)
                dg = dm * retn * rgv * (sg * (1.0 + g * (1.0 - sg)))
                dretn = dm * rgv * silu
                d_o = rr * dretn - ret * (rr * rr * rr) * jnp.mean(ret * dretn, axis=-1, keepdims=True)
                st = st_ref[c, hd]
                d_s = dstate[hd]
                a_ = _bdot(rq, rk, NT) * intra
                d_a = _bdot(d_o, v, NT) * intra
                d_qr = _bdot(d_a, rk, NN) + _bdot(d_o, st, NT) * qd
                d_kr = _bdot(d_a, rq, TN) + _bdot(v, d_s, NT) * kd
                d_v = _bdot(a_, d_o, TN) + _bdot(rk * kd, d_s, NN)
                dstate[hd] = d_s * cd + _bdot(rq * qd, d_o, TN)
                dh_ref[rows, pl.ds(hd * Dh, Dh)] = _unrot(d_qr, cs, sn).astype(BF16)
                dh_ref[rows, pl.ds(R + hd * Dh, Dh)] = (_unrot(d_kr, cs, sn) * scale).astype(BF16)
                dh_ref[rows, pl.ds(2 * R + hd * Dh, Dh)] = d_v.astype(BF16)
                dh_ref[rows, pl.ds(3 * R + hd * Dh, Dh)] = dg.astype(BF16)

    rb = lambda i: nb - 1 - i
    return pl.pallas_call(
        body, name=name, grid=(nb,),
        in_specs=[pl.BlockSpec((ts, 4 * R), lambda i: (rb(i), 0)),
                  pl.BlockSpec((ts, Dh // 2), lambda i: (rb(i), 0)), pl.BlockSpec((ts, Dh // 2), lambda i: (rb(i), 0)),
                  pl.BlockSpec((1, R), lambda i: (0, 0)),
                  pl.BlockSpec((ch, H, Dh, Dh), lambda i: (rb(i), 0, 0, 0)),
                  pl.BlockSpec((ts, R), lambda i: (rb(i), 0)),
                  pl.BlockSpec((ts, R), lambda i: (rb(i), 0))],
        out_specs=[pl.BlockSpec((ts, 4 * R), lambda i: (rb(i), 0)), pl.BlockSpec((1, R), lambda i: (0, 0))],
        out_shape=[jax.ShapeDtypeStruct((S, 6 * R), BF16), jax.ShapeDtypeStruct((1, R), F32)],
        scratch_shapes=[pltpu.VMEM((H, Dh, Dh), F32)],
        compiler_params=_cparams(("arbitrary",)),
    )(h, cos, sin, ret_g, states, ret_raw, dmix)


GELU_C = math.sqrt(2.0 / math.pi)
GELU_A = 0.044715


def _gelu_parts(y):
    t = jnp.tanh(GELU_C * (y + GELU_A * y * y * y))
    val = 0.5 * y * (1.0 + t)
    grad = 0.5 * (1.0 + t) + 0.5 * y * (1.0 - t * t) * GELU_C * (1.0 + 3.0 * GELU_A * y * y)
    return val, grad


def _neg_expm1(x):
    series = -x * (1.0 + x * (1.0 / 2.0) * (1.0 + x * (1.0 / 3.0) * (1.0 + x * (1.0 / 4.0) * (
        1.0 + x * (1.0 / 5.0) * (1.0 + x * (1.0 / 6.0) * (1.0 + x * (1.0 / 7.0)))))))
    return jnp.where(x > -0.35, series, 1.0 - jnp.exp(x))


def _log_sigmoid(x):
    return jnp.minimum(x, 0.0) - jnp.log1p(jnp.exp(-jnp.abs(x)))


def _lru_gates(uc, wa_ref, ba_ref, wx_ref, bx_ref):
    nbk = wa_ref.shape[0]
    bd = wa_ref.shape[1]
    rs, gs = [], []
    for n in range(nbk):
        ucn = uc[:, n * bd:(n + 1) * bd]
        rs.append(_sigmoid(_bdot(ucn, wa_ref[n], NN) + ba_ref[:, pl.ds(n * bd, bd)]))
        gs.append(_sigmoid(_bdot(ucn, wx_ref[n], NN) + bx_ref[:, pl.ds(n * bd, bd)]))
    return jnp.concatenate(rs, axis=-1), jnp.concatenate(gs, axis=-1)


def _lru_fwd(h, conv_w, conv_b, wa, ba, wx, bx, lam, *, name, ts=256):
    S = h.shape[0]
    W = lam.shape[1]
    K = conv_w.shape[0]
    ts = min(ts, S)

    def body(u_ref, y_ref, cw_ref, cb_ref, wa_ref, ba_ref, wx_ref, bx_ref, lam_ref, hl_ref, mix_ref, tail, hlast):
        i = pl.program_id(0)

        @pl.when(i == 0)
        def _():
            tail[...] = jnp.zeros_like(tail)
            hlast[...] = jnp.zeros_like(hlast)

        u = u_ref[...]
        tl = tail[...]
        uc = cb_ref[...] + cw_ref[K - 1:K, :] * u
        for k in range(K - 1):
            uc = uc + cw_ref[k:k + 1, :] * _shift_down(u, tl, K - 1 - k)
        tail[...] = u[ts - SUB:ts]
        r, ig = _lru_gates(uc, wa_ref, ba_ref, wx_ref, bx_ref)
        log_a = LRU_C * r * _log_sigmoid(lam_ref[...])
        a = jnp.exp(log_a)
        b = jnp.sqrt(_neg_expm1(2.0 * log_a)) * (ig * uc)
        in_tile = _row_iota((ts, W)) & (SUB - 1)
        d = 1
        while d < SUB:
            a_s = jnp.where(in_tile < d, 1.0, pltpu.roll(a, d, 0))
            b_s = jnp.where(in_tile < d, 0.0, pltpu.roll(b, d, 0))
            b = a * b_s + b
            a = a * a_s
            d *= 2
        before = hlast[SUB - 1:SUB, :]
        for k in range(ts // SUB):
            tile = slice(k * SUB, (k + 1) * SUB)
            h_tile = a[tile] * before + b[tile]
            hl_ref[tile, :] = h_tile
            before = h_tile[SUB - 1:SUB, :]
        hlast[...] = hl_ref[ts - SUB:ts, :]
        gy, _ = _gelu_parts(y_ref[...])
        mix_ref[...] = (hl_ref[...] * gy).astype(BF16)

    full = lambda shape: pl.BlockSpec(shape, lambda i: tuple(0 for _ in shape))
    return pl.pallas_call(
        body, name=name, grid=(S // ts,),
        in_specs=[pl.BlockSpec((ts, W), lambda i: (i, 4)), pl.BlockSpec((ts, W), lambda i: (i, 5)),
                  full(conv_w.shape), full(conv_b.shape), full(wa.shape), full(ba.shape), full(wx.shape),
                  full(bx.shape), full(lam.shape)],
        out_specs=[pl.BlockSpec((ts, W), lambda i: (i, 0)), pl.BlockSpec((ts, W), lambda i: (i, 1))],
        out_shape=[jax.ShapeDtypeStruct((S, W), F32), jax.ShapeDtypeStruct((S, 2 * W), BF16)],
        scratch_shapes=[pltpu.VMEM((SUB, W), F32), pltpu.VMEM((SUB, W), F32)],
        compiler_params=_cparams(("arbitrary",)),
    )(h, h, conv_w, conv_b, wa, ba, wx, bx, lam)


def _lru_bwd(h, hl, dmix, dh, conv_w, conv_b, wa, ba, wx, bx, lam, *, name, ts=256):
    S = h.shape[0]
    W = lam.shape[1]
    K = conv_w.shape[0]
    nbk, bd = wa.shape[0], wa.shape[1]
    ts = min(ts, S)
    nb = S // ts
    t8 = ts // SUB

    def body(u_ref, y_ref, uh_ref, hl_ref, hh_ref, dm_ref, cw_ref, cb_ref, wa_ref, ba_ref, wx_ref, bx_ref, lam_ref,
             dh_in, dh_ref, dcw_ref, dcb_ref, dwa_ref, dba_ref, dwx_ref, dbx_ref, dlam_ref, carry, head, lam_buf):
        i = pl.program_id(0)
        blk = nb - 1 - i

        @pl.when(i == 0)
        def _():
            carry[...] = jnp.zeros_like(carry)
            head[...] = jnp.zeros_like(head)
            for ref in (dcw_ref, dcb_ref, dwa_ref, dba_ref, dwx_ref, dbx_ref, dlam_ref):
                ref[...] = jnp.zeros_like(ref)

        inside = (blk > 0).astype(F32)
        u = u_ref[...]
        tl = uh_ref[...] * inside
        sh = [_shift_down(u, tl, K - 1 - k) for k in range(K)]
        uc = cb_ref[...]
        for k in range(K):
            uc = uc + cw_ref[k:k + 1, :] * sh[k]
        r, ig = _lru_gates(uc, wa_ref, ba_ref, wx_ref, bx_ref)
        lam_v = lam_ref[...]
        ls = _log_sigmoid(lam_v)
        log_a = LRU_C * r * ls
        a = jnp.exp(log_a)
        mult = jnp.sqrt(_neg_expm1(2.0 * log_a))
        hcur = hl_ref[...]
        hprev = _shift_down(hcur, hh_ref[...] * inside, 1)
        gy, dgy = _gelu_parts(y_ref[...])
        dm = dm_ref[...].astype(F32)
        d_y = dm * hcur * dgy
        rid = _row_iota((ts, W))
        bq = dm * gy + jnp.where(rid == ts - 1, carry[0:1, :], 0.0)
        aq = jnp.where(rid == ts - 1, 0.0, pltpu.roll(a, ts - 1, 0))
        in_tile = rid & (SUB - 1)
        d = 1
        while d < SUB:
            a_s = jnp.where(in_tile >= SUB - d, 1.0, pltpu.roll(aq, ts - d, 0))
            b_s = jnp.where(in_tile >= SUB - d, 0.0, pltpu.roll(bq, ts - d, 0))
            bq = bq + aq * b_s
            aq = aq * a_s
            d *= 2
        after_row = jnp.zeros((1, W), F32)
        for k in reversed(range(ts // SUB)):
            tile = slice(k * SUB, (k + 1) * SUB)
            lam_tile = aq[tile] * after_row + bq[tile]
            lam_buf[tile, :] = lam_tile
            after_row = lam_tile[0:1, :]
        lam_t = lam_buf[...]
        carry[...] = (a * lam_t)[0:SUB]
        d_a = lam_t * hprev
        d_mult = lam_t * (ig * uc)
        d_i = lam_t * mult * uc
        d_uc = lam_t * mult * ig
        d_log_a = d_a * a - d_mult * (a * a) / mult
        d_r = d_log_a * (LRU_C * ls)
        dlam_ref[...] += jnp.sum(d_log_a * (LRU_C * r), axis=0, keepdims=True) * _sigmoid(-lam_v)
        d_pr = d_r * r * (1.0 - r)
        d_pi = d_i * ig * (1.0 - ig)
        dba_ref[...] += jnp.sum(d_pr, axis=0, keepdims=True)
        dbx_ref[...] += jnp.sum(d_pi, axis=0, keepdims=True)
        extra = []
        for n in range(nbk):
            sl = slice(n * bd, (n + 1) * bd)
            ucn = uc[:, sl]
            dwa_ref[n] += _bdot(ucn, d_pr[:, sl], TN)
            dwx_ref[n] += _bdot(ucn, d_pi[:, sl], TN)
            extra.append(_bdot(d_pr[:, sl], wa_ref[n], NT) + _bdot(d_pi[:, sl], wx_ref[n], NT))
        d_uc = d_uc + jnp.concatenate(extra, axis=-1)
        dcb_ref[...] += jnp.sum(d_uc, axis=0, keepdims=True)
        rid8 = _row_iota((SUB, W))
        dcw = jnp.zeros((SUB, W), F32)
        for k in range(K):
            dcw = dcw + jnp.where(rid8 == k, jnp.sum(d_uc * sh[k], axis=0, keepdims=True), 0.0)
        dcw_ref[...] += dcw
        hd = head[...]
        d_u = cw_ref[K - 1:K, :] * d_uc
        for j in range(1, K):
            d_u = d_u + cw_ref[K - 1 - j:K - j, :] * _shift_up(d_uc, hd, j)
        head[...] = d_uc[0:SUB]
        dh_ref[:, 0:W] = d_u.astype(BF16)
        dh_ref[:, W:2 * W] = d_y.astype(BF16)

    rb = lambda i: nb - 1 - i
    prev8 = lambda i: jnp.maximum(rb(i) * t8 - 1, 0)
    full = lambda shape: pl.BlockSpec(shape, lambda i: tuple(0 for _ in shape))
    small = [jax.ShapeDtypeStruct((SUB, W), F32), jax.ShapeDtypeStruct((1, W), F32),
             jax.ShapeDtypeStruct(wa.shape, F32), jax.ShapeDtypeStruct((1, W), F32),
             jax.ShapeDtypeStruct(wx.shape, F32), jax.ShapeDtypeStruct((1, W), F32),
             jax.ShapeDtypeStruct((1, W), F32)]
    return pl.pallas_call(
        body, name=name, grid=(nb,),
        in_specs=[pl.BlockSpec((ts, W), lambda i: (rb(i), 4)), pl.BlockSpec((ts, W), lambda i: (rb(i), 5)),
                  pl.BlockSpec((SUB, W), lambda i: (prev8(i), 4)),
                  pl.BlockSpec((ts, W), lambda i: (rb(i), 0)), pl.BlockSpec((SUB, W), lambda i: (prev8(i), 0)),
                  pl.BlockSpec((ts, W), lambda i: (rb(i), 1)),
                  full(conv_w.shape), full(conv_b.shape), full(wa.shape), full(ba.shape), full(wx.shape),
                  full(bx.shape), full(lam.shape), pl.BlockSpec(memory_space=pl.ANY)],
        out_specs=[pl.BlockSpec((ts, 2 * W), lambda i: (rb(i), 2))] + [full(s.shape) for s in small],
        out_shape=[jax.ShapeDtypeStruct(dh.shape, dh.dtype)] + small,
        scratch_shapes=[pltpu.VMEM((SUB, W), F32), pltpu.VMEM((SUB, W), F32), pltpu.VMEM((ts, W), F32)],
        input_output_aliases={13: 0},
        compiler_params=_cparams(("arbitrary",)),
    )(h, h, h, hl, hl, dmix, conv_w, conv_b, wa, ba, wx, bx, lam, dh)


def _xattn_fwd(q, km, vm, *, name, ts=1024):
    S, D = q.shape
    M = km.shape[0]
    H = XA_HEADS
    Dh = D // H
    ts = min(ts, S)
    scale = Dh ** -0.5

    def body(q_ref, k_ref, v_ref, o_ref):
        for hd in range(H):
            cols = pl.ds(hd * Dh, Dh)
            s = _bdot(q_ref[:, cols], k_ref[:, cols], NT) * scale
            s = s - jnp.max(s, axis=-1, keepdims=True)
            e = jnp.exp(s)
            p = e / jnp.sum(e, axis=-1, keepdims=True)
            o_ref[:, cols] = _bdot(p, v_ref[:, cols], NN).astype(o_ref.dtype)

    return pl.pallas_call(
        body, name=name, grid=(S // ts,),
        in_specs=[pl.BlockSpec((ts, D), lambda i: (i, 0)), pl.BlockSpec((M, D), lambda i: (0, 0)),
                  pl.BlockSpec((M, D), lambda i: (0, 0))],
        out_specs=pl.BlockSpec((ts, D), lambda i: (i, 0)),
        out_shape=jax.ShapeDtypeStruct((S, D), BF16),
        compiler_params=_cparams(("parallel",)),
    )(q, km, vm)


def _xattn_bwd(q, km, vm, d_o, *, name, ts=1024):
    S, D = q.shape
    M = km.shape[0]
    H = XA_HEADS
    Dh = D // H
    ts = min(ts, S)
    scale = Dh ** -0.5

    def body(q_ref, k_ref, v_ref, do_ref, dq_ref, dk_ref, dv_ref):
        i = pl.program_id(0)

        @pl.when(i == 0)
        def _():
            dk_ref[...] = jnp.zeros_like(dk_ref)
            dv_ref[...] = jnp.zeros_like(dv_ref)

        for hd in range(H):
            cols = pl.ds(hd * Dh, Dh)
            qh, kh, vh, doh = q_ref[:, cols], k_ref[:, cols], v_ref[:, cols], do_ref[:, cols]
            s = _bdot(qh, kh, NT) * scale
            s = s - jnp.max(s, axis=-1, keepdims=True)
            e = jnp.exp(s)
            p = e / jnp.sum(e, axis=-1, keepdims=True)
            dp = _bdot(doh, vh, NT)
            ds = p * (dp - jnp.sum(dp * p, axis=-1, keepdims=True)) * scale
            dq_ref[:, cols] = _bdot(ds, kh, NN).astype(dq_ref.dtype)
            dk_ref[:, cols] += _bdot(ds, qh, TN)
            dv_ref[:, cols] += _bdot(p, doh, TN)

    row = pl.BlockSpec((ts, D), lambda i: (i, 0))
    mem = pl.BlockSpec((M, D), lambda i: (0, 0))
    return pl.pallas_call(
        body, name=name, grid=(S // ts,), in_specs=[row, mem, mem, row], out_specs=[row, mem, mem],
        out_shape=[jax.ShapeDtypeStruct((S, D), BF16), jax.ShapeDtypeStruct((M, D), F32),
                   jax.ShapeDtypeStruct((M, D), F32)],
        compiler_params=_cparams(("arbitrary",)),
    )(q, km, vm, d_o)


def _conv_rows(v, tail, cw_ref, cb_ref):
    K = cw_ref.shape[0]
    sh = [_shift_down(v, tail, K - 1 - k) for k in range(K)]
    out = cb_ref[...]
    for k in range(K):
        out = out + cw_ref[k:k + 1, :] * sh[k]
    return out, sh


FFN_SUB = 256


def _ffn_up_gate(xn, w_up, cw, cb, *, name, tm=1024, tn=512):
    S, D = xn.shape
    F2 = w_up.shape[1]
    F = F2 // 2
    tm, tn = min(tm, S), min(tn, F)
    sub = min(FFN_SUB, tm)
    nj = F // tn
    K = cw.shape[0]

    def body(x_ref, wa_ref, wb_ref, cwa_ref, cwb_ref, cba_ref, cbb_ref, act_ref, ha_ref, hb_ref, ac_ref, bc_ref, ta, tb):
        i = pl.program_id(1)

        @pl.when(i == 0)
        def _():
            ta[...] = jnp.zeros_like(ta)
            tb[...] = jnp.zeros_like(tb)

        tail_a, tail_b = ta[...], tb[...]
        for s in range(tm // sub):
            rows = pl.ds(s * sub, sub)
            xs = x_ref[rows, :]
            ha = _bdot(xs, wa_ref[...], NN)
            hb = _bdot(xs, wb_ref[...], NN)
            ac, _ = _conv_rows(ha, tail_a, cwa_ref, cba_ref)
            bc, _ = _conv_rows(hb, tail_b, cwb_ref, cbb_ref)
            tail_a, tail_b = ha[sub - SUB:sub], hb[sub - SUB:sub]
            ha_ref[rows, :] = ha
            hb_ref[rows, :] = hb
            ac_ref[rows, :] = ac
            bc_ref[rows, :] = bc
            act_ref[rows, :] = (ac * _sigmoid(ac) * bc).astype(act_ref.dtype)
        ta[...] = tail_a
        tb[...] = tail_b

    blk = pl.BlockSpec((tm, tn), lambda j, i: (i, j))
    return pl.pallas_call(
        body, name=name, grid=(nj, S // tm),
        in_specs=[pl.BlockSpec((tm, D), lambda j, i: (i, 0)),
                  pl.BlockSpec((D, tn), lambda j, i: (0, j)), pl.BlockSpec((D, tn), lambda j, i: (0, j + nj)),
                  pl.BlockSpec((K, tn), lambda j, i: (0, j)), pl.BlockSpec((K, tn), lambda j, i: (0, j + nj)),
                  pl.BlockSpec((1, tn), lambda j, i: (0, j)), pl.BlockSpec((1, tn), lambda j, i: (0, j + nj))],
        out_specs=[blk] * 5,
        out_shape=[jax.ShapeDtypeStruct((S, F), BF16)] + [jax.ShapeDtypeStruct((S, F), F32)] * 4,
        scratch_shapes=[pltpu.VMEM((SUB, tn), F32), pltpu.VMEM((SUB, tn), F32)],
        compiler_params=_cparams(("parallel", "arbitrary")),
    )(xn, w_up, w_up, cw, cw, cb, cb)


def _ffn_bwd(dx, w_down, hh_a, hh_b, c_a, c_b, act, xn, cw, *, name, tm=1024, tn=256):
    S, D = dx.shape
    F = hh_a.shape[1]
    tm, tn = min(tm, S), min(tn, F)
    sub = min(FFN_SUB, tm)
    nj = F // tn
    nb = S // tm
    K = cw.shape[0]

    def body(dx_ref, wd_ref, a_ref, b_ref, ac_ref, bc_ref, act_ref, xn_ref, cwa_ref, cwb_ref,
             da_ref, db_ref, ga_ref, gb_ref, dwd_ref, dwu_ref, ha, hb, acc_d, acc_a, acc_b):
        i = pl.program_id(1)

        @pl.when(i == 0)
        def _():
            for ref in (ha, hb, ga_ref, gb_ref, acc_d, acc_a, acc_b):
                ref[...] = jnp.zeros_like(ref)

        rid8 = _row_iota((SUB, tn))
        heads = [ha[...], hb[...]]
        gsums = [jnp.zeros((SUB, tn), F32), jnp.zeros((SUB, tn), F32)]
        for s in reversed(range(tm // sub)):
            rows = pl.ds(s * sub, sub)
            dv = _bdot(dx_ref[rows, :], wd_ref[...], NT)
            ac, bc = ac_ref[rows, :], bc_ref[rows, :]
            sg = _sigmoid(ac)
            d_bc = dv * ac * sg
            d_ac = dv * bc * sg * (1.0 + ac * (1.0 - sg))
            for which, (d_c, h_ref, cw_ref, o_ref) in enumerate(((d_ac, a_ref, cwa_ref, da_ref),
                                                                 (d_bc, b_ref, cwb_ref, db_ref))):
                ahead = [d_c] + [_shift_up(d_c, heads[which], j) for j in range(1, K)]
                heads[which] = d_c[0:SUB]
                d_in = cw_ref[K - 1:K, :] * d_c
                for j in range(1, K):
                    d_in = d_in + cw_ref[K - 1 - j:K - j, :] * ahead[j]
                o_ref[rows, :] = d_in.astype(o_ref.dtype)
                hv = h_ref[rows, :]
                gsum = gsums[which] + jnp.where(rid8 == K, jnp.sum(d_c, axis=0, keepdims=True), 0.0)
                for k in range(K):
                    gsum = gsum + jnp.where(rid8 == k, jnp.sum(ahead[K - 1 - k] * hv, axis=0, keepdims=True), 0.0)
                gsums[which] = gsum
        ha[...], hb[...] = heads
        ga_ref[...] += gsums[0]
        gb_ref[...] += gsums[1]
        acc_d[...] += _bdot(act_ref[...], dx_ref[...], TN)
        acc_a[...] += _bdot(xn_ref[...], da_ref[...], TN)
        acc_b[...] += _bdot(xn_ref[...], db_ref[...], TN)

        @pl.when(i == nb - 1)
        def _():
            dwd_ref[...] = acc_d[...].astype(dwd_ref.dtype)
            dwu_ref[0] = acc_a[...].astype(dwu_ref.dtype)
            dwu_ref[1] = acc_b[...].astype(dwu_ref.dtype)

    rb = lambda i: nb - 1 - i
    blk = pl.BlockSpec((tm, tn), lambda j, i: (rb(i), j))
    acc = pl.BlockSpec((SUB, tn), lambda j, i: (0, j))
    rows_d = pl.BlockSpec((tm, D), lambda j, i: (rb(i), 0))
    return pl.pallas_call(
        body, name=name, grid=(nj, nb),
        in_specs=[rows_d, pl.BlockSpec((tn, D), lambda j, i: (j, 0)), blk, blk, blk, blk, blk, rows_d,
                  pl.BlockSpec((K, tn), lambda j, i: (0, j)), pl.BlockSpec((K, tn), lambda j, i: (0, j + nj))],
        out_specs=[blk, blk, acc, acc, pl.BlockSpec((tn, D), lambda j, i: (j, 0)),
                   pl.BlockSpec((2, D, tn), lambda j, i: (0, 0, j))],
        out_shape=[jax.ShapeDtypeStruct((S, F), BF16), jax.ShapeDtypeStruct((S, F), BF16),
                   jax.ShapeDtypeStruct((SUB, F), F32), jax.ShapeDtypeStruct((SUB, F), F32),
                   jax.ShapeDtypeStruct((F, D), BF16), jax.ShapeDtypeStruct((2, D, F), BF16)],
        scratch_shapes=[pltpu.VMEM((SUB, tn), F32), pltpu.VMEM((SUB, tn), F32), pltpu.VMEM((tn, D), F32),
                        pltpu.VMEM((D, tn), F32), pltpu.VMEM((D, tn), F32)],
        compiler_params=_cparams(("parallel", "arbitrary")),
    )(dx, w_down, hh_a, hh_b, c_a, c_b, act, xn, cw, cw)


ADAM_BLOCK_ELEMS = 384 * 1024


def _adamw(w, m, v, parts, *, name):
    R, C = w.shape
    n = parts.shape[0]
    tr = R
    for cand in (1024, 512, 256, 128, 64, 32, 16):
        if R % cand == 0 and cand * C <= ADAM_BLOCK_ELEMS:
            tr = cand
            break
    c1 = 1.0 - ADAM_B1 ** ADAM_STEP
    c2 = 1.0 - ADAM_B2 ** ADAM_STEP

    def body(w_ref, m_ref, v_ref, p_ref, g_ref, d_ref, nm_ref, nv_ref):
        g = p_ref[0].astype(F32)
        for k in range(1, n):
            g = g + p_ref[k].astype(F32)
        m_new = ADAM_B1 * m_ref[...] + (1.0 - ADAM_B1) * g
        v_new = ADAM_B2 * v_ref[...] + (1.0 - ADAM_B2) * (g * g)
        m_hat = m_new / c1
        v_hat = v_new / c2
        g_ref[...] = g
        d_ref[...] = -ADAM_LR * (m_hat / (jnp.sqrt(v_hat) + ADAM_EPS) + ADAM_WD * w_ref[...])
        nm_ref[...] = m_new
        nv_ref[...] = v_new

    blk = pl.BlockSpec((tr, C), lambda i: (i, 0))
    sds = jax.ShapeDtypeStruct((R, C), F32)
    return pl.pallas_call(
        body, name=name, grid=(R // tr,),
        in_specs=[blk, blk, blk, pl.BlockSpec((n, tr, C), lambda i: (0, i, 0))],
        out_specs=[blk, blk, blk, blk], out_shape=[sds, sds, sds, sds],
        compiler_params=_cparams(("parallel",)),
    )(w, m, v, parts)


def _mesh_place():
    x, y, c = lax.axis_index("x"), lax.axis_index("y"), lax.axis_index("c")
    others = [(1 - x, y), (x, 1 - y), (1 - x, 1 - y)]
    return x, y, c, others


HBM_SPEC = pl.BlockSpec(memory_space=pltpu.HBM)
SEM_SPEC = pl.BlockSpec(memory_space=pltpu.SEMAPHORE)
ANY_SPEC = pl.BlockSpec(memory_space=pl.ANY)
EFFECT = pltpu.SideEffectType.DATAFLOW_SIDE_EFFECTING


def _in_hbm(a):
    return pltpu.with_memory_space_constraint(a, pltpu.HBM)


def _split_start(srcs, lands, copies, n_cp, *, name):
    n_s, n_l = len(srcs), len(lands)

    def body(*refs):
        src_refs, land_refs = refs[:n_s], refs[n_s:n_s + n_l]
        ssem, rsem = refs[n_s + n_l], refs[n_s + n_l + 1]
        token = refs[-1]
        for outgoing, _ in copies(src_refs, land_refs, ssem, rsem):
            outgoing.start()
        token[...] = jnp.zeros_like(token)

    outs = pl.pallas_call(
        body, name=name,
        out_shape=(pltpu.SemaphoreType.DMA((n_cp,)), pltpu.SemaphoreType.DMA((n_cp,)),
                   *[pltpu.HBM(a.shape, a.dtype) for a in srcs], *[pltpu.HBM(a.shape, a.dtype) for a in lands],
                   jax.ShapeDtypeStruct((SUB, LANE), F32)),
        in_specs=[HBM_SPEC] * (n_s + n_l),
        out_specs=(SEM_SPEC, SEM_SPEC, *[HBM_SPEC] * (n_s + n_l), pl.BlockSpec(memory_space=pltpu.VMEM)),
        input_output_aliases={i: 2 + i for i in range(n_s + n_l)},
        compiler_params=pltpu.CompilerParams(has_side_effects=EFFECT),
    )(*[_in_hbm(a) for a in srcs], *[_in_hbm(a) for a in lands])
    ssem, rsem = outs[0], outs[1]
    return ssem, rsem, list(outs[2:2 + n_s]), list(outs[2 + n_s:2 + n_s + n_l]), outs[-1]


def _split_wait(srcs, lands, ssem, rsem, after, copies, *, name):
    n_s, n_l = len(srcs), len(lands)

    def body(*refs):
        src_refs, land_refs = refs[:n_s], refs[n_s:n_s + n_l]
        s_ref, r_ref = refs[n_s + n_l], refs[n_s + n_l + 1]
        for outgoing, incoming in copies(src_refs, land_refs, s_ref, r_ref):
            outgoing.wait_send()
            incoming.wait_recv()

    outs = pl.pallas_call(
        body, name=name,
        out_shape=(*[pltpu.HBM(a.shape, a.dtype) for a in srcs], *[pltpu.HBM(a.shape, a.dtype) for a in lands]),
        in_specs=[HBM_SPEC] * (n_s + n_l) + [SEM_SPEC, SEM_SPEC, ANY_SPEC], out_specs=[HBM_SPEC] * (n_s + n_l),
        input_output_aliases={i: i for i in range(n_s + n_l)},
        compiler_params=pltpu.CompilerParams(has_side_effects=EFFECT),
    )(*srcs, *lands, ssem, rsem, after)
    return list(outs[:n_s]), list(outs[n_s:])


PLACE_BLOCK_ELEMS = 512 * 1024


def _place_rows(r, w):
    return _div_tile(r, max(16, PLACE_BLOCK_ELEMS // w), 16)


def _cast_place(shard, chip, axis, after, *, name):
    r, w = shard.shape
    tr = _place_rows(r, w)
    nb = r // tr
    full = (r * N_CHIP, w) if axis == 0 else (r, w * N_CHIP)
    has_after = after is not None

    def body(chip_ref, s_ref, *rest):
        rest[-1][...] = s_ref[...].astype(BF16)

    out_map = (lambda i, ch: (ch[0] * nb + i, 0)) if axis == 0 else (lambda i, ch: (i, ch[0]))
    grid_spec = pltpu.PrefetchScalarGridSpec(
        num_scalar_prefetch=1, grid=(nb,),
        in_specs=[pl.BlockSpec((tr, w), lambda i, ch: (i, 0))] + has_after * [ANY_SPEC],
        out_specs=pl.BlockSpec((tr, w), out_map))
    return pl.pallas_call(body, name=name, grid_spec=grid_spec, out_shape=jax.ShapeDtypeStruct(full, BF16),
                          compiler_params=_cparams(("parallel",)))(chip, shard, *(has_after * [after]))


def _grad_shard_shape(g, axis):
    if g.ndim == 3:
        return g.shape[1], 2 * g.shape[2] // N_CHIP
    return (g.shape[0] // N_CHIP, g.shape[1]) if axis == 0 else (g.shape[0], g.shape[1] // N_CHIP)


def _slot_place(g, ids, axis, *, name):
    r, w = _grad_shard_shape(g, axis)
    tr = _place_rows(r, w)
    nb = r // tr

    def body(ids_ref, g_ref, o_ref):
        o_ref[...] = g_ref[...]

    if g.ndim == 3:
        in_spec = pl.BlockSpec((None, tr, w), lambda i, ids_: (ids_[0] // 2, i, ids_[0] % 2))
    elif axis == 0:
        in_spec = pl.BlockSpec((tr, w), lambda i, ids_: (ids_[0] * nb + i, 0))
    else:
        in_spec = pl.BlockSpec((tr, w), lambda i, ids_: (i, ids_[0]))
    grid_spec = pltpu.PrefetchScalarGridSpec(
        num_scalar_prefetch=1, grid=(nb,), in_specs=[in_spec],
        out_specs=pl.BlockSpec((None, tr, w), lambda i, ids_: (ids_[1], i, 0)))
    return pl.pallas_call(body, name=name, grid_spec=grid_spec, out_shape=jax.ShapeDtypeStruct((N_DEV, r, w), g.dtype),
                          compiler_params=_cparams(("parallel",)))(ids, g)


class _WeightGather:
    def __init__(self, placed, shard_shapes, axes, splits, tag):
        self.placed, self.shard_shapes, self.axes, self.splits, self.tag = list(placed), shard_shapes, axes, splits, tag
        self.n = len(placed)

    def _region(self, land_refs, it, chip, half):
        r, w = self.shard_shapes[it]
        by_rows = self.axes[it] == 0
        if self.splits[it] and half is not None:
            rows = pl.ds(pl.multiple_of(half * (r // 2) + (chip * r if by_rows else 0), 16), r // 2)
        else:
            rows = pl.ds(chip * r if by_rows else 0, r)
        cols = pl.ds(0, w) if by_rows else pl.ds(pl.multiple_of(chip * w, LANE), w)
        return land_refs[it].at[rows, cols]

    def _ici(self, src_refs, land_refs, ssem, rsem):
        x, y, c, others = _mesh_place()
        pairs = []
        for it in range(self.n):
            for j, chip in enumerate(others):
                def mk(chip_from, it=it, j=j, chip=chip):
                    return pltpu.make_async_remote_copy(
                        src_ref=self._region(land_refs, it, 2 * x + y, c), dst_ref=self._region(land_refs, it, chip_from, c),
                        send_sem=ssem.at[3 * it + j], recv_sem=rsem.at[3 * it + j], device_id=(*chip, c),
                        device_id_type=MESH)
                pairs.append((mk(2 * x + y), mk(2 * chip[0] + chip[1])))
        return pairs

    def start(self):
        self.ssem, self.rsem, _, self.lands, token = _split_start(
            [], self.placed, self._ici, 3 * self.n, name="gather_start_" + self.tag)
        return token

    def _d2d(self, src_refs, land_refs, ssem, rsem):
        x, y, c, others = _mesh_place()
        pairs = []
        for it in range(self.n):
            if self.splits[it]:
                for chip in others:
                    def mk(half, it=it, chip=chip, k=len(pairs)):
                        reg = self._region(land_refs, it, 2 * chip[0] + chip[1], half)
                        return pltpu.make_async_remote_copy(src_ref=reg, dst_ref=reg, send_sem=ssem.at[k], recv_sem=rsem.at[k],
                                                            device_id=(x, y, 1 - c), device_id_type=MESH)
                    pairs.append((mk(c), mk(1 - c)))
        return pairs

    def forward(self, after):
        _, lands = _split_wait([], self.lands, self.ssem, self.rsem, after, self._ici,
                               name="gather_wait_" + self.tag)
        self.fsem, self.frsem, _, self.lands, token = _split_start(
            [], lands, self._d2d, 3 * sum(self.splits), name="gather_fwd_" + self.tag)
        return token

    def finish_forward(self, after):
        _, lands = _split_wait([], self.lands, self.fsem, self.frsem, after, self._d2d,
                               name="gather_fwd_wait_" + self.tag)
        return lands

    def finish(self, after):
        _, lands = _split_wait([], self.lands, self.ssem, self.rsem, after, self._ici,
                               name="gather_wait_" + self.tag)
        n = self.n
        n_fwd = 3 * sum(self.splits)
        if n_fwd == 0:
            return lands

        def body(*refs):
            out_refs = refs[n:2 * n]
            fsend, frecv = refs[2 * n:]
            x, y, c, others = _mesh_place()
            sibling = (x, y, 1 - c)

            def fwd(it, slot, chip, half):
                reg = self._region(out_refs, it, 2 * chip[0] + chip[1], half)
                return pltpu.make_async_remote_copy(src_ref=reg, dst_ref=reg, send_sem=fsend.at[slot],
                                                    recv_sem=frecv.at[slot], device_id=sibling, device_id_type=MESH)

            sends, recvs = [], []
            for it in range(n):
                if self.splits[it]:
                    for chip in others:
                        sends.append(fwd(it, len(sends), chip, c))
                        recvs.append(fwd(it, len(recvs), chip, 1 - c))
            for cp in sends:
                cp.start()
            for cp in recvs:
                cp.wait_recv()
            for cp in sends:
                cp.wait_send()

        fulls = pl.pallas_call(
            body, name="gather_d2d_" + self.tag, in_specs=[ANY_SPEC] * n, out_specs=[ANY_SPEC] * n,
            out_shape=[jax.ShapeDtypeStruct(a.shape, a.dtype) for a in lands],
            scratch_shapes=[pltpu.SemaphoreType.DMA((n_fwd,)), pltpu.SemaphoreType.DMA((n_fwd,))],
            input_output_aliases={i: i for i in range(n)},
        )(*lands)
        return list(fulls)


class _GradGather:
    def __init__(self, grads, axes, tag):
        self.grads, self.axes, self.tag = list(grads), axes, tag
        self.n = len(grads)
        self.shard_shapes = [_grad_shard_shape(g, ax) for g, ax in zip(grads, axes)]

    def _piece(self, src_refs, it, chip):
        r, w = self.shard_shapes[it]
        if self.grads[it].ndim == 3:
            return src_refs[it].at[chip // 2, :, pl.ds(pl.multiple_of((chip % 2) * w, LANE), w)]
        if self.axes[it] == 0:
            return src_refs[it].at[pl.ds(pl.multiple_of(chip * r, 16), r), :]
        return src_refs[it].at[:, pl.ds(pl.multiple_of(chip * w, LANE), w)]

    PER_ITEM = 4

    def _remote(self, src_refs, land_refs, ssem, rsem):
        x, y, c, others = _mesh_place()
        me = 4 * x + 2 * y + c
        pairs = []
        for it in range(self.n):
            def mk(k, piece_chip, slot, to, it=it):
                return pltpu.make_async_remote_copy(
                    src_ref=self._piece(src_refs, it, piece_chip), dst_ref=land_refs[it].at[slot],
                    send_sem=ssem.at[self.PER_ITEM * it + k], recv_sem=rsem.at[self.PER_ITEM * it + k], device_id=to,
                    device_id_type=MESH)
            for j, chip in enumerate(others):
                chip_id = 2 * chip[0] + chip[1]
                pairs.append((mk(j, chip_id, me, (*chip, c)), mk(j, chip_id, 2 * chip_id + c, (*chip, c))))
            sibling = (x, y, 1 - c)
            pairs.append((mk(3, 2 * x + y, me, sibling), mk(3, 2 * x + y, 4 * x + 2 * y + 1 - c, sibling)))
        return pairs

    def start(self):
        x, y, c = lax.axis_index("x"), lax.axis_index("y"), lax.axis_index("c")
        ids = jnp.stack([2 * x + y, 4 * x + 2 * y + c]).astype(jnp.int32)
        lands = [_slot_place(g, ids, ax, name="grads_own_%s%d" % (self.tag, it))
                 for it, (g, ax) in enumerate(zip(self.grads, self.axes))]
        self.ssem, self.rsem, self.srcs, self.lands, token = _split_start(
            self.grads, lands, self._remote, self.PER_ITEM * self.n, name="grads_start_" + self.tag)
        return token

    def _forward(self, src_refs, land_refs, ssem, rsem):
        x, y, c, others = _mesh_place()
        pairs = []
        for it in range(self.n):
            for j, ch in enumerate(others):
                def mk(slot, it=it, j=j):
                    return pltpu.make_async_remote_copy(
                        src_ref=land_refs[it].at[slot], dst_ref=land_refs[it].at[slot], send_sem=ssem.at[3 * it + j],
                        recv_sem=rsem.at[3 * it + j], device_id=(x, y, 1 - c), device_id_type=MESH)
                pairs.append((mk(4 * ch[0] + 2 * ch[1] + c), mk(4 * ch[0] + 2 * ch[1] + 1 - c)))
        return pairs

    def forward(self, after):
        _, lands = _split_wait(self.srcs, self.lands, self.ssem, self.rsem, after, self._remote,
                               name="grads_wait_" + self.tag)
        self.fsem, self.frsem, _, self.lands, token = _split_start(
            [], lands, self._forward, 3 * self.n, name="grads_fwd_" + self.tag)
        return token

    def finish(self, after):
        _, lands = _split_wait([], self.lands, self.fsem, self.frsem, after, self._forward,
                               name="grads_fwd_wait_" + self.tag)
        return lands


def _allreduce_small(vec, *, name):
    R, L = vec.shape

    def body(v_ref, o_ref, buf, send, recv, lsem):
        x, y, c, others = _mesh_place()
        me = 4 * x + 2 * y + c
        sibling = (x, y, 1 - c)

        def copy(k, slot, to, src=None):
            return pltpu.make_async_remote_copy(
                src_ref=buf.at[slot] if src is None else src, dst_ref=buf.at[slot], send_sem=send.at[k],
                recv_sem=recv.at[k], device_id=to, device_id_type=MESH)

        def slot_of(chip, core):
            return 4 * chip[0] + 2 * chip[1] + core

        mine = pltpu.make_async_copy(v_ref, buf.at[me], lsem)
        mine.start()
        first = [copy(0, me, sibling, src=v_ref)]
        first += [copy(1 + j, me, (*chip, c), src=v_ref) for j, chip in enumerate(others)]
        for cp in first:
            cp.start()
        passed = [copy(4 + j, slot_of(chip, c), sibling) for j, chip in enumerate(others)]
        for j, chip in enumerate(others):
            copy(1 + j, slot_of(chip, c), (*chip, c)).wait_recv()
            passed[j].start()
        copy(0, slot_of((x, y), 1 - c), sibling).wait_recv()
        for j, chip in enumerate(others):
            copy(4 + j, slot_of(chip, 1 - c), sibling).wait_recv()
        for cp in first + passed:
            cp.wait_send()
        mine.wait()
        total = buf[0]
        for k in range(1, N_DEV):
            total = total + buf[k]
        o_ref[...] = total

    return pl.pallas_call(
        body, name=name, in_specs=[pl.BlockSpec(memory_space=pltpu.VMEM)],
        out_specs=pl.BlockSpec(memory_space=pltpu.VMEM), out_shape=jax.ShapeDtypeStruct((R, L), F32),
        scratch_shapes=[pltpu.VMEM((N_DEV, R, L), F32), pltpu.SemaphoreType.DMA((7,)), pltpu.SemaphoreType.DMA((7,)),
                        pltpu.SemaphoreType.DMA],
        compiler_params=pltpu.CompilerParams(vmem_limit_bytes=VMEM_LIMIT),
    )(vec)


PACK_ALIGN = 1024


def _pack(arrs, row_multiple):
    flat = []
    for a in arrs:
        f = a.reshape(-1).astype(F32)
        flat.append(jnp.pad(f, (0, (-f.shape[0]) % PACK_ALIGN)))
    v = jnp.concatenate(flat)
    v = jnp.pad(v, (0, (-v.shape[0]) % (LANE * row_multiple)))
    return v.reshape(-1, LANE)


def _unpack(v, shapes):
    flat = v.reshape(-1)
    out, off = [], 0
    for s in shapes:
        size = math.prod(s)
        out.append(flat[off:off + size].reshape(s))
        off += size + (-size) % PACK_ALIGN
    return out


def _div_tile(dim, cap, mult=LANE):
    best = None
    for cand in range(mult, min(cap, dim) + 1, mult):
        if dim % cand == 0:
            best = cand
    return dim if best is None else best


WEIGHT_NAMES = ('norm1_g', 'w_in', 'ret_g', 'rg_conv_w', 'rg_conv_b', 'rg_wa', 'rg_ba', 'rg_wx', 'rg_bx', 'rg_lambda',
                'w_out', 'norm2_g', 'norm_mem_g', 'xa_wq', 'xa_wk', 'xa_wv', 'xa_wo', 'norm3_g', 'ffn_w_up',
                'ffn_conv_w', 'ffn_conv_b', 'ffn_w_down', 'final_g')
BIG_AXIS = {'w_in': 1, 'w_out': 0, 'xa_wq': 0, 'xa_wk': 0, 'xa_wv': 0, 'xa_wo': 0, 'ffn_w_up': 1, 'ffn_w_down': 0}
SMALL_SHARDED = ('rg_conv_w', 'ffn_conv_w')


def _step(x, mem, positions, loss_target, W, Mo, Vo):
    S, D = x.shape[1], x.shape[2]
    xs, mems, tgt = x[0], mem[0], loss_target[0]
    n_mem = mems.shape[0]
    pos_col = positions.reshape(S, 1)
    chip = 2 * lax.axis_index("x") + lax.axis_index("y")

    big = list(BIG_AXIS)
    shards = {n: W[n][0] for n in big}
    G = {}
    gather_groups = (('w_in', 'rg_conv_w'), ('w_out', 'xa_wq', 'xa_wk', 'xa_wv', 'xa_wo'),
                     ('ffn_w_up', 'ffn_conv_w'), ('ffn_w_down',))
    gathers, tok = [], None
    chip1 = jnp.reshape(chip, (1,)).astype(jnp.int32)
    for gi, names in enumerate(gather_groups):
        placed = []
        for n in names:
            if n in BIG_AXIS:
                placed.append(_cast_place(shards[n], chip1, BIG_AXIS[n], tok, name="place_" + n))
            else:
                s = W[n][0] if tok is None else W[n][0] + tok[0, 0]
                full = lax.empty((s.shape[0], s.shape[1] * N_CHIP), s.dtype)
                placed.append(lax.dynamic_update_slice(full, s, (0, chip * s.shape[1])))
        ag = _WeightGather(placed, [W[n][0].shape for n in names], [BIG_AXIS.get(n, 1) for n in names],
                           [n in BIG_AXIS for n in names], "g%d" % gi)
        tok = ag.start()
        gathers.append(ag)

    def finish_gather(gi, after):
        G.update(zip(gather_groups[gi], gathers[gi].finish(after)))

    def finish_forward(gi, after):
        G.update(zip(gather_groups[gi], gathers[gi].finish_forward(after)))

    R = W['ret_g'].shape[1]
    Wl = W['rg_lambda'].shape[1]
    IN = W['w_in'].shape[2] * N_CHIP
    F2 = W['ffn_w_up'].shape[2] * N_CHIP
    F = F2 // 2

    norm1_g, norm2_g, norm3_g = W['norm1_g'] + tok[0, 0], W['norm2_g'], W['norm3_g']
    norm_mem_g, final_g, ret_g = W['norm_mem_g'], W['final_g'].reshape(1, D), W['ret_g']
    rg_cb = W['rg_conv_b']
    wa, wx = W['rg_wa'][0], W['rg_wx'][0]
    ba, bx = W['rg_ba'].reshape(1, Wl), W['rg_bx'].reshape(1, Wl)
    lam = W['rg_lambda']
    ffn_cb = W['ffn_conv_b']

    def fwd_mm(a, wname, N, K, **kw):
        return _mm(a, G[wname], mode="nn", M=a.shape[0], N=N, K=K, tm=_div_tile(a.shape[0], 1024),
                   tn=_div_tile(N, 2048, 512) if K <= 3072 else 512, tk=K, **kw)

    def fwd_mm_norm(a, wname, res, g, name):
        return _mm(a, G[wname], mode="nn", M=a.shape[0], N=D, K=a.shape[1], tm=512, tn=D, tk=_div_tile(a.shape[1], 2048),
                   out_dtype=F32, res=res, norm_g=g, name=name)

    def bwd_x_mm(d, wname, N, K, **kw):
        return _mm(d, G[wname], mode="nt", M=d.shape[0], N=N, K=K, tm=_div_tile(d.shape[0], 1024),
                   tn=_div_tile(N, 2048 if K <= 3072 else 512, 256), tk=K, **kw)

    def bwd_w_mm(a, d, M, N, **kw):
        Ks = a.shape[0]
        return _mm(a, d, mode="tn", M=M, N=N, K=Ks, out_dtype=BF16, tm=_div_tile(M, 1024, 256),
                   tn=_div_tile(N, 1024, 256), tk=_div_tile(Ks, 4096 if d.dtype == BF16 else 1024), **kw)

    xn1 = _rmsnorm_fwd(xs, norm1_g, name="norm1_fwd")
    half = (R // RET_HEADS) // 2
    inv = (ROPE_BASE ** (-jnp.arange(half, dtype=F32) / half)).reshape(1, half)
    cos, sin = _rope_table(pos_col, inv + tok[0, 0], name="rope_table")
    finish_gather(0, cos)
    rg_cw = G['rg_conv_w']
    h = fwd_mm(xn1, 'w_in', IN, D, out_dtype=F32, name="mm_in")
    hl, mix = _lru_fwd(h, rg_cw, rg_cb, wa, ba, wx, bx, lam, name="lru_fwd")
    t1 = gathers[1].forward(hl)
    ret_raw, states, mix = _ret_fwd(h, cos, sin, ret_g + t1[0, 0], mix, name="ret_fwd")
    finish_forward(1, mix)
    x1, xn2 = fwd_mm_norm(mix, 'w_out', xs, norm2_g, "mm_out")
    memn = _rmsnorm_fwd(mems, norm_mem_g, name="norm_mem_fwd")
    km = fwd_mm(memn, 'xa_wk', D, D, out_dtype=BF16, name="mm_k")
    vm = fwd_mm(memn, 'xa_wv', D, D, out_dtype=BF16, name="mm_v")
    t2 = gathers[2].forward(x1)
    q = fwd_mm(xn2, 'xa_wq', D, D, out_dtype=BF16, after=t2, name="mm_q")
    o = _xattn_fwd(q, km, vm, name="xattn_fwd")
    x2, xn3 = fwd_mm_norm(o, 'xa_wo', x1, norm3_g, "mm_o")
    finish_forward(2, xn3)
    t3 = gathers[3].forward(xn3)
    ffn_cw = G['ffn_conv_w']
    act, hh_a, hh_b, hc_a, hc_b = _ffn_up_gate(xn3, G['ffn_w_up'], ffn_cw, ffn_cb + t3[0, 0], name="ffn_up_gate")
    finish_forward(3, act)
    x3 = fwd_mm(act, 'ffn_w_down', D, F, out_dtype=F32, res=x2, name="mm_down")
    dx3, d_final, loss8, dx3h = _final_loss(x3, tgt, final_g, name="final_loss")

    gw = {}
    grad_groups = []

    def start_grads(names, tag):
        gg = _GradGather([gw[n] for n in names], [BIG_AXIS[n] for n in names], tag)
        grad_groups.append((names, gg))
        return gg.start()

    dhh_a, dhh_b, gcw_a, gcw_b, gw['ffn_w_down'], gw['ffn_w_up'] = _ffn_bwd(
        dx3h, G['ffn_w_down'], hh_a, hh_b, hc_a, hc_b, act, xn3, ffn_cw, name="ffn_bwd")
    tok_a = start_grads(('ffn_w_down', 'ffn_w_up'), "a")
    dxn3 = bwd_x_mm(dhh_a, 'ffn_w_up', D, F, out_dtype=F32, after=tok_a, name="mm_dxn3_a")
    dxn3 = bwd_x_mm(dhh_b, 'ffn_w_up', D, F, out_dtype=BF16, b_off=(0, F), res=dxn3, name="mm_dxn3_b")
    dx2, d_norm3, dx2h = _rmsnorm_bwd(x2, dxn3, norm3_g, dx3, name="norm3_bwd", emit_bf16=True)
    Kc = ffn_cw.shape[0]
    d_ffn_cw = jnp.concatenate([gcw_a[:Kc], gcw_b[:Kc]], axis=1)
    d_ffn_cb = jnp.concatenate([gcw_a[Kc:Kc + 1], gcw_b[Kc:Kc + 1]], axis=1)

    d_o = bwd_x_mm(dx2h, 'xa_wo', D, D, out_dtype=BF16, name="mm_do")
    gw['xa_wo'] = bwd_w_mm(o, dx2h, D, D, name="mm_dw_o")
    dq, dk, dv = _xattn_bwd(q, km, vm, d_o, name="xattn_bwd")
    gw['xa_wq'] = bwd_w_mm(xn2, dq, D, D, name="mm_dw_q")
    dxn2 = bwd_x_mm(dq, 'xa_wq', D, D, out_dtype=BF16, name="mm_dxn2")
    gw['xa_wk'] = bwd_w_mm(memn, dk, D, D, name="mm_dw_k")
    gw['xa_wv'] = bwd_w_mm(memn, dv, D, D, name="mm_dw_v")
    dmemn = bwd_x_mm(dk, 'xa_wk', D, D, out_dtype=F32, name="mm_dmem_k")
    dmemn = bwd_x_mm(dv, 'xa_wv', D, D, out_dtype=F32, res=dmemn, name="mm_dmem_v")
    _, d_norm_mem = _rmsnorm_bwd(mems, dmemn, norm_mem_g, None, name="norm_mem_bwd")
    dx1, d_norm2, dx1h = _rmsnorm_bwd(x1, dxn2, norm2_g, dx2, name="norm2_bwd", emit_bf16=True)

    gw['w_out'] = bwd_w_mm(mix, dx1h, D, D, name="mm_dw_out")
    tok_b = start_grads(('xa_wo', 'xa_wq', 'xa_wk', 'xa_wv', 'w_out'), "b")
    dmix = bwd_x_mm(dx1h, 'w_out', D, D, out_dtype=BF16, after=tok_b, name="mm_dmix")
    dh, d_ret_g = _ret_bwd(h, cos, sin, ret_g, states, ret_raw, dmix, name="ret_bwd")
    dh, d_rcw, d_rcb, d_wa, d_ba, d_wx, d_bx, d_lam = _lru_bwd(
        h, hl, dmix, dh, rg_cw, rg_cb, wa, ba, wx, bx, lam, name="lru_bwd")
    gw['w_in'] = bwd_w_mm(xn1, dh, D, IN, name="mm_dw_in")
    tok_c = start_grads(('w_in',), "c")
    dxn1 = bwd_x_mm(dh, 'w_in', D, IN, out_dtype=BF16, after=tok_c, name="mm_dxn1")
    grad_x, d_norm1 = _rmsnorm_bwd(xs, dxn1, norm1_g, dx1, name="norm1_bwd")

    small_parts = {
        'norm1_g': d_norm1, 'ret_g': d_ret_g, 'rg_conv_w': d_rcw[:rg_cw.shape[0]], 'rg_conv_b': d_rcb,
        'rg_wa': d_wa, 'rg_ba': d_ba, 'rg_wx': d_wx, 'rg_bx': d_bx, 'rg_lambda': d_lam, 'norm2_g': d_norm2,
        'norm_mem_g': d_norm_mem, 'norm3_g': d_norm3, 'ffn_conv_w': d_ffn_cw, 'ffn_conv_b': d_ffn_cb,
        'final_g': d_final}
    small = [n for n in WEIGHT_NAMES if n not in BIG_AXIS]
    red_shapes = [(1,)] + [tuple(small_parts[n].shape) for n in small]
    fwd_tok = sum(gg.forward(d_norm1)[0:1, 0:1] for _, gg in grad_groups)
    reduced = _allreduce_small(_pack([loss8[0:1, 0:1] + fwd_tok] + [small_parts[n] for n in small], SUB),
                               name="allreduce_small")
    red = _unpack(reduced, red_shapes)
    loss = red[0][0]
    g_small = dict(zip(small, red[1:]))
    for n in SMALL_SHARDED:
        w_local = W[n].shape[-1]
        g_small[n] = lax.dynamic_slice_in_dim(g_small[n], chip * w_local, w_local, axis=1)

    out_g, out_d, out_m, out_v = {}, {}, {}, {}
    rows = 512
    pk = lambda d: _pack([d[n] for n in small], rows)
    g_pack = _pack([g_small[n] for n in small], rows)
    res_small = _adamw(pk(W), pk(Mo), pk(Vo), g_pack[None], name="adamw_small")
    shapes_small = [tuple(W[n].shape) for n in small]
    for dst, packed in zip((out_g, out_d, out_m, out_v), res_small):
        for n, val in zip(small, _unpack(packed, shapes_small)):
            dst[n] = val
    last = res_small[0]
    for names, gg in grad_groups:
        for n, land in zip(names, gg.finish(last)):
            g, d, m_new, v_new = _adamw(shards[n], Mo[n][0], Vo[n][0], land, name="adamw_" + n)
            out_g[n], out_d[n], out_m[n], out_v[n] = (t.reshape(W[n].shape) for t in (g, d, m_new, v_new))
            last = g
    return (loss, grad_x[None], *[out_g[n] for n in WEIGHT_NAMES], *[out_d[n] for n in WEIGHT_NAMES],
            *[out_m[n] for n in WEIGHT_NAMES], *[out_v[n] for n in WEIGHT_NAMES])


def kernel(x, mem, positions, norm1_g, w_in, ret_g, rg_conv_w, rg_conv_b, rg_wa, rg_ba, rg_wx, rg_bx, rg_lambda, w_out, norm2_g, norm_mem_g, xa_wq, xa_wk, xa_wv, xa_wo, norm3_g, ffn_w_up, ffn_conv_w, ffn_conv_b, ffn_w_down, final_g, loss_target, m_norm1_g, m_w_in, m_ret_g, m_rg_conv_w, m_rg_conv_b, m_rg_wa, m_rg_ba, m_rg_wx, m_rg_bx, m_rg_lambda, m_w_out, m_norm2_g, m_norm_mem_g, m_xa_wq, m_xa_wk, m_xa_wv, m_xa_wo, m_norm3_g, m_ffn_w_up, m_ffn_conv_w, m_ffn_conv_b, m_ffn_w_down, m_final_g, v_norm1_g, v_w_in, v_ret_g, v_rg_conv_w, v_rg_conv_b, v_rg_wa, v_rg_ba, v_rg_wx, v_rg_bx, v_rg_lambda, v_w_out, v_norm2_g, v_norm_mem_g, v_xa_wq, v_xa_wk, v_xa_wv, v_xa_wo, v_norm3_g, v_ffn_w_up, v_ffn_conv_w, v_ffn_conv_b, v_ffn_w_down, v_final_g):
    W = dict(zip(WEIGHT_NAMES, (norm1_g, w_in, ret_g, rg_conv_w, rg_conv_b, rg_wa, rg_ba, rg_wx, rg_bx, rg_lambda, w_out,
                                norm2_g, norm_mem_g, xa_wq, xa_wk, xa_wv, xa_wo, norm3_g, ffn_w_up, ffn_conv_w,
                                ffn_conv_b, ffn_w_down, final_g)))
    Mo = dict(zip(WEIGHT_NAMES, (m_norm1_g, m_w_in, m_ret_g, m_rg_conv_w, m_rg_conv_b, m_rg_wa, m_rg_ba, m_rg_wx, m_rg_bx,
                                 m_rg_lambda, m_w_out, m_norm2_g, m_norm_mem_g, m_xa_wq, m_xa_wk, m_xa_wv, m_xa_wo,
                                 m_norm3_g, m_ffn_w_up, m_ffn_conv_w, m_ffn_conv_b, m_ffn_w_down, m_final_g)))
    Vo = dict(zip(WEIGHT_NAMES, (v_norm1_g, v_w_in, v_ret_g, v_rg_conv_w, v_rg_conv_b, v_rg_wa, v_rg_ba, v_rg_wx, v_rg_bx,
                                 v_rg_lambda, v_w_out, v_norm2_g, v_norm_mem_g, v_xa_wq, v_xa_wk, v_xa_wv, v_xa_wo,
                                 v_norm3_g, v_ffn_w_up, v_ffn_conv_w, v_ffn_conv_b, v_ffn_w_down, v_final_g)))
    return _step(x, mem, positions, loss_target, W, Mo, Vo)
```

```python
import math

import jax
import jax.numpy as jnp
from jax import lax
from jax.experimental import pallas as pl
from jax.experimental.pallas import tpu as pltpu

F32 = jnp.float32
BF16 = jnp.bfloat16

EPS = 1e-6
RET_HEADS = 4
RET_CHUNK = 128
ROPE_BASE = 10000.0
LRU_BLOCKS = 8
LRU_C = 8.0
XA_HEADS = 4

ADAM_LR = 0.001
ADAM_B1 = 0.9
ADAM_B2 = 0.999
ADAM_EPS = 1e-08
ADAM_WD = 0.01
ADAM_STEP = 10

N_DEV = 8
N_CHIP = 4
MESH = pl.DeviceIdType.MESH
SUB = 8
LANE = 128
VMEM_LIMIT = 56 * 1024 * 1024

NN = ((1,), (0,))
NT = ((1,), (1,))
TN = ((0,), (0,))


def _cparams(sem):
    return pltpu.CompilerParams(dimension_semantics=sem, vmem_limit_bytes=VMEM_LIMIT)


def _sigmoid(v):
    return 1.0 / (1.0 + jnp.exp(-v))


def _bdot(a, b, dims):
    return lax.dot_general(a.astype(BF16), b.astype(BF16), (dims, ((), ())), preferred_element_type=F32)


def _row_iota(shape):
    return lax.broadcasted_iota(jnp.int32, shape, 0)


def _shift_down(v, tail, k):
    if k == 0:
        return v
    r = pltpu.roll(v, k, 0)
    rt = pltpu.roll(tail, k, 0)
    first = jnp.where(_row_iota(rt.shape) < k, rt, r[0:SUB])
    return jnp.concatenate([first, r[SUB:]], axis=0)


def _shift_up(v, head, k):
    if k == 0:
        return v
    n = v.shape[0]
    r = pltpu.roll(v, n - k, 0)
    rh = pltpu.roll(head, SUB - k, 0)
    last = jnp.where(_row_iota(rh.shape) >= SUB - k, rh, r[n - SUB:n])
    return jnp.concatenate([r[:n - SUB], last], axis=0)


def _mm(a, b, *, mode, M, N, K, out_dtype, name, tm=512, tn=512, tk=512, b_off=(0, 0), res=None, norm_g=None,
        after=None):
    tm, tn, tk = min(tm, M), min(tn, N), min(tk, K)
    assert M % tm == 0 and N % tn == 0 and K % tk == 0, (name, M, N, K, tm, tn, tk)
    nk = K // tk
    if mode == "nn":
        a_blk, b_blk, dims = (tm, tk), (tk, tn), NN
        a_map = lambda i, j, k: (i, k)
        b_map = lambda i, j, k: (k + b_off[0] // tk, j + b_off[1] // tn)
    elif mode == "nt":
        a_blk, b_blk, dims = (tm, tk), (tn, tk), NT
        a_map = lambda i, j, k: (i, k)
        b_map = lambda i, j, k: (j + b_off[0] // tn, k + b_off[1] // tk)
    else:
        a_blk, b_blk, dims = (tk, tm), (tk, tn), TN
        a_map = lambda i, j, k: (k, i)
        b_map = lambda i, j, k: (k + b_off[0] // tk, j + b_off[1] // tn)
    assert b_off[0] % b_blk[0] == 0 and b_off[1] % b_blk[1] == 0, (name, b_off, b_blk)
    has_res, has_norm, has_after = res is not None, norm_g is not None, after is not None
    assert not has_norm or tn == N

    def body(*refs):
        refs = list(refs)
        a_ref, b_ref = refs[0], refs[1]
        pos = 2
        r_ref = g_ref = n_ref = None
        if has_res:
            r_ref = refs[pos]
            pos += 1
        if has_norm:
            g_ref = refs[pos]
            pos += 1
        pos += has_after
        o_ref = refs[pos]
        pos += 1
        if has_norm:
            n_ref = refs[pos]
            pos += 1
        acc = refs[pos] if nk > 1 else None
        k = pl.program_id(2)
        part = _bdot(a_ref[...], b_ref[...], dims)

        def finish(total):
            if has_res:
                total = total + r_ref[...].astype(F32)
            o_ref[...] = total.astype(o_ref.dtype)
            if has_norm:
                r = lax.rsqrt(jnp.mean(total * total, axis=-1, keepdims=True) + EPS)
                n_ref[...] = (total * r * g_ref[...]).astype(n_ref.dtype)

        if nk == 1:
            finish(part)
        else:
            @pl.when(k == 0)
            def _():
                acc[...] = part

            @pl.when(k > 0)
            def _():
                acc[...] += part

            @pl.when(k == nk - 1)
            def _():
                finish(acc[...])

    in_specs = [pl.BlockSpec(a_blk, a_map), pl.BlockSpec(b_blk, b_map)]
    args = [a, b]
    if has_res:
        in_specs.append(pl.BlockSpec((tm, tn), lambda i, j, k: (i, j)))
        args.append(res)
    if has_norm:
        in_specs.append(pl.BlockSpec((1, N), lambda i, j, k: (0, 0)))
        args.append(norm_g)
    if has_after:
        in_specs.append(pl.BlockSpec(memory_space=pl.ANY))
        args.append(after)
    out_shape = jax.ShapeDtypeStruct((M, N), out_dtype)
    out_specs = pl.BlockSpec((tm, tn), lambda i, j, k: (i, j))
    if has_norm:
        out_shape = [out_shape, jax.ShapeDtypeStruct((M, N), BF16)]
        out_specs = [out_specs, pl.BlockSpec((tm, tn), lambda i, j, k: (i, j))]
    return pl.pallas_call(
        body, name=name, grid=(M // tm, N // tn, nk), in_specs=in_specs,
        out_specs=out_specs, out_shape=out_shape,
        scratch_shapes=[pltpu.VMEM((tm, tn), F32)] if nk > 1 else [],
        compiler_params=_cparams(("parallel", "parallel", "arbitrary")),
    )(*args)


def _rmsnorm_fwd(x, g, *, name, ts=512):
    S, D = x.shape
    ts = min(ts, S)

    def body(x_ref, g_ref, o_ref):
        xv = x_ref[...]
        r = lax.rsqrt(jnp.mean(xv * xv, axis=-1, keepdims=True) + EPS)
        o_ref[...] = (xv * r * g_ref[...]).astype(o_ref.dtype)

    return pl.pallas_call(
        body, name=name, grid=(S // ts,),
        in_specs=[pl.BlockSpec((ts, D), lambda i: (i, 0)), pl.BlockSpec((1, D), lambda i: (0, 0))],
        out_specs=pl.BlockSpec((ts, D), lambda i: (i, 0)),
        out_shape=jax.ShapeDtypeStruct((S, D), BF16),
        compiler_params=_cparams(("parallel",)),
    )(x, g)


def _rmsnorm_bwd(x, dxn, g, res, *, name, ts=512, emit_bf16=False):
    S, D = x.shape
    ts = min(ts, S)
    has_res = res is not None

    def body(*refs):
        refs = list(refs)
        dx16_ref = refs.pop() if emit_bf16 else None
        if has_res:
            x_ref, d_ref, g_ref, r_ref, dx_ref, dg_ref = refs
        else:
            x_ref, d_ref, g_ref, dx_ref, dg_ref = refs
        i = pl.program_id(0)
        xv = x_ref[...]
        dv = d_ref[...].astype(F32)
        r = lax.rsqrt(jnp.mean(xv * xv, axis=-1, keepdims=True) + EPS)
        gd = dv * g_ref[...]
        proj = jnp.mean(xv * gd, axis=-1, keepdims=True)
        dx = r * gd - xv * (r * r * r) * proj
        if has_res:
            dx = dx + r_ref[...]
        dx_ref[...] = dx
        if emit_bf16:
            dx16_ref[...] = dx.astype(BF16)
        part = jnp.sum(dv * xv * r, axis=0, keepdims=True)

        @pl.when(i == 0)
        def _():
            dg_ref[...] = part

        @pl.when(i > 0)
        def _():
            dg_ref[...] += part

    row = pl.BlockSpec((ts, D), lambda i: (i, 0))
    vec = pl.BlockSpec((1, D), lambda i: (0, 0))
    in_specs = [row, row, vec] + ([row] if has_res else [])
    args = [x, dxn, g] + ([res] if has_res else [])
    extra = emit_bf16 * [jax.ShapeDtypeStruct((S, D), BF16)]
    return pl.pallas_call(
        body, name=name, grid=(S // ts,), in_specs=in_specs, out_specs=[row, vec] + emit_bf16 * [row],
        out_shape=[jax.ShapeDtypeStruct((S, D), F32), jax.ShapeDtypeStruct((1, D), F32)] + extra,
        compiler_params=_cparams(("arbitrary",)),
    )(*args)


def _mm_norm_bwd(d, w, x, g, res, *, name, tm=512):
    M, K = d.shape
    N = w.shape[0]
    tm = min(tm, M)

    def body(d_ref, w_ref, x_ref, g_ref, r_ref, dx_ref, dx16_ref, dg_ref):
        i = pl.program_id(0)
        dv = _bdot(d_ref[...], w_ref[...], NT)
        xv = x_ref[...]
        r = lax.rsqrt(jnp.mean(xv * xv, axis=-1, keepdims=True) + EPS)
        gd = dv * g_ref[...]
        proj = jnp.mean(xv * gd, axis=-1, keepdims=True)
        dx = r * gd - xv * (r * r * r) * proj + r_ref[...]
        dx_ref[...] = dx
        dx16_ref[...] = dx.astype(BF16)
        part = jnp.sum(dv * xv * r, axis=0, keepdims=True)

        @pl.when(i == 0)
        def _():
            dg_ref[...] = part

        @pl.when(i > 0)
        def _():
            dg_ref[...] += part

    row = pl.BlockSpec((tm, N), lambda i: (i, 0))
    vec = pl.BlockSpec((1, N), lambda i: (0, 0))
    return pl.pallas_call(
        body, name=name, grid=(M // tm,),
        in_specs=[pl.BlockSpec((tm, K), lambda i: (i, 0)), pl.BlockSpec((N, K), lambda i: (0, 0)), row, vec, row],
        out_specs=[row, row, vec],
        out_shape=[jax.ShapeDtypeStruct((M, N), F32), jax.ShapeDtypeStruct((M, N), BF16),
                   jax.ShapeDtypeStruct((1, N), F32)],
        compiler_params=_cparams(("arbitrary",)),
    )(d, w, x, g, res)


def _final_loss(x, target, g, *, name, ts=512):
    S, D = x.shape
    ts = min(ts, S)

    def body(x_ref, t_ref, g_ref, dx_ref, dg_ref, loss_ref, dx16_ref):
        i = pl.program_id(0)
        xv = x_ref[...]
        gv = g_ref[...]
        r = lax.rsqrt(jnp.mean(xv * xv, axis=-1, keepdims=True) + EPS)
        y = xv * r * gv
        err = y - t_ref[...]
        row_loss = jnp.mean(err * err, axis=-1, keepdims=True)
        lpart = 0.5 * jnp.sum(row_loss, axis=0, keepdims=True)
        dy = err * (1.0 / D)
        gd = dy * gv
        proj = jnp.mean(xv * gd, axis=-1, keepdims=True)
        dx = r * gd - xv * (r * r * r) * proj
        dx_ref[...] = dx
        dx16_ref[...] = dx.astype(BF16)
        part = jnp.sum(dy * xv * r, axis=0, keepdims=True)
        lfull = jnp.broadcast_to(lpart, loss_ref.shape)

        @pl.when(i == 0)
        def _():
            dg_ref[...] = part
            loss_ref[...] = lfull

        @pl.when(i > 0)
        def _():
            dg_ref[...] += part
            loss_ref[...] += lfull

    row = pl.BlockSpec((ts, D), lambda i: (i, 0))
    vec = pl.BlockSpec((1, D), lambda i: (0, 0))
    return pl.pallas_call(
        body, name=name, grid=(S // ts,), in_specs=[row, row, vec],
        out_specs=[row, vec, pl.BlockSpec((SUB, LANE), lambda i: (0, 0)), row],
        out_shape=[jax.ShapeDtypeStruct((S, D), F32), jax.ShapeDtypeStruct((1, D), F32),
                   jax.ShapeDtypeStruct((SUB, LANE), F32), jax.ShapeDtypeStruct((S, D), BF16)],
        compiler_params=_cparams(("arbitrary",)),
    )(x, target, g)


def _rope_table(pos_col, inv, *, name, ts=1024):
    S = pos_col.shape[0]
    ts = min(ts, S)
    half = inv.shape[1]

    def body(p_ref, inv_ref, c_ref, s_ref):
        ang = p_ref[...].astype(F32) * inv_ref[...]
        c_ref[...] = jnp.cos(ang)
        s_ref[...] = jnp.sin(ang)

    tab = pl.BlockSpec((ts, half), lambda i: (i, 0))
    return pl.pallas_call(
        body, name=name, grid=(S // ts,),
        in_specs=[pl.BlockSpec((ts, 1), lambda i: (i, 0)), pl.BlockSpec((1, half), lambda i: (0, 0))],
        out_specs=[tab, tab],
        out_shape=[jax.ShapeDtypeStruct((S, half), F32), jax.ShapeDtypeStruct((S, half), F32)],
        compiler_params=_cparams(("parallel",)),
    )(pos_col, inv)


def _ret_consts(C, log_g):
    ii = lax.broadcasted_iota(jnp.int32, (C, C), 0)
    jj = lax.broadcasted_iota(jnp.int32, (C, C), 1)
    diff = (ii - jj).astype(F32)
    intra = jnp.where(ii >= jj, jnp.exp(log_g * jnp.maximum(diff, 0.0)), 0.0)
    idx = lax.broadcasted_iota(jnp.int32, (C, 1), 0).astype(F32)
    qd = jnp.exp(log_g * (idx + 1.0))
    kd = jnp.exp(log_g * (C - 1.0 - idx))
    cd = math.exp(log_g * C)
    return intra, qd, kd, cd


def _rot(t, cs, sn):
    half = t.shape[-1] // 2
    t1, t2 = t[:, :half], t[:, half:]
    return jnp.concatenate([t1 * cs - t2 * sn, t1 * sn + t2 * cs], axis=-1)


def _unrot(d, cs, sn):
    half = d.shape[-1] // 2
    d1, d2 = d[:, :half], d[:, half:]
    return jnp.concatenate([d1 * cs + d2 * sn, d2 * cs - d1 * sn], axis=-1)


def _ret_fwd(h, cos, sin, ret_g, mix, *, name, ch=4):
    S = h.shape[0]
    R = ret_g.shape[1]
    H, C = RET_HEADS, RET_CHUNK
    Dh = R // H
    ts = ch * C
    assert S % ts == 0
    log_gs = [math.log(1.0 - 2.0 ** (-5.0 - hd)) for hd in range(H)]
    scale = Dh ** -0.5

    def body(x_ref, c_ref, s_ref, rg_ref, mix_in, ret_ref, st_ref, mix_ref, state):
        i = pl.program_id(0)

        @pl.when(i == 0)
        def _():
            state[...] = jnp.zeros_like(state)

        for c in range(ch):
            rows = pl.ds(c * C, C)
            cs, sn = c_ref[rows, :], s_ref[rows, :]
            for hd in range(H):
                intra, qd, kd, cd = _ret_consts(C, log_gs[hd])
                q = x_ref[rows, pl.ds(hd * Dh, Dh)]
                k = x_ref[rows, pl.ds(R + hd * Dh, Dh)]
                v = x_ref[rows, pl.ds(2 * R + hd * Dh, Dh)]
                g = x_ref[rows, pl.ds(3 * R + hd * Dh, Dh)]
                rq = _rot(q, cs, sn)
                rk = _rot(k, cs, sn) * scale
                st = state[hd]
                st_ref[c, hd] = st.astype(BF16)
                s_ = _bdot(rq, rk, NT) * intra
                ret = _bdot(s_, v, NN) + _bdot(rq * qd, st, NN)
                state[hd] = st * cd + _bdot(rk * kd, v, TN)
                ret_ref[rows, pl.ds(hd * Dh, Dh)] = ret
                rr = lax.rsqrt(jnp.mean(ret * ret, axis=-1, keepdims=True) + EPS)
                out = ret * rr * rg_ref[:, pl.ds(hd * Dh, Dh)] * (g * _sigmoid(g))
                mix_ref[rows, pl.ds(hd * Dh, Dh)] = out.astype(BF16)

    n_chunks = S // C
    return pl.pallas_call(
        body, name=name, grid=(S // ts,),
        in_specs=[pl.BlockSpec((ts, 4 * R), lambda i: (i, 0)),
                  pl.BlockSpec((ts, Dh // 2), lambda i: (i, 0)), pl.BlockSpec((ts, Dh // 2), lambda i: (i, 0)),
                  pl.BlockSpec((1, R), lambda i: (0, 0)), pl.BlockSpec(memory_space=pl.ANY)],
        out_specs=[pl.BlockSpec((ts, R), lambda i: (i, 0)),
                   pl.BlockSpec((ch, H, Dh, Dh), lambda i: (i, 0, 0, 0)),
                   pl.BlockSpec((ts, R), lambda i: (i, 0))],
        out_shape=[jax.ShapeDtypeStruct((S, R), F32), jax.ShapeDtypeStruct((n_chunks, H, Dh, Dh), BF16),
                   jax.ShapeDtypeStruct(mix.shape, mix.dtype)],
        scratch_shapes=[pltpu.VMEM((H, Dh, Dh), F32)],
        input_output_aliases={4: 2},
        compiler_params=_cparams(("arbitrary",)),
    )(h, cos, sin, ret_g, mix)


def _ret_bwd(h, cos, sin, ret_g, states, ret_raw, dmix, *, name, ch=4):
    S = h.shape[0]
    R = ret_g.shape[1]
    H, C = RET_HEADS, RET_CHUNK
    Dh = R // H
    ts = ch * C
    nb = S // ts
    log_gs = [math.log(1.0 - 2.0 ** (-5.0 - hd)) for hd in range(H)]
    scale = Dh ** -0.5

    def body(x_ref, c_ref, s_ref, rg_ref, st_ref, ret_ref, dm_ref, dh_ref, drg_ref, dstate):
        i = pl.program_id(0)

        @pl.when(i == 0)
        def _():
            dstate[...] = jnp.zeros_like(dstate)
            drg_ref[...] = jnp.zeros_like(drg_ref)

        for c in reversed(range(ch)):
            rows = pl.ds(c * C, C)
            cs, sn = c_ref[rows, :], s_ref[rows, :]
            for hd in range(H):
                intra, qd, kd, cd = _ret_consts(C, log_gs[hd])
                cols = pl.ds(hd * Dh, Dh)
                q = x_ref[rows, pl.ds(hd * Dh, Dh)]
                k = x_ref[rows, pl.ds(R + hd * Dh, Dh)]
                v = x_ref[rows, pl.ds(2 * R + hd * Dh, Dh)]
                g = x_ref[rows, pl.ds(3 * R + hd * Dh, Dh)]
                rq = _rot(q, cs, sn)
                rk = _rot(k, cs, sn) * scale
                ret = ret_ref[rows, cols]
                dm = dm_ref[rows, cols].astype(F32)
                rgv = rg_ref[:, cols]
                rr = lax.rsqrt(jnp.mean(ret * ret, axis=-1, keepdims=True) + EPS)
                retn = ret * rr
                sg = _sigmoid(g)
                silu = g * sg
                drg_ref[:, cols] += jnp.sum(dm * retn * silu, axis=0, keepdims=True)
                dg = dm * retn * rgv * (sg * (1.0 + g * (1.0 - sg)))
                dretn = dm * rgv * silu
                d_o = rr * dretn - ret * (rr * rr * rr) * jnp.mean(ret * dretn, axis=-1, keepdims=True)
                st = st_ref[c, hd]
                d_s = dstate[hd]
                a_ = _bdot(rq, rk, NT) * intra
                d_a = _bdot(d_o, v, NT) * intra
                d_qr = _bdot(d_a, rk, NN) + _bdot(d_o, st, NT) * qd
                d_kr = _bdot(d_a, rq, TN) + _bdot(v, d_s, NT) * kd
                d_v = _bdot(a_, d_o, TN) + _bdot(rk * kd, d_s, NN)
                dstate[hd] = d_s * cd + _bdot(rq * qd, d_o, TN)
                dh_ref[rows, pl.ds(hd * Dh, Dh)] = _unrot(d_qr, cs, sn).astype(BF16)
                dh_ref[rows, pl.ds(R + hd * Dh, Dh)] = (_unrot(d_kr, cs, sn) * scale).astype(BF16)
                dh_ref[rows, pl.ds(2 * R + hd * Dh, Dh)] = d_v.astype(BF16)
                dh_ref[rows, pl.ds(3 * R + hd * Dh, Dh)] = dg.astype(BF16)

    rb = lambda i: nb - 1 - i
    return pl.pallas_call(
        body, name=name, grid=(nb,),
        in_specs=[pl.BlockSpec((ts, 4 * R), lambda i: (rb(i), 0)),
                  pl.BlockSpec((ts, Dh // 2), lambda i: (rb(i), 0)), pl.BlockSpec((ts, Dh // 2), lambda i: (rb(i), 0)),
                  pl.BlockSpec((1, R), lambda i: (0, 0)),
                  pl.BlockSpec((ch, H, Dh, Dh), lambda i: (rb(i), 0, 0, 0)),
                  pl.BlockSpec((ts, R), lambda i: (rb(i), 0)),
                  pl.BlockSpec((ts, R), lambda i: (rb(i), 0))],
        out_specs=[pl.BlockSpec((ts, 4 * R), lambda i: (rb(i), 0)), pl.BlockSpec((1, R), lambda i: (0, 0))],
        out_shape=[jax.ShapeDtypeStruct((S, 6 * R), BF16), jax.ShapeDtypeStruct((1, R), F32)],
        scratch_shapes=[pltpu.VMEM((H, Dh, Dh), F32)],
        compiler_params=_cparams(("arbitrary",)),
    )(h, cos, sin, ret_g, states, ret_raw, dmix)


GELU_C = math.sqrt(2.0 / math.pi)
GELU_A = 0.044715


def _gelu_parts(y):
    t = jnp.tanh(GELU_C * (y + GELU_A * y * y * y))
    val = 0.5 * y * (1.0 + t)
    grad = 0.5 * (1.0 + t) + 0.5 * y * (1.0 - t * t) * GELU_C * (1.0 + 3.0 * GELU_A * y * y)
    return val, grad


def _neg_expm1(x):
    series = -x * (1.0 + x * (1.0 / 2.0) * (1.0 + x * (1.0 / 3.0) * (1.0 + x * (1.0 / 4.0) * (
        1.0 + x * (1.0 / 5.0) * (1.0 + x * (1.0 / 6.0) * (1.0 + x * (1.0 / 7.0)))))))
    return jnp.where(x > -0.35, series, 1.0 - jnp.exp(x))


def _log_sigmoid(x):
    return jnp.minimum(x, 0.0) - jnp.log1p(jnp.exp(-jnp.abs(x)))


def _lru_gates(uc, wa_ref, ba_ref, wx_ref, bx_ref):
    nbk = wa_ref.shape[0]
    bd = wa_ref.shape[1]
    rs, gs = [], []
    for n in range(nbk):
        ucn = uc[:, n * bd:(n + 1) * bd]
        rs.append(_sigmoid(_bdot(ucn, wa_ref[n], NN) + ba_ref[:, pl.ds(n * bd, bd)]))
        gs.append(_sigmoid(_bdot(ucn, wx_ref[n], NN) + bx_ref[:, pl.ds(n * bd, bd)]))
    return jnp.concatenate(rs, axis=-1), jnp.concatenate(gs, axis=-1)


def _lru_fwd(h, conv_w, conv_b, wa, ba, wx, bx, lam, *, name, ts=256):
    S = h.shape[0]
    W = lam.shape[1]
    K = conv_w.shape[0]
    ts = min(ts, S)

    def body(u_ref, y_ref, cw_ref, cb_ref, wa_ref, ba_ref, wx_ref, bx_ref, lam_ref, hl_ref, mix_ref, tail, hlast):
        i = pl.program_id(0)

        @pl.when(i == 0)
        def _():
            tail[...] = jnp.zeros_like(tail)
            hlast[...] = jnp.zeros_like(hlast)

        u = u_ref[...]
        tl = tail[...]
        uc = cb_ref[...] + cw_ref[K - 1:K, :] * u
        for k in range(K - 1):
            uc = uc + cw_ref[k:k + 1, :] * _shift_down(u, tl, K - 1 - k)
        tail[...] = u[ts - SUB:ts]
        r, ig = _lru_gates(uc, wa_ref, ba_ref, wx_ref, bx_ref)
        log_a = LRU_C * r * _log_sigmoid(lam_ref[...])
        a = jnp.exp(log_a)
        b = jnp.sqrt(_neg_expm1(2.0 * log_a)) * (ig * uc)
        in_tile = _row_iota((ts, W)) & (SUB - 1)
        d = 1
        while d < SUB:
            a_s = jnp.where(in_tile < d, 1.0, pltpu.roll(a, d, 0))
            b_s = jnp.where(in_tile < d, 0.0, pltpu.roll(b, d, 0))
            b = a * b_s + b
            a = a * a_s
            d *= 2
        before = hlast[SUB - 1:SUB, :]
        for k in range(ts // SUB):
            tile = slice(k * SUB, (k + 1) * SUB)
            h_tile = a[tile] * before + b[tile]
            hl_ref[tile, :] = h_tile
            before = h_tile[SUB - 1:SUB, :]
        hlast[...] = hl_ref[ts - SUB:ts, :]
        gy, _ = _gelu_parts(y_ref[...])
        mix_ref[...] = (hl_ref[...] * gy).astype(BF16)

    full = lambda shape: pl.BlockSpec(shape, lambda i: tuple(0 for _ in shape))
    return pl.pallas_call(
        body, name=name, grid=(S // ts,),
        in_specs=[pl.BlockSpec((ts, W), lambda i: (i, 4)), pl.BlockSpec((ts, W), lambda i: (i, 5)),
                  full(conv_w.shape), full(conv_b.shape), full(wa.shape), full(ba.shape), full(wx.shape),
                  full(bx.shape), full(lam.shape)],
        out_specs=[pl.BlockSpec((ts, W), lambda i: (i, 0)), pl.BlockSpec((ts, W), lambda i: (i, 1))],
        out_shape=[jax.ShapeDtypeStruct((S, W), F32), jax.ShapeDtypeStruct((S, 2 * W), BF16)],
        scratch_shapes=[pltpu.VMEM((SUB, W), F32), pltpu.VMEM((SUB, W), F32)],
        compiler_params=_cparams(("arbitrary",)),
    )(h, h, conv_w, conv_b, wa, ba, wx, bx, lam)


def _lru_bwd(h, hl, dmix, dh, conv_w, conv_b, wa, ba, wx, bx, lam, *, name, ts=256):
    S = h.shape[0]
    W = lam.shape[1]
    K = conv_w.shape[0]
    nbk, bd = wa.shape[0], wa.shape[1]
    ts = min(ts, S)
    nb = S // ts
    t8 = ts // SUB

    def body(u_ref, y_ref, uh_ref, hl_ref, hh_ref, dm_ref, cw_ref, cb_ref, wa_ref, ba_ref, wx_ref, bx_ref, lam_ref,
             dh_in, dh_ref, dcw_ref, dcb_ref, dwa_ref, dba_ref, dwx_ref, dbx_ref, dlam_ref, carry, head, lam_buf):
        i = pl.program_id(0)
        blk = nb - 1 - i

        @pl.when(i == 0)
        def _():
            carry[...] = jnp.zeros_like(carry)
            head[...] = jnp.zeros_like(head)
            for ref in (dcw_ref, dcb_ref, dwa_ref, dba_ref, dwx_ref, dbx_ref, dlam_ref):
                ref[...] = jnp.zeros_like(ref)

        inside = (blk > 0).astype(F32)
        u = u_ref[...]
        tl = uh_ref[...] * inside
        sh = [_shift_down(u, tl, K - 1 - k) for k in range(K)]
        uc = cb_ref[...]
        for k in range(K):
            uc = uc + cw_ref[k:k + 1, :] * sh[k]
        r, ig = _lru_gates(uc, wa_ref, ba_ref, wx_ref, bx_ref)
        lam_v = lam_ref[...]
        ls = _log_sigmoid(lam_v)
        log_a = LRU_C * r * ls
        a = jnp.exp(log_a)
        mult = jnp.sqrt(_neg_expm1(2.0 * log_a))
        hcur = hl_ref[...]
        hprev = _shift_down(hcur, hh_ref[...] * inside, 1)
        gy, dgy = _gelu_parts(y_ref[...])
        dm = dm_ref[...].astype(F32)
        d_y = dm * hcur * dgy
        rid = _row_iota((ts, W))
        bq = dm * gy + jnp.where(rid == ts - 1, carry[0:1, :], 0.0)
        aq = jnp.where(rid == ts - 1, 0.0, pltpu.roll(a, ts - 1, 0))
        in_tile = rid & (SUB - 1)
        d = 1
        while d < SUB:
            a_s = jnp.where(in_tile >= SUB - d, 1.0, pltpu.roll(aq, ts - d, 0))
            b_s = jnp.where(in_tile >= SUB - d, 0.0, pltpu.roll(bq, ts - d, 0))
            bq = bq + aq * b_s
            aq = aq * a_s
            d *= 2
        after_row = jnp.zeros((1, W), F32)
        for k in reversed(range(ts // SUB)):
            tile = slice(k * SUB, (k + 1) * SUB)
            lam_tile = aq[tile] * after_row + bq[tile]
            lam_buf[tile, :] = lam_tile
            after_row = lam_tile[0:1, :]
        lam_t = lam_buf[...]
        carry[...] = (a * lam_t)[0:SUB]
        d_a = lam_t * hprev
        d_mult = lam_t * (ig * uc)
        d_i = lam_t * mult * uc
        d_uc = lam_t * mult * ig
        d_log_a = d_a * a - d_mult * (a * a) / mult
        d_r = d_log_a * (LRU_C * ls)
        dlam_ref[...] += jnp.sum(d_log_a * (LRU_C * r), axis=0, keepdims=True) * _sigmoid(-lam_v)
        d_pr = d_r * r * (1.0 - r)
        d_pi = d_i * ig * (1.0 - ig)
        dba_ref[...] += jnp.sum(d_pr, axis=0, keepdims=True)
        dbx_ref[...] += jnp.sum(d_pi, axis=0, keepdims=True)
        extra = []
        for n in range(nbk):
            sl = slice(n * bd, (n + 1) * bd)
            ucn = uc[:, sl]
            dwa_ref[n] += _bdot(ucn, d_pr[:, sl], TN)
            dwx_ref[n] += _bdot(ucn, d_pi[:, sl], TN)
            extra.append(_bdot(d_pr[:, sl], wa_ref[n], NT) + _bdot(d_pi[:, sl], wx_ref[n], NT))
        d_uc = d_uc + jnp.concatenate(extra, axis=-1)
        dcb_ref[...] += jnp.sum(d_uc, axis=0, keepdims=True)
        rid8 = _row_iota((SUB, W))
        dcw = jnp.zeros((SUB, W), F32)
        for k in range(K):
            dcw = dcw + jnp.where(rid8 == k, jnp.sum(d_uc * sh[k], axis=0, keepdims=True), 0.0)
        dcw_ref[...] += dcw
        hd = head[...]
        d_u = cw_ref[K - 1:K, :] * d_uc
        for j in range(1, K):
            d_u = d_u + cw_ref[K - 1 - j:K - j, :] * _shift_up(d_uc, hd, j)
        head[...] = d_uc[0:SUB]
        dh_ref[:, 0:W] = d_u.astype(BF16)
        dh_ref[:, W:2 * W] = d_y.astype(BF16)

    rb = lambda i: nb - 1 - i
    prev8 = lambda i: jnp.maximum(rb(i) * t8 - 1, 0)
    full = lambda shape: pl.BlockSpec(shape, lambda i: tuple(0 for _ in shape))
    small = [jax.ShapeDtypeStruct((SUB, W), F32), jax.ShapeDtypeStruct((1, W), F32),
             jax.ShapeDtypeStruct(wa.shape, F32), jax.ShapeDtypeStruct((1, W), F32),
             jax.ShapeDtypeStruct(wx.shape, F32), jax.ShapeDtypeStruct((1, W), F32),
             jax.ShapeDtypeStruct((1, W), F32)]
    return pl.pallas_call(
        body, name=name, grid=(nb,),
        in_specs=[pl.BlockSpec((ts, W), lambda i: (rb(i), 4)), pl.BlockSpec((ts, W), lambda i: (rb(i), 5)),
                  pl.BlockSpec((SUB, W), lambda i: (prev8(i), 4)),
                  pl.BlockSpec((ts, W), lambda i: (rb(i), 0)), pl.BlockSpec((SUB, W), lambda i: (prev8(i), 0)),
                  pl.BlockSpec((ts, W), lambda i: (rb(i), 1)),
                  full(conv_w.shape), full(conv_b.shape), full(wa.shape), full(ba.shape), full(wx.shape),
                  full(bx.shape), full(lam.shape), pl.BlockSpec(memory_space=pl.ANY)],
        out_specs=[pl.BlockSpec((ts, 2 * W), lambda i: (rb(i), 2))] + [full(s.shape) for s in small],
        out_shape=[jax.ShapeDtypeStruct(dh.shape, dh.dtype)] + small,
        scratch_shapes=[pltpu.VMEM((SUB, W), F32), pltpu.VMEM((SUB, W), F32), pltpu.VMEM((ts, W), F32)],
        input_output_aliases={13: 0},
        compiler_params=_cparams(("arbitrary",)),
    )(h, h, h, hl, hl, dmix, conv_w, conv_b, wa, ba, wx, bx, lam, dh)


def _xattn_fwd(q, km, vm, *, name, ts=1024):
    S, D = q.shape
    M = km.shape[0]
    H = XA_HEADS
    Dh = D // H
    ts = min(ts, S)
    scale = Dh ** -0.5

    def body(q_ref, k_ref, v_ref, o_ref):
        for hd in range(H):
            cols = pl.ds(hd * Dh, Dh)
            s = _bdot(q_ref[:, cols], k_ref[:, cols], NT) * scale
            s = s - jnp.max(s, axis=-1, keepdims=True)
            e = jnp.exp(s)
            p = e / jnp.sum(e, axis=-1, keepdims=True)
            o_ref[:, cols] = _bdot(p, v_ref[:, cols], NN).astype(o_ref.dtype)

    return pl.pallas_call(
        body, name=name, grid=(S // ts,),
        in_specs=[pl.BlockSpec((ts, D), lambda i: (i, 0)), pl.BlockSpec((M, D), lambda i: (0, 0)),
                  pl.BlockSpec((M, D), lambda i: (0, 0))],
        out_specs=pl.BlockSpec((ts, D), lambda i: (i, 0)),
        out_shape=jax.ShapeDtypeStruct((S, D), BF16),
        compiler_params=_cparams(("parallel",)),
    )(q, km, vm)


def _xattn_bwd(q, km, vm, d_o, *, name, ts=1024):
    S, D = q.shape
    M = km.shape[0]
    H = XA_HEADS
    Dh = D // H
    ts = min(ts, S)
    scale = Dh ** -0.5

    def body(q_ref, k_ref, v_ref, do_ref, dq_ref, dk_ref, dv_ref):
        i = pl.program_id(0)

        @pl.when(i == 0)
        def _():
            dk_ref[...] = jnp.zeros_like(dk_ref)
            dv_ref[...] = jnp.zeros_like(dv_ref)

        for hd in range(H):
            cols = pl.ds(hd * Dh, Dh)
            qh, kh, vh, doh = q_ref[:, cols], k_ref[:, cols], v_ref[:, cols], do_ref[:, cols]
            s = _bdot(qh, kh, NT) * scale
            s = s - jnp.max(s, axis=-1, keepdims=True)
            e = jnp.exp(s)
            p = e / jnp.sum(e, axis=-1, keepdims=True)
            dp = _bdot(doh, vh, NT)
            ds = p * (dp - jnp.sum(dp * p, axis=-1, keepdims=True)) * scale
            dq_ref[:, cols] = _bdot(ds, kh, NN).astype(dq_ref.dtype)
            dk_ref[:, cols] += _bdot(ds, qh, TN)
            dv_ref[:, cols] += _bdot(p, doh, TN)

    row = pl.BlockSpec((ts, D), lambda i: (i, 0))
    mem = pl.BlockSpec((M, D), lambda i: (0, 0))
    return pl.pallas_call(
        body, name=name, grid=(S // ts,), in_specs=[row, mem, mem, row], out_specs=[row, mem, mem],
        out_shape=[jax.ShapeDtypeStruct((S, D), BF16), jax.ShapeDtypeStruct((M, D), F32),
                   jax.ShapeDtypeStruct((M, D), F32)],
        compiler_params=_cparams(("arbitrary",)),
    )(q, km, vm, d_o)


def _conv_rows(v, tail, cw_ref, cb_ref):
    K = cw_ref.shape[0]
    sh = [_shift_down(v, tail, K - 1 - k) for k in range(K)]
    out = cb_ref[...]
    for k in range(K):
        out = out + cw_ref[k:k + 1, :] * sh[k]
    return out, sh


FFN_SUB = 256


def _ffn_up_gate(xn, w_up, cw, cb, *, name, tm=1024, tn=512):
    S, D = xn.shape
    F2 = w_up.shape[1]
    F = F2 // 2
    tm, tn = min(tm, S), min(tn, F)
    sub = min(FFN_SUB, tm)
    nj = F // tn
    K = cw.shape[0]

    def body(x_ref, wa_ref, wb_ref, cwa_ref, cwb_ref, cba_ref, cbb_ref, act_ref, ha_ref, hb_ref, ac_ref, bc_ref, ta, tb):
        i = pl.program_id(1)

        @pl.when(i == 0)
        def _():
            ta[...] = jnp.zeros_like(ta)
            tb[...] = jnp.zeros_like(tb)

        tail_a, tail_b = ta[...], tb[...]
        for s in range(tm // sub):
            rows = pl.ds(s * sub, sub)
            xs = x_ref[rows, :]
            ha = _bdot(xs, wa_ref[...], NN)
            hb = _bdot(xs, wb_ref[...], NN)
            ac, _ = _conv_rows(ha, tail_a, cwa_ref, cba_ref)
            bc, _ = _conv_rows(hb, tail_b, cwb_ref, cbb_ref)
            tail_a, tail_b = ha[sub - SUB:sub], hb[sub - SUB:sub]
            ha_ref[rows, :] = ha
            hb_ref[rows, :] = hb
            ac_ref[rows, :] = ac
            bc_ref[rows, :] = bc
            act_ref[rows, :] = (ac * _sigmoid(ac) * bc).astype(act_ref.dtype)
        ta[...] = tail_a
        tb[...] = tail_b

    blk = pl.BlockSpec((tm, tn), lambda j, i: (i, j))
    return pl.pallas_call(
        body, name=name, grid=(nj, S // tm),
        in_specs=[pl.BlockSpec((tm, D), lambda j, i: (i, 0)),
                  pl.BlockSpec((D, tn), lambda j, i: (0, j)), pl.BlockSpec((D, tn), lambda j, i: (0, j + nj)),
                  pl.BlockSpec((K, tn), lambda j, i: (0, j)), pl.BlockSpec((K, tn), lambda j, i: (0, j + nj)),
                  pl.BlockSpec((1, tn), lambda j, i: (0, j)), pl.BlockSpec((1, tn), lambda j, i: (0, j + nj))],
        out_specs=[blk] * 5,
        out_shape=[jax.ShapeDtypeStruct((S, F), BF16)] + [jax.ShapeDtypeStruct((S, F), F32)] * 4,
        scratch_shapes=[pltpu.VMEM((SUB, tn), F32), pltpu.VMEM((SUB, tn), F32)],
        compiler_params=_cparams(("parallel", "arbitrary")),
    )(xn, w_up, w_up, cw, cw, cb, cb)


def _ffn_bwd(dx, w_down, hh_a, hh_b, c_a, c_b, act, xn, cw, *, name, tm=1024, tn=256):
    S, D = dx.shape
    F = hh_a.shape[1]
    tm, tn = min(tm, S), min(tn, F)
    sub = min(FFN_SUB, tm)
    nj = F // tn
    nb = S // tm
    K = cw.shape[0]

    def body(dx_ref, wd_ref, a_ref, b_ref, ac_ref, bc_ref, act_ref, xn_ref, cwa_ref, cwb_ref,
             da_ref, db_ref, ga_ref, gb_ref, dwd_ref, dwu_ref, ha, hb, acc_d, acc_a, acc_b):
        i = pl.program_id(1)

        @pl.when(i == 0)
        def _():
            for ref in (ha, hb, ga_ref, gb_ref, acc_d, acc_a, acc_b):
                ref[...] = jnp.zeros_like(ref)

        rid8 = _row_iota((SUB, tn))
        heads = [ha[...], hb[...]]
        gsums = [jnp.zeros((SUB, tn), F32), jnp.zeros((SUB, tn), F32)]
        for s in reversed(range(tm // sub)):
            rows = pl.ds(s * sub, sub)
            dv = _bdot(dx_ref[rows, :], wd_ref[...], NT)
            ac, bc = ac_ref[rows, :], bc_ref[rows, :]
            sg = _sigmoid(ac)
            d_bc = dv * ac * sg
            d_ac = dv * bc * sg * (1.0 + ac * (1.0 - sg))
            for which, (d_c, h_ref, cw_ref, o_ref) in enumerate(((d_ac, a_ref, cwa_ref, da_ref),
                                                                 (d_bc, b_ref, cwb_ref, db_ref))):
                ahead = [d_c] + [_shift_up(d_c, heads[which], j) for j in range(1, K)]
                heads[which] = d_c[0:SUB]
                d_in = cw_ref[K - 1:K, :] * d_c
                for j in range(1, K):
                    d_in = d_in + cw_ref[K - 1 - j:K - j, :] * ahead[j]
                o_ref[rows, :] = d_in.astype(o_ref.dtype)
                hv = h_ref[rows, :]
                gsum = gsums[which] + jnp.where(rid8 == K, jnp.sum(d_c, axis=0, keepdims=True), 0.0)
                for k in range(K):
                    gsum = gsum + jnp.where(rid8 == k, jnp.sum(ahead[K - 1 - k] * hv, axis=0, keepdims=True), 0.0)
                gsums[which] = gsum
        ha[...], hb[...] = heads
        ga_ref[...] += gsums[0]
        gb_ref[...] += gsums[1]
        acc_d[...] += _bdot(act_ref[...], dx_ref[...], TN)
        acc_a[...] += _bdot(xn_ref[...], da_ref[...], TN)
        acc_b[...] += _bdot(xn_ref[...], db_ref[...], TN)

        @pl.when(i == nb - 1)
        def _():
            dwd_ref[...] = acc_d[...].astype(dwd_ref.dtype)
            dwu_ref[0] = acc_a[...].astype(dwu_ref.dtype)
            dwu_ref[1] = acc_b[...].astype(dwu_ref.dtype)

    rb = lambda i: nb - 1 - i
    blk = pl.BlockSpec((tm, tn), lambda j, i: (rb(i), j))
    acc = pl.BlockSpec((SUB, tn), lambda j, i: (0, j))
    rows_d = pl.BlockSpec((tm, D), lambda j, i: (rb(i), 0))
    return pl.pallas_call(
        body, name=name, grid=(nj, nb),
        in_specs=[rows_d, pl.BlockSpec((tn, D), lambda j, i: (j, 0)), blk, blk, blk, blk, blk, rows_d,
                  pl.BlockSpec((K, tn), lambda j, i: (0, j)), pl.BlockSpec((K, tn), lambda j, i: (0, j + nj))],
        out_specs=[blk, blk, acc, acc, pl.BlockSpec((tn, D), lambda j, i: (j, 0)),
                   pl.BlockSpec((2, D, tn), lambda j, i: (0, 0, j))],
        out_shape=[jax.ShapeDtypeStruct((S, F), BF16), jax.ShapeDtypeStruct((S, F), BF16),
                   jax.ShapeDtypeStruct((SUB, F), F32), jax.ShapeDtypeStruct((SUB, F), F32),
                   jax.ShapeDtypeStruct((F, D), BF16), jax.ShapeDtypeStruct((2, D, F), BF16)],
        scratch_shapes=[pltpu.VMEM((SUB, tn), F32), pltpu.VMEM((SUB, tn), F32), pltpu.VMEM((tn, D), F32),
                        pltpu.VMEM((D, tn), F32), pltpu.VMEM((D, tn), F32)],
        compiler_params=_cparams(("parallel", "arbitrary")),
    )(dx, w_down, hh_a, hh_b, c_a, c_b, act, xn, cw, cw)


ADAM_BLOCK_ELEMS = 256 * 1024


def _adamw(w, m, v, parts, *, name):
    R, C = w.shape
    n = parts.shape[0]
    tr = R
    for cand in (1024, 512, 256, 128, 64, 32, 16):
        if R % cand == 0 and cand * C <= ADAM_BLOCK_ELEMS:
            tr = cand
            break
    c1 = 1.0 - ADAM_B1 ** ADAM_STEP
    c2 = 1.0 - ADAM_B2 ** ADAM_STEP

    def body(w_ref, m_ref, v_ref, p_ref, g_ref, d_ref, nm_ref, nv_ref):
        g = p_ref[0].astype(F32)
        for k in range(1, n):
            g = g + p_ref[k].astype(F32)
        m_new = ADAM_B1 * m_ref[...] + (1.0 - ADAM_B1) * g
        v_new = ADAM_B2 * v_ref[...] + (1.0 - ADAM_B2) * (g * g)
        m_hat = m_new / c1
        v_hat = v_new / c2
        g_ref[...] = g
        d_ref[...] = -ADAM_LR * (m_hat / (jnp.sqrt(v_hat) + ADAM_EPS) + ADAM_WD * w_ref[...])
        nm_ref[...] = m_new
        nv_ref[...] = v_new

    blk = pl.BlockSpec((tr, C), lambda i: (i, 0))
    sds = jax.ShapeDtypeStruct((R, C), F32)
    return pl.pallas_call(
        body, name=name, grid=(R // tr,),
        in_specs=[blk, blk, blk, pl.BlockSpec((n, tr, C), lambda i: (0, i, 0))],
        out_specs=[blk, blk, blk, blk], out_shape=[sds, sds, sds, sds],
        compiler_params=_cparams(("parallel",)),
    )(w, m, v, parts)


def _mesh_place():
    x, y, c = lax.axis_index("x"), lax.axis_index("y"), lax.axis_index("c")
    others = [(1 - x, y), (x, 1 - y), (1 - x, 1 - y)]
    return x, y, c, others


HBM_SPEC = pl.BlockSpec(memory_space=pltpu.HBM)
SEM_SPEC = pl.BlockSpec(memory_space=pltpu.SEMAPHORE)
ANY_SPEC = pl.BlockSpec(memory_space=pl.ANY)
EFFECT = pltpu.SideEffectType.DATAFLOW_SIDE_EFFECTING


def _in_hbm(a):
    return pltpu.with_memory_space_constraint(a, pltpu.HBM)


def _split_start(srcs, lands, copies, n_cp, *, name):
    n_s, n_l = len(srcs), len(lands)

    def body(*refs):
        src_refs, land_refs = refs[:n_s], refs[n_s:n_s + n_l]
        ssem, rsem = refs[n_s + n_l], refs[n_s + n_l + 1]
        token = refs[-1]
        for outgoing, _ in copies(src_refs, land_refs, ssem, rsem):
            outgoing.start()
        token[...] = jnp.zeros_like(token)

    outs = pl.pallas_call(
        body, name=name,
        out_shape=(pltpu.SemaphoreType.DMA((n_cp,)), pltpu.SemaphoreType.DMA((n_cp,)),
                   *[pltpu.HBM(a.shape, a.dtype) for a in srcs], *[pltpu.HBM(a.shape, a.dtype) for a in lands],
                   jax.ShapeDtypeStruct((SUB, LANE), F32)),
        in_specs=[HBM_SPEC] * (n_s + n_l),
        out_specs=(SEM_SPEC, SEM_SPEC, *[HBM_SPEC] * (n_s + n_l), pl.BlockSpec(memory_space=pltpu.VMEM)),
        input_output_aliases={i: 2 + i for i in range(n_s + n_l)},
        compiler_params=pltpu.CompilerParams(has_side_effects=EFFECT),
    )(*[_in_hbm(a) for a in srcs], *[_in_hbm(a) for a in lands])
    ssem, rsem = outs[0], outs[1]
    return ssem, rsem, list(outs[2:2 + n_s]), list(outs[2 + n_s:2 + n_s + n_l]), outs[-1]


def _split_wait(srcs, lands, ssem, rsem, after, copies, *, name):
    n_s, n_l = len(srcs), len(lands)

    def body(*refs):
        src_refs, land_refs = refs[:n_s], refs[n_s:n_s + n_l]
        s_ref, r_ref = refs[n_s + n_l], refs[n_s + n_l + 1]
        for outgoing, incoming in copies(src_refs, land_refs, s_ref, r_ref):
            outgoing.wait_send()
            incoming.wait_recv()

    outs = pl.pallas_call(
        body, name=name,
        out_shape=(*[pltpu.HBM(a.shape, a.dtype) for a in srcs], *[pltpu.HBM(a.shape, a.dtype) for a in lands]),
        in_specs=[HBM_SPEC] * (n_s + n_l) + [SEM_SPEC, SEM_SPEC, ANY_SPEC], out_specs=[HBM_SPEC] * (n_s + n_l),
        input_output_aliases={i: i for i in range(n_s + n_l)},
        compiler_params=pltpu.CompilerParams(has_side_effects=EFFECT),
    )(*srcs, *lands, ssem, rsem, after)
    return list(outs[:n_s]), list(outs[n_s:])


PLACE_BLOCK_ELEMS = 512 * 1024


def _place_rows(r, w):
    return _div_tile(r, max(16, PLACE_BLOCK_ELEMS // w), 16)


def _cast_place(shard, chip, axis, after, *, name):
    r, w = shard.shape
    tr = _place_rows(r, w)
    nb = r // tr
    full = (r * N_CHIP, w) if axis == 0 else (r, w * N_CHIP)
    has_after = after is not None

    def body(chip_ref, s_ref, *rest):
        rest[-1][...] = s_ref[...].astype(BF16)

    out_map = (lambda i, ch: (ch[0] * nb + i, 0)) if axis == 0 else (lambda i, ch: (i, ch[0]))
    grid_spec = pltpu.PrefetchScalarGridSpec(
        num_scalar_prefetch=1, grid=(nb,),
        in_specs=[pl.BlockSpec((tr, w), lambda i, ch: (i, 0))] + has_after * [ANY_SPEC],
        out_specs=pl.BlockSpec((tr, w), out_map))
    return pl.pallas_call(body, name=name, grid_spec=grid_spec, out_shape=jax.ShapeDtypeStruct(full, BF16),
                          compiler_params=_cparams(("parallel",)))(chip, shard, *(has_after * [after]))


def _grad_shard_shape(g, axis):
    if g.ndim == 3:
        return g.shape[1], 2 * g.shape[2] // N_CHIP
    return (g.shape[0] // N_CHIP, g.shape[1]) if axis == 0 else (g.shape[0], g.shape[1] // N_CHIP)


def _slot_place(g, ids, axis, *, name):
    r, w = _grad_shard_shape(g, axis)
    tr = _place_rows(r, w)
    nb = r // tr

    def body(ids_ref, g_ref, o_ref):
        o_ref[...] = g_ref[...]

    if g.ndim == 3:
        in_spec = pl.BlockSpec((None, tr, w), lambda i, ids_: (ids_[0] // 2, i, ids_[0] % 2))
    elif axis == 0:
        in_spec = pl.BlockSpec((tr, w), lambda i, ids_: (ids_[0] * nb + i, 0))
    else:
        in_spec = pl.BlockSpec((tr, w), lambda i, ids_: (i, ids_[0]))
    grid_spec = pltpu.PrefetchScalarGridSpec(
        num_scalar_prefetch=1, grid=(nb,), in_specs=[in_spec],
        out_specs=pl.BlockSpec((None, tr, w), lambda i, ids_: (ids_[1], i, 0)))
    return pl.pallas_call(body, name=name, grid_spec=grid_spec, out_shape=jax.ShapeDtypeStruct((N_DEV, r, w), g.dtype),
                          compiler_params=_cparams(("parallel",)))(ids, g)


class _WeightGather:
    def __init__(self, placed, shard_shapes, axes, splits, tag):
        self.placed, self.shard_shapes, self.axes, self.splits, self.tag = list(placed), shard_shapes, axes, splits, tag
        self.n = len(placed)

    def _region(self, land_refs, it, chip, half):
        r, w = self.shard_shapes[it]
        by_rows = self.axes[it] == 0
        if self.splits[it] and half is not None:
            rows = pl.ds(pl.multiple_of(half * (r // 2) + (chip * r if by_rows else 0), 16), r // 2)
        else:
            rows = pl.ds(chip * r if by_rows else 0, r)
        cols = pl.ds(0, w) if by_rows else pl.ds(pl.multiple_of(chip * w, LANE), w)
        return land_refs[it].at[rows, cols]

    def _ici(self, src_refs, land_refs, ssem, rsem):
        x, y, c, others = _mesh_place()
        pairs = []
        for it in range(self.n):
            for j, chip in enumerate(others):
                def mk(chip_from, it=it, j=j, chip=chip):
                    return pltpu.make_async_remote_copy(
                        src_ref=self._region(land_refs, it, 2 * x + y, c), dst_ref=self._region(land_refs, it, chip_from, c),
                        send_sem=ssem.at[3 * it + j], recv_sem=rsem.at[3 * it + j], device_id=(*chip, c),
                        device_id_type=MESH)
                pairs.append((mk(2 * x + y), mk(2 * chip[0] + chip[1])))
        return pairs

    def start(self):
        self.ssem, self.rsem, _, self.lands, token = _split_start(
            [], self.placed, self._ici, 3 * self.n, name="gather_start_" + self.tag)
        return token

    def _d2d(self, src_refs, land_refs, ssem, rsem):
        x, y, c, others = _mesh_place()
        pairs = []
        for it in range(self.n):
            if self.splits[it]:
                for chip in others:
                    def mk(half, it=it, chip=chip, k=len(pairs)):
                        reg = self._region(land_refs, it, 2 * chip[0] + chip[1], half)
                        return pltpu.make_async_remote_copy(src_ref=reg, dst_ref=reg, send_sem=ssem.at[k], recv_sem=rsem.at[k],
                                                            device_id=(x, y, 1 - c), device_id_type=MESH)
                    pairs.append((mk(c), mk(1 - c)))
        return pairs

    def forward(self, after):
        _, lands = _split_wait([], self.lands, self.ssem, self.rsem, after, self._ici,
                               name="gather_wait_" + self.tag)
        self.fsem, self.frsem, _, self.lands, token = _split_start(
            [], lands, self._d2d, 3 * sum(self.splits), name="gather_fwd_" + self.tag)
        return token

    def finish_forward(self, after):
        _, lands = _split_wait([], self.lands, self.fsem, self.frsem, after, self._d2d,
                               name="gather_fwd_wait_" + self.tag)
        return lands

    def finish(self, after):
        _, lands = _split_wait([], self.lands, self.ssem, self.rsem, after, self._ici,
                               name="gather_wait_" + self.tag)
        n = self.n
        n_fwd = 3 * sum(self.splits)
        if n_fwd == 0:
            return lands

        def body(*refs):
            out_refs = refs[n:2 * n]
            fsend, frecv = refs[2 * n:]
            x, y, c, others = _mesh_place()
            sibling = (x, y, 1 - c)

            def fwd(it, slot, chip, half):
                reg = self._region(out_refs, it, 2 * chip[0] + chip[1], half)
                return pltpu.make_async_remote_copy(src_ref=reg, dst_ref=reg, send_sem=fsend.at[slot],
                                                    recv_sem=frecv.at[slot], device_id=sibling, device_id_type=MESH)

            sends, recvs = [], []
            for it in range(n):
                if self.splits[it]:
                    for chip in others:
                        sends.append(fwd(it, len(sends), chip, c))
                        recvs.append(fwd(it, len(recvs), chip, 1 - c))
            for cp in sends:
                cp.start()
            for cp in recvs:
                cp.wait_recv()
            for cp in sends:
                cp.wait_send()

        fulls = pl.pallas_call(
            body, name="gather_d2d_" + self.tag, in_specs=[ANY_SPEC] * n, out_specs=[ANY_SPEC] * n,
            out_shape=[jax.ShapeDtypeStruct(a.shape, a.dtype) for a in lands],
            scratch_shapes=[pltpu.SemaphoreType.DMA((n_fwd,)), pltpu.SemaphoreType.DMA((n_fwd,))],
            input_output_aliases={i: i for i in range(n)},
        )(*lands)
        return list(fulls)


class _GradGather:
    def __init__(self, grads, axes, tag):
        self.grads, self.axes, self.tag = list(grads), axes, tag
        self.n = len(grads)
        self.shard_shapes = [_grad_shard_shape(g, ax) for g, ax in zip(grads, axes)]

    def _piece(self, src_refs, it, chip):
        r, w = self.shard_shapes[it]
        if self.grads[it].ndim == 3:
            return src_refs[it].at[chip // 2, :, pl.ds(pl.multiple_of((chip % 2) * w, LANE), w)]
        if self.axes[it] == 0:
            return src_refs[it].at[pl.ds(pl.multiple_of(chip * r, 16), r), :]
        return src_refs[it].at[:, pl.ds(pl.multiple_of(chip * w, LANE), w)]

    PER_ITEM = 4

    def _remote(self, src_refs, land_refs, ssem, rsem):
        x, y, c, others = _mesh_place()
        me = 4 * x + 2 * y + c
        pairs = []
        for it in range(self.n):
            def mk(k, piece_chip, slot, to, it=it):
                return pltpu.make_async_remote_copy(
                    src_ref=self._piece(src_refs, it, piece_chip), dst_ref=land_refs[it].at[slot],
                    send_sem=ssem.at[self.PER_ITEM * it + k], recv_sem=rsem.at[self.PER_ITEM * it + k], device_id=to,
                    device_id_type=MESH)
            for j, chip in enumerate(others):
                chip_id = 2 * chip[0] + chip[1]
                pairs.append((mk(j, chip_id, me, (*chip, c)), mk(j, chip_id, 2 * chip_id + c, (*chip, c))))
            sibling = (x, y, 1 - c)
            pairs.append((mk(3, 2 * x + y, me, sibling), mk(3, 2 * x + y, 4 * x + 2 * y + 1 - c, sibling)))
        return pairs

    def start(self):
        x, y, c = lax.axis_index("x"), lax.axis_index("y"), lax.axis_index("c")
        ids = jnp.stack([2 * x + y, 4 * x + 2 * y + c]).astype(jnp.int32)
        lands = [_slot_place(g, ids, ax, name="grads_own_%s%d" % (self.tag, it))
                 for it, (g, ax) in enumerate(zip(self.grads, self.axes))]
        self.ssem, self.rsem, self.srcs, self.lands, token = _split_start(
            self.grads, lands, self._remote, self.PER_ITEM * self.n, name="grads_start_" + self.tag)
        return token

    def _forward(self, src_refs, land_refs, ssem, rsem):
        x, y, c, others = _mesh_place()
        pairs = []
        for it in range(self.n):
            for j, ch in enumerate(others):
                def mk(slot, it=it, j=j):
                    return pltpu.make_async_remote_copy(
                        src_ref=land_refs[it].at[slot], dst_ref=land_refs[it].at[slot], send_sem=ssem.at[3 * it + j],
                        recv_sem=rsem.at[3 * it + j], device_id=(x, y, 1 - c), device_id_type=MESH)
                pairs.append((mk(4 * ch[0] + 2 * ch[1] + c), mk(4 * ch[0] + 2 * ch[1] + 1 - c)))
        return pairs

    def forward(self, after):
        _, lands = _split_wait(self.srcs, self.lands, self.ssem, self.rsem, after, self._remote,
                               name="grads_wait_" + self.tag)
        self.fsem, self.frsem, _, self.lands, token = _split_start(
            [], lands, self._forward, 3 * self.n, name="grads_fwd_" + self.tag)
        return token

    def finish(self, after):
        _, lands = _split_wait([], self.lands, self.fsem, self.frsem, after, self._forward,
                               name="grads_fwd_wait_" + self.tag)
        return lands


def _allreduce_small(vec, *, name):
    R, L = vec.shape

    def body(v_ref, o_ref, buf, send, recv, lsem):
        x, y, c, others = _mesh_place()
        me = 4 * x + 2 * y + c
        sibling = (x, y, 1 - c)

        def copy(k, slot, to, src=None):
            return pltpu.make_async_remote_copy(
                src_ref=buf.at[slot] if src is None else src, dst_ref=buf.at[slot], send_sem=send.at[k],
                recv_sem=recv.at[k], device_id=to, device_id_type=MESH)

        def slot_of(chip, core):
            return 4 * chip[0] + 2 * chip[1] + core

        mine = pltpu.make_async_copy(v_ref, buf.at[me], lsem)
        mine.start()
        first = [copy(0, me, sibling, src=v_ref)]
        first += [copy(1 + j, me, (*chip, c), src=v_ref) for j, chip in enumerate(others)]
        for cp in first:
            cp.start()
        passed = [copy(4 + j, slot_of(chip, c), sibling) for j, chip in enumerate(others)]
        for j, chip in enumerate(others):
            copy(1 + j, slot_of(chip, c), (*chip, c)).wait_recv()
            passed[j].start()
        copy(0, slot_of((x, y), 1 - c), sibling).wait_recv()
        for j, chip in enumerate(others):
            copy(4 + j, slot_of(chip, 1 - c), sibling).wait_recv()
        for cp in first + passed:
            cp.wait_send()
        mine.wait()
        total = buf[0]
        for k in range(1, N_DEV):
            total = total + buf[k]
        o_ref[...] = total

    return pl.pallas_call(
        body, name=name, in_specs=[pl.BlockSpec(memory_space=pltpu.VMEM)],
        out_specs=pl.BlockSpec(memory_space=pltpu.VMEM), out_shape=jax.ShapeDtypeStruct((R, L), F32),
        scratch_shapes=[pltpu.VMEM((N_DEV, R, L), F32), pltpu.SemaphoreType.DMA((7,)), pltpu.SemaphoreType.DMA((7,)),
                        pltpu.SemaphoreType.DMA],
        compiler_params=pltpu.CompilerParams(vmem_limit_bytes=VMEM_LIMIT),
    )(vec)


PACK_ALIGN = 1024


def _pack(arrs, row_multiple):
    flat = []
    for a in arrs:
        f = a.reshape(-1).astype(F32)
        flat.append(jnp.pad(f, (0, (-f.shape[0]) % PACK_ALIGN)))
    v = jnp.concatenate(flat)
    v = jnp.pad(v, (0, (-v.shape[0]) % (LANE * row_multiple)))
    return v.reshape(-1, LANE)


def _unpack(v, shapes):
    flat = v.reshape(-1)
    out, off = [], 0
    for s in shapes:
        size = math.prod(s)
        out.append(flat[off:off + size].reshape(s))
        off += size + (-size) % PACK_ALIGN
    return out


def _div_tile(dim, cap, mult=LANE):
    best = None
    for cand in range(mult, min(cap, dim) + 1, mult):
        if dim % cand == 0:
            best = cand
    return dim if best is None else best


WEIGHT_NAMES = ('norm1_g', 'w_in', 'ret_g', 'rg_conv_w', 'rg_conv_b', 'rg_wa', 'rg_ba', 'rg_wx', 'rg_bx', 'rg_lambda',
                'w_out', 'norm2_g', 'norm_mem_g', 'xa_wq', 'xa_wk', 'xa_wv', 'xa_wo', 'norm3_g', 'ffn_w_up',
                'ffn_conv_w', 'ffn_conv_b', 'ffn_w_down', 'final_g')
BIG_AXIS = {'w_in': 1, 'w_out': 0, 'xa_wq': 0, 'xa_wk': 0, 'xa_wv': 0, 'xa_wo': 0, 'ffn_w_up': 1, 'ffn_w_down': 0}
SMALL_SHARDED = ('rg_conv_w', 'ffn_conv_w')


def _step(x, mem, positions, loss_target, W, Mo, Vo):
    S, D = x.shape[1], x.shape[2]
    xs, mems, tgt = x[0], mem[0], loss_target[0]
    n_mem = mems.shape[0]
    pos_col = positions.reshape(S, 1)
    chip = 2 * lax.axis_index("x") + lax.axis_index("y")

    big = list(BIG_AXIS)
    shards = {n: W[n][0] for n in big}
    G = {}
    gather_groups = (('w_in', 'rg_conv_w'), ('w_out', 'xa_wq', 'xa_wk', 'xa_wv', 'xa_wo'),
                     ('ffn_w_up', 'ffn_conv_w'), ('ffn_w_down',))
    gathers, tok = [], None
    chip1 = jnp.reshape(chip, (1,)).astype(jnp.int32)
    for gi, names in enumerate(gather_groups):
        placed = []
        for n in names:
            if n in BIG_AXIS:
                placed.append(_cast_place(shards[n], chip1, BIG_AXIS[n], tok, name="place_" + n))
            else:
                s = W[n][0] if tok is None else W[n][0] + tok[0, 0]
                full = lax.empty((s.shape[0], s.shape[1] * N_CHIP), s.dtype)
                placed.append(lax.dynamic_update_slice(full, s, (0, chip * s.shape[1])))
        ag = _WeightGather(placed, [W[n][0].shape for n in names], [BIG_AXIS.get(n, 1) for n in names],
                           [n in BIG_AXIS for n in names], "g%d" % gi)
        tok = ag.start()
        gathers.append(ag)

    def finish_gather(gi, after):
        G.update(zip(gather_groups[gi], gathers[gi].finish(after)))

    def finish_forward(gi, after):
        G.update(zip(gather_groups[gi], gathers[gi].finish_forward(after)))

    R = W['ret_g'].shape[1]
    Wl = W['rg_lambda'].shape[1]
    IN = W['w_in'].shape[2] * N_CHIP
    F2 = W['ffn_w_up'].shape[2] * N_CHIP
    F = F2 // 2

    norm1_g, norm2_g, norm3_g = W['norm1_g'] + tok[0, 0], W['norm2_g'], W['norm3_g']
    norm_mem_g, final_g, ret_g = W['norm_mem_g'], W['final_g'].reshape(1, D), W['ret_g']
    rg_cb = W['rg_conv_b']
    wa, wx = W['rg_wa'][0], W['rg_wx'][0]
    ba, bx = W['rg_ba'].reshape(1, Wl), W['rg_bx'].reshape(1, Wl)
    lam = W['rg_lambda']
    ffn_cb = W['ffn_conv_b']

    def fwd_mm(a, wname, N, K, **kw):
        return _mm(a, G[wname], mode="nn", M=a.shape[0], N=N, K=K, tm=_div_tile(a.shape[0], 1024),
                   tn=_div_tile(N, 2048, 512) if K <= 3072 else 512, tk=K, **kw)

    def fwd_mm_norm(a, wname, res, g, name):
        return _mm(a, G[wname], mode="nn", M=a.shape[0], N=D, K=a.shape[1], tm=512, tn=D, tk=_div_tile(a.shape[1], 2048),
                   out_dtype=F32, res=res, norm_g=g, name=name)

    def bwd_x_mm(d, wname, N, K, **kw):
        return _mm(d, G[wname], mode="nt", M=d.shape[0], N=N, K=K, tm=_div_tile(d.shape[0], 1024),
                   tn=_div_tile(N, 2048 if K <= 3072 else 512, 256), tk=K, **kw)

    def bwd_w_mm(a, d, M, N, **kw):
        Ks = a.shape[0]
        return _mm(a, d, mode="tn", M=M, N=N, K=Ks, out_dtype=BF16, tm=_div_tile(M, 1024, 256),
                   tn=_div_tile(N, 1024, 256), tk=_div_tile(Ks, 4096 if d.dtype == BF16 else 1024), **kw)

    xn1 = _rmsnorm_fwd(xs, norm1_g, name="norm1_fwd")
    half = (R // RET_HEADS) // 2
    inv = (ROPE_BASE ** (-jnp.arange(half, dtype=F32) / half)).reshape(1, half)
    cos, sin = _rope_table(pos_col, inv + tok[0, 0], name="rope_table")
    finish_gather(0, cos)
    rg_cw = G['rg_conv_w']
    h = fwd_mm(xn1, 'w_in', IN, D, out_dtype=F32, name="mm_in")
    hl, mix = _lru_fwd(h, rg_cw, rg_cb, wa, ba, wx, bx, lam, name="lru_fwd")
    t1 = gathers[1].forward(hl)
    ret_raw, states, mix = _ret_fwd(h, cos, sin, ret_g + t1[0, 0], mix, name="ret_fwd")
    finish_forward(1, mix)
    x1, xn2 = fwd_mm_norm(mix, 'w_out', xs, norm2_g, "mm_out")
    memn = _rmsnorm_fwd(mems, norm_mem_g, name="norm_mem_fwd")
    km = fwd_mm(memn, 'xa_wk', D, D, out_dtype=BF16, name="mm_k")
    vm = fwd_mm(memn, 'xa_wv', D, D, out_dtype=BF16, name="mm_v")
    t2 = gathers[2].forward(x1)
    q = fwd_mm(xn2, 'xa_wq', D, D, out_dtype=BF16, after=t2, name="mm_q")
    o = _xattn_fwd(q, km, vm, name="xattn_fwd")
    x2, xn3 = fwd_mm_norm(o, 'xa_wo', x1, norm3_g, "mm_o")
    finish_forward(2, xn3)
    t3 = gathers[3].forward(xn3)
    ffn_cw = G['ffn_conv_w']
    act, hh_a, hh_b, hc_a, hc_b = _ffn_up_gate(xn3, G['ffn_w_up'], ffn_cw, ffn_cb + t3[0, 0], name="ffn_up_gate")
    finish_forward(3, act)
    x3 = fwd_mm(act, 'ffn_w_down', D, F, out_dtype=F32, res=x2, name="mm_down")
    dx3, d_final, loss8, dx3h = _final_loss(x3, tgt, final_g, name="final_loss")

    gw = {}
    grad_groups = []

    def start_grads(names, tag):
        gg = _GradGather([gw[n] for n in names], [BIG_AXIS[n] for n in names], tag)
        grad_groups.append((names, gg))
        return gg.start()

    dhh_a, dhh_b, gcw_a, gcw_b, gw['ffn_w_down'], gw['ffn_w_up'] = _ffn_bwd(
        dx3h, G['ffn_w_down'], hh_a, hh_b, hc_a, hc_b, act, xn3, ffn_cw, name="ffn_bwd")
    tok_a = start_grads(('ffn_w_down', 'ffn_w_up'), "a")
    dxn3 = bwd_x_mm(dhh_a, 'ffn_w_up', D, F, out_dtype=F32, after=tok_a, name="mm_dxn3_a")
    dxn3 = bwd_x_mm(dhh_b, 'ffn_w_up', D, F, out_dtype=BF16, b_off=(0, F), res=dxn3, name="mm_dxn3_b")
    dx2, d_norm3, dx2h = _rmsnorm_bwd(x2, dxn3, norm3_g, dx3, name="norm3_bwd", emit_bf16=True)
    Kc = ffn_cw.shape[0]
    d_ffn_cw = jnp.concatenate([gcw_a[:Kc], gcw_b[:Kc]], axis=1)
    d_ffn_cb = jnp.concatenate([gcw_a[Kc:Kc + 1], gcw_b[Kc:Kc + 1]], axis=1)

    d_o = bwd_x_mm(dx2h, 'xa_wo', D, D, out_dtype=BF16, name="mm_do")
    gw['xa_wo'] = bwd_w_mm(o, dx2h, D, D, name="mm_dw_o")
    dq, dk, dv = _xattn_bwd(q, km, vm, d_o, name="xattn_bwd")
    gw['xa_wq'] = bwd_w_mm(xn2, dq, D, D, name="mm_dw_q")
    gw['xa_wk'] = bwd_w_mm(memn, dk, D, D, name="mm_dw_k")
    gw['xa_wv'] = bwd_w_mm(memn, dv, D, D, name="mm_dw_v")
    dmemn = bwd_x_mm(dk, 'xa_wk', D, D, out_dtype=F32, name="mm_dmem_k")
    dmemn = bwd_x_mm(dv, 'xa_wv', D, D, out_dtype=F32, res=dmemn, name="mm_dmem_v")
    _, d_norm_mem = _rmsnorm_bwd(mems, dmemn, norm_mem_g, None, name="norm_mem_bwd")
    dx1, dx1h, d_norm2 = _mm_norm_bwd(dq, G['xa_wq'], x1, norm2_g, dx2, name="mm_dxn2_norm2_bwd")

    gw['w_out'] = bwd_w_mm(mix, dx1h, D, D, name="mm_dw_out")
    tok_b = start_grads(('xa_wo', 'xa_wq', 'xa_wk', 'xa_wv', 'w_out'), "b")
    dmix = bwd_x_mm(dx1h, 'w_out', D, D, out_dtype=BF16, after=tok_b, name="mm_dmix")
    dh, d_ret_g = _ret_bwd(h, cos, sin, ret_g, states, ret_raw, dmix, name="ret_bwd")
    dh, d_rcw, d_rcb, d_wa, d_ba, d_wx, d_bx, d_lam = _lru_bwd(
        h, hl, dmix, dh, rg_cw, rg_cb, wa, ba, wx, bx, lam, name="lru_bwd")
    gw['w_in'] = bwd_w_mm(xn1, dh, D, IN, name="mm_dw_in")
    tok_c = start_grads(('w_in',), "c")
    dxn1 = bwd_x_mm(dh, 'w_in', D, IN, out_dtype=BF16, after=tok_c, name="mm_dxn1")
    grad_x, d_norm1 = _rmsnorm_bwd(xs, dxn1, norm1_g, dx1, name="norm1_bwd")

    small_parts = {
        'norm1_g': d_norm1, 'ret_g': d_ret_g, 'rg_conv_w': d_rcw[:rg_cw.shape[0]], 'rg_conv_b': d_rcb,
        'rg_wa': d_wa, 'rg_ba': d_ba, 'rg_wx': d_wx, 'rg_bx': d_bx, 'rg_lambda': d_lam, 'norm2_g': d_norm2,
        'norm_mem_g': d_norm_mem, 'norm3_g': d_norm3, 'ffn_conv_w': d_ffn_cw, 'ffn_conv_b': d_ffn_cb,
        'final_g': d_final}
    small = [n for n in WEIGHT_NAMES if n not in BIG_AXIS]
    red_shapes = [(1,)] + [tuple(small_parts[n].shape) for n in small]
    fwd_tok = sum(gg.forward(d_norm1)[0:1, 0:1] for _, gg in grad_groups)
    reduced = _allreduce_small(_pack([loss8[0:1, 0:1] + fwd_tok] + [small_parts[n] for n in small], SUB),
                               name="allreduce_small")
    red = _unpack(reduced, red_shapes)
    loss = red[0][0]
    g_small = dict(zip(small, red[1:]))
    for n in SMALL_SHARDED:
        w_local = W[n].shape[-1]
        g_small[n] = lax.dynamic_slice_in_dim(g_small[n], chip * w_local, w_local, axis=1)

    out_g, out_d, out_m, out_v = {}, {}, {}, {}
    rows = 512
    pk = lambda d: _pack([d[n] for n in small], rows)
    g_pack = _pack([g_small[n] for n in small], rows)
    res_small = _adamw(pk(W), pk(Mo), pk(Vo), g_pack[None], name="adamw_small")
    shapes_small = [tuple(W[n].shape) for n in small]
    for dst, packed in zip((out_g, out_d, out_m, out_v), res_small):
        for n, val in zip(small, _unpack(packed, shapes_small)):
            dst[n] = val
    last = res_small[0]
    for names, gg in grad_groups:
        for n, land in zip(names, gg.finish(last)):
            g, d, m_new, v_new = _adamw(shards[n], Mo[n][0], Vo[n][0], land, name="adamw_" + n)
            out_g[n], out_d[n], out_m[n], out_v[n] = (t.reshape(W[n].shape) for t in (g, d, m_new, v_new))
            last = g
    return (loss, grad_x[None], *[out_g[n] for n in WEIGHT_NAMES], *[out_d[n] for n in WEIGHT_NAMES],
            *[out_m[n] for n in WEIGHT_NAMES], *[out_v[n] for n in WEIGHT_NAMES])


def kernel(x, mem, positions, norm1_g, w_in, ret_g, rg_conv_w, rg_conv_b, rg_wa, rg_ba, rg_wx, rg_bx, rg_lambda, w_out, norm2_g, norm_mem_g, xa_wq, xa_wk, xa_wv, xa_wo, norm3_g, ffn_w_up, ffn_conv_w, ffn_conv_b, ffn_w_down, final_g, loss_target, m_norm1_g, m_w_in, m_ret_g, m_rg_conv_w, m_rg_conv_b, m_rg_wa, m_rg_ba, m_rg_wx, m_rg_bx, m_rg_lambda, m_w_out, m_norm2_g, m_norm_mem_g, m_xa_wq, m_xa_wk, m_xa_wv, m_xa_wo, m_norm3_g, m_ffn_w_up, m_ffn_conv_w, m_ffn_conv_b, m_ffn_w_down, m_final_g, v_norm1_g, v_w_in, v_ret_g, v_rg_conv_w, v_rg_conv_b, v_rg_wa, v_rg_ba, v_rg_wx, v_rg_bx, v_rg_lambda, v_w_out, v_norm2_g, v_norm_mem_g, v_xa_wq, v_xa_wk, v_xa_wv, v_xa_wo, v_norm3_g, v_ffn_w_up, v_ffn_conv_w, v_ffn_conv_b, v_ffn_w_down, v_final_g):
    W = dict(zip(WEIGHT_NAMES, (norm1_g, w_in, ret_g, rg_conv_w, rg_conv_b, rg_wa, rg_ba, rg_wx, rg_bx, rg_lambda, w_out,
                                norm2_g, norm_mem_g, xa_wq, xa_wk, xa_wv, xa_wo, norm3_g, ffn_w_up, ffn_conv_w,
                                ffn_conv_b, ffn_w_down, final_g)))
    Mo = dict(zip(WEIGHT_NAMES, (m_norm1_g, m_w_in, m_ret_g, m_rg_conv_w, m_rg_conv_b, m_rg_wa, m_rg_ba, m_rg_wx, m_rg_bx,
                                 m_rg_lambda, m_w_out, m_norm2_g, m_norm_mem_g, m_xa_wq, m_xa_wk, m_xa_wv, m_xa_wo,
                                 m_norm3_g, m_ffn_w_up, m_ffn_conv_w, m_ffn_conv_b, m_ffn_w_down, m_final_g)))
    Vo = dict(zip(WEIGHT_NAMES, (v_norm1_g, v_w_in, v_ret_g, v_rg_conv_w, v_rg_conv_b, v_rg_wa, v_rg_ba, v_rg_wx, v_rg_bx,
                                 v_rg_lambda, v_w_out, v_norm2_g, v_norm_mem_g, v_xa_wq, v_xa_wk, v_xa_wv, v_xa_wo,
                                 v_norm3_g, v_ffn_w_up, v_ffn_conv_w, v_ffn_conv_b, v_ffn_w_down, v_final_g)))
    return _step(x, mem, positions, loss_target, W, Mo, Vo)
```
